```python
import jax, jax.numpy as jnp
from jax import lax
import numpy as np

D_MODEL = 1024
BATCH = 8
SEQ = 16384
DEPTH = 1

D_MIX = D_MODEL
D_POOL = D_MIX // 2
POOL_WINDOWS = (2, 4, 8, 16)
N_POOL_GROUPS = len(POOL_WINDOWS)
POOL_GROUP_DIM = D_POOL // N_POOL_GROUPS
D_ATTN = D_MIX - D_POOL
HEAD_DIM = 64
N_HEADS = D_ATTN // HEAD_DIM
Q_BLOCK = 128
D_FF = ((8 * D_MODEL // 3 + 127) // 128) * 128
CONV_WIDTH = 3
EPS = 1e-6
D_IN_PROJ = D_POOL + 3 * D_ATTN

kernel_name = "hymba_pool_stickbreaking_convffn"


def rms_normalize(x):
    xf = x.astype(jnp.float32)
    return xf * lax.rsqrt(jnp.mean(xf * xf, axis=-1, keepdims=True) + EPS)


def rmsnorm(x, gain):
    return (rms_normalize(x) * gain.astype(jnp.float32)).astype(x.dtype)


def multiscale_pool(u, w_pool):
    B, S, _ = u.shape
    uf = u.astype(jnp.float32)
    cs = jnp.pad(jnp.cumsum(uf, axis=1), ((0, 0), (1, 0), (0, 0)))
    t1 = jnp.arange(1, S + 1)
    groups = []
    for g, w in enumerate(POOL_WINDOWS):
        sl = slice(g * POOL_GROUP_DIM, (g + 1) * POOL_GROUP_DIM)
        c = cs[:, :, sl]
        lagged = jnp.pad(c[:, :S + 1 - w], ((0, 0), (w - 1, 0), (0, 0)))
        cnt = jnp.minimum(t1, w).astype(jnp.float32)[None, :, None]
        groups.append((c[:, 1:] - lagged) / cnt - uf[:, :, sl])
    p = jnp.stack(groups, axis=2)
    y = jnp.einsum('bsgc,gcd->bsgd', p, w_pool.astype(jnp.float32))
    return y.reshape(B, S, D_POOL)


def stick_breaking_attention(q, k, v):
    B, H, S, Dh = q.shape
    nb = S // Q_BLOCK
    qb = q.astype(jnp.float32).reshape(B, H, nb, Q_BLOCK, Dh).transpose(2, 0, 1, 3, 4)
    kf = k.astype(jnp.float32)
    vf = v.astype(jnp.float32)
    key_pos = jnp.arange(S)
    scale = Dh ** -0.5

    def block(args):
        i, qi = args
        z = jnp.einsum('bhqd,bhkd->bhqk', qi, kf) * scale
        q_pos = i * Q_BLOCK + jnp.arange(Q_BLOCK)
        causal = key_pos[None, :] < q_pos[:, None]
        log_1m_beta = jnp.where(causal, jax.nn.log_sigmoid(-z), 0.0)
        stick = lax.cumsum(log_1m_beta, axis=3, reverse=True) - log_1m_beta
        a = jnp.where(causal, jnp.exp(jax.nn.log_sigmoid(z) + stick), 0.0)
        return jnp.einsum('bhqk,bhkd->bhqd', a, vf)

    out = lax.map(block, (jnp.arange(nb), qb))
    return out.transpose(1, 2, 0, 3, 4).reshape(B, H, S, Dh)


def causal_depthwise_conv(h, conv_w, conv_b):
    S = h.shape[1]
    hp = jnp.pad(h, ((0, 0), (CONV_WIDTH - 1, 0), (0, 0)))
    y = conv_b
    for kk in range(CONV_WIDTH):
        y = y + conv_w[kk] * hp[:, kk:kk + S]
    return y


def _fwd_setup_inputs(seed: int = 0) -> dict:
    key = jax.random.key(seed)
    ks = jax.random.split(key, 20)
    f32 = jnp.float32

    def nrm(k, shape, scale):
        return jax.random.normal(k, shape, f32) * scale

    def gain(k, n):
        return 1.0 + 0.05 * jax.random.normal(k, (DEPTH, n), f32)

    return {
        "x": jax.random.normal(ks[0], (BATCH, SEQ, D_MODEL), f32),
        "norm_mix_pre": gain(ks[1], D_MODEL),
        "w_in": nrm(ks[2], (DEPTH, D_MODEL, D_IN_PROJ), D_MODEL ** -0.5),
        "w_pool": nrm(ks[3], (DEPTH, N_POOL_GROUPS, POOL_GROUP_DIM, POOL_GROUP_DIM), POOL_GROUP_DIM ** -0.5),
        "pool_scale": gain(ks[4], D_POOL),
        "attn_scale": gain(ks[5], D_ATTN),
        "w_out": nrm(ks[6], (DEPTH, D_MIX, D_MODEL), D_MIX ** -0.5),
        "norm_mix_post": gain(ks[7], D_MODEL),
        "norm_ffn_pre": gain(ks[8], D_MODEL),
        "w_up": nrm(ks[9], (DEPTH, D_MODEL, 2 * D_FF), D_MODEL ** -0.5),
        "conv_w": nrm(ks[10], (DEPTH, CONV_WIDTH, 2 * D_FF), CONV_WIDTH ** -0.5),
        "conv_b": nrm(ks[11], (DEPTH, 2 * D_FF), 0.01),
        "w_down": nrm(ks[12], (DEPTH, D_FF, D_MODEL), D_FF ** -0.5),
        "norm_ffn_post": gain(ks[13], D_MODEL),
    }


def _fwd_reference(x, norm_mix_pre, w_in, w_pool, pool_scale, attn_scale, w_out, norm_mix_post,
              norm_ffn_pre, w_up, conv_w, conv_b, w_down, norm_ffn_post):
    B, S, _ = x.shape
    for l in range(DEPTH):
        h = rmsnorm(x, norm_mix_pre[l])
        proj = h @ w_in[l]
        u_pool = proj[..., :D_POOL]
        q = proj[..., D_POOL:D_POOL + D_ATTN]
        k = proj[..., D_POOL + D_ATTN:D_POOL + 2 * D_ATTN]
        v = proj[..., D_POOL + 2 * D_ATTN:]

        pool_out = multiscale_pool(u_pool, w_pool[l])

        def heads(t):
            return t.reshape(B, S, N_HEADS, HEAD_DIM).transpose(0, 2, 1, 3)
        attn_out = stick_breaking_attention(heads(q), heads(k), heads(v))
        attn_out = attn_out.transpose(0, 2, 1, 3).reshape(B, S, D_ATTN)

        merged = jnp.concatenate([
            rms_normalize(pool_out) * pool_scale[l].astype(jnp.float32),
            rms_normalize(attn_out) * attn_scale[l].astype(jnp.float32),
        ], axis=-1).astype(x.dtype)
        mix = merged @ w_out[l]
        x = x + rmsnorm(mix, norm_mix_post[l])

        h = rmsnorm(x, norm_ffn_pre[l])
        up = causal_depthwise_conv(h @ w_up[l], conv_w[l], conv_b[l])
        gate, val = up[..., :D_FF], up[..., D_FF:]
        f = (jax.nn.silu(gate) * val) @ w_down[l]
        x = x + rmsnorm(f, norm_ffn_post[l])
    return x


import jax as _jax
import jax.numpy as _jnp

TWIN_FORMAT = 'train_step'
FWD_PARAMS = ['x', 'norm_mix_pre', 'w_in', 'w_pool', 'pool_scale', 'attn_scale', 'w_out', 'norm_mix_post', 'norm_ffn_pre', 'w_up', 'conv_w', 'conv_b', 'w_down', 'norm_ffn_post']
TWIN_WEIGHTS = ['norm_mix_pre', 'w_in', 'w_pool', 'pool_scale', 'attn_scale', 'w_out', 'norm_mix_post', 'norm_ffn_pre', 'w_up', 'conv_w', 'conv_b', 'w_down', 'norm_ffn_post']
TWIN_DIFF_INPUT = 'x'
TWIN_INPUTS = ['x', 'norm_mix_pre', 'w_in', 'w_pool', 'pool_scale', 'attn_scale', 'w_out', 'norm_mix_post', 'norm_ffn_pre', 'w_up', 'conv_w', 'conv_b', 'w_down', 'norm_ffn_post', 'loss_target', 'm_norm_mix_pre', 'm_w_in', 'm_w_pool', 'm_pool_scale', 'm_attn_scale', 'm_w_out', 'm_norm_mix_post', 'm_norm_ffn_pre', 'm_w_up', 'm_conv_w', 'm_conv_b', 'm_w_down', 'm_norm_ffn_post', 'v_norm_mix_pre', 'v_w_in', 'v_w_pool', 'v_pool_scale', 'v_attn_scale', 'v_w_out', 'v_norm_mix_post', 'v_norm_ffn_pre', 'v_w_up', 'v_conv_w', 'v_conv_b', 'v_w_down', 'v_norm_ffn_post']
TWIN_OUTPUTS = ['loss', 'grad_x', 'grad_norm_mix_pre', 'grad_w_in', 'grad_w_pool', 'grad_pool_scale', 'grad_attn_scale', 'grad_w_out', 'grad_norm_mix_post', 'grad_norm_ffn_pre', 'grad_w_up', 'grad_conv_w', 'grad_conv_b', 'grad_w_down', 'grad_norm_ffn_post', 'delta_norm_mix_pre', 'delta_w_in', 'delta_w_pool', 'delta_pool_scale', 'delta_attn_scale', 'delta_w_out', 'delta_norm_mix_post', 'delta_norm_ffn_pre', 'delta_w_up', 'delta_conv_w', 'delta_conv_b', 'delta_w_down', 'delta_norm_ffn_post', 'new_m_norm_mix_pre', 'new_m_w_in', 'new_m_w_pool', 'new_m_pool_scale', 'new_m_attn_scale', 'new_m_w_out', 'new_m_norm_mix_post', 'new_m_norm_ffn_pre', 'new_m_w_up', 'new_m_conv_w', 'new_m_conv_b', 'new_m_w_down', 'new_m_norm_ffn_post', 'new_v_norm_mix_pre', 'new_v_w_in', 'new_v_w_pool', 'new_v_pool_scale', 'new_v_attn_scale', 'new_v_w_out', 'new_v_norm_mix_post', 'new_v_norm_ffn_pre', 'new_v_w_up', 'new_v_conv_w', 'new_v_conv_b', 'new_v_w_down', 'new_v_norm_ffn_post']
TWIN_LEAF_KINDS = {'loss': 'loss', 'grad_x': 'grad_x', 'grad_norm_mix_pre': 'grad_w', 'grad_w_in': 'grad_w', 'grad_w_pool': 'grad_w', 'grad_pool_scale': 'grad_w', 'grad_attn_scale': 'grad_w', 'grad_w_out': 'grad_w', 'grad_norm_mix_post': 'grad_w', 'grad_norm_ffn_pre': 'grad_w', 'grad_w_up': 'grad_w', 'grad_conv_w': 'grad_w', 'grad_conv_b': 'grad_w', 'grad_w_down': 'grad_w', 'grad_norm_ffn_post': 'grad_w', 'delta_norm_mix_pre': 'delta_w', 'delta_w_in': 'delta_w', 'delta_w_pool': 'delta_w', 'delta_pool_scale': 'delta_w', 'delta_attn_scale': 'delta_w', 'delta_w_out': 'delta_w', 'delta_norm_mix_post': 'delta_w', 'delta_norm_ffn_pre': 'delta_w', 'delta_w_up': 'delta_w', 'delta_conv_w': 'delta_w', 'delta_conv_b': 'delta_w', 'delta_w_down': 'delta_w', 'delta_norm_ffn_post': 'delta_w', 'new_m_norm_mix_pre': 'new_m', 'new_m_w_in': 'new_m', 'new_m_w_pool': 'new_m', 'new_m_pool_scale': 'new_m', 'new_m_attn_scale': 'new_m', 'new_m_w_out': 'new_m', 'new_m_norm_mix_post': 'new_m', 'new_m_norm_ffn_pre': 'new_m', 'new_m_w_up': 'new_m', 'new_m_conv_w': 'new_m', 'new_m_conv_b': 'new_m', 'new_m_w_down': 'new_m', 'new_m_norm_ffn_post': 'new_m', 'new_v_norm_mix_pre': 'new_v', 'new_v_w_in': 'new_v', 'new_v_w_pool': 'new_v', 'new_v_pool_scale': 'new_v', 'new_v_attn_scale': 'new_v', 'new_v_w_out': 'new_v', 'new_v_norm_mix_post': 'new_v', 'new_v_norm_ffn_pre': 'new_v', 'new_v_w_up': 'new_v', 'new_v_conv_w': 'new_v', 'new_v_conv_b': 'new_v', 'new_v_w_down': 'new_v', 'new_v_norm_ffn_post': 'new_v'}


def _forward(args):
    return _fwd_reference(*[args[k] for k in FWD_PARAMS])


def _output_shape():
    def fwd():
        inp = _fwd_setup_inputs(0)
        return _fwd_reference(*[inp[k] for k in FWD_PARAMS])
    out = _jax.eval_shape(fwd)
    return out.shape, out.dtype

N_MICROBATCH = 1
ADAM_LR = 0.001
ADAM_B1 = 0.9
ADAM_B2 = 0.999
ADAM_EPS = 1e-08
ADAM_WD = 0.01
ADAM_STEP = 10
PER_EXAMPLE_BATCH_AXIS = {'x': 0, 'loss_target': 0}
SHARED_INPUTS = []
_WEIGHT_DTYPES = {'norm_mix_pre': _jnp.float32, 'w_in': _jnp.float32, 'w_pool': _jnp.float32, 'pool_scale': _jnp.float32, 'attn_scale': _jnp.float32, 'w_out': _jnp.float32, 'norm_mix_post': _jnp.float32, 'norm_ffn_pre': _jnp.float32, 'w_up': _jnp.float32, 'conv_w': _jnp.float32, 'conv_b': _jnp.float32, 'w_down': _jnp.float32, 'norm_ffn_post': _jnp.float32}
MOMENT_SCALE = {'norm_mix_pre': 1.708449e+00, 'w_in': 1.125914e+00, 'w_pool': 2.197213e+00, 'pool_scale': 2.627762e+00, 'attn_scale': 1.253713e+00, 'w_out': 1.707637e+00, 'norm_mix_post': 1.277821e+02, 'norm_ffn_pre': 1.183090e+00, 'w_up': 4.934885e-01, 'conv_w': 5.669053e-01, 'conv_b': 1.273178e+00, 'w_down': 1.028302e+00, 'norm_ffn_post': 1.284281e+02}


def _to_microbatches(a, axis):
    t = _jnp.moveaxis(a, axis, 0)
    t = t.reshape((N_MICROBATCH, t.shape[0] // N_MICROBATCH) + t.shape[1:])
    return _jnp.moveaxis(t, 1, axis + 1)


def setup_inputs(seed: int = 0) -> dict:
    inp = _fwd_setup_inputs(seed)
    key = _jax.random.fold_in(_jax.random.key(seed), 7919)
    shape, _ = _output_shape()
    out = dict(inp)
    out["loss_target"] = _jax.random.normal(_jax.random.fold_in(key, 0), shape, _jnp.float32)
    for i, name in enumerate(TWIN_WEIGHTS):
        w = inp[name].astype(_jnp.float32)
        if MOMENT_SCALE is None:
            s = _jnp.sqrt(_jnp.mean(_jnp.square(w)) + 1e-30)
        else:
            s = MOMENT_SCALE[name]
        km, kv = _jax.random.split(_jax.random.fold_in(key, i + 1))
        out[name] = w
        out["m_" + name] = s * _jax.random.normal(km, w.shape, _jnp.float32)
        out["v_" + name] = (s * s) * _jax.random.uniform(kv, w.shape, _jnp.float32, 0.5, 1.5)
    if N_MICROBATCH > 1:
        for name, axis in PER_EXAMPLE_BATCH_AXIS.items():
            out[name] = _to_microbatches(out[name], axis)
    return {'x': out['x'], 'norm_mix_pre': out['norm_mix_pre'], 'w_in': out['w_in'], 'w_pool': out['w_pool'], 'pool_scale': out['pool_scale'], 'attn_scale': out['attn_scale'], 'w_out': out['w_out'], 'norm_mix_post': out['norm_mix_post'], 'norm_ffn_pre': out['norm_ffn_pre'], 'w_up': out['w_up'], 'conv_w': out['conv_w'], 'conv_b': out['conv_b'], 'w_down': out['w_down'], 'norm_ffn_post': out['norm_ffn_post'], 'loss_target': out['loss_target'], 'm_norm_mix_pre': out['m_norm_mix_pre'], 'm_w_in': out['m_w_in'], 'm_w_pool': out['m_w_pool'], 'm_pool_scale': out['m_pool_scale'], 'm_attn_scale': out['m_attn_scale'], 'm_w_out': out['m_w_out'], 'm_norm_mix_post': out['m_norm_mix_post'], 'm_norm_ffn_pre': out['m_norm_ffn_pre'], 'm_w_up': out['m_w_up'], 'm_conv_w': out['m_conv_w'], 'm_conv_b': out['m_conv_b'], 'm_w_down': out['m_w_down'], 'm_norm_ffn_post': out['m_norm_ffn_post'], 'v_norm_mix_pre': out['v_norm_mix_pre'], 'v_w_in': out['v_w_in'], 'v_w_pool': out['v_w_pool'], 'v_pool_scale': out['v_pool_scale'], 'v_attn_scale': out['v_attn_scale'], 'v_w_out': out['v_w_out'], 'v_norm_mix_post': out['v_norm_mix_post'], 'v_norm_ffn_pre': out['v_norm_ffn_pre'], 'v_w_up': out['v_w_up'], 'v_conv_w': out['v_conv_w'], 'v_conv_b': out['v_conv_b'], 'v_w_down': out['v_w_down'], 'v_norm_ffn_post': out['v_norm_ffn_post']}


def _loss(weights, diff, rest, loss_target):
    with _jax.named_scope("forward"):
        args = {**rest, TWIN_DIFF_INPUT: diff, **{k: w.astype(_WEIGHT_DTYPES[k]) for k, w in weights.items()}}
        y = _forward(args)
    with _jax.named_scope("loss_head"):
        err = _jnp.square(y.astype(_jnp.float32) - loss_target)
        return 0.5 * _jnp.sum(_jnp.mean(err, axis=-1)) if err.ndim else 0.5 * err


def _adamw(w, g, m, v):
    m = ADAM_B1 * m + (1.0 - ADAM_B1) * g
    v = ADAM_B2 * v + (1.0 - ADAM_B2) * _jnp.square(g)
    m_hat = m / (1.0 - ADAM_B1 ** ADAM_STEP)
    v_hat = v / (1.0 - ADAM_B2 ** ADAM_STEP)
    delta = -ADAM_LR * (m_hat / (_jnp.sqrt(v_hat) + ADAM_EPS) + ADAM_WD * w)
    return delta, m, v


def reference(x, norm_mix_pre, w_in, w_pool, pool_scale, attn_scale, w_out, norm_mix_post, norm_ffn_pre, w_up, conv_w, conv_b, w_down, norm_ffn_post, loss_target, m_norm_mix_pre, m_w_in, m_w_pool, m_pool_scale, m_attn_scale, m_w_out, m_norm_mix_post, m_norm_ffn_pre, m_w_up, m_conv_w, m_conv_b, m_w_down, m_norm_ffn_post, v_norm_mix_pre, v_w_in, v_w_pool, v_pool_scale, v_attn_scale, v_w_out, v_norm_mix_post, v_norm_ffn_pre, v_w_up, v_conv_w, v_conv_b, v_w_down, v_norm_ffn_post):
    given = dict(x=x, norm_mix_pre=norm_mix_pre, w_in=w_in, w_pool=w_pool, pool_scale=pool_scale, attn_scale=attn_scale, w_out=w_out, norm_mix_post=norm_mix_post, norm_ffn_pre=norm_ffn_pre, w_up=w_up, conv_w=conv_w, conv_b=conv_b, w_down=w_down, norm_ffn_post=norm_ffn_post, loss_target=loss_target, m_norm_mix_pre=m_norm_mix_pre, m_w_in=m_w_in, m_w_pool=m_w_pool, m_pool_scale=m_pool_scale, m_attn_scale=m_attn_scale, m_w_out=m_w_out, m_norm_mix_post=m_norm_mix_post, m_norm_ffn_pre=m_norm_ffn_pre, m_w_up=m_w_up, m_conv_w=m_conv_w, m_conv_b=m_conv_b, m_w_down=m_w_down, m_norm_ffn_post=m_norm_ffn_post, v_norm_mix_pre=v_norm_mix_pre, v_w_in=v_w_in, v_w_pool=v_w_pool, v_pool_scale=v_pool_scale, v_attn_scale=v_attn_scale, v_w_out=v_w_out, v_norm_mix_post=v_norm_mix_post, v_norm_ffn_pre=v_norm_ffn_pre, v_w_up=v_w_up, v_conv_w=v_conv_w, v_conv_b=v_conv_b, v_w_down=v_w_down, v_norm_ffn_post=v_norm_ffn_post)
    weights = {n: given[n] for n in TWIN_WEIGHTS}
    shared = {n: given[n] for n in SHARED_INPUTS}
    per_example = {n: given[n] for n in ['x']}
    grad_fn = _jax.value_and_grad(_loss, argnums=(0, 1))

    def one_microbatch(ex, loss_target):
        ex = dict(ex)
        diff = ex.pop(TWIN_DIFF_INPUT)
        return grad_fn(weights, diff, {**shared, **ex}, loss_target)

    if N_MICROBATCH == 1:
        loss, (grad_w, grad_x) = one_microbatch(per_example, given["loss_target"])
    else:
        def body(carry, xs):
            loss_sum, grad_sum = carry
            l_k, (gw_k, gx_k) = one_microbatch(xs[0], xs[1])
            with _jax.named_scope("update"):
                return (loss_sum + l_k, _jax.tree.map(_jnp.add, grad_sum, gw_k)), gx_k

        init = (_jnp.zeros((), _jnp.float32), _jax.tree.map(_jnp.zeros_like, weights))
        (loss, grad_w), grad_x = _jax.lax.scan(body, init, (per_example, given["loss_target"]))
    with _jax.named_scope("update"):
        delta_w, new_m, new_v = {}, {}, {}
        for n in TWIN_WEIGHTS:
            delta_w[n], new_m[n], new_v[n] = _adamw(weights[n], grad_w[n], given["m_" + n], given["v_" + n])
    return (loss, grad_x, *[grad_w[n] for n in TWIN_WEIGHTS], *[delta_w[n] for n in TWIN_WEIGHTS],
            *[new_m[n] for n in TWIN_WEIGHTS], *[new_v[n] for n in TWIN_WEIGHTS])
```

```python
import functools

import jax
import jax.numpy as jnp
from jax import lax
from jax.experimental import pallas as pl
from jax.experimental.pallas import tpu as pltpu

F32 = jnp.float32
BF16 = jnp.bfloat16

D_MODEL = 1024
D_POOL = 512
D_ATTN = 512
POOL_WINDOWS = (2, 4, 8, 16)
POOL_GROUP = 128
POOL_HALO = 16
CONV_HALO = 8
D_FF = 2816
FF_TILE = 1408
N_SHARD = 4
EPS = 1e-6
Q_SCALE = 0.125
ATT_BLOCK = 256
HEAD_PAIR = 128
MIB = 1 << 20

ADAM_LR = 0.001
ADAM_B1 = 0.9
ADAM_B2 = 0.999
ADAM_EPS = 1e-08
ADAM_WD = 0.01
ADAM_STEP = 10

NT_DIMS = (((1,), (1,)), ((), ()))
TN_DIMS = (((0,), (0,)), ((), ()))
MESH = pl.DeviceIdType.MESH
ANY = pl.BlockSpec(memory_space=pl.ANY)


def _call(body, **kw):
    return pl.pallas_call(body, **kw)


def _params(sem=None, vmem_mb=48):
    return pltpu.CompilerParams(dimension_semantics=sem, vmem_limit_bytes=vmem_mb * MIB)


def _rstd(v):
    return lax.rsqrt(jnp.mean(v * v, axis=-1, keepdims=True) + EPS)


def _dot(a, b):
    return jnp.dot(a, b, preferred_element_type=F32)


def _dot_nt(a, b):
    return lax.dot_general(a, b, NT_DIMS, preferred_element_type=F32)


def _dot_tn(a, b):
    return lax.dot_general(a, b, TN_DIMS, preferred_element_type=F32)


def _row_tile(rows, cap):
    t = min(rows, cap)
    t -= t % 8
    while rows % t:
        t -= 8
    return t


def _full(shape):
    nd = len(shape)
    return pl.BlockSpec(shape, lambda *_: (0,) * nd)


def _chip_peers():
    x, y, c = lax.axis_index("x"), lax.axis_index("y"), lax.axis_index("c")
    return x, y, c, [(1 - x, y), (x, 1 - y), (1 - x, 1 - y)]


def _cast_bf16(a, name):
    def body(a_ref, o_ref):
        o_ref[...] = a_ref[...].astype(BF16)

    return _call(body, name=name, out_shape=jax.ShapeDtypeStruct(a.shape, BF16),
                 grid=(1,), in_specs=[_full(a.shape)], out_specs=_full(a.shape),
                 compiler_params=_params(("arbitrary",)))(a)


def _gather_shards(shards):
    n = len(shards)

    def body(*refs):
        ins, outs = refs[:n], refs[n:2 * n]
        send, recv, loc = refs[2 * n:]
        x, y, c, chips = _chip_peers()
        b = 2 * x + y
        local = [pltpu.make_async_copy(ins[t], outs[t].at[b], loc.at[t]) for t in range(n)]
        for cp in local:
            cp.start()
        remote = []
        for t in range(n):
            for k, (px, py) in enumerate(chips):
                remote.append(pltpu.make_async_remote_copy(
                    src_ref=ins[t], dst_ref=outs[t].at[b],
                    send_sem=send.at[3 * t + k], recv_sem=recv.at[3 * t + k],
                    device_id=(px, py, c), device_id_type=MESH))
        for cp in remote:
            cp.start()
        for cp in remote:
            cp.wait()
        for cp in local:
            cp.wait()

    return _call(
        body, name="gather_weights",
        out_shape=[jax.ShapeDtypeStruct((N_SHARD,) + s.shape, s.dtype) for s in shards],
        in_specs=[ANY] * n, out_specs=[ANY] * n,
        scratch_shapes=[pltpu.SemaphoreType.DMA((3 * n,)), pltpu.SemaphoreType.DMA((3 * n,)),
                        pltpu.SemaphoreType.DMA((n,))],
    )(*shards)


def _scatter_grads(grads, small):
    n = len(grads)

    def body(*refs):
        ins, small_in = refs[:n], refs[n]
        outs, small_out = refs[n + 1:2 * n + 1], refs[2 * n + 1]
        send, recv, loc, ssend, srecv = refs[2 * n + 2:]
        x, y, c, chips = _chip_peers()
        b = 2 * x + y
        me = 4 * x + 2 * y + c
        local = [pltpu.make_async_copy(ins[t].at[b], outs[t].at[3], loc.at[t]) for t in range(n)]
        local.append(pltpu.make_async_copy(small_in, small_out.at[me], loc.at[n]))
        for cp in local:
            cp.start()
        remote = []
        for t in range(n):
            for k, (px, py) in enumerate(chips):
                remote.append(pltpu.make_async_remote_copy(
                    src_ref=ins[t].at[2 * px + py], dst_ref=outs[t].at[k],
                    send_sem=send.at[3 * t + k], recv_sem=recv.at[3 * t + k],
                    device_id=(px, py, c), device_id_type=MESH))
        for r in range(1, 8):
            px = 1 - x if r & 4 else x
            py = 1 - y if r & 2 else y
            pc = 1 - c if r & 1 else c
            remote.append(pltpu.make_async_remote_copy(
                src_ref=small_in, dst_ref=small_out.at[me],
                send_sem=ssend.at[r - 1], recv_sem=srecv.at[r - 1],
                device_id=(px, py, pc), device_id_type=MESH))
        for cp in remote:
            cp.start()
        for cp in remote:
            cp.wait()
        for cp in local:
            cp.wait()

    out_shape = [jax.ShapeDtypeStruct(g.shape, g.dtype) for g in grads]
    out_shape.append(jax.ShapeDtypeStruct((8,) + small.shape, small.dtype))
    res = _call(
        body, name="scatter_grads", out_shape=out_shape,
        in_specs=[ANY] * (n + 1), out_specs=[ANY] * (n + 1),
        scratch_shapes=[pltpu.SemaphoreType.DMA((3 * n,)), pltpu.SemaphoreType.DMA((3 * n,)),
                        pltpu.SemaphoreType.DMA((n + 1,)),
                        pltpu.SemaphoreType.DMA((7,)), pltpu.SemaphoreType.DMA((7,))],
    )(*grads, small)
    return res[:n], res[n]


def _swap_with_sibling(parts):
    n = len(parts)

    def body(*refs):
        ins, outs = refs[:n], refs[n:2 * n]
        send, recv = refs[2 * n:]
        x, y, c = lax.axis_index("x"), lax.axis_index("y"), lax.axis_index("c")
        copies = [pltpu.make_async_remote_copy(
            src_ref=ins[t], dst_ref=outs[t], send_sem=send.at[t], recv_sem=recv.at[t],
            device_id=(x, y, 1 - c), device_id_type=MESH) for t in range(n)]
        for cp in copies:
            cp.start()
        for cp in copies:
            cp.wait()

    return _call(
        body, name="swap_sibling",
        out_shape=[jax.ShapeDtypeStruct(p.shape, p.dtype) for p in parts],
        in_specs=[ANY] * n, out_specs=[ANY] * n,
        scratch_shapes=[pltpu.SemaphoreType.DMA((n,)), pltpu.SemaphoreType.DMA((n,))],
    )(*parts)


def _sum_slots(buf, order, name):
    k, rows, cols = buf.shape
    tr = _row_tile(rows, 256)

    def body(b_ref, o_ref):
        acc = b_ref[order[0]]
        for s in order[1:]:
            acc = acc + b_ref[s]
        o_ref[...] = acc

    return _call(body, name=name, out_shape=jax.ShapeDtypeStruct((rows, cols), F32),
                 grid=(rows // tr,),
                 in_specs=[pl.BlockSpec((k, tr, cols), lambda i: (0, i, 0))],
                 out_specs=pl.BlockSpec((tr, cols), lambda i: (i, 0)),
                 compiler_params=_params(("parallel",)))(buf)


def _adamw(grad_parts, w, m, v, name):
    rows, cols = w.shape
    tr = _row_tile(rows, 256)
    npart = len(grad_parts)

    def body(*refs):
        gp = refs[:npart]
        w_ref, m_ref, v_ref, g_out, d_out, m_out, v_out = refs[npart:]
        g = gp[0][...]
        for p in gp[1:]:
            g = g + p[...]
        mm = ADAM_B1 * m_ref[...] + (1.0 - ADAM_B1) * g
        vv = ADAM_B2 * v_ref[...] + (1.0 - ADAM_B2) * jnp.square(g)
        m_hat = mm / (1.0 - ADAM_B1 ** ADAM_STEP)
        v_hat = vv / (1.0 - ADAM_B2 ** ADAM_STEP)
        g_out[...] = g
        d_out[...] = -ADAM_LR * (m_hat / (jnp.sqrt(v_hat) + ADAM_EPS) + ADAM_WD * w_ref[...])
        m_out[...] = mm
        v_out[...] = vv

    spec = pl.BlockSpec((tr, cols), lambda i: (i, 0))
    shp = jax.ShapeDtypeStruct((rows, cols), F32)
    return _call(body, name=name, out_shape=[shp] * 4, grid=(rows // tr,),
                 in_specs=[spec] * (npart + 3), out_specs=[spec] * 4,
                 compiler_params=_params(("parallel",)))(*grad_parts, w, m, v)


def _in_proj(x, g1, w_in):
    s = x.shape[0]
    ts = 512

    def body(x_ref, g_ref, w_ref, u_ref, q_ref, k_ref, v_ref, h_ref):
        xv = x_ref[...]
        h = (xv * _rstd(xv) * g_ref[...]).astype(BF16)
        h_ref[...] = h
        u_ref[...] = _dot(h, w_ref[0])
        q_ref[...] = (_dot(h, w_ref[1]) * Q_SCALE).astype(BF16)
        k_ref[...] = _dot(h, w_ref[2]).astype(BF16)
        v_ref[...] = _dot(h, w_ref[3]).astype(BF16)

    row = lambda w: pl.BlockSpec((ts, w), lambda i: (i, 0))
    half = jax.ShapeDtypeStruct((s, D_POOL), BF16)
    return _call(
        body, name="in_proj",
        out_shape=[jax.ShapeDtypeStruct((s, D_POOL), F32), half, half, half,
                   jax.ShapeDtypeStruct((s, D_MODEL), BF16)],
        grid=(s // ts,),
        in_specs=[row(D_MODEL), _full((1, D_MODEL)), _full(w_in.shape)],
        out_specs=[row(D_POOL)] * 4 + [row(D_MODEL)],
        compiler_params=_params(("parallel",)))(x, g1, w_in)


def _pool_means(ext_ref, g, window, ts, row0):
    cols = slice(g * POOL_GROUP, (g + 1) * POOL_GROUP)
    cur = ext_ref[POOL_HALO:POOL_HALO + ts, cols]
    acc = cur
    for d in range(1, window):
        acc = acc + ext_ref[POOL_HALO - d:POOL_HALO - d + ts, cols]
    t1 = row0 + 1 + lax.broadcasted_iota(jnp.int32, (ts, 1), 0)
    cnt = jnp.minimum(t1, window).astype(F32)
    return acc / cnt - cur, cnt


def _pool_fwd(u, w_pool, pool_scale):
    s = u.shape[0]
    ts = 512
    per = ts // POOL_HALO

    def body(u_ref, halo_ref, wp_ref, ps_ref, o_ref, ext_ref, y_ref):
        i = pl.program_id(0)
        ext_ref[0:POOL_HALO, :] = jnp.where(i > 0, halo_ref[...], 0.0)
        ext_ref[POOL_HALO:, :] = u_ref[...]
        for g, window in enumerate(POOL_WINDOWS):
            p, _ = _pool_means(ext_ref, g, window, ts, i * ts)
            y_ref[:, g * POOL_GROUP:(g + 1) * POOL_GROUP] = _dot(
                p.astype(BF16), wp_ref[g].astype(BF16))
        y = y_ref[...]
        o_ref[...] = (y * _rstd(y) * ps_ref[...]).astype(BF16)

    return _call(
        body, name="pool_fwd", out_shape=jax.ShapeDtypeStruct((s, D_POOL), BF16),
        grid=(s // ts,),
        in_specs=[pl.BlockSpec((ts, D_POOL), lambda i: (i, 0)),
                  pl.BlockSpec((POOL_HALO, D_POOL), lambda i: (jnp.maximum(i * per - 1, 0), 0)),
                  _full(w_pool.shape), _full((1, D_POOL))],
        out_specs=pl.BlockSpec((ts, D_POOL), lambda i: (i, 0)),
        scratch_shapes=[pltpu.VMEM((ts + POOL_HALO, D_POOL), F32), pltpu.VMEM((ts, D_POOL), F32)],
        compiler_params=_params(("parallel",)))(u, u, w_pool, pool_scale)


def _split_bf16(v):
    hi = v.astype(BF16)
    lo = (v - hi.astype(F32)).astype(BF16)
    return hi, lo


def _tri(kind):
    r = lax.broadcasted_iota(jnp.int32, (ATT_BLOCK, ATT_BLOCK), 0)
    c = lax.broadcasted_iota(jnp.int32, (ATT_BLOCK, ATT_BLOCK), 1)
    return jnp.where(r > c if kind == "suffix" else r < c, 1.0, 0.0).astype(BF16)


def _causal_mask():
    r = lax.broadcasted_iota(jnp.int32, (ATT_BLOCK, ATT_BLOCK), 0)
    c = lax.broadcasted_iota(jnp.int32, (ATT_BLOCK, ATT_BLOCK), 1)
    return c < r


def _attn_fwd(q, k, v):
    s = q.shape[0]
    tb = ATT_BLOCK
    nq = s // tb

    def body(q_ref, k_ref, v_ref, o_ref, t_ref):
        i = pl.program_id(1)
        upper = _tri("suffix")
        causal = _causal_mask()
        lane = lax.broadcasted_iota(jnp.int32, (1, HEAD_PAIR), 1)
        first = lane < 64
        q2 = q_ref[...]
        zero = jnp.zeros_like(q2)

        def step(j, carry, qe, masked):
            c, o = carry
            start = pl.multiple_of(j * tb, tb)
            kj = k_ref[pl.ds(start, tb), :]
            vj = v_ref[pl.ds(start, tb), :]
            z = _dot_nt(qe, kj)
            lb = -(jnp.maximum(z, 0.0) + jnp.log(1.0 + jnp.exp(-jnp.abs(z))))
            if masked:
                lb = jnp.where(causal, lb, 0.0)
            hi, lo = _split_bf16(lb)
            suf = _dot(hi, upper) + _dot(lo, upper)
            a = jnp.exp(z + lb + suf + c)
            if masked:
                a = jnp.where(causal, a, 0.0)
            o = o + _dot(a.astype(BF16), vj)
            c = c + suf[:, 0:1] + lb[:, 0:1]
            return c, o

        outs = []
        for e in range(2):
            qe = jnp.where(first, q2, zero) if e == 0 else jnp.where(first, zero, q2)
            carry = (jnp.zeros((tb, 1), F32), jnp.zeros((tb, HEAD_PAIR), F32))
            carry = step(i, carry, qe, True)
            carry = lax.fori_loop(
                0, i, lambda n, cr, qe=qe: step(i - 1 - n, cr, qe, False), carry)
            t_ref[:, e:e + 1] = carry[0]
            outs.append(carry[1])
        o_ref[...] = jnp.where(first, outs[0], outs[1])

    return _call(
        body, name="attn_fwd",
        out_shape=[jax.ShapeDtypeStruct((s, D_ATTN), F32),
                   jax.ShapeDtypeStruct((4, s, 2), F32)],
        grid=(4, nq),
        in_specs=[pl.BlockSpec((tb, HEAD_PAIR), lambda h, i: (i, h)),
                  pl.BlockSpec((s, HEAD_PAIR), lambda h, i: (0, h)),
                  pl.BlockSpec((s, HEAD_PAIR), lambda h, i: (0, h))],
        out_specs=[pl.BlockSpec((tb, HEAD_PAIR), lambda h, i: (i, h)),
                   pl.BlockSpec((None, tb, 2), lambda h, i: (h, i, 0))],
        compiler_params=_params(("parallel", "arbitrary")))(q, k, v)


def _mix_out(attn, mpool, x, attn_scale, w_out, g2, g3):
    s = x.shape[0]
    ts = 512

    def body(a_ref, p_ref, x_ref, as_ref, w_ref, g2_ref, g3_ref, ma_ref, mix_ref, x1_ref, h2_ref):
        ao = a_ref[...]
        ma = (ao * _rstd(ao) * as_ref[...]).astype(BF16)
        ma_ref[...] = ma
        mix = _dot(p_ref[...], w_ref[0:D_POOL, :]) + _dot(ma, w_ref[D_POOL:, :])
        mix_ref[...] = mix
        x1 = x_ref[...] + mix * _rstd(mix) * g2_ref[...]
        x1_ref[...] = x1
        h2_ref[...] = (x1 * _rstd(x1) * g3_ref[...]).astype(BF16)

    row = lambda w: pl.BlockSpec((ts, w), lambda i: (i, 0))
    return _call(
        body, name="mix_out",
        out_shape=[jax.ShapeDtypeStruct((s, D_ATTN), BF16), jax.ShapeDtypeStruct((s, D_MODEL), F32),
                   jax.ShapeDtypeStruct((s, D_MODEL), F32), jax.ShapeDtypeStruct((s, D_MODEL), BF16)],
        grid=(s // ts,),
        in_specs=[row(D_ATTN), row(D_POOL), row(D_MODEL), _full((1, D_ATTN)),
                  _full((D_MODEL, D_MODEL)), _full((1, D_MODEL)), _full((1, D_MODEL))],
        out_specs=[row(D_ATTN), row(D_MODEL), row(D_MODEL), row(D_MODEL)],
        compiler_params=_params(("parallel",)))(attn, mpool, x, attn_scale, w_out, g2, g3)


def _conv_rows(ext_ref, cw, cb, ts):
    y = cb + cw[0:1, :] * ext_ref[CONV_HALO - 2:CONV_HALO - 2 + ts, :]
    y = y + cw[1:2, :] * ext_ref[CONV_HALO - 1:CONV_HALO - 1 + ts, :]
    return y + cw[2:3, :] * ext_ref[CONV_HALO:CONV_HALO + ts, :]


def _sigmoid(v):
    return 1.0 / (1.0 + jnp.exp(-v))


def _ffn_up(h2, w_up, conv_w, conv_b):
    s = h2.shape[0]
    ts = 256
    tn = FF_TILE

    def body(h_ref, wg_ref, wv_ref, cwg_ref, cwv_ref, cbg_ref, cbv_ref,
             ug_ref, uv_ref, f_ref, extg, extv):
        i = pl.program_id(1)

        @pl.when(i == 0)
        def _():
            extg[0:CONV_HALO, :] = jnp.zeros((CONV_HALO, tn), F32)
            extv[0:CONV_HALO, :] = jnp.zeros((CONV_HALO, tn), F32)

        h = h_ref[...]
        ug = _dot(h, wg_ref[...])
        uv = _dot(h, wv_ref[...])
        ug_ref[...] = ug
        uv_ref[...] = uv
        extg[CONV_HALO:, :] = ug
        extv[CONV_HALO:, :] = uv
        gate = _conv_rows(extg, cwg_ref[...], cbg_ref[...], ts)
        val = _conv_rows(extv, cwv_ref[...], cbv_ref[...], ts)
        f_ref[...] = (gate * _sigmoid(gate) * val).astype(BF16)
        extg[0:CONV_HALO, :] = extg[ts:ts + CONV_HALO, :]
        extv[0:CONV_HALO, :] = extv[ts:ts + CONV_HALO, :]

    out_blk = pl.BlockSpec((None, ts, tn), lambda n, i: (n, i, 0))
    act = jax.ShapeDtypeStruct((2, s, tn), F32)
    return _call(
        body, name="ffn_up",
        out_shape=[act, act, jax.ShapeDtypeStruct((2, s, tn), BF16)],
        grid=(2, s // ts),
        in_specs=[pl.BlockSpec((ts, D_MODEL), lambda n, i: (i, 0)),
                  pl.BlockSpec((None, D_MODEL, tn), lambda n, i: (n, 0, 0)),
                  pl.BlockSpec((None, D_MODEL, tn), lambda n, i: (n + 2, 0, 0)),
                  pl.BlockSpec((None, 3, tn), lambda n, i: (n, 0, 0)),
                  pl.BlockSpec((None, 3, tn), lambda n, i: (n + 2, 0, 0)),
                  pl.BlockSpec((None, 1, tn), lambda n, i: (n, 0, 0)),
                  pl.BlockSpec((None, 1, tn), lambda n, i: (n + 2, 0, 0))],
        out_specs=[out_blk, out_blk, out_blk],
        scratch_shapes=[pltpu.VMEM((ts + CONV_HALO, tn), F32), pltpu.VMEM((ts + CONV_HALO, tn), F32)],
        compiler_params=_params(("arbitrary", "arbitrary")))(
            h2, w_up, w_up, conv_w, conv_w, conv_b, conv_b)


def _ffn_down(f_in, w_down, x1, target, g4):
    s = x1.shape[0]
    ts = 512

    def body(f_ref, w_ref, x1_ref, t_ref, g_ref, df_ref, dy_ref, loss_ref, dg_ref):
        @pl.when(pl.program_id(0) == 0)
        def _():
            loss_ref[...] = jnp.zeros_like(loss_ref)
            dg_ref[...] = jnp.zeros_like(dg_ref)

        f = _dot(f_ref[0], w_ref[0:FF_TILE, :]) + _dot(f_ref[1], w_ref[FF_TILE:, :])
        rf = _rstd(f)
        fn = f * rf
        g = g_ref[...]
        err = (x1_ref[...] + fn * g) - t_ref[...]
        loss_ref[...] += 0.5 * jnp.sum(jnp.mean(err * err, axis=-1))
        dy = err * (1.0 / D_MODEL)
        dy_ref[...] = dy
        dg_ref[...] += jnp.sum(dy * fn, axis=0, keepdims=True)
        dfn = dy * g
        df_ref[...] = (rf * (dfn - fn * jnp.mean(dfn * fn, axis=-1, keepdims=True))).astype(BF16)

    row = pl.BlockSpec((ts, D_MODEL), lambda i: (i, 0))
    return _call(
        body, name="ffn_down",
        out_shape=[jax.ShapeDtypeStruct((s, D_MODEL), BF16), jax.ShapeDtypeStruct((s, D_MODEL), F32),
                   jax.ShapeDtypeStruct((8, 128), F32), jax.ShapeDtypeStruct((1, D_MODEL), F32)],
        grid=(s // ts,),
        in_specs=[pl.BlockSpec((2, ts, FF_TILE), lambda i: (0, i, 0)), _full((D_FF, D_MODEL)),
                  row, row, _full((1, D_MODEL))],
        out_specs=[row, row, _full((8, 128)), _full((1, D_MODEL))],
        compiler_params=_params(("arbitrary",)))(f_in, w_down, x1, target, g4)


def _tn_matmul(a, b, name, ts=512):
    na, s, ka = a.shape
    nb, _, nbc = b.shape

    def body(a_ref, b_ref, o_ref):
        @pl.when(pl.program_id(2) == 0)
        def _():
            o_ref[...] = jnp.zeros_like(o_ref)

        o_ref[...] += _dot_tn(a_ref[...].astype(BF16), b_ref[...].astype(BF16))

    return _call(
        body, name=name, out_shape=jax.ShapeDtypeStruct((na, nb, ka, nbc), F32),
        grid=(na, nb, s // ts),
        in_specs=[pl.BlockSpec((None, ts, ka), lambda i, j, r: (i, r, 0)),
                  pl.BlockSpec((None, ts, nbc), lambda i, j, r: (j, r, 0))],
        out_specs=pl.BlockSpec((None, None, ka, nbc), lambda i, j, r: (i, j, 0, 0)),
        compiler_params=_params(("parallel", "parallel", "arbitrary")))(a, b)


def _ffn_bwd_act(df, w_down, upre_g, upre_v, conv_w, conv_b):
    s = df.shape[0]
    ts = 256
    tn = FF_TILE
    nr = s // ts
    per = ts // CONV_HALO

    def body(df_ref, wd_ref, ug_ref, uv_ref, hg_ref, hv_ref, cwg_ref, cwv_ref, cbg_ref, cbv_ref,
             dug_ref, duv_ref, dcwg_ref, dcwv_ref, dcbg_ref, dcbv_ref, extg, extv, dxg, dxv):
        i = pl.program_id(1)
        first_rows = i == nr - 1

        @pl.when(i == 0)
        def _():
            dxg[ts:, :] = jnp.zeros((CONV_HALO, tn), F32)
            dxv[ts:, :] = jnp.zeros((CONV_HALO, tn), F32)
            for r in (dcwg_ref, dcwv_ref, dcbg_ref, dcbv_ref):
                r[...] = jnp.zeros_like(r)

        extg[0:CONV_HALO, :] = jnp.where(first_rows, 0.0, hg_ref[...])
        extv[0:CONV_HALO, :] = jnp.where(first_rows, 0.0, hv_ref[...])
        extg[CONV_HALO:, :] = ug_ref[...]
        extv[CONV_HALO:, :] = uv_ref[...]
        cwg, cwv = cwg_ref[...], cwv_ref[...]
        gate = _conv_rows(extg, cwg, cbg_ref[...], ts)
        val = _conv_rows(extv, cwv, cbv_ref[...], ts)
        sg = _sigmoid(gate)
        dfin = _dot_nt(df_ref[...], wd_ref[...])
        dval = dfin * (gate * sg)
        dgate = dfin * val * (sg * (1.0 + gate * (1.0 - sg)))

        def conv_bwd(dact, ext, dx, cw, dcw_ref, dcb_ref, du_ref):
            dx[0:ts, :] = dact
            dcb_ref[...] += jnp.sum(dact, axis=0, keepdims=True)
            for kk in range(3):
                lo = CONV_HALO - 2 + kk
                dcw_ref[kk:kk + 1, :] += jnp.sum(dact * ext[lo:lo + ts, :], axis=0, keepdims=True)
            du = cw[2:3, :] * dact + cw[1:2, :] * dx[1:1 + ts, :] + cw[0:1, :] * dx[2:2 + ts, :]
            du_ref[...] = du.astype(BF16)
            dx[ts:, :] = dx[0:CONV_HALO, :]

        conv_bwd(dgate, extg, dxg, cwg, dcwg_ref, dcbg_ref, dug_ref)
        conv_bwd(dval, extv, dxv, cwv, dcwv_ref, dcbv_ref, duv_ref)

    rows = lambda n, i: (n, nr - 1 - i, 0)
    halo = lambda n, i: (n, jnp.maximum((nr - 1 - i) * per - 1, 0), 0)
    act_blk = pl.BlockSpec((None, ts, tn), rows)
    halo_blk = pl.BlockSpec((None, CONV_HALO, tn), halo)
    cw_blk = lambda off: pl.BlockSpec((None, 3, tn), lambda n, i: (n + off, 0, 0))
    cb_blk = lambda off: pl.BlockSpec((None, 1, tn), lambda n, i: (n + off, 0, 0))
    acc_w = pl.BlockSpec((None, 3, tn), lambda n, i: (n, 0, 0))
    acc_b = pl.BlockSpec((None, 1, tn), lambda n, i: (n, 0, 0))
    dact = jax.ShapeDtypeStruct((2, s, tn), BF16)
    return _call(
        body, name="ffn_bwd_act",
        out_shape=[dact, dact, jax.ShapeDtypeStruct((2, 3, tn), F32), jax.ShapeDtypeStruct((2, 3, tn), F32),
                   jax.ShapeDtypeStruct((2, 1, tn), F32), jax.ShapeDtypeStruct((2, 1, tn), F32)],
        grid=(2, nr),
        in_specs=[pl.BlockSpec((ts, D_MODEL), lambda n, i: (nr - 1 - i, 0)),
                  pl.BlockSpec((tn, D_MODEL), lambda n, i: (n, 0)),
                  act_blk, act_blk, halo_blk, halo_blk,
                  cw_blk(0), cw_blk(2), cb_blk(0), cb_blk(2)],
        out_specs=[act_blk, act_blk, acc_w, acc_w, acc_b, acc_b],
        scratch_shapes=[pltpu.VMEM((ts + CONV_HALO, tn), F32)] * 4,
        compiler_params=_params(("arbitrary", "arbitrary")))(
            df, w_down, upre_g, upre_v, upre_g, upre_v, conv_w, conv_w, conv_b, conv_b)


def _ffn_bwd_in(dug, duv, w_up, x1, dy, mix, g3, g2):
    s = x1.shape[0]
    ts = 256

    def body(dg_ref, dv_ref, w_ref, x1_ref, dy_ref, mix_ref, g3_ref, g2_ref,
             dx1_ref, dmix_ref, dg3_ref, dg2_ref):
        @pl.when(pl.program_id(0) == 0)
        def _():
            dg3_ref[...] = jnp.zeros_like(dg3_ref)
            dg2_ref[...] = jnp.zeros_like(dg2_ref)

        dh = _dot_nt(dg_ref[0], w_ref[0]) + _dot_nt(dg_ref[1], w_ref[1])
        dh = dh + _dot_nt(dv_ref[0], w_ref[2]) + _dot_nt(dv_ref[1], w_ref[3])
        x1 = x1_ref[...]
        r3 = _rstd(x1)
        xn = x1 * r3
        dg3_ref[...] += jnp.sum(dh * xn, axis=0, keepdims=True)
        dxn = dh * g3_ref[...]
        dx1 = dy_ref[...] + r3 * (dxn - xn * jnp.mean(dxn * xn, axis=-1, keepdims=True))
        dx1_ref[...] = dx1
        mix = mix_ref[...]
        rm = _rstd(mix)
        mn = mix * rm
        dg2_ref[...] += jnp.sum(dx1 * mn, axis=0, keepdims=True)
        dmn = dx1 * g2_ref[...]
        dmix_ref[...] = (rm * (dmn - mn * jnp.mean(dmn * mn, axis=-1, keepdims=True))).astype(BF16)

    row = pl.BlockSpec((ts, D_MODEL), lambda i: (i, 0))
    act = pl.BlockSpec((2, ts, FF_TILE), lambda i: (0, i, 0))
    vec = _full((1, D_MODEL))
    return _call(
        body, name="ffn_bwd_in",
        out_shape=[jax.ShapeDtypeStruct((s, D_MODEL), F32), jax.ShapeDtypeStruct((s, D_MODEL), BF16),
                   jax.ShapeDtypeStruct((1, D_MODEL), F32), jax.ShapeDtypeStruct((1, D_MODEL), F32)],
        grid=(s // ts,),
        in_specs=[act, act, _full(w_up.shape), row, row, row, vec, vec],
        out_specs=[row, row, vec, vec],
        compiler_params=_params(("arbitrary",), vmem_mb=56))(dug, duv, w_up, x1, dy, mix, g3, g2)


def _mix_bwd(dmix, w_out, attn, attn_scale):
    s = dmix.shape[0]
    ts = 512

    def body(dm_ref, w_ref, a_ref, as_ref, dp_ref, do_ref, das_ref):
        @pl.when(pl.program_id(0) == 0)
        def _():
            das_ref[...] = jnp.zeros_like(das_ref)

        dm = dm_ref[...]
        dp_ref[...] = _dot_nt(dm, w_ref[0:D_POOL, :])
        da = _dot_nt(dm, w_ref[D_POOL:, :])
        ao = a_ref[...]
        ra = _rstd(ao)
        an = ao * ra
        das_ref[...] += jnp.sum(da * an, axis=0, keepdims=True)
        dan = da * as_ref[...]
        do_ref[...] = (ra * (dan - an * jnp.mean(dan * an, axis=-1, keepdims=True))).astype(BF16)

    row = lambda w: pl.BlockSpec((ts, w), lambda i: (i, 0))
    return _call(
        body, name="mix_bwd",
        out_shape=[jax.ShapeDtypeStruct((s, D_POOL), F32), jax.ShapeDtypeStruct((s, D_ATTN), BF16),
                   jax.ShapeDtypeStruct((1, D_ATTN), F32)],
        grid=(s // ts,),
        in_specs=[row(D_MODEL), _full((D_MODEL, D_MODEL)), row(D_ATTN), _full((1, D_ATTN))],
        out_specs=[row(D_POOL), row(D_ATTN), _full((1, D_ATTN))],
        compiler_params=_params(("arbitrary",)))(dmix, w_out, attn, attn_scale)


def _attn_bwd(q, k, v, do, totals):
    s = q.shape[0]
    tb = ATT_BLOCK
    nq = s // tb

    def body(q_ref, do_ref, t_ref, k_hbm, v_hbm, dq_ref, dk_hbm, dv_hbm,
             k_scr, v_scr, dk_acc, dv_acc):
        hp = pl.program_id(0)
        i = pl.program_id(1)
        lanes = pl.ds(pl.multiple_of(hp * HEAD_PAIR, HEAD_PAIR), HEAD_PAIR)

        @pl.when(i == 0)
        def _():
            pltpu.sync_copy(k_hbm.at[:, lanes], k_scr)
            pltpu.sync_copy(v_hbm.at[:, lanes], v_scr)
            dk_acc[...] = jnp.zeros_like(dk_acc)
            dv_acc[...] = jnp.zeros_like(dv_acc)

        upper = _tri("suffix")
        lower = _tri("prefix")
        causal = _causal_mask()
        lane = lax.broadcasted_iota(jnp.int32, (1, HEAD_PAIR), 1)
        first = lane < 64
        q2 = q_ref[...]
        do2 = do_ref[...]
        zero = jnp.zeros_like(q2)

        def step(j, carry, qe, doe, tot, masked):
            c, cp, dq = carry
            start = pl.multiple_of(j * tb, tb)
            kj = k_scr[pl.ds(start, tb), :]
            vj = v_scr[pl.ds(start, tb), :]
            z = _dot_nt(qe, kj)
            ez = jnp.exp(-jnp.abs(z))
            inv = 1.0 / (1.0 + ez)
            sig = jnp.where(z >= 0.0, inv, ez * inv)
            lb = -(jnp.maximum(z, 0.0) + jnp.log(1.0 + ez))
            if masked:
                lb = jnp.where(causal, lb, 0.0)
            hi, lo = _split_bf16(lb)
            suf = _dot(hi, upper) + _dot(lo, upper)
            c = c + suf[:, 0:1] + lb[:, 0:1]
            a = jnp.exp(z + lb + suf + (tot - c))
            if masked:
                a = jnp.where(causal, a, 0.0)
            dw = a * _dot_nt(doe, vj)
            hi, lo = _split_bf16(dw)
            pre = _dot(hi, lower) + _dot(lo, lower)
            dz = dw - sig * (dw + pre + cp)
            if masked:
                dz = jnp.where(causal, dz, 0.0)
            cp = cp + pre[:, tb - 1:tb] + dw[:, tb - 1:tb]
            dzb = dz.astype(BF16)
            dq = dq + _dot(dzb, kj)
            dk_acc[pl.ds(start, tb), :] += _dot_tn(dzb, qe)
            dv_acc[pl.ds(start, tb), :] += _dot_tn(a.astype(BF16), doe)
            return c, cp, dq

        outs = []
        for e in range(2):
            qe = jnp.where(first, q2, zero) if e == 0 else jnp.where(first, zero, q2)
            doe = jnp.where(first, do2, zero) if e == 0 else jnp.where(first, zero, do2)
            tot = t_ref[:, e:e + 1]
            carry = (jnp.zeros((tb, 1), F32), jnp.zeros((tb, 1), F32), jnp.zeros((tb, HEAD_PAIR), F32))
            carry = lax.fori_loop(
                0, i, lambda j, cr, qe=qe, doe=doe, tot=tot: step(j, cr, qe, doe, tot, False), carry)
            carry = step(i, carry, qe, doe, tot, True)
            outs.append(carry[2])
        dq_ref[...] = (jnp.where(first, outs[0], outs[1]) * Q_SCALE).astype(BF16)

        @pl.when(i == nq - 1)
        def _():
            pltpu.sync_copy(dk_acc, dk_hbm.at[:, lanes])
            pltpu.sync_copy(dv_acc, dv_hbm.at[:, lanes])

    blk = pl.BlockSpec((tb, HEAD_PAIR), lambda h, i: (i, h))
    grad = jax.ShapeDtypeStruct((s, D_ATTN), F32)
    return _call(
        body, name="attn_bwd",
        out_shape=[jax.ShapeDtypeStruct((s, D_ATTN), BF16), grad, grad],
        grid=(4, nq),
        in_specs=[blk, blk, pl.BlockSpec((None, tb, 2), lambda h, i: (h, i, 0)), ANY, ANY],
        out_specs=[blk, ANY, ANY],
        scratch_shapes=[pltpu.VMEM((s, HEAD_PAIR), BF16), pltpu.VMEM((s, HEAD_PAIR), BF16),
                        pltpu.VMEM((s, HEAD_PAIR), F32), pltpu.VMEM((s, HEAD_PAIR), F32)],
        compiler_params=_params(("arbitrary", "arbitrary")))(q, do, totals, k, v)


def _pool_bwd(u, dmp, w_pool, pool_scale):
    s = u.shape[0]
    ts = 512
    nr = s // ts
    per = ts // POOL_HALO

    def body(u_ref, halo_ref, dm_ref, wp_ref, ps_ref, du_ref, dwp_ref, dps_ref, ext_ref, y_ref, dext_ref):
        i = pl.program_id(0)
        rb = nr - 1 - i

        @pl.when(i == 0)
        def _():
            dext_ref[ts:, :] = jnp.zeros((POOL_HALO, D_POOL), F32)
            dwp_ref[...] = jnp.zeros_like(dwp_ref)
            dps_ref[...] = jnp.zeros_like(dps_ref)

        ext_ref[0:POOL_HALO, :] = jnp.where(rb > 0, halo_ref[...], 0.0)
        ext_ref[POOL_HALO:, :] = u_ref[...]
        ps, cnts = [], []
        for g, window in enumerate(POOL_WINDOWS):
            p, cnt = _pool_means(ext_ref, g, window, ts, rb * ts)
            ps.append(p.astype(BF16))
            cnts.append(cnt)
            y_ref[:, g * POOL_GROUP:(g + 1) * POOL_GROUP] = _dot(ps[g], wp_ref[g].astype(BF16))
        y = y_ref[...]
        r = _rstd(y)
        yn = y * r
        dm = dm_ref[...]
        dps_ref[...] += jnp.sum(dm * yn, axis=0, keepdims=True)
        dn = dm * ps_ref[...]
        dy = r * (dn - yn * jnp.mean(dn * yn, axis=-1, keepdims=True))
        for g, window in enumerate(POOL_WINDOWS):
            cols = slice(g * POOL_GROUP, (g + 1) * POOL_GROUP)
            dyg = dy[:, cols].astype(BF16)
            dwp_ref[g] += _dot_tn(ps[g], dyg)
            dp = _dot_nt(dyg, wp_ref[g].astype(BF16))
            dext_ref[0:ts, cols] = dp / cnts[g]
            acc = dext_ref[0:ts, cols]
            for d in range(1, window):
                acc = acc + dext_ref[d:d + ts, cols]
            du_ref[:, cols] = (acc - dp).astype(BF16)
        dext_ref[ts:, :] = dext_ref[0:POOL_HALO, :]

    rows = pl.BlockSpec((ts, D_POOL), lambda i: (nr - 1 - i, 0))
    return _call(
        body, name="pool_bwd",
        out_shape=[jax.ShapeDtypeStruct((s, D_POOL), BF16), jax.ShapeDtypeStruct(w_pool.shape, F32),
                   jax.ShapeDtypeStruct((1, D_POOL), F32)],
        grid=(nr,),
        in_specs=[rows,
                  pl.BlockSpec((POOL_HALO, D_POOL), lambda i: (jnp.maximum((nr - 1 - i) * per - 1, 0), 0)),
                  rows, _full(w_pool.shape), _full((1, D_POOL))],
        out_specs=[rows, _full(w_pool.shape), _full((1, D_POOL))],
        scratch_shapes=[pltpu.VMEM((ts + POOL_HALO, D_POOL), F32), pltpu.VMEM((ts, D_POOL), F32),
                        pltpu.VMEM((ts + POOL_HALO, D_POOL), F32)],
        compiler_params=_params(("arbitrary",)))(u, u, dmp, w_pool, pool_scale)


def _in_proj_bwd(du, dq, dk, dv, w_in, x, dx1, g1):
    s = x.shape[0]
    ts = 512

    def body(du_ref, dq_ref, dk_ref, dv_ref, w_ref, x_ref, dx1_ref, g_ref, gx_ref, dg_ref):
        @pl.when(pl.program_id(0) == 0)
        def _():
            dg_ref[...] = jnp.zeros_like(dg_ref)

        dh = _dot_nt(du_ref[...], w_ref[0]) + _dot_nt(dq_ref[...], w_ref[1])
        dh = dh + _dot_nt(dk_ref[...].astype(BF16), w_ref[2]) + _dot_nt(dv_ref[...].astype(BF16), w_ref[3])
        xv = x_ref[...]
        r = _rstd(xv)
        xn = xv * r
        dg_ref[...] += jnp.sum(dh * xn, axis=0, keepdims=True)
        dxn = dh * g_ref[...]
        gx_ref[...] = dx1_ref[...] + r * (dxn - xn * jnp.mean(dxn * xn, axis=-1, keepdims=True))

    row = lambda w: pl.BlockSpec((ts, w), lambda i: (i, 0))
    return _call(
        body, name="in_proj_bwd",
        out_shape=[jax.ShapeDtypeStruct((s, D_MODEL), F32), jax.ShapeDtypeStruct((1, D_MODEL), F32)],
        grid=(s // ts,),
        in_specs=[row(D_POOL)] * 4 + [_full(w_in.shape), row(D_MODEL), row(D_MODEL), _full((1, D_MODEL))],
        out_specs=[row(D_MODEL), _full((1, D_MODEL))],
        compiler_params=_params(("arbitrary",)))(du, dq, dk, dv, w_in, x, dx1, g1)


_SMALL = ("norm_mix_pre", "w_pool", "pool_scale", "attn_scale", "norm_mix_post",
          "norm_ffn_pre", "conv_b", "norm_ffn_post")
_SMALL_SIZE = {"norm_mix_pre": 1024, "w_pool": 65536, "pool_scale": 512, "attn_scale": 512,
               "norm_mix_post": 1024, "norm_ffn_pre": 1024, "conv_b": 5632, "norm_ffn_post": 1024}
_SMALL_ROWS = 600
_CONVW_ROWS = 132
_PACK_ROWS = _SMALL_ROWS + _CONVW_ROWS + 4


def _pack_small(parts):
    flat = jnp.concatenate([parts[n].reshape(-1) for n in _SMALL])
    flat = jnp.pad(flat, (0, _SMALL_ROWS * 128 - flat.shape[0]))
    return flat.reshape(_SMALL_ROWS, 128)


def _unpack_small(packed, like):
    flat = packed.reshape(-1)
    out, off = {}, 0
    for n in _SMALL:
        out[n] = flat[off:off + _SMALL_SIZE[n]].reshape(like[n].shape)
        off += _SMALL_SIZE[n]
    return out


def kernel(x, norm_mix_pre, w_in, w_pool, pool_scale, attn_scale, w_out, norm_mix_post, norm_ffn_pre, w_up, conv_w, conv_b, w_down, norm_ffn_post, loss_target, m_norm_mix_pre, m_w_in, m_w_pool, m_pool_scale, m_attn_scale, m_w_out, m_norm_mix_post, m_norm_ffn_pre, m_w_up, m_conv_w, m_conv_b, m_w_down, m_norm_ffn_post, v_norm_mix_pre, v_w_in, v_w_pool, v_pool_scale, v_attn_scale, v_w_out, v_norm_mix_post, v_norm_ffn_pre, v_w_up, v_conv_w, v_conv_b, v_w_down, v_norm_ffn_post):
    weights = dict(norm_mix_pre=norm_mix_pre, w_in=w_in, w_pool=w_pool, pool_scale=pool_scale,
                   attn_scale=attn_scale, w_out=w_out, norm_mix_post=norm_mix_post,
                   norm_ffn_pre=norm_ffn_pre, w_up=w_up, conv_w=conv_w, conv_b=conv_b,
                   w_down=w_down, norm_ffn_post=norm_ffn_post)
    mom1 = dict(norm_mix_pre=m_norm_mix_pre, w_in=m_w_in, w_pool=m_w_pool, pool_scale=m_pool_scale,
                attn_scale=m_attn_scale, w_out=m_w_out, norm_mix_post=m_norm_mix_post,
                norm_ffn_pre=m_norm_ffn_pre, w_up=m_w_up, conv_w=m_conv_w, conv_b=m_conv_b,
                w_down=m_w_down, norm_ffn_post=m_norm_ffn_post)
    mom2 = dict(norm_mix_pre=v_norm_mix_pre, w_in=v_w_in, w_pool=v_w_pool, pool_scale=v_pool_scale,
                attn_scale=v_attn_scale, w_out=v_w_out, norm_mix_post=v_norm_mix_post,
                norm_ffn_pre=v_norm_ffn_pre, w_up=v_w_up, conv_w=v_conv_w, conv_b=v_conv_b,
                w_down=v_w_down, norm_ffn_post=v_norm_ffn_post)
    order = list(weights)

    xs = x[0]
    target = loss_target[0]
    wp = w_pool[0]
    shard = lax.axis_index("x") * 2 + lax.axis_index("y")

    shards = [_cast_bf16(w_in[0], "cast_w_in"), _cast_bf16(w_out[0], "cast_w_out"),
              _cast_bf16(w_up[0], "cast_w_up"), _cast_bf16(w_down[0], "cast_w_down"), conv_w[0]]
    win_g, wout_g, wup_g, wdown_g, convw_g = _gather_shards(shards)
    wout_f = wout_g.reshape(D_MODEL, D_MODEL)
    wdown_f = wdown_g.reshape(D_FF, D_MODEL)
    convb_g = conv_b[0].reshape(N_SHARD, 1, FF_TILE)

    u, q, k, v, h1 = _in_proj(xs, norm_mix_pre, win_g)
    mpool = _pool_fwd(u, wp, pool_scale)
    attn, totals = _attn_fwd(q, k, v)
    mattn, mix, x1, h2 = _mix_out(attn, mpool, xs, attn_scale, wout_f, norm_mix_post, norm_ffn_pre)
    upre_g, upre_v, f_in = _ffn_up(h2, wup_g, convw_g, convb_g)
    df, dy, loss_tile, d_post = _ffn_down(f_in, wdown_f, x1, target, norm_ffn_post)

    d_wdown = _tn_matmul(f_in, df[None], "dw_down")
    dug, duv, dcw_g, dcw_v, dcb_g, dcb_v = _ffn_bwd_act(df, wdown_f, upre_g, upre_v, convw_g, convb_g)
    d_wup = jnp.concatenate([_tn_matmul(h2[None], dug, "dw_up_gate")[0],
                             _tn_matmul(h2[None], duv, "dw_up_value")[0]], axis=0)
    dx1, dmix, d_ffn_pre, d_mix_post = _ffn_bwd_in(dug, duv, wup_g, x1, dy, mix, norm_ffn_pre, norm_mix_post)
    d_wout = jnp.concatenate([_tn_matmul(mpool[None], dmix[None], "dw_out_pool")[0, 0],
                              _tn_matmul(mattn[None], dmix[None], "dw_out_attn")[0, 0]], axis=0)
    dmp, do, d_attn_scale = _mix_bwd(dmix, wout_f, attn, attn_scale)
    dq, dk, dv = _attn_bwd(q, k, v, do, totals)
    du, d_wpool, d_pool_scale = _pool_bwd(u, dmp, wp, pool_scale)
    d_win = jnp.stack([_tn_matmul(h1[None], t[None], "dw_in_%d" % n)[0, 0]
                       for n, t in enumerate((du, dq, dk, dv))])
    grad_x, d_mix_pre = _in_proj_bwd(du, dq, dk, dv, win_g, xs, dx1, norm_mix_pre)

    d_convw = jnp.concatenate([dcw_g, dcw_v], axis=0)
    d_convb = jnp.concatenate([dcb_g, dcb_v], axis=0).reshape(1, 2 * D_FF)
    small_parts = dict(norm_mix_pre=d_mix_pre, w_pool=d_wpool, pool_scale=d_pool_scale,
                       attn_scale=d_attn_scale, norm_mix_post=d_mix_post, norm_ffn_pre=d_ffn_pre,
                       conv_b=d_convb, norm_ffn_post=d_post)
    packed = jnp.concatenate([_pack_small(small_parts), d_convw.reshape(_CONVW_ROWS, 128),
                              jnp.zeros((4, 128), F32)], axis=0)
    big = [d_win, d_wout.reshape(N_SHARD, D_MODEL // N_SHARD, D_MODEL), d_wup,
           d_wdown.reshape(N_SHARD, D_FF // N_SHARD, D_MODEL)]
    recv, gathered = _scatter_grads(big, packed)
    quarter = [_sum_slots(r, (3, 0, 1, 2), "sum_chips_%d" % n) for n, r in enumerate(recv)]
    sibling = _swap_with_sibling(quarter)
    small_sum = _sum_slots(gathered, tuple(range(8)), "sum_small")

    results = {}
    for n, name in enumerate(("w_in", "w_out", "w_up", "w_down")):
        res = _adamw([quarter[n], sibling[n]], weights[name][0], mom1[name][0], mom2[name][0],
                     "adamw_" + name)
        results[name] = [t[None] for t in res]
    g_convw = lax.dynamic_slice_in_dim(
        small_sum[_SMALL_ROWS:_SMALL_ROWS + _CONVW_ROWS].reshape(N_SHARD, 3, FF_TILE), shard, 1, axis=0)[0]
    convw_pad = lambda t: jnp.pad(t, ((0, 5), (0, 0)))
    res = _adamw([convw_pad(g_convw)], convw_pad(conv_w[0]), convw_pad(m_conv_w[0]),
                 convw_pad(v_conv_w[0]), "adamw_conv_w")
    results["conv_w"] = [t[:3][None] for t in res]
    pack_w = _pack_small(weights)
    pack_m = _pack_small(mom1)
    pack_v = _pack_small(mom2)
    res = _adamw([small_sum[:_SMALL_ROWS]], pack_w, pack_m, pack_v, "adamw_small")
    unpacked = [_unpack_small(t, weights) for t in res]
    for name in _SMALL:
        results[name] = [t[name] for t in unpacked]

    loss = lax.psum(loss_tile[0, 0], ("x", "y", "c"))
    outs = [loss, grad_x[None]]
    for slot in range(4):
        outs.extend(results[name][slot] for name in order)
    return tuple(outs)
```

```python
import functools

import jax
import jax.numpy as jnp
from jax import lax
from jax.experimental import pallas as pl
from jax.experimental.pallas import tpu as pltpu

F32 = jnp.float32
BF16 = jnp.bfloat16

D_MODEL = 1024
D_POOL = 512
D_ATTN = 512
POOL_WINDOWS = (2, 4, 8, 16)
POOL_GROUP = 128
POOL_HALO = 16
CONV_HALO = 8
D_FF = 2816
FF_TILE = 1408
N_SHARD = 4
EPS = 1e-6
Q_SCALE = 0.125
ATT_BLOCK = 256
HEAD_PAIR = 128
MIB = 1 << 20
LOG2E = 1.4426950408889634

ADAM_LR = 0.001
ADAM_B1 = 0.9
ADAM_B2 = 0.999
ADAM_EPS = 1e-08
ADAM_WD = 0.01
ADAM_STEP = 10

NT_DIMS = (((1,), (1,)), ((), ()))
TN_DIMS = (((0,), (0,)), ((), ()))
MESH = pl.DeviceIdType.MESH
ANY = pl.BlockSpec(memory_space=pl.ANY)


def _call(body, **kw):
    return pl.pallas_call(body, **kw)


def _params(sem=None, vmem_mb=48):
    return pltpu.CompilerParams(dimension_semantics=sem, vmem_limit_bytes=vmem_mb * MIB)


def _rstd(v):
    return lax.rsqrt(jnp.mean(v * v, axis=-1, keepdims=True) + EPS)


def _dot(a, b):
    return jnp.dot(a, b, preferred_element_type=F32)


def _dot_nt(a, b):
    return lax.dot_general(a, b, NT_DIMS, preferred_element_type=F32)


def _dot_tn(a, b):
    return lax.dot_general(a, b, TN_DIMS, preferred_element_type=F32)


def _row_tile(rows, cap):
    t = min(rows, cap)
    t -= t % 8
    while rows % t:
        t -= 8
    return t


def _full(shape):
    nd = len(shape)
    return pl.BlockSpec(shape, lambda *_: (0,) * nd)


def _chip_peers():
    x, y, c = lax.axis_index("x"), lax.axis_index("y"), lax.axis_index("c")
    return x, y, c, [(1 - x, y), (x, 1 - y), (1 - x, 1 - y)]


def _cast_bf16(a, name):
    def body(a_ref, o_ref):
        o_ref[...] = a_ref[...].astype(BF16)

    return _call(body, name=name, out_shape=jax.ShapeDtypeStruct(a.shape, BF16),
                 grid=(1,), in_specs=[_full(a.shape)], out_specs=_full(a.shape),
                 compiler_params=_params(("arbitrary",)))(a)


def _gather_shards(shards):
    n = len(shards)

    def body(*refs):
        ins, outs = refs[:n], refs[n:2 * n]
        send, recv, loc = refs[2 * n:]
        x, y, c, chips = _chip_peers()
        b = 2 * x + y
        local = [pltpu.make_async_copy(ins[t], outs[t].at[b], loc.at[t]) for t in range(n)]
        for cp in local:
            cp.start()
        remote = []
        for t in range(n):
            for k, (px, py) in enumerate(chips):
                remote.append(pltpu.make_async_remote_copy(
                    src_ref=ins[t], dst_ref=outs[t].at[b],
                    send_sem=send.at[3 * t + k], recv_sem=recv.at[3 * t + k],
                    device_id=(px, py, c), device_id_type=MESH))
        for cp in remote:
            cp.start()
        for cp in remote:
            cp.wait()
        for cp in local:
            cp.wait()

    return _call(
        body, name="gather_weights",
        out_shape=[jax.ShapeDtypeStruct((N_SHARD,) + s.shape, s.dtype) for s in shards],
        in_specs=[ANY] * n, out_specs=[ANY] * n,
        scratch_shapes=[pltpu.SemaphoreType.DMA((3 * n,)), pltpu.SemaphoreType.DMA((3 * n,)),
                        pltpu.SemaphoreType.DMA((n,))],
    )(*shards)


def _scatter_grads(grads, small):
    n = len(grads)

    def body(*refs):
        ins, small_in = refs[:n], refs[n]
        outs, small_out = refs[n + 1:2 * n + 1], refs[2 * n + 1]
        send, recv, loc, ssend, srecv = refs[2 * n + 2:]
        x, y, c, chips = _chip_peers()
        b = 2 * x + y
        me = 4 * x + 2 * y + c
        local = [pltpu.make_async_copy(ins[t].at[b], outs[t].at[3], loc.at[t]) for t in range(n)]
        local.append(pltpu.make_async_copy(small_in, small_out.at[me], loc.at[n]))
        for cp in local:
            cp.start()
        remote = []
        for t in range(n):
            for k, (px, py) in enumerate(chips):
                remote.append(pltpu.make_async_remote_copy(
                    src_ref=ins[t].at[2 * px + py], dst_ref=outs[t].at[k],
                    send_sem=send.at[3 * t + k], recv_sem=recv.at[3 * t + k],
                    device_id=(px, py, c), device_id_type=MESH))
        for r in range(1, 8):
            px = 1 - x if r & 4 else x
            py = 1 - y if r & 2 else y
            pc = 1 - c if r & 1 else c
            remote.append(pltpu.make_async_remote_copy(
                src_ref=small_in, dst_ref=small_out.at[me],
                send_sem=ssend.at[r - 1], recv_sem=srecv.at[r - 1],
                device_id=(px, py, pc), device_id_type=MESH))
        for cp in remote:
            cp.start()
        for cp in remote:
            cp.wait()
        for cp in local:
            cp.wait()

    out_shape = [jax.ShapeDtypeStruct(g.shape, g.dtype) for g in grads]
    out_shape.append(jax.ShapeDtypeStruct((8,) + small.shape, small.dtype))
    res = _call(
        body, name="scatter_grads", out_shape=out_shape,
        in_specs=[ANY] * (n + 1), out_specs=[ANY] * (n + 1),
        scratch_shapes=[pltpu.SemaphoreType.DMA((3 * n,)), pltpu.SemaphoreType.DMA((3 * n,)),
                        pltpu.SemaphoreType.DMA((n + 1,)),
                        pltpu.SemaphoreType.DMA((7,)), pltpu.SemaphoreType.DMA((7,))],
    )(*grads, small)
    return res[:n], res[n]


def _swap_with_sibling(parts):
    n = len(parts)

    def body(*refs):
        ins, outs = refs[:n], refs[n:2 * n]
        send, recv = refs[2 * n:]
        x, y, c = lax.axis_index("x"), lax.axis_index("y"), lax.axis_index("c")
        copies = [pltpu.make_async_remote_copy(
            src_ref=ins[t], dst_ref=outs[t], send_sem=send.at[t], recv_sem=recv.at[t],
            device_id=(x, y, 1 - c), device_id_type=MESH) for t in range(n)]
        for cp in copies:
            cp.start()
        for cp in copies:
            cp.wait()

    return _call(
        body, name="swap_sibling",
        out_shape=[jax.ShapeDtypeStruct(p.shape, p.dtype) for p in parts],
        in_specs=[ANY] * n, out_specs=[ANY] * n,
        scratch_shapes=[pltpu.SemaphoreType.DMA((n,)), pltpu.SemaphoreType.DMA((n,))],
    )(*parts)


def _sum_slots(buf, order, name):
    k, rows, cols = buf.shape
    tr = _row_tile(rows, 256)

    def body(b_ref, o_ref):
        acc = b_ref[order[0]]
        for s in order[1:]:
            acc = acc + b_ref[s]
        o_ref[...] = acc

    return _call(body, name=name, out_shape=jax.ShapeDtypeStruct((rows, cols), F32),
                 grid=(rows // tr,),
                 in_specs=[pl.BlockSpec((k, tr, cols), lambda i: (0, i, 0))],
                 out_specs=pl.BlockSpec((tr, cols), lambda i: (i, 0)),
                 compiler_params=_params(("parallel",)))(buf)


def _adamw(grad_parts, w, m, v, name):
    rows, cols = w.shape
    tr = _row_tile(rows, 256)
    npart = len(grad_parts)

    def body(*refs):
        gp = refs[:npart]
        w_ref, m_ref, v_ref, g_out, d_out, m_out, v_out = refs[npart:]
        g = gp[0][...]
        for p in gp[1:]:
            g = g + p[...]
        mm = ADAM_B1 * m_ref[...] + (1.0 - ADAM_B1) * g
        vv = ADAM_B2 * v_ref[...] + (1.0 - ADAM_B2) * jnp.square(g)
        m_hat = mm / (1.0 - ADAM_B1 ** ADAM_STEP)
        v_hat = vv / (1.0 - ADAM_B2 ** ADAM_STEP)
        g_out[...] = g
        d_out[...] = -ADAM_LR * (m_hat / (jnp.sqrt(v_hat) + ADAM_EPS) + ADAM_WD * w_ref[...])
        m_out[...] = mm
        v_out[...] = vv

    spec = pl.BlockSpec((tr, cols), lambda i: (i, 0))
    shp = jax.ShapeDtypeStruct((rows, cols), F32)
    return _call(body, name=name, out_shape=[shp] * 4, grid=(rows // tr,),
                 in_specs=[spec] * (npart + 3), out_specs=[spec] * 4,
                 compiler_params=_params(("parallel",)))(*grad_parts, w, m, v)


def _in_proj(x, g1, w_in):
    s = x.shape[0]
    ts = 512

    def body(x_ref, g_ref, w_ref, u_ref, q_ref, k_ref, v_ref, h_ref):
        xv = x_ref[...]
        h = (xv * _rstd(xv) * g_ref[...]).astype(BF16)
        h_ref[...] = h
        u_ref[...] = _dot(h, w_ref[0])
        q_ref[...] = (_dot(h, w_ref[1]) * Q_SCALE).astype(BF16)
        k_ref[...] = _dot(h, w_ref[2]).astype(BF16)
        v_ref[...] = _dot(h, w_ref[3]).astype(BF16)

    row = lambda w: pl.BlockSpec((ts, w), lambda i: (i, 0))
    half = jax.ShapeDtypeStruct((s, D_POOL), BF16)
    return _call(
        body, name="in_proj",
        out_shape=[jax.ShapeDtypeStruct((s, D_POOL), F32), half, half, half,
                   jax.ShapeDtypeStruct((s, D_MODEL), BF16)],
        grid=(s // ts,),
        in_specs=[row(D_MODEL), _full((1, D_MODEL)), _full(w_in.shape)],
        out_specs=[row(D_POOL)] * 4 + [row(D_MODEL)],
        compiler_params=_params(("parallel",)))(x, g1, w_in)


def _pool_means(ext_ref, g, window, ts, row0):
    cols = slice(g * POOL_GROUP, (g + 1) * POOL_GROUP)
    cur = ext_ref[POOL_HALO:POOL_HALO + ts, cols]
    acc = cur
    for d in range(1, window):
        acc = acc + ext_ref[POOL_HALO - d:POOL_HALO - d + ts, cols]
    t1 = row0 + 1 + lax.broadcasted_iota(jnp.int32, (ts, 1), 0)
    cnt = jnp.minimum(t1, window).astype(F32)
    return acc / cnt - cur, cnt


def _pool_fwd(u, w_pool, pool_scale):
    s = u.shape[0]
    ts = 512
    per = ts // POOL_HALO

    def body(u_ref, halo_ref, wp_ref, ps_ref, o_ref, ext_ref, y_ref):
        i = pl.program_id(0)
        ext_ref[0:POOL_HALO, :] = jnp.where(i > 0, halo_ref[...], 0.0)
        ext_ref[POOL_HALO:, :] = u_ref[...]
        for g, window in enumerate(POOL_WINDOWS):
            p, _ = _pool_means(ext_ref, g, window, ts, i * ts)
            y_ref[:, g * POOL_GROUP:(g + 1) * POOL_GROUP] = _dot(
                p.astype(BF16), wp_ref[g].astype(BF16))
        y = y_ref[...]
        o_ref[...] = (y * _rstd(y) * ps_ref[...]).astype(BF16)

    return _call(
        body, name="pool_fwd", out_shape=jax.ShapeDtypeStruct((s, D_POOL), BF16),
        grid=(s // ts,),
        in_specs=[pl.BlockSpec((ts, D_POOL), lambda i: (i, 0)),
                  pl.BlockSpec((POOL_HALO, D_POOL), lambda i: (jnp.maximum(i * per - 1, 0), 0)),
                  _full(w_pool.shape), _full((1, D_POOL))],
        out_specs=pl.BlockSpec((ts, D_POOL), lambda i: (i, 0)),
        scratch_shapes=[pltpu.VMEM((ts + POOL_HALO, D_POOL), F32), pltpu.VMEM((ts, D_POOL), F32)],
        compiler_params=_params(("parallel",)))(u, u, w_pool, pool_scale)


def _split_bf16(v):
    hi = v.astype(BF16)
    lo = (v - hi.astype(F32)).astype(BF16)
    return hi, lo


def _tri(kind):
    r = lax.broadcasted_iota(jnp.int32, (ATT_BLOCK, ATT_BLOCK), 0)
    c = lax.broadcasted_iota(jnp.int32, (ATT_BLOCK, ATT_BLOCK), 1)
    return jnp.where(r > c if kind == "suffix" else r < c, 1.0, 0.0).astype(BF16)


def _causal_mask():
    r = lax.broadcasted_iota(jnp.int32, (ATT_BLOCK, ATT_BLOCK), 0)
    c = lax.broadcasted_iota(jnp.int32, (ATT_BLOCK, ATT_BLOCK), 1)
    return c < r


def _softplus(z, with_sigmoid=False):
    ope = 1.0 + jnp.exp(jnp.minimum(z, 80.0))
    sp = jnp.maximum(z, jnp.log(ope))
    if with_sigmoid:
        return sp, 1.0 - 1.0 / ope
    return sp


def _attn_fwd(q, k, v):
    s = q.shape[0]
    tb = ATT_BLOCK
    nq = s // tb

    def body(q_ref, k_ref, v_ref, o_ref, t_ref):
        i = pl.program_id(1)
        upper = _tri("suffix")
        upper2 = jnp.concatenate([upper, upper], axis=0)
        causal = _causal_mask()
        lane = lax.broadcasted_iota(jnp.int32, (1, HEAD_PAIR), 1)
        first = lane < 64
        q2 = q_ref[...]
        zero = jnp.zeros_like(q2)
        qs = (jnp.where(first, q2, zero), jnp.where(first, zero, q2))

        def step(j, carry, masked):
            cs, o = carry[:2], carry[2]
            start = pl.multiple_of(j * tb, tb)
            kj = k_ref[pl.ds(start, tb), :]
            vj = v_ref[pl.ds(start, tb), :]
            vcat = jnp.concatenate([jnp.where(first, vj, zero), jnp.where(first, zero, vj)], axis=0)
            probs, new_cs = [], []
            for e in range(2):
                z = _dot_nt(qs[e], kj)
                sp = _softplus(z)
                if masked:
                    sp = jnp.where(causal, sp, 0.0)
                hi, lo = _split_bf16(sp)
                suf = _dot(jnp.concatenate([hi, lo], axis=1), upper2)
                a = jnp.exp(z - sp - suf - cs[e])
                if masked:
                    a = jnp.where(causal, a, 0.0)
                probs.append(a.astype(BF16))
                new_cs.append(cs[e] + suf[:, 0:1] + sp[:, 0:1])
            o = o + _dot(jnp.concatenate(probs, axis=1), vcat)
            return new_cs[0], new_cs[1], o

        def two_steps(n, carry):
            return step(i - 2 - 2 * n, step(i - 1 - 2 * n, carry, False), False)

        col = jnp.zeros((tb, 1), F32)
        carry = step(i, (col, col, jnp.zeros((tb, HEAD_PAIR), F32)), True)
        carry = lax.fori_loop(0, i // 2, two_steps, carry)
        carry = lax.cond(i % 2 == 1, lambda cr: step(0, cr, False), lambda cr: cr, carry)
        t_ref[:, 0:1] = carry[0]
        t_ref[:, 1:2] = carry[1]
        o_ref[...] = carry[2]

    return _call(
        body, name="attn_fwd",
        out_shape=[jax.ShapeDtypeStruct((s, D_ATTN), F32),
                   jax.ShapeDtypeStruct((4, s, 2), F32)],
        grid=(4, nq),
        in_specs=[pl.BlockSpec((tb, HEAD_PAIR), lambda h, i: (i, h)),
                  pl.BlockSpec((s, HEAD_PAIR), lambda h, i: (0, h)),
                  pl.BlockSpec((s, HEAD_PAIR), lambda h, i: (0, h))],
        out_specs=[pl.BlockSpec((tb, HEAD_PAIR), lambda h, i: (i, h)),
                   pl.BlockSpec((None, tb, 2), lambda h, i: (h, i, 0))],
        compiler_params=_params(("parallel", "arbitrary")))(q, k, v)


def _mix_out(attn, mpool, x, attn_scale, w_out, g2, g3):
    s = x.shape[0]
    ts = 512

    def body(a_ref, p_ref, x_ref, as_ref, w_ref, g2_ref, g3_ref, ma_ref, mix_ref, x1_ref, h2_ref):
        ao = a_ref[...]
        ma = (ao * _rstd(ao) * as_ref[...]).astype(BF16)
        ma_ref[...] = ma
        mix = _dot(p_ref[...], w_ref[0:D_POOL, :]) + _dot(ma, w_ref[D_POOL:, :])
        mix_ref[...] = mix
        x1 = x_ref[...] + mix * _rstd(mix) * g2_ref[...]
        x1_ref[...] = x1
        h2_ref[...] = (x1 * _rstd(x1) * g3_ref[...]).astype(BF16)

    row = lambda w: pl.BlockSpec((ts, w), lambda i: (i, 0))
    return _call(
        body, name="mix_out",
        out_shape=[jax.ShapeDtypeStruct((s, D_ATTN), BF16), jax.ShapeDtypeStruct((s, D_MODEL), F32),
                   jax.ShapeDtypeStruct((s, D_MODEL), F32), jax.ShapeDtypeStruct((s, D_MODEL), BF16)],
        grid=(s // ts,),
        in_specs=[row(D_ATTN), row(D_POOL), row(D_MODEL), _full((1, D_ATTN)),
                  _full((D_MODEL, D_MODEL)), _full((1, D_MODEL)), _full((1, D_MODEL))],
        out_specs=[row(D_ATTN), row(D_MODEL), row(D_MODEL), row(D_MODEL)],
        compiler_params=_params(("parallel",)))(attn, mpool, x, attn_scale, w_out, g2, g3)


def _conv_rows(ext_ref, cw, cb, ts):
    y = cb + cw[0:1, :] * ext_ref[CONV_HALO - 2:CONV_HALO - 2 + ts, :]
    y = y + cw[1:2, :] * ext_ref[CONV_HALO - 1:CONV_HALO - 1 + ts, :]
    return y + cw[2:3, :] * ext_ref[CONV_HALO:CONV_HALO + ts, :]


def _sigmoid(v):
    return 1.0 / (1.0 + jnp.exp(-v))


def _ffn_up(h2, w_up, conv_w, conv_b):
    s = h2.shape[0]
    ts = 256
    tn = FF_TILE

    def body(h_ref, wg_ref, wv_ref, cwg_ref, cwv_ref, cbg_ref, cbv_ref,
             ug_ref, uv_ref, f_ref, extg, extv):
        i = pl.program_id(1)

        @pl.when(i == 0)
        def _():
            extg[0:CONV_HALO, :] = jnp.zeros((CONV_HALO, tn), F32)
            extv[0:CONV_HALO, :] = jnp.zeros((CONV_HALO, tn), F32)

        h = h_ref[...]
        ug = _dot(h, wg_ref[...])
        uv = _dot(h, wv_ref[...])
        ug_ref[...] = ug
        uv_ref[...] = uv
        extg[CONV_HALO:, :] = ug
        extv[CONV_HALO:, :] = uv
        gate = _conv_rows(extg, cwg_ref[...], cbg_ref[...], ts)
        val = _conv_rows(extv, cwv_ref[...], cbv_ref[...], ts)
        f_ref[...] = (gate * _sigmoid(gate) * val).astype(BF16)
        extg[0:CONV_HALO, :] = extg[ts:ts + CONV_HALO, :]
        extv[0:CONV_HALO, :] = extv[ts:ts + CONV_HALO, :]

    out_blk = pl.BlockSpec((None, ts, tn), lambda n, i: (n, i, 0))
    act = jax.ShapeDtypeStruct((2, s, tn), F32)
    return _call(
        body, name="ffn_up",
        out_shape=[act, act, jax.ShapeDtypeStruct((2, s, tn), BF16)],
        grid=(2, s // ts),
        in_specs=[pl.BlockSpec((ts, D_MODEL), lambda n, i: (i, 0)),
                  pl.BlockSpec((None, D_MODEL, tn), lambda n, i: (n, 0, 0)),
                  pl.BlockSpec((None, D_MODEL, tn), lambda n, i: (n + 2, 0, 0)),
                  pl.BlockSpec((None, 3, tn), lambda n, i: (n, 0, 0)),
                  pl.BlockSpec((None, 3, tn), lambda n, i: (n + 2, 0, 0)),
                  pl.BlockSpec((None, 1, tn), lambda n, i: (n, 0, 0)),
                  pl.BlockSpec((None, 1, tn), lambda n, i: (n + 2, 0, 0))],
        out_specs=[out_blk, out_blk, out_blk],
        scratch_shapes=[pltpu.VMEM((ts + CONV_HALO, tn), F32), pltpu.VMEM((ts + CONV_HALO, tn), F32)],
        compiler_params=_params(("arbitrary", "arbitrary")))(
            h2, w_up, w_up, conv_w, conv_w, conv_b, conv_b)


def _ffn_down(f_in, w_down, x1, target, g4):
    s = x1.shape[0]
    ts = 512

    def body(f_ref, w_ref, x1_ref, t_ref, g_ref, df_ref, dy_ref, loss_ref, dg_ref):
        @pl.when(pl.program_id(0) == 0)
        def _():
            loss_ref[...] = jnp.zeros_like(loss_ref)
            dg_ref[...] = jnp.zeros_like(dg_ref)

        f = _dot(f_ref[0], w_ref[0:FF_TILE, :]) + _dot(f_ref[1], w_ref[FF_TILE:, :])
        rf = _rstd(f)
        fn = f * rf
        g = g_ref[...]
        err = (x1_ref[...] + fn * g) - t_ref[...]
        loss_ref[...] += 0.5 * jnp.sum(jnp.mean(err * err, axis=-1))
        dy = err * (1.0 / D_MODEL)
        dy_ref[...] = dy
        dg_ref[...] += jnp.sum(dy * fn, axis=0, keepdims=True)
        dfn = dy * g
        df_ref[...] = (rf * (dfn - fn * jnp.mean(dfn * fn, axis=-1, keepdims=True))).astype(BF16)

    row = pl.BlockSpec((ts, D_MODEL), lambda i: (i, 0))
    return _call(
        body, name="ffn_down",
        out_shape=[jax.ShapeDtypeStruct((s, D_MODEL), BF16), jax.ShapeDtypeStruct((s, D_MODEL), F32),
                   jax.ShapeDtypeStruct((8, 128), F32), jax.ShapeDtypeStruct((1, D_MODEL), F32)],
        grid=(s // ts,),
        in_specs=[pl.BlockSpec((2, ts, FF_TILE), lambda i: (0, i, 0)), _full((D_FF, D_MODEL)),
                  row, row, _full((1, D_MODEL))],
        out_specs=[row, row, _full((8, 128)), _full((1, D_MODEL))],
        compiler_params=_params(("arbitrary",)))(f_in, w_down, x1, target, g4)


def _tn_matmul(a, b, name, ts=512):
    na, s, ka = a.shape
    nb, _, nbc = b.shape

    def body(a_ref, b_ref, o_ref):
        @pl.when(pl.program_id(2) == 0)
        def _():
            o_ref[...] = jnp.zeros_like(o_ref)

        o_ref[...] += _dot_tn(a_ref[...].astype(BF16), b_ref[...].astype(BF16))

    return _call(
        body, name=name, out_shape=jax.ShapeDtypeStruct((na, nb, ka, nbc), F32),
        grid=(na, nb, s // ts),
        in_specs=[pl.BlockSpec((None, ts, ka), lambda i, j, r: (i, r, 0)),
                  pl.BlockSpec((None, ts, nbc), lambda i, j, r: (j, r, 0))],
        out_specs=pl.BlockSpec((None, None, ka, nbc), lambda i, j, r: (i, j, 0, 0)),
        compiler_params=_params(("parallel", "parallel", "arbitrary")))(a, b)


def _ffn_bwd_act(df, w_down, upre_g, upre_v, conv_w, conv_b):
    s = df.shape[0]
    ts = 256
    tn = FF_TILE
    nr = s // ts
    per = ts // CONV_HALO

    def body(df_ref, wd_ref, ug_ref, uv_ref, hg_ref, hv_ref, cwg_ref, cwv_ref, cbg_ref, cbv_ref,
             dug_ref, duv_ref, dcwg_ref, dcwv_ref, dcbg_ref, dcbv_ref, extg, extv, dxg, dxv):
        i = pl.program_id(1)
        first_rows = i == nr - 1

        @pl.when(i == 0)
        def _():
            dxg[ts:, :] = jnp.zeros((CONV_HALO, tn), F32)
            dxv[ts:, :] = jnp.zeros((CONV_HALO, tn), F32)
            for r in (dcwg_ref, dcwv_ref, dcbg_ref, dcbv_ref):
                r[...] = jnp.zeros_like(r)

        extg[0:CONV_HALO, :] = jnp.where(first_rows, 0.0, hg_ref[...])
        extv[0:CONV_HALO, :] = jnp.where(first_rows, 0.0, hv_ref[...])
        extg[CONV_HALO:, :] = ug_ref[...]
        extv[CONV_HALO:, :] = uv_ref[...]
        cwg, cwv = cwg_ref[...], cwv_ref[...]
        gate = _conv_rows(extg, cwg, cbg_ref[...], ts)
        val = _conv_rows(extv, cwv, cbv_ref[...], ts)
        sg = _sigmoid(gate)
        dfin = _dot_nt(df_ref[...], wd_ref[...])
        dval = dfin * (gate * sg)
        dgate = dfin * val * (sg * (1.0 + gate * (1.0 - sg)))

        def conv_bwd(dact, ext, dx, cw, dcw_ref, dcb_ref, du_ref):
            dx[0:ts, :] = dact
            dcb_ref[...] += jnp.sum(dact, axis=0, keepdims=True)
            for kk in range(3):
                lo = CONV_HALO - 2 + kk
                dcw_ref[kk:kk + 1, :] += jnp.sum(dact * ext[lo:lo + ts, :], axis=0, keepdims=True)
            du = cw[2:3, :] * dact + cw[1:2, :] * dx[1:1 + ts, :] + cw[0:1, :] * dx[2:2 + ts, :]
            du_ref[...] = du.astype(BF16)
            dx[ts:, :] = dx[0:CONV_HALO, :]

        conv_bwd(dgate, extg, dxg, cwg, dcwg_ref, dcbg_ref, dug_ref)
        conv_bwd(dval, extv, dxv, cwv, dcwv_ref, dcbv_ref, duv_ref)

    rows = lambda n, i: (n, nr - 1 - i, 0)
    halo = lambda n, i: (n, jnp.maximum((nr - 1 - i) * per - 1, 0), 0)
    act_blk = pl.BlockSpec((None, ts, tn), rows)
    halo_blk = pl.BlockSpec((None, CONV_HALO, tn), halo)
    cw_blk = lambda off: pl.BlockSpec((None, 3, tn), lambda n, i: (n + off, 0, 0))
    cb_blk = lambda off: pl.BlockSpec((None, 1, tn), lambda n, i: (n + off, 0, 0))
    acc_w = pl.BlockSpec((None, 3, tn), lambda n, i: (n, 0, 0))
    acc_b = pl.BlockSpec((None, 1, tn), lambda n, i: (n, 0, 0))
    dact = jax.ShapeDtypeStruct((2, s, tn), BF16)
    return _call(
        body, name="ffn_bwd_act",
        out_shape=[dact, dact, jax.ShapeDtypeStruct((2, 3, tn), F32), jax.ShapeDtypeStruct((2, 3, tn), F32),
                   jax.ShapeDtypeStruct((2, 1, tn), F32), jax.ShapeDtypeStruct((2, 1, tn), F32)],
        grid=(2, nr),
        in_specs=[pl.BlockSpec((ts, D_MODEL), lambda n, i: (nr - 1 - i, 0)),
                  pl.BlockSpec((tn, D_MODEL), lambda n, i: (n, 0)),
                  act_blk, act_blk, halo_blk, halo_blk,
                  cw_blk(0), cw_blk(2), cb_blk(0), cb_blk(2)],
        out_specs=[act_blk, act_blk, acc_w, acc_w, acc_b, acc_b],
        scratch_shapes=[pltpu.VMEM((ts + CONV_HALO, tn), F32)] * 4,
        compiler_params=_params(("arbitrary", "arbitrary")))(
            df, w_down, upre_g, upre_v, upre_g, upre_v, conv_w, conv_w, conv_b, conv_b)


def _ffn_bwd_in(dug, duv, w_up, x1, dy, mix, g3, g2):
    s = x1.shape[0]
    ts = 256

    def body(dg_ref, dv_ref, w_ref, x1_ref, dy_ref, mix_ref, g3_ref, g2_ref,
             dx1_ref, dmix_ref, dg3_ref, dg2_ref):
        @pl.when(pl.program_id(0) == 0)
        def _():
            dg3_ref[...] = jnp.zeros_like(dg3_ref)
            dg2_ref[...] = jnp.zeros_like(dg2_ref)

        dh = _dot_nt(dg_ref[0], w_ref[0]) + _dot_nt(dg_ref[1], w_ref[1])
        dh = dh + _dot_nt(dv_ref[0], w_ref[2]) + _dot_nt(dv_ref[1], w_ref[3])
        x1 = x1_ref[...]
        r3 = _rstd(x1)
        xn = x1 * r3
        dg3_ref[...] += jnp.sum(dh * xn, axis=0, keepdims=True)
        dxn = dh * g3_ref[...]
        dx1 = dy_ref[...] + r3 * (dxn - xn * jnp.mean(dxn * xn, axis=-1, keepdims=True))
        dx1_ref[...] = dx1
        mix = mix_ref[...]
        rm = _rstd(mix)
        mn = mix * rm
        dg2_ref[...] += jnp.sum(dx1 * mn, axis=0, keepdims=True)
        dmn = dx1 * g2_ref[...]
        dmix_ref[...] = (rm * (dmn - mn * jnp.mean(dmn * mn, axis=-1, keepdims=True))).astype(BF16)

    row = pl.BlockSpec((ts, D_MODEL), lambda i: (i, 0))
    act = pl.BlockSpec((2, ts, FF_TILE), lambda i: (0, i, 0))
    vec = _full((1, D_MODEL))
    return _call(
        body, name="ffn_bwd_in",
        out_shape=[jax.ShapeDtypeStruct((s, D_MODEL), F32), jax.ShapeDtypeStruct((s, D_MODEL), BF16),
                   jax.ShapeDtypeStruct((1, D_MODEL), F32), jax.ShapeDtypeStruct((1, D_MODEL), F32)],
        grid=(s // ts,),
        in_specs=[act, act, _full(w_up.shape), row, row, row, vec, vec],
        out_specs=[row, row, vec, vec],
        compiler_params=_params(("arbitrary",), vmem_mb=56))(dug, duv, w_up, x1, dy, mix, g3, g2)


def _mix_bwd(dmix, w_out, attn, attn_scale):
    s = dmix.shape[0]
    ts = 512

    def body(dm_ref, w_ref, a_ref, as_ref, dp_ref, do_ref, das_ref):
        @pl.when(pl.program_id(0) == 0)
        def _():
            das_ref[...] = jnp.zeros_like(das_ref)

        dm = dm_ref[...]
        dp_ref[...] = _dot_nt(dm, w_ref[0:D_POOL, :])
        da = _dot_nt(dm, w_ref[D_POOL:, :])
        ao = a_ref[...]
        ra = _rstd(ao)
        an = ao * ra
        das_ref[...] += jnp.sum(da * an, axis=0, keepdims=True)
        dan = da * as_ref[...]
        do_ref[...] = (ra * (dan - an * jnp.mean(dan * an, axis=-1, keepdims=True))).astype(BF16)

    row = lambda w: pl.BlockSpec((ts, w), lambda i: (i, 0))
    return _call(
        body, name="mix_bwd",
        out_shape=[jax.ShapeDtypeStruct((s, D_POOL), F32), jax.ShapeDtypeStruct((s, D_ATTN), BF16),
                   jax.ShapeDtypeStruct((1, D_ATTN), F32)],
        grid=(s // ts,),
        in_specs=[row(D_MODEL), _full((D_MODEL, D_MODEL)), row(D_ATTN), _full((1, D_ATTN))],
        out_specs=[row(D_POOL), row(D_ATTN), _full((1, D_ATTN))],
        compiler_params=_params(("arbitrary",)))(dmix, w_out, attn, attn_scale)


def _attn_bwd(q, k, v, do, totals):
    s = q.shape[0]
    tb = ATT_BLOCK
    nq = s // tb

    def body(q_ref, do_ref, t_ref, k_hbm, v_hbm, dq_ref, dk_hbm, dv_hbm,
             k_scr, v_scr, dk_acc, dv_acc):
        hp = pl.program_id(0)
        i = pl.program_id(1)
        lanes = pl.ds(pl.multiple_of(hp * HEAD_PAIR, HEAD_PAIR), HEAD_PAIR)

        @pl.when(i == 0)
        def _():
            pltpu.sync_copy(k_hbm.at[:, lanes], k_scr)
            pltpu.sync_copy(v_hbm.at[:, lanes], v_scr)
            dk_acc[...] = jnp.zeros_like(dk_acc)
            dv_acc[...] = jnp.zeros_like(dv_acc)

        upper = _tri("suffix")
        upper2 = jnp.concatenate([upper, upper], axis=0)
        lower = _tri("prefix")
        lower2 = jnp.concatenate([lower, lower], axis=0)
        causal = _causal_mask()
        lane = lax.broadcasted_iota(jnp.int32, (1, HEAD_PAIR), 1)
        first = lane < 64
        q2 = q_ref[...]
        do2 = do_ref[...]
        zero = jnp.zeros_like(q2)
        qs = (jnp.where(first, q2, zero), jnp.where(first, zero, q2))
        dos = (jnp.where(first, do2, zero), jnp.where(first, zero, do2))
        qcat = jnp.concatenate(qs, axis=0)
        docat = jnp.concatenate(dos, axis=0)
        tots = (t_ref[:, 0:1], t_ref[:, 1:2])

        def step(j, carry, masked):
            cs, cps, dq = carry[0:2], carry[2:4], carry[4]
            start = pl.multiple_of(j * tb, tb)
            kj = k_scr[pl.ds(start, tb), :]
            vj = v_scr[pl.ds(start, tb), :]
            kcat = jnp.concatenate([jnp.where(first, kj, zero), jnp.where(first, zero, kj)], axis=0)
            dzs, probs, new_cs, new_cps = [], [], [], []
            for e in range(2):
                z = _dot_nt(qs[e], kj)
                sp, sig = _softplus(z, True)
                if masked:
                    sp = jnp.where(causal, sp, 0.0)
                hi, lo = _split_bf16(sp)
                suf = _dot(jnp.concatenate([hi, lo], axis=1), upper2)
                c = cs[e] + suf[:, 0:1] + sp[:, 0:1]
                a = jnp.exp(z - sp - suf - (tots[e] - c))
                if masked:
                    a = jnp.where(causal, a, 0.0)
                dw = a * _dot_nt(dos[e], vj)
                hi, lo = _split_bf16(dw)
                pre = _dot(jnp.concatenate([hi, lo], axis=1), lower2)
                dz = dw - sig * (dw + pre + cps[e])
                if masked:
                    dz = jnp.where(causal, dz, 0.0)
                new_cs.append(c)
                new_cps.append(cps[e] + pre[:, tb - 1:tb] + dw[:, tb - 1:tb])
                dzs.append(dz.astype(BF16))
                probs.append(a.astype(BF16))
            dq = dq + _dot(jnp.concatenate(dzs, axis=1), kcat)
            dk_acc[pl.ds(start, tb), :] += _dot_tn(jnp.concatenate(dzs, axis=0), qcat)
            dv_acc[pl.ds(start, tb), :] += _dot_tn(jnp.concatenate(probs, axis=0), docat)
            return new_cs[0], new_cs[1], new_cps[0], new_cps[1], dq

        col = jnp.zeros((tb, 1), F32)
        carry = (col, col, col, col, jnp.zeros((tb, HEAD_PAIR), F32))
        carry = lax.fori_loop(
            0, i // 2, lambda n, cr: step(2 * n + 1, step(2 * n, cr, False), False), carry)
        carry = lax.cond(i % 2 == 1, lambda cr: step(i - 1, cr, False), lambda cr: cr, carry)
        carry = step(i, carry, True)
        dq_ref[...] = (carry[4] * Q_SCALE).astype(BF16)

        @pl.when(i == nq - 1)
        def _():
            pltpu.sync_copy(dk_acc, dk_hbm.at[:, lanes])
            pltpu.sync_copy(dv_acc, dv_hbm.at[:, lanes])

    blk = pl.BlockSpec((tb, HEAD_PAIR), lambda h, i: (i, h))
    grad = jax.ShapeDtypeStruct((s, D_ATTN), F32)
    return _call(
        body, name="attn_bwd",
        out_shape=[jax.ShapeDtypeStruct((s, D_ATTN), BF16), grad, grad],
        grid=(4, nq),
        in_specs=[blk, blk, pl.BlockSpec((None, tb, 2), lambda h, i: (h, i, 0)), ANY, ANY],
        out_specs=[blk, ANY, ANY],
        scratch_shapes=[pltpu.VMEM((s, HEAD_PAIR), BF16), pltpu.VMEM((s, HEAD_PAIR), BF16),
                        pltpu.VMEM((s, HEAD_PAIR), F32), pltpu.VMEM((s, HEAD_PAIR), F32)],
        compiler_params=_params(("arbitrary", "arbitrary")))(q, do, totals, k, v)


def _pool_bwd(u, dmp, w_pool, pool_scale):
    s = u.shape[0]
    ts = 512
    nr = s // ts
    per = ts // POOL_HALO

    def body(u_ref, halo_ref, dm_ref, wp_ref, ps_ref, du_ref, dwp_ref, dps_ref, ext_ref, y_ref, dext_ref):
        i = pl.program_id(0)
        rb = nr - 1 - i

        @pl.when(i == 0)
        def _():
            dext_ref[ts:, :] = jnp.zeros((POOL_HALO, D_POOL), F32)
            dwp_ref[...] = jnp.zeros_like(dwp_ref)
            dps_ref[...] = jnp.zeros_like(dps_ref)

        ext_ref[0:POOL_HALO, :] = jnp.where(rb > 0, halo_ref[...], 0.0)
        ext_ref[POOL_HALO:, :] = u_ref[...]
        ps, cnts = [], []
        for g, window in enumerate(POOL_WINDOWS):
            p, cnt = _pool_means(ext_ref, g, window, ts, rb * ts)
            ps.append(p.astype(BF16))
            cnts.append(cnt)
            y_ref[:, g * POOL_GROUP:(g + 1) * POOL_GROUP] = _dot(ps[g], wp_ref[g].astype(BF16))
        y = y_ref[...]
        r = _rstd(y)
        yn = y * r
        dm = dm_ref[...]
        dps_ref[...] += jnp.sum(dm * yn, axis=0, keepdims=True)
        dn = dm * ps_ref[...]
        dy = r * (dn - yn * jnp.mean(dn * yn, axis=-1, keepdims=True))
        for g, window in enumerate(POOL_WINDOWS):
            cols = slice(g * POOL_GROUP, (g + 1) * POOL_GROUP)
            dyg = dy[:, cols].astype(BF16)
            dwp_ref[g] += _dot_tn(ps[g], dyg)
            dp = _dot_nt(dyg, wp_ref[g].astype(BF16))
            dext_ref[0:ts, cols] = dp / cnts[g]
            acc = dext_ref[0:ts, cols]
            for d in range(1, window):
                acc = acc + dext_ref[d:d + ts, cols]
            du_ref[:, cols] = (acc - dp).astype(BF16)
        dext_ref[ts:, :] = dext_ref[0:POOL_HALO, :]

    rows = pl.BlockSpec((ts, D_POOL), lambda i: (nr - 1 - i, 0))
    return _call(
        body, name="pool_bwd",
        out_shape=[jax.ShapeDtypeStruct((s, D_POOL), BF16), jax.ShapeDtypeStruct(w_pool.shape, F32),
                   jax.ShapeDtypeStruct((1, D_POOL), F32)],
        grid=(nr,),
        in_specs=[rows,
                  pl.BlockSpec((POOL_HALO, D_POOL), lambda i: (jnp.maximum((nr - 1 - i) * per - 1, 0), 0)),
                  rows, _full(w_pool.shape), _full((1, D_POOL))],
        out_specs=[rows, _full(w_pool.shape), _full((1, D_POOL))],
        scratch_shapes=[pltpu.VMEM((ts + POOL_HALO, D_POOL), F32), pltpu.VMEM((ts, D_POOL), F32),
                        pltpu.VMEM((ts + POOL_HALO, D_POOL), F32)],
        compiler_params=_params(("arbitrary",)))(u, u, dmp, w_pool, pool_scale)


def _in_proj_bwd(du, dq, dk, dv, w_in, x, dx1, g1):
    s = x.shape[0]
    ts = 512

    def body(du_ref, dq_ref, dk_ref, dv_ref, w_ref, x_ref, dx1_ref, g_ref, gx_ref, dg_ref):
        @pl.when(pl.program_id(0) == 0)
        def _():
            dg_ref[...] = jnp.zeros_like(dg_ref)

        dh = _dot_nt(du_ref[...], w_ref[0]) + _dot_nt(dq_ref[...], w_ref[1])
        dh = dh + _dot_nt(dk_ref[...].astype(BF16), w_ref[2]) + _dot_nt(dv_ref[...].astype(BF16), w_ref[3])
        xv = x_ref[...]
        r = _rstd(xv)
        xn = xv * r
        dg_ref[...] += jnp.sum(dh * xn, axis=0, keepdims=True)
        dxn = dh * g_ref[...]
        gx_ref[...] = dx1_ref[...] + r * (dxn - xn * jnp.mean(dxn * xn, axis=-1, keepdims=True))

    row = lambda w: pl.BlockSpec((ts, w), lambda i: (i, 0))
    return _call(
        body, name="in_proj_bwd",
        out_shape=[jax.ShapeDtypeStruct((s, D_MODEL), F32), jax.ShapeDtypeStruct((1, D_MODEL), F32)],
        grid=(s // ts,),
        in_specs=[row(D_POOL)] * 4 + [_full(w_in.shape), row(D_MODEL), row(D_MODEL), _full((1, D_MODEL))],
        out_specs=[row(D_MODEL), _full((1, D_MODEL))],
        compiler_params=_params(("arbitrary",)))(du, dq, dk, dv, w_in, x, dx1, g1)


_SMALL = ("norm_mix_pre", "w_pool", "pool_scale", "attn_scale", "norm_mix_post",
          "norm_ffn_pre", "conv_b", "norm_ffn_post")
_SMALL_SIZE = {"norm_mix_pre": 1024, "w_pool": 65536, "pool_scale": 512, "attn_scale": 512,
               "norm_mix_post": 1024, "norm_ffn_pre": 1024, "conv_b": 5632, "norm_ffn_post": 1024}
_SMALL_ROWS = 600
_CONVW_ROWS = 132
_PACK_ROWS = _SMALL_ROWS + _CONVW_ROWS + 4


def _pack_small(parts):
    flat = jnp.concatenate([parts[n].reshape(-1) for n in _SMALL])
    flat = jnp.pad(flat, (0, _SMALL_ROWS * 128 - flat.shape[0]))
    return flat.reshape(_SMALL_ROWS, 128)


def _unpack_small(packed, like):
    flat = packed.reshape(-1)
    out, off = {}, 0
    for n in _SMALL:
        out[n] = flat[off:off + _SMALL_SIZE[n]].reshape(like[n].shape)
        off += _SMALL_SIZE[n]
    return out


def kernel(x, norm_mix_pre, w_in, w_pool, pool_scale, attn_scale, w_out, norm_mix_post, norm_ffn_pre, w_up, conv_w, conv_b, w_down, norm_ffn_post, loss_target, m_norm_mix_pre, m_w_in, m_w_pool, m_pool_scale, m_attn_scale, m_w_out, m_norm_mix_post, m_norm_ffn_pre, m_w_up, m_conv_w, m_conv_b, m_w_down, m_norm_ffn_post, v_norm_mix_pre, v_w_in, v_w_pool, v_pool_scale, v_attn_scale, v_w_out, v_norm_mix_post, v_norm_ffn_pre, v_w_up, v_conv_w, v_conv_b, v_w_down, v_norm_ffn_post):
    weights = dict(norm_mix_pre=norm_mix_pre, w_in=w_in, w_pool=w_pool, pool_scale=pool_scale,
                   attn_scale=attn_scale, w_out=w_out, norm_mix_post=norm_mix_post,
                   norm_ffn_pre=norm_ffn_pre, w_up=w_up, conv_w=conv_w, conv_b=conv_b,
                   w_down=w_down, norm_ffn_post=norm_ffn_post)
    mom1 = dict(norm_mix_pre=m_norm_mix_pre, w_in=m_w_in, w_pool=m_w_pool, pool_scale=m_pool_scale,
                attn_scale=m_attn_scale, w_out=m_w_out, norm_mix_post=m_norm_mix_post,
                norm_ffn_pre=m_norm_ffn_pre, w_up=m_w_up, conv_w=m_conv_w, conv_b=m_conv_b,
                w_down=m_w_down, norm_ffn_post=m_norm_ffn_post)
    mom2 = dict(norm_mix_pre=v_norm_mix_pre, w_in=v_w_in, w_pool=v_w_pool, pool_scale=v_pool_scale,
                attn_scale=v_attn_scale, w_out=v_w_out, norm_mix_post=v_norm_mix_post,
                norm_ffn_pre=v_norm_ffn_pre, w_up=v_w_up, conv_w=v_conv_w, conv_b=v_conv_b,
                w_down=v_w_down, norm_ffn_post=v_norm_ffn_post)
    order = list(weights)

    xs = x[0]
    target = loss_target[0]
    wp = w_pool[0]
    shard = lax.axis_index("x") * 2 + lax.axis_index("y")

    shards = [_cast_bf16(w_in[0], "cast_w_in"), _cast_bf16(w_out[0], "cast_w_out"),
              _cast_bf16(w_up[0], "cast_w_up"), _cast_bf16(w_down[0], "cast_w_down"), conv_w[0]]
    win_g, wout_g, wup_g, wdown_g, convw_g = _gather_shards(shards)
    wout_f = wout_g.reshape(D_MODEL, D_MODEL)
    wdown_f = wdown_g.reshape(D_FF, D_MODEL)
    convb_g = conv_b[0].reshape(N_SHARD, 1, FF_TILE)

    u, q, k, v, h1 = _in_proj(xs, norm_mix_pre, win_g)
    mpool = _pool_fwd(u, wp, pool_scale)
    attn, totals = _attn_fwd(q, k, v)
    mattn, mix, x1, h2 = _mix_out(attn, mpool, xs, attn_scale, wout_f, norm_mix_post, norm_ffn_pre)
    upre_g, upre_v, f_in = _ffn_up(h2, wup_g, convw_g, convb_g)
    df, dy, loss_tile, d_post = _ffn_down(f_in, wdown_f, x1, target, norm_ffn_post)

    d_wdown = _tn_matmul(f_in, df[None], "dw_down")
    dug, duv, dcw_g, dcw_v, dcb_g, dcb_v = _ffn_bwd_act(df, wdown_f, upre_g, upre_v, convw_g, convb_g)
    d_wup = jnp.concatenate([_tn_matmul(h2[None], dug, "dw_up_gate")[0],
                             _tn_matmul(h2[None], duv, "dw_up_value")[0]], axis=0)
    dx1, dmix, d_ffn_pre, d_mix_post = _ffn_bwd_in(dug, duv, wup_g, x1, dy, mix, norm_ffn_pre, norm_mix_post)
    d_wout = jnp.concatenate([_tn_matmul(mpool[None], dmix[None], "dw_out_pool")[0, 0],
                              _tn_matmul(mattn[None], dmix[None], "dw_out_attn")[0, 0]], axis=0)
    dmp, do, d_attn_scale = _mix_bwd(dmix, wout_f, attn, attn_scale)
    dq, dk, dv = _attn_bwd(q, k, v, do, totals)
    du, d_wpool, d_pool_scale = _pool_bwd(u, dmp, wp, pool_scale)
    d_win = jnp.stack([_tn_matmul(h1[None], t[None], "dw_in_%d" % n)[0, 0]
                       for n, t in enumerate((du, dq, dk, dv))])
    grad_x, d_mix_pre = _in_proj_bwd(du, dq, dk, dv, win_g, xs, dx1, norm_mix_pre)

    d_convw = jnp.concatenate([dcw_g, dcw_v], axis=0)
    d_convb = jnp.concatenate([dcb_g, dcb_v], axis=0).reshape(1, 2 * D_FF)
    small_parts = dict(norm_mix_pre=d_mix_pre, w_pool=d_wpool, pool_scale=d_pool_scale,
                       attn_scale=d_attn_scale, norm_mix_post=d_mix_post, norm_ffn_pre=d_ffn_pre,
                       conv_b=d_convb, norm_ffn_post=d_post)
    packed = jnp.concatenate([_pack_small(small_parts), d_convw.reshape(_CONVW_ROWS, 128),
                              jnp.zeros((4, 128), F32)], axis=0)
    big = [d_win, d_wout.reshape(N_SHARD, D_MODEL // N_SHARD, D_MODEL), d_wup,
           d_wdown.reshape(N_SHARD, D_FF // N_SHARD, D_MODEL)]
    recv, gathered = _scatter_grads(big, packed)
    quarter = [_sum_slots(r, (3, 0, 1, 2), "sum_chips_%d" % n) for n, r in enumerate(recv)]
    sibling = _swap_with_sibling(quarter)
    small_sum = _sum_slots(gathered, tuple(range(8)), "sum_small")

    results = {}
    for n, name in enumerate(("w_in", "w_out", "w_up", "w_down")):
        res = _adamw([quarter[n], sibling[n]], weights[name][0], mom1[name][0], mom2[name][0],
                     "adamw_" + name)
        results[name] = [t[None] for t in res]
    g_convw = lax.dynamic_slice_in_dim(
        small_sum[_SMALL_ROWS:_SMALL_ROWS + _CONVW_ROWS].reshape(N_SHARD, 3, FF_TILE), shard, 1, axis=0)[0]
    convw_pad = lambda t: jnp.pad(t, ((0, 5), (0, 0)))
    res = _adamw([convw_pad(g_convw)], convw_pad(conv_w[0]), convw_pad(m_conv_w[0]),
                 convw_pad(v_conv_w[0]), "adamw_conv_w")
    results["conv_w"] = [t[:3][None] for t in res]
    pack_w = _pack_small(weights)
    pack_m = _pack_small(mom1)
    pack_v = _pack_small(mom2)
    res = _adamw([small_sum[:_SMALL_ROWS]], pack_w, pack_m, pack_v, "adamw_small")
    unpacked = [_unpack_small(t, weights) for t in res]
    for name in _SMALL:
        results[name] = [t[name] for t in unpacked]

    loss = lax.psum(loss_tile[0, 0], ("x", "y", "c"))
    outs = [loss, grad_x[None]]
    for slot in range(4):
        outs.extend(results[name][slot] for name in order)
    return tuple(outs)
```

```python
import functools

import jax
import jax.numpy as jnp
from jax import lax
from jax.experimental import pallas as pl
from jax.experimental.pallas import tpu as pltpu

F32 = jnp.float32
BF16 = jnp.bfloat16

D_MODEL = 1024
D_POOL = 512
D_ATTN = 512
POOL_WINDOWS = (2, 4, 8, 16)
POOL_GROUP = 128
POOL_HALO = 16
CONV_HALO = 8
D_FF = 2816
FF_TILE = 1408
N_SHARD = 4
EPS = 1e-6
Q_SCALE = 0.125
ATT_BLOCK = 256
HEAD_PAIR = 128
MIB = 1 << 20
LOG2E = 1.4426950408889634

ADAM_LR = 0.001
ADAM_B1 = 0.9
ADAM_B2 = 0.999
ADAM_EPS = 1e-08
ADAM_WD = 0.01
ADAM_STEP = 10

NT_DIMS = (((1,), (1,)), ((), ()))
TN_DIMS = (((0,), (0,)), ((), ()))
MESH = pl.DeviceIdType.MESH
ANY = pl.BlockSpec(memory_space=pl.ANY)


def _call(body, **kw):
    return pl.pallas_call(body, **kw)


def _params(sem=None, vmem_mb=48):
    return pltpu.CompilerParams(dimension_semantics=sem, vmem_limit_bytes=vmem_mb * MIB)


def _rstd(v):
    return lax.rsqrt(jnp.mean(v * v, axis=-1, keepdims=True) + EPS)


def _dot(a, b):
    return jnp.dot(a, b, preferred_element_type=F32)


def _dot_nt(a, b):
    return lax.dot_general(a, b, NT_DIMS, preferred_element_type=F32)


def _dot_tn(a, b):
    return lax.dot_general(a, b, TN_DIMS, preferred_element_type=F32)


def _row_tile(rows, cap):
    t = min(rows, cap)
    t -= t % 8
    while rows % t:
        t -= 8
    return t


def _full(shape):
    nd = len(shape)
    return pl.BlockSpec(shape, lambda *_: (0,) * nd)


def _chip_peers():
    x, y, c = lax.axis_index("x"), lax.axis_index("y"), lax.axis_index("c")
    return x, y, c, [(1 - x, y), (x, 1 - y), (1 - x, 1 - y)]


def _cast_bf16(a, name):
    def body(a_ref, o_ref):
        o_ref[...] = a_ref[...].astype(BF16)

    return _call(body, name=name, out_shape=jax.ShapeDtypeStruct(a.shape, BF16),
                 grid=(1,), in_specs=[_full(a.shape)], out_specs=_full(a.shape),
                 compiler_params=_params(("arbitrary",)))(a)


def _gather_shards(shards):
    n = len(shards)

    def body(*refs):
        ins, outs = refs[:n], refs[n:2 * n]
        send, recv, loc = refs[2 * n:]
        x, y, c, chips = _chip_peers()
        b = 2 * x + y
        local = [pltpu.make_async_copy(ins[t], outs[t].at[b], loc.at[t]) for t in range(n)]
        for cp in local:
            cp.start()
        remote = []
        for t in range(n):
            for k, (px, py) in enumerate(chips):
                remote.append(pltpu.make_async_remote_copy(
                    src_ref=ins[t], dst_ref=outs[t].at[b],
                    send_sem=send.at[3 * t + k], recv_sem=recv.at[3 * t + k],
                    device_id=(px, py, c), device_id_type=MESH))
        for cp in remote:
            cp.start()
        for cp in remote:
            cp.wait()
        for cp in local:
            cp.wait()

    return _call(
        body, name="gather_weights",
        out_shape=[jax.ShapeDtypeStruct((N_SHARD,) + s.shape, s.dtype) for s in shards],
        in_specs=[ANY] * n, out_specs=[ANY] * n,
        scratch_shapes=[pltpu.SemaphoreType.DMA((3 * n,)), pltpu.SemaphoreType.DMA((3 * n,)),
                        pltpu.SemaphoreType.DMA((n,))],
    )(*shards)


def _scatter_grads(grads, small):
    n = len(grads)

    def body(*refs):
        ins, small_in = refs[:n], refs[n]
        outs, small_out = refs[n + 1:2 * n + 1], refs[2 * n + 1]
        send, recv, loc, ssend, srecv = refs[2 * n + 2:]
        x, y, c, chips = _chip_peers()
        b = 2 * x + y
        me = 4 * x + 2 * y + c
        local = [pltpu.make_async_copy(ins[t].at[b], outs[t].at[3], loc.at[t]) for t in range(n)]
        local.append(pltpu.make_async_copy(small_in, small_out.at[me], loc.at[n]))
        for cp in local:
            cp.start()
        remote = []
        for t in range(n):
            for k, (px, py) in enumerate(chips):
                remote.append(pltpu.make_async_remote_copy(
                    src_ref=ins[t].at[2 * px + py], dst_ref=outs[t].at[k],
                    send_sem=send.at[3 * t + k], recv_sem=recv.at[3 * t + k],
                    device_id=(px, py, c), device_id_type=MESH))
        for r in range(1, 8):
            px = 1 - x if r & 4 else x
            py = 1 - y if r & 2 else y
            pc = 1 - c if r & 1 else c
            remote.append(pltpu.make_async_remote_copy(
                src_ref=small_in, dst_ref=small_out.at[me],
                send_sem=ssend.at[r - 1], recv_sem=srecv.at[r - 1],
                device_id=(px, py, pc), device_id_type=MESH))
        for cp in remote:
            cp.start()
        for cp in remote:
            cp.wait()
        for cp in local:
            cp.wait()

    out_shape = [jax.ShapeDtypeStruct(g.shape, g.dtype) for g in grads]
    out_shape.append(jax.ShapeDtypeStruct((8,) + small.shape, small.dtype))
    res = _call(
        body, name="scatter_grads", out_shape=out_shape,
        in_specs=[ANY] * (n + 1), out_specs=[ANY] * (n + 1),
        scratch_shapes=[pltpu.SemaphoreType.DMA((3 * n,)), pltpu.SemaphoreType.DMA((3 * n,)),
                        pltpu.SemaphoreType.DMA((n + 1,)),
                        pltpu.SemaphoreType.DMA((7,)), pltpu.SemaphoreType.DMA((7,))],
    )(*grads, small)
    return res[:n], res[n]


def _swap_with_sibling(parts):
    n = len(parts)

    def body(*refs):
        ins, outs = refs[:n], refs[n:2 * n]
        send, recv = refs[2 * n:]
        x, y, c = lax.axis_index("x"), lax.axis_index("y"), lax.axis_index("c")
        copies = [pltpu.make_async_remote_copy(
            src_ref=ins[t], dst_ref=outs[t], send_sem=send.at[t], recv_sem=recv.at[t],
            device_id=(x, y, 1 - c), device_id_type=MESH) for t in range(n)]
        for cp in copies:
            cp.start()
        for cp in copies:
            cp.wait()

    return _call(
        body, name="swap_sibling",
        out_shape=[jax.ShapeDtypeStruct(p.shape, p.dtype) for p in parts],
        in_specs=[ANY] * n, out_specs=[ANY] * n,
        scratch_shapes=[pltpu.SemaphoreType.DMA((n,)), pltpu.SemaphoreType.DMA((n,))],
    )(*parts)


def _sum_slots(buf, order, name):
    k, rows, cols = buf.shape
    tr = _row_tile(rows, 256)

    def body(b_ref, o_ref):
        acc = b_ref[order[0]]
        for s in order[1:]:
            acc = acc + b_ref[s]
        o_ref[...] = acc

    return _call(body, name=name, out_shape=jax.ShapeDtypeStruct((rows, cols), F32),
                 grid=(rows // tr,),
                 in_specs=[pl.BlockSpec((k, tr, cols), lambda i: (0, i, 0))],
                 out_specs=pl.BlockSpec((tr, cols), lambda i: (i, 0)),
                 compiler_params=_params(("parallel",)))(buf)


def _adamw(grad_parts, w, m, v, name):
    rows, cols = w.shape
    tr = _row_tile(rows, 256)
    npart = len(grad_parts)

    def body(*refs):
        gp = refs[:npart]
        w_ref, m_ref, v_ref, g_out, d_out, m_out, v_out = refs[npart:]
        g = gp[0][...]
        for p in gp[1:]:
            g = g + p[...]
        mm = ADAM_B1 * m_ref[...] + (1.0 - ADAM_B1) * g
        vv = ADAM_B2 * v_ref[...] + (1.0 - ADAM_B2) * jnp.square(g)
        m_hat = mm / (1.0 - ADAM_B1 ** ADAM_STEP)
        v_hat = vv / (1.0 - ADAM_B2 ** ADAM_STEP)
        g_out[...] = g
        d_out[...] = -ADAM_LR * (m_hat / (jnp.sqrt(v_hat) + ADAM_EPS) + ADAM_WD * w_ref[...])
        m_out[...] = mm
        v_out[...] = vv

    spec = pl.BlockSpec((tr, cols), lambda i: (i, 0))
    shp = jax.ShapeDtypeStruct((rows, cols), F32)
    return _call(body, name=name, out_shape=[shp] * 4, grid=(rows // tr,),
                 in_specs=[spec] * (npart + 3), out_specs=[spec] * 4,
                 compiler_params=_params(("parallel",)))(*grad_parts, w, m, v)


def _in_proj(x, g1, w_in):
    s = x.shape[0]
    ts = 512

    def body(x_ref, g_ref, w_ref, u_ref, q_ref, k_ref, v_ref, h_ref):
        xv = x_ref[...]
        h = (xv * _rstd(xv) * g_ref[...]).astype(BF16)
        h_ref[...] = h
        u_ref[...] = _dot(h, w_ref[0])
        q_ref[...] = (_dot(h, w_ref[1]) * Q_SCALE).astype(BF16)
        k_ref[...] = _dot(h, w_ref[2]).astype(BF16)
        v_ref[...] = _dot(h, w_ref[3]).astype(BF16)

    row = lambda w: pl.BlockSpec((ts, w), lambda i: (i, 0))
    half = jax.ShapeDtypeStruct((s, D_POOL), BF16)
    return _call(
        body, name="in_proj",
        out_shape=[jax.ShapeDtypeStruct((s, D_POOL), F32), half, half, half,
                   jax.ShapeDtypeStruct((s, D_MODEL), BF16)],
        grid=(s // ts,),
        in_specs=[row(D_MODEL), _full((1, D_MODEL)), _full(w_in.shape)],
        out_specs=[row(D_POOL)] * 4 + [row(D_MODEL)],
        compiler_params=_params(("parallel",)))(x, g1, w_in)


def _pool_means(ext_ref, g, window, ts, row0):
    cols = slice(g * POOL_GROUP, (g + 1) * POOL_GROUP)
    cur = ext_ref[POOL_HALO:POOL_HALO + ts, cols]
    acc = cur
    for d in range(1, window):
        acc = acc + ext_ref[POOL_HALO - d:POOL_HALO - d + ts, cols]
    t1 = row0 + 1 + lax.broadcasted_iota(jnp.int32, (ts, 1), 0)
    cnt = jnp.minimum(t1, window).astype(F32)
    return acc / cnt - cur, cnt


def _pool_fwd(u, w_pool, pool_scale):
    s = u.shape[0]
    ts = 512
    per = ts // POOL_HALO

    def body(u_ref, halo_ref, wp_ref, ps_ref, o_ref, ext_ref, y_ref):
        i = pl.program_id(0)
        ext_ref[0:POOL_HALO, :] = jnp.where(i > 0, halo_ref[...], 0.0)
        ext_ref[POOL_HALO:, :] = u_ref[...]
        for g, window in enumerate(POOL_WINDOWS):
            p, _ = _pool_means(ext_ref, g, window, ts, i * ts)
            y_ref[:, g * POOL_GROUP:(g + 1) * POOL_GROUP] = _dot(
                p.astype(BF16), wp_ref[g].astype(BF16))
        y = y_ref[...]
        o_ref[...] = (y * _rstd(y) * ps_ref[...]).astype(BF16)

    return _call(
        body, name="pool_fwd", out_shape=jax.ShapeDtypeStruct((s, D_POOL), BF16),
        grid=(s // ts,),
        in_specs=[pl.BlockSpec((ts, D_POOL), lambda i: (i, 0)),
                  pl.BlockSpec((POOL_HALO, D_POOL), lambda i: (jnp.maximum(i * per - 1, 0), 0)),
                  _full(w_pool.shape), _full((1, D_POOL))],
        out_specs=pl.BlockSpec((ts, D_POOL), lambda i: (i, 0)),
        scratch_shapes=[pltpu.VMEM((ts + POOL_HALO, D_POOL), F32), pltpu.VMEM((ts, D_POOL), F32)],
        compiler_params=_params(("parallel",)))(u, u, w_pool, pool_scale)


def _tri(kind):
    r = lax.broadcasted_iota(jnp.int32, (ATT_BLOCK, ATT_BLOCK), 0)
    c = lax.broadcasted_iota(jnp.int32, (ATT_BLOCK, ATT_BLOCK), 1)
    return jnp.where(r >= c if kind == "suffix" else r <= c, 1.0, 0.0).astype(BF16)


def _causal_mask():
    r = lax.broadcasted_iota(jnp.int32, (ATT_BLOCK, ATT_BLOCK), 0)
    c = lax.broadcasted_iota(jnp.int32, (ATT_BLOCK, ATT_BLOCK), 1)
    return c < r


def _softplus(z, with_sigmoid=False):
    ope = 1.0 + jnp.exp(jnp.minimum(z, 80.0))
    sp = jnp.maximum(z, jnp.log(ope))
    if with_sigmoid:
        return sp, 1.0 - 1.0 / ope
    return sp


def _attn_fwd(q, k, v):
    s = q.shape[0]
    tb = ATT_BLOCK
    nq = s // tb

    def body(q_ref, k_ref, v_ref, o_ref, t_ref):
        i = pl.program_id(1)
        upper = _tri("suffix")
        causal = _causal_mask()
        lane = lax.broadcasted_iota(jnp.int32, (1, HEAD_PAIR), 1)
        first = lane < 64
        q2 = q_ref[...]
        zero = jnp.zeros_like(q2)
        qs = (jnp.where(first, q2, zero), jnp.where(first, zero, q2))

        def step(j, carry, masked):
            cs, o = carry[:2], carry[2]
            start = pl.multiple_of(j * tb, tb)
            kj = k_ref[pl.ds(start, tb), :]
            vj = v_ref[pl.ds(start, tb), :]
            vcat = jnp.concatenate([jnp.where(first, vj, zero), jnp.where(first, zero, vj)], axis=0)
            probs, new_cs = [], []
            for e in range(2):
                z = _dot_nt(qs[e], kj)
                sp = _softplus(z)
                if masked:
                    sp = jnp.where(causal, sp, 0.0)
                incl = _dot(sp.astype(BF16), upper)
                a = jnp.exp(z - incl - cs[e])
                if masked:
                    a = jnp.where(causal, a, 0.0)
                probs.append(a.astype(BF16))
                new_cs.append(cs[e] + incl[:, 0:1])
            o = o + _dot(jnp.concatenate(probs, axis=1), vcat)
            return new_cs[0], new_cs[1], o

        def two_steps(n, carry):
            return step(i - 2 - 2 * n, step(i - 1 - 2 * n, carry, False), False)

        col = jnp.zeros((tb, 1), F32)
        carry = step(i, (col, col, jnp.zeros((tb, HEAD_PAIR), F32)), True)
        carry = lax.fori_loop(0, i // 2, two_steps, carry)
        carry = lax.cond(i % 2 == 1, lambda cr: step(0, cr, False), lambda cr: cr, carry)
        t_ref[:, 0:1] = carry[0]
        t_ref[:, 1:2] = carry[1]
        o_ref[...] = carry[2]

    return _call(
        body, name="attn_fwd",
        out_shape=[jax.ShapeDtypeStruct((s, D_ATTN), F32),
                   jax.ShapeDtypeStruct((4, s, 2), F32)],
        grid=(4, nq),
        in_specs=[pl.BlockSpec((tb, HEAD_PAIR), lambda h, i: (i, h)),
                  pl.BlockSpec((s, HEAD_PAIR), lambda h, i: (0, h)),
                  pl.BlockSpec((s, HEAD_PAIR), lambda h, i: (0, h))],
        out_specs=[pl.BlockSpec((tb, HEAD_PAIR), lambda h, i: (i, h)),
                   pl.BlockSpec((None, tb, 2), lambda h, i: (h, i, 0))],
        compiler_params=_params(("parallel", "arbitrary")))(q, k, v)


def _mix_out(attn, mpool, x, attn_scale, w_out, g2, g3):
    s = x.shape[0]
    ts = 512

    def body(a_ref, p_ref, x_ref, as_ref, w_ref, g2_ref, g3_ref, ma_ref, mix_ref, x1_ref, h2_ref):
        ao = a_ref[...]
        ma = (ao * _rstd(ao) * as_ref[...]).astype(BF16)
        ma_ref[...] = ma
        mix = _dot(p_ref[...], w_ref[0:D_POOL, :]) + _dot(ma, w_ref[D_POOL:, :])
        mix_ref[...] = mix
        x1 = x_ref[...] + mix * _rstd(mix) * g2_ref[...]
        x1_ref[...] = x1
        h2_ref[...] = (x1 * _rstd(x1) * g3_ref[...]).astype(BF16)

    row = lambda w: pl.BlockSpec((ts, w), lambda i: (i, 0))
    return _call(
        body, name="mix_out",
        out_shape=[jax.ShapeDtypeStruct((s, D_ATTN), BF16), jax.ShapeDtypeStruct((s, D_MODEL), F32),
                   jax.ShapeDtypeStruct((s, D_MODEL), F32), jax.ShapeDtypeStruct((s, D_MODEL), BF16)],
        grid=(s // ts,),
        in_specs=[row(D_ATTN), row(D_POOL), row(D_MODEL), _full((1, D_ATTN)),
                  _full((D_MODEL, D_MODEL)), _full((1, D_MODEL)), _full((1, D_MODEL))],
        out_specs=[row(D_ATTN), row(D_MODEL), row(D_MODEL), row(D_MODEL)],
        compiler_params=_params(("parallel",)))(attn, mpool, x, attn_scale, w_out, g2, g3)


def _conv_rows(ext_ref, cw, cb, ts):
    y = cb + cw[0:1, :] * ext_ref[CONV_HALO - 2:CONV_HALO - 2 + ts, :]
    y = y + cw[1:2, :] * ext_ref[CONV_HALO - 1:CONV_HALO - 1 + ts, :]
    return y + cw[2:3, :] * ext_ref[CONV_HALO:CONV_HALO + ts, :]


def _sigmoid(v):
    return 1.0 / (1.0 + jnp.exp(-v))


def _ffn_up(h2, w_up, conv_w, conv_b):
    s = h2.shape[0]
    ts = 256
    tn = FF_TILE

    def body(h_ref, wg_ref, wv_ref, cwg_ref, cwv_ref, cbg_ref, cbv_ref,
             ug_ref, uv_ref, f_ref, extg, extv):
        i = pl.program_id(1)

        @pl.when(i == 0)
        def _():
            extg[0:CONV_HALO, :] = jnp.zeros((CONV_HALO, tn), F32)
            extv[0:CONV_HALO, :] = jnp.zeros((CONV_HALO, tn), F32)

        h = h_ref[...]
        ug = _dot(h, wg_ref[...])
        uv = _dot(h, wv_ref[...])
        ug_ref[...] = ug
        uv_ref[...] = uv
        extg[CONV_HALO:, :] = ug
        extv[CONV_HALO:, :] = uv
        gate = _conv_rows(extg, cwg_ref[...], cbg_ref[...], ts)
        val = _conv_rows(extv, cwv_ref[...], cbv_ref[...], ts)
        f_ref[...] = (gate * _sigmoid(gate) * val).astype(BF16)
        extg[0:CONV_HALO, :] = extg[ts:ts + CONV_HALO, :]
        extv[0:CONV_HALO, :] = extv[ts:ts + CONV_HALO, :]

    out_blk = pl.BlockSpec((None, ts, tn), lambda n, i: (n, i, 0))
    act = jax.ShapeDtypeStruct((2, s, tn), F32)
    return _call(
        body, name="ffn_up",
        out_shape=[act, act, jax.ShapeDtypeStruct((2, s, tn), BF16)],
        grid=(2, s // ts),
        in_specs=[pl.BlockSpec((ts, D_MODEL), lambda n, i: (i, 0)),
                  pl.BlockSpec((None, D_MODEL, tn), lambda n, i: (n, 0, 0)),
                  pl.BlockSpec((None, D_MODEL, tn), lambda n, i: (n + 2, 0, 0)),
                  pl.BlockSpec((None, 3, tn), lambda n, i: (n, 0, 0)),
                  pl.BlockSpec((None, 3, tn), lambda n, i: (n + 2, 0, 0)),
                  pl.BlockSpec((None, 1, tn), lambda n, i: (n, 0, 0)),
                  pl.BlockSpec((None, 1, tn), lambda n, i: (n + 2, 0, 0))],
        out_specs=[out_blk, out_blk, out_blk],
        scratch_shapes=[pltpu.VMEM((ts + CONV_HALO, tn), F32), pltpu.VMEM((ts + CONV_HALO, tn), F32)],
        compiler_params=_params(("arbitrary", "arbitrary")))(
            h2, w_up, w_up, conv_w, conv_w, conv_b, conv_b)


def _ffn_down(f_in, w_down, x1, target, g4):
    s = x1.shape[0]
    ts = 512

    def body(f_ref, w_ref, x1_ref, t_ref, g_ref, df_ref, dy_ref, loss_ref, dg_ref):
        @pl.when(pl.program_id(0) == 0)
        def _():
            loss_ref[...] = jnp.zeros_like(loss_ref)
            dg_ref[...] = jnp.zeros_like(dg_ref)

        f = _dot(f_ref[0], w_ref[0:FF_TILE, :]) + _dot(f_ref[1], w_ref[FF_TILE:, :])
        rf = _rstd(f)
        fn = f * rf
        g = g_ref[...]
        err = (x1_ref[...] + fn * g) - t_ref[...]
        loss_ref[...] += 0.5 * jnp.sum(jnp.mean(err * err, axis=-1))
        dy = err * (1.0 / D_MODEL)
        dy_ref[...] = dy
        dg_ref[...] += jnp.sum(dy * fn, axis=0, keepdims=True)
        dfn = dy * g
        df_ref[...] = (rf * (dfn - fn * jnp.mean(dfn * fn, axis=-1, keepdims=True))).astype(BF16)

    row = pl.BlockSpec((ts, D_MODEL), lambda i: (i, 0))
    return _call(
        body, name="ffn_down",
        out_shape=[jax.ShapeDtypeStruct((s, D_MODEL), BF16), jax.ShapeDtypeStruct((s, D_MODEL), F32),
                   jax.ShapeDtypeStruct((8, 128), F32), jax.ShapeDtypeStruct((1, D_MODEL), F32)],
        grid=(s // ts,),
        in_specs=[pl.BlockSpec((2, ts, FF_TILE), lambda i: (0, i, 0)), _full((D_FF, D_MODEL)),
                  row, row, _full((1, D_MODEL))],
        out_specs=[row, row, _full((8, 128)), _full((1, D_MODEL))],
        compiler_params=_params(("arbitrary",)))(f_in, w_down, x1, target, g4)


def _tn_matmul(a, b, name, ts=512):
    na, s, ka = a.shape
    nb, _, nbc = b.shape

    def body(a_ref, b_ref, o_ref):
        @pl.when(pl.program_id(2) == 0)
        def _():
            o_ref[...] = jnp.zeros_like(o_ref)

        o_ref[...] += _dot_tn(a_ref[...].astype(BF16), b_ref[...].astype(BF16))

    return _call(
        body, name=name, out_shape=jax.ShapeDtypeStruct((na, nb, ka, nbc), F32),
        grid=(na, nb, s // ts),
        in_specs=[pl.BlockSpec((None, ts, ka), lambda i, j, r: (i, r, 0)),
                  pl.BlockSpec((None, ts, nbc), lambda i, j, r: (j, r, 0))],
        out_specs=pl.BlockSpec((None, None, ka, nbc), lambda i, j, r: (i, j, 0, 0)),
        compiler_params=_params(("parallel", "parallel", "arbitrary")))(a, b)


def _ffn_bwd_act(df, w_down, upre_g, upre_v, conv_w, conv_b):
    s = df.shape[0]
    ts = 256
    tn = FF_TILE
    nr = s // ts
    per = ts // CONV_HALO

    def body(df_ref, wd_ref, ug_ref, uv_ref, hg_ref, hv_ref, cwg_ref, cwv_ref, cbg_ref, cbv_ref,
             dug_ref, duv_ref, dcwg_ref, dcwv_ref, dcbg_ref, dcbv_ref, extg, extv, dxg, dxv):
        i = pl.program_id(1)
        first_rows = i == nr - 1

        @pl.when(i == 0)
        def _():
            dxg[ts:, :] = jnp.zeros((CONV_HALO, tn), F32)
            dxv[ts:, :] = jnp.zeros((CONV_HALO, tn), F32)
            for r in (dcwg_ref, dcwv_ref, dcbg_ref, dcbv_ref):
                r[...] = jnp.zeros_like(r)

        extg[0:CONV_HALO, :] = jnp.where(first_rows, 0.0, hg_ref[...])
        extv[0:CONV_HALO, :] = jnp.where(first_rows, 0.0, hv_ref[...])
        extg[CONV_HALO:, :] = ug_ref[...]
        extv[CONV_HALO:, :] = uv_ref[...]
        cwg, cwv = cwg_ref[...], cwv_ref[...]
        gate = _conv_rows(extg, cwg, cbg_ref[...], ts)
        val = _conv_rows(extv, cwv, cbv_ref[...], ts)
        sg = _sigmoid(gate)
        dfin = _dot_nt(df_ref[...], wd_ref[...])
        dval = dfin * (gate * sg)
        dgate = dfin * val * (sg * (1.0 + gate * (1.0 - sg)))

        def conv_bwd(dact, ext, dx, cw, dcw_ref, dcb_ref, du_ref):
            dx[0:ts, :] = dact
            dcb_ref[...] += jnp.sum(dact, axis=0, keepdims=True)
            for kk in range(3):
                lo = CONV_HALO - 2 + kk
                dcw_ref[kk:kk + 1, :] += jnp.sum(dact * ext[lo:lo + ts, :], axis=0, keepdims=True)
            du = cw[2:3, :] * dact + cw[1:2, :] * dx[1:1 + ts, :] + cw[0:1, :] * dx[2:2 + ts, :]
            du_ref[...] = du.astype(BF16)
            dx[ts:, :] = dx[0:CONV_HALO, :]

        conv_bwd(dgate, extg, dxg, cwg, dcwg_ref, dcbg_ref, dug_ref)
        conv_bwd(dval, extv, dxv, cwv, dcwv_ref, dcbv_ref, duv_ref)

    rows = lambda n, i: (n, nr - 1 - i, 0)
    halo = lambda n, i: (n, jnp.maximum((nr - 1 - i) * per - 1, 0), 0)
    act_blk = pl.BlockSpec((None, ts, tn), rows)
    halo_blk = pl.BlockSpec((None, CONV_HALO, tn), halo)
    cw_blk = lambda off: pl.BlockSpec((None, 3, tn), lambda n, i: (n + off, 0, 0))
    cb_blk = lambda off: pl.BlockSpec((None, 1, tn), lambda n, i: (n + off, 0, 0))
    acc_w = pl.BlockSpec((None, 3, tn), lambda n, i: (n, 0, 0))
    acc_b = pl.BlockSpec((None, 1, tn), lambda n, i: (n, 0, 0))
    dact = jax.ShapeDtypeStruct((2, s, tn), BF16)
    return _call(
        body, name="ffn_bwd_act",
        out_shape=[dact, dact, jax.ShapeDtypeStruct((2, 3, tn), F32), jax.ShapeDtypeStruct((2, 3, tn), F32),
                   jax.ShapeDtypeStruct((2, 1, tn), F32), jax.ShapeDtypeStruct((2, 1, tn), F32)],
        grid=(2, nr),
        in_specs=[pl.BlockSpec((ts, D_MODEL), lambda n, i: (nr - 1 - i, 0)),
                  pl.BlockSpec((tn, D_MODEL), lambda n, i: (n, 0)),
                  act_blk, act_blk, halo_blk, halo_blk,
                  cw_blk(0), cw_blk(2), cb_blk(0), cb_blk(2)],
        out_specs=[act_blk, act_blk, acc_w, acc_w, acc_b, acc_b],
        scratch_shapes=[pltpu.VMEM((ts + CONV_HALO, tn), F32)] * 4,
        compiler_params=_params(("arbitrary", "arbitrary")))(
            df, w_down, upre_g, upre_v, upre_g, upre_v, conv_w, conv_w, conv_b, conv_b)


def _ffn_bwd_in(dug, duv, w_up, x1, dy, mix, g3, g2):
    s = x1.shape[0]
    ts = 256

    def body(dg_ref, dv_ref, w_ref, x1_ref, dy_ref, mix_ref, g3_ref, g2_ref,
             dx1_ref, dmix_ref, dg3_ref, dg2_ref):
        @pl.when(pl.program_id(0) == 0)
        def _():
            dg3_ref[...] = jnp.zeros_like(dg3_ref)
            dg2_ref[...] = jnp.zeros_like(dg2_ref)

        dh = _dot_nt(dg_ref[0], w_ref[0]) + _dot_nt(dg_ref[1], w_ref[1])
        dh = dh + _dot_nt(dv_ref[0], w_ref[2]) + _dot_nt(dv_ref[1], w_ref[3])
        x1 = x1_ref[...]
        r3 = _rstd(x1)
        xn = x1 * r3
        dg3_ref[...] += jnp.sum(dh * xn, axis=0, keepdims=True)
        dxn = dh * g3_ref[...]
        dx1 = dy_ref[...] + r3 * (dxn - xn * jnp.mean(dxn * xn, axis=-1, keepdims=True))
        dx1_ref[...] = dx1
        mix = mix_ref[...]
        rm = _rstd(mix)
        mn = mix * rm
        dg2_ref[...] += jnp.sum(dx1 * mn, axis=0, keepdims=True)
        dmn = dx1 * g2_ref[...]
        dmix_ref[...] = (rm * (dmn - mn * jnp.mean(dmn * mn, axis=-1, keepdims=True))).astype(BF16)

    row = pl.BlockSpec((ts, D_MODEL), lambda i: (i, 0))
    act = pl.BlockSpec((2, ts, FF_TILE), lambda i: (0, i, 0))
    vec = _full((1, D_MODEL))
    return _call(
        body, name="ffn_bwd_in",
        out_shape=[jax.ShapeDtypeStruct((s, D_MODEL), F32), jax.ShapeDtypeStruct((s, D_MODEL), BF16),
                   jax.ShapeDtypeStruct((1, D_MODEL), F32), jax.ShapeDtypeStruct((1, D_MODEL), F32)],
        grid=(s // ts,),
        in_specs=[act, act, _full(w_up.shape), row, row, row, vec, vec],
        out_specs=[row, row, vec, vec],
        compiler_params=_params(("arbitrary",), vmem_mb=56))(dug, duv, w_up, x1, dy, mix, g3, g2)


def _mix_bwd(dmix, w_out, attn, attn_scale):
    s = dmix.shape[0]
    ts = 512

    def body(dm_ref, w_ref, a_ref, as_ref, dp_ref, do_ref, das_ref):
        @pl.when(pl.program_id(0) == 0)
        def _():
            das_ref[...] = jnp.zeros_like(das_ref)

        dm = dm_ref[...]
        dp_ref[...] = _dot_nt(dm, w_ref[0:D_POOL, :])
        da = _dot_nt(dm, w_ref[D_POOL:, :])
        ao = a_ref[...]
        ra = _rstd(ao)
        an = ao * ra
        das_ref[...] += jnp.sum(da * an, axis=0, keepdims=True)
        dan = da * as_ref[...]
        do_ref[...] = (ra * (dan - an * jnp.mean(dan * an, axis=-1, keepdims=True))).astype(BF16)

    row = lambda w: pl.BlockSpec((ts, w), lambda i: (i, 0))
    return _call(
        body, name="mix_bwd",
        out_shape=[jax.ShapeDtypeStruct((s, D_POOL), F32), jax.ShapeDtypeStruct((s, D_ATTN), BF16),
                   jax.ShapeDtypeStruct((1, D_ATTN), F32)],
        grid=(s // ts,),
        in_specs=[row(D_MODEL), _full((D_MODEL, D_MODEL)), row(D_ATTN), _full((1, D_ATTN))],
        out_specs=[row(D_POOL), row(D_ATTN), _full((1, D_ATTN))],
        compiler_params=_params(("arbitrary",)))(dmix, w_out, attn, attn_scale)


def _attn_bwd(q, k, v, do, totals):
    s = q.shape[0]
    tb = ATT_BLOCK
    nq = s // tb

    def body(q_ref, do_ref, t_ref, k_hbm, v_hbm, dq_ref, dk_hbm, dv_hbm,
             k_scr, v_scr, dkt_acc, dvt_acc, stage):
        hp = pl.program_id(0)
        i = pl.program_id(1)
        lanes = pl.ds(pl.multiple_of(hp * HEAD_PAIR, HEAD_PAIR), HEAD_PAIR)

        @pl.when(i == 0)
        def _():
            pltpu.sync_copy(k_hbm.at[:, lanes], k_scr)
            pltpu.sync_copy(v_hbm.at[:, lanes], v_scr)
            dkt_acc[...] = jnp.zeros_like(dkt_acc)
            dvt_acc[...] = jnp.zeros_like(dvt_acc)

        upper = _tri("suffix")
        lower = _tri("prefix")
        causal = _causal_mask()
        lane = lax.broadcasted_iota(jnp.int32, (1, HEAD_PAIR), 1)
        first = lane < 64
        q2 = q_ref[...]
        do2 = do_ref[...]
        zero = jnp.zeros_like(q2)
        qs = (jnp.where(first, q2, zero), jnp.where(first, zero, q2))
        dos = (jnp.where(first, do2, zero), jnp.where(first, zero, do2))
        qcat_t = jnp.concatenate(qs, axis=0).astype(F32).T.astype(BF16)
        docat_t = jnp.concatenate(dos, axis=0).astype(F32).T.astype(BF16)
        tots = (t_ref[:, 0:1], t_ref[:, 1:2])

        def step(j, carry, masked):
            cs, cps, dq = carry[0:2], carry[2:4], carry[4]
            start = pl.multiple_of(j * tb, tb)
            kj = k_scr[pl.ds(start, tb), :]
            vj = v_scr[pl.ds(start, tb), :]
            kcat = jnp.concatenate([jnp.where(first, kj, zero), jnp.where(first, zero, kj)], axis=0)
            dzs, probs, new_cs, new_cps = [], [], [], []
            for e in range(2):
                z = _dot_nt(qs[e], kj)
                sp, sig = _softplus(z, True)
                if masked:
                    sp = jnp.where(causal, sp, 0.0)
                incl = _dot(sp.astype(BF16), upper)
                c = cs[e] + incl[:, 0:1]
                a = jnp.exp(z - incl - (tots[e] - c))
                if masked:
                    a = jnp.where(causal, a, 0.0)
                dw = a * _dot_nt(dos[e], vj)
                pre = _dot(dw.astype(BF16), lower) + cps[e]
                dz = dw - sig * pre
                if masked:
                    dz = jnp.where(causal, dz, 0.0)
                new_cs.append(c)
                new_cps.append(pre[:, tb - 1:tb])
                dzs.append(dz.astype(BF16))
                probs.append(a.astype(BF16))
            dq = dq + _dot(jnp.concatenate(dzs, axis=1), kcat)
            dkt_acc[:, pl.ds(start, tb)] += _dot(qcat_t, jnp.concatenate(dzs, axis=0))
            dvt_acc[:, pl.ds(start, tb)] += _dot(docat_t, jnp.concatenate(probs, axis=0))
            return new_cs[0], new_cs[1], new_cps[0], new_cps[1], dq

        col = jnp.zeros((tb, 1), F32)
        carry = (col, col, col, col, jnp.zeros((tb, HEAD_PAIR), F32))
        carry = lax.fori_loop(
            0, i // 2, lambda n, cr: step(2 * n + 1, step(2 * n, cr, False), False), carry)
        carry = lax.cond(i % 2 == 1, lambda cr: step(i - 1, cr, False), lambda cr: cr, carry)
        carry = step(i, carry, True)
        dq_ref[...] = (carry[4] * Q_SCALE).astype(BF16)

        @pl.when(i == nq - 1)
        def _():
            for acc, dst in ((dkt_acc, dk_hbm), (dvt_acc, dv_hbm)):
                def flip(n, _, acc=acc):
                    at = pl.ds(pl.multiple_of(n * tb, tb), tb)
                    stage[at, :] = acc[:, at].T
                    return 0
                lax.fori_loop(0, nq, flip, 0)
                pltpu.sync_copy(stage, dst.at[:, lanes])

    blk = pl.BlockSpec((tb, HEAD_PAIR), lambda h, i: (i, h))
    grad = jax.ShapeDtypeStruct((s, D_ATTN), F32)
    return _call(
        body, name="attn_bwd",
        out_shape=[jax.ShapeDtypeStruct((s, D_ATTN), BF16), grad, grad],
        grid=(4, nq),
        in_specs=[blk, blk, pl.BlockSpec((None, tb, 2), lambda h, i: (h, i, 0)), ANY, ANY],
        out_specs=[blk, ANY, ANY],
        scratch_shapes=[pltpu.VMEM((s, HEAD_PAIR), BF16), pltpu.VMEM((s, HEAD_PAIR), BF16),
                        pltpu.VMEM((HEAD_PAIR, s), F32), pltpu.VMEM((HEAD_PAIR, s), F32),
                        pltpu.VMEM((s, HEAD_PAIR), F32)],
        compiler_params=_params(("arbitrary", "arbitrary"), vmem_mb=56))(q, do, totals, k, v)


def _pool_bwd(u, dmp, w_pool, pool_scale):
    s = u.shape[0]
    ts = 512
    nr = s // ts
    per = ts // POOL_HALO

    def body(u_ref, halo_ref, dm_ref, wp_ref, ps_ref, du_ref, dwp_ref, dps_ref, ext_ref, y_ref, dext_ref):
        i = pl.program_id(0)
        rb = nr - 1 - i

        @pl.when(i == 0)
        def _():
            dext_ref[ts:, :] = jnp.zeros((POOL_HALO, D_POOL), F32)
            dwp_ref[...] = jnp.zeros_like(dwp_ref)
            dps_ref[...] = jnp.zeros_like(dps_ref)

        ext_ref[0:POOL_HALO, :] = jnp.where(rb > 0, halo_ref[...], 0.0)
        ext_ref[POOL_HALO:, :] = u_ref[...]
        ps, cnts = [], []
        for g, window in enumerate(POOL_WINDOWS):
            p, cnt = _pool_means(ext_ref, g, window, ts, rb * ts)
            ps.append(p.astype(BF16))
            cnts.append(cnt)
            y_ref[:, g * POOL_GROUP:(g + 1) * POOL_GROUP] = _dot(ps[g], wp_ref[g].astype(BF16))
        y = y_ref[...]
        r = _rstd(y)
        yn = y * r
        dm = dm_ref[...]
        dps_ref[...] += jnp.sum(dm * yn, axis=0, keepdims=True)
        dn = dm * ps_ref[...]
        dy = r * (dn - yn * jnp.mean(dn * yn, axis=-1, keepdims=True))
        for g, window in enumerate(POOL_WINDOWS):
            cols = slice(g * POOL_GROUP, (g + 1) * POOL_GROUP)
            dyg = dy[:, cols].astype(BF16)
            dwp_ref[g] += _dot_tn(ps[g], dyg)
            dp = _dot_nt(dyg, wp_ref[g].astype(BF16))
            dext_ref[0:ts, cols] = dp / cnts[g]
            acc = dext_ref[0:ts, cols]
            for d in range(1, window):
                acc = acc + dext_ref[d:d + ts, cols]
            du_ref[:, cols] = (acc - dp).astype(BF16)
        dext_ref[ts:, :] = dext_ref[0:POOL_HALO, :]

    rows = pl.BlockSpec((ts, D_POOL), lambda i: (nr - 1 - i, 0))
    return _call(
        body, name="pool_bwd",
        out_shape=[jax.ShapeDtypeStruct((s, D_POOL), BF16), jax.ShapeDtypeStruct(w_pool.shape, F32),
                   jax.ShapeDtypeStruct((1, D_POOL), F32)],
        grid=(nr,),
        in_specs=[rows,
                  pl.BlockSpec((POOL_HALO, D_POOL), lambda i: (jnp.maximum((nr - 1 - i) * per - 1, 0), 0)),
                  rows, _full(w_pool.shape), _full((1, D_POOL))],
        out_specs=[rows, _full(w_pool.shape), _full((1, D_POOL))],
        scratch_shapes=[pltpu.VMEM((ts + POOL_HALO, D_POOL), F32), pltpu.VMEM((ts, D_POOL), F32),
                        pltpu.VMEM((ts + POOL_HALO, D_POOL), F32)],
        compiler_params=_params(("arbitrary",)))(u, u, dmp, w_pool, pool_scale)


def _in_proj_bwd(du, dq, dk, dv, w_in, x, dx1, g1):
    s = x.shape[0]
    ts = 512

    def body(du_ref, dq_ref, dk_ref, dv_ref, w_ref, x_ref, dx1_ref, g_ref, gx_ref, dg_ref):
        @pl.when(pl.program_id(0) == 0)
        def _():
            dg_ref[...] = jnp.zeros_like(dg_ref)

        dh = _dot_nt(du_ref[...], w_ref[0]) + _dot_nt(dq_ref[...], w_ref[1])
        dh = dh + _dot_nt(dk_ref[...].astype(BF16), w_ref[2]) + _dot_nt(dv_ref[...].astype(BF16), w_ref[3])
        xv = x_ref[...]
        r = _rstd(xv)
        xn = xv * r
        dg_ref[...] += jnp.sum(dh * xn, axis=0, keepdims=True)
        dxn = dh * g_ref[...]
        gx_ref[...] = dx1_ref[...] + r * (dxn - xn * jnp.mean(dxn * xn, axis=-1, keepdims=True))

    row = lambda w: pl.BlockSpec((ts, w), lambda i: (i, 0))
    return _call(
        body, name="in_proj_bwd",
        out_shape=[jax.ShapeDtypeStruct((s, D_MODEL), F32), jax.ShapeDtypeStruct((1, D_MODEL), F32)],
        grid=(s // ts,),
        in_specs=[row(D_POOL)] * 4 + [_full(w_in.shape), row(D_MODEL), row(D_MODEL), _full((1, D_MODEL))],
        out_specs=[row(D_MODEL), _full((1, D_MODEL))],
        compiler_params=_params(("arbitrary",)))(du, dq, dk, dv, w_in, x, dx1, g1)


_SMALL = ("norm_mix_pre", "w_pool", "pool_scale", "attn_scale", "norm_mix_post",
          "norm_ffn_pre", "conv_b", "norm_ffn_post")
_SMALL_SIZE = {"norm_mix_pre": 1024, "w_pool": 65536, "pool_scale": 512, "attn_scale": 512,
               "norm_mix_post": 1024, "norm_ffn_pre": 1024, "conv_b": 5632, "norm_ffn_post": 1024}
_SMALL_ROWS = 600
_CONVW_ROWS = 132
_PACK_ROWS = _SMALL_ROWS + _CONVW_ROWS + 4


def _pack_small(parts):
    flat = jnp.concatenate([parts[n].reshape(-1) for n in _SMALL])
    flat = jnp.pad(flat, (0, _SMALL_ROWS * 128 - flat.shape[0]))
    return flat.reshape(_SMALL_ROWS, 128)


def _unpack_small(packed, like):
    flat = packed.reshape(-1)
    out, off = {}, 0
    for n in _SMALL:
        out[n] = flat[off:off + _SMALL_SIZE[n]].reshape(like[n].shape)
        off += _SMALL_SIZE[n]
    return out


def kernel(x, norm_mix_pre, w_in, w_pool, pool_scale, attn_scale, w_out, norm_mix_post, norm_ffn_pre, w_up, conv_w, conv_b, w_down, norm_ffn_post, loss_target, m_norm_mix_pre, m_w_in, m_w_pool, m_pool_scale, m_attn_scale, m_w_out, m_norm_mix_post, m_norm_ffn_pre, m_w_up, m_conv_w, m_conv_b, m_w_down, m_norm_ffn_post, v_norm_mix_pre, v_w_in, v_w_pool, v_pool_scale, v_attn_scale, v_w_out, v_norm_mix_post, v_norm_ffn_pre, v_w_up, v_conv_w, v_conv_b, v_w_down, v_norm_ffn_post):
    weights = dict(norm_mix_pre=norm_mix_pre, w_in=w_in, w_pool=w_pool, pool_scale=pool_scale,
                   attn_scale=attn_scale, w_out=w_out, norm_mix_post=norm_mix_post,
                   norm_ffn_pre=norm_ffn_pre, w_up=w_up, conv_w=conv_w, conv_b=conv_b,
                   w_down=w_down, norm_ffn_post=norm_ffn_post)
    mom1 = dict(norm_mix_pre=m_norm_mix_pre, w_in=m_w_in, w_pool=m_w_pool, pool_scale=m_pool_scale,
                attn_scale=m_attn_scale, w_out=m_w_out, norm_mix_post=m_norm_mix_post,
                norm_ffn_pre=m_norm_ffn_pre, w_up=m_w_up, conv_w=m_conv_w, conv_b=m_conv_b,
                w_down=m_w_down, norm_ffn_post=m_norm_ffn_post)
    mom2 = dict(norm_mix_pre=v_norm_mix_pre, w_in=v_w_in, w_pool=v_w_pool, pool_scale=v_pool_scale,
                attn_scale=v_attn_scale, w_out=v_w_out, norm_mix_post=v_norm_mix_post,
                norm_ffn_pre=v_norm_ffn_pre, w_up=v_w_up, conv_w=v_conv_w, conv_b=v_conv_b,
                w_down=v_w_down, norm_ffn_post=v_norm_ffn_post)
    order = list(weights)

    xs = x[0]
    target = loss_target[0]
    wp = w_pool[0]
    shard = lax.axis_index("x") * 2 + lax.axis_index("y")

    shards = [_cast_bf16(w_in[0], "cast_w_in"), _cast_bf16(w_out[0], "cast_w_out"),
              _cast_bf16(w_up[0], "cast_w_up"), _cast_bf16(w_down[0], "cast_w_down"), conv_w[0]]
    win_g, wout_g, wup_g, wdown_g, convw_g = _gather_shards(shards)
    wout_f = wout_g.reshape(D_MODEL, D_MODEL)
    wdown_f = wdown_g.reshape(D_FF, D_MODEL)
    convb_g = conv_b[0].reshape(N_SHARD, 1, FF_TILE)

    u, q, k, v, h1 = _in_proj(xs, norm_mix_pre, win_g)
    mpool = _pool_fwd(u, wp, pool_scale)
    attn, totals = _attn_fwd(q, k, v)
    mattn, mix, x1, h2 = _mix_out(attn, mpool, xs, attn_scale, wout_f, norm_mix_post, norm_ffn_pre)
    upre_g, upre_v, f_in = _ffn_up(h2, wup_g, convw_g, convb_g)
    df, dy, loss_tile, d_post = _ffn_down(f_in, wdown_f, x1, target, norm_ffn_post)

    d_wdown = _tn_matmul(f_in, df[None], "dw_down")
    dug, duv, dcw_g, dcw_v, dcb_g, dcb_v = _ffn_bwd_act(df, wdown_f, upre_g, upre_v, convw_g, convb_g)
    d_wup = jnp.concatenate([_tn_matmul(h2[None], dug, "dw_up_gate")[0],
                             _tn_matmul(h2[None], duv, "dw_up_value")[0]], axis=0)
    dx1, dmix, d_ffn_pre, d_mix_post = _ffn_bwd_in(dug, duv, wup_g, x1, dy, mix, norm_ffn_pre, norm_mix_post)
    d_wout = jnp.concatenate([_tn_matmul(mpool[None], dmix[None], "dw_out_pool")[0, 0],
                              _tn_matmul(mattn[None], dmix[None], "dw_out_attn")[0, 0]], axis=0)
    dmp, do, d_attn_scale = _mix_bwd(dmix, wout_f, attn, attn_scale)
    dq, dk, dv = _attn_bwd(q, k, v, do, totals)
    du, d_wpool, d_pool_scale = _pool_bwd(u, dmp, wp, pool_scale)
    d_win = jnp.stack([_tn_matmul(h1[None], t[None], "dw_in_%d" % n)[0, 0]
                       for n, t in enumerate((du, dq, dk, dv))])
    grad_x, d_mix_pre = _in_proj_bwd(du, dq, dk, dv, win_g, xs, dx1, norm_mix_pre)

    d_convw = jnp.concatenate([dcw_g, dcw_v], axis=0)
    d_convb = jnp.concatenate([dcb_g, dcb_v], axis=0).reshape(1, 2 * D_FF)
    small_parts = dict(norm_mix_pre=d_mix_pre, w_pool=d_wpool, pool_scale=d_pool_scale,
                       attn_scale=d_attn_scale, norm_mix_post=d_mix_post, norm_ffn_pre=d_ffn_pre,
                       conv_b=d_convb, norm_ffn_post=d_post)
    packed = jnp.concatenate([_pack_small(small_parts), d_convw.reshape(_CONVW_ROWS, 128),
                              jnp.zeros((4, 128), F32)], axis=0)
    big = [d_win, d_wout.reshape(N_SHARD, D_MODEL // N_SHARD, D_MODEL), d_wup,
           d_wdown.reshape(N_SHARD, D_FF // N_SHARD, D_MODEL)]
    recv, gathered = _scatter_grads(big, packed)
    quarter = [_sum_slots(r, (3, 0, 1, 2), "sum_chips_%d" % n) for n, r in enumerate(recv)]
    sibling = _swap_with_sibling(quarter)
    small_sum = _sum_slots(gathered, tuple(range(8)), "sum_small")

    results = {}
    for n, name in enumerate(("w_in", "w_out", "w_up", "w_down")):
        res = _adamw([quarter[n], sibling[n]], weights[name][0], mom1[name][0], mom2[name][0],
                     "adamw_" + name)
        results[name] = [t[None] for t in res]
    g_convw = lax.dynamic_slice_in_dim(
        small_sum[_SMALL_ROWS:_SMALL_ROWS + _CONVW_ROWS].reshape(N_SHARD, 3, FF_TILE), shard, 1, axis=0)[0]
    convw_pad = lambda t: jnp.pad(t, ((0, 5), (0, 0)))
    res = _adamw([convw_pad(g_convw)], convw_pad(conv_w[0]), convw_pad(m_conv_w[0]),
                 convw_pad(v_conv_w[0]), "adamw_conv_w")
    results["conv_w"] = [t[:3][None] for t in res]
    pack_w = _pack_small(weights)
    pack_m = _pack_small(mom1)
    pack_v = _pack_small(mom2)
    res = _adamw([small_sum[:_SMALL_ROWS]], pack_w, pack_m, pack_v, "adamw_small")
    unpacked = [_unpack_small(t, weights) for t in res]
    for name in _SMALL:
        results[name] = [t[name] for t in unpacked]

    loss = lax.psum(loss_tile[0, 0], ("x", "y", "c"))
    outs = [loss, grad_x[None]]
    for slot in range(4):
        outs.extend(results[name][slot] for name in order)
    return tuple(outs)
```

```python
import functools

import jax
import jax.numpy as jnp
from jax import lax
from jax.experimental import pallas as pl
from jax.experimental.pallas import tpu as pltpu

F32 = jnp.float32
BF16 = jnp.bfloat16

D_MODEL = 1024
D_POOL = 512
D_ATTN = 512
POOL_WINDOWS = (2, 4, 8, 16)
POOL_GROUP = 128
POOL_HALO = 16
CONV_HALO = 8
D_FF = 2816
FF_TILE = 1408
N_SHARD = 4
EPS = 1e-6
Q_SCALE = 0.125
ATT_BLOCK = 256
HEAD_PAIR = 128
MIB = 1 << 20
NEG_BIG = -1e30

ADAM_LR = 0.001
ADAM_B1 = 0.9
ADAM_B2 = 0.999
ADAM_EPS = 1e-08
ADAM_WD = 0.01
ADAM_STEP = 10

NT_DIMS = (((1,), (1,)), ((), ()))
TN_DIMS = (((0,), (0,)), ((), ()))
MESH = pl.DeviceIdType.MESH
ANY = pl.BlockSpec(memory_space=pl.ANY)


def _call(body, **kw):
    return pl.pallas_call(body, **kw)


def _params(sem=None, vmem_mb=48):
    return pltpu.CompilerParams(dimension_semantics=sem, vmem_limit_bytes=vmem_mb * MIB)


def _rstd(v):
    return lax.rsqrt(jnp.mean(v * v, axis=-1, keepdims=True) + EPS)


def _dot(a, b):
    return jnp.dot(a, b, preferred_element_type=F32)


def _dot_nt(a, b):
    return lax.dot_general(a, b, NT_DIMS, preferred_element_type=F32)


def _dot_tn(a, b):
    return lax.dot_general(a, b, TN_DIMS, preferred_element_type=F32)


def _row_tile(rows, cap):
    t = min(rows, cap)
    t -= t % 8
    while rows % t:
        t -= 8
    return t


def _full(shape):
    nd = len(shape)
    return pl.BlockSpec(shape, lambda *_: (0,) * nd)


def _chip_peers():
    x, y, c = lax.axis_index("x"), lax.axis_index("y"), lax.axis_index("c")
    return x, y, c, [(1 - x, y), (x, 1 - y), (1 - x, 1 - y)]


def _cast_bf16(a, name):
    def body(a_ref, o_ref):
        o_ref[...] = a_ref[...].astype(BF16)

    return _call(body, name=name, out_shape=jax.ShapeDtypeStruct(a.shape, BF16),
                 grid=(1,), in_specs=[_full(a.shape)], out_specs=_full(a.shape),
                 compiler_params=_params(("arbitrary",)))(a)


def _gather_shards(shards):
    n = len(shards)

    def body(*refs):
        ins, outs = refs[:n], refs[n:2 * n]
        send, recv, loc = refs[2 * n:]
        x, y, c, chips = _chip_peers()
        b = 2 * x + y
        local = [pltpu.make_async_copy(ins[t], outs[t].at[b], loc.at[t]) for t in range(n)]
        for cp in local:
            cp.start()
        remote = []
        for t in range(n):
            for k, (px, py) in enumerate(chips):
                remote.append(pltpu.make_async_remote_copy(
                    src_ref=ins[t], dst_ref=outs[t].at[b],
                    send_sem=send.at[3 * t + k], recv_sem=recv.at[3 * t + k],
                    device_id=(px, py, c), device_id_type=MESH))
        for cp in remote:
            cp.start()
        for cp in remote:
            cp.wait()
        for cp in local:
            cp.wait()

    return _call(
        body, name="gather_weights",
        out_shape=[jax.ShapeDtypeStruct((N_SHARD,) + s.shape, s.dtype) for s in shards],
        in_specs=[ANY] * n, out_specs=[ANY] * n,
        scratch_shapes=[pltpu.SemaphoreType.DMA((3 * n,)), pltpu.SemaphoreType.DMA((3 * n,)),
                        pltpu.SemaphoreType.DMA((n,))],
    )(*shards)


def _scatter_grads(grads, small):
    n = len(grads)

    def body(*refs):
        ins, small_in = refs[:n], refs[n]
        outs, small_out = refs[n + 1:2 * n + 1], refs[2 * n + 1]
        send, recv, loc, ssend, srecv = refs[2 * n + 2:]
        x, y, c, chips = _chip_peers()
        b = 2 * x + y
        me = 4 * x + 2 * y + c
        local = [pltpu.make_async_copy(ins[t].at[b], outs[t].at[3], loc.at[t]) for t in range(n)]
        local.append(pltpu.make_async_copy(small_in, small_out.at[me], loc.at[n]))
        for cp in local:
            cp.start()
        remote = []
        for t in range(n):
            for k, (px, py) in enumerate(chips):
                remote.append(pltpu.make_async_remote_copy(
                    src_ref=ins[t].at[2 * px + py], dst_ref=outs[t].at[k],
                    send_sem=send.at[3 * t + k], recv_sem=recv.at[3 * t + k],
                    device_id=(px, py, c), device_id_type=MESH))
        for r in range(1, 8):
            px = 1 - x if r & 4 else x
            py = 1 - y if r & 2 else y
            pc = 1 - c if r & 1 else c
            remote.append(pltpu.make_async_remote_copy(
                src_ref=small_in, dst_ref=small_out.at[me],
                send_sem=ssend.at[r - 1], recv_sem=srecv.at[r - 1],
                device_id=(px, py, pc), device_id_type=MESH))
        for cp in remote:
            cp.start()
        for cp in remote:
            cp.wait()
        for cp in local:
            cp.wait()

    out_shape = [jax.ShapeDtypeStruct(g.shape, g.dtype) for g in grads]
    out_shape.append(jax.ShapeDtypeStruct((8,) + small.shape, small.dtype))
    res = _call(
        body, name="scatter_grads", out_shape=out_shape,
        in_specs=[ANY] * (n + 1), out_specs=[ANY] * (n + 1),
        scratch_shapes=[pltpu.SemaphoreType.DMA((3 * n,)), pltpu.SemaphoreType.DMA((3 * n,)),
                        pltpu.SemaphoreType.DMA((n + 1,)),
                        pltpu.SemaphoreType.DMA((7,)), pltpu.SemaphoreType.DMA((7,))],
    )(*grads, small)
    return res[:n], res[n]


def _swap_with_sibling(parts):
    n = len(parts)

    def body(*refs):
        ins, outs = refs[:n], refs[n:2 * n]
        send, recv = refs[2 * n:]
        x, y, c = lax.axis_index("x"), lax.axis_index("y"), lax.axis_index("c")
        copies = [pltpu.make_async_remote_copy(
            src_ref=ins[t], dst_ref=outs[t], send_sem=send.at[t], recv_sem=recv.at[t],
            device_id=(x, y, 1 - c), device_id_type=MESH) for t in range(n)]
        for cp in copies:
            cp.start()
        for cp in copies:
            cp.wait()

    return _call(
        body, name="swap_sibling",
        out_shape=[jax.ShapeDtypeStruct(p.shape, p.dtype) for p in parts],
        in_specs=[ANY] * n, out_specs=[ANY] * n,
        scratch_shapes=[pltpu.SemaphoreType.DMA((n,)), pltpu.SemaphoreType.DMA((n,))],
    )(*parts)


def _sum_slots(buf, order, name):
    k, rows, cols = buf.shape
    tr = _row_tile(rows, 256)

    def body(b_ref, o_ref):
        acc = b_ref[order[0]]
        for s in order[1:]:
            acc = acc + b_ref[s]
        o_ref[...] = acc

    return _call(body, name=name, out_shape=jax.ShapeDtypeStruct((rows, cols), F32),
                 grid=(rows // tr,),
                 in_specs=[pl.BlockSpec((k, tr, cols), lambda i: (0, i, 0))],
                 out_specs=pl.BlockSpec((tr, cols), lambda i: (i, 0)),
                 compiler_params=_params(("parallel",)))(buf)


def _adamw(grad_parts, w, m, v, name):
    rows, cols = w.shape
    tr = _row_tile(rows, 256)
    npart = len(grad_parts)

    def body(*refs):
        gp = refs[:npart]
        w_ref, m_ref, v_ref, g_out, d_out, m_out, v_out = refs[npart:]
        g = gp[0][...]
        for p in gp[1:]:
            g = g + p[...]
        mm = ADAM_B1 * m_ref[...] + (1.0 - ADAM_B1) * g
        vv = ADAM_B2 * v_ref[...] + (1.0 - ADAM_B2) * jnp.square(g)
        m_hat = mm / (1.0 - ADAM_B1 ** ADAM_STEP)
        v_hat = vv / (1.0 - ADAM_B2 ** ADAM_STEP)
        g_out[...] = g
        d_out[...] = -ADAM_LR * (m_hat / (jnp.sqrt(v_hat) + ADAM_EPS) + ADAM_WD * w_ref[...])
        m_out[...] = mm
        v_out[...] = vv

    spec = pl.BlockSpec((tr, cols), lambda i: (i, 0))
    shp = jax.ShapeDtypeStruct((rows, cols), F32)
    return _call(body, name=name, out_shape=[shp] * 4, grid=(rows // tr,),
                 in_specs=[spec] * (npart + 3), out_specs=[spec] * 4,
                 compiler_params=_params(("parallel",)))(*grad_parts, w, m, v)


def _in_proj(x, g1, w_in):
    s = x.shape[0]
    ts = 512

    def body(x_ref, g_ref, w_ref, u_ref, q_ref, k_ref, v_ref, h_ref):
        xv = x_ref[...]
        h = (xv * _rstd(xv) * g_ref[...]).astype(BF16)
        h_ref[...] = h
        u_ref[...] = _dot(h, w_ref[0])
        q_ref[...] = (_dot(h, w_ref[1]) * Q_SCALE).astype(BF16)
        k_ref[...] = _dot(h, w_ref[2]).astype(BF16)
        v_ref[...] = _dot(h, w_ref[3]).astype(BF16)

    row = lambda w: pl.BlockSpec((ts, w), lambda i: (i, 0))
    half = jax.ShapeDtypeStruct((s, D_POOL), BF16)
    return _call(
        body, name="in_proj",
        out_shape=[jax.ShapeDtypeStruct((s, D_POOL), F32), half, half, half,
                   jax.ShapeDtypeStruct((s, D_MODEL), BF16)],
        grid=(s // ts,),
        in_specs=[row(D_MODEL), _full((1, D_MODEL)), _full(w_in.shape)],
        out_specs=[row(D_POOL)] * 4 + [row(D_MODEL)],
        compiler_params=_params(("parallel",)))(x, g1, w_in)


def _pool_means(ext_ref, g, window, ts, row0):
    cols = slice(g * POOL_GROUP, (g + 1) * POOL_GROUP)
    cur = ext_ref[POOL_HALO:POOL_HALO + ts, cols]
    acc = cur
    for d in range(1, window):
        acc = acc + ext_ref[POOL_HALO - d:POOL_HALO - d + ts, cols]
    t1 = row0 + 1 + lax.broadcasted_iota(jnp.int32, (ts, 1), 0)
    cnt = jnp.minimum(t1, window).astype(F32)
    return acc / cnt - cur, cnt


def _pool_fwd(u, w_pool, pool_scale):
    s = u.shape[0]
    ts = 512
    per = ts // POOL_HALO

    def body(u_ref, halo_ref, wp_ref, ps_ref, o_ref, ext_ref, y_ref):
        i = pl.program_id(0)
        ext_ref[0:POOL_HALO, :] = jnp.where(i > 0, halo_ref[...], 0.0)
        ext_ref[POOL_HALO:, :] = u_ref[...]
        for g, window in enumerate(POOL_WINDOWS):
            p, _ = _pool_means(ext_ref, g, window, ts, i * ts)
            y_ref[:, g * POOL_GROUP:(g + 1) * POOL_GROUP] = _dot(
                p.astype(BF16), wp_ref[g].astype(BF16))
        y = y_ref[...]
        o_ref[...] = (y * _rstd(y) * ps_ref[...]).astype(BF16)

    return _call(
        body, name="pool_fwd", out_shape=jax.ShapeDtypeStruct((s, D_POOL), BF16),
        grid=(s // ts,),
        in_specs=[pl.BlockSpec((ts, D_POOL), lambda i: (i, 0)),
                  pl.BlockSpec((POOL_HALO, D_POOL), lambda i: (jnp.maximum(i * per - 1, 0), 0)),
                  _full(w_pool.shape), _full((1, D_POOL))],
        out_specs=pl.BlockSpec((ts, D_POOL), lambda i: (i, 0)),
        scratch_shapes=[pltpu.VMEM((ts + POOL_HALO, D_POOL), F32), pltpu.VMEM((ts, D_POOL), F32)],
        compiler_params=_params(("parallel",)))(u, u, w_pool, pool_scale)


def _tri(kind):
    r = lax.broadcasted_iota(jnp.int32, (ATT_BLOCK, ATT_BLOCK), 0)
    c = lax.broadcasted_iota(jnp.int32, (ATT_BLOCK, ATT_BLOCK), 1)
    return jnp.where(r >= c if kind == "suffix" else r <= c, 1.0, 0.0).astype(BF16)


def _causal_mask():
    r = lax.broadcasted_iota(jnp.int32, (ATT_BLOCK, ATT_BLOCK), 0)
    c = lax.broadcasted_iota(jnp.int32, (ATT_BLOCK, ATT_BLOCK), 1)
    return c < r


def _softplus(z, with_sigmoid=False):
    ope = 1.0 + jnp.exp(jnp.minimum(z, 80.0))
    sp = jnp.maximum(z, jnp.log(ope))
    if with_sigmoid:
        return sp, 1.0 - 1.0 / ope
    return sp


def _attn_fwd(q, k, v):
    s = q.shape[0]
    tb = ATT_BLOCK
    nq = s // tb

    def body(q_ref, k_ref, v_ref, o_ref, t_ref, *bufs):
        i = pl.program_id(1)
        upper = _tri("suffix")
        causal = _causal_mask()
        lane = lax.broadcasted_iota(jnp.int32, (1, HEAD_PAIR), 1)
        first = lane < 64
        q2 = q_ref[...]
        zero = jnp.zeros_like(q2)
        qs = (jnp.where(first, q2, zero), jnp.where(first, zero, q2))

        def scores(j, cs, masked):
            kj = k_ref[pl.ds(pl.multiple_of(j * tb, tb), tb), :]
            new_cs, args = [], []
            for e in range(2):
                z = _dot_nt(qs[e], kj)
                sp = _softplus(z)
                if masked:
                    sp = jnp.where(causal, sp, 0.0)
                incl = _dot(sp.astype(BF16), upper)
                arg = z - incl - cs[e]
                if masked:
                    arg = jnp.where(causal, arg, NEG_BIG)
                args.append(arg)
                new_cs.append(cs[e] + incl[:, 0:1])
            return new_cs, args

        def weigh(j, args, o):
            vj = v_ref[pl.ds(pl.multiple_of(j * tb, tb), tb), :]
            vcat = jnp.concatenate([jnp.where(first, vj, zero), jnp.where(first, zero, vj)], axis=0)
            probs = [jnp.exp(arg).astype(BF16) for arg in args]
            return o + _dot(jnp.concatenate(probs, axis=1), vcat)

        z_buf, zc_buf, in_buf = [[[bufs[4 * kind + 2 * slot + e] for e in range(2)]
                                  for slot in range(2)] for kind in range(3)]
        pr_buf = [bufs[12], bufs[13]]

        @pl.when((pl.program_id(0) == 0) & (i == 0))
        def _():
            for b in bufs:
                b[...] = jnp.zeros_like(b)

        def block_rows(p):
            return pl.ds(pl.multiple_of(jnp.clip(i - 1 - p, 0, nq - 1) * tb, tb), tb)

        def trip(t, w, carry):
            r = 1 - w
            cs, o = list(carry[0:2]), carry[2]
            live3 = (t - 3 >= 0) & (t - 3 < i)
            vj = v_ref[block_rows(t - 3), :]
            vcat = jnp.concatenate([jnp.where(first, vj, zero), jnp.where(first, zero, vj)], axis=0)
            o = o + jnp.where(live3, _dot(pr_buf[r][...], vcat), 0.0)
            kj = k_ref[block_rows(t), :]
            for e in range(2):
                z_buf[w][e][...] = _dot_nt(qs[e], kj)
            for e in range(2):
                z = z_buf[r][e][...]
                zc_buf[w][e][...] = z
                in_buf[w][e][...] = _dot(_softplus(z).astype(BF16), upper)
            live2 = (t - 2 >= 0) & (t - 2 < i)
            for e in range(2):
                incl = in_buf[r][e][...]
                arg = zc_buf[r][e][...] - incl - jnp.where(live2, cs[e], -NEG_BIG)
                pr_buf[w][:, e * tb:(e + 1) * tb] = jnp.exp(arg).astype(BF16)
                cs[e] = jnp.where(live2, cs[e] + incl[:, 0:1], cs[e])
            return cs[0], cs[1], o

        col = jnp.zeros((tb, 1), F32)
        cs, args = scores(i, (col, col), True)
        carry = (cs[0], cs[1], weigh(i, args, jnp.zeros((tb, HEAD_PAIR), F32)))
        carry = lax.fori_loop(0, jnp.where(i > 0, (i + 4) // 2, 0),
                              lambda n, cr: trip(2 * n + 1, 1, trip(2 * n, 0, cr)), carry)
        t_ref[:, 0:1] = carry[0]
        t_ref[:, 1:2] = carry[1]
        o_ref[...] = carry[2]

    score_buf = pltpu.VMEM((tb, tb), F32)
    return _call(
        body, name="attn_fwd",
        out_shape=[jax.ShapeDtypeStruct((s, D_ATTN), F32),
                   jax.ShapeDtypeStruct((4, s, 2), F32)],
        grid=(4, nq),
        in_specs=[pl.BlockSpec((tb, HEAD_PAIR), lambda h, i: (i, h)),
                  pl.BlockSpec((s, HEAD_PAIR), lambda h, i: (0, h)),
                  pl.BlockSpec((s, HEAD_PAIR), lambda h, i: (0, h))],
        out_specs=[pl.BlockSpec((tb, HEAD_PAIR), lambda h, i: (i, h)),
                   pl.BlockSpec((None, tb, 2), lambda h, i: (h, i, 0))],
        scratch_shapes=[score_buf] * 12 + [pltpu.VMEM((tb, 2 * tb), BF16)] * 2,
        compiler_params=_params(("arbitrary", "arbitrary")))(q, k, v)


def _mix_out(attn, mpool, x, attn_scale, w_out, g2, g3):
    s = x.shape[0]
    ts = 512

    def body(a_ref, p_ref, x_ref, as_ref, w_ref, g2_ref, g3_ref, ma_ref, mix_ref, x1_ref, h2_ref):
        ao = a_ref[...]
        ma = (ao * _rstd(ao) * as_ref[...]).astype(BF16)
        ma_ref[...] = ma
        mix = _dot(p_ref[...], w_ref[0:D_POOL, :]) + _dot(ma, w_ref[D_POOL:, :])
        mix_ref[...] = mix
        x1 = x_ref[...] + mix * _rstd(mix) * g2_ref[...]
        x1_ref[...] = x1
        h2_ref[...] = (x1 * _rstd(x1) * g3_ref[...]).astype(BF16)

    row = lambda w: pl.BlockSpec((ts, w), lambda i: (i, 0))
    return _call(
        body, name="mix_out",
        out_shape=[jax.ShapeDtypeStruct((s, D_ATTN), BF16), jax.ShapeDtypeStruct((s, D_MODEL), F32),
                   jax.ShapeDtypeStruct((s, D_MODEL), F32), jax.ShapeDtypeStruct((s, D_MODEL), BF16)],
        grid=(s // ts,),
        in_specs=[row(D_ATTN), row(D_POOL), row(D_MODEL), _full((1, D_ATTN)),
                  _full((D_MODEL, D_MODEL)), _full((1, D_MODEL)), _full((1, D_MODEL))],
        out_specs=[row(D_ATTN), row(D_MODEL), row(D_MODEL), row(D_MODEL)],
        compiler_params=_params(("parallel",)))(attn, mpool, x, attn_scale, w_out, g2, g3)


def _conv_rows(ext_ref, cw, cb, ts):
    y = cb + cw[0:1, :] * ext_ref[CONV_HALO - 2:CONV_HALO - 2 + ts, :]
    y = y + cw[1:2, :] * ext_ref[CONV_HALO - 1:CONV_HALO - 1 + ts, :]
    return y + cw[2:3, :] * ext_ref[CONV_HALO:CONV_HALO + ts, :]


def _sigmoid(v):
    return 1.0 / (1.0 + jnp.exp(-v))


def _ffn_up(h2, w_up, conv_w, conv_b):
    s = h2.shape[0]
    ts = 256
    tn = FF_TILE

    def body(h_ref, wg_ref, wv_ref, cwg_ref, cwv_ref, cbg_ref, cbv_ref,
             ug_ref, uv_ref, f_ref, extg, extv):
        i = pl.program_id(1)

        @pl.when(i == 0)
        def _():
            extg[0:CONV_HALO, :] = jnp.zeros((CONV_HALO, tn), F32)
            extv[0:CONV_HALO, :] = jnp.zeros((CONV_HALO, tn), F32)

        h = h_ref[...]
        ug = _dot(h, wg_ref[...])
        uv = _dot(h, wv_ref[...])
        ug_ref[...] = ug
        uv_ref[...] = uv
        extg[CONV_HALO:, :] = ug
        extv[CONV_HALO:, :] = uv
        gate = _conv_rows(extg, cwg_ref[...], cbg_ref[...], ts)
        val = _conv_rows(extv, cwv_ref[...], cbv_ref[...], ts)
        f_ref[...] = (gate * _sigmoid(gate) * val).astype(BF16)
        extg[0:CONV_HALO, :] = extg[ts:ts + CONV_HALO, :]
        extv[0:CONV_HALO, :] = extv[ts:ts + CONV_HALO, :]

    out_blk = pl.BlockSpec((None, ts, tn), lambda n, i: (n, i, 0))
    act = jax.ShapeDtypeStruct((2, s, tn), F32)
    return _call(
        body, name="ffn_up",
        out_shape=[act, act, jax.ShapeDtypeStruct((2, s, tn), BF16)],
        grid=(2, s // ts),
        in_specs=[pl.BlockSpec((ts, D_MODEL), lambda n, i: (i, 0)),
                  pl.BlockSpec((None, D_MODEL, tn), lambda n, i: (n, 0, 0)),
                  pl.BlockSpec((None, D_MODEL, tn), lambda n, i: (n + 2, 0, 0)),
                  pl.BlockSpec((None, 3, tn), lambda n, i: (n, 0, 0)),
                  pl.BlockSpec((None, 3, tn), lambda n, i: (n + 2, 0, 0)),
                  pl.BlockSpec((None, 1, tn), lambda n, i: (n, 0, 0)),
                  pl.BlockSpec((None, 1, tn), lambda n, i: (n + 2, 0, 0))],
        out_specs=[out_blk, out_blk, out_blk],
        scratch_shapes=[pltpu.VMEM((ts + CONV_HALO, tn), F32), pltpu.VMEM((ts + CONV_HALO, tn), F32)],
        compiler_params=_params(("arbitrary", "arbitrary")))(
            h2, w_up, w_up, conv_w, conv_w, conv_b, conv_b)


def _ffn_down(f_in, w_down, x1, target, g4):
    s = x1.shape[0]
    ts = 512

    def body(f_ref, w_ref, x1_ref, t_ref, g_ref, df_ref, dy_ref, loss_ref, dg_ref):
        @pl.when(pl.program_id(0) == 0)
        def _():
            loss_ref[...] = jnp.zeros_like(loss_ref)
            dg_ref[...] = jnp.zeros_like(dg_ref)

        f = _dot(f_ref[0], w_ref[0:FF_TILE, :]) + _dot(f_ref[1], w_ref[FF_TILE:, :])
        rf = _rstd(f)
        fn = f * rf
        g = g_ref[...]
        err = (x1_ref[...] + fn * g) - t_ref[...]
        loss_ref[...] += 0.5 * jnp.sum(jnp.mean(err * err, axis=-1))
        dy = err * (1.0 / D_MODEL)
        dy_ref[...] = dy
        dg_ref[...] += jnp.sum(dy * fn, axis=0, keepdims=True)
        dfn = dy * g
        df_ref[...] = (rf * (dfn - fn * jnp.mean(dfn * fn, axis=-1, keepdims=True))).astype(BF16)

    row = pl.BlockSpec((ts, D_MODEL), lambda i: (i, 0))
    return _call(
        body, name="ffn_down",
        out_shape=[jax.ShapeDtypeStruct((s, D_MODEL), BF16), jax.ShapeDtypeStruct((s, D_MODEL), F32),
                   jax.ShapeDtypeStruct((8, 128), F32), jax.ShapeDtypeStruct((1, D_MODEL), F32)],
        grid=(s // ts,),
        in_specs=[pl.BlockSpec((2, ts, FF_TILE), lambda i: (0, i, 0)), _full((D_FF, D_MODEL)),
                  row, row, _full((1, D_MODEL))],
        out_specs=[row, row, _full((8, 128)), _full((1, D_MODEL))],
        compiler_params=_params(("arbitrary",)))(f_in, w_down, x1, target, g4)


def _tn_matmul(a, b, name, ts=512):
    na, s, ka = a.shape
    nb, _, nbc = b.shape

    def body(a_ref, b_ref, o_ref):
        @pl.when(pl.program_id(2) == 0)
        def _():
            o_ref[...] = jnp.zeros_like(o_ref)

        o_ref[...] += _dot_tn(a_ref[...].astype(BF16), b_ref[...].astype(BF16))

    return _call(
        body, name=name, out_shape=jax.ShapeDtypeStruct((na, nb, ka, nbc), F32),
        grid=(na, nb, s // ts),
        in_specs=[pl.BlockSpec((None, ts, ka), lambda i, j, r: (i, r, 0)),
                  pl.BlockSpec((None, ts, nbc), lambda i, j, r: (j, r, 0))],
        out_specs=pl.BlockSpec((None, None, ka, nbc), lambda i, j, r: (i, j, 0, 0)),
        compiler_params=_params(("parallel", "parallel", "arbitrary")))(a, b)


def _ffn_bwd_act(df, w_down, upre_g, upre_v, conv_w, conv_b):
    s = df.shape[0]
    ts = 256
    tn = FF_TILE
    nr = s // ts
    per = ts // CONV_HALO

    def body(df_ref, wd_ref, ug_ref, uv_ref, hg_ref, hv_ref, cwg_ref, cwv_ref, cbg_ref, cbv_ref,
             dug_ref, duv_ref, dcwg_ref, dcwv_ref, dcbg_ref, dcbv_ref, extg, extv, dxg, dxv):
        i = pl.program_id(1)
        first_rows = i == nr - 1

        @pl.when(i == 0)
        def _():
            dxg[ts:, :] = jnp.zeros((CONV_HALO, tn), F32)
            dxv[ts:, :] = jnp.zeros((CONV_HALO, tn), F32)
            for r in (dcwg_ref, dcwv_ref, dcbg_ref, dcbv_ref):
                r[...] = jnp.zeros_like(r)

        extg[0:CONV_HALO, :] = jnp.where(first_rows, 0.0, hg_ref[...])
        extv[0:CONV_HALO, :] = jnp.where(first_rows, 0.0, hv_ref[...])
        extg[CONV_HALO:, :] = ug_ref[...]
        extv[CONV_HALO:, :] = uv_ref[...]
        cwg, cwv = cwg_ref[...], cwv_ref[...]
        gate = _conv_rows(extg, cwg, cbg_ref[...], ts)
        val = _conv_rows(extv, cwv, cbv_ref[...], ts)
        sg = _sigmoid(gate)
        dfin = _dot_nt(df_ref[...], wd_ref[...])
        dval = dfin * (gate * sg)
        dgate = dfin * val * (sg * (1.0 + gate * (1.0 - sg)))

        def conv_bwd(dact, ext, dx, cw, dcw_ref, dcb_ref, du_ref):
            dx[0:ts, :] = dact
            dcb_ref[...] += jnp.sum(dact, axis=0, keepdims=True)
            for kk in range(3):
                lo = CONV_HALO - 2 + kk
                dcw_ref[kk:kk + 1, :] += jnp.sum(dact * ext[lo:lo + ts, :], axis=0, keepdims=True)
            du = cw[2:3, :] * dact + cw[1:2, :] * dx[1:1 + ts, :] + cw[0:1, :] * dx[2:2 + ts, :]
            du_ref[...] = du.astype(BF16)
            dx[ts:, :] = dx[0:CONV_HALO, :]

        conv_bwd(dgate, extg, dxg, cwg, dcwg_ref, dcbg_ref, dug_ref)
        conv_bwd(dval, extv, dxv, cwv, dcwv_ref, dcbv_ref, duv_ref)

    rows = lambda n, i: (n, nr - 1 - i, 0)
    halo = lambda n, i: (n, jnp.maximum((nr - 1 - i) * per - 1, 0), 0)
    act_blk = pl.BlockSpec((None, ts, tn), rows)
    halo_blk = pl.BlockSpec((None, CONV_HALO, tn), halo)
    cw_blk = lambda off: pl.BlockSpec((None, 3, tn), lambda n, i: (n + off, 0, 0))
    cb_blk = lambda off: pl.BlockSpec((None, 1, tn), lambda n, i: (n + off, 0, 0))
    acc_w = pl.BlockSpec((None, 3, tn), lambda n, i: (n, 0, 0))
    acc_b = pl.BlockSpec((None, 1, tn), lambda n, i: (n, 0, 0))
    dact = jax.ShapeDtypeStruct((2, s, tn), BF16)
    return _call(
        body, name="ffn_bwd_act",
        out_shape=[dact, dact, jax.ShapeDtypeStruct((2, 3, tn), F32), jax.ShapeDtypeStruct((2, 3, tn), F32),
                   jax.ShapeDtypeStruct((2, 1, tn), F32), jax.ShapeDtypeStruct((2, 1, tn), F32)],
        grid=(2, nr),
        in_specs=[pl.BlockSpec((ts, D_MODEL), lambda n, i: (nr - 1 - i, 0)),
                  pl.BlockSpec((tn, D_MODEL), lambda n, i: (n, 0)),
                  act_blk, act_blk, halo_blk, halo_blk,
                  cw_blk(0), cw_blk(2), cb_blk(0), cb_blk(2)],
        out_specs=[act_blk, act_blk, acc_w, acc_w, acc_b, acc_b],
        scratch_shapes=[pltpu.VMEM((ts + CONV_HALO, tn), F32)] * 4,
        compiler_params=_params(("arbitrary", "arbitrary")))(
            df, w_down, upre_g, upre_v, upre_g, upre_v, conv_w, conv_w, conv_b, conv_b)


def _ffn_bwd_in(dug, duv, w_up, x1, dy, mix, g3, g2):
    s = x1.shape[0]
    ts = 256

    def body(dg_ref, dv_ref, w_ref, x1_ref, dy_ref, mix_ref, g3_ref, g2_ref,
             dx1_ref, dmix_ref, dg3_ref, dg2_ref):
        @pl.when(pl.program_id(0) == 0)
        def _():
            dg3_ref[...] = jnp.zeros_like(dg3_ref)
            dg2_ref[...] = jnp.zeros_like(dg2_ref)

        dh = _dot_nt(dg_ref[0], w_ref[0]) + _dot_nt(dg_ref[1], w_ref[1])
        dh = dh + _dot_nt(dv_ref[0], w_ref[2]) + _dot_nt(dv_ref[1], w_ref[3])
        x1 = x1_ref[...]
        r3 = _rstd(x1)
        xn = x1 * r3
        dg3_ref[...] += jnp.sum(dh * xn, axis=0, keepdims=True)
        dxn = dh * g3_ref[...]
        dx1 = dy_ref[...] + r3 * (dxn - xn * jnp.mean(dxn * xn, axis=-1, keepdims=True))
        dx1_ref[...] = dx1
        mix = mix_ref[...]
        rm = _rstd(mix)
        mn = mix * rm
        dg2_ref[...] += jnp.sum(dx1 * mn, axis=0, keepdims=True)
        dmn = dx1 * g2_ref[...]
        dmix_ref[...] = (rm * (dmn - mn * jnp.mean(dmn * mn, axis=-1, keepdims=True))).astype(BF16)

    row = pl.BlockSpec((ts, D_MODEL), lambda i: (i, 0))
    act = pl.BlockSpec((2, ts, FF_TILE), lambda i: (0, i, 0))
    vec = _full((1, D_MODEL))
    return _call(
        body, name="ffn_bwd_in",
        out_shape=[jax.ShapeDtypeStruct((s, D_MODEL), F32), jax.ShapeDtypeStruct((s, D_MODEL), BF16),
                   jax.ShapeDtypeStruct((1, D_MODEL), F32), jax.ShapeDtypeStruct((1, D_MODEL), F32)],
        grid=(s // ts,),
        in_specs=[act, act, _full(w_up.shape), row, row, row, vec, vec],
        out_specs=[row, row, vec, vec],
        compiler_params=_params(("arbitrary",), vmem_mb=56))(dug, duv, w_up, x1, dy, mix, g3, g2)


def _mix_bwd(dmix, w_out, attn, attn_scale):
    s = dmix.shape[0]
    ts = 512

    def body(dm_ref, w_ref, a_ref, as_ref, dp_ref, do_ref, das_ref):
        @pl.when(pl.program_id(0) == 0)
        def _():
            das_ref[...] = jnp.zeros_like(das_ref)

        dm = dm_ref[...]
        dp_ref[...] = _dot_nt(dm, w_ref[0:D_POOL, :])
        da = _dot_nt(dm, w_ref[D_POOL:, :])
        ao = a_ref[...]
        ra = _rstd(ao)
        an = ao * ra
        das_ref[...] += jnp.sum(da * an, axis=0, keepdims=True)
        dan = da * as_ref[...]
        do_ref[...] = (ra * (dan - an * jnp.mean(dan * an, axis=-1, keepdims=True))).astype(BF16)

    row = lambda w: pl.BlockSpec((ts, w), lambda i: (i, 0))
    return _call(
        body, name="mix_bwd",
        out_shape=[jax.ShapeDtypeStruct((s, D_POOL), F32), jax.ShapeDtypeStruct((s, D_ATTN), BF16),
                   jax.ShapeDtypeStruct((1, D_ATTN), F32)],
        grid=(s // ts,),
        in_specs=[row(D_MODEL), _full((D_MODEL, D_MODEL)), row(D_ATTN), _full((1, D_ATTN))],
        out_specs=[row(D_POOL), row(D_ATTN), _full((1, D_ATTN))],
        compiler_params=_params(("arbitrary",)))(dmix, w_out, attn, attn_scale)


def _attn_bwd(q, k, v, do, totals):
    s = q.shape[0]
    tb = ATT_BLOCK
    nq = s // tb

    def body(q_ref, do_ref, t_ref, k_hbm, v_hbm, dq_ref, dk_hbm, dv_hbm,
             k_scr, v_scr, dkt_acc, dvt_acc, stage, *bufs):
        hp = pl.program_id(0)
        i = pl.program_id(1)
        lanes = pl.ds(pl.multiple_of(hp * HEAD_PAIR, HEAD_PAIR), HEAD_PAIR)

        @pl.when(i == 0)
        def _():
            pltpu.sync_copy(k_hbm.at[:, lanes], k_scr)
            pltpu.sync_copy(v_hbm.at[:, lanes], v_scr)
            dkt_acc[...] = jnp.zeros_like(dkt_acc)
            dvt_acc[...] = jnp.zeros_like(dvt_acc)

        upper = _tri("suffix")
        lower = _tri("prefix")
        causal = _causal_mask()
        lane = lax.broadcasted_iota(jnp.int32, (1, HEAD_PAIR), 1)
        first = lane < 64
        q2 = q_ref[...]
        do2 = do_ref[...]
        zero = jnp.zeros_like(q2)
        qs = (jnp.where(first, q2, zero), jnp.where(first, zero, q2))
        dos = (jnp.where(first, do2, zero), jnp.where(first, zero, do2))
        qcat_t = jnp.concatenate(qs, axis=0).astype(F32).T.astype(BF16)
        docat_t = jnp.concatenate(dos, axis=0).astype(F32).T.astype(BF16)
        tots = (t_ref[:, 0:1], t_ref[:, 1:2])

        z_buf, zc_buf, sg_buf, sg2_buf, in_buf, da_buf, dw_buf, pre_buf = [
            [[bufs[4 * kind + 2 * slot + e] for e in range(2)] for slot in range(2)]
            for kind in range(8)]
        pr_buf, dzr_buf, dzc_buf = bufs[32:34], bufs[34:36], bufs[36:38]

        for e in range(2):
            z_buf[1][e][...] = jnp.full((tb, tb), NEG_BIG, F32)
            zc_buf[1][e][...] = jnp.full((tb, tb), NEG_BIG, F32)
            for buf in (sg_buf, sg2_buf, in_buf, da_buf, dw_buf, pre_buf):
                buf[1][e][...] = jnp.zeros((tb, tb), F32)
        for buf in (pr_buf, dzr_buf, dzc_buf):
            buf[1][...] = jnp.zeros_like(buf[1])

        def rows(p):
            return pl.ds(pl.multiple_of(jnp.clip(p, 0, nq - 1) * tb, tb), tb)

        def split_heads(block):
            return jnp.concatenate([jnp.where(first, block, zero), jnp.where(first, zero, block)], axis=0)

        def trip(t, w, carry):
            r = 1 - w
            cs, cps, dq = list(carry[0:2]), list(carry[2:4]), carry[4]
            live4 = (t - 4 >= 0) & (t - 4 < i)
            dq = dq + jnp.where(live4, _dot(dzc_buf[r][...], split_heads(k_scr[rows(t - 4), :])), 0.0)
            dkt_acc[:, rows(t - 4)] += jnp.where(live4, _dot(qcat_t, dzr_buf[r][...]), 0.0)
            dvt_acc[:, rows(t - 3)] += _dot(docat_t, pr_buf[r][...])
            kj = k_scr[rows(t), :]
            for e in range(2):
                z_buf[w][e][...] = _dot_nt(qs[e], kj)
            vj = v_scr[rows(t - 1), :]
            for e in range(2):
                z = z_buf[r][e][...]
                sp, sig = _softplus(z, True)
                zc_buf[w][e][...] = z
                sg_buf[w][e][...] = sig
                in_buf[w][e][...] = _dot(sp.astype(BF16), upper)
                da_buf[w][e][...] = _dot_nt(dos[e], vj)
            for e in range(2):
                incl = in_buf[r][e][...]
                cs[e] = cs[e] + incl[:, 0:1]
                off = jnp.where(t - 2 < i, tots[e] - cs[e], -NEG_BIG)
                a = jnp.exp(zc_buf[r][e][...] - incl - off)
                dw = a * da_buf[r][e][...]
                dw_buf[w][e][...] = dw
                sg2_buf[w][e][...] = sg_buf[r][e][...]
                pr_buf[w][e * tb:(e + 1) * tb, :] = a.astype(BF16)
                pre_buf[w][e][...] = _dot(dw.astype(BF16), lower)
            for e in range(2):
                pre = pre_buf[r][e][...] + cps[e]
                dzb = (dw_buf[r][e][...] - sg2_buf[r][e][...] * pre).astype(BF16)
                cps[e] = pre[:, tb - 1:tb]
                dzr_buf[w][e * tb:(e + 1) * tb, :] = dzb
                dzc_buf[w][:, e * tb:(e + 1) * tb] = dzb
            return cs[0], cs[1], cps[0], cps[1], dq

        col = jnp.zeros((tb, 1), F32)
        carry = (col, col, col, col, jnp.zeros((tb, HEAD_PAIR), F32))
        carry = lax.fori_loop(0, jnp.where(i > 0, (i + 5) // 2, 0),
                              lambda n, cr: trip(2 * n + 1, 1, trip(2 * n, 0, cr)), carry)

        cps, dq = carry[2:4], carry[4]
        kj = k_scr[rows(i), :]
        vj = v_scr[rows(i), :]
        dzs, probs = [], []
        for e in range(2):
            z = _dot_nt(qs[e], kj)
            sp, sig = _softplus(z, True)
            incl = _dot(jnp.where(causal, sp, 0.0).astype(BF16), upper)
            a = jnp.where(causal, jnp.exp(z - incl), 0.0)
            dw = a * _dot_nt(dos[e], vj)
            pre = _dot(dw.astype(BF16), lower) + cps[e]
            dzs.append(jnp.where(causal, dw - sig * pre, 0.0).astype(BF16))
            probs.append(a.astype(BF16))
        dq = dq + _dot(jnp.concatenate(dzs, axis=1), split_heads(kj))
        dkt_acc[:, rows(i)] += _dot(qcat_t, jnp.concatenate(dzs, axis=0))
        dvt_acc[:, rows(i)] += _dot(docat_t, jnp.concatenate(probs, axis=0))
        dq_ref[...] = (dq * Q_SCALE).astype(BF16)

        @pl.when(i == nq - 1)
        def _():
            for acc, dst in ((dkt_acc, dk_hbm), (dvt_acc, dv_hbm)):
                def flip(n, _, acc=acc):
                    at = pl.ds(pl.multiple_of(n * tb, tb), tb)
                    stage[at, :] = acc[:, at].T
                    return 0
                lax.fori_loop(0, nq, flip, 0)
                pltpu.sync_copy(stage, dst.at[:, lanes])

    blk = pl.BlockSpec((tb, HEAD_PAIR), lambda h, i: (i, h))
    grad = jax.ShapeDtypeStruct((s, D_ATTN), F32)
    return _call(
        body, name="attn_bwd",
        out_shape=[jax.ShapeDtypeStruct((s, D_ATTN), BF16), grad, grad],
        grid=(4, nq),
        in_specs=[blk, blk, pl.BlockSpec((None, tb, 2), lambda h, i: (h, i, 0)), ANY, ANY],
        out_specs=[blk, ANY, ANY],
        scratch_shapes=[pltpu.VMEM((s, HEAD_PAIR), BF16), pltpu.VMEM((s, HEAD_PAIR), BF16),
                        pltpu.VMEM((HEAD_PAIR, s), F32), pltpu.VMEM((HEAD_PAIR, s), F32),
                        pltpu.VMEM((s, HEAD_PAIR), F32)]
        + [pltpu.VMEM((tb, tb), F32)] * 32
        + [pltpu.VMEM((2 * tb, tb), BF16)] * 4 + [pltpu.VMEM((tb, 2 * tb), BF16)] * 2,
        compiler_params=_params(("arbitrary", "arbitrary"), vmem_mb=60))(q, do, totals, k, v)


def _pool_bwd(u, dmp, w_pool, pool_scale):
    s = u.shape[0]
    ts = 512
    nr = s // ts
    per = ts // POOL_HALO

    def body(u_ref, halo_ref, dm_ref, wp_ref, ps_ref, du_ref, dwp_ref, dps_ref, ext_ref, y_ref, dext_ref):
        i = pl.program_id(0)
        rb = nr - 1 - i

        @pl.when(i == 0)
        def _():
            dext_ref[ts:, :] = jnp.zeros((POOL_HALO, D_POOL), F32)
            dwp_ref[...] = jnp.zeros_like(dwp_ref)
            dps_ref[...] = jnp.zeros_like(dps_ref)

        ext_ref[0:POOL_HALO, :] = jnp.where(rb > 0, halo_ref[...], 0.0)
        ext_ref[POOL_HALO:, :] = u_ref[...]
        ps, cnts = [], []
        for g, window in enumerate(POOL_WINDOWS):
            p, cnt = _pool_means(ext_ref, g, window, ts, rb * ts)
            ps.append(p.astype(BF16))
            cnts.append(cnt)
            y_ref[:, g * POOL_GROUP:(g + 1) * POOL_GROUP] = _dot(ps[g], wp_ref[g].astype(BF16))
        y = y_ref[...]
        r = _rstd(y)
        yn = y * r
        dm = dm_ref[...]
        dps_ref[...] += jnp.sum(dm * yn, axis=0, keepdims=True)
        dn = dm * ps_ref[...]
        dy = r * (dn - yn * jnp.mean(dn * yn, axis=-1, keepdims=True))
        for g, window in enumerate(POOL_WINDOWS):
            cols = slice(g * POOL_GROUP, (g + 1) * POOL_GROUP)
            dyg = dy[:, cols].astype(BF16)
            dwp_ref[g] += _dot_tn(ps[g], dyg)
            dp = _dot_nt(dyg, wp_ref[g].astype(BF16))
            dext_ref[0:ts, cols] = dp / cnts[g]
            acc = dext_ref[0:ts, cols]
            for d in range(1, window):
                acc = acc + dext_ref[d:d + ts, cols]
            du_ref[:, cols] = (acc - dp).astype(BF16)
        dext_ref[ts:, :] = dext_ref[0:POOL_HALO, :]

    rows = pl.BlockSpec((ts, D_POOL), lambda i: (nr - 1 - i, 0))
    return _call(
        body, name="pool_bwd",
        out_shape=[jax.ShapeDtypeStruct((s, D_POOL), BF16), jax.ShapeDtypeStruct(w_pool.shape, F32),
                   jax.ShapeDtypeStruct((1, D_POOL), F32)],
        grid=(nr,),
        in_specs=[rows,
                  pl.BlockSpec((POOL_HALO, D_POOL), lambda i: (jnp.maximum((nr - 1 - i) * per - 1, 0), 0)),
                  rows, _full(w_pool.shape), _full((1, D_POOL))],
        out_specs=[rows, _full(w_pool.shape), _full((1, D_POOL))],
        scratch_shapes=[pltpu.VMEM((ts + POOL_HALO, D_POOL), F32), pltpu.VMEM((ts, D_POOL), F32),
                        pltpu.VMEM((ts + POOL_HALO, D_POOL), F32)],
        compiler_params=_params(("arbitrary",)))(u, u, dmp, w_pool, pool_scale)


def _in_proj_bwd(du, dq, dk, dv, w_in, x, dx1, g1):
    s = x.shape[0]
    ts = 512

    def body(du_ref, dq_ref, dk_ref, dv_ref, w_ref, x_ref, dx1_ref, g_ref, gx_ref, dg_ref):
        @pl.when(pl.program_id(0) == 0)
        def _():
            dg_ref[...] = jnp.zeros_like(dg_ref)

        dh = _dot_nt(du_ref[...], w_ref[0]) + _dot_nt(dq_ref[...], w_ref[1])
        dh = dh + _dot_nt(dk_ref[...].astype(BF16), w_ref[2]) + _dot_nt(dv_ref[...].astype(BF16), w_ref[3])
        xv = x_ref[...]
        r = _rstd(xv)
        xn = xv * r
        dg_ref[...] += jnp.sum(dh * xn, axis=0, keepdims=True)
        dxn = dh * g_ref[...]
        gx_ref[...] = dx1_ref[...] + r * (dxn - xn * jnp.mean(dxn * xn, axis=-1, keepdims=True))

    row = lambda w: pl.BlockSpec((ts, w), lambda i: (i, 0))
    return _call(
        body, name="in_proj_bwd",
        out_shape=[jax.ShapeDtypeStruct((s, D_MODEL), F32), jax.ShapeDtypeStruct((1, D_MODEL), F32)],
        grid=(s // ts,),
        in_specs=[row(D_POOL)] * 4 + [_full(w_in.shape), row(D_MODEL), row(D_MODEL), _full((1, D_MODEL))],
        out_specs=[row(D_MODEL), _full((1, D_MODEL))],
        compiler_params=_params(("arbitrary",)))(du, dq, dk, dv, w_in, x, dx1, g1)


_SMALL = ("norm_mix_pre", "w_pool", "pool_scale", "attn_scale", "norm_mix_post",
          "norm_ffn_pre", "conv_b", "norm_ffn_post")
_SMALL_SIZE = {"norm_mix_pre": 1024, "w_pool": 65536, "pool_scale": 512, "attn_scale": 512,
               "norm_mix_post": 1024, "norm_ffn_pre": 1024, "conv_b": 5632, "norm_ffn_post": 1024}
_SMALL_ROWS = 600
_CONVW_ROWS = 132
_PACK_ROWS = _SMALL_ROWS + _CONVW_ROWS + 4


def _pack_small(parts):
    flat = jnp.concatenate([parts[n].reshape(-1) for n in _SMALL])
    flat = jnp.pad(flat, (0, _SMALL_ROWS * 128 - flat.shape[0]))
    return flat.reshape(_SMALL_ROWS, 128)


def _unpack_small(packed, like):
    flat = packed.reshape(-1)
    out, off = {}, 0
    for n in _SMALL:
        out[n] = flat[off:off + _SMALL_SIZE[n]].reshape(like[n].shape)
        off += _SMALL_SIZE[n]
    return out


def kernel(x, norm_mix_pre, w_in, w_pool, pool_scale, attn_scale, w_out, norm_mix_post, norm_ffn_pre, w_up, conv_w, conv_b, w_down, norm_ffn_post, loss_target, m_norm_mix_pre, m_w_in, m_w_pool, m_pool_scale, m_attn_scale, m_w_out, m_norm_mix_post, m_norm_ffn_pre, m_w_up, m_conv_w, m_conv_b, m_w_down, m_norm_ffn_post, v_norm_mix_pre, v_w_in, v_w_pool, v_pool_scale, v_attn_scale, v_w_out, v_norm_mix_post, v_norm_ffn_pre, v_w_up, v_conv_w, v_conv_b, v_w_down, v_norm_ffn_post):
    weights = dict(norm_mix_pre=norm_mix_pre, w_in=w_in, w_pool=w_pool, pool_scale=pool_scale,
                   attn_scale=attn_scale, w_out=w_out, norm_mix_post=norm_mix_post,
                   norm_ffn_pre=norm_ffn_pre, w_up=w_up, conv_w=conv_w, conv_b=conv_b,
                   w_down=w_down, norm_ffn_post=norm_ffn_post)
    mom1 = dict(norm_mix_pre=m_norm_mix_pre, w_in=m_w_in, w_pool=m_w_pool, pool_scale=m_pool_scale,
                attn_scale=m_attn_scale, w_out=m_w_out, norm_mix_post=m_norm_mix_post,
                norm_ffn_pre=m_norm_ffn_pre, w_up=m_w_up, conv_w=m_conv_w, conv_b=m_conv_b,
                w_down=m_w_down, norm_ffn_post=m_norm_ffn_post)
    mom2 = dict(norm_mix_pre=v_norm_mix_pre, w_in=v_w_in, w_pool=v_w_pool, pool_scale=v_pool_scale,
                attn_scale=v_attn_scale, w_out=v_w_out, norm_mix_post=v_norm_mix_post,
                norm_ffn_pre=v_norm_ffn_pre, w_up=v_w_up, conv_w=v_conv_w, conv_b=v_conv_b,
                w_down=v_w_down, norm_ffn_post=v_norm_ffn_post)
    order = list(weights)

    xs = x[0]
    target = loss_target[0]
    wp = w_pool[0]
    shard = lax.axis_index("x") * 2 + lax.axis_index("y")

    shards = [_cast_bf16(w_in[0], "cast_w_in"), _cast_bf16(w_out[0], "cast_w_out"),
              _cast_bf16(w_up[0], "cast_w_up"), _cast_bf16(w_down[0], "cast_w_down"), conv_w[0]]
    win_g, wout_g, wup_g, wdown_g, convw_g = _gather_shards(shards)
    wout_f = wout_g.reshape(D_MODEL, D_MODEL)
    wdown_f = wdown_g.reshape(D_FF, D_MODEL)
    convb_g = conv_b[0].reshape(N_SHARD, 1, FF_TILE)

    u, q, k, v, h1 = _in_proj(xs, norm_mix_pre, win_g)
    mpool = _pool_fwd(u, wp, pool_scale)
    attn, totals = _attn_fwd(q, k, v)
    mattn, mix, x1, h2 = _mix_out(attn, mpool, xs, attn_scale, wout_f, norm_mix_post, norm_ffn_pre)
    upre_g, upre_v, f_in = _ffn_up(h2, wup_g, convw_g, convb_g)
    df, dy, loss_tile, d_post = _ffn_down(f_in, wdown_f, x1, target, norm_ffn_post)

    d_wdown = _tn_matmul(f_in, df[None], "dw_down")
    dug, duv, dcw_g, dcw_v, dcb_g, dcb_v = _ffn_bwd_act(df, wdown_f, upre_g, upre_v, convw_g, convb_g)
    d_wup = jnp.concatenate([_tn_matmul(h2[None], dug, "dw_up_gate")[0],
                             _tn_matmul(h2[None], duv, "dw_up_value")[0]], axis=0)
    dx1, dmix, d_ffn_pre, d_mix_post = _ffn_bwd_in(dug, duv, wup_g, x1, dy, mix, norm_ffn_pre, norm_mix_post)
    d_wout = jnp.concatenate([_tn_matmul(mpool[None], dmix[None], "dw_out_pool")[0, 0],
                              _tn_matmul(mattn[None], dmix[None], "dw_out_attn")[0, 0]], axis=0)
    dmp, do, d_attn_scale = _mix_bwd(dmix, wout_f, attn, attn_scale)
    dq, dk, dv = _attn_bwd(q, k, v, do, totals)
    du, d_wpool, d_pool_scale = _pool_bwd(u, dmp, wp, pool_scale)
    d_win = jnp.stack([_tn_matmul(h1[None], t[None], "dw_in_%d" % n)[0, 0]
                       for n, t in enumerate((du, dq, dk, dv))])
    grad_x, d_mix_pre = _in_proj_bwd(du, dq, dk, dv, win_g, xs, dx1, norm_mix_pre)

    d_convw = jnp.concatenate([dcw_g, dcw_v], axis=0)
    d_convb = jnp.concatenate([dcb_g, dcb_v], axis=0).reshape(1, 2 * D_FF)
    small_parts = dict(norm_mix_pre=d_mix_pre, w_pool=d_wpool, pool_scale=d_pool_scale,
                       attn_scale=d_attn_scale, norm_mix_post=d_mix_post, norm_ffn_pre=d_ffn_pre,
                       conv_b=d_convb, norm_ffn_post=d_post)
    packed = jnp.concatenate([_pack_small(small_parts), d_convw.reshape(_CONVW_ROWS, 128),
                              jnp.zeros((4, 128), F32)], axis=0)
    big = [d_win, d_wout.reshape(N_SHARD, D_MODEL // N_SHARD, D_MODEL), d_wup,
           d_wdown.reshape(N_SHARD, D_FF // N_SHARD, D_MODEL)]
    recv, gathered = _scatter_grads(big, packed)
    quarter = [_sum_slots(r, (3, 0, 1, 2), "sum_chips_%d" % n) for n, r in enumerate(recv)]
    sibling = _swap_with_sibling(quarter)
    small_sum = _sum_slots(gathered, tuple(range(8)), "sum_small")

    results = {}
    for n, name in enumerate(("w_in", "w_out", "w_up", "w_down")):
        res = _adamw([quarter[n], sibling[n]], weights[name][0], mom1[name][0], mom2[name][0],
                     "adamw_" + name)
        results[name] = [t[None] for t in res]
    g_convw = lax.dynamic_slice_in_dim(
        small_sum[_SMALL_ROWS:_SMALL_ROWS + _CONVW_ROWS].reshape(N_SHARD, 3, FF_TILE), shard, 1, axis=0)[0]
    convw_pad = lambda t: jnp.pad(t, ((0, 5), (0, 0)))
    res = _adamw([convw_pad(g_convw)], convw_pad(conv_w[0]), convw_pad(m_conv_w[0]),
                 convw_pad(v_conv_w[0]), "adamw_conv_w")
    results["conv_w"] = [t[:3][None] for t in res]
    pack_w = _pack_small(weights)
    pack_m = _pack_small(mom1)
    pack_v = _pack_small(mom2)
    res = _adamw([small_sum[:_SMALL_ROWS]], pack_w, pack_m, pack_v, "adamw_small")
    unpacked = [_unpack_small(t, weights) for t in res]
    for name in _SMALL:
        results[name] = [t[name] for t in unpacked]

    loss = lax.psum(loss_tile[0, 0], ("x", "y", "c"))
    outs = [loss, grad_x[None]]
    for slot in range(4):
        outs.extend(results[name][slot] for name in order)
    return tuple(outs)
```

```python
import functools

import jax
import jax.numpy as jnp
from jax import lax
from jax.experimental import pallas as pl
from jax.experimental.pallas import tpu as pltpu

F32 = jnp.float32
BF16 = jnp.bfloat16

D_MODEL = 1024
D_POOL = 512
D_ATTN = 512
POOL_WINDOWS = (2, 4, 8, 16)
POOL_GROUP = 128
POOL_HALO = 16
CONV_HALO = 8
D_FF = 2816
FF_TILE = 1408
N_SHARD = 4
EPS = 1e-6
Q_SCALE = 0.125
ATT_BLOCK = 256
HEAD_PAIR = 128
MIB = 1 << 20
NEG_BIG = -1e30

ADAM_LR = 0.001
ADAM_B1 = 0.9
ADAM_B2 = 0.999
ADAM_EPS = 1e-08
ADAM_WD = 0.01
ADAM_STEP = 10

NT_DIMS = (((1,), (1,)), ((), ()))
TN_DIMS = (((0,), (0,)), ((), ()))
MESH = pl.DeviceIdType.MESH
ANY = pl.BlockSpec(memory_space=pl.ANY)


def _call(body, **kw):
    return pl.pallas_call(body, **kw)


def _params(sem=None, vmem_mb=48):
    return pltpu.CompilerParams(dimension_semantics=sem, vmem_limit_bytes=vmem_mb * MIB)


def _rstd(v):
    return lax.rsqrt(jnp.mean(v * v, axis=-1, keepdims=True) + EPS)


def _dot(a, b):
    return jnp.dot(a, b, preferred_element_type=F32)


def _dot_nt(a, b):
    return lax.dot_general(a, b, NT_DIMS, preferred_element_type=F32)


def _dot_tn(a, b):
    return lax.dot_general(a, b, TN_DIMS, preferred_element_type=F32)


def _row_tile(rows, cap):
    t = min(rows, cap)
    t -= t % 8
    while rows % t:
        t -= 8
    return t


def _full(shape):
    nd = len(shape)
    return pl.BlockSpec(shape, lambda *_: (0,) * nd)


def _chip_peers():
    x, y, c = lax.axis_index("x"), lax.axis_index("y"), lax.axis_index("c")
    return x, y, c, [(1 - x, y), (x, 1 - y), (1 - x, 1 - y)]


def _cast_bf16(a, name):
    def body(a_ref, o_ref):
        o_ref[...] = a_ref[...].astype(BF16)

    return _call(body, name=name, out_shape=jax.ShapeDtypeStruct(a.shape, BF16),
                 grid=(1,), in_specs=[_full(a.shape)], out_specs=_full(a.shape),
                 compiler_params=_params(("arbitrary",)))(a)


def _gather_shards(shards):
    n = len(shards)

    def body(*refs):
        ins, outs = refs[:n], refs[n:2 * n]
        send, recv, loc = refs[2 * n:]
        x, y, c, chips = _chip_peers()
        b = 2 * x + y
        local = [pltpu.make_async_copy(ins[t], outs[t].at[b], loc.at[t]) for t in range(n)]
        for cp in local:
            cp.start()
        remote = []
        for t in range(n):
            for k, (px, py) in enumerate(chips):
                remote.append(pltpu.make_async_remote_copy(
                    src_ref=ins[t], dst_ref=outs[t].at[b],
                    send_sem=send.at[3 * t + k], recv_sem=recv.at[3 * t + k],
                    device_id=(px, py, c), device_id_type=MESH))
        for cp in remote:
            cp.start()
        for cp in remote:
            cp.wait()
        for cp in local:
            cp.wait()

    return _call(
        body, name="gather_weights",
        out_shape=[jax.ShapeDtypeStruct((N_SHARD,) + s.shape, s.dtype) for s in shards],
        in_specs=[ANY] * n, out_specs=[ANY] * n,
        scratch_shapes=[pltpu.SemaphoreType.DMA((3 * n,)), pltpu.SemaphoreType.DMA((3 * n,)),
                        pltpu.SemaphoreType.DMA((n,))],
    )(*shards)


def _scatter_grads(grads, small):
    n = len(grads)

    def body(*refs):
        ins, small_in = refs[:n], refs[n]
        outs, small_out = refs[n + 1:2 * n + 1], refs[2 * n + 1]
        send, recv, loc, ssend, srecv = refs[2 * n + 2:]
        x, y, c, chips = _chip_peers()
        b = 2 * x + y
        me = 4 * x + 2 * y + c
        local = [pltpu.make_async_copy(ins[t].at[b], outs[t].at[3], loc.at[t]) for t in range(n)]
        local.append(pltpu.make_async_copy(small_in, small_out.at[me], loc.at[n]))
        for cp in local:
            cp.start()
        remote = []
        for t in range(n):
            for k, (px, py) in enumerate(chips):
                remote.append(pltpu.make_async_remote_copy(
                    src_ref=ins[t].at[2 * px + py], dst_ref=outs[t].at[k],
                    send_sem=send.at[3 * t + k], recv_sem=recv.at[3 * t + k],
                    device_id=(px, py, c), device_id_type=MESH))
        for r in range(1, 8):
            px = 1 - x if r & 4 else x
            py = 1 - y if r & 2 else y
            pc = 1 - c if r & 1 else c
            remote.append(pltpu.make_async_remote_copy(
                src_ref=small_in, dst_ref=small_out.at[me],
                send_sem=ssend.at[r - 1], recv_sem=srecv.at[r - 1],
                device_id=(px, py, pc), device_id_type=MESH))
        for cp in remote:
            cp.start()
        for cp in remote:
            cp.wait()
        for cp in local:
            cp.wait()

    out_shape = [jax.ShapeDtypeStruct(g.shape, g.dtype) for g in grads]
    out_shape.append(jax.ShapeDtypeStruct((8,) + small.shape, small.dtype))
    res = _call(
        body, name="scatter_grads", out_shape=out_shape,
        in_specs=[ANY] * (n + 1), out_specs=[ANY] * (n + 1),
        scratch_shapes=[pltpu.SemaphoreType.DMA((3 * n,)), pltpu.SemaphoreType.DMA((3 * n,)),
                        pltpu.SemaphoreType.DMA((n + 1,)),
                        pltpu.SemaphoreType.DMA((7,)), pltpu.SemaphoreType.DMA((7,))],
    )(*grads, small)
    return res[:n], res[n]


def _swap_with_sibling(parts):
    n = len(parts)

    def body(*refs):
        ins, outs = refs[:n], refs[n:2 * n]
        send, recv = refs[2 * n:]
        x, y, c = lax.axis_index("x"), lax.axis_index("y"), lax.axis_index("c")
        copies = [pltpu.make_async_remote_copy(
            src_ref=ins[t], dst_ref=outs[t], send_sem=send.at[t], recv_sem=recv.at[t],
            device_id=(x, y, 1 - c), device_id_type=MESH) for t in range(n)]
        for cp in copies:
            cp.start()
        for cp in copies:
            cp.wait()

    return _call(
        body, name="swap_sibling",
        out_shape=[jax.ShapeDtypeStruct(p.shape, p.dtype) for p in parts],
        in_specs=[ANY] * n, out_specs=[ANY] * n,
        scratch_shapes=[pltpu.SemaphoreType.DMA((n,)), pltpu.SemaphoreType.DMA((n,))],
    )(*parts)


def _sum_slots(buf, order, name):
    k, rows, cols = buf.shape
    tr = _row_tile(rows, 256)

    def body(b_ref, o_ref):
        acc = b_ref[order[0]].astype(F32)
        for s in order[1:]:
            acc = acc + b_ref[s].astype(F32)
        o_ref[...] = acc

    return _call(body, name=name, out_shape=jax.ShapeDtypeStruct((rows, cols), F32),
                 grid=(rows // tr,),
                 in_specs=[pl.BlockSpec((k, tr, cols), lambda i: (0, i, 0))],
                 out_specs=pl.BlockSpec((tr, cols), lambda i: (i, 0)),
                 compiler_params=_params(("parallel",)))(buf)


def _adamw(grad_parts, w, m, v, name):
    rows, cols = w.shape
    tr = _row_tile(rows, 256)
    npart = len(grad_parts)

    def body(*refs):
        gp = refs[:npart]
        w_ref, m_ref, v_ref, g_out, d_out, m_out, v_out = refs[npart:]
        g = gp[0][...]
        for p in gp[1:]:
            g = g + p[...]
        mm = ADAM_B1 * m_ref[...] + (1.0 - ADAM_B1) * g
        vv = ADAM_B2 * v_ref[...] + (1.0 - ADAM_B2) * jnp.square(g)
        m_hat = mm / (1.0 - ADAM_B1 ** ADAM_STEP)
        v_hat = vv / (1.0 - ADAM_B2 ** ADAM_STEP)
        g_out[...] = g
        d_out[...] = -ADAM_LR * (m_hat / (jnp.sqrt(v_hat) + ADAM_EPS) + ADAM_WD * w_ref[...])
        m_out[...] = mm
        v_out[...] = vv

    spec = pl.BlockSpec((tr, cols), lambda i: (i, 0))
    shp = jax.ShapeDtypeStruct((rows, cols), F32)
    return _call(body, name=name, out_shape=[shp] * 4, grid=(rows // tr,),
                 in_specs=[spec] * (npart + 3), out_specs=[spec] * 4,
                 compiler_params=_params(("parallel",)))(*grad_parts, w, m, v)


def _in_proj(x, g1, w_in):
    s = x.shape[0]
    ts = 512

    def body(x_ref, g_ref, w_ref, u_ref, q_ref, k_ref, v_ref, h_ref):
        xv = x_ref[...]
        h = (xv * _rstd(xv) * g_ref[...]).astype(BF16)
        h_ref[...] = h
        u_ref[...] = _dot(h, w_ref[0])
        q_ref[...] = (_dot(h, w_ref[1]) * Q_SCALE).astype(BF16)
        k_ref[...] = _dot(h, w_ref[2]).astype(BF16)
        v_ref[...] = _dot(h, w_ref[3]).astype(BF16)

    row = lambda w: pl.BlockSpec((ts, w), lambda i: (i, 0))
    half = jax.ShapeDtypeStruct((s, D_POOL), BF16)
    return _call(
        body, name="in_proj",
        out_shape=[jax.ShapeDtypeStruct((s, D_POOL), F32), half, half, half,
                   jax.ShapeDtypeStruct((s, D_MODEL), BF16)],
        grid=(s // ts,),
        in_specs=[row(D_MODEL), _full((1, D_MODEL)), _full(w_in.shape)],
        out_specs=[row(D_POOL)] * 4 + [row(D_MODEL)],
        compiler_params=_params(("parallel",)))(x, g1, w_in)


def _pool_means(ext_ref, g, window, ts, row0):
    cols = slice(g * POOL_GROUP, (g + 1) * POOL_GROUP)
    cur = ext_ref[POOL_HALO:POOL_HALO + ts, cols]
    acc = cur
    for d in range(1, window):
        acc = acc + ext_ref[POOL_HALO - d:POOL_HALO - d + ts, cols]
    t1 = row0 + 1 + lax.broadcasted_iota(jnp.int32, (ts, 1), 0)
    cnt = jnp.minimum(t1, window).astype(F32)
    return acc / cnt - cur, cnt


def _pool_fwd(u, w_pool, pool_scale):
    s = u.shape[0]
    ts = 512
    per = ts // POOL_HALO

    def body(u_ref, halo_ref, wp_ref, ps_ref, o_ref, ext_ref, y_ref):
        i = pl.program_id(0)
        ext_ref[0:POOL_HALO, :] = jnp.where(i > 0, halo_ref[...], 0.0)
        ext_ref[POOL_HALO:, :] = u_ref[...]
        for g, window in enumerate(POOL_WINDOWS):
            p, _ = _pool_means(ext_ref, g, window, ts, i * ts)
            y_ref[:, g * POOL_GROUP:(g + 1) * POOL_GROUP] = _dot(
                p.astype(BF16), wp_ref[g].astype(BF16))
        y = y_ref[...]
        o_ref[...] = (y * _rstd(y) * ps_ref[...]).astype(BF16)

    return _call(
        body, name="pool_fwd", out_shape=jax.ShapeDtypeStruct((s, D_POOL), BF16),
        grid=(s // ts,),
        in_specs=[pl.BlockSpec((ts, D_POOL), lambda i: (i, 0)),
                  pl.BlockSpec((POOL_HALO, D_POOL), lambda i: (jnp.maximum(i * per - 1, 0), 0)),
                  _full(w_pool.shape), _full((1, D_POOL))],
        out_specs=pl.BlockSpec((ts, D_POOL), lambda i: (i, 0)),
        scratch_shapes=[pltpu.VMEM((ts + POOL_HALO, D_POOL), F32), pltpu.VMEM((ts, D_POOL), F32)],
        compiler_params=_params(("parallel",)))(u, u, w_pool, pool_scale)


def _tri(kind):
    r = lax.broadcasted_iota(jnp.int32, (ATT_BLOCK, ATT_BLOCK), 0)
    c = lax.broadcasted_iota(jnp.int32, (ATT_BLOCK, ATT_BLOCK), 1)
    return jnp.where(r >= c if kind == "suffix" else r <= c, 1.0, 0.0).astype(BF16)


def _causal_mask():
    r = lax.broadcasted_iota(jnp.int32, (ATT_BLOCK, ATT_BLOCK), 0)
    c = lax.broadcasted_iota(jnp.int32, (ATT_BLOCK, ATT_BLOCK), 1)
    return c < r


def _softplus(z, with_sigmoid=False):
    ope = 1.0 + jnp.exp(jnp.minimum(z, 80.0))
    sp = jnp.maximum(z, jnp.log(ope))
    if with_sigmoid:
        return sp, 1.0 - 1.0 / ope
    return sp


def _attn_fwd(q, k, v):
    s = q.shape[0]
    tb = ATT_BLOCK
    nq = s // tb

    def body(q_ref, k_ref, v_ref, o_ref, t_ref, *bufs):
        i = pl.program_id(1)
        upper = _tri("suffix")
        causal = _causal_mask()
        lane = lax.broadcasted_iota(jnp.int32, (1, HEAD_PAIR), 1)
        first = lane < 64
        q2 = q_ref[...]
        zero = jnp.zeros_like(q2)
        qs = (jnp.where(first, q2, zero), jnp.where(first, zero, q2))

        def scores(j, cs, masked):
            kj = k_ref[pl.ds(pl.multiple_of(j * tb, tb), tb), :]
            new_cs, args = [], []
            for e in range(2):
                z = _dot_nt(qs[e], kj)
                sp = _softplus(z)
                if masked:
                    sp = jnp.where(causal, sp, 0.0)
                incl = _dot(sp.astype(BF16), upper)
                arg = z - incl - cs[e]
                if masked:
                    arg = jnp.where(causal, arg, NEG_BIG)
                args.append(arg)
                new_cs.append(cs[e] + incl[:, 0:1])
            return new_cs, args

        def weigh(j, args, o):
            vj = v_ref[pl.ds(pl.multiple_of(j * tb, tb), tb), :]
            vcat = jnp.concatenate([jnp.where(first, vj, zero), jnp.where(first, zero, vj)], axis=0)
            probs = [jnp.exp(arg).astype(BF16) for arg in args]
            return o + _dot(jnp.concatenate(probs, axis=1), vcat)

        z_buf, zc_buf, in_buf = [[[bufs[4 * kind + 2 * slot + e] for e in range(2)]
                                  for slot in range(2)] for kind in range(3)]
        pr_buf = [bufs[12], bufs[13]]

        @pl.when((pl.program_id(0) == 0) & (i == 0))
        def _():
            for b in bufs:
                b[...] = jnp.zeros_like(b)

        def block_rows(p):
            return pl.ds(pl.multiple_of(jnp.clip(i - 1 - p, 0, nq - 1) * tb, tb), tb)

        def trip(t, w, carry):
            r = 1 - w
            cs, o = list(carry[0:2]), carry[2]
            live3 = (t - 3 >= 0) & (t - 3 < i)
            vj = v_ref[block_rows(t - 3), :]
            vcat = jnp.concatenate([jnp.where(first, vj, zero), jnp.where(first, zero, vj)], axis=0)
            o = o + jnp.where(live3, _dot(pr_buf[r][...], vcat), 0.0)
            kj = k_ref[block_rows(t), :]
            for e in range(2):
                z_buf[w][e][...] = _dot_nt(qs[e], kj)
            for e in range(2):
                z = z_buf[r][e][...]
                zc_buf[w][e][...] = z
                in_buf[w][e][...] = _dot(_softplus(z).astype(BF16), upper)
            live2 = (t - 2 >= 0) & (t - 2 < i)
            for e in range(2):
                incl = in_buf[r][e][...]
                arg = zc_buf[r][e][...] - incl - jnp.where(live2, cs[e], -NEG_BIG)
                pr_buf[w][:, e * tb:(e + 1) * tb] = jnp.exp(arg).astype(BF16)
                cs[e] = jnp.where(live2, cs[e] + incl[:, 0:1], cs[e])
            return cs[0], cs[1], o

        col = jnp.zeros((tb, 1), F32)
        cs, args = scores(i, (col, col), True)
        carry = (cs[0], cs[1], weigh(i, args, jnp.zeros((tb, HEAD_PAIR), F32)))
        def four_trips(n, cr):
            for u in range(4):
                cr = trip(4 * n + u, u % 2, cr)
            return cr

        carry = lax.fori_loop(0, jnp.where(i > 0, (i + 6) // 4, 0), four_trips, carry)
        t_ref[:, 0:1] = carry[0]
        t_ref[:, 1:2] = carry[1]
        o_ref[...] = carry[2]

    score_buf = pltpu.VMEM((tb, tb), F32)
    return _call(
        body, name="attn_fwd",
        out_shape=[jax.ShapeDtypeStruct((s, D_ATTN), F32),
                   jax.ShapeDtypeStruct((4, s, 2), F32)],
        grid=(4, nq),
        in_specs=[pl.BlockSpec((tb, HEAD_PAIR), lambda h, i: (i, h)),
                  pl.BlockSpec((s, HEAD_PAIR), lambda h, i: (0, h)),
                  pl.BlockSpec((s, HEAD_PAIR), lambda h, i: (0, h))],
        out_specs=[pl.BlockSpec((tb, HEAD_PAIR), lambda h, i: (i, h)),
                   pl.BlockSpec((None, tb, 2), lambda h, i: (h, i, 0))],
        scratch_shapes=[score_buf] * 12 + [pltpu.VMEM((tb, 2 * tb), BF16)] * 2,
        compiler_params=_params(("arbitrary", "arbitrary")))(q, k, v)


def _mix_out(attn, mpool, x, attn_scale, w_out, g2, g3):
    s = x.shape[0]
    ts = 512

    def body(a_ref, p_ref, x_ref, as_ref, w_ref, g2_ref, g3_ref, ma_ref, mix_ref, x1_ref, h2_ref):
        ao = a_ref[...]
        ma = (ao * _rstd(ao) * as_ref[...]).astype(BF16)
        ma_ref[...] = ma
        mix = _dot(p_ref[...], w_ref[0:D_POOL, :]) + _dot(ma, w_ref[D_POOL:, :])
        mix_ref[...] = mix
        x1 = x_ref[...] + mix * _rstd(mix) * g2_ref[...]
        x1_ref[...] = x1
        h2_ref[...] = (x1 * _rstd(x1) * g3_ref[...]).astype(BF16)

    row = lambda w: pl.BlockSpec((ts, w), lambda i: (i, 0))
    return _call(
        body, name="mix_out",
        out_shape=[jax.ShapeDtypeStruct((s, D_ATTN), BF16), jax.ShapeDtypeStruct((s, D_MODEL), F32),
                   jax.ShapeDtypeStruct((s, D_MODEL), F32), jax.ShapeDtypeStruct((s, D_MODEL), BF16)],
        grid=(s // ts,),
        in_specs=[row(D_ATTN), row(D_POOL), row(D_MODEL), _full((1, D_ATTN)),
                  _full((D_MODEL, D_MODEL)), _full((1, D_MODEL)), _full((1, D_MODEL))],
        out_specs=[row(D_ATTN), row(D_MODEL), row(D_MODEL), row(D_MODEL)],
        compiler_params=_params(("parallel",)))(attn, mpool, x, attn_scale, w_out, g2, g3)


def _conv_rows(ext_ref, cw, cb, ts):
    y = cb + cw[0:1, :] * ext_ref[CONV_HALO - 2:CONV_HALO - 2 + ts, :]
    y = y + cw[1:2, :] * ext_ref[CONV_HALO - 1:CONV_HALO - 1 + ts, :]
    return y + cw[2:3, :] * ext_ref[CONV_HALO:CONV_HALO + ts, :]


def _sigmoid(v):
    return 1.0 / (1.0 + jnp.exp(-v))


def _ffn_up(h2, w_up, conv_w, conv_b):
    s = h2.shape[0]
    ts = 256
    tn = FF_TILE

    def body(h_ref, wg_ref, wv_ref, cwg_ref, cwv_ref, cbg_ref, cbv_ref,
             ug_ref, uv_ref, f_ref, extg, extv):
        i = pl.program_id(1)

        @pl.when(i == 0)
        def _():
            extg[0:CONV_HALO, :] = jnp.zeros((CONV_HALO, tn), F32)
            extv[0:CONV_HALO, :] = jnp.zeros((CONV_HALO, tn), F32)

        h = h_ref[...]
        ug = _dot(h, wg_ref[...])
        uv = _dot(h, wv_ref[...])
        ug_ref[...] = ug
        uv_ref[...] = uv
        extg[CONV_HALO:, :] = ug
        extv[CONV_HALO:, :] = uv
        gate = _conv_rows(extg, cwg_ref[...], cbg_ref[...], ts)
        val = _conv_rows(extv, cwv_ref[...], cbv_ref[...], ts)
        f_ref[...] = (gate * _sigmoid(gate) * val).astype(BF16)
        extg[0:CONV_HALO, :] = extg[ts:ts + CONV_HALO, :]
        extv[0:CONV_HALO, :] = extv[ts:ts + CONV_HALO, :]

    out_blk = pl.BlockSpec((None, ts, tn), lambda n, i: (n, i, 0))
    act = jax.ShapeDtypeStruct((2, s, tn), F32)
    return _call(
        body, name="ffn_up",
        out_shape=[act, act, jax.ShapeDtypeStruct((2, s, tn), BF16)],
        grid=(2, s // ts),
        in_specs=[pl.BlockSpec((ts, D_MODEL), lambda n, i: (i, 0)),
                  pl.BlockSpec((None, D_MODEL, tn), lambda n, i: (n, 0, 0)),
                  pl.BlockSpec((None, D_MODEL, tn), lambda n, i: (n + 2, 0, 0)),
                  pl.BlockSpec((None, 3, tn), lambda n, i: (n, 0, 0)),
                  pl.BlockSpec((None, 3, tn), lambda n, i: (n + 2, 0, 0)),
                  pl.BlockSpec((None, 1, tn), lambda n, i: (n, 0, 0)),
                  pl.BlockSpec((None, 1, tn), lambda n, i: (n + 2, 0, 0))],
        out_specs=[out_blk, out_blk, out_blk],
        scratch_shapes=[pltpu.VMEM((ts + CONV_HALO, tn), F32), pltpu.VMEM((ts + CONV_HALO, tn), F32)],
        compiler_params=_params(("arbitrary", "arbitrary")))(
            h2, w_up, w_up, conv_w, conv_w, conv_b, conv_b)


def _ffn_down(f_in, w_down, x1, target, g4):
    s = x1.shape[0]
    ts = 512

    def body(f_ref, w_ref, x1_ref, t_ref, g_ref, df_ref, dy_ref, loss_ref, dg_ref):
        @pl.when(pl.program_id(0) == 0)
        def _():
            loss_ref[...] = jnp.zeros_like(loss_ref)
            dg_ref[...] = jnp.zeros_like(dg_ref)

        f = _dot(f_ref[0], w_ref[0:FF_TILE, :]) + _dot(f_ref[1], w_ref[FF_TILE:, :])
        rf = _rstd(f)
        fn = f * rf
        g = g_ref[...]
        err = (x1_ref[...] + fn * g) - t_ref[...]
        loss_ref[...] += 0.5 * jnp.sum(jnp.mean(err * err, axis=-1))
        dy = err * (1.0 / D_MODEL)
        dy_ref[...] = dy
        dg_ref[...] += jnp.sum(dy * fn, axis=0, keepdims=True)
        dfn = dy * g
        df_ref[...] = (rf * (dfn - fn * jnp.mean(dfn * fn, axis=-1, keepdims=True))).astype(BF16)

    row = pl.BlockSpec((ts, D_MODEL), lambda i: (i, 0))
    return _call(
        body, name="ffn_down",
        out_shape=[jax.ShapeDtypeStruct((s, D_MODEL), BF16), jax.ShapeDtypeStruct((s, D_MODEL), F32),
                   jax.ShapeDtypeStruct((8, 128), F32), jax.ShapeDtypeStruct((1, D_MODEL), F32)],
        grid=(s // ts,),
        in_specs=[pl.BlockSpec((2, ts, FF_TILE), lambda i: (0, i, 0)), _full((D_FF, D_MODEL)),
                  row, row, _full((1, D_MODEL))],
        out_specs=[row, row, _full((8, 128)), _full((1, D_MODEL))],
        compiler_params=_params(("arbitrary",)))(f_in, w_down, x1, target, g4)


def _tn_matmul(a, b, name, ts=512):
    na, s, ka = a.shape
    nb, _, nbc = b.shape
    steps = s // ts

    def body(a_ref, b_ref, o_ref, acc_ref):
        @pl.when(pl.program_id(2) == 0)
        def _():
            acc_ref[...] = jnp.zeros_like(acc_ref)

        acc_ref[...] += _dot_tn(a_ref[...].astype(BF16), b_ref[...].astype(BF16))

        @pl.when(pl.program_id(2) == steps - 1)
        def _():
            o_ref[...] = acc_ref[...].astype(BF16)

    return _call(
        body, name=name, out_shape=jax.ShapeDtypeStruct((na, nb, ka, nbc), BF16),
        grid=(na, nb, steps),
        in_specs=[pl.BlockSpec((None, ts, ka), lambda i, j, r: (i, r, 0)),
                  pl.BlockSpec((None, ts, nbc), lambda i, j, r: (j, r, 0))],
        out_specs=pl.BlockSpec((None, None, ka, nbc), lambda i, j, r: (i, j, 0, 0)),
        scratch_shapes=[pltpu.VMEM((ka, nbc), F32)],
        compiler_params=_params(("parallel", "parallel", "arbitrary")))(a, b)


def _ffn_bwd_act(df, w_down, upre_g, upre_v, conv_w, conv_b):
    s = df.shape[0]
    ts = 256
    tn = FF_TILE
    nr = s // ts
    per = ts // CONV_HALO

    def body(df_ref, wd_ref, ug_ref, uv_ref, hg_ref, hv_ref, cwg_ref, cwv_ref, cbg_ref, cbv_ref,
             dug_ref, duv_ref, dcwg_ref, dcwv_ref, dcbg_ref, dcbv_ref, extg, extv, dxg, dxv):
        i = pl.program_id(1)
        first_rows = i == nr - 1

        @pl.when(i == 0)
        def _():
            dxg[ts:, :] = jnp.zeros((CONV_HALO, tn), F32)
            dxv[ts:, :] = jnp.zeros((CONV_HALO, tn), F32)
            for r in (dcwg_ref, dcwv_ref, dcbg_ref, dcbv_ref):
                r[...] = jnp.zeros_like(r)

        extg[0:CONV_HALO, :] = jnp.where(first_rows, 0.0, hg_ref[...])
        extv[0:CONV_HALO, :] = jnp.where(first_rows, 0.0, hv_ref[...])
        extg[CONV_HALO:, :] = ug_ref[...]
        extv[CONV_HALO:, :] = uv_ref[...]
        cwg, cwv = cwg_ref[...], cwv_ref[...]
        gate = _conv_rows(extg, cwg, cbg_ref[...], ts)
        val = _conv_rows(extv, cwv, cbv_ref[...], ts)
        sg = _sigmoid(gate)
        dfin = _dot_nt(df_ref[...], wd_ref[...])
        dval = dfin * (gate * sg)
        dgate = dfin * val * (sg * (1.0 + gate * (1.0 - sg)))

        def conv_bwd(dact, ext, dx, cw, dcw_ref, dcb_ref, du_ref):
            dx[0:ts, :] = dact
            dcb_ref[...] += jnp.sum(dact, axis=0, keepdims=True)
            for kk in range(3):
                lo = CONV_HALO - 2 + kk
                dcw_ref[kk:kk + 1, :] += jnp.sum(dact * ext[lo:lo + ts, :], axis=0, keepdims=True)
            du = cw[2:3, :] * dact + cw[1:2, :] * dx[1:1 + ts, :] + cw[0:1, :] * dx[2:2 + ts, :]
            du_ref[...] = du.astype(BF16)
            dx[ts:, :] = dx[0:CONV_HALO, :]

        conv_bwd(dgate, extg, dxg, cwg, dcwg_ref, dcbg_ref, dug_ref)
        conv_bwd(dval, extv, dxv, cwv, dcwv_ref, dcbv_ref, duv_ref)

    rows = lambda n, i: (n, nr - 1 - i, 0)
    halo = lambda n, i: (n, jnp.maximum((nr - 1 - i) * per - 1, 0), 0)
    act_blk = pl.BlockSpec((None, ts, tn), rows)
    halo_blk = pl.BlockSpec((None, CONV_HALO, tn), halo)
    cw_blk = lambda off: pl.BlockSpec((None, 3, tn), lambda n, i: (n + off, 0, 0))
    cb_blk = lambda off: pl.BlockSpec((None, 1, tn), lambda n, i: (n + off, 0, 0))
    acc_w = pl.BlockSpec((None, 3, tn), lambda n, i: (n, 0, 0))
    acc_b = pl.BlockSpec((None, 1, tn), lambda n, i: (n, 0, 0))
    dact = jax.ShapeDtypeStruct((2, s, tn), BF16)
    return _call(
        body, name="ffn_bwd_act",
        out_shape=[dact, dact, jax.ShapeDtypeStruct((2, 3, tn), F32), jax.ShapeDtypeStruct((2, 3, tn), F32),
                   jax.ShapeDtypeStruct((2, 1, tn), F32), jax.ShapeDtypeStruct((2, 1, tn), F32)],
        grid=(2, nr),
        in_specs=[pl.BlockSpec((ts, D_MODEL), lambda n, i: (nr - 1 - i, 0)),
                  pl.BlockSpec((tn, D_MODEL), lambda n, i: (n, 0)),
                  act_blk, act_blk, halo_blk, halo_blk,
                  cw_blk(0), cw_blk(2), cb_blk(0), cb_blk(2)],
        out_specs=[act_blk, act_blk, acc_w, acc_w, acc_b, acc_b],
        scratch_shapes=[pltpu.VMEM((ts + CONV_HALO, tn), F32)] * 4,
        compiler_params=_params(("arbitrary", "arbitrary")))(
            df, w_down, upre_g, upre_v, upre_g, upre_v, conv_w, conv_w, conv_b, conv_b)


def _ffn_bwd_in(dug, duv, w_up, x1, dy, mix, g3, g2):
    s = x1.shape[0]
    ts = 256

    def body(dg_ref, dv_ref, w_ref, x1_ref, dy_ref, mix_ref, g3_ref, g2_ref,
             dx1_ref, dmix_ref, dg3_ref, dg2_ref):
        @pl.when(pl.program_id(0) == 0)
        def _():
            dg3_ref[...] = jnp.zeros_like(dg3_ref)
            dg2_ref[...] = jnp.zeros_like(dg2_ref)

        dh = _dot_nt(dg_ref[0], w_ref[0]) + _dot_nt(dg_ref[1], w_ref[1])
        dh = dh + _dot_nt(dv_ref[0], w_ref[2]) + _dot_nt(dv_ref[1], w_ref[3])
        x1 = x1_ref[...]
        r3 = _rstd(x1)
        xn = x1 * r3
        dg3_ref[...] += jnp.sum(dh * xn, axis=0, keepdims=True)
        dxn = dh * g3_ref[...]
        dx1 = dy_ref[...] + r3 * (dxn - xn * jnp.mean(dxn * xn, axis=-1, keepdims=True))
        dx1_ref[...] = dx1
        mix = mix_ref[...]
        rm = _rstd(mix)
        mn = mix * rm
        dg2_ref[...] += jnp.sum(dx1 * mn, axis=0, keepdims=True)
        dmn = dx1 * g2_ref[...]
        dmix_ref[...] = (rm * (dmn - mn * jnp.mean(dmn * mn, axis=-1, keepdims=True))).astype(BF16)

    row = pl.BlockSpec((ts, D_MODEL), lambda i: (i, 0))
    act = pl.BlockSpec((2, ts, FF_TILE), lambda i: (0, i, 0))
    vec = _full((1, D_MODEL))
    return _call(
        body, name="ffn_bwd_in",
        out_shape=[jax.ShapeDtypeStruct((s, D_MODEL), F32), jax.ShapeDtypeStruct((s, D_MODEL), BF16),
                   jax.ShapeDtypeStruct((1, D_MODEL), F32), jax.ShapeDtypeStruct((1, D_MODEL), F32)],
        grid=(s // ts,),
        in_specs=[act, act, _full(w_up.shape), row, row, row, vec, vec],
        out_specs=[row, row, vec, vec],
        compiler_params=_params(("arbitrary",), vmem_mb=56))(dug, duv, w_up, x1, dy, mix, g3, g2)


def _mix_bwd(dmix, w_out, attn, attn_scale):
    s = dmix.shape[0]
    ts = 512

    def body(dm_ref, w_ref, a_ref, as_ref, dp_ref, do_ref, das_ref):
        @pl.when(pl.program_id(0) == 0)
        def _():
            das_ref[...] = jnp.zeros_like(das_ref)

        dm = dm_ref[...]
        dp_ref[...] = _dot_nt(dm, w_ref[0:D_POOL, :])
        da = _dot_nt(dm, w_ref[D_POOL:, :])
        ao = a_ref[...]
        ra = _rstd(ao)
        an = ao * ra
        das_ref[...] += jnp.sum(da * an, axis=0, keepdims=True)
        dan = da * as_ref[...]
        do_ref[...] = (ra * (dan - an * jnp.mean(dan * an, axis=-1, keepdims=True))).astype(BF16)

    row = lambda w: pl.BlockSpec((ts, w), lambda i: (i, 0))
    return _call(
        body, name="mix_bwd",
        out_shape=[jax.ShapeDtypeStruct((s, D_POOL), F32), jax.ShapeDtypeStruct((s, D_ATTN), BF16),
                   jax.ShapeDtypeStruct((1, D_ATTN), F32)],
        grid=(s // ts,),
        in_specs=[row(D_MODEL), _full((D_MODEL, D_MODEL)), row(D_ATTN), _full((1, D_ATTN))],
        out_specs=[row(D_POOL), row(D_ATTN), _full((1, D_ATTN))],
        compiler_params=_params(("arbitrary",)))(dmix, w_out, attn, attn_scale)


def _attn_bwd(q, k, v, do, totals):
    s = q.shape[0]
    tb = ATT_BLOCK
    nq = s // tb

    def body(q_ref, do_ref, t_ref, k_hbm, v_hbm, dq_ref, dk_hbm, dv_hbm,
             k_scr, v_scr, dkt_acc, dvt_acc, stage, *bufs):
        hp = pl.program_id(0)
        i = pl.program_id(1)
        lanes = pl.ds(pl.multiple_of(hp * HEAD_PAIR, HEAD_PAIR), HEAD_PAIR)

        @pl.when(i == 0)
        def _():
            pltpu.sync_copy(k_hbm.at[:, lanes], k_scr)
            pltpu.sync_copy(v_hbm.at[:, lanes], v_scr)
            dkt_acc[...] = jnp.zeros_like(dkt_acc)
            dvt_acc[...] = jnp.zeros_like(dvt_acc)

        upper = _tri("suffix")
        lower = _tri("prefix")
        causal = _causal_mask()
        lane = lax.broadcasted_iota(jnp.int32, (1, HEAD_PAIR), 1)
        first = lane < 64
        q2 = q_ref[...]
        do2 = do_ref[...]
        zero = jnp.zeros_like(q2)
        qs = (jnp.where(first, q2, zero), jnp.where(first, zero, q2))
        dos = (jnp.where(first, do2, zero), jnp.where(first, zero, do2))
        qcat_t = jnp.concatenate(qs, axis=0).astype(F32).T.astype(BF16)
        docat_t = jnp.concatenate(dos, axis=0).astype(F32).T.astype(BF16)
        tots = (t_ref[:, 0:1], t_ref[:, 1:2])

        z_buf, zc_buf, sg_buf, sg2_buf, in_buf, da_buf, dw_buf, pre_buf = [
            [[bufs[4 * kind + 2 * slot + e] for e in range(2)] for slot in range(2)]
            for kind in range(8)]
        pr_buf, dzr_buf, dzc_buf = bufs[32:34], bufs[34:36], bufs[36:38]

        for e in range(2):
            z_buf[1][e][...] = jnp.full((tb, tb), NEG_BIG, F32)
            zc_buf[1][e][...] = jnp.full((tb, tb), NEG_BIG, F32)
            for buf in (sg_buf, sg2_buf, in_buf, da_buf, dw_buf, pre_buf):
                buf[1][e][...] = jnp.zeros((tb, tb), F32)
        for buf in (pr_buf, dzr_buf, dzc_buf):
            buf[1][...] = jnp.zeros_like(buf[1])

        def rows(p):
            return pl.ds(pl.multiple_of(jnp.clip(p, 0, nq - 1) * tb, tb), tb)

        def split_heads(block):
            return jnp.concatenate([jnp.where(first, block, zero), jnp.where(first, zero, block)], axis=0)

        def trip(t, w, carry):
            r = 1 - w
            cs, cps, dq = list(carry[0:2]), list(carry[2:4]), carry[4]
            live4 = (t - 4 >= 0) & (t - 4 < i)
            dq = dq + jnp.where(live4, _dot(dzc_buf[r][...], split_heads(k_scr[rows(t - 4), :])), 0.0)
            dkt_acc[:, rows(t - 4)] += jnp.where(live4, _dot(qcat_t, dzr_buf[r][...]), 0.0)
            dvt_acc[:, rows(t - 3)] += _dot(docat_t, pr_buf[r][...])
            kj = k_scr[rows(t), :]
            for e in range(2):
                z_buf[w][e][...] = _dot_nt(qs[e], kj)
            vj = v_scr[rows(t - 1), :]
            for e in range(2):
                z = z_buf[r][e][...]
                sp, sig = _softplus(z, True)
                zc_buf[w][e][...] = z
                sg_buf[w][e][...] = sig
                in_buf[w][e][...] = _dot(sp.astype(BF16), upper)
                da_buf[w][e][...] = _dot_nt(dos[e], vj)
            for e in range(2):
                incl = in_buf[r][e][...]
                cs[e] = cs[e] + incl[:, 0:1]
                off = jnp.where(t - 2 < i, tots[e] - cs[e], -NEG_BIG)
                a = jnp.exp(zc_buf[r][e][...] - incl - off)
                dw = a * da_buf[r][e][...]
                dw_buf[w][e][...] = dw
                sg2_buf[w][e][...] = sg_buf[r][e][...]
                pr_buf[w][e * tb:(e + 1) * tb, :] = a.astype(BF16)
                pre_buf[w][e][...] = _dot(dw.astype(BF16), lower)
            for e in range(2):
                pre = pre_buf[r][e][...] + cps[e]
                dzb = (dw_buf[r][e][...] - sg2_buf[r][e][...] * pre).astype(BF16)
                cps[e] = pre[:, tb - 1:tb]
                dzr_buf[w][e * tb:(e + 1) * tb, :] = dzb
                dzc_buf[w][:, e * tb:(e + 1) * tb] = dzb
            return cs[0], cs[1], cps[0], cps[1], dq

        col = jnp.zeros((tb, 1), F32)
        carry = (col, col, col, col, jnp.zeros((tb, HEAD_PAIR), F32))
        def four_trips(n, cr):
            for u in range(4):
                cr = trip(4 * n + u, u % 2, cr)
            return cr

        carry = lax.fori_loop(0, jnp.where(i > 0, (i + 7) // 4, 0), four_trips, carry)

        cps, dq = carry[2:4], carry[4]
        kj = k_scr[rows(i), :]
        vj = v_scr[rows(i), :]
        dzs, probs = [], []
        for e in range(2):
            z = _dot_nt(qs[e], kj)
            sp, sig = _softplus(z, True)
            incl = _dot(jnp.where(causal, sp, 0.0).astype(BF16), upper)
            a = jnp.where(causal, jnp.exp(z - incl), 0.0)
            dw = a * _dot_nt(dos[e], vj)
            pre = _dot(dw.astype(BF16), lower) + cps[e]
            dzs.append(jnp.where(causal, dw - sig * pre, 0.0).astype(BF16))
            probs.append(a.astype(BF16))
        dq = dq + _dot(jnp.concatenate(dzs, axis=1), split_heads(kj))
        dkt_acc[:, rows(i)] += _dot(qcat_t, jnp.concatenate(dzs, axis=0))
        dvt_acc[:, rows(i)] += _dot(docat_t, jnp.concatenate(probs, axis=0))
        dq_ref[...] = (dq * Q_SCALE).astype(BF16)

        @pl.when(i == nq - 1)
        def _():
            for acc, dst in ((dkt_acc, dk_hbm), (dvt_acc, dv_hbm)):
                def flip(n, _, acc=acc):
                    at = pl.ds(pl.multiple_of(n * tb, tb), tb)
                    stage[at, :] = acc[:, at].T
                    return 0
                lax.fori_loop(0, nq, flip, 0)
                pltpu.sync_copy(stage, dst.at[:, lanes])

    blk = pl.BlockSpec((tb, HEAD_PAIR), lambda h, i: (i, h))
    grad = jax.ShapeDtypeStruct((s, D_ATTN), F32)
    return _call(
        body, name="attn_bwd",
        out_shape=[jax.ShapeDtypeStruct((s, D_ATTN), BF16), grad, grad],
        grid=(4, nq),
        in_specs=[blk, blk, pl.BlockSpec((None, tb, 2), lambda h, i: (h, i, 0)), ANY, ANY],
        out_specs=[blk, ANY, ANY],
        scratch_shapes=[pltpu.VMEM((s, HEAD_PAIR), BF16), pltpu.VMEM((s, HEAD_PAIR), BF16),
                        pltpu.VMEM((HEAD_PAIR, s), F32), pltpu.VMEM((HEAD_PAIR, s), F32),
                        pltpu.VMEM((s, HEAD_PAIR), F32)]
        + [pltpu.VMEM((tb, tb), F32)] * 32
        + [pltpu.VMEM((2 * tb, tb), BF16)] * 4 + [pltpu.VMEM((tb, 2 * tb), BF16)] * 2,
        compiler_params=_params(("arbitrary", "arbitrary"), vmem_mb=60))(q, do, totals, k, v)


def _pool_bwd(u, dmp, w_pool, pool_scale):
    s = u.shape[0]
    ts = 512
    nr = s // ts
    per = ts // POOL_HALO

    def body(u_ref, halo_ref, dm_ref, wp_ref, ps_ref, du_ref, dwp_ref, dps_ref, ext_ref, y_ref, dext_ref):
        i = pl.program_id(0)
        rb = nr - 1 - i

        @pl.when(i == 0)
        def _():
            dext_ref[ts:, :] = jnp.zeros((POOL_HALO, D_POOL), F32)
            dwp_ref[...] = jnp.zeros_like(dwp_ref)
            dps_ref[...] = jnp.zeros_like(dps_ref)

        ext_ref[0:POOL_HALO, :] = jnp.where(rb > 0, halo_ref[...], 0.0)
        ext_ref[POOL_HALO:, :] = u_ref[...]
        ps, cnts = [], []
        for g, window in enumerate(POOL_WINDOWS):
            p, cnt = _pool_means(ext_ref, g, window, ts, rb * ts)
            ps.append(p.astype(BF16))
            cnts.append(cnt)
            y_ref[:, g * POOL_GROUP:(g + 1) * POOL_GROUP] = _dot(ps[g], wp_ref[g].astype(BF16))
        y = y_ref[...]
        r = _rstd(y)
        yn = y * r
        dm = dm_ref[...]
        dps_ref[...] += jnp.sum(dm * yn, axis=0, keepdims=True)
        dn = dm * ps_ref[...]
        dy = r * (dn - yn * jnp.mean(dn * yn, axis=-1, keepdims=True))
        for g, window in enumerate(POOL_WINDOWS):
            cols = slice(g * POOL_GROUP, (g + 1) * POOL_GROUP)
            dyg = dy[:, cols].astype(BF16)
            dwp_ref[g] += _dot_tn(ps[g], dyg)
            dp = _dot_nt(dyg, wp_ref[g].astype(BF16))
            dext_ref[0:ts, cols] = dp / cnts[g]
            acc = dext_ref[0:ts, cols]
            for d in range(1, window):
                acc = acc + dext_ref[d:d + ts, cols]
            du_ref[:, cols] = (acc - dp).astype(BF16)
        dext_ref[ts:, :] = dext_ref[0:POOL_HALO, :]

    rows = pl.BlockSpec((ts, D_POOL), lambda i: (nr - 1 - i, 0))
    return _call(
        body, name="pool_bwd",
        out_shape=[jax.ShapeDtypeStruct((s, D_POOL), BF16), jax.ShapeDtypeStruct(w_pool.shape, F32),
                   jax.ShapeDtypeStruct((1, D_POOL), F32)],
        grid=(nr,),
        in_specs=[rows,
                  pl.BlockSpec((POOL_HALO, D_POOL), lambda i: (jnp.maximum((nr - 1 - i) * per - 1, 0), 0)),
                  rows, _full(w_pool.shape), _full((1, D_POOL))],
        out_specs=[rows, _full(w_pool.shape), _full((1, D_POOL))],
        scratch_shapes=[pltpu.VMEM((ts + POOL_HALO, D_POOL), F32), pltpu.VMEM((ts, D_POOL), F32),
                        pltpu.VMEM((ts + POOL_HALO, D_POOL), F32)],
        compiler_params=_params(("arbitrary",)))(u, u, dmp, w_pool, pool_scale)


def _in_proj_bwd(du, dq, dk, dv, w_in, x, dx1, g1):
    s = x.shape[0]
    ts = 512

    def body(du_ref, dq_ref, dk_ref, dv_ref, w_ref, x_ref, dx1_ref, g_ref, gx_ref, dg_ref):
        @pl.when(pl.program_id(0) == 0)
        def _():
            dg_ref[...] = jnp.zeros_like(dg_ref)

        dh = _dot_nt(du_ref[...], w_ref[0]) + _dot_nt(dq_ref[...], w_ref[1])
        dh = dh + _dot_nt(dk_ref[...].astype(BF16), w_ref[2]) + _dot_nt(dv_ref[...].astype(BF16), w_ref[3])
        xv = x_ref[...]
        r = _rstd(xv)
        xn = xv * r
        dg_ref[...] += jnp.sum(dh * xn, axis=0, keepdims=True)
        dxn = dh * g_ref[...]
        gx_ref[...] = dx1_ref[...] + r * (dxn - xn * jnp.mean(dxn * xn, axis=-1, keepdims=True))

    row = lambda w: pl.BlockSpec((ts, w), lambda i: (i, 0))
    return _call(
        body, name="in_proj_bwd",
        out_shape=[jax.ShapeDtypeStruct((s, D_MODEL), F32), jax.ShapeDtypeStruct((1, D_MODEL), F32)],
        grid=(s // ts,),
        in_specs=[row(D_POOL)] * 4 + [_full(w_in.shape), row(D_MODEL), row(D_MODEL), _full((1, D_MODEL))],
        out_specs=[row(D_MODEL), _full((1, D_MODEL))],
        compiler_params=_params(("arbitrary",)))(du, dq, dk, dv, w_in, x, dx1, g1)


_SMALL = ("norm_mix_pre", "w_pool", "pool_scale", "attn_scale", "norm_mix_post",
          "norm_ffn_pre", "conv_b", "norm_ffn_post")
_SMALL_SIZE = {"norm_mix_pre": 1024, "w_pool": 65536, "pool_scale": 512, "attn_scale": 512,
               "norm_mix_post": 1024, "norm_ffn_pre": 1024, "conv_b": 5632, "norm_ffn_post": 1024}
_SMALL_ROWS = 600
_CONVW_ROWS = 132
_PACK_ROWS = _SMALL_ROWS + _CONVW_ROWS + 4


def _pack_small(parts):
    flat = jnp.concatenate([parts[n].reshape(-1) for n in _SMALL])
    flat = jnp.pad(flat, (0, _SMALL_ROWS * 128 - flat.shape[0]))
    return flat.reshape(_SMALL_ROWS, 128)


def _unpack_small(packed, like):
    flat = packed.reshape(-1)
    out, off = {}, 0
    for n in _SMALL:
        out[n] = flat[off:off + _SMALL_SIZE[n]].reshape(like[n].shape)
        off += _SMALL_SIZE[n]
    return out


def kernel(x, norm_mix_pre, w_in, w_pool, pool_scale, attn_scale, w_out, norm_mix_post, norm_ffn_pre, w_up, conv_w, conv_b, w_down, norm_ffn_post, loss_target, m_norm_mix_pre, m_w_in, m_w_pool, m_pool_scale, m_attn_scale, m_w_out, m_norm_mix_post, m_norm_ffn_pre, m_w_up, m_conv_w, m_conv_b, m_w_down, m_norm_ffn_post, v_norm_mix_pre, v_w_in, v_w_pool, v_pool_scale, v_attn_scale, v_w_out, v_norm_mix_post, v_norm_ffn_pre, v_w_up, v_conv_w, v_conv_b, v_w_down, v_norm_ffn_post):
    weights = dict(norm_mix_pre=norm_mix_pre, w_in=w_in, w_pool=w_pool, pool_scale=pool_scale,
                   attn_scale=attn_scale, w_out=w_out, norm_mix_post=norm_mix_post,
                   norm_ffn_pre=norm_ffn_pre, w_up=w_up, conv_w=conv_w, conv_b=conv_b,
                   w_down=w_down, norm_ffn_post=norm_ffn_post)
    mom1 = dict(norm_mix_pre=m_norm_mix_pre, w_in=m_w_in, w_pool=m_w_pool, pool_scale=m_pool_scale,
                attn_scale=m_attn_scale, w_out=m_w_out, norm_mix_post=m_norm_mix_post,
                norm_ffn_pre=m_norm_ffn_pre, w_up=m_w_up, conv_w=m_conv_w, conv_b=m_conv_b,
                w_down=m_w_down, norm_ffn_post=m_norm_ffn_post)
    mom2 = dict(norm_mix_pre=v_norm_mix_pre, w_in=v_w_in, w_pool=v_w_pool, pool_scale=v_pool_scale,
                attn_scale=v_attn_scale, w_out=v_w_out, norm_mix_post=v_norm_mix_post,
                norm_ffn_pre=v_norm_ffn_pre, w_up=v_w_up, conv_w=v_conv_w, conv_b=v_conv_b,
                w_down=v_w_down, norm_ffn_post=v_norm_ffn_post)
    order = list(weights)

    xs = x[0]
    target = loss_target[0]
    wp = w_pool[0]
    shard = lax.axis_index("x") * 2 + lax.axis_index("y")

    shards = [_cast_bf16(w_in[0], "cast_w_in"), _cast_bf16(w_out[0], "cast_w_out"),
              _cast_bf16(w_up[0], "cast_w_up"), _cast_bf16(w_down[0], "cast_w_down"), conv_w[0]]
    win_g, wout_g, wup_g, wdown_g, convw_g = _gather_shards(shards)
    wout_f = wout_g.reshape(D_MODEL, D_MODEL)
    wdown_f = wdown_g.reshape(D_FF, D_MODEL)
    convb_g = conv_b[0].reshape(N_SHARD, 1, FF_TILE)

    u, q, k, v, h1 = _in_proj(xs, norm_mix_pre, win_g)
    mpool = _pool_fwd(u, wp, pool_scale)
    attn, totals = _attn_fwd(q, k, v)
    mattn, mix, x1, h2 = _mix_out(attn, mpool, xs, attn_scale, wout_f, norm_mix_post, norm_ffn_pre)
    upre_g, upre_v, f_in = _ffn_up(h2, wup_g, convw_g, convb_g)
    df, dy, loss_tile, d_post = _ffn_down(f_in, wdown_f, x1, target, norm_ffn_post)

    d_wdown = _tn_matmul(f_in, df[None], "dw_down")
    dug, duv, dcw_g, dcw_v, dcb_g, dcb_v = _ffn_bwd_act(df, wdown_f, upre_g, upre_v, convw_g, convb_g)
    d_wup = jnp.concatenate([_tn_matmul(h2[None], dug, "dw_up_gate")[0],
                             _tn_matmul(h2[None], duv, "dw_up_value")[0]], axis=0)
    dx1, dmix, d_ffn_pre, d_mix_post = _ffn_bwd_in(dug, duv, wup_g, x1, dy, mix, norm_ffn_pre, norm_mix_post)
    d_wout = jnp.concatenate([_tn_matmul(mpool[None], dmix[None], "dw_out_pool")[0, 0],
                              _tn_matmul(mattn[None], dmix[None], "dw_out_attn")[0, 0]], axis=0)
    dmp, do, d_attn_scale = _mix_bwd(dmix, wout_f, attn, attn_scale)
    dq, dk, dv = _attn_bwd(q, k, v, do, totals)
    du, d_wpool, d_pool_scale = _pool_bwd(u, dmp, wp, pool_scale)
    d_win = jnp.stack([_tn_matmul(h1[None], t[None], "dw_in_%d" % n)[0, 0]
                       for n, t in enumerate((du, dq, dk, dv))])
    grad_x, d_mix_pre = _in_proj_bwd(du, dq, dk, dv, win_g, xs, dx1, norm_mix_pre)

    d_convw = jnp.concatenate([dcw_g, dcw_v], axis=0)
    d_convb = jnp.concatenate([dcb_g, dcb_v], axis=0).reshape(1, 2 * D_FF)
    small_parts = dict(norm_mix_pre=d_mix_pre, w_pool=d_wpool, pool_scale=d_pool_scale,
                       attn_scale=d_attn_scale, norm_mix_post=d_mix_post, norm_ffn_pre=d_ffn_pre,
                       conv_b=d_convb, norm_ffn_post=d_post)
    packed = jnp.concatenate([_pack_small(small_parts), d_convw.reshape(_CONVW_ROWS, 128),
                              loss_tile[0:4]], axis=0)
    big = [d_win, d_wout.reshape(N_SHARD, D_MODEL // N_SHARD, D_MODEL), d_wup,
           d_wdown.reshape(N_SHARD, D_FF // N_SHARD, D_MODEL)]
    recv, gathered = _scatter_grads(big, packed)
    quarter = [_sum_slots(r, (3, 0, 1, 2), "sum_chips_%d" % n) for n, r in enumerate(recv)]
    sibling = _swap_with_sibling(quarter)
    small_sum = _sum_slots(gathered, tuple(range(8)), "sum_small")

    results = {}
    for n, name in enumerate(("w_in", "w_out", "w_up", "w_down")):
        res = _adamw([quarter[n], sibling[n]], weights[name][0], mom1[name][0], mom2[name][0],
                     "adamw_" + name)
        results[name] = [t[None] for t in res]
    g_convw = lax.dynamic_slice_in_dim(
        small_sum[_SMALL_ROWS:_SMALL_ROWS + _CONVW_ROWS].reshape(N_SHARD, 3, FF_TILE), shard, 1, axis=0)[0]
    convw_pad = lambda t: jnp.pad(t, ((0, 5), (0, 0)))
    res = _adamw([convw_pad(g_convw)], convw_pad(conv_w[0]), convw_pad(m_conv_w[0]),
                 convw_pad(v_conv_w[0]), "adamw_conv_w")
    results["conv_w"] = [t[:3][None] for t in res]
    pack_w = _pack_small(weights)
    pack_m = _pack_small(mom1)
    pack_v = _pack_small(mom2)
    res = _adamw([small_sum[:_SMALL_ROWS]], pack_w, pack_m, pack_v, "adamw_small")
    unpacked = [_unpack_small(t, weights) for t in res]
    for name in _SMALL:
        results[name] = [t[name] for t in unpacked]

    loss = small_sum[_SMALL_ROWS + _CONVW_ROWS, 0]
    outs = [loss, grad_x[None]]
    for slot in range(4):
        outs.extend(results[name][slot] for name in order)
    return tuple(outs)
```

```python
import functools

import jax
import jax.numpy as jnp
from jax import lax
from jax.experimental import pallas as pl
from jax.experimental.pallas import tpu as pltpu

F32 = jnp.float32
BF16 = jnp.bfloat16

D_MODEL = 1024
D_POOL = 512
D_ATTN = 512
POOL_WINDOWS = (2, 4, 8, 16)
POOL_GROUP = 128
POOL_HALO = 16
CONV_HALO = 8
D_FF = 2816
FF_TILE = 1408
N_SHARD = 4
EPS = 1e-6
Q_SCALE = 0.125
ATT_BLOCK = 256
HEAD_PAIR = 128
MIB = 1 << 20
NEG_BIG = -1e30

ADAM_LR = 0.001
ADAM_B1 = 0.9
ADAM_B2 = 0.999
ADAM_EPS = 1e-08
ADAM_WD = 0.01
ADAM_STEP = 10

NT_DIMS = (((1,), (1,)), ((), ()))
TN_DIMS = (((0,), (0,)), ((), ()))
MESH = pl.DeviceIdType.MESH
ANY = pl.BlockSpec(memory_space=pl.ANY)


def _call(body, **kw):
    return pl.pallas_call(body, **kw)


def _params(sem=None, vmem_mb=48):
    return pltpu.CompilerParams(dimension_semantics=sem, vmem_limit_bytes=vmem_mb * MIB)


def _rstd(v):
    return lax.rsqrt(jnp.mean(v * v, axis=-1, keepdims=True) + EPS)


def _dot(a, b):
    return jnp.dot(a, b, preferred_element_type=F32)


def _dot_nt(a, b):
    return lax.dot_general(a, b, NT_DIMS, preferred_element_type=F32)


def _dot_tn(a, b):
    return lax.dot_general(a, b, TN_DIMS, preferred_element_type=F32)


def _row_tile(rows, cap):
    t = min(rows, cap)
    t -= t % 8
    while rows % t:
        t -= 8
    return t


def _full(shape):
    nd = len(shape)
    return pl.BlockSpec(shape, lambda *_: (0,) * nd)


def _chip_peers():
    x, y, c = lax.axis_index("x"), lax.axis_index("y"), lax.axis_index("c")
    return x, y, c, [(1 - x, y), (x, 1 - y), (1 - x, 1 - y)]


def _cast_bf16(a, name):
    def body(a_ref, o_ref):
        o_ref[...] = a_ref[...].astype(BF16)

    return _call(body, name=name, out_shape=jax.ShapeDtypeStruct(a.shape, BF16),
                 grid=(1,), in_specs=[_full(a.shape)], out_specs=_full(a.shape),
                 compiler_params=_params(("arbitrary",)))(a)


def _gather_shards(shards):
    n = len(shards)

    def body(*refs):
        ins, outs = refs[:n], refs[n:2 * n]
        send, recv, loc = refs[2 * n:]
        x, y, c, chips = _chip_peers()
        b = 2 * x + y
        local = [pltpu.make_async_copy(ins[t], outs[t].at[b], loc.at[t]) for t in range(n)]
        for cp in local:
            cp.start()
        remote = []
        for t in range(n):
            for k, (px, py) in enumerate(chips):
                remote.append(pltpu.make_async_remote_copy(
                    src_ref=ins[t], dst_ref=outs[t].at[b],
                    send_sem=send.at[3 * t + k], recv_sem=recv.at[3 * t + k],
                    device_id=(px, py, c), device_id_type=MESH))
        for cp in remote:
            cp.start()
        for cp in remote:
            cp.wait()
        for cp in local:
            cp.wait()

    return _call(
        body, name="gather_weights",
        out_shape=[jax.ShapeDtypeStruct((N_SHARD,) + s.shape, s.dtype) for s in shards],
        in_specs=[ANY] * n, out_specs=[ANY] * n,
        scratch_shapes=[pltpu.SemaphoreType.DMA((3 * n,)), pltpu.SemaphoreType.DMA((3 * n,)),
                        pltpu.SemaphoreType.DMA((n,))],
    )(*shards)


def _scatter_grads(grads, small):
    n = len(grads)

    def body(*refs):
        ins, small_in = refs[:n], refs[n]
        outs, small_out = refs[n + 1:2 * n + 1], refs[2 * n + 1]
        send, recv, loc, ssend, srecv = refs[2 * n + 2:]
        x, y, c, chips = _chip_peers()
        b = 2 * x + y
        me = 4 * x + 2 * y + c
        local = [pltpu.make_async_copy(ins[t].at[b], outs[t].at[3], loc.at[t]) for t in range(n)]
        local.append(pltpu.make_async_copy(small_in, small_out.at[me], loc.at[n]))
        for cp in local:
            cp.start()
        remote = []
        for t in range(n):
            for k, (px, py) in enumerate(chips):
                remote.append(pltpu.make_async_remote_copy(
                    src_ref=ins[t].at[2 * px + py], dst_ref=outs[t].at[k],
                    send_sem=send.at[3 * t + k], recv_sem=recv.at[3 * t + k],
                    device_id=(px, py, c), device_id_type=MESH))
        for r in range(1, 8):
            px = 1 - x if r & 4 else x
            py = 1 - y if r & 2 else y
            pc = 1 - c if r & 1 else c
            remote.append(pltpu.make_async_remote_copy(
                src_ref=small_in, dst_ref=small_out.at[me],
                send_sem=ssend.at[r - 1], recv_sem=srecv.at[r - 1],
                device_id=(px, py, pc), device_id_type=MESH))
        for cp in remote:
            cp.start()
        for cp in remote:
            cp.wait()
        for cp in local:
            cp.wait()

    out_shape = [jax.ShapeDtypeStruct(g.shape, g.dtype) for g in grads]
    out_shape.append(jax.ShapeDtypeStruct((8,) + small.shape, small.dtype))
    res = _call(
        body, name="scatter_grads", out_shape=out_shape,
        in_specs=[ANY] * (n + 1), out_specs=[ANY] * (n + 1),
        scratch_shapes=[pltpu.SemaphoreType.DMA((3 * n,)), pltpu.SemaphoreType.DMA((3 * n,)),
                        pltpu.SemaphoreType.DMA((n + 1,)),
                        pltpu.SemaphoreType.DMA((7,)), pltpu.SemaphoreType.DMA((7,))],
    )(*grads, small)
    return res[:n], res[n]


def _swap_with_sibling(parts):
    n = len(parts)

    def body(*refs):
        ins, outs = refs[:n], refs[n:2 * n]
        send, recv = refs[2 * n:]
        x, y, c = lax.axis_index("x"), lax.axis_index("y"), lax.axis_index("c")
        copies = [pltpu.make_async_remote_copy(
            src_ref=ins[t], dst_ref=outs[t], send_sem=send.at[t], recv_sem=recv.at[t],
            device_id=(x, y, 1 - c), device_id_type=MESH) for t in range(n)]
        for cp in copies:
            cp.start()
        for cp in copies:
            cp.wait()

    return _call(
        body, name="swap_sibling",
        out_shape=[jax.ShapeDtypeStruct(p.shape, p.dtype) for p in parts],
        in_specs=[ANY] * n, out_specs=[ANY] * n,
        scratch_shapes=[pltpu.SemaphoreType.DMA((n,)), pltpu.SemaphoreType.DMA((n,))],
    )(*parts)


def _sum_slots(buf, order, name):
    k, rows, cols = buf.shape
    tr = _row_tile(rows, 256)

    def body(b_ref, o_ref):
        acc = b_ref[order[0]].astype(F32)
        for s in order[1:]:
            acc = acc + b_ref[s].astype(F32)
        o_ref[...] = acc

    return _call(body, name=name, out_shape=jax.ShapeDtypeStruct((rows, cols), F32),
                 grid=(rows // tr,),
                 in_specs=[pl.BlockSpec((k, tr, cols), lambda i: (0, i, 0))],
                 out_specs=pl.BlockSpec((tr, cols), lambda i: (i, 0)),
                 compiler_params=_params(("parallel",)))(buf)


def _adamw(grad_parts, w, m, v, name):
    rows, cols = w.shape
    tr = _row_tile(rows, 256)
    npart = len(grad_parts)

    def body(*refs):
        gp = refs[:npart]
        w_ref, m_ref, v_ref, g_out, d_out, m_out, v_out = refs[npart:]
        g = gp[0][...]
        for p in gp[1:]:
            g = g + p[...]
        mm = ADAM_B1 * m_ref[...] + (1.0 - ADAM_B1) * g
        vv = ADAM_B2 * v_ref[...] + (1.0 - ADAM_B2) * jnp.square(g)
        m_hat = mm / (1.0 - ADAM_B1 ** ADAM_STEP)
        v_hat = vv / (1.0 - ADAM_B2 ** ADAM_STEP)
        g_out[...] = g
        d_out[...] = -ADAM_LR * (m_hat / (jnp.sqrt(v_hat) + ADAM_EPS) + ADAM_WD * w_ref[...])
        m_out[...] = mm
        v_out[...] = vv

    spec = pl.BlockSpec((tr, cols), lambda i: (i, 0))
    shp = jax.ShapeDtypeStruct((rows, cols), F32)
    return _call(body, name=name, out_shape=[shp] * 4, grid=(rows // tr,),
                 in_specs=[spec] * (npart + 3), out_specs=[spec] * 4,
                 compiler_params=_params(("parallel",)))(*grad_parts, w, m, v)


def _in_proj(x, g1, w_in):
    s = x.shape[0]
    ts = 512

    def body(x_ref, g_ref, w_ref, u_ref, q_ref, k_ref, v_ref, vt_ref, h_ref):
        xv = x_ref[...]
        h = (xv * _rstd(xv) * g_ref[...]).astype(BF16)
        h_ref[...] = h
        u_ref[...] = _dot(h, w_ref[0])
        q_ref[...] = (_dot(h, w_ref[1]) * Q_SCALE).astype(BF16)
        k_ref[...] = _dot(h, w_ref[2]).astype(BF16)
        v = _dot(h, w_ref[3])
        v_ref[...] = v.astype(BF16)
        vt_ref[...] = v.T.astype(BF16)

    row = lambda w: pl.BlockSpec((ts, w), lambda i: (i, 0))
    half = jax.ShapeDtypeStruct((s, D_POOL), BF16)
    return _call(
        body, name="in_proj",
        out_shape=[jax.ShapeDtypeStruct((s, D_POOL), F32), half, half, half,
                   jax.ShapeDtypeStruct((D_ATTN, s), BF16), jax.ShapeDtypeStruct((s, D_MODEL), BF16)],
        grid=(s // ts,),
        in_specs=[row(D_MODEL), _full((1, D_MODEL)), _full(w_in.shape)],
        out_specs=[row(D_POOL)] * 4 + [pl.BlockSpec((D_ATTN, ts), lambda i: (0, i)), row(D_MODEL)],
        compiler_params=_params(("parallel",)))(x, g1, w_in)


def _pool_means(ext_ref, g, window, ts, row0):
    cols = slice(g * POOL_GROUP, (g + 1) * POOL_GROUP)
    cur = ext_ref[POOL_HALO:POOL_HALO + ts, cols]
    acc = cur
    for d in range(1, window):
        acc = acc + ext_ref[POOL_HALO - d:POOL_HALO - d + ts, cols]
    t1 = row0 + 1 + lax.broadcasted_iota(jnp.int32, (ts, 1), 0)
    cnt = jnp.minimum(t1, window).astype(F32)
    return acc / cnt - cur, cnt


def _pool_fwd(u, w_pool, pool_scale):
    s = u.shape[0]
    ts = 512
    per = ts // POOL_HALO

    def body(u_ref, halo_ref, wp_ref, ps_ref, o_ref, ext_ref, y_ref):
        i = pl.program_id(0)
        ext_ref[0:POOL_HALO, :] = jnp.where(i > 0, halo_ref[...], 0.0)
        ext_ref[POOL_HALO:, :] = u_ref[...]
        for g, window in enumerate(POOL_WINDOWS):
            p, _ = _pool_means(ext_ref, g, window, ts, i * ts)
            y_ref[:, g * POOL_GROUP:(g + 1) * POOL_GROUP] = _dot(
                p.astype(BF16), wp_ref[g].astype(BF16))
        y = y_ref[...]
        o_ref[...] = (y * _rstd(y) * ps_ref[...]).astype(BF16)

    return _call(
        body, name="pool_fwd", out_shape=jax.ShapeDtypeStruct((s, D_POOL), BF16),
        grid=(s // ts,),
        in_specs=[pl.BlockSpec((ts, D_POOL), lambda i: (i, 0)),
                  pl.BlockSpec((POOL_HALO, D_POOL), lambda i: (jnp.maximum(i * per - 1, 0), 0)),
                  _full(w_pool.shape), _full((1, D_POOL))],
        out_specs=pl.BlockSpec((ts, D_POOL), lambda i: (i, 0)),
        scratch_shapes=[pltpu.VMEM((ts + POOL_HALO, D_POOL), F32), pltpu.VMEM((ts, D_POOL), F32)],
        compiler_params=_params(("parallel",)))(u, u, w_pool, pool_scale)


def _tri(kind):
    r = lax.broadcasted_iota(jnp.int32, (ATT_BLOCK, ATT_BLOCK), 0)
    c = lax.broadcasted_iota(jnp.int32, (ATT_BLOCK, ATT_BLOCK), 1)
    return jnp.where(r >= c if kind == "suffix" else r <= c, 1.0, 0.0).astype(BF16)


def _causal_mask():
    r = lax.broadcasted_iota(jnp.int32, (ATT_BLOCK, ATT_BLOCK), 0)
    c = lax.broadcasted_iota(jnp.int32, (ATT_BLOCK, ATT_BLOCK), 1)
    return c < r


def _softplus(z, with_sigmoid=False):
    ope = 1.0 + jnp.exp(jnp.minimum(z, 80.0))
    sp = jnp.maximum(z, jnp.log(ope))
    if with_sigmoid:
        return sp, 1.0 - 1.0 / ope
    return sp


def _attn_fwd(q, k, vt):
    s = q.shape[0]
    tb = ATT_BLOCK
    nq = s // tb

    def body(q_ref, k_ref, vt_ref, o_ref, t_ref, *bufs):
        i = pl.program_id(1)
        suffix = _tri("prefix")
        r_idx = lax.broadcasted_iota(jnp.int32, (tb, tb), 0)
        c_idx = lax.broadcasted_iota(jnp.int32, (tb, tb), 1)
        causal = r_idx < c_idx
        lane = lax.broadcasted_iota(jnp.int32, (1, HEAD_PAIR), 1)
        first = lane < 64
        top = lax.broadcasted_iota(jnp.int32, (HEAD_PAIR, 1), 0) < 64
        q2 = q_ref[...]
        zero = jnp.zeros_like(q2)
        qs = (jnp.where(first, q2, zero), jnp.where(first, zero, q2))

        def values_t(cols):
            vt = vt_ref[:, cols]
            none = jnp.zeros_like(vt)
            return jnp.concatenate([jnp.where(top, vt, none), jnp.where(top, none, vt)], axis=1)

        def scores(j, cs, masked):
            kj = k_ref[pl.ds(pl.multiple_of(j * tb, tb), tb), :]
            new_cs, args = [], []
            for e in range(2):
                z = _dot_nt(kj, qs[e])
                sp = _softplus(z)
                if masked:
                    sp = jnp.where(causal, sp, 0.0)
                incl = _dot(suffix, sp.astype(BF16))
                arg = z - incl - cs[e]
                if masked:
                    arg = jnp.where(causal, arg, NEG_BIG)
                args.append(arg)
                new_cs.append(cs[e] + incl[0:1, :])
            return new_cs, args

        def weigh(j, args, o):
            probs = [jnp.exp(arg).astype(BF16) for arg in args]
            return o + _dot(values_t(pl.ds(pl.multiple_of(j * tb, tb), tb)), jnp.concatenate(probs, axis=0))

        z_buf, zc_buf, in_buf = [[[bufs[4 * kind + 2 * slot + e] for e in range(2)]
                                  for slot in range(2)] for kind in range(3)]
        pr_buf = [bufs[12], bufs[13]]

        @pl.when((pl.program_id(0) == 0) & (i == 0))
        def _():
            for b in bufs:
                b[...] = jnp.zeros_like(b)

        def block_rows(p):
            return pl.ds(pl.multiple_of(jnp.clip(i - 1 - p, 0, nq - 1) * tb, tb), tb)

        def trip(t, w, carry):
            r = 1 - w
            cs, o = list(carry[0:2]), carry[2]
            live3 = (t - 3 >= 0) & (t - 3 < i)
            o = o + jnp.where(live3, _dot(values_t(block_rows(t - 3)), pr_buf[r][...]), 0.0)
            kj = k_ref[block_rows(t), :]
            for e in range(2):
                z_buf[w][e][...] = _dot_nt(kj, qs[e])
            for e in range(2):
                z = z_buf[r][e][...]
                zc_buf[w][e][...] = z
                in_buf[w][e][...] = _dot(suffix, _softplus(z).astype(BF16))
            live2 = (t - 2 >= 0) & (t - 2 < i)
            for e in range(2):
                incl = in_buf[r][e][...]
                arg = zc_buf[r][e][...] - incl - jnp.where(live2, cs[e], -NEG_BIG)
                pr_buf[w][e * tb:(e + 1) * tb, :] = jnp.exp(arg).astype(BF16)
                cs[e] = jnp.where(live2, cs[e] + incl[0:1, :], cs[e])
            return cs[0], cs[1], o

        def four_trips(n, cr):
            for u in range(4):
                cr = trip(4 * n + u, u % 2, cr)
            return cr

        row = jnp.zeros((1, tb), F32)
        cs, args = scores(i, (row, row), True)
        carry = (cs[0], cs[1], weigh(i, args, jnp.zeros((HEAD_PAIR, tb), F32)))
        carry = lax.fori_loop(0, jnp.where(i > 0, (i + 6) // 4, 0), four_trips, carry)
        o_ref[...] = carry[2].T
        totals = jnp.where(r_idx == 0, carry[0], jnp.where(r_idx == 1, carry[1], 0.0))
        t_ref[...] = totals.T[:, 0:2]

    score_buf = pltpu.VMEM((tb, tb), F32)
    return _call(
        body, name="attn_fwd",
        out_shape=[jax.ShapeDtypeStruct((s, D_ATTN), F32),
                   jax.ShapeDtypeStruct((4, s, 2), F32)],
        grid=(4, nq),
        in_specs=[pl.BlockSpec((tb, HEAD_PAIR), lambda h, i: (i, h)),
                  pl.BlockSpec((s, HEAD_PAIR), lambda h, i: (0, h)),
                  pl.BlockSpec((HEAD_PAIR, s), lambda h, i: (h, 0))],
        out_specs=[pl.BlockSpec((tb, HEAD_PAIR), lambda h, i: (i, h)),
                   pl.BlockSpec((None, tb, 2), lambda h, i: (h, i, 0))],
        scratch_shapes=[score_buf] * 12 + [pltpu.VMEM((2 * tb, tb), BF16)] * 2,
        compiler_params=_params(("arbitrary", "arbitrary")))(q, k, vt)


def _mix_out(attn, mpool, x, attn_scale, w_out, g2, g3):
    s = x.shape[0]
    ts = 512

    def body(a_ref, p_ref, x_ref, as_ref, w_ref, g2_ref, g3_ref, ma_ref, mix_ref, x1_ref, h2_ref):
        ao = a_ref[...]
        ma = (ao * _rstd(ao) * as_ref[...]).astype(BF16)
        ma_ref[...] = ma
        mix = _dot(p_ref[...], w_ref[0:D_POOL, :]) + _dot(ma, w_ref[D_POOL:, :])
        mix_ref[...] = mix
        x1 = x_ref[...] + mix * _rstd(mix) * g2_ref[...]
        x1_ref[...] = x1
        h2_ref[...] = (x1 * _rstd(x1) * g3_ref[...]).astype(BF16)

    row = lambda w: pl.BlockSpec((ts, w), lambda i: (i, 0))
    return _call(
        body, name="mix_out",
        out_shape=[jax.ShapeDtypeStruct((s, D_ATTN), BF16), jax.ShapeDtypeStruct((s, D_MODEL), F32),
                   jax.ShapeDtypeStruct((s, D_MODEL), F32), jax.ShapeDtypeStruct((s, D_MODEL), BF16)],
        grid=(s // ts,),
        in_specs=[row(D_ATTN), row(D_POOL), row(D_MODEL), _full((1, D_ATTN)),
                  _full((D_MODEL, D_MODEL)), _full((1, D_MODEL)), _full((1, D_MODEL))],
        out_specs=[row(D_ATTN), row(D_MODEL), row(D_MODEL), row(D_MODEL)],
        compiler_params=_params(("parallel",)))(attn, mpool, x, attn_scale, w_out, g2, g3)


def _conv_rows(ext_ref, cw, cb, ts):
    y = cb + cw[0:1, :] * ext_ref[CONV_HALO - 2:CONV_HALO - 2 + ts, :]
    y = y + cw[1:2, :] * ext_ref[CONV_HALO - 1:CONV_HALO - 1 + ts, :]
    return y + cw[2:3, :] * ext_ref[CONV_HALO:CONV_HALO + ts, :]


def _sigmoid(v):
    return 1.0 / (1.0 + jnp.exp(-v))


def _ffn_up(h2, w_up, conv_w, conv_b):
    s = h2.shape[0]
    ts = 256
    tn = FF_TILE

    def body(h_ref, wg_ref, wv_ref, cwg_ref, cwv_ref, cbg_ref, cbv_ref,
             ug_ref, uv_ref, f_ref, extg, extv):
        i = pl.program_id(1)

        @pl.when(i == 0)
        def _():
            extg[0:CONV_HALO, :] = jnp.zeros((CONV_HALO, tn), F32)
            extv[0:CONV_HALO, :] = jnp.zeros((CONV_HALO, tn), F32)

        h = h_ref[...]
        ug = _dot(h, wg_ref[...])
        uv = _dot(h, wv_ref[...])
        ug_ref[...] = ug
        uv_ref[...] = uv
        extg[CONV_HALO:, :] = ug
        extv[CONV_HALO:, :] = uv
        gate = _conv_rows(extg, cwg_ref[...], cbg_ref[...], ts)
        val = _conv_rows(extv, cwv_ref[...], cbv_ref[...], ts)
        f_ref[...] = (gate * _sigmoid(gate) * val).astype(BF16)
        extg[0:CONV_HALO, :] = extg[ts:ts + CONV_HALO, :]
        extv[0:CONV_HALO, :] = extv[ts:ts + CONV_HALO, :]

    out_blk = pl.BlockSpec((None, ts, tn), lambda n, i: (n, i, 0))
    act = jax.ShapeDtypeStruct((2, s, tn), F32)
    return _call(
        body, name="ffn_up",
        out_shape=[act, act, jax.ShapeDtypeStruct((2, s, tn), BF16)],
        grid=(2, s // ts),
        in_specs=[pl.BlockSpec((ts, D_MODEL), lambda n, i: (i, 0)),
                  pl.BlockSpec((None, D_MODEL, tn), lambda n, i: (n, 0, 0)),
                  pl.BlockSpec((None, D_MODEL, tn), lambda n, i: (n + 2, 0, 0)),
                  pl.BlockSpec((None, 3, tn), lambda n, i: (n, 0, 0)),
                  pl.BlockSpec((None, 3, tn), lambda n, i: (n + 2, 0, 0)),
                  pl.BlockSpec((None, 1, tn), lambda n, i: (n, 0, 0)),
                  pl.BlockSpec((None, 1, tn), lambda n, i: (n + 2, 0, 0))],
        out_specs=[out_blk, out_blk, out_blk],
        scratch_shapes=[pltpu.VMEM((ts + CONV_HALO, tn), F32), pltpu.VMEM((ts + CONV_HALO, tn), F32)],
        compiler_params=_params(("arbitrary", "arbitrary")))(
            h2, w_up, w_up, conv_w, conv_w, conv_b, conv_b)


def _ffn_down(f_in, w_down, x1, target, g4):
    s = x1.shape[0]
    ts = 512

    def body(f_ref, w_ref, x1_ref, t_ref, g_ref, df_ref, dy_ref, loss_ref, dg_ref):
        @pl.when(pl.program_id(0) == 0)
        def _():
            loss_ref[...] = jnp.zeros_like(loss_ref)
            dg_ref[...] = jnp.zeros_like(dg_ref)

        f = _dot(f_ref[0], w_ref[0:FF_TILE, :]) + _dot(f_ref[1], w_ref[FF_TILE:, :])
        rf = _rstd(f)
        fn = f * rf
        g = g_ref[...]
        err = (x1_ref[...] + fn * g) - t_ref[...]
        loss_ref[...] += 0.5 * jnp.sum(jnp.mean(err * err, axis=-1))
        dy = err * (1.0 / D_MODEL)
        dy_ref[...] = dy
        dg_ref[...] += jnp.sum(dy * fn, axis=0, keepdims=True)
        dfn = dy * g
        df_ref[...] = (rf * (dfn - fn * jnp.mean(dfn * fn, axis=-1, keepdims=True))).astype(BF16)

    row = pl.BlockSpec((ts, D_MODEL), lambda i: (i, 0))
    return _call(
        body, name="ffn_down",
        out_shape=[jax.ShapeDtypeStruct((s, D_MODEL), BF16), jax.ShapeDtypeStruct((s, D_MODEL), F32),
                   jax.ShapeDtypeStruct((8, 128), F32), jax.ShapeDtypeStruct((1, D_MODEL), F32)],
        grid=(s // ts,),
        in_specs=[pl.BlockSpec((2, ts, FF_TILE), lambda i: (0, i, 0)), _full((D_FF, D_MODEL)),
                  row, row, _full((1, D_MODEL))],
        out_specs=[row, row, _full((8, 128)), _full((1, D_MODEL))],
        compiler_params=_params(("arbitrary",)))(f_in, w_down, x1, target, g4)


def _tn_matmul(a, b, name, ts=512):
    na, s, ka = a.shape
    nb, _, nbc = b.shape
    steps = s // ts

    def body(a_ref, b_ref, o_ref, acc_ref):
        @pl.when(pl.program_id(2) == 0)
        def _():
            acc_ref[...] = jnp.zeros_like(acc_ref)

        acc_ref[...] += _dot_tn(a_ref[...].astype(BF16), b_ref[...].astype(BF16))

        @pl.when(pl.program_id(2) == steps - 1)
        def _():
            o_ref[...] = acc_ref[...].astype(BF16)

    return _call(
        body, name=name, out_shape=jax.ShapeDtypeStruct((na, nb, ka, nbc), BF16),
        grid=(na, nb, steps),
        in_specs=[pl.BlockSpec((None, ts, ka), lambda i, j, r: (i, r, 0)),
                  pl.BlockSpec((None, ts, nbc), lambda i, j, r: (j, r, 0))],
        out_specs=pl.BlockSpec((None, None, ka, nbc), lambda i, j, r: (i, j, 0, 0)),
        scratch_shapes=[pltpu.VMEM((ka, nbc), F32)],
        compiler_params=_params(("parallel", "parallel", "arbitrary")))(a, b)


def _ffn_bwd_act(df, w_down, upre_g, upre_v, conv_w, conv_b):
    s = df.shape[0]
    ts = 256
    tn = FF_TILE
    nr = s // ts
    per = ts // CONV_HALO

    def body(df_ref, wd_ref, ug_ref, uv_ref, hg_ref, hv_ref, cwg_ref, cwv_ref, cbg_ref, cbv_ref,
             dug_ref, duv_ref, dcwg_ref, dcwv_ref, dcbg_ref, dcbv_ref, extg, extv, dxg, dxv):
        i = pl.program_id(1)
        first_rows = i == nr - 1

        @pl.when(i == 0)
        def _():
            dxg[ts:, :] = jnp.zeros((CONV_HALO, tn), F32)
            dxv[ts:, :] = jnp.zeros((CONV_HALO, tn), F32)
            for r in (dcwg_ref, dcwv_ref, dcbg_ref, dcbv_ref):
                r[...] = jnp.zeros_like(r)

        extg[0:CONV_HALO, :] = jnp.where(first_rows, 0.0, hg_ref[...])
        extv[0:CONV_HALO, :] = jnp.where(first_rows, 0.0, hv_ref[...])
        extg[CONV_HALO:, :] = ug_ref[...]
        extv[CONV_HALO:, :] = uv_ref[...]
        cwg, cwv = cwg_ref[...], cwv_ref[...]
        gate = _conv_rows(extg, cwg, cbg_ref[...], ts)
        val = _conv_rows(extv, cwv, cbv_ref[...], ts)
        sg = _sigmoid(gate)
        dfin = _dot_nt(df_ref[...], wd_ref[...])
        dval = dfin * (gate * sg)
        dgate = dfin * val * (sg * (1.0 + gate * (1.0 - sg)))

        def conv_bwd(dact, ext, dx, cw, dcw_ref, dcb_ref, du_ref):
            dx[0:ts, :] = dact
            dcb_ref[...] += jnp.sum(dact, axis=0, keepdims=True)
            for kk in range(3):
                lo = CONV_HALO - 2 + kk
                dcw_ref[kk:kk + 1, :] += jnp.sum(dact * ext[lo:lo + ts, :], axis=0, keepdims=True)
            du = cw[2:3, :] * dact + cw[1:2, :] * dx[1:1 + ts, :] + cw[0:1, :] * dx[2:2 + ts, :]
            du_ref[...] = du.astype(BF16)
            dx[ts:, :] = dx[0:CONV_HALO, :]

        conv_bwd(dgate, extg, dxg, cwg, dcwg_ref, dcbg_ref, dug_ref)
        conv_bwd(dval, extv, dxv, cwv, dcwv_ref, dcbv_ref, duv_ref)

    rows = lambda n, i: (n, nr - 1 - i, 0)
    halo = lambda n, i: (n, jnp.maximum((nr - 1 - i) * per - 1, 0), 0)
    act_blk = pl.BlockSpec((None, ts, tn), rows)
    halo_blk = pl.BlockSpec((None, CONV_HALO, tn), halo)
    cw_blk = lambda off: pl.BlockSpec((None, 3, tn), lambda n, i: (n + off, 0, 0))
    cb_blk = lambda off: pl.BlockSpec((None, 1, tn), lambda n, i: (n + off, 0, 0))
    acc_w = pl.BlockSpec((None, 3, tn), lambda n, i: (n, 0, 0))
    acc_b = pl.BlockSpec((None, 1, tn), lambda n, i: (n, 0, 0))
    dact = jax.ShapeDtypeStruct((2, s, tn), BF16)
    return _call(
        body, name="ffn_bwd_act",
        out_shape=[dact, dact, jax.ShapeDtypeStruct((2, 3, tn), F32), jax.ShapeDtypeStruct((2, 3, tn), F32),
                   jax.ShapeDtypeStruct((2, 1, tn), F32), jax.ShapeDtypeStruct((2, 1, tn), F32)],
        grid=(2, nr),
        in_specs=[pl.BlockSpec((ts, D_MODEL), lambda n, i: (nr - 1 - i, 0)),
                  pl.BlockSpec((tn, D_MODEL), lambda n, i: (n, 0)),
                  act_blk, act_blk, halo_blk, halo_blk,
                  cw_blk(0), cw_blk(2), cb_blk(0), cb_blk(2)],
        out_specs=[act_blk, act_blk, acc_w, acc_w, acc_b, acc_b],
        scratch_shapes=[pltpu.VMEM((ts + CONV_HALO, tn), F32)] * 4,
        compiler_params=_params(("arbitrary", "arbitrary")))(
            df, w_down, upre_g, upre_v, upre_g, upre_v, conv_w, conv_w, conv_b, conv_b)


def _ffn_bwd_in(dug, duv, w_up, x1, dy, mix, g3, g2):
    s = x1.shape[0]
    ts = 256

    def body(dg_ref, dv_ref, w_ref, x1_ref, dy_ref, mix_ref, g3_ref, g2_ref,
             dx1_ref, dmix_ref, dg3_ref, dg2_ref):
        @pl.when(pl.program_id(0) == 0)
        def _():
            dg3_ref[...] = jnp.zeros_like(dg3_ref)
            dg2_ref[...] = jnp.zeros_like(dg2_ref)

        dh = _dot_nt(dg_ref[0], w_ref[0]) + _dot_nt(dg_ref[1], w_ref[1])
        dh = dh + _dot_nt(dv_ref[0], w_ref[2]) + _dot_nt(dv_ref[1], w_ref[3])
        x1 = x1_ref[...]
        r3 = _rstd(x1)
        xn = x1 * r3
        dg3_ref[...] += jnp.sum(dh * xn, axis=0, keepdims=True)
        dxn = dh * g3_ref[...]
        dx1 = dy_ref[...] + r3 * (dxn - xn * jnp.mean(dxn * xn, axis=-1, keepdims=True))
        dx1_ref[...] = dx1
        mix = mix_ref[...]
        rm = _rstd(mix)
        mn = mix * rm
        dg2_ref[...] += jnp.sum(dx1 * mn, axis=0, keepdims=True)
        dmn = dx1 * g2_ref[...]
        dmix_ref[...] = (rm * (dmn - mn * jnp.mean(dmn * mn, axis=-1, keepdims=True))).astype(BF16)

    row = pl.BlockSpec((ts, D_MODEL), lambda i: (i, 0))
    act = pl.BlockSpec((2, ts, FF_TILE), lambda i: (0, i, 0))
    vec = _full((1, D_MODEL))
    return _call(
        body, name="ffn_bwd_in",
        out_shape=[jax.ShapeDtypeStruct((s, D_MODEL), F32), jax.ShapeDtypeStruct((s, D_MODEL), BF16),
                   jax.ShapeDtypeStruct((1, D_MODEL), F32), jax.ShapeDtypeStruct((1, D_MODEL), F32)],
        grid=(s // ts,),
        in_specs=[act, act, _full(w_up.shape), row, row, row, vec, vec],
        out_specs=[row, row, vec, vec],
        compiler_params=_params(("arbitrary",), vmem_mb=56))(dug, duv, w_up, x1, dy, mix, g3, g2)


def _mix_bwd(dmix, w_out, attn, attn_scale):
    s = dmix.shape[0]
    ts = 512

    def body(dm_ref, w_ref, a_ref, as_ref, dp_ref, do_ref, das_ref):
        @pl.when(pl.program_id(0) == 0)
        def _():
            das_ref[...] = jnp.zeros_like(das_ref)

        dm = dm_ref[...]
        dp_ref[...] = _dot_nt(dm, w_ref[0:D_POOL, :])
        da = _dot_nt(dm, w_ref[D_POOL:, :])
        ao = a_ref[...]
        ra = _rstd(ao)
        an = ao * ra
        das_ref[...] += jnp.sum(da * an, axis=0, keepdims=True)
        dan = da * as_ref[...]
        do_ref[...] = (ra * (dan - an * jnp.mean(dan * an, axis=-1, keepdims=True))).astype(BF16)

    row = lambda w: pl.BlockSpec((ts, w), lambda i: (i, 0))
    return _call(
        body, name="mix_bwd",
        out_shape=[jax.ShapeDtypeStruct((s, D_POOL), F32), jax.ShapeDtypeStruct((s, D_ATTN), BF16),
                   jax.ShapeDtypeStruct((1, D_ATTN), F32)],
        grid=(s // ts,),
        in_specs=[row(D_MODEL), _full((D_MODEL, D_MODEL)), row(D_ATTN), _full((1, D_ATTN))],
        out_specs=[row(D_POOL), row(D_ATTN), _full((1, D_ATTN))],
        compiler_params=_params(("arbitrary",)))(dmix, w_out, attn, attn_scale)


def _attn_bwd(q, k, v, do, totals):
    s = q.shape[0]
    tb = ATT_BLOCK
    nq = s // tb

    def body(q_ref, do_ref, t_ref, k_hbm, v_hbm, dq_ref, dk_hbm, dv_hbm,
             k_scr, v_scr, dkt_acc, dvt_acc, stage, *bufs):
        hp = pl.program_id(0)
        i = pl.program_id(1)
        lanes = pl.ds(pl.multiple_of(hp * HEAD_PAIR, HEAD_PAIR), HEAD_PAIR)

        @pl.when(i == 0)
        def _():
            pltpu.sync_copy(k_hbm.at[:, lanes], k_scr)
            pltpu.sync_copy(v_hbm.at[:, lanes], v_scr)
            dkt_acc[...] = jnp.zeros_like(dkt_acc)
            dvt_acc[...] = jnp.zeros_like(dvt_acc)

        upper = _tri("suffix")
        lower = _tri("prefix")
        causal = _causal_mask()
        lane = lax.broadcasted_iota(jnp.int32, (1, HEAD_PAIR), 1)
        first = lane < 64
        q2 = q_ref[...]
        do2 = do_ref[...]
        zero = jnp.zeros_like(q2)
        qs = (jnp.where(first, q2, zero), jnp.where(first, zero, q2))
        dos = (jnp.where(first, do2, zero), jnp.where(first, zero, do2))
        qcat_t = jnp.concatenate(qs, axis=0).astype(F32).T.astype(BF16)
        docat_t = jnp.concatenate(dos, axis=0).astype(F32).T.astype(BF16)
        tots = (t_ref[:, 0:1], t_ref[:, 1:2])

        z_buf, zc_buf, sg_buf, sg2_buf, in_buf, da_buf, dw_buf, pre_buf = [
            [[bufs[4 * kind + 2 * slot + e] for e in range(2)] for slot in range(2)]
            for kind in range(8)]
        pr_buf, dzr_buf, dzc_buf = bufs[32:34], bufs[34:36], bufs[36:38]

        for e in range(2):
            z_buf[1][e][...] = jnp.full((tb, tb), NEG_BIG, F32)
            zc_buf[1][e][...] = jnp.full((tb, tb), NEG_BIG, F32)
            for buf in (sg_buf, sg2_buf, in_buf, da_buf, dw_buf, pre_buf):
                buf[1][e][...] = jnp.zeros((tb, tb), F32)
        for buf in (pr_buf, dzr_buf, dzc_buf):
            buf[1][...] = jnp.zeros_like(buf[1])

        def rows(p):
            return pl.ds(pl.multiple_of(jnp.clip(p, 0, nq - 1) * tb, tb), tb)

        def split_heads(block):
            return jnp.concatenate([jnp.where(first, block, zero), jnp.where(first, zero, block)], axis=0)

        def trip(t, w, carry):
            r = 1 - w
            cs, cps, dq = list(carry[0:2]), list(carry[2:4]), carry[4]
            live4 = (t - 4 >= 0) & (t - 4 < i)
            dq = dq + jnp.where(live4, _dot(dzc_buf[r][...], split_heads(k_scr[rows(t - 4), :])), 0.0)
            dkt_acc[:, rows(t - 4)] += jnp.where(live4, _dot(qcat_t, dzr_buf[r][...]), 0.0)
            dvt_acc[:, rows(t - 3)] += _dot(docat_t, pr_buf[r][...])
            kj = k_scr[rows(t), :]
            for e in range(2):
                z_buf[w][e][...] = _dot_nt(qs[e], kj)
            vj = v_scr[rows(t - 1), :]
            for e in range(2):
                z = z_buf[r][e][...]
                sp, sig = _softplus(z, True)
                zc_buf[w][e][...] = z
                sg_buf[w][e][...] = sig
                in_buf[w][e][...] = _dot(sp.astype(BF16), upper)
                da_buf[w][e][...] = _dot_nt(dos[e], vj)
            for e in range(2):
                incl = in_buf[r][e][...]
                cs[e] = cs[e] + incl[:, 0:1]
                off = jnp.where(t - 2 < i, tots[e] - cs[e], -NEG_BIG)
                a = jnp.exp(zc_buf[r][e][...] - incl - off)
                dw = a * da_buf[r][e][...]
                dw_buf[w][e][...] = dw
                sg2_buf[w][e][...] = sg_buf[r][e][...]
                pr_buf[w][e * tb:(e + 1) * tb, :] = a.astype(BF16)
                pre_buf[w][e][...] = _dot(dw.astype(BF16), lower)
            for e in range(2):
                pre = pre_buf[r][e][...] + cps[e]
                dzb = (dw_buf[r][e][...] - sg2_buf[r][e][...] * pre).astype(BF16)
                cps[e] = pre[:, tb - 1:tb]
                dzr_buf[w][e * tb:(e + 1) * tb, :] = dzb
                dzc_buf[w][:, e * tb:(e + 1) * tb] = dzb
            return cs[0], cs[1], cps[0], cps[1], dq

        col = jnp.zeros((tb, 1), F32)
        carry = (col, col, col, col, jnp.zeros((tb, HEAD_PAIR), F32))
        def four_trips(n, cr):
            for u in range(4):
                cr = trip(4 * n + u, u % 2, cr)
            return cr

        carry = lax.fori_loop(0, jnp.where(i > 0, (i + 7) // 4, 0), four_trips, carry)

        cps, dq = carry[2:4], carry[4]
        kj = k_scr[rows(i), :]
        vj = v_scr[rows(i), :]
        dzs, probs = [], []
        for e in range(2):
            z = _dot_nt(qs[e], kj)
            sp, sig = _softplus(z, True)
            incl = _dot(jnp.where(causal, sp, 0.0).astype(BF16), upper)
            a = jnp.where(causal, jnp.exp(z - incl), 0.0)
            dw = a * _dot_nt(dos[e], vj)
            pre = _dot(dw.astype(BF16), lower) + cps[e]
            dzs.append(jnp.where(causal, dw - sig * pre, 0.0).astype(BF16))
            probs.append(a.astype(BF16))
        dq = dq + _dot(jnp.concatenate(dzs, axis=1), split_heads(kj))
        dkt_acc[:, rows(i)] += _dot(qcat_t, jnp.concatenate(dzs, axis=0))
        dvt_acc[:, rows(i)] += _dot(docat_t, jnp.concatenate(probs, axis=0))
        dq_ref[...] = (dq * Q_SCALE).astype(BF16)

        @pl.when(i == nq - 1)
        def _():
            for acc, dst in ((dkt_acc, dk_hbm), (dvt_acc, dv_hbm)):
                def flip(n, _, acc=acc):
                    at = pl.ds(pl.multiple_of(n * tb, tb), tb)
                    stage[at, :] = acc[:, at].T
                    return 0
                lax.fori_loop(0, nq, flip, 0)
                pltpu.sync_copy(stage, dst.at[:, lanes])

    blk = pl.BlockSpec((tb, HEAD_PAIR), lambda h, i: (i, h))
    grad = jax.ShapeDtypeStruct((s, D_ATTN), F32)
    return _call(
        body, name="attn_bwd",
        out_shape=[jax.ShapeDtypeStruct((s, D_ATTN), BF16), grad, grad],
        grid=(4, nq),
        in_specs=[blk, blk, pl.BlockSpec((None, tb, 2), lambda h, i: (h, i, 0)), ANY, ANY],
        out_specs=[blk, ANY, ANY],
        scratch_shapes=[pltpu.VMEM((s, HEAD_PAIR), BF16), pltpu.VMEM((s, HEAD_PAIR), BF16),
                        pltpu.VMEM((HEAD_PAIR, s), F32), pltpu.VMEM((HEAD_PAIR, s), F32),
                        pltpu.VMEM((s, HEAD_PAIR), F32)]
        + [pltpu.VMEM((tb, tb), F32)] * 32
        + [pltpu.VMEM((2 * tb, tb), BF16)] * 4 + [pltpu.VMEM((tb, 2 * tb), BF16)] * 2,
        compiler_params=_params(("arbitrary", "arbitrary"), vmem_mb=60))(q, do, totals, k, v)


def _pool_bwd(u, dmp, w_pool, pool_scale):
    s = u.shape[0]
    ts = 512
    nr = s // ts
    per = ts // POOL_HALO

    def body(u_ref, halo_ref, dm_ref, wp_ref, ps_ref, du_ref, dwp_ref, dps_ref, ext_ref, y_ref, dext_ref):
        i = pl.program_id(0)
        rb = nr - 1 - i

        @pl.when(i == 0)
        def _():
            dext_ref[ts:, :] = jnp.zeros((POOL_HALO, D_POOL), F32)
            dwp_ref[...] = jnp.zeros_like(dwp_ref)
            dps_ref[...] = jnp.zeros_like(dps_ref)

        ext_ref[0:POOL_HALO, :] = jnp.where(rb > 0, halo_ref[...], 0.0)
        ext_ref[POOL_HALO:, :] = u_ref[...]
        ps, cnts = [], []
        for g, window in enumerate(POOL_WINDOWS):
            p, cnt = _pool_means(ext_ref, g, window, ts, rb * ts)
            ps.append(p.astype(BF16))
            cnts.append(cnt)
            y_ref[:, g * POOL_GROUP:(g + 1) * POOL_GROUP] = _dot(ps[g], wp_ref[g].astype(BF16))
        y = y_ref[...]
        r = _rstd(y)
        yn = y * r
        dm = dm_ref[...]
        dps_ref[...] += jnp.sum(dm * yn, axis=0, keepdims=True)
        dn = dm * ps_ref[...]
        dy = r * (dn - yn * jnp.mean(dn * yn, axis=-1, keepdims=True))
        for g, window in enumerate(POOL_WINDOWS):
            cols = slice(g * POOL_GROUP, (g + 1) * POOL_GROUP)
            dyg = dy[:, cols].astype(BF16)
            dwp_ref[g] += _dot_tn(ps[g], dyg)
            dp = _dot_nt(dyg, wp_ref[g].astype(BF16))
            dext_ref[0:ts, cols] = dp / cnts[g]
            acc = dext_ref[0:ts, cols]
            for d in range(1, window):
                acc = acc + dext_ref[d:d + ts, cols]
            du_ref[:, cols] = (acc - dp).astype(BF16)
        dext_ref[ts:, :] = dext_ref[0:POOL_HALO, :]

    rows = pl.BlockSpec((ts, D_POOL), lambda i: (nr - 1 - i, 0))
    return _call(
        body, name="pool_bwd",
        out_shape=[jax.ShapeDtypeStruct((s, D_POOL), BF16), jax.ShapeDtypeStruct(w_pool.shape, F32),
                   jax.ShapeDtypeStruct((1, D_POOL), F32)],
        grid=(nr,),
        in_specs=[rows,
                  pl.BlockSpec((POOL_HALO, D_POOL), lambda i: (jnp.maximum((nr - 1 - i) * per - 1, 0), 0)),
                  rows, _full(w_pool.shape), _full((1, D_POOL))],
        out_specs=[rows, _full(w_pool.shape), _full((1, D_POOL))],
        scratch_shapes=[pltpu.VMEM((ts + POOL_HALO, D_POOL), F32), pltpu.VMEM((ts, D_POOL), F32),
                        pltpu.VMEM((ts + POOL_HALO, D_POOL), F32)],
        compiler_params=_params(("arbitrary",)))(u, u, dmp, w_pool, pool_scale)


def _in_proj_bwd(du, dq, dk, dv, w_in, x, dx1, g1):
    s = x.shape[0]
    ts = 512

    def body(du_ref, dq_ref, dk_ref, dv_ref, w_ref, x_ref, dx1_ref, g_ref, gx_ref, dg_ref):
        @pl.when(pl.program_id(0) == 0)
        def _():
            dg_ref[...] = jnp.zeros_like(dg_ref)

        dh = _dot_nt(du_ref[...], w_ref[0]) + _dot_nt(dq_ref[...], w_ref[1])
        dh = dh + _dot_nt(dk_ref[...].astype(BF16), w_ref[2]) + _dot_nt(dv_ref[...].astype(BF16), w_ref[3])
        xv = x_ref[...]
        r = _rstd(xv)
        xn = xv * r
        dg_ref[...] += jnp.sum(dh * xn, axis=0, keepdims=True)
        dxn = dh * g_ref[...]
        gx_ref[...] = dx1_ref[...] + r * (dxn - xn * jnp.mean(dxn * xn, axis=-1, keepdims=True))

    row = lambda w: pl.BlockSpec((ts, w), lambda i: (i, 0))
    return _call(
        body, name="in_proj_bwd",
        out_shape=[jax.ShapeDtypeStruct((s, D_MODEL), F32), jax.ShapeDtypeStruct((1, D_MODEL), F32)],
        grid=(s // ts,),
        in_specs=[row(D_POOL)] * 4 + [_full(w_in.shape), row(D_MODEL), row(D_MODEL), _full((1, D_MODEL))],
        out_specs=[row(D_MODEL), _full((1, D_MODEL))],
        compiler_params=_params(("arbitrary",)))(du, dq, dk, dv, w_in, x, dx1, g1)


_SMALL = ("norm_mix_pre", "w_pool", "pool_scale", "attn_scale", "norm_mix_post",
          "norm_ffn_pre", "conv_b", "norm_ffn_post")
_SMALL_SIZE = {"norm_mix_pre": 1024, "w_pool": 65536, "pool_scale": 512, "attn_scale": 512,
               "norm_mix_post": 1024, "norm_ffn_pre": 1024, "conv_b": 5632, "norm_ffn_post": 1024}
_SMALL_ROWS = 600
_CONVW_ROWS = 132
_PACK_ROWS = _SMALL_ROWS + _CONVW_ROWS + 4


def _pack_small(parts):
    flat = jnp.concatenate([parts[n].reshape(-1) for n in _SMALL])
    flat = jnp.pad(flat, (0, _SMALL_ROWS * 128 - flat.shape[0]))
    return flat.reshape(_SMALL_ROWS, 128)


def _unpack_small(packed, like):
    flat = packed.reshape(-1)
    out, off = {}, 0
    for n in _SMALL:
        out[n] = flat[off:off + _SMALL_SIZE[n]].reshape(like[n].shape)
        off += _SMALL_SIZE[n]
    return out


def kernel(x, norm_mix_pre, w_in, w_pool, pool_scale, attn_scale, w_out, norm_mix_post, norm_ffn_pre, w_up, conv_w, conv_b, w_down, norm_ffn_post, loss_target, m_norm_mix_pre, m_w_in, m_w_pool, m_pool_scale, m_attn_scale, m_w_out, m_norm_mix_post, m_norm_ffn_pre, m_w_up, m_conv_w, m_conv_b, m_w_down, m_norm_ffn_post, v_norm_mix_pre, v_w_in, v_w_pool, v_pool_scale, v_attn_scale, v_w_out, v_norm_mix_post, v_norm_ffn_pre, v_w_up, v_conv_w, v_conv_b, v_w_down, v_norm_ffn_post):
    weights = dict(norm_mix_pre=norm_mix_pre, w_in=w_in, w_pool=w_pool, pool_scale=pool_scale,
                   attn_scale=attn_scale, w_out=w_out, norm_mix_post=norm_mix_post,
                   norm_ffn_pre=norm_ffn_pre, w_up=w_up, conv_w=conv_w, conv_b=conv_b,
                   w_down=w_down, norm_ffn_post=norm_ffn_post)
    mom1 = dict(norm_mix_pre=m_norm_mix_pre, w_in=m_w_in, w_pool=m_w_pool, pool_scale=m_pool_scale,
                attn_scale=m_attn_scale, w_out=m_w_out, norm_mix_post=m_norm_mix_post,
                norm_ffn_pre=m_norm_ffn_pre, w_up=m_w_up, conv_w=m_conv_w, conv_b=m_conv_b,
                w_down=m_w_down, norm_ffn_post=m_norm_ffn_post)
    mom2 = dict(norm_mix_pre=v_norm_mix_pre, w_in=v_w_in, w_pool=v_w_pool, pool_scale=v_pool_scale,
                attn_scale=v_attn_scale, w_out=v_w_out, norm_mix_post=v_norm_mix_post,
                norm_ffn_pre=v_norm_ffn_pre, w_up=v_w_up, conv_w=v_conv_w, conv_b=v_conv_b,
                w_down=v_w_down, norm_ffn_post=v_norm_ffn_post)
    order = list(weights)

    xs = x[0]
    target = loss_target[0]
    wp = w_pool[0]
    shard = lax.axis_index("x") * 2 + lax.axis_index("y")

    shards = [_cast_bf16(w_in[0], "cast_w_in"), _cast_bf16(w_out[0], "cast_w_out"),
              _cast_bf16(w_up[0], "cast_w_up"), _cast_bf16(w_down[0], "cast_w_down"), conv_w[0]]
    win_g, wout_g, wup_g, wdown_g, convw_g = _gather_shards(shards)
    wout_f = wout_g.reshape(D_MODEL, D_MODEL)
    wdown_f = wdown_g.reshape(D_FF, D_MODEL)
    convb_g = conv_b[0].reshape(N_SHARD, 1, FF_TILE)

    u, q, k, v, vt, h1 = _in_proj(xs, norm_mix_pre, win_g)
    mpool = _pool_fwd(u, wp, pool_scale)
    attn, totals = _attn_fwd(q, k, vt)
    mattn, mix, x1, h2 = _mix_out(attn, mpool, xs, attn_scale, wout_f, norm_mix_post, norm_ffn_pre)
    upre_g, upre_v, f_in = _ffn_up(h2, wup_g, convw_g, convb_g)
    df, dy, loss_tile, d_post = _ffn_down(f_in, wdown_f, x1, target, norm_ffn_post)

    d_wdown = _tn_matmul(f_in, df[None], "dw_down")
    dug, duv, dcw_g, dcw_v, dcb_g, dcb_v = _ffn_bwd_act(df, wdown_f, upre_g, upre_v, convw_g, convb_g)
    d_wup = jnp.concatenate([_tn_matmul(h2[None], dug, "dw_up_gate")[0],
                             _tn_matmul(h2[None], duv, "dw_up_value")[0]], axis=0)
    dx1, dmix, d_ffn_pre, d_mix_post = _ffn_bwd_in(dug, duv, wup_g, x1, dy, mix, norm_ffn_pre, norm_mix_post)
    d_wout = jnp.concatenate([_tn_matmul(mpool[None], dmix[None], "dw_out_pool")[0, 0],
                              _tn_matmul(mattn[None], dmix[None], "dw_out_attn")[0, 0]], axis=0)
    dmp, do, d_attn_scale = _mix_bwd(dmix, wout_f, attn, attn_scale)
    dq, dk, dv = _attn_bwd(q, k, v, do, totals)
    du, d_wpool, d_pool_scale = _pool_bwd(u, dmp, wp, pool_scale)
    d_win = jnp.stack([_tn_matmul(h1[None], t[None], "dw_in_%d" % n)[0, 0]
                       for n, t in enumerate((du, dq, dk, dv))])
    grad_x, d_mix_pre = _in_proj_bwd(du, dq, dk, dv, win_g, xs, dx1, norm_mix_pre)

    d_convw = jnp.concatenate([dcw_g, dcw_v], axis=0)
    d_convb = jnp.concatenate([dcb_g, dcb_v], axis=0).reshape(1, 2 * D_FF)
    small_parts = dict(norm_mix_pre=d_mix_pre, w_pool=d_wpool, pool_scale=d_pool_scale,
                       attn_scale=d_attn_scale, norm_mix_post=d_mix_post, norm_ffn_pre=d_ffn_pre,
                       conv_b=d_convb, norm_ffn_post=d_post)
    packed = jnp.concatenate([_pack_small(small_parts), d_convw.reshape(_CONVW_ROWS, 128),
                              loss_tile[0:4]], axis=0)
    big = [d_win, d_wout.reshape(N_SHARD, D_MODEL // N_SHARD, D_MODEL), d_wup,
           d_wdown.reshape(N_SHARD, D_FF // N_SHARD, D_MODEL)]
    recv, gathered = _scatter_grads(big, packed)
    quarter = [_sum_slots(r, (3, 0, 1, 2), "sum_chips_%d" % n) for n, r in enumerate(recv)]
    sibling = _swap_with_sibling(quarter)
    small_sum = _sum_slots(gathered, tuple(range(8)), "sum_small")

    results = {}
    for n, name in enumerate(("w_in", "w_out", "w_up", "w_down")):
        res = _adamw([quarter[n], sibling[n]], weights[name][0], mom1[name][0], mom2[name][0],
                     "adamw_" + name)
        results[name] = [t[None] for t in res]
    g_convw = lax.dynamic_slice_in_dim(
        small_sum[_SMALL_ROWS:_SMALL_ROWS + _CONVW_ROWS].reshape(N_SHARD, 3, FF_TILE), shard, 1, axis=0)[0]
    convw_pad = lambda t: jnp.pad(t, ((0, 5), (0, 0)))
    res = _adamw([convw_pad(g_convw)], convw_pad(conv_w[0]), convw_pad(m_conv_w[0]),
                 convw_pad(v_conv_w[0]), "adamw_conv_w")
    results["conv_w"] = [t[:3][None] for t in res]
    pack_w = _pack_small(weights)
    pack_m = _pack_small(mom1)
    pack_v = _pack_small(mom2)
    res = _adamw([small_sum[:_SMALL_ROWS]], pack_w, pack_m, pack_v, "adamw_small")
    unpacked = [_unpack_small(t, weights) for t in res]
    for name in _SMALL:
        results[name] = [t[name] for t in unpacked]

    loss = small_sum[_SMALL_ROWS + _CONVW_ROWS, 0]
    outs = [loss, grad_x[None]]
    for slot in range(4):
        outs.extend(results[name][slot] for name in order)
    return tuple(outs)
```

```python
import functools

import jax
import jax.numpy as jnp
from jax import lax
from jax.experimental import pallas as pl
from jax.experimental.pallas import tpu as pltpu

F32 = jnp.float32
BF16 = jnp.bfloat16

D_MODEL = 1024
D_POOL = 512
D_ATTN = 512
POOL_WINDOWS = (2, 4, 8, 16)
POOL_GROUP = 128
POOL_HALO = 16
CONV_HALO = 8
D_FF = 2816
FF_TILE = 1408
N_SHARD = 4
EPS = 1e-6
Q_SCALE = 0.125
ATT_BLOCK = 256
HEAD_PAIR = 128
MIB = 1 << 20
NEG_BIG = -1e30

ADAM_LR = 0.001
ADAM_B1 = 0.9
ADAM_B2 = 0.999
ADAM_EPS = 1e-08
ADAM_WD = 0.01
ADAM_STEP = 10

NT_DIMS = (((1,), (1,)), ((), ()))
TN_DIMS = (((0,), (0,)), ((), ()))
MESH = pl.DeviceIdType.MESH
ANY = pl.BlockSpec(memory_space=pl.ANY)


def _call(body, **kw):
    return pl.pallas_call(body, **kw)


def _params(sem=None, vmem_mb=48):
    return pltpu.CompilerParams(dimension_semantics=sem, vmem_limit_bytes=vmem_mb * MIB)


def _rstd(v):
    return lax.rsqrt(jnp.mean(v * v, axis=-1, keepdims=True) + EPS)


def _dot(a, b):
    return jnp.dot(a, b, preferred_element_type=F32)


def _dot_nt(a, b):
    return lax.dot_general(a, b, NT_DIMS, preferred_element_type=F32)


def _dot_tn(a, b):
    return lax.dot_general(a, b, TN_DIMS, preferred_element_type=F32)


def _row_tile(rows, cap):
    t = min(rows, cap)
    t -= t % 8
    while rows % t:
        t -= 8
    return t


def _full(shape):
    nd = len(shape)
    return pl.BlockSpec(shape, lambda *_: (0,) * nd)


def _chip_peers():
    x, y, c = lax.axis_index("x"), lax.axis_index("y"), lax.axis_index("c")
    return x, y, c, [(1 - x, y), (x, 1 - y), (1 - x, 1 - y)]


def _cast_bf16(a, name):
    def body(a_ref, o_ref):
        o_ref[...] = a_ref[...].astype(BF16)

    return _call(body, name=name, out_shape=jax.ShapeDtypeStruct(a.shape, BF16),
                 grid=(1,), in_specs=[_full(a.shape)], out_specs=_full(a.shape),
                 compiler_params=_params(("arbitrary",)))(a)


def _gather_shards(shards):
    n = len(shards)

    def body(*refs):
        ins, outs = refs[:n], refs[n:2 * n]
        send, recv, loc = refs[2 * n:]
        x, y, c, chips = _chip_peers()
        b = 2 * x + y
        local = [pltpu.make_async_copy(ins[t], outs[t].at[b], loc.at[t]) for t in range(n)]
        for cp in local:
            cp.start()
        remote = []
        for t in range(n):
            for k, (px, py) in enumerate(chips):
                remote.append(pltpu.make_async_remote_copy(
                    src_ref=ins[t], dst_ref=outs[t].at[b],
                    send_sem=send.at[3 * t + k], recv_sem=recv.at[3 * t + k],
                    device_id=(px, py, c), device_id_type=MESH))
        for cp in remote:
            cp.start()
        for cp in remote:
            cp.wait()
        for cp in local:
            cp.wait()

    return _call(
        body, name="gather_weights",
        out_shape=[jax.ShapeDtypeStruct((N_SHARD,) + s.shape, s.dtype) for s in shards],
        in_specs=[ANY] * n, out_specs=[ANY] * n,
        scratch_shapes=[pltpu.SemaphoreType.DMA((3 * n,)), pltpu.SemaphoreType.DMA((3 * n,)),
                        pltpu.SemaphoreType.DMA((n,))],
    )(*shards)


def _scatter_grads(grads, small):
    n = len(grads)

    def body(*refs):
        ins, small_in = refs[:n], refs[n]
        outs, small_out = refs[n + 1:2 * n + 1], refs[2 * n + 1]
        send, recv, loc, ssend, srecv = refs[2 * n + 2:]
        x, y, c, chips = _chip_peers()
        b = 2 * x + y
        me = 4 * x + 2 * y + c
        local = [pltpu.make_async_copy(ins[t].at[b], outs[t].at[3], loc.at[t]) for t in range(n)]
        local.append(pltpu.make_async_copy(small_in, small_out.at[me], loc.at[n]))
        for cp in local:
            cp.start()
        remote = []
        for t in range(n):
            for k, (px, py) in enumerate(chips):
                remote.append(pltpu.make_async_remote_copy(
                    src_ref=ins[t].at[2 * px + py], dst_ref=outs[t].at[k],
                    send_sem=send.at[3 * t + k], recv_sem=recv.at[3 * t + k],
                    device_id=(px, py, c), device_id_type=MESH))
        for r in range(1, 8):
            px = 1 - x if r & 4 else x
            py = 1 - y if r & 2 else y
            pc = 1 - c if r & 1 else c
            remote.append(pltpu.make_async_remote_copy(
                src_ref=small_in, dst_ref=small_out.at[me],
                send_sem=ssend.at[r - 1], recv_sem=srecv.at[r - 1],
                device_id=(px, py, pc), device_id_type=MESH))
        for cp in remote:
            cp.start()
        for cp in remote:
            cp.wait()
        for cp in local:
            cp.wait()

    out_shape = [jax.ShapeDtypeStruct(g.shape, g.dtype) for g in grads]
    out_shape.append(jax.ShapeDtypeStruct((8,) + small.shape, small.dtype))
    res = _call(
        body, name="scatter_grads", out_shape=out_shape,
        in_specs=[ANY] * (n + 1), out_specs=[ANY] * (n + 1),
        scratch_shapes=[pltpu.SemaphoreType.DMA((3 * n,)), pltpu.SemaphoreType.DMA((3 * n,)),
                        pltpu.SemaphoreType.DMA((n + 1,)),
                        pltpu.SemaphoreType.DMA((7,)), pltpu.SemaphoreType.DMA((7,))],
    )(*grads, small)
    return res[:n], res[n]


def _swap_with_sibling(parts):
    n = len(parts)

    def body(*refs):
        ins, outs = refs[:n], refs[n:2 * n]
        send, recv = refs[2 * n:]
        x, y, c = lax.axis_index("x"), lax.axis_index("y"), lax.axis_index("c")
        copies = [pltpu.make_async_remote_copy(
            src_ref=ins[t], dst_ref=outs[t], send_sem=send.at[t], recv_sem=recv.at[t],
            device_id=(x, y, 1 - c), device_id_type=MESH) for t in range(n)]
        for cp in copies:
            cp.start()
        for cp in copies:
            cp.wait()

    return _call(
        body, name="swap_sibling",
        out_shape=[jax.ShapeDtypeStruct(p.shape, p.dtype) for p in parts],
        in_specs=[ANY] * n, out_specs=[ANY] * n,
        scratch_shapes=[pltpu.SemaphoreType.DMA((n,)), pltpu.SemaphoreType.DMA((n,))],
    )(*parts)


def _sum_slots(buf, order, name):
    k, rows, cols = buf.shape
    tr = _row_tile(rows, 256)

    def body(b_ref, o_ref):
        acc = b_ref[order[0]].astype(F32)
        for s in order[1:]:
            acc = acc + b_ref[s].astype(F32)
        o_ref[...] = acc

    return _call(body, name=name, out_shape=jax.ShapeDtypeStruct((rows, cols), F32),
                 grid=(rows // tr,),
                 in_specs=[pl.BlockSpec((k, tr, cols), lambda i: (0, i, 0))],
                 out_specs=pl.BlockSpec((tr, cols), lambda i: (i, 0)),
                 compiler_params=_params(("parallel",)))(buf)


def _adamw(grad_parts, w, m, v, name):
    rows, cols = w.shape
    tr = _row_tile(rows, 256)
    npart = len(grad_parts)

    def body(*refs):
        gp = refs[:npart]
        w_ref, m_ref, v_ref, g_out, d_out, m_out, v_out = refs[npart:]
        g = gp[0][...]
        for p in gp[1:]:
            g = g + p[...]
        mm = ADAM_B1 * m_ref[...] + (1.0 - ADAM_B1) * g
        vv = ADAM_B2 * v_ref[...] + (1.0 - ADAM_B2) * jnp.square(g)
        m_hat = mm / (1.0 - ADAM_B1 ** ADAM_STEP)
        v_hat = vv / (1.0 - ADAM_B2 ** ADAM_STEP)
        g_out[...] = g
        d_out[...] = -ADAM_LR * (m_hat / (jnp.sqrt(v_hat) + ADAM_EPS) + ADAM_WD * w_ref[...])
        m_out[...] = mm
        v_out[...] = vv

    spec = pl.BlockSpec((tr, cols), lambda i: (i, 0))
    shp = jax.ShapeDtypeStruct((rows, cols), F32)
    return _call(body, name=name, out_shape=[shp] * 4, grid=(rows // tr,),
                 in_specs=[spec] * (npart + 3), out_specs=[spec] * 4,
                 compiler_params=_params(("parallel",)))(*grad_parts, w, m, v)


def _in_proj(x, g1, w_in):
    s = x.shape[0]
    ts = 512

    def body(x_ref, g_ref, w_ref, u_ref, q_ref, k_ref, v_ref, vt_ref, h_ref):
        xv = x_ref[...]
        h = (xv * _rstd(xv) * g_ref[...]).astype(BF16)
        h_ref[...] = h
        u_ref[...] = _dot(h, w_ref[0])
        q_ref[...] = (_dot(h, w_ref[1]) * Q_SCALE).astype(BF16)
        k_ref[...] = _dot(h, w_ref[2]).astype(BF16)
        v = _dot(h, w_ref[3])
        v_ref[...] = v.astype(BF16)
        vt = v.T.astype(BF16)
        for n in range(ts // ATT_BLOCK):
            vt_ref[n] = vt[:, n * ATT_BLOCK:(n + 1) * ATT_BLOCK]

    row = lambda w: pl.BlockSpec((ts, w), lambda i: (i, 0))
    half = jax.ShapeDtypeStruct((s, D_POOL), BF16)
    return _call(
        body, name="in_proj",
        out_shape=[jax.ShapeDtypeStruct((s, D_POOL), F32), half, half, half,
                   jax.ShapeDtypeStruct((s // ATT_BLOCK, D_ATTN, ATT_BLOCK), BF16),
                   jax.ShapeDtypeStruct((s, D_MODEL), BF16)],
        grid=(s // ts,),
        in_specs=[row(D_MODEL), _full((1, D_MODEL)), _full(w_in.shape)],
        out_specs=[row(D_POOL)] * 4
        + [pl.BlockSpec((ts // ATT_BLOCK, D_ATTN, ATT_BLOCK), lambda i: (i, 0, 0)), row(D_MODEL)],
        compiler_params=_params(("parallel",)))(x, g1, w_in)


def _pool_means(ext_ref, g, window, ts, row0):
    cols = slice(g * POOL_GROUP, (g + 1) * POOL_GROUP)
    cur = ext_ref[POOL_HALO:POOL_HALO + ts, cols]
    acc = cur
    for d in range(1, window):
        acc = acc + ext_ref[POOL_HALO - d:POOL_HALO - d + ts, cols]
    t1 = row0 + 1 + lax.broadcasted_iota(jnp.int32, (ts, 1), 0)
    cnt = jnp.minimum(t1, window).astype(F32)
    return acc / cnt - cur, cnt


def _pool_fwd(u, w_pool, pool_scale):
    s = u.shape[0]
    ts = 512
    per = ts // POOL_HALO

    def body(u_ref, halo_ref, wp_ref, ps_ref, o_ref, ext_ref, y_ref):
        i = pl.program_id(0)
        ext_ref[0:POOL_HALO, :] = jnp.where(i > 0, halo_ref[...], 0.0)
        ext_ref[POOL_HALO:, :] = u_ref[...]
        for g, window in enumerate(POOL_WINDOWS):
            p, _ = _pool_means(ext_ref, g, window, ts, i * ts)
            y_ref[:, g * POOL_GROUP:(g + 1) * POOL_GROUP] = _dot(
                p.astype(BF16), wp_ref[g].astype(BF16))
        y = y_ref[...]
        o_ref[...] = (y * _rstd(y) * ps_ref[...]).astype(BF16)

    return _call(
        body, name="pool_fwd", out_shape=jax.ShapeDtypeStruct((s, D_POOL), BF16),
        grid=(s // ts,),
        in_specs=[pl.BlockSpec((ts, D_POOL), lambda i: (i, 0)),
                  pl.BlockSpec((POOL_HALO, D_POOL), lambda i: (jnp.maximum(i * per - 1, 0), 0)),
                  _full(w_pool.shape), _full((1, D_POOL))],
        out_specs=pl.BlockSpec((ts, D_POOL), lambda i: (i, 0)),
        scratch_shapes=[pltpu.VMEM((ts + POOL_HALO, D_POOL), F32), pltpu.VMEM((ts, D_POOL), F32)],
        compiler_params=_params(("parallel",)))(u, u, w_pool, pool_scale)


def _tri(kind):
    r = lax.broadcasted_iota(jnp.int32, (ATT_BLOCK, ATT_BLOCK), 0)
    c = lax.broadcasted_iota(jnp.int32, (ATT_BLOCK, ATT_BLOCK), 1)
    return jnp.where(r >= c if kind == "suffix" else r <= c, 1.0, 0.0).astype(BF16)


def _causal_mask():
    r = lax.broadcasted_iota(jnp.int32, (ATT_BLOCK, ATT_BLOCK), 0)
    c = lax.broadcasted_iota(jnp.int32, (ATT_BLOCK, ATT_BLOCK), 1)
    return c < r


def _softplus(z, with_sigmoid=False):
    ope = 1.0 + jnp.exp(jnp.minimum(z, 80.0))
    sp = jnp.maximum(z, jnp.log(ope))
    if with_sigmoid:
        return sp, 1.0 - 1.0 / ope
    return sp


def _attn_fwd(q, k, vt):
    s = q.shape[0]
    tb = ATT_BLOCK
    nq = s // tb

    def body(q_ref, k_ref, vt_ref, o_ref, t_ref, *bufs):
        i = pl.program_id(1)
        suffix = _tri("prefix")
        r_idx = lax.broadcasted_iota(jnp.int32, (tb, tb), 0)
        c_idx = lax.broadcasted_iota(jnp.int32, (tb, tb), 1)
        causal = r_idx < c_idx
        lane = lax.broadcasted_iota(jnp.int32, (1, HEAD_PAIR), 1)
        first = lane < 64
        top = lax.broadcasted_iota(jnp.int32, (HEAD_PAIR, 1), 0) < 64
        q2 = q_ref[...]
        zero = jnp.zeros_like(q2)
        qs = (jnp.where(first, q2, zero), jnp.where(first, zero, q2))

        def values_t(j):
            vt = vt_ref[j]
            none = jnp.zeros_like(vt)
            return jnp.concatenate([jnp.where(top, vt, none), jnp.where(top, none, vt)], axis=1)

        def scores(j, cs, masked):
            kj = k_ref[pl.ds(pl.multiple_of(j * tb, tb), tb), :]
            new_cs, args = [], []
            for e in range(2):
                z = _dot_nt(kj, qs[e])
                sp = _softplus(z)
                if masked:
                    sp = jnp.where(causal, sp, 0.0)
                incl = _dot(suffix, sp.astype(BF16))
                arg = z - incl - cs[e]
                if masked:
                    arg = jnp.where(causal, arg, NEG_BIG)
                args.append(arg)
                new_cs.append(cs[e] + incl[0:1, :])
            return new_cs, args

        def weigh(j, args, o):
            probs = [jnp.exp(arg).astype(BF16) for arg in args]
            return o + _dot(values_t(j), jnp.concatenate(probs, axis=0))

        z_buf, zc_buf, in_buf = [[[bufs[4 * kind + 2 * slot + e] for e in range(2)]
                                  for slot in range(2)] for kind in range(3)]
        pr_buf = [bufs[12], bufs[13]]

        @pl.when((pl.program_id(0) == 0) & (i == 0))
        def _():
            for b in bufs:
                b[...] = jnp.zeros_like(b)

        def block_of(p):
            return jnp.clip(i - 1 - p, 0, nq - 1)

        def trip(t, w, carry):
            r = 1 - w
            cs, o = list(carry[0:2]), carry[2]
            live3 = (t - 3 >= 0) & (t - 3 < i)
            o = o + jnp.where(live3, _dot(values_t(block_of(t - 3)), pr_buf[r][...]), 0.0)
            kj = k_ref[pl.ds(pl.multiple_of(block_of(t) * tb, tb), tb), :]
            for e in range(2):
                z_buf[w][e][...] = _dot_nt(kj, qs[e])
            for e in range(2):
                z = z_buf[r][e][...]
                zc_buf[w][e][...] = z
                in_buf[w][e][...] = _dot(suffix, _softplus(z).astype(BF16))
            live2 = (t - 2 >= 0) & (t - 2 < i)
            for e in range(2):
                incl = in_buf[r][e][...]
                arg = zc_buf[r][e][...] - incl - jnp.where(live2, cs[e], -NEG_BIG)
                pr_buf[w][e * tb:(e + 1) * tb, :] = jnp.exp(arg).astype(BF16)
                cs[e] = jnp.where(live2, cs[e] + incl[0:1, :], cs[e])
            return cs[0], cs[1], o

        def four_trips(n, cr):
            for u in range(4):
                cr = trip(4 * n + u, u % 2, cr)
            return cr

        row = jnp.zeros((1, tb), F32)
        cs, args = scores(i, (row, row), True)
        carry = (cs[0], cs[1], weigh(i, args, jnp.zeros((HEAD_PAIR, tb), F32)))
        carry = lax.fori_loop(0, jnp.where(i > 0, (i + 6) // 4, 0), four_trips, carry)
        o_ref[...] = carry[2].T
        totals = jnp.where(r_idx == 0, carry[0], jnp.where(r_idx == 1, carry[1], 0.0))
        t_ref[...] = totals.T[:, 0:2]

    score_buf = pltpu.VMEM((tb, tb), F32)
    return _call(
        body, name="attn_fwd",
        out_shape=[jax.ShapeDtypeStruct((s, D_ATTN), F32),
                   jax.ShapeDtypeStruct((4, s, 2), F32)],
        grid=(4, nq),
        in_specs=[pl.BlockSpec((tb, HEAD_PAIR), lambda h, i: (i, h)),
                  pl.BlockSpec((s, HEAD_PAIR), lambda h, i: (0, h)),
                  pl.BlockSpec((nq, HEAD_PAIR, tb), lambda h, i: (0, h, 0))],
        out_specs=[pl.BlockSpec((tb, HEAD_PAIR), lambda h, i: (i, h)),
                   pl.BlockSpec((None, tb, 2), lambda h, i: (h, i, 0))],
        scratch_shapes=[score_buf] * 12 + [pltpu.VMEM((2 * tb, tb), BF16)] * 2,
        compiler_params=_params(("arbitrary", "arbitrary")))(q, k, vt)


def _mix_out(attn, mpool, x, attn_scale, w_out, g2, g3):
    s = x.shape[0]
    ts = 512

    def body(a_ref, p_ref, x_ref, as_ref, w_ref, g2_ref, g3_ref, ma_ref, mix_ref, x1_ref, h2_ref):
        ao = a_ref[...]
        ma = (ao * _rstd(ao) * as_ref[...]).astype(BF16)
        ma_ref[...] = ma
        mix = _dot(p_ref[...], w_ref[0:D_POOL, :]) + _dot(ma, w_ref[D_POOL:, :])
        mix_ref[...] = mix
        x1 = x_ref[...] + mix * _rstd(mix) * g2_ref[...]
        x1_ref[...] = x1
        h2_ref[...] = (x1 * _rstd(x1) * g3_ref[...]).astype(BF16)

    row = lambda w: pl.BlockSpec((ts, w), lambda i: (i, 0))
    return _call(
        body, name="mix_out",
        out_shape=[jax.ShapeDtypeStruct((s, D_ATTN), BF16), jax.ShapeDtypeStruct((s, D_MODEL), F32),
                   jax.ShapeDtypeStruct((s, D_MODEL), F32), jax.ShapeDtypeStruct((s, D_MODEL), BF16)],
        grid=(s // ts,),
        in_specs=[row(D_ATTN), row(D_POOL), row(D_MODEL), _full((1, D_ATTN)),
                  _full((D_MODEL, D_MODEL)), _full((1, D_MODEL)), _full((1, D_MODEL))],
        out_specs=[row(D_ATTN), row(D_MODEL), row(D_MODEL), row(D_MODEL)],
        compiler_params=_params(("parallel",)))(attn, mpool, x, attn_scale, w_out, g2, g3)


def _conv_rows(ext_ref, cw, cb, ts):
    y = cb + cw[0:1, :] * ext_ref[CONV_HALO - 2:CONV_HALO - 2 + ts, :]
    y = y + cw[1:2, :] * ext_ref[CONV_HALO - 1:CONV_HALO - 1 + ts, :]
    return y + cw[2:3, :] * ext_ref[CONV_HALO:CONV_HALO + ts, :]


def _sigmoid(v):
    return 1.0 / (1.0 + jnp.exp(-v))


def _ffn_up(h2, w_up, conv_w, conv_b):
    s = h2.shape[0]
    ts = 256
    tn = FF_TILE

    def body(h_ref, wg_ref, wv_ref, cwg_ref, cwv_ref, cbg_ref, cbv_ref,
             ug_ref, uv_ref, f_ref, extg, extv):
        i = pl.program_id(1)

        @pl.when(i == 0)
        def _():
            extg[0:CONV_HALO, :] = jnp.zeros((CONV_HALO, tn), F32)
            extv[0:CONV_HALO, :] = jnp.zeros((CONV_HALO, tn), F32)

        h = h_ref[...]
        ug = _dot(h, wg_ref[...])
        uv = _dot(h, wv_ref[...])
        ug_ref[...] = ug
        uv_ref[...] = uv
        extg[CONV_HALO:, :] = ug
        extv[CONV_HALO:, :] = uv
        gate = _conv_rows(extg, cwg_ref[...], cbg_ref[...], ts)
        val = _conv_rows(extv, cwv_ref[...], cbv_ref[...], ts)
        f_ref[...] = (gate * _sigmoid(gate) * val).astype(BF16)
        extg[0:CONV_HALO, :] = extg[ts:ts + CONV_HALO, :]
        extv[0:CONV_HALO, :] = extv[ts:ts + CONV_HALO, :]

    out_blk = pl.BlockSpec((None, ts, tn), lambda n, i: (n, i, 0))
    act = jax.ShapeDtypeStruct((2, s, tn), F32)
    return _call(
        body, name="ffn_up",
        out_shape=[act, act, jax.ShapeDtypeStruct((2, s, tn), BF16)],
        grid=(2, s // ts),
        in_specs=[pl.BlockSpec((ts, D_MODEL), lambda n, i: (i, 0)),
                  pl.BlockSpec((None, D_MODEL, tn), lambda n, i: (n, 0, 0)),
                  pl.BlockSpec((None, D_MODEL, tn), lambda n, i: (n + 2, 0, 0)),
                  pl.BlockSpec((None, 3, tn), lambda n, i: (n, 0, 0)),
                  pl.BlockSpec((None, 3, tn), lambda n, i: (n + 2, 0, 0)),
                  pl.BlockSpec((None, 1, tn), lambda n, i: (n, 0, 0)),
                  pl.BlockSpec((None, 1, tn), lambda n, i: (n + 2, 0, 0))],
        out_specs=[out_blk, out_blk, out_blk],
        scratch_shapes=[pltpu.VMEM((ts + CONV_HALO, tn), F32), pltpu.VMEM((ts + CONV_HALO, tn), F32)],
        compiler_params=_params(("arbitrary", "arbitrary")))(
            h2, w_up, w_up, conv_w, conv_w, conv_b, conv_b)


def _ffn_down(f_in, w_down, x1, target, g4):
    s = x1.shape[0]
    ts = 512

    def body(f_ref, w_ref, x1_ref, t_ref, g_ref, df_ref, dy_ref, loss_ref, dg_ref):
        @pl.when(pl.program_id(0) == 0)
        def _():
            loss_ref[...] = jnp.zeros_like(loss_ref)
            dg_ref[...] = jnp.zeros_like(dg_ref)

        f = _dot(f_ref[0], w_ref[0:FF_TILE, :]) + _dot(f_ref[1], w_ref[FF_TILE:, :])
        rf = _rstd(f)
        fn = f * rf
        g = g_ref[...]
        err = (x1_ref[...] + fn * g) - t_ref[...]
        loss_ref[...] += 0.5 * jnp.sum(jnp.mean(err * err, axis=-1))
        dy = err * (1.0 / D_MODEL)
        dy_ref[...] = dy
        dg_ref[...] += jnp.sum(dy * fn, axis=0, keepdims=True)
        dfn = dy * g
        df_ref[...] = (rf * (dfn - fn * jnp.mean(dfn * fn, axis=-1, keepdims=True))).astype(BF16)

    row = pl.BlockSpec((ts, D_MODEL), lambda i: (i, 0))
    return _call(
        body, name="ffn_down",
        out_shape=[jax.ShapeDtypeStruct((s, D_MODEL), BF16), jax.ShapeDtypeStruct((s, D_MODEL), F32),
                   jax.ShapeDtypeStruct((8, 128), F32), jax.ShapeDtypeStruct((1, D_MODEL), F32)],
        grid=(s // ts,),
        in_specs=[pl.BlockSpec((2, ts, FF_TILE), lambda i: (0, i, 0)), _full((D_FF, D_MODEL)),
                  row, row, _full((1, D_MODEL))],
        out_specs=[row, row, _full((8, 128)), _full((1, D_MODEL))],
        compiler_params=_params(("arbitrary",)))(f_in, w_down, x1, target, g4)


def _tn_matmul(a, b, name, ts=512):
    na, s, ka = a.shape
    nb, _, nbc = b.shape
    steps = s // ts

    def body(a_ref, b_ref, o_ref, acc_ref):
        @pl.when(pl.program_id(2) == 0)
        def _():
            acc_ref[...] = jnp.zeros_like(acc_ref)

        acc_ref[...] += _dot_tn(a_ref[...].astype(BF16), b_ref[...].astype(BF16))

        @pl.when(pl.program_id(2) == steps - 1)
        def _():
            o_ref[...] = acc_ref[...].astype(BF16)

    return _call(
        body, name=name, out_shape=jax.ShapeDtypeStruct((na, nb, ka, nbc), BF16),
        grid=(na, nb, steps),
        in_specs=[pl.BlockSpec((None, ts, ka), lambda i, j, r: (i, r, 0)),
                  pl.BlockSpec((None, ts, nbc), lambda i, j, r: (j, r, 0))],
        out_specs=pl.BlockSpec((None, None, ka, nbc), lambda i, j, r: (i, j, 0, 0)),
        scratch_shapes=[pltpu.VMEM((ka, nbc), F32)],
        compiler_params=_params(("parallel", "parallel", "arbitrary")))(a, b)


def _ffn_bwd_act(df, w_down, upre_g, upre_v, conv_w, conv_b):
    s = df.shape[0]
    ts = 256
    tn = FF_TILE
    nr = s // ts
    per = ts // CONV_HALO

    def body(df_ref, wd_ref, ug_ref, uv_ref, hg_ref, hv_ref, cwg_ref, cwv_ref, cbg_ref, cbv_ref,
             dug_ref, duv_ref, dcwg_ref, dcwv_ref, dcbg_ref, dcbv_ref, extg, extv, dxg, dxv):
        i = pl.program_id(1)
        first_rows = i == nr - 1

        @pl.when(i == 0)
        def _():
            dxg[ts:, :] = jnp.zeros((CONV_HALO, tn), F32)
            dxv[ts:, :] = jnp.zeros((CONV_HALO, tn), F32)
            for r in (dcwg_ref, dcwv_ref, dcbg_ref, dcbv_ref):
                r[...] = jnp.zeros_like(r)

        extg[0:CONV_HALO, :] = jnp.where(first_rows, 0.0, hg_ref[...])
        extv[0:CONV_HALO, :] = jnp.where(first_rows, 0.0, hv_ref[...])
        extg[CONV_HALO:, :] = ug_ref[...]
        extv[CONV_HALO:, :] = uv_ref[...]
        cwg, cwv = cwg_ref[...], cwv_ref[...]
        gate = _conv_rows(extg, cwg, cbg_ref[...], ts)
        val = _conv_rows(extv, cwv, cbv_ref[...], ts)
        sg = _sigmoid(gate)
        dfin = _dot_nt(df_ref[...], wd_ref[...])
        dval = dfin * (gate * sg)
        dgate = dfin * val * (sg * (1.0 + gate * (1.0 - sg)))

        def conv_bwd(dact, ext, dx, cw, dcw_ref, dcb_ref, du_ref):
            dx[0:ts, :] = dact
            dcb_ref[...] += jnp.sum(dact, axis=0, keepdims=True)
            for kk in range(3):
                lo = CONV_HALO - 2 + kk
                dcw_ref[kk:kk + 1, :] += jnp.sum(dact * ext[lo:lo + ts, :], axis=0, keepdims=True)
            du = cw[2:3, :] * dact + cw[1:2, :] * dx[1:1 + ts, :] + cw[0:1, :] * dx[2:2 + ts, :]
            du_ref[...] = du.astype(BF16)
            dx[ts:, :] = dx[0:CONV_HALO, :]

        conv_bwd(dgate, extg, dxg, cwg, dcwg_ref, dcbg_ref, dug_ref)
        conv_bwd(dval, extv, dxv, cwv, dcwv_ref, dcbv_ref, duv_ref)

    rows = lambda n, i: (n, nr - 1 - i, 0)
    halo = lambda n, i: (n, jnp.maximum((nr - 1 - i) * per - 1, 0), 0)
    act_blk = pl.BlockSpec((None, ts, tn), rows)
    halo_blk = pl.BlockSpec((None, CONV_HALO, tn), halo)
    cw_blk = lambda off: pl.BlockSpec((None, 3, tn), lambda n, i: (n + off, 0, 0))
    cb_blk = lambda off: pl.BlockSpec((None, 1, tn), lambda n, i: (n + off, 0, 0))
    acc_w = pl.BlockSpec((None, 3, tn), lambda n, i: (n, 0, 0))
    acc_b = pl.BlockSpec((None, 1, tn), lambda n, i: (n, 0, 0))
    dact = jax.ShapeDtypeStruct((2, s, tn), BF16)
    return _call(
        body, name="ffn_bwd_act",
        out_shape=[dact, dact, jax.ShapeDtypeStruct((2, 3, tn), F32), jax.ShapeDtypeStruct((2, 3, tn), F32),
                   jax.ShapeDtypeStruct((2, 1, tn), F32), jax.ShapeDtypeStruct((2, 1, tn), F32)],
        grid=(2, nr),
        in_specs=[pl.BlockSpec((ts, D_MODEL), lambda n, i: (nr - 1 - i, 0)),
                  pl.BlockSpec((tn, D_MODEL), lambda n, i: (n, 0)),
                  act_blk, act_blk, halo_blk, halo_blk,
                  cw_blk(0), cw_blk(2), cb_blk(0), cb_blk(2)],
        out_specs=[act_blk, act_blk, acc_w, acc_w, acc_b, acc_b],
        scratch_shapes=[pltpu.VMEM((ts + CONV_HALO, tn), F32)] * 4,
        compiler_params=_params(("arbitrary", "arbitrary")))(
            df, w_down, upre_g, upre_v, upre_g, upre_v, conv_w, conv_w, conv_b, conv_b)


def _ffn_bwd_in(dug, duv, w_up, x1, dy, mix, g3, g2):
    s = x1.shape[0]
    ts = 256

    def body(dg_ref, dv_ref, w_ref, x1_ref, dy_ref, mix_ref, g3_ref, g2_ref,
             dx1_ref, dmix_ref, dg3_ref, dg2_ref):
        @pl.when(pl.program_id(0) == 0)
        def _():
            dg3_ref[...] = jnp.zeros_like(dg3_ref)
            dg2_ref[...] = jnp.zeros_like(dg2_ref)

        dh = _dot_nt(dg_ref[0], w_ref[0]) + _dot_nt(dg_ref[1], w_ref[1])
        dh = dh + _dot_nt(dv_ref[0], w_ref[2]) + _dot_nt(dv_ref[1], w_ref[3])
        x1 = x1_ref[...]
        r3 = _rstd(x1)
        xn = x1 * r3
        dg3_ref[...] += jnp.sum(dh * xn, axis=0, keepdims=True)
        dxn = dh * g3_ref[...]
        dx1 = dy_ref[...] + r3 * (dxn - xn * jnp.mean(dxn * xn, axis=-1, keepdims=True))
        dx1_ref[...] = dx1
        mix = mix_ref[...]
        rm = _rstd(mix)
        mn = mix * rm
        dg2_ref[...] += jnp.sum(dx1 * mn, axis=0, keepdims=True)
        dmn = dx1 * g2_ref[...]
        dmix_ref[...] = (rm * (dmn - mn * jnp.mean(dmn * mn, axis=-1, keepdims=True))).astype(BF16)

    row = pl.BlockSpec((ts, D_MODEL), lambda i: (i, 0))
    act = pl.BlockSpec((2, ts, FF_TILE), lambda i: (0, i, 0))
    vec = _full((1, D_MODEL))
    return _call(
        body, name="ffn_bwd_in",
        out_shape=[jax.ShapeDtypeStruct((s, D_MODEL), F32), jax.ShapeDtypeStruct((s, D_MODEL), BF16),
                   jax.ShapeDtypeStruct((1, D_MODEL), F32), jax.ShapeDtypeStruct((1, D_MODEL), F32)],
        grid=(s // ts,),
        in_specs=[act, act, _full(w_up.shape), row, row, row, vec, vec],
        out_specs=[row, row, vec, vec],
        compiler_params=_params(("arbitrary",), vmem_mb=56))(dug, duv, w_up, x1, dy, mix, g3, g2)


def _mix_bwd(dmix, w_out, attn, attn_scale):
    s = dmix.shape[0]
    ts = 512

    def body(dm_ref, w_ref, a_ref, as_ref, dp_ref, do_ref, das_ref):
        @pl.when(pl.program_id(0) == 0)
        def _():
            das_ref[...] = jnp.zeros_like(das_ref)

        dm = dm_ref[...]
        dp_ref[...] = _dot_nt(dm, w_ref[0:D_POOL, :])
        da = _dot_nt(dm, w_ref[D_POOL:, :])
        ao = a_ref[...]
        ra = _rstd(ao)
        an = ao * ra
        das_ref[...] += jnp.sum(da * an, axis=0, keepdims=True)
        dan = da * as_ref[...]
        do_ref[...] = (ra * (dan - an * jnp.mean(dan * an, axis=-1, keepdims=True))).astype(BF16)

    row = lambda w: pl.BlockSpec((ts, w), lambda i: (i, 0))
    return _call(
        body, name="mix_bwd",
        out_shape=[jax.ShapeDtypeStruct((s, D_POOL), F32), jax.ShapeDtypeStruct((s, D_ATTN), BF16),
                   jax.ShapeDtypeStruct((1, D_ATTN), F32)],
        grid=(s // ts,),
        in_specs=[row(D_MODEL), _full((D_MODEL, D_MODEL)), row(D_ATTN), _full((1, D_ATTN))],
        out_specs=[row(D_POOL), row(D_ATTN), _full((1, D_ATTN))],
        compiler_params=_params(("arbitrary",)))(dmix, w_out, attn, attn_scale)


def _attn_bwd(q, k, v, do, totals):
    s = q.shape[0]
    tb = ATT_BLOCK
    nq = s // tb

    def body(q_ref, do_ref, t_ref, k_hbm, v_hbm, dq_ref, dk_hbm, dv_hbm,
             k_scr, v_scr, dkt_acc, dvt_acc, stage, *bufs):
        hp = pl.program_id(0)
        i = pl.program_id(1)
        lanes = pl.ds(pl.multiple_of(hp * HEAD_PAIR, HEAD_PAIR), HEAD_PAIR)

        @pl.when(i == 0)
        def _():
            pltpu.sync_copy(k_hbm.at[:, lanes], k_scr)
            pltpu.sync_copy(v_hbm.at[:, lanes], v_scr)
            dkt_acc[...] = jnp.zeros_like(dkt_acc)
            dvt_acc[...] = jnp.zeros_like(dvt_acc)

        upper = _tri("suffix")
        lower = _tri("prefix")
        causal = _causal_mask()
        lane = lax.broadcasted_iota(jnp.int32, (1, HEAD_PAIR), 1)
        first = lane < 64
        q2 = q_ref[...]
        do2 = do_ref[...]
        zero = jnp.zeros_like(q2)
        qs = (jnp.where(first, q2, zero), jnp.where(first, zero, q2))
        dos = (jnp.where(first, do2, zero), jnp.where(first, zero, do2))
        qcat_t = jnp.concatenate(qs, axis=0).astype(F32).T.astype(BF16)
        docat_t = jnp.concatenate(dos, axis=0).astype(F32).T.astype(BF16)
        tots = (t_ref[:, 0:1], t_ref[:, 1:2])

        z_buf, zc_buf, sg_buf, sg2_buf, in_buf, da_buf, dw_buf, pre_buf = [
            [[bufs[4 * kind + 2 * slot + e] for e in range(2)] for slot in range(2)]
            for kind in range(8)]
        pr_buf, dzr_buf, dzc_buf = bufs[32:34], bufs[34:36], bufs[36:38]

        for e in range(2):
            z_buf[1][e][...] = jnp.full((tb, tb), NEG_BIG, F32)
            zc_buf[1][e][...] = jnp.full((tb, tb), NEG_BIG, F32)
            for buf in (sg_buf, sg2_buf, in_buf, da_buf, dw_buf, pre_buf):
                buf[1][e][...] = jnp.zeros((tb, tb), F32)
        for buf in (pr_buf, dzr_buf, dzc_buf):
            buf[1][...] = jnp.zeros_like(buf[1])

        def rows(p):
            return pl.ds(pl.multiple_of(jnp.clip(p, 0, nq - 1) * tb, tb), tb)

        def split_heads(block):
            return jnp.concatenate([jnp.where(first, block, zero), jnp.where(first, zero, block)], axis=0)

        def trip(t, w, carry):
            r = 1 - w
            cs, cps, dq = list(carry[0:2]), list(carry[2:4]), carry[4]
            live4 = (t - 4 >= 0) & (t - 4 < i)
            dq = dq + jnp.where(live4, _dot(dzc_buf[r][...], split_heads(k_scr[rows(t - 4), :])), 0.0)
            dkt_acc[jnp.clip(t - 4, 0, nq - 1)] += jnp.where(live4, _dot(qcat_t, dzr_buf[r][...]), 0.0)
            dvt_acc[jnp.clip(t - 3, 0, nq - 1)] += _dot(docat_t, pr_buf[r][...])
            kj = k_scr[rows(t), :]
            for e in range(2):
                z_buf[w][e][...] = _dot_nt(qs[e], kj)
            vj = v_scr[rows(t - 1), :]
            for e in range(2):
                z = z_buf[r][e][...]
                sp, sig = _softplus(z, True)
                zc_buf[w][e][...] = z
                sg_buf[w][e][...] = sig
                in_buf[w][e][...] = _dot(sp.astype(BF16), upper)
                da_buf[w][e][...] = _dot_nt(dos[e], vj)
            for e in range(2):
                incl = in_buf[r][e][...]
                cs[e] = cs[e] + incl[:, 0:1]
                off = jnp.where(t - 2 < i, tots[e] - cs[e], -NEG_BIG)
                a = jnp.exp(zc_buf[r][e][...] - incl - off)
                dw = a * da_buf[r][e][...]
                dw_buf[w][e][...] = dw
                sg2_buf[w][e][...] = sg_buf[r][e][...]
                pr_buf[w][e * tb:(e + 1) * tb, :] = a.astype(BF16)
                pre_buf[w][e][...] = _dot(dw.astype(BF16), lower)
            for e in range(2):
                pre = pre_buf[r][e][...] + cps[e]
                dzb = (dw_buf[r][e][...] - sg2_buf[r][e][...] * pre).astype(BF16)
                cps[e] = pre[:, tb - 1:tb]
                dzr_buf[w][e * tb:(e + 1) * tb, :] = dzb
                dzc_buf[w][:, e * tb:(e + 1) * tb] = dzb
            return cs[0], cs[1], cps[0], cps[1], dq

        col = jnp.zeros((tb, 1), F32)
        carry = (col, col, col, col, jnp.zeros((tb, HEAD_PAIR), F32))
        def four_trips(n, cr):
            for u in range(4):
                cr = trip(4 * n + u, u % 2, cr)
            return cr

        carry = lax.fori_loop(0, jnp.where(i > 0, (i + 7) // 4, 0), four_trips, carry)

        cps, dq = carry[2:4], carry[4]
        kj = k_scr[rows(i), :]
        vj = v_scr[rows(i), :]
        dzs, probs = [], []
        for e in range(2):
            z = _dot_nt(qs[e], kj)
            sp, sig = _softplus(z, True)
            incl = _dot(jnp.where(causal, sp, 0.0).astype(BF16), upper)
            a = jnp.where(causal, jnp.exp(z - incl), 0.0)
            dw = a * _dot_nt(dos[e], vj)
            pre = _dot(dw.astype(BF16), lower) + cps[e]
            dzs.append(jnp.where(causal, dw - sig * pre, 0.0).astype(BF16))
            probs.append(a.astype(BF16))
        dq = dq + _dot(jnp.concatenate(dzs, axis=1), split_heads(kj))
        dkt_acc[i] += _dot(qcat_t, jnp.concatenate(dzs, axis=0))
        dvt_acc[i] += _dot(docat_t, jnp.concatenate(probs, axis=0))
        dq_ref[...] = (dq * Q_SCALE).astype(BF16)

        @pl.when(i == nq - 1)
        def _():
            for acc, dst in ((dkt_acc, dk_hbm), (dvt_acc, dv_hbm)):
                def flip(n, _, acc=acc):
                    at = pl.ds(pl.multiple_of(n * tb, tb), tb)
                    stage[at, :] = acc[n].T
                    return 0
                lax.fori_loop(0, nq, flip, 0)
                pltpu.sync_copy(stage, dst.at[:, lanes])

    blk = pl.BlockSpec((tb, HEAD_PAIR), lambda h, i: (i, h))
    grad = jax.ShapeDtypeStruct((s, D_ATTN), F32)
    return _call(
        body, name="attn_bwd",
        out_shape=[jax.ShapeDtypeStruct((s, D_ATTN), BF16), grad, grad],
        grid=(4, nq),
        in_specs=[blk, blk, pl.BlockSpec((None, tb, 2), lambda h, i: (h, i, 0)), ANY, ANY],
        out_specs=[blk, ANY, ANY],
        scratch_shapes=[pltpu.VMEM((s, HEAD_PAIR), BF16), pltpu.VMEM((s, HEAD_PAIR), BF16),
                        pltpu.VMEM((nq, HEAD_PAIR, tb), F32), pltpu.VMEM((nq, HEAD_PAIR, tb), F32),
                        pltpu.VMEM((s, HEAD_PAIR), F32)]
        + [pltpu.VMEM((tb, tb), F32)] * 32
        + [pltpu.VMEM((2 * tb, tb), BF16)] * 4 + [pltpu.VMEM((tb, 2 * tb), BF16)] * 2,
        compiler_params=_params(("arbitrary", "arbitrary"), vmem_mb=60))(q, do, totals, k, v)


def _pool_bwd(u, dmp, w_pool, pool_scale):
    s = u.shape[0]
    ts = 512
    nr = s // ts
    per = ts // POOL_HALO

    def body(u_ref, halo_ref, dm_ref, wp_ref, ps_ref, du_ref, dwp_ref, dps_ref, ext_ref, y_ref, dext_ref):
        i = pl.program_id(0)
        rb = nr - 1 - i

        @pl.when(i == 0)
        def _():
            dext_ref[ts:, :] = jnp.zeros((POOL_HALO, D_POOL), F32)
            dwp_ref[...] = jnp.zeros_like(dwp_ref)
            dps_ref[...] = jnp.zeros_like(dps_ref)

        ext_ref[0:POOL_HALO, :] = jnp.where(rb > 0, halo_ref[...], 0.0)
        ext_ref[POOL_HALO:, :] = u_ref[...]
        ps, cnts = [], []
        for g, window in enumerate(POOL_WINDOWS):
            p, cnt = _pool_means(ext_ref, g, window, ts, rb * ts)
            ps.append(p.astype(BF16))
            cnts.append(cnt)
            y_ref[:, g * POOL_GROUP:(g + 1) * POOL_GROUP] = _dot(ps[g], wp_ref[g].astype(BF16))
        y = y_ref[...]
        r = _rstd(y)
        yn = y * r
        dm = dm_ref[...]
        dps_ref[...] += jnp.sum(dm * yn, axis=0, keepdims=True)
        dn = dm * ps_ref[...]
        dy = r * (dn - yn * jnp.mean(dn * yn, axis=-1, keepdims=True))
        for g, window in enumerate(POOL_WINDOWS):
            cols = slice(g * POOL_GROUP, (g + 1) * POOL_GROUP)
            dyg = dy[:, cols].astype(BF16)
            dwp_ref[g] += _dot_tn(ps[g], dyg)
            dp = _dot_nt(dyg, wp_ref[g].astype(BF16))
            dext_ref[0:ts, cols] = dp / cnts[g]
            acc = dext_ref[0:ts, cols]
            for d in range(1, window):
                acc = acc + dext_ref[d:d + ts, cols]
            du_ref[:, cols] = (acc - dp).astype(BF16)
        dext_ref[ts:, :] = dext_ref[0:POOL_HALO, :]

    rows = pl.BlockSpec((ts, D_POOL), lambda i: (nr - 1 - i, 0))
    return _call(
        body, name="pool_bwd",
        out_shape=[jax.ShapeDtypeStruct((s, D_POOL), BF16), jax.ShapeDtypeStruct(w_pool.shape, F32),
                   jax.ShapeDtypeStruct((1, D_POOL), F32)],
        grid=(nr,),
        in_specs=[rows,
                  pl.BlockSpec((POOL_HALO, D_POOL), lambda i: (jnp.maximum((nr - 1 - i) * per - 1, 0), 0)),
                  rows, _full(w_pool.shape), _full((1, D_POOL))],
        out_specs=[rows, _full(w_pool.shape), _full((1, D_POOL))],
        scratch_shapes=[pltpu.VMEM((ts + POOL_HALO, D_POOL), F32), pltpu.VMEM((ts, D_POOL), F32),
                        pltpu.VMEM((ts + POOL_HALO, D_POOL), F32)],
        compiler_params=_params(("arbitrary",)))(u, u, dmp, w_pool, pool_scale)


def _in_proj_bwd(du, dq, dk, dv, w_in, x, dx1, g1):
    s = x.shape[0]
    ts = 512

    def body(du_ref, dq_ref, dk_ref, dv_ref, w_ref, x_ref, dx1_ref, g_ref, gx_ref, dg_ref):
        @pl.when(pl.program_id(0) == 0)
        def _():
            dg_ref[...] = jnp.zeros_like(dg_ref)

        dh = _dot_nt(du_ref[...], w_ref[0]) + _dot_nt(dq_ref[...], w_ref[1])
        dh = dh + _dot_nt(dk_ref[...].astype(BF16), w_ref[2]) + _dot_nt(dv_ref[...].astype(BF16), w_ref[3])
        xv = x_ref[...]
        r = _rstd(xv)
        xn = xv * r
        dg_ref[...] += jnp.sum(dh * xn, axis=0, keepdims=True)
        dxn = dh * g_ref[...]
        gx_ref[...] = dx1_ref[...] + r * (dxn - xn * jnp.mean(dxn * xn, axis=-1, keepdims=True))

    row = lambda w: pl.BlockSpec((ts, w), lambda i: (i, 0))
    return _call(
        body, name="in_proj_bwd",
        out_shape=[jax.ShapeDtypeStruct((s, D_MODEL), F32), jax.ShapeDtypeStruct((1, D_MODEL), F32)],
        grid=(s // ts,),
        in_specs=[row(D_POOL)] * 4 + [_full(w_in.shape), row(D_MODEL), row(D_MODEL), _full((1, D_MODEL))],
        out_specs=[row(D_MODEL), _full((1, D_MODEL))],
        compiler_params=_params(("arbitrary",)))(du, dq, dk, dv, w_in, x, dx1, g1)


_SMALL = ("norm_mix_pre", "w_pool", "pool_scale", "attn_scale", "norm_mix_post",
          "norm_ffn_pre", "conv_b", "norm_ffn_post")
_SMALL_SIZE = {"norm_mix_pre": 1024, "w_pool": 65536, "pool_scale": 512, "attn_scale": 512,
               "norm_mix_post": 1024, "norm_ffn_pre": 1024, "conv_b": 5632, "norm_ffn_post": 1024}
_SMALL_ROWS = 600
_CONVW_ROWS = 132
_PACK_ROWS = _SMALL_ROWS + _CONVW_ROWS + 4


def _pack_small(parts):
    flat = jnp.concatenate([parts[n].reshape(-1) for n in _SMALL])
    flat = jnp.pad(flat, (0, _SMALL_ROWS * 128 - flat.shape[0]))
    return flat.reshape(_SMALL_ROWS, 128)


def _unpack_small(packed, like):
    flat = packed.reshape(-1)
    out, off = {}, 0
    for n in _SMALL:
        out[n] = flat[off:off + _SMALL_SIZE[n]].reshape(like[n].shape)
        off += _SMALL_SIZE[n]
    return out


def kernel(x, norm_mix_pre, w_in, w_pool, pool_scale, attn_scale, w_out, norm_mix_post, norm_ffn_pre, w_up, conv_w, conv_b, w_down, norm_ffn_post, loss_target, m_norm_mix_pre, m_w_in, m_w_pool, m_pool_scale, m_attn_scale, m_w_out, m_norm_mix_post, m_norm_ffn_pre, m_w_up, m_conv_w, m_conv_b, m_w_down, m_norm_ffn_post, v_norm_mix_pre, v_w_in, v_w_pool, v_pool_scale, v_attn_scale, v_w_out, v_norm_mix_post, v_norm_ffn_pre, v_w_up, v_conv_w, v_conv_b, v_w_down, v_norm_ffn_post):
    weights = dict(norm_mix_pre=norm_mix_pre, w_in=w_in, w_pool=w_pool, pool_scale=pool_scale,
                   attn_scale=attn_scale, w_out=w_out, norm_mix_post=norm_mix_post,
                   norm_ffn_pre=norm_ffn_pre, w_up=w_up, conv_w=conv_w, conv_b=conv_b,
                   w_down=w_down, norm_ffn_post=norm_ffn_post)
    mom1 = dict(norm_mix_pre=m_norm_mix_pre, w_in=m_w_in, w_pool=m_w_pool, pool_scale=m_pool_scale,
                attn_scale=m_attn_scale, w_out=m_w_out, norm_mix_post=m_norm_mix_post,
                norm_ffn_pre=m_norm_ffn_pre, w_up=m_w_up, conv_w=m_conv_w, conv_b=m_conv_b,
                w_down=m_w_down, norm_ffn_post=m_norm_ffn_post)
    mom2 = dict(norm_mix_pre=v_norm_mix_pre, w_in=v_w_in, w_pool=v_w_pool, pool_scale=v_pool_scale,
                attn_scale=v_attn_scale, w_out=v_w_out, norm_mix_post=v_norm_mix_post,
                norm_ffn_pre=v_norm_ffn_pre, w_up=v_w_up, conv_w=v_conv_w, conv_b=v_conv_b,
                w_down=v_w_down, norm_ffn_post=v_norm_ffn_post)
    order = list(weights)

    xs = x[0]
    target = loss_target[0]
    wp = w_pool[0]
    shard = lax.axis_index("x") * 2 + lax.axis_index("y")

    shards = [_cast_bf16(w_in[0], "cast_w_in"), _cast_bf16(w_out[0], "cast_w_out"),
              _cast_bf16(w_up[0], "cast_w_up"), _cast_bf16(w_down[0], "cast_w_down"), conv_w[0]]
    win_g, wout_g, wup_g, wdown_g, convw_g = _gather_shards(shards)
    wout_f = wout_g.reshape(D_MODEL, D_MODEL)
    wdown_f = wdown_g.reshape(D_FF, D_MODEL)
    convb_g = conv_b[0].reshape(N_SHARD, 1, FF_TILE)

    u, q, k, v, vt, h1 = _in_proj(xs, norm_mix_pre, win_g)
    mpool = _pool_fwd(u, wp, pool_scale)
    attn, totals = _attn_fwd(q, k, vt)
    mattn, mix, x1, h2 = _mix_out(attn, mpool, xs, attn_scale, wout_f, norm_mix_post, norm_ffn_pre)
    upre_g, upre_v, f_in = _ffn_up(h2, wup_g, convw_g, convb_g)
    df, dy, loss_tile, d_post = _ffn_down(f_in, wdown_f, x1, target, norm_ffn_post)

    d_wdown = _tn_matmul(f_in, df[None], "dw_down")
    dug, duv, dcw_g, dcw_v, dcb_g, dcb_v = _ffn_bwd_act(df, wdown_f, upre_g, upre_v, convw_g, convb_g)
    d_wup = jnp.concatenate([_tn_matmul(h2[None], dug, "dw_up_gate")[0],
                             _tn_matmul(h2[None], duv, "dw_up_value")[0]], axis=0)
    dx1, dmix, d_ffn_pre, d_mix_post = _ffn_bwd_in(dug, duv, wup_g, x1, dy, mix, norm_ffn_pre, norm_mix_post)
    d_wout = jnp.concatenate([_tn_matmul(mpool[None], dmix[None], "dw_out_pool")[0, 0],
                              _tn_matmul(mattn[None], dmix[None], "dw_out_attn")[0, 0]], axis=0)
    dmp, do, d_attn_scale = _mix_bwd(dmix, wout_f, attn, attn_scale)
    dq, dk, dv = _attn_bwd(q, k, v, do, totals)
    du, d_wpool, d_pool_scale = _pool_bwd(u, dmp, wp, pool_scale)
    d_win = jnp.stack([_tn_matmul(h1[None], t[None], "dw_in_%d" % n)[0, 0]
                       for n, t in enumerate((du, dq, dk, dv))])
    grad_x, d_mix_pre = _in_proj_bwd(du, dq, dk, dv, win_g, xs, dx1, norm_mix_pre)

    d_convw = jnp.concatenate([dcw_g, dcw_v], axis=0)
    d_convb = jnp.concatenate([dcb_g, dcb_v], axis=0).reshape(1, 2 * D_FF)
    small_parts = dict(norm_mix_pre=d_mix_pre, w_pool=d_wpool, pool_scale=d_pool_scale,
                       attn_scale=d_attn_scale, norm_mix_post=d_mix_post, norm_ffn_pre=d_ffn_pre,
                       conv_b=d_convb, norm_ffn_post=d_post)
    packed = jnp.concatenate([_pack_small(small_parts), d_convw.reshape(_CONVW_ROWS, 128),
                              loss_tile[0:4]], axis=0)
    big = [d_win, d_wout.reshape(N_SHARD, D_MODEL // N_SHARD, D_MODEL), d_wup,
           d_wdown.reshape(N_SHARD, D_FF // N_SHARD, D_MODEL)]
    recv, gathered = _scatter_grads(big, packed)
    quarter = [_sum_slots(r, (3, 0, 1, 2), "sum_chips_%d" % n) for n, r in enumerate(recv)]
    sibling = _swap_with_sibling(quarter)
    small_sum = _sum_slots(gathered, tuple(range(8)), "sum_small")

    results = {}
    for n, name in enumerate(("w_in", "w_out", "w_up", "w_down")):
        res = _adamw([quarter[n], sibling[n]], weights[name][0], mom1[name][0], mom2[name][0],
                     "adamw_" + name)
        results[name] = [t[None] for t in res]
    g_convw = lax.dynamic_slice_in_dim(
        small_sum[_SMALL_ROWS:_SMALL_ROWS + _CONVW_ROWS].reshape(N_SHARD, 3, FF_TILE), shard, 1, axis=0)[0]
    convw_pad = lambda t: jnp.pad(t, ((0, 5), (0, 0)))
    res = _adamw([convw_pad(g_convw)], convw_pad(conv_w[0]), convw_pad(m_conv_w[0]),
                 convw_pad(v_conv_w[0]), "adamw_conv_w")
    results["conv_w"] = [t[:3][None] for t in res]
    pack_w = _pack_small(weights)
    pack_m = _pack_small(mom1)
    pack_v = _pack_small(mom2)
    res = _adamw([small_sum[:_SMALL_ROWS]], pack_w, pack_m, pack_v, "adamw_small")
    unpacked = [_unpack_small(t, weights) for t in res]
    for name in _SMALL:
        results[name] = [t[name] for t in unpacked]

    loss = small_sum[_SMALL_ROWS + _CONVW_ROWS, 0]
    outs = [loss, grad_x[None]]
    for slot in range(4):
        outs.extend(results[name][slot] for name in order)
    return tuple(outs)
```

```python
import functools

import jax
import jax.numpy as jnp
from jax import lax
from jax.experimental import pallas as pl
from jax.experimental.pallas import tpu as pltpu

F32 = jnp.float32
BF16 = jnp.bfloat16

D_MODEL = 1024
D_POOL = 512
D_ATTN = 512
POOL_WINDOWS = (2, 4, 8, 16)
POOL_GROUP = 128
POOL_HALO = 16
CONV_HALO = 8
D_FF = 2816
FF_TILE = 1408
N_SHARD = 4
EPS = 1e-6
Q_SCALE = 0.125
ATT_BLOCK = 256
HEAD_PAIR = 128
MIB = 1 << 20
NEG_BIG = -1e30

ADAM_LR = 0.001
ADAM_B1 = 0.9
ADAM_B2 = 0.999
ADAM_EPS = 1e-08
ADAM_WD = 0.01
ADAM_STEP = 10

NT_DIMS = (((1,), (1,)), ((), ()))
TN_DIMS = (((0,), (0,)), ((), ()))
MESH = pl.DeviceIdType.MESH
ANY = pl.BlockSpec(memory_space=pl.ANY)
HBM_SPEC = pl.BlockSpec(memory_space=pltpu.HBM)
SEM_SPEC = pl.BlockSpec(memory_space=pltpu.SEMAPHORE)
DATAFLOW = pltpu.SideEffectType.DATAFLOW_SIDE_EFFECTING


def _call(body, **kw):
    return pl.pallas_call(body, **kw)


def _params(sem=None, vmem_mb=48):
    return pltpu.CompilerParams(dimension_semantics=sem, vmem_limit_bytes=vmem_mb * MIB)


def _rstd(v):
    return lax.rsqrt(jnp.mean(v * v, axis=-1, keepdims=True) + EPS)


def _dot(a, b):
    return jnp.dot(a, b, preferred_element_type=F32)


def _dot_nt(a, b):
    return lax.dot_general(a, b, NT_DIMS, preferred_element_type=F32)


def _dot_tn(a, b):
    return lax.dot_general(a, b, TN_DIMS, preferred_element_type=F32)


def _row_tile(rows, cap):
    t = min(rows, cap)
    t -= t % 8
    while rows % t:
        t -= 8
    return t


def _full(shape):
    nd = len(shape)
    return pl.BlockSpec(shape, lambda *_: (0,) * nd)


def _chip_peers():
    x, y, c = lax.axis_index("x"), lax.axis_index("y"), lax.axis_index("c")
    return x, y, c, [(1 - x, y), (x, 1 - y), (1 - x, 1 - y)]


def _cast_bf16(a, name):
    def body(a_ref, o_ref):
        o_ref[...] = a_ref[...].astype(BF16)

    return _call(body, name=name, out_shape=jax.ShapeDtypeStruct(a.shape, BF16),
                 grid=(1,), in_specs=[_full(a.shape)], out_specs=_full(a.shape),
                 compiler_params=_params(("arbitrary",)))(a)


def _place_own(srcs, own_slice, dst_slot, name):
    n = len(srcs)

    def body(*refs):
        ins, outs, sem = refs[:n], refs[n:2 * n], refs[2 * n]
        b = 2 * lax.axis_index("x") + lax.axis_index("y")
        copies = [pltpu.make_async_copy(ins[t].at[b] if own_slice else ins[t],
                                        outs[t].at[b if dst_slot is None else dst_slot], sem.at[t])
                  for t in range(n)]
        for cp in copies:
            cp.start()
        for cp in copies:
            cp.wait()

    shapes = [a.shape if own_slice else (N_SHARD,) + a.shape for a in srcs]
    return _call(body, name=name,
                 out_shape=[jax.ShapeDtypeStruct(shp, a.dtype) for shp, a in zip(shapes, srcs)],
                 in_specs=[ANY] * n, out_specs=[ANY] * n,
                 scratch_shapes=[pltpu.SemaphoreType.DMA((n,))])(*srcs)


def _exchange_copies(srcs, lands, send, recv, own_slice):
    x, y, c, chips = _chip_peers()
    copies = []
    for t in range(len(srcs)):
        for k, (px, py) in enumerate(chips):
            copies.append(pltpu.make_async_remote_copy(
                src_ref=srcs[t].at[2 * px + py] if own_slice else srcs[t],
                dst_ref=lands[t].at[k] if own_slice else lands[t].at[2 * x + y],
                send_sem=send.at[3 * t + k], recv_sem=recv.at[3 * t + k],
                device_id=(px, py, c), device_id_type=MESH))
    return copies


def _exchange_start(srcs, lands, own_slice, name):
    n = len(srcs)

    def body(*refs):
        for cp in _exchange_copies(refs[:n], refs[n:2 * n], refs[2 * n], refs[2 * n + 1], own_slice):
            cp.start()
        refs[-1][...] = jnp.zeros_like(refs[-1])

    operands = list(srcs) + list(lands)
    res = _call(
        body, name=name,
        out_shape=[pltpu.SemaphoreType.DMA((3 * n,)), pltpu.SemaphoreType.DMA((3 * n,))]
        + [pltpu.HBM(a.shape, a.dtype) for a in operands] + [jax.ShapeDtypeStruct((8, 128), F32)],
        in_specs=[HBM_SPEC] * (2 * n),
        out_specs=[SEM_SPEC, SEM_SPEC] + [HBM_SPEC] * (2 * n) + [pl.BlockSpec(memory_space=pltpu.VMEM)],
        input_output_aliases={j: j + 2 for j in range(2 * n)},
        compiler_params=pltpu.CompilerParams(has_side_effects=DATAFLOW),
    )(*[pltpu.with_memory_space_constraint(a, pltpu.HBM) for a in operands])
    return res[0], res[1], res[2:2 + n], res[2 + n:2 + 2 * n], res[-1]


def _exchange_wait(send, recv, srcs, lands, after, own_slice, name):
    n = len(srcs)

    def body(*refs):
        for cp in _exchange_copies(refs[:n], refs[n:2 * n], refs[2 * n], refs[2 * n + 1], own_slice):
            cp.wait_send()
            cp.wait_recv()

    operands = list(srcs) + list(lands)
    res = _call(
        body, name=name, out_shape=[pltpu.HBM(a.shape, a.dtype) for a in operands],
        in_specs=[HBM_SPEC] * (2 * n) + [SEM_SPEC, SEM_SPEC, ANY], out_specs=[HBM_SPEC] * (2 * n),
        input_output_aliases={j: j for j in range(2 * n)},
        compiler_params=pltpu.CompilerParams(has_side_effects=DATAFLOW),
    )(*operands, send, recv, after)
    return res[n:]


def _gather_shards(shards):
    n = len(shards)

    def body(*refs):
        ins, outs = refs[:n], refs[n:2 * n]
        send, recv, loc = refs[2 * n:]
        x, y, c, chips = _chip_peers()
        b = 2 * x + y
        local = [pltpu.make_async_copy(ins[t], outs[t].at[b], loc.at[t]) for t in range(n)]
        for cp in local:
            cp.start()
        remote = []
        for t in range(n):
            for k, (px, py) in enumerate(chips):
                remote.append(pltpu.make_async_remote_copy(
                    src_ref=ins[t], dst_ref=outs[t].at[b],
                    send_sem=send.at[3 * t + k], recv_sem=recv.at[3 * t + k],
                    device_id=(px, py, c), device_id_type=MESH))
        for cp in remote:
            cp.start()
        for cp in remote:
            cp.wait()
        for cp in local:
            cp.wait()

    return _call(
        body, name="gather_w_in",
        out_shape=[jax.ShapeDtypeStruct((N_SHARD,) + s.shape, s.dtype) for s in shards],
        in_specs=[ANY] * n, out_specs=[ANY] * n,
        scratch_shapes=[pltpu.SemaphoreType.DMA((3 * n,)), pltpu.SemaphoreType.DMA((3 * n,)),
                        pltpu.SemaphoreType.DMA((n,))],
    )(*shards)


def _scatter_grads(grads, small):
    n = len(grads)

    def body(*refs):
        ins, small_in = refs[:n], refs[n]
        outs, small_out = refs[n + 1:2 * n + 1], refs[2 * n + 1]
        send, recv, loc, ssend, srecv = refs[2 * n + 2:]
        x, y, c, chips = _chip_peers()
        b = 2 * x + y
        me = 4 * x + 2 * y + c
        local = [pltpu.make_async_copy(ins[t].at[b], outs[t].at[3], loc.at[t]) for t in range(n)]
        local.append(pltpu.make_async_copy(small_in, small_out.at[me], loc.at[n]))
        for cp in local:
            cp.start()
        remote = []
        for t in range(n):
            for k, (px, py) in enumerate(chips):
                remote.append(pltpu.make_async_remote_copy(
                    src_ref=ins[t].at[2 * px + py], dst_ref=outs[t].at[k],
                    send_sem=send.at[3 * t + k], recv_sem=recv.at[3 * t + k],
                    device_id=(px, py, c), device_id_type=MESH))
        for r in range(1, 8):
            px = 1 - x if r & 4 else x
            py = 1 - y if r & 2 else y
            pc = 1 - c if r & 1 else c
            remote.append(pltpu.make_async_remote_copy(
                src_ref=small_in, dst_ref=small_out.at[me],
                send_sem=ssend.at[r - 1], recv_sem=srecv.at[r - 1],
                device_id=(px, py, pc), device_id_type=MESH))
        for cp in remote:
            cp.start()
        for cp in remote:
            cp.wait()
        for cp in local:
            cp.wait()

    out_shape = [jax.ShapeDtypeStruct(g.shape, g.dtype) for g in grads]
    out_shape.append(jax.ShapeDtypeStruct((8,) + small.shape, small.dtype))
    res = _call(
        body, name="scatter_grads", out_shape=out_shape,
        in_specs=[ANY] * (n + 1), out_specs=[ANY] * (n + 1),
        scratch_shapes=[pltpu.SemaphoreType.DMA((3 * n,)), pltpu.SemaphoreType.DMA((3 * n,)),
                        pltpu.SemaphoreType.DMA((n + 1,)),
                        pltpu.SemaphoreType.DMA((7,)), pltpu.SemaphoreType.DMA((7,))],
    )(*grads, small)
    return res[:n], res[n]


def _swap_with_sibling(parts):
    n = len(parts)

    def body(*refs):
        ins, outs = refs[:n], refs[n:2 * n]
        send, recv = refs[2 * n:]
        x, y, c = lax.axis_index("x"), lax.axis_index("y"), lax.axis_index("c")
        copies = [pltpu.make_async_remote_copy(
            src_ref=ins[t], dst_ref=outs[t], send_sem=send.at[t], recv_sem=recv.at[t],
            device_id=(x, y, 1 - c), device_id_type=MESH) for t in range(n)]
        for cp in copies:
            cp.start()
        for cp in copies:
            cp.wait()

    return _call(
        body, name="swap_sibling",
        out_shape=[jax.ShapeDtypeStruct(p.shape, p.dtype) for p in parts],
        in_specs=[ANY] * n, out_specs=[ANY] * n,
        scratch_shapes=[pltpu.SemaphoreType.DMA((n,)), pltpu.SemaphoreType.DMA((n,))],
    )(*parts)


def _sum_slots(buf, order, name):
    k, rows, cols = buf.shape
    tr = _row_tile(rows, 256)

    def body(b_ref, o_ref):
        acc = b_ref[order[0]].astype(F32)
        for s in order[1:]:
            acc = acc + b_ref[s].astype(F32)
        o_ref[...] = acc

    return _call(body, name=name, out_shape=jax.ShapeDtypeStruct((rows, cols), F32),
                 grid=(rows // tr,),
                 in_specs=[pl.BlockSpec((k, tr, cols), lambda i: (0, i, 0))],
                 out_specs=pl.BlockSpec((tr, cols), lambda i: (i, 0)),
                 compiler_params=_params(("parallel",)))(buf)


def _adamw(grad_parts, w, m, v, name):
    rows, cols = w.shape
    tr = _row_tile(rows, 256)
    npart = len(grad_parts)

    def body(*refs):
        gp = refs[:npart]
        w_ref, m_ref, v_ref, g_out, d_out, m_out, v_out = refs[npart:]
        g = gp[0][...]
        for p in gp[1:]:
            g = g + p[...]
        mm = ADAM_B1 * m_ref[...] + (1.0 - ADAM_B1) * g
        vv = ADAM_B2 * v_ref[...] + (1.0 - ADAM_B2) * jnp.square(g)
        m_hat = mm / (1.0 - ADAM_B1 ** ADAM_STEP)
        v_hat = vv / (1.0 - ADAM_B2 ** ADAM_STEP)
        g_out[...] = g
        d_out[...] = -ADAM_LR * (m_hat / (jnp.sqrt(v_hat) + ADAM_EPS) + ADAM_WD * w_ref[...])
        m_out[...] = mm
        v_out[...] = vv

    spec = pl.BlockSpec((tr, cols), lambda i: (i, 0))
    shp = jax.ShapeDtypeStruct((rows, cols), F32)
    return _call(body, name=name, out_shape=[shp] * 4, grid=(rows // tr,),
                 in_specs=[spec] * (npart + 3), out_specs=[spec] * 4,
                 compiler_params=_params(("parallel",)))(*grad_parts, w, m, v)


def _in_proj(x, g1, w_in):
    s = x.shape[0]
    ts = 512

    def body(x_ref, g_ref, w_ref, u_ref, q_ref, k_ref, v_ref, vt_ref, h_ref):
        xv = x_ref[...]
        h = (xv * _rstd(xv) * g_ref[...]).astype(BF16)
        h_ref[...] = h
        u_ref[...] = _dot(h, w_ref[0])
        q_ref[...] = (_dot(h, w_ref[1]) * Q_SCALE).astype(BF16)
        k_ref[...] = _dot(h, w_ref[2]).astype(BF16)
        v = _dot(h, w_ref[3])
        v_ref[...] = v.astype(BF16)
        vt = v.T.astype(BF16)
        for n in range(ts // ATT_BLOCK):
            vt_ref[n] = vt[:, n * ATT_BLOCK:(n + 1) * ATT_BLOCK]

    row = lambda w: pl.BlockSpec((ts, w), lambda i: (i, 0))
    half = jax.ShapeDtypeStruct((s, D_POOL), BF16)
    return _call(
        body, name="in_proj",
        out_shape=[jax.ShapeDtypeStruct((s, D_POOL), F32), half, half, half,
                   jax.ShapeDtypeStruct((s // ATT_BLOCK, D_ATTN, ATT_BLOCK), BF16),
                   jax.ShapeDtypeStruct((s, D_MODEL), BF16)],
        grid=(s // ts,),
        in_specs=[row(D_MODEL), _full((1, D_MODEL)), _full(w_in.shape)],
        out_specs=[row(D_POOL)] * 4
        + [pl.BlockSpec((ts // ATT_BLOCK, D_ATTN, ATT_BLOCK), lambda i: (i, 0, 0)), row(D_MODEL)],
        compiler_params=_params(("parallel",)))(x, g1, w_in)


def _pool_means(ext_ref, g, window, ts, row0):
    cols = slice(g * POOL_GROUP, (g + 1) * POOL_GROUP)
    cur = ext_ref[POOL_HALO:POOL_HALO + ts, cols]
    acc = cur
    for d in range(1, window):
        acc = acc + ext_ref[POOL_HALO - d:POOL_HALO - d + ts, cols]
    t1 = row0 + 1 + lax.broadcasted_iota(jnp.int32, (ts, 1), 0)
    cnt = jnp.minimum(t1, window).astype(F32)
    return acc / cnt - cur, cnt


def _pool_fwd(u, w_pool, pool_scale):
    s = u.shape[0]
    ts = 512
    per = ts // POOL_HALO

    def body(u_ref, halo_ref, wp_ref, ps_ref, o_ref, ext_ref, y_ref):
        i = pl.program_id(0)
        ext_ref[0:POOL_HALO, :] = jnp.where(i > 0, halo_ref[...], 0.0)
        ext_ref[POOL_HALO:, :] = u_ref[...]
        for g, window in enumerate(POOL_WINDOWS):
            p, _ = _pool_means(ext_ref, g, window, ts, i * ts)
            y_ref[:, g * POOL_GROUP:(g + 1) * POOL_GROUP] = _dot(
                p.astype(BF16), wp_ref[g].astype(BF16))
        y = y_ref[...]
        o_ref[...] = (y * _rstd(y) * ps_ref[...]).astype(BF16)

    return _call(
        body, name="pool_fwd", out_shape=jax.ShapeDtypeStruct((s, D_POOL), BF16),
        grid=(s // ts,),
        in_specs=[pl.BlockSpec((ts, D_POOL), lambda i: (i, 0)),
                  pl.BlockSpec((POOL_HALO, D_POOL), lambda i: (jnp.maximum(i * per - 1, 0), 0)),
                  _full(w_pool.shape), _full((1, D_POOL))],
        out_specs=pl.BlockSpec((ts, D_POOL), lambda i: (i, 0)),
        scratch_shapes=[pltpu.VMEM((ts + POOL_HALO, D_POOL), F32), pltpu.VMEM((ts, D_POOL), F32)],
        compiler_params=_params(("parallel",)))(u, u, w_pool, pool_scale)


def _tri(kind):
    r = lax.broadcasted_iota(jnp.int32, (ATT_BLOCK, ATT_BLOCK), 0)
    c = lax.broadcasted_iota(jnp.int32, (ATT_BLOCK, ATT_BLOCK), 1)
    return jnp.where(r >= c if kind == "suffix" else r <= c, 1.0, 0.0).astype(BF16)


def _causal_mask():
    r = lax.broadcasted_iota(jnp.int32, (ATT_BLOCK, ATT_BLOCK), 0)
    c = lax.broadcasted_iota(jnp.int32, (ATT_BLOCK, ATT_BLOCK), 1)
    return c < r


def _softplus(z, with_sigmoid=False):
    ope = 1.0 + jnp.exp(jnp.minimum(z, 80.0))
    sp = jnp.maximum(z, jnp.log(ope))
    if with_sigmoid:
        return sp, 1.0 - 1.0 / ope
    return sp


def _attn_fwd(q, k, vt):
    s = q.shape[0]
    tb = ATT_BLOCK
    nq = s // tb

    def body(q_ref, k_ref, vt_ref, o_ref, t_ref, *bufs):
        i = pl.program_id(1)
        suffix = _tri("prefix")
        r_idx = lax.broadcasted_iota(jnp.int32, (tb, tb), 0)
        c_idx = lax.broadcasted_iota(jnp.int32, (tb, tb), 1)
        causal = r_idx < c_idx
        lane = lax.broadcasted_iota(jnp.int32, (1, HEAD_PAIR), 1)
        first = lane < 64
        top = lax.broadcasted_iota(jnp.int32, (HEAD_PAIR, 1), 0) < 64
        q2 = q_ref[...]
        zero = jnp.zeros_like(q2)
        qs = (jnp.where(first, q2, zero), jnp.where(first, zero, q2))

        def values_t(j):
            vt = vt_ref[j]
            none = jnp.zeros_like(vt)
            return jnp.concatenate([jnp.where(top, vt, none), jnp.where(top, none, vt)], axis=1)

        def scores(j, cs, masked):
            kj = k_ref[pl.ds(pl.multiple_of(j * tb, tb), tb), :]
            new_cs, args = [], []
            for e in range(2):
                z = _dot_nt(kj, qs[e])
                sp = _softplus(z)
                if masked:
                    sp = jnp.where(causal, sp, 0.0)
                incl = _dot(suffix, sp.astype(BF16))
                arg = z - incl - cs[e]
                if masked:
                    arg = jnp.where(causal, arg, NEG_BIG)
                args.append(arg)
                new_cs.append(cs[e] + incl[0:1, :])
            return new_cs, args

        def weigh(j, args, o):
            probs = [jnp.exp(arg).astype(BF16) for arg in args]
            return o + _dot(values_t(j), jnp.concatenate(probs, axis=0))

        z_buf, zc_buf, in_buf = [[[bufs[4 * kind + 2 * slot + e] for e in range(2)]
                                  for slot in range(2)] for kind in range(3)]
        pr_buf = [bufs[12], bufs[13]]

        @pl.when((pl.program_id(0) == 0) & (i == 0))
        def _():
            for b in bufs:
                b[...] = jnp.zeros_like(b)

        def block_of(p):
            return jnp.clip(i - 1 - p, 0, nq - 1)

        def trip(t, w, carry):
            r = 1 - w
            cs, o = list(carry[0:2]), carry[2]
            live3 = (t - 3 >= 0) & (t - 3 < i)
            o = o + jnp.where(live3, _dot(values_t(block_of(t - 3)), pr_buf[r][...]), 0.0)
            kj = k_ref[pl.ds(pl.multiple_of(block_of(t) * tb, tb), tb), :]
            for e in range(2):
                z_buf[w][e][...] = _dot_nt(kj, qs[e])
            for e in range(2):
                z = z_buf[r][e][...]
                zc_buf[w][e][...] = z
                in_buf[w][e][...] = _dot(suffix, _softplus(z).astype(BF16))
            live2 = (t - 2 >= 0) & (t - 2 < i)
            for e in range(2):
                incl = in_buf[r][e][...]
                arg = zc_buf[r][e][...] - incl - jnp.where(live2, cs[e], -NEG_BIG)
                pr_buf[w][e * tb:(e + 1) * tb, :] = jnp.exp(arg).astype(BF16)
                cs[e] = jnp.where(live2, cs[e] + incl[0:1, :], cs[e])
            return cs[0], cs[1], o

        def four_trips(n, cr):
            for u in range(4):
                cr = trip(4 * n + u, u % 2, cr)
            return cr

        row = jnp.zeros((1, tb), F32)
        cs, args = scores(i, (row, row), True)
        carry = (cs[0], cs[1], weigh(i, args, jnp.zeros((HEAD_PAIR, tb), F32)))
        carry = lax.fori_loop(0, jnp.where(i > 0, (i + 6) // 4, 0), four_trips, carry)
        o_ref[...] = carry[2].T
        totals = jnp.where(r_idx == 0, carry[0], jnp.where(r_idx == 1, carry[1], 0.0))
        t_ref[...] = totals.T[:, 0:2]

    score_buf = pltpu.VMEM((tb, tb), F32)
    return _call(
        body, name="attn_fwd",
        out_shape=[jax.ShapeDtypeStruct((s, D_ATTN), F32),
                   jax.ShapeDtypeStruct((4, s, 2), F32)],
        grid=(4, nq),
        in_specs=[pl.BlockSpec((tb, HEAD_PAIR), lambda h, i: (i, h)),
                  pl.BlockSpec((s, HEAD_PAIR), lambda h, i: (0, h)),
                  pl.BlockSpec((nq, HEAD_PAIR, tb), lambda h, i: (0, h, 0))],
        out_specs=[pl.BlockSpec((tb, HEAD_PAIR), lambda h, i: (i, h)),
                   pl.BlockSpec((None, tb, 2), lambda h, i: (h, i, 0))],
        scratch_shapes=[score_buf] * 12 + [pltpu.VMEM((2 * tb, tb), BF16)] * 2,
        compiler_params=_params(("arbitrary", "arbitrary")))(q, k, vt)


def _mix_out(attn, mpool, x, attn_scale, w_out, g2, g3):
    s = x.shape[0]
    ts = 512

    def body(a_ref, p_ref, x_ref, as_ref, w_ref, g2_ref, g3_ref, ma_ref, mix_ref, x1_ref, h2_ref):
        ao = a_ref[...]
        ma = (ao * _rstd(ao) * as_ref[...]).astype(BF16)
        ma_ref[...] = ma
        mix = _dot(p_ref[...], w_ref[0:D_POOL, :]) + _dot(ma, w_ref[D_POOL:, :])
        mix_ref[...] = mix
        x1 = x_ref[...] + mix * _rstd(mix) * g2_ref[...]
        x1_ref[...] = x1
        h2_ref[...] = (x1 * _rstd(x1) * g3_ref[...]).astype(BF16)

    row = lambda w: pl.BlockSpec((ts, w), lambda i: (i, 0))
    return _call(
        body, name="mix_out",
        out_shape=[jax.ShapeDtypeStruct((s, D_ATTN), BF16), jax.ShapeDtypeStruct((s, D_MODEL), F32),
                   jax.ShapeDtypeStruct((s, D_MODEL), F32), jax.ShapeDtypeStruct((s, D_MODEL), BF16)],
        grid=(s // ts,),
        in_specs=[row(D_ATTN), row(D_POOL), row(D_MODEL), _full((1, D_ATTN)),
                  _full((D_MODEL, D_MODEL)), _full((1, D_MODEL)), _full((1, D_MODEL))],
        out_specs=[row(D_ATTN), row(D_MODEL), row(D_MODEL), row(D_MODEL)],
        compiler_params=_params(("parallel",)))(attn, mpool, x, attn_scale, w_out, g2, g3)


def _conv_rows(ext_ref, cw, cb, ts):
    y = cb + cw[0:1, :] * ext_ref[CONV_HALO - 2:CONV_HALO - 2 + ts, :]
    y = y + cw[1:2, :] * ext_ref[CONV_HALO - 1:CONV_HALO - 1 + ts, :]
    return y + cw[2:3, :] * ext_ref[CONV_HALO:CONV_HALO + ts, :]


def _sigmoid(v):
    return 1.0 / (1.0 + jnp.exp(-v))


def _ffn_up(h2, w_up, conv_w, conv_b):
    s = h2.shape[0]
    ts = 256
    tn = FF_TILE

    def body(h_ref, wg_ref, wv_ref, cwg_ref, cwv_ref, cbg_ref, cbv_ref,
             ug_ref, uv_ref, f_ref, extg, extv):
        i = pl.program_id(1)

        @pl.when(i == 0)
        def _():
            extg[0:CONV_HALO, :] = jnp.zeros((CONV_HALO, tn), F32)
            extv[0:CONV_HALO, :] = jnp.zeros((CONV_HALO, tn), F32)

        h = h_ref[...]
        ug = _dot(h, wg_ref[...])
        uv = _dot(h, wv_ref[...])
        ug_ref[...] = ug
        uv_ref[...] = uv
        extg[CONV_HALO:, :] = ug
        extv[CONV_HALO:, :] = uv
        gate = _conv_rows(extg, cwg_ref[...], cbg_ref[...], ts)
        val = _conv_rows(extv, cwv_ref[...], cbv_ref[...], ts)
        f_ref[...] = (gate * _sigmoid(gate) * val).astype(BF16)
        extg[0:CONV_HALO, :] = extg[ts:ts + CONV_HALO, :]
        extv[0:CONV_HALO, :] = extv[ts:ts + CONV_HALO, :]

    out_blk = pl.BlockSpec((None, ts, tn), lambda n, i: (n, i, 0))
    act = jax.ShapeDtypeStruct((2, s, tn), F32)
    return _call(
        body, name="ffn_up",
        out_shape=[act, act, jax.ShapeDtypeStruct((2, s, tn), BF16)],
        grid=(2, s // ts),
        in_specs=[pl.BlockSpec((ts, D_MODEL), lambda n, i: (i, 0)),
                  pl.BlockSpec((None, D_MODEL, tn), lambda n, i: (n, 0, 0)),
                  pl.BlockSpec((None, D_MODEL, tn), lambda n, i: (n + 2, 0, 0)),
                  pl.BlockSpec((None, 3, tn), lambda n, i: (n, 0, 0)),
                  pl.BlockSpec((None, 3, tn), lambda n, i: (n + 2, 0, 0)),
                  pl.BlockSpec((None, 1, tn), lambda n, i: (n, 0, 0)),
                  pl.BlockSpec((None, 1, tn), lambda n, i: (n + 2, 0, 0))],
        out_specs=[out_blk, out_blk, out_blk],
        scratch_shapes=[pltpu.VMEM((ts + CONV_HALO, tn), F32), pltpu.VMEM((ts + CONV_HALO, tn), F32)],
        compiler_params=_params(("arbitrary", "arbitrary")))(
            h2, w_up, w_up, conv_w, conv_w, conv_b, conv_b)


def _ffn_down(f_in, w_down, x1, target, g4):
    s = x1.shape[0]
    ts = 512

    def body(f_ref, w_ref, x1_ref, t_ref, g_ref, df_ref, dy_ref, loss_ref, dg_ref):
        @pl.when(pl.program_id(0) == 0)
        def _():
            loss_ref[...] = jnp.zeros_like(loss_ref)
            dg_ref[...] = jnp.zeros_like(dg_ref)

        f = _dot(f_ref[0], w_ref[0:FF_TILE, :]) + _dot(f_ref[1], w_ref[FF_TILE:, :])
        rf = _rstd(f)
        fn = f * rf
        g = g_ref[...]
        err = (x1_ref[...] + fn * g) - t_ref[...]
        loss_ref[...] += 0.5 * jnp.sum(jnp.mean(err * err, axis=-1))
        dy = err * (1.0 / D_MODEL)
        dy_ref[...] = dy
        dg_ref[...] += jnp.sum(dy * fn, axis=0, keepdims=True)
        dfn = dy * g
        df_ref[...] = (rf * (dfn - fn * jnp.mean(dfn * fn, axis=-1, keepdims=True))).astype(BF16)

    row = pl.BlockSpec((ts, D_MODEL), lambda i: (i, 0))
    return _call(
        body, name="ffn_down",
        out_shape=[jax.ShapeDtypeStruct((s, D_MODEL), BF16), jax.ShapeDtypeStruct((s, D_MODEL), F32),
                   jax.ShapeDtypeStruct((8, 128), F32), jax.ShapeDtypeStruct((1, D_MODEL), F32)],
        grid=(s // ts,),
        in_specs=[pl.BlockSpec((2, ts, FF_TILE), lambda i: (0, i, 0)), _full((D_FF, D_MODEL)),
                  row, row, _full((1, D_MODEL))],
        out_specs=[row, row, _full((8, 128)), _full((1, D_MODEL))],
        compiler_params=_params(("arbitrary",)))(f_in, w_down, x1, target, g4)


def _tn_matmul(a, b, name, ts=512):
    na, s, ka = a.shape
    nb, _, nbc = b.shape
    steps = s // ts

    def body(a_ref, b_ref, o_ref, acc_ref):
        @pl.when(pl.program_id(2) == 0)
        def _():
            acc_ref[...] = jnp.zeros_like(acc_ref)

        acc_ref[...] += _dot_tn(a_ref[...].astype(BF16), b_ref[...].astype(BF16))

        @pl.when(pl.program_id(2) == steps - 1)
        def _():
            o_ref[...] = acc_ref[...].astype(BF16)

    return _call(
        body, name=name, out_shape=jax.ShapeDtypeStruct((na, nb, ka, nbc), BF16),
        grid=(na, nb, steps),
        in_specs=[pl.BlockSpec((None, ts, ka), lambda i, j, r: (i, r, 0)),
                  pl.BlockSpec((None, ts, nbc), lambda i, j, r: (j, r, 0))],
        out_specs=pl.BlockSpec((None, None, ka, nbc), lambda i, j, r: (i, j, 0, 0)),
        scratch_shapes=[pltpu.VMEM((ka, nbc), F32)],
        compiler_params=_params(("parallel", "parallel", "arbitrary")))(a, b)


def _ffn_bwd_act(df, w_down, upre_g, upre_v, conv_w, conv_b):
    s = df.shape[0]
    ts = 256
    tn = FF_TILE
    nr = s // ts
    per = ts // CONV_HALO

    def body(df_ref, wd_ref, ug_ref, uv_ref, hg_ref, hv_ref, cwg_ref, cwv_ref, cbg_ref, cbv_ref,
             dug_ref, duv_ref, dcwg_ref, dcwv_ref, dcbg_ref, dcbv_ref, extg, extv, dxg, dxv):
        i = pl.program_id(1)
        first_rows = i == nr - 1

        @pl.when(i == 0)
        def _():
            dxg[ts:, :] = jnp.zeros((CONV_HALO, tn), F32)
            dxv[ts:, :] = jnp.zeros((CONV_HALO, tn), F32)
            for r in (dcwg_ref, dcwv_ref, dcbg_ref, dcbv_ref):
                r[...] = jnp.zeros_like(r)

        extg[0:CONV_HALO, :] = jnp.where(first_rows, 0.0, hg_ref[...])
        extv[0:CONV_HALO, :] = jnp.where(first_rows, 0.0, hv_ref[...])
        extg[CONV_HALO:, :] = ug_ref[...]
        extv[CONV_HALO:, :] = uv_ref[...]
        cwg, cwv = cwg_ref[...], cwv_ref[...]
        gate = _conv_rows(extg, cwg, cbg_ref[...], ts)
        val = _conv_rows(extv, cwv, cbv_ref[...], ts)
        sg = _sigmoid(gate)
        dfin = _dot_nt(df_ref[...], wd_ref[...])
        dval = dfin * (gate * sg)
        dgate = dfin * val * (sg * (1.0 + gate * (1.0 - sg)))

        def conv_bwd(dact, ext, dx, cw, dcw_ref, dcb_ref, du_ref):
            dx[0:ts, :] = dact
            dcb_ref[...] += jnp.sum(dact, axis=0, keepdims=True)
            for kk in range(3):
                lo = CONV_HALO - 2 + kk
                dcw_ref[kk:kk + 1, :] += jnp.sum(dact * ext[lo:lo + ts, :], axis=0, keepdims=True)
            du = cw[2:3, :] * dact + cw[1:2, :] * dx[1:1 + ts, :] + cw[0:1, :] * dx[2:2 + ts, :]
            du_ref[...] = du.astype(BF16)
            dx[ts:, :] = dx[0:CONV_HALO, :]

        conv_bwd(dgate, extg, dxg, cwg, dcwg_ref, dcbg_ref, dug_ref)
        conv_bwd(dval, extv, dxv, cwv, dcwv_ref, dcbv_ref, duv_ref)

    rows = lambda n, i: (n, nr - 1 - i, 0)
    halo = lambda n, i: (n, jnp.maximum((nr - 1 - i) * per - 1, 0), 0)
    act_blk = pl.BlockSpec((None, ts, tn), rows)
    halo_blk = pl.BlockSpec((None, CONV_HALO, tn), halo)
    cw_blk = lambda off: pl.BlockSpec((None, 3, tn), lambda n, i: (n + off, 0, 0))
    cb_blk = lambda off: pl.BlockSpec((None, 1, tn), lambda n, i: (n + off, 0, 0))
    acc_w = pl.BlockSpec((None, 3, tn), lambda n, i: (n, 0, 0))
    acc_b = pl.BlockSpec((None, 1, tn), lambda n, i: (n, 0, 0))
    dact = jax.ShapeDtypeStruct((2, s, tn), BF16)
    return _call(
        body, name="ffn_bwd_act",
        out_shape=[dact, dact, jax.ShapeDtypeStruct((2, 3, tn), F32), jax.ShapeDtypeStruct((2, 3, tn), F32),
                   jax.ShapeDtypeStruct((2, 1, tn), F32), jax.ShapeDtypeStruct((2, 1, tn), F32)],
        grid=(2, nr),
        in_specs=[pl.BlockSpec((ts, D_MODEL), lambda n, i: (nr - 1 - i, 0)),
                  pl.BlockSpec((tn, D_MODEL), lambda n, i: (n, 0)),
                  act_blk, act_blk, halo_blk, halo_blk,
                  cw_blk(0), cw_blk(2), cb_blk(0), cb_blk(2)],
        out_specs=[act_blk, act_blk, acc_w, acc_w, acc_b, acc_b],
        scratch_shapes=[pltpu.VMEM((ts + CONV_HALO, tn), F32)] * 4,
        compiler_params=_params(("arbitrary", "arbitrary")))(
            df, w_down, upre_g, upre_v, upre_g, upre_v, conv_w, conv_w, conv_b, conv_b)


def _ffn_bwd_in(dug, duv, w_up, x1, dy, mix, g3, g2):
    s = x1.shape[0]
    ts = 256

    def body(dg_ref, dv_ref, w_ref, x1_ref, dy_ref, mix_ref, g3_ref, g2_ref,
             dx1_ref, dmix_ref, dg3_ref, dg2_ref):
        @pl.when(pl.program_id(0) == 0)
        def _():
            dg3_ref[...] = jnp.zeros_like(dg3_ref)
            dg2_ref[...] = jnp.zeros_like(dg2_ref)

        dh = _dot_nt(dg_ref[0], w_ref[0]) + _dot_nt(dg_ref[1], w_ref[1])
        dh = dh + _dot_nt(dv_ref[0], w_ref[2]) + _dot_nt(dv_ref[1], w_ref[3])
        x1 = x1_ref[...]
        r3 = _rstd(x1)
        xn = x1 * r3
        dg3_ref[...] += jnp.sum(dh * xn, axis=0, keepdims=True)
        dxn = dh * g3_ref[...]
        dx1 = dy_ref[...] + r3 * (dxn - xn * jnp.mean(dxn * xn, axis=-1, keepdims=True))
        dx1_ref[...] = dx1
        mix = mix_ref[...]
        rm = _rstd(mix)
        mn = mix * rm
        dg2_ref[...] += jnp.sum(dx1 * mn, axis=0, keepdims=True)
        dmn = dx1 * g2_ref[...]
        dmix_ref[...] = (rm * (dmn - mn * jnp.mean(dmn * mn, axis=-1, keepdims=True))).astype(BF16)

    row = pl.BlockSpec((ts, D_MODEL), lambda i: (i, 0))
    act = pl.BlockSpec((2, ts, FF_TILE), lambda i: (0, i, 0))
    vec = _full((1, D_MODEL))
    return _call(
        body, name="ffn_bwd_in",
        out_shape=[jax.ShapeDtypeStruct((s, D_MODEL), F32), jax.ShapeDtypeStruct((s, D_MODEL), BF16),
                   jax.ShapeDtypeStruct((1, D_MODEL), F32), jax.ShapeDtypeStruct((1, D_MODEL), F32)],
        grid=(s // ts,),
        in_specs=[act, act, _full(w_up.shape), row, row, row, vec, vec],
        out_specs=[row, row, vec, vec],
        compiler_params=_params(("arbitrary",), vmem_mb=56))(dug, duv, w_up, x1, dy, mix, g3, g2)


def _mix_bwd(dmix, w_out, attn, attn_scale):
    s = dmix.shape[0]
    ts = 512

    def body(dm_ref, w_ref, a_ref, as_ref, dp_ref, do_ref, das_ref):
        @pl.when(pl.program_id(0) == 0)
        def _():
            das_ref[...] = jnp.zeros_like(das_ref)

        dm = dm_ref[...]
        dp_ref[...] = _dot_nt(dm, w_ref[0:D_POOL, :])
        da = _dot_nt(dm, w_ref[D_POOL:, :])
        ao = a_ref[...]
        ra = _rstd(ao)
        an = ao * ra
        das_ref[...] += jnp.sum(da * an, axis=0, keepdims=True)
        dan = da * as_ref[...]
        do_ref[...] = (ra * (dan - an * jnp.mean(dan * an, axis=-1, keepdims=True))).astype(BF16)

    row = lambda w: pl.BlockSpec((ts, w), lambda i: (i, 0))
    return _call(
        body, name="mix_bwd",
        out_shape=[jax.ShapeDtypeStruct((s, D_POOL), F32), jax.ShapeDtypeStruct((s, D_ATTN), BF16),
                   jax.ShapeDtypeStruct((1, D_ATTN), F32)],
        grid=(s // ts,),
        in_specs=[row(D_MODEL), _full((D_MODEL, D_MODEL)), row(D_ATTN), _full((1, D_ATTN))],
        out_specs=[row(D_POOL), row(D_ATTN), _full((1, D_ATTN))],
        compiler_params=_params(("arbitrary",)))(dmix, w_out, attn, attn_scale)


def _attn_bwd(q, k, v, do, totals):
    s = q.shape[0]
    tb = ATT_BLOCK
    nq = s // tb

    def body(q_ref, do_ref, t_ref, k_hbm, v_hbm, dq_ref, dk_hbm, dv_hbm,
             k_scr, v_scr, dkt_acc, dvt_acc, stage, *bufs):
        hp = pl.program_id(0)
        i = pl.program_id(1)
        lanes = pl.ds(pl.multiple_of(hp * HEAD_PAIR, HEAD_PAIR), HEAD_PAIR)

        @pl.when(i == 0)
        def _():
            pltpu.sync_copy(k_hbm.at[:, lanes], k_scr)
            pltpu.sync_copy(v_hbm.at[:, lanes], v_scr)
            dkt_acc[...] = jnp.zeros_like(dkt_acc)
            dvt_acc[...] = jnp.zeros_like(dvt_acc)

        upper = _tri("suffix")
        lower = _tri("prefix")
        causal = _causal_mask()
        lane = lax.broadcasted_iota(jnp.int32, (1, HEAD_PAIR), 1)
        first = lane < 64
        q2 = q_ref[...]
        do2 = do_ref[...]
        zero = jnp.zeros_like(q2)
        qs = (jnp.where(first, q2, zero), jnp.where(first, zero, q2))
        dos = (jnp.where(first, do2, zero), jnp.where(first, zero, do2))
        qcat_t = jnp.concatenate(qs, axis=0).astype(F32).T.astype(BF16)
        docat_t = jnp.concatenate(dos, axis=0).astype(F32).T.astype(BF16)
        tots = (t_ref[:, 0:1], t_ref[:, 1:2])

        z_buf, zc_buf, sg_buf, sg2_buf, in_buf, da_buf, dw_buf, pre_buf = [
            [[bufs[4 * kind + 2 * slot + e] for e in range(2)] for slot in range(2)]
            for kind in range(8)]
        pr_buf, dzr_buf, dzc_buf = bufs[32:34], bufs[34:36], bufs[36:38]

        for e in range(2):
            z_buf[1][e][...] = jnp.full((tb, tb), NEG_BIG, F32)
            zc_buf[1][e][...] = jnp.full((tb, tb), NEG_BIG, F32)
            for buf in (sg_buf, sg2_buf, in_buf, da_buf, dw_buf, pre_buf):
                buf[1][e][...] = jnp.zeros((tb, tb), F32)
        for buf in (pr_buf, dzr_buf, dzc_buf):
            buf[1][...] = jnp.zeros_like(buf[1])

        def rows(p):
            return pl.ds(pl.multiple_of(jnp.clip(p, 0, nq - 1) * tb, tb), tb)

        def split_heads(block):
            return jnp.concatenate([jnp.where(first, block, zero), jnp.where(first, zero, block)], axis=0)

        def trip(t, w, carry):
            r = 1 - w
            cs, cps, dq = list(carry[0:2]), list(carry[2:4]), carry[4]
            live4 = (t - 4 >= 0) & (t - 4 < i)
            dq = dq + jnp.where(live4, _dot(dzc_buf[r][...], split_heads(k_scr[rows(t - 4), :])), 0.0)
            dkt_acc[jnp.clip(t - 4, 0, nq - 1)] += jnp.where(live4, _dot(qcat_t, dzr_buf[r][...]), 0.0)
            dvt_acc[jnp.clip(t - 3, 0, nq - 1)] += _dot(docat_t, pr_buf[r][...])
            kj = k_scr[rows(t), :]
            for e in range(2):
                z_buf[w][e][...] = _dot_nt(qs[e], kj)
            vj = v_scr[rows(t - 1), :]
            for e in range(2):
                z = z_buf[r][e][...]
                sp, sig = _softplus(z, True)
                zc_buf[w][e][...] = z
                sg_buf[w][e][...] = sig
                in_buf[w][e][...] = _dot(sp.astype(BF16), upper)
                da_buf[w][e][...] = _dot_nt(dos[e], vj)
            for e in range(2):
                incl = in_buf[r][e][...]
                cs[e] = cs[e] + incl[:, 0:1]
                off = jnp.where(t - 2 < i, tots[e] - cs[e], -NEG_BIG)
                a = jnp.exp(zc_buf[r][e][...] - incl - off)
                dw = a * da_buf[r][e][...]
                dw_buf[w][e][...] = dw
                sg2_buf[w][e][...] = sg_buf[r][e][...]
                pr_buf[w][e * tb:(e + 1) * tb, :] = a.astype(BF16)
                pre_buf[w][e][...] = _dot(dw.astype(BF16), lower)
            for e in range(2):
                pre = pre_buf[r][e][...] + cps[e]
                dzb = (dw_buf[r][e][...] - sg2_buf[r][e][...] * pre).astype(BF16)
                cps[e] = pre[:, tb - 1:tb]
                dzr_buf[w][e * tb:(e + 1) * tb, :] = dzb
                dzc_buf[w][:, e * tb:(e + 1) * tb] = dzb
            return cs[0], cs[1], cps[0], cps[1], dq

        col = jnp.zeros((tb, 1), F32)
        carry = (col, col, col, col, jnp.zeros((tb, HEAD_PAIR), F32))
        def four_trips(n, cr):
            for u in range(4):
                cr = trip(4 * n + u, u % 2, cr)
            return cr

        carry = lax.fori_loop(0, jnp.where(i > 0, (i + 7) // 4, 0), four_trips, carry)

        cps, dq = carry[2:4], carry[4]
        kj = k_scr[rows(i), :]
        vj = v_scr[rows(i), :]
        dzs, probs = [], []
        for e in range(2):
            z = _dot_nt(qs[e], kj)
            sp, sig = _softplus(z, True)
            incl = _dot(jnp.where(causal, sp, 0.0).astype(BF16), upper)
            a = jnp.where(causal, jnp.exp(z - incl), 0.0)
            dw = a * _dot_nt(dos[e], vj)
            pre = _dot(dw.astype(BF16), lower) + cps[e]
            dzs.append(jnp.where(causal, dw - sig * pre, 0.0).astype(BF16))
            probs.append(a.astype(BF16))
        dq = dq + _dot(jnp.concatenate(dzs, axis=1), split_heads(kj))
        dkt_acc[i] += _dot(qcat_t, jnp.concatenate(dzs, axis=0))
        dvt_acc[i] += _dot(docat_t, jnp.concatenate(probs, axis=0))
        dq_ref[...] = (dq * Q_SCALE).astype(BF16)

        @pl.when(i == nq - 1)
        def _():
            for acc, dst in ((dkt_acc, dk_hbm), (dvt_acc, dv_hbm)):
                def flip(n, _, acc=acc):
                    at = pl.ds(pl.multiple_of(n * tb, tb), tb)
                    stage[at, :] = acc[n].T
                    return 0
                lax.fori_loop(0, nq, flip, 0)
                pltpu.sync_copy(stage, dst.at[:, lanes])

    blk = pl.BlockSpec((tb, HEAD_PAIR), lambda h, i: (i, h))
    grad = jax.ShapeDtypeStruct((s, D_ATTN), F32)
    return _call(
        body, name="attn_bwd",
        out_shape=[jax.ShapeDtypeStruct((s, D_ATTN), BF16), grad, grad],
        grid=(4, nq),
        in_specs=[blk, blk, pl.BlockSpec((None, tb, 2), lambda h, i: (h, i, 0)), ANY, ANY],
        out_specs=[blk, ANY, ANY],
        scratch_shapes=[pltpu.VMEM((s, HEAD_PAIR), BF16), pltpu.VMEM((s, HEAD_PAIR), BF16),
                        pltpu.VMEM((nq, HEAD_PAIR, tb), F32), pltpu.VMEM((nq, HEAD_PAIR, tb), F32),
                        pltpu.VMEM((s, HEAD_PAIR), F32)]
        + [pltpu.VMEM((tb, tb), F32)] * 32
        + [pltpu.VMEM((2 * tb, tb), BF16)] * 4 + [pltpu.VMEM((tb, 2 * tb), BF16)] * 2,
        compiler_params=_params(("arbitrary", "arbitrary"), vmem_mb=60))(q, do, totals, k, v)


def _pool_bwd(u, dmp, w_pool, pool_scale):
    s = u.shape[0]
    ts = 512
    nr = s // ts
    per = ts // POOL_HALO

    def body(u_ref, halo_ref, dm_ref, wp_ref, ps_ref, du_ref, dwp_ref, dps_ref, ext_ref, y_ref, dext_ref):
        i = pl.program_id(0)
        rb = nr - 1 - i

        @pl.when(i == 0)
        def _():
            dext_ref[ts:, :] = jnp.zeros((POOL_HALO, D_POOL), F32)
            dwp_ref[...] = jnp.zeros_like(dwp_ref)
            dps_ref[...] = jnp.zeros_like(dps_ref)

        ext_ref[0:POOL_HALO, :] = jnp.where(rb > 0, halo_ref[...], 0.0)
        ext_ref[POOL_HALO:, :] = u_ref[...]
        ps, cnts = [], []
        for g, window in enumerate(POOL_WINDOWS):
            p, cnt = _pool_means(ext_ref, g, window, ts, rb * ts)
            ps.append(p.astype(BF16))
            cnts.append(cnt)
            y_ref[:, g * POOL_GROUP:(g + 1) * POOL_GROUP] = _dot(ps[g], wp_ref[g].astype(BF16))
        y = y_ref[...]
        r = _rstd(y)
        yn = y * r
        dm = dm_ref[...]
        dps_ref[...] += jnp.sum(dm * yn, axis=0, keepdims=True)
        dn = dm * ps_ref[...]
        dy = r * (dn - yn * jnp.mean(dn * yn, axis=-1, keepdims=True))
        for g, window in enumerate(POOL_WINDOWS):
            cols = slice(g * POOL_GROUP, (g + 1) * POOL_GROUP)
            dyg = dy[:, cols].astype(BF16)
            dwp_ref[g] += _dot_tn(ps[g], dyg)
            dp = _dot_nt(dyg, wp_ref[g].astype(BF16))
            dext_ref[0:ts, cols] = dp / cnts[g]
            acc = dext_ref[0:ts, cols]
            for d in range(1, window):
                acc = acc + dext_ref[d:d + ts, cols]
            du_ref[:, cols] = (acc - dp).astype(BF16)
        dext_ref[ts:, :] = dext_ref[0:POOL_HALO, :]

    rows = pl.BlockSpec((ts, D_POOL), lambda i: (nr - 1 - i, 0))
    return _call(
        body, name="pool_bwd",
        out_shape=[jax.ShapeDtypeStruct((s, D_POOL), BF16), jax.ShapeDtypeStruct(w_pool.shape, F32),
                   jax.ShapeDtypeStruct((1, D_POOL), F32)],
        grid=(nr,),
        in_specs=[rows,
                  pl.BlockSpec((POOL_HALO, D_POOL), lambda i: (jnp.maximum((nr - 1 - i) * per - 1, 0), 0)),
                  rows, _full(w_pool.shape), _full((1, D_POOL))],
        out_specs=[rows, _full(w_pool.shape), _full((1, D_POOL))],
        scratch_shapes=[pltpu.VMEM((ts + POOL_HALO, D_POOL), F32), pltpu.VMEM((ts, D_POOL), F32),
                        pltpu.VMEM((ts + POOL_HALO, D_POOL), F32)],
        compiler_params=_params(("arbitrary",)))(u, u, dmp, w_pool, pool_scale)


def _in_proj_bwd(du, dq, dk, dv, w_in, x, dx1, g1):
    s = x.shape[0]
    ts = 512

    def body(du_ref, dq_ref, dk_ref, dv_ref, w_ref, x_ref, dx1_ref, g_ref, gx_ref, dg_ref):
        @pl.when(pl.program_id(0) == 0)
        def _():
            dg_ref[...] = jnp.zeros_like(dg_ref)

        dh = _dot_nt(du_ref[...], w_ref[0]) + _dot_nt(dq_ref[...], w_ref[1])
        dh = dh + _dot_nt(dk_ref[...].astype(BF16), w_ref[2]) + _dot_nt(dv_ref[...].astype(BF16), w_ref[3])
        xv = x_ref[...]
        r = _rstd(xv)
        xn = xv * r
        dg_ref[...] += jnp.sum(dh * xn, axis=0, keepdims=True)
        dxn = dh * g_ref[...]
        gx_ref[...] = dx1_ref[...] + r * (dxn - xn * jnp.mean(dxn * xn, axis=-1, keepdims=True))

    row = lambda w: pl.BlockSpec((ts, w), lambda i: (i, 0))
    return _call(
        body, name="in_proj_bwd",
        out_shape=[jax.ShapeDtypeStruct((s, D_MODEL), F32), jax.ShapeDtypeStruct((1, D_MODEL), F32)],
        grid=(s // ts,),
        in_specs=[row(D_POOL)] * 4 + [_full(w_in.shape), row(D_MODEL), row(D_MODEL), _full((1, D_MODEL))],
        out_specs=[row(D_MODEL), _full((1, D_MODEL))],
        compiler_params=_params(("arbitrary",)))(du, dq, dk, dv, w_in, x, dx1, g1)


_SMALL = ("norm_mix_pre", "w_pool", "pool_scale", "attn_scale", "norm_mix_post",
          "norm_ffn_pre", "conv_b", "norm_ffn_post")
_SMALL_SIZE = {"norm_mix_pre": 1024, "w_pool": 65536, "pool_scale": 512, "attn_scale": 512,
               "norm_mix_post": 1024, "norm_ffn_pre": 1024, "conv_b": 5632, "norm_ffn_post": 1024}
_SMALL_ROWS = 600
_CONVW_ROWS = 132
_PACK_ROWS = _SMALL_ROWS + _CONVW_ROWS + 4


def _pack_small(parts):
    flat = jnp.concatenate([parts[n].reshape(-1) for n in _SMALL])
    flat = jnp.pad(flat, (0, _SMALL_ROWS * 128 - flat.shape[0]))
    return flat.reshape(_SMALL_ROWS, 128)


def _unpack_small(packed, like):
    flat = packed.reshape(-1)
    out, off = {}, 0
    for n in _SMALL:
        out[n] = flat[off:off + _SMALL_SIZE[n]].reshape(like[n].shape)
        off += _SMALL_SIZE[n]
    return out


def kernel(x, norm_mix_pre, w_in, w_pool, pool_scale, attn_scale, w_out, norm_mix_post, norm_ffn_pre, w_up, conv_w, conv_b, w_down, norm_ffn_post, loss_target, m_norm_mix_pre, m_w_in, m_w_pool, m_pool_scale, m_attn_scale, m_w_out, m_norm_mix_post, m_norm_ffn_pre, m_w_up, m_conv_w, m_conv_b, m_w_down, m_norm_ffn_post, v_norm_mix_pre, v_w_in, v_w_pool, v_pool_scale, v_attn_scale, v_w_out, v_norm_mix_post, v_norm_ffn_pre, v_w_up, v_conv_w, v_conv_b, v_w_down, v_norm_ffn_post):
    weights = dict(norm_mix_pre=norm_mix_pre, w_in=w_in, w_pool=w_pool, pool_scale=pool_scale,
                   attn_scale=attn_scale, w_out=w_out, norm_mix_post=norm_mix_post,
                   norm_ffn_pre=norm_ffn_pre, w_up=w_up, conv_w=conv_w, conv_b=conv_b,
                   w_down=w_down, norm_ffn_post=norm_ffn_post)
    mom1 = dict(norm_mix_pre=m_norm_mix_pre, w_in=m_w_in, w_pool=m_w_pool, pool_scale=m_pool_scale,
                attn_scale=m_attn_scale, w_out=m_w_out, norm_mix_post=m_norm_mix_post,
                norm_ffn_pre=m_norm_ffn_pre, w_up=m_w_up, conv_w=m_conv_w, conv_b=m_conv_b,
                w_down=m_w_down, norm_ffn_post=m_norm_ffn_post)
    mom2 = dict(norm_mix_pre=v_norm_mix_pre, w_in=v_w_in, w_pool=v_w_pool, pool_scale=v_pool_scale,
                attn_scale=v_attn_scale, w_out=v_w_out, norm_mix_post=v_norm_mix_post,
                norm_ffn_pre=v_norm_ffn_pre, w_up=v_w_up, conv_w=v_conv_w, conv_b=v_conv_b,
                w_down=v_w_down, norm_ffn_post=v_norm_ffn_post)
    order = list(weights)

    xs = x[0]
    target = loss_target[0]
    wp = w_pool[0]
    shard = lax.axis_index("x") * 2 + lax.axis_index("y")

    shards = [_cast_bf16(w_in[0], "cast_w_in"), _cast_bf16(w_out[0], "cast_w_out"),
              _cast_bf16(w_up[0], "cast_w_up"), _cast_bf16(w_down[0], "cast_w_down"), conv_w[0]]
    win_g, = _gather_shards(shards[:1])
    g_send, g_recv, g_srcs, g_lands, g_token = _exchange_start(
        shards[1:], _place_own(shards[1:], False, None, "place_own_shards"), False, "gather_start")
    convb_g = conv_b[0].reshape(N_SHARD, 1, FF_TILE)

    u, q, k, v, vt, h1 = _in_proj(xs, norm_mix_pre + g_token[0:1, 0:1], win_g)
    mpool = _pool_fwd(u, wp, pool_scale)
    attn, totals = _attn_fwd(q, k, vt)
    wout_g, wup_g, wdown_g, convw_g = _exchange_wait(g_send, g_recv, g_srcs, g_lands, attn, False, "gather_wait")
    wout_f = wout_g.reshape(D_MODEL, D_MODEL)
    wdown_f = wdown_g.reshape(D_FF, D_MODEL)
    mattn, mix, x1, h2 = _mix_out(attn, mpool, xs, attn_scale, wout_f, norm_mix_post, norm_ffn_pre)
    upre_g, upre_v, f_in = _ffn_up(h2, wup_g, convw_g, convb_g)
    df, dy, loss_tile, d_post = _ffn_down(f_in, wdown_f, x1, target, norm_ffn_post)

    d_wdown = _tn_matmul(f_in, df[None], "dw_down")
    dug, duv, dcw_g, dcw_v, dcb_g, dcb_v = _ffn_bwd_act(df, wdown_f, upre_g, upre_v, convw_g, convb_g)
    d_wup = jnp.concatenate([_tn_matmul(h2[None], dug, "dw_up_gate")[0],
                             _tn_matmul(h2[None], duv, "dw_up_value")[0]], axis=0)
    early = [d_wup, d_wdown.reshape(N_SHARD, D_FF // N_SHARD, D_MODEL)]
    s_send, s_recv, s_srcs, s_lands, s_token = _exchange_start(
        early, _place_own(early, True, 3, "place_own_slices"), True, "scatter_start")
    dx1, dmix, d_ffn_pre, d_mix_post = _ffn_bwd_in(
        dug, duv, wup_g, x1, dy, mix, norm_ffn_pre + s_token[0:1, 0:1], norm_mix_post)
    d_wout = jnp.concatenate([_tn_matmul(mpool[None], dmix[None], "dw_out_pool")[0, 0],
                              _tn_matmul(mattn[None], dmix[None], "dw_out_attn")[0, 0]], axis=0)
    dmp, do, d_attn_scale = _mix_bwd(dmix, wout_f, attn, attn_scale)
    dq, dk, dv = _attn_bwd(q, k, v, do, totals)
    du, d_wpool, d_pool_scale = _pool_bwd(u, dmp, wp, pool_scale)
    d_win = jnp.stack([_tn_matmul(h1[None], t[None], "dw_in_%d" % n)[0, 0]
                       for n, t in enumerate((du, dq, dk, dv))])
    grad_x, d_mix_pre = _in_proj_bwd(du, dq, dk, dv, win_g, xs, dx1, norm_mix_pre)

    d_convw = jnp.concatenate([dcw_g, dcw_v], axis=0)
    d_convb = jnp.concatenate([dcb_g, dcb_v], axis=0).reshape(1, 2 * D_FF)
    small_parts = dict(norm_mix_pre=d_mix_pre, w_pool=d_wpool, pool_scale=d_pool_scale,
                       attn_scale=d_attn_scale, norm_mix_post=d_mix_post, norm_ffn_pre=d_ffn_pre,
                       conv_b=d_convb, norm_ffn_post=d_post)
    packed = jnp.concatenate([_pack_small(small_parts), d_convw.reshape(_CONVW_ROWS, 128),
                              loss_tile[0:4]], axis=0)
    late, gathered = _scatter_grads([d_win, d_wout.reshape(N_SHARD, D_MODEL // N_SHARD, D_MODEL)], packed)
    recv = list(late) + list(_exchange_wait(s_send, s_recv, s_srcs, s_lands, grad_x, True, "scatter_wait"))
    quarter = [_sum_slots(r, (3, 0, 1, 2), "sum_chips_%d" % n) for n, r in enumerate(recv)]
    sibling = _swap_with_sibling(quarter)
    small_sum = _sum_slots(gathered, tuple(range(8)), "sum_small")

    results = {}
    for n, name in enumerate(("w_in", "w_out", "w_up", "w_down")):
        res = _adamw([quarter[n], sibling[n]], weights[name][0], mom1[name][0], mom2[name][0],
                     "adamw_" + name)
        results[name] = [t[None] for t in res]
    g_convw = lax.dynamic_slice_in_dim(
        small_sum[_SMALL_ROWS:_SMALL_ROWS + _CONVW_ROWS].reshape(N_SHARD, 3, FF_TILE), shard, 1, axis=0)[0]
    convw_pad = lambda t: jnp.pad(t, ((0, 5), (0, 0)))
    res = _adamw([convw_pad(g_convw)], convw_pad(conv_w[0]), convw_pad(m_conv_w[0]),
                 convw_pad(v_conv_w[0]), "adamw_conv_w")
    results["conv_w"] = [t[:3][None] for t in res]
    pack_w = _pack_small(weights)
    pack_m = _pack_small(mom1)
    pack_v = _pack_small(mom2)
    res = _adamw([small_sum[:_SMALL_ROWS]], pack_w, pack_m, pack_v, "adamw_small")
    unpacked = [_unpack_small(t, weights) for t in res]
    for name in _SMALL:
        results[name] = [t[name] for t in unpacked]

    loss = small_sum[_SMALL_ROWS + _CONVW_ROWS, 0]
    outs = [loss, grad_x[None]]
    for slot in range(4):
        outs.extend(results[name][slot] for name in order)
    return tuple(outs)
```

```python
import functools

import jax
import jax.numpy as jnp
from jax import lax
from jax.experimental import pallas as pl
from jax.experimental.pallas import tpu as pltpu

F32 = jnp.float32
BF16 = jnp.bfloat16

D_MODEL = 1024
D_POOL = 512
D_ATTN = 512
POOL_WINDOWS = (2, 4, 8, 16)
POOL_GROUP = 128
POOL_HALO = 16
CONV_HALO = 8
D_FF = 2816
FF_TILE = 1408
N_SHARD = 4
EPS = 1e-6
Q_SCALE = 0.125
ATT_BLOCK = 256
HEAD_PAIR = 128
MIB = 1 << 20
NEG_BIG = -1e30

ADAM_LR = 0.001
ADAM_B1 = 0.9
ADAM_B2 = 0.999
ADAM_EPS = 1e-08
ADAM_WD = 0.01
ADAM_STEP = 10

NT_DIMS = (((1,), (1,)), ((), ()))
TN_DIMS = (((0,), (0,)), ((), ()))
MESH = pl.DeviceIdType.MESH
ANY = pl.BlockSpec(memory_space=pl.ANY)
HBM_SPEC = pl.BlockSpec(memory_space=pltpu.HBM)
SEM_SPEC = pl.BlockSpec(memory_space=pltpu.SEMAPHORE)
DATAFLOW = pltpu.SideEffectType.DATAFLOW_SIDE_EFFECTING


def _call(body, **kw):
    return pl.pallas_call(body, **kw)


def _params(sem=None, vmem_mb=48):
    return pltpu.CompilerParams(dimension_semantics=sem, vmem_limit_bytes=vmem_mb * MIB)


def _rstd(v):
    return lax.rsqrt(jnp.mean(v * v, axis=-1, keepdims=True) + EPS)


def _dot(a, b):
    return jnp.dot(a, b, preferred_element_type=F32)


def _dot_nt(a, b):
    return lax.dot_general(a, b, NT_DIMS, preferred_element_type=F32)


def _dot_tn(a, b):
    return lax.dot_general(a, b, TN_DIMS, preferred_element_type=F32)


def _row_tile(rows, cap):
    t = min(rows, cap)
    t -= t % 8
    while rows % t:
        t -= 8
    return t


def _full(shape):
    nd = len(shape)
    return pl.BlockSpec(shape, lambda *_: (0,) * nd)


def _chip_peers():
    x, y, c = lax.axis_index("x"), lax.axis_index("y"), lax.axis_index("c")
    return x, y, c, [(1 - x, y), (x, 1 - y), (1 - x, 1 - y)]


def _cast_bf16(a, name):
    def body(a_ref, o_ref):
        o_ref[...] = a_ref[...].astype(BF16)

    return _call(body, name=name, out_shape=jax.ShapeDtypeStruct(a.shape, BF16),
                 grid=(1,), in_specs=[_full(a.shape)], out_specs=_full(a.shape),
                 compiler_params=_params(("arbitrary",)))(a)


def _into_slot(a, slot, dtype, name):
    nd = a.ndim

    def body(slot_ref, a_ref, o_ref):
        o_ref[...] = a_ref[...].astype(dtype)

    return _call(
        body, name=name, out_shape=jax.ShapeDtypeStruct((N_SHARD,) + a.shape, dtype),
        grid_spec=pltpu.PrefetchScalarGridSpec(
            num_scalar_prefetch=1, grid=(1,),
            in_specs=[pl.BlockSpec(a.shape, lambda i, slot_ref: (0,) * nd)],
            out_specs=pl.BlockSpec((None,) + a.shape, lambda i, slot_ref: (slot_ref[0],) + (0,) * nd)),
        compiler_params=_params(("arbitrary",)))(slot, a)


def _exchange_copies(srcs, lands, send, recv):
    x, y, c, chips = _chip_peers()
    copies = []
    for t in range(len(lands)):
        for k, (px, py) in enumerate(chips):
            copies.append(pltpu.make_async_remote_copy(
                src_ref=lands[t].at[2 * x + y] if srcs is None else srcs[t].at[2 * px + py],
                dst_ref=lands[t].at[2 * x + y] if srcs is None else lands[t].at[k],
                send_sem=send.at[3 * t + k], recv_sem=recv.at[3 * t + k],
                device_id=(px, py, c), device_id_type=MESH))
    return copies


def _exchange_start(srcs, lands, after, name):
    n = len(lands)
    operands = list(lands) if srcs is None else list(srcs) + list(lands)
    m = len(operands)

    def body(*refs):
        for cp in _exchange_copies(None if srcs is None else refs[:n], refs[m - n:m], refs[m + 1], refs[m + 2]):
            cp.start()
        refs[-1][...] = jnp.zeros_like(refs[-1])

    res = _call(
        body, name=name,
        out_shape=[pltpu.SemaphoreType.DMA((3 * n,)), pltpu.SemaphoreType.DMA((3 * n,))]
        + [pltpu.HBM(a.shape, a.dtype) for a in operands] + [jax.ShapeDtypeStruct((8, 128), F32)],
        in_specs=[HBM_SPEC] * m + [ANY],
        out_specs=[SEM_SPEC, SEM_SPEC] + [HBM_SPEC] * m + [pl.BlockSpec(memory_space=pltpu.VMEM)],
        input_output_aliases={j: j + 2 for j in range(m)},
        compiler_params=pltpu.CompilerParams(has_side_effects=DATAFLOW),
    )(*[pltpu.with_memory_space_constraint(a, pltpu.HBM) for a in operands], after)
    return res[0], res[1], res[2:2 + m], res[-1]


def _exchange_wait(send, recv, operands, scatter, after, name):
    m = len(operands)
    n = m // 2 if scatter else m

    def body(*refs):
        for cp in _exchange_copies(refs[:n] if scatter else None, refs[m - n:m], refs[m], refs[m + 1]):
            cp.wait_send()
            cp.wait_recv()

    res = _call(
        body, name=name, out_shape=[pltpu.HBM(a.shape, a.dtype) for a in operands],
        in_specs=[HBM_SPEC] * m + [SEM_SPEC, SEM_SPEC, ANY], out_specs=[HBM_SPEC] * m,
        input_output_aliases={j: j for j in range(m)},
        compiler_params=pltpu.CompilerParams(has_side_effects=DATAFLOW),
    )(*operands, send, recv, after)
    return res[:m - n], res[m - n:]


def _gather_shards(shards):
    n = len(shards)

    def body(*refs):
        ins, outs = refs[:n], refs[n:2 * n]
        send, recv, loc = refs[2 * n:]
        x, y, c, chips = _chip_peers()
        b = 2 * x + y
        local = [pltpu.make_async_copy(ins[t], outs[t].at[b], loc.at[t]) for t in range(n)]
        for cp in local:
            cp.start()
        remote = []
        for t in range(n):
            for k, (px, py) in enumerate(chips):
                remote.append(pltpu.make_async_remote_copy(
                    src_ref=ins[t], dst_ref=outs[t].at[b],
                    send_sem=send.at[3 * t + k], recv_sem=recv.at[3 * t + k],
                    device_id=(px, py, c), device_id_type=MESH))
        for cp in remote:
            cp.start()
        for cp in remote:
            cp.wait()
        for cp in local:
            cp.wait()

    return _call(
        body, name="gather_w_in",
        out_shape=[jax.ShapeDtypeStruct((N_SHARD,) + s.shape, s.dtype) for s in shards],
        in_specs=[ANY] * n, out_specs=[ANY] * n,
        scratch_shapes=[pltpu.SemaphoreType.DMA((3 * n,)), pltpu.SemaphoreType.DMA((3 * n,)),
                        pltpu.SemaphoreType.DMA((n,))],
    )(*shards)


def _scatter_grads(grads, small):
    n = len(grads)

    def body(*refs):
        ins, small_in = refs[:n], refs[n]
        outs, small_out = refs[n + 1:2 * n + 1], refs[2 * n + 1]
        send, recv, loc, ssend, srecv = refs[2 * n + 2:]
        x, y, c, chips = _chip_peers()
        b = 2 * x + y
        me = 4 * x + 2 * y + c
        local = [pltpu.make_async_copy(ins[t].at[b], outs[t].at[3], loc.at[t]) for t in range(n)]
        local.append(pltpu.make_async_copy(small_in, small_out.at[me], loc.at[n]))
        for cp in local:
            cp.start()
        remote = []
        for t in range(n):
            for k, (px, py) in enumerate(chips):
                remote.append(pltpu.make_async_remote_copy(
                    src_ref=ins[t].at[2 * px + py], dst_ref=outs[t].at[k],
                    send_sem=send.at[3 * t + k], recv_sem=recv.at[3 * t + k],
                    device_id=(px, py, c), device_id_type=MESH))
        for r in range(1, 8):
            px = 1 - x if r & 4 else x
            py = 1 - y if r & 2 else y
            pc = 1 - c if r & 1 else c
            remote.append(pltpu.make_async_remote_copy(
                src_ref=small_in, dst_ref=small_out.at[me],
                send_sem=ssend.at[r - 1], recv_sem=srecv.at[r - 1],
                device_id=(px, py, pc), device_id_type=MESH))
        for cp in remote:
            cp.start()
        for cp in remote:
            cp.wait()
        for cp in local:
            cp.wait()

    out_shape = [jax.ShapeDtypeStruct(g.shape, g.dtype) for g in grads]
    out_shape.append(jax.ShapeDtypeStruct((8,) + small.shape, small.dtype))
    res = _call(
        body, name="scatter_grads", out_shape=out_shape,
        in_specs=[ANY] * (n + 1), out_specs=[ANY] * (n + 1),
        scratch_shapes=[pltpu.SemaphoreType.DMA((3 * n,)), pltpu.SemaphoreType.DMA((3 * n,)),
                        pltpu.SemaphoreType.DMA((n + 1,)),
                        pltpu.SemaphoreType.DMA((7,)), pltpu.SemaphoreType.DMA((7,))],
    )(*grads, small)
    return res[:n], res[n]


def _swap_with_sibling(parts):
    n = len(parts)

    def body(*refs):
        ins, outs = refs[:n], refs[n:2 * n]
        send, recv = refs[2 * n:]
        x, y, c = lax.axis_index("x"), lax.axis_index("y"), lax.axis_index("c")
        copies = [pltpu.make_async_remote_copy(
            src_ref=ins[t], dst_ref=outs[t], send_sem=send.at[t], recv_sem=recv.at[t],
            device_id=(x, y, 1 - c), device_id_type=MESH) for t in range(n)]
        for cp in copies:
            cp.start()
        for cp in copies:
            cp.wait()

    return _call(
        body, name="swap_sibling",
        out_shape=[jax.ShapeDtypeStruct(p.shape, p.dtype) for p in parts],
        in_specs=[ANY] * n, out_specs=[ANY] * n,
        scratch_shapes=[pltpu.SemaphoreType.DMA((n,)), pltpu.SemaphoreType.DMA((n,))],
    )(*parts)


def _sum_slots(buf, order, name):
    k, rows, cols = buf.shape
    tr = _row_tile(rows, 256)

    def body(b_ref, o_ref):
        acc = b_ref[order[0]].astype(F32)
        for s in order[1:]:
            acc = acc + b_ref[s].astype(F32)
        o_ref[...] = acc

    return _call(body, name=name, out_shape=jax.ShapeDtypeStruct((rows, cols), F32),
                 grid=(rows // tr,),
                 in_specs=[pl.BlockSpec((k, tr, cols), lambda i: (0, i, 0))],
                 out_specs=pl.BlockSpec((tr, cols), lambda i: (i, 0)),
                 compiler_params=_params(("parallel",)))(buf)


def _sum_own_and_received(src, slot, land, name):
    _, rows, cols = src.shape
    tr = _row_tile(rows, 256)

    def body(slot_ref, s_ref, l_ref, o_ref):
        acc = s_ref[...].astype(F32)
        for k in range(3):
            acc = acc + l_ref[k].astype(F32)
        o_ref[...] = acc

    return _call(
        body, name=name, out_shape=jax.ShapeDtypeStruct((rows, cols), F32),
        grid_spec=pltpu.PrefetchScalarGridSpec(
            num_scalar_prefetch=1, grid=(rows // tr,),
            in_specs=[pl.BlockSpec((None, tr, cols), lambda i, slot_ref: (slot_ref[0], i, 0)),
                      pl.BlockSpec((3, tr, cols), lambda i, slot_ref: (0, i, 0))],
            out_specs=pl.BlockSpec((tr, cols), lambda i, slot_ref: (i, 0))),
        compiler_params=_params(("parallel",)))(slot, src, land)


def _adamw(grad_parts, w, m, v, name):
    rows, cols = w.shape
    tr = _row_tile(rows, 256)
    npart = len(grad_parts)

    def body(*refs):
        gp = refs[:npart]
        w_ref, m_ref, v_ref, g_out, d_out, m_out, v_out = refs[npart:]
        g = gp[0][...]
        for p in gp[1:]:
            g = g + p[...]
        mm = ADAM_B1 * m_ref[...] + (1.0 - ADAM_B1) * g
        vv = ADAM_B2 * v_ref[...] + (1.0 - ADAM_B2) * jnp.square(g)
        m_hat = mm / (1.0 - ADAM_B1 ** ADAM_STEP)
        v_hat = vv / (1.0 - ADAM_B2 ** ADAM_STEP)
        g_out[...] = g
        d_out[...] = -ADAM_LR * (m_hat / (jnp.sqrt(v_hat) + ADAM_EPS) + ADAM_WD * w_ref[...])
        m_out[...] = mm
        v_out[...] = vv

    spec = pl.BlockSpec((tr, cols), lambda i: (i, 0))
    shp = jax.ShapeDtypeStruct((rows, cols), F32)
    return _call(body, name=name, out_shape=[shp] * 4, grid=(rows // tr,),
                 in_specs=[spec] * (npart + 3), out_specs=[spec] * 4,
                 compiler_params=_params(("parallel",)))(*grad_parts, w, m, v)


def _in_proj(x, g1, w_in):
    s = x.shape[0]
    ts = 512

    def body(x_ref, g_ref, w_ref, u_ref, q_ref, k_ref, v_ref, vt_ref, h_ref):
        xv = x_ref[...]
        h = (xv * _rstd(xv) * g_ref[...]).astype(BF16)
        h_ref[...] = h
        u_ref[...] = _dot(h, w_ref[0])
        q_ref[...] = (_dot(h, w_ref[1]) * Q_SCALE).astype(BF16)
        k_ref[...] = _dot(h, w_ref[2]).astype(BF16)
        v = _dot(h, w_ref[3])
        v_ref[...] = v.astype(BF16)
        vt = v.T.astype(BF16)
        for n in range(ts // ATT_BLOCK):
            vt_ref[n] = vt[:, n * ATT_BLOCK:(n + 1) * ATT_BLOCK]

    row = lambda w: pl.BlockSpec((ts, w), lambda i: (i, 0))
    half = jax.ShapeDtypeStruct((s, D_POOL), BF16)
    return _call(
        body, name="in_proj",
        out_shape=[jax.ShapeDtypeStruct((s, D_POOL), F32), half, half, half,
                   jax.ShapeDtypeStruct((s // ATT_BLOCK, D_ATTN, ATT_BLOCK), BF16),
                   jax.ShapeDtypeStruct((s, D_MODEL), BF16)],
        grid=(s // ts,),
        in_specs=[row(D_MODEL), _full((1, D_MODEL)), _full(w_in.shape)],
        out_specs=[row(D_POOL)] * 4
        + [pl.BlockSpec((ts // ATT_BLOCK, D_ATTN, ATT_BLOCK), lambda i: (i, 0, 0)), row(D_MODEL)],
        compiler_params=_params(("parallel",)))(x, g1, w_in)


def _pool_means(ext_ref, g, window, ts, row0):
    cols = slice(g * POOL_GROUP, (g + 1) * POOL_GROUP)
    cur = ext_ref[POOL_HALO:POOL_HALO + ts, cols]
    acc = cur
    for d in range(1, window):
        acc = acc + ext_ref[POOL_HALO - d:POOL_HALO - d + ts, cols]
    t1 = row0 + 1 + lax.broadcasted_iota(jnp.int32, (ts, 1), 0)
    cnt = jnp.minimum(t1, window).astype(F32)
    return acc / cnt - cur, cnt


def _pool_fwd(u, w_pool, pool_scale):
    s = u.shape[0]
    ts = 512
    per = ts // POOL_HALO

    def body(u_ref, halo_ref, wp_ref, ps_ref, o_ref, ext_ref, y_ref):
        i = pl.program_id(0)
        ext_ref[0:POOL_HALO, :] = jnp.where(i > 0, halo_ref[...], 0.0)
        ext_ref[POOL_HALO:, :] = u_ref[...]
        for g, window in enumerate(POOL_WINDOWS):
            p, _ = _pool_means(ext_ref, g, window, ts, i * ts)
            y_ref[:, g * POOL_GROUP:(g + 1) * POOL_GROUP] = _dot(
                p.astype(BF16), wp_ref[g].astype(BF16))
        y = y_ref[...]
        o_ref[...] = (y * _rstd(y) * ps_ref[...]).astype(BF16)

    return _call(
        body, name="pool_fwd", out_shape=jax.ShapeDtypeStruct((s, D_POOL), BF16),
        grid=(s // ts,),
        in_specs=[pl.BlockSpec((ts, D_POOL), lambda i: (i, 0)),
                  pl.BlockSpec((POOL_HALO, D_POOL), lambda i: (jnp.maximum(i * per - 1, 0), 0)),
                  _full(w_pool.shape), _full((1, D_POOL))],
        out_specs=pl.BlockSpec((ts, D_POOL), lambda i: (i, 0)),
        scratch_shapes=[pltpu.VMEM((ts + POOL_HALO, D_POOL), F32), pltpu.VMEM((ts, D_POOL), F32)],
        compiler_params=_params(("parallel",)))(u, u, w_pool, pool_scale)


def _tri(kind):
    r = lax.broadcasted_iota(jnp.int32, (ATT_BLOCK, ATT_BLOCK), 0)
    c = lax.broadcasted_iota(jnp.int32, (ATT_BLOCK, ATT_BLOCK), 1)
    return jnp.where(r >= c if kind == "suffix" else r <= c, 1.0, 0.0).astype(BF16)


def _causal_mask():
    r = lax.broadcasted_iota(jnp.int32, (ATT_BLOCK, ATT_BLOCK), 0)
    c = lax.broadcasted_iota(jnp.int32, (ATT_BLOCK, ATT_BLOCK), 1)
    return c < r


def _softplus(z, with_sigmoid=False):
    ope = 1.0 + jnp.exp(jnp.minimum(z, 80.0))
    sp = jnp.maximum(z, jnp.log(ope))
    if with_sigmoid:
        return sp, 1.0 - 1.0 / ope
    return sp


def _attn_fwd(q, k, vt):
    s = q.shape[0]
    tb = ATT_BLOCK
    nq = s // tb

    def body(q_ref, k_ref, vt_ref, o_ref, t_ref, *bufs):
        i = pl.program_id(1)
        suffix = _tri("prefix")
        r_idx = lax.broadcasted_iota(jnp.int32, (tb, tb), 0)
        c_idx = lax.broadcasted_iota(jnp.int32, (tb, tb), 1)
        causal = r_idx < c_idx
        lane = lax.broadcasted_iota(jnp.int32, (1, HEAD_PAIR), 1)
        first = lane < 64
        top = lax.broadcasted_iota(jnp.int32, (HEAD_PAIR, 1), 0) < 64
        q2 = q_ref[...]
        zero = jnp.zeros_like(q2)
        qs = (jnp.where(first, q2, zero), jnp.where(first, zero, q2))

        def values_t(j):
            vt = vt_ref[j]
            none = jnp.zeros_like(vt)
            return jnp.concatenate([jnp.where(top, vt, none), jnp.where(top, none, vt)], axis=1)

        def scores(j, cs, masked):
            kj = k_ref[pl.ds(pl.multiple_of(j * tb, tb), tb), :]
            new_cs, args = [], []
            for e in range(2):
                z = _dot_nt(kj, qs[e])
                sp = _softplus(z)
                if masked:
                    sp = jnp.where(causal, sp, 0.0)
                incl = _dot(suffix, sp.astype(BF16))
                arg = z - incl - cs[e]
                if masked:
                    arg = jnp.where(causal, arg, NEG_BIG)
                args.append(arg)
                new_cs.append(cs[e] + incl[0:1, :])
            return new_cs, args

        def weigh(j, args, o):
            probs = [jnp.exp(arg).astype(BF16) for arg in args]
            return o + _dot(values_t(j), jnp.concatenate(probs, axis=0))

        z_buf, zc_buf, in_buf = [[[bufs[4 * kind + 2 * slot + e] for e in range(2)]
                                  for slot in range(2)] for kind in range(3)]
        pr_buf = [bufs[12], bufs[13]]

        @pl.when((pl.program_id(0) == 0) & (i == 0))
        def _():
            for b in bufs:
                b[...] = jnp.zeros_like(b)

        def block_of(p):
            return jnp.clip(i - 1 - p, 0, nq - 1)

        def trip(t, w, carry):
            r = 1 - w
            cs, o = list(carry[0:2]), carry[2]
            live3 = (t - 3 >= 0) & (t - 3 < i)
            o = o + jnp.where(live3, _dot(values_t(block_of(t - 3)), pr_buf[r][...]), 0.0)
            kj = k_ref[pl.ds(pl.multiple_of(block_of(t) * tb, tb), tb), :]
            for e in range(2):
                z_buf[w][e][...] = _dot_nt(kj, qs[e])
            for e in range(2):
                z = z_buf[r][e][...]
                zc_buf[w][e][...] = z
                in_buf[w][e][...] = _dot(suffix, _softplus(z).astype(BF16))
            live2 = (t - 2 >= 0) & (t - 2 < i)
            for e in range(2):
                incl = in_buf[r][e][...]
                arg = zc_buf[r][e][...] - incl - jnp.where(live2, cs[e], -NEG_BIG)
                pr_buf[w][e * tb:(e + 1) * tb, :] = jnp.exp(arg).astype(BF16)
                cs[e] = jnp.where(live2, cs[e] + incl[0:1, :], cs[e])
            return cs[0], cs[1], o

        def four_trips(n, cr):
            for u in range(4):
                cr = trip(4 * n + u, u % 2, cr)
            return cr

        row = jnp.zeros((1, tb), F32)
        cs, args = scores(i, (row, row), True)
        carry = (cs[0], cs[1], weigh(i, args, jnp.zeros((HEAD_PAIR, tb), F32)))
        carry = lax.fori_loop(0, jnp.where(i > 0, (i + 6) // 4, 0), four_trips, carry)
        o_ref[...] = carry[2].T
        totals = jnp.where(r_idx == 0, carry[0], jnp.where(r_idx == 1, carry[1], 0.0))
        t_ref[...] = totals.T[:, 0:2]

    score_buf = pltpu.VMEM((tb, tb), F32)
    return _call(
        body, name="attn_fwd",
        out_shape=[jax.ShapeDtypeStruct((s, D_ATTN), F32),
                   jax.ShapeDtypeStruct((4, s, 2), F32)],
        grid=(4, nq),
        in_specs=[pl.BlockSpec((tb, HEAD_PAIR), lambda h, i: (i, h)),
                  pl.BlockSpec((s, HEAD_PAIR), lambda h, i: (0, h)),
                  pl.BlockSpec((nq, HEAD_PAIR, tb), lambda h, i: (0, h, 0))],
        out_specs=[pl.BlockSpec((tb, HEAD_PAIR), lambda h, i: (i, h)),
                   pl.BlockSpec((None, tb, 2), lambda h, i: (h, i, 0))],
        scratch_shapes=[score_buf] * 12 + [pltpu.VMEM((2 * tb, tb), BF16)] * 2,
        compiler_params=_params(("arbitrary", "arbitrary")))(q, k, vt)


def _mix_out(attn, mpool, x, attn_scale, w_out, g2, g3):
    s = x.shape[0]
    ts = 512

    def body(a_ref, p_ref, x_ref, as_ref, w_ref, g2_ref, g3_ref, ma_ref, mix_ref, x1_ref, h2_ref):
        ao = a_ref[...]
        ma = (ao * _rstd(ao) * as_ref[...]).astype(BF16)
        ma_ref[...] = ma
        mix = _dot(p_ref[...], w_ref[0:D_POOL, :]) + _dot(ma, w_ref[D_POOL:, :])
        mix_ref[...] = mix
        x1 = x_ref[...] + mix * _rstd(mix) * g2_ref[...]
        x1_ref[...] = x1
        h2_ref[...] = (x1 * _rstd(x1) * g3_ref[...]).astype(BF16)

    row = lambda w: pl.BlockSpec((ts, w), lambda i: (i, 0))
    return _call(
        body, name="mix_out",
        out_shape=[jax.ShapeDtypeStruct((s, D_ATTN), BF16), jax.ShapeDtypeStruct((s, D_MODEL), F32),
                   jax.ShapeDtypeStruct((s, D_MODEL), F32), jax.ShapeDtypeStruct((s, D_MODEL), BF16)],
        grid=(s // ts,),
        in_specs=[row(D_ATTN), row(D_POOL), row(D_MODEL), _full((1, D_ATTN)),
                  _full((D_MODEL, D_MODEL)), _full((1, D_MODEL)), _full((1, D_MODEL))],
        out_specs=[row(D_ATTN), row(D_MODEL), row(D_MODEL), row(D_MODEL)],
        compiler_params=_params(("parallel",)))(attn, mpool, x, attn_scale, w_out, g2, g3)


def _conv_rows(ext_ref, cw, cb, ts):
    y = cb + cw[0:1, :] * ext_ref[CONV_HALO - 2:CONV_HALO - 2 + ts, :]
    y = y + cw[1:2, :] * ext_ref[CONV_HALO - 1:CONV_HALO - 1 + ts, :]
    return y + cw[2:3, :] * ext_ref[CONV_HALO:CONV_HALO + ts, :]


def _sigmoid(v):
    return 1.0 / (1.0 + jnp.exp(-v))


def _ffn_up(h2, w_up, conv_w, conv_b):
    s = h2.shape[0]
    ts = 256
    tn = FF_TILE

    def body(h_ref, wg_ref, wv_ref, cwg_ref, cwv_ref, cbg_ref, cbv_ref,
             ug_ref, uv_ref, f_ref, extg, extv):
        i = pl.program_id(1)

        @pl.when(i == 0)
        def _():
            extg[0:CONV_HALO, :] = jnp.zeros((CONV_HALO, tn), F32)
            extv[0:CONV_HALO, :] = jnp.zeros((CONV_HALO, tn), F32)

        h = h_ref[...]
        ug = _dot(h, wg_ref[...])
        uv = _dot(h, wv_ref[...])
        ug_ref[...] = ug
        uv_ref[...] = uv
        extg[CONV_HALO:, :] = ug
        extv[CONV_HALO:, :] = uv
        gate = _conv_rows(extg, cwg_ref[...], cbg_ref[...], ts)
        val = _conv_rows(extv, cwv_ref[...], cbv_ref[...], ts)
        f_ref[...] = (gate * _sigmoid(gate) * val).astype(BF16)
        extg[0:CONV_HALO, :] = extg[ts:ts + CONV_HALO, :]
        extv[0:CONV_HALO, :] = extv[ts:ts + CONV_HALO, :]

    out_blk = pl.BlockSpec((None, ts, tn), lambda n, i: (n, i, 0))
    act = jax.ShapeDtypeStruct((2, s, tn), F32)
    return _call(
        body, name="ffn_up",
        out_shape=[act, act, jax.ShapeDtypeStruct((2, s, tn), BF16)],
        grid=(2, s // ts),
        in_specs=[pl.BlockSpec((ts, D_MODEL), lambda n, i: (i, 0)),
                  pl.BlockSpec((None, D_MODEL, tn), lambda n, i: (n, 0, 0)),
                  pl.BlockSpec((None, D_MODEL, tn), lambda n, i: (n + 2, 0, 0)),
                  pl.BlockSpec((None, 3, tn), lambda n, i: (n, 0, 0)),
                  pl.BlockSpec((None, 3, tn), lambda n, i: (n + 2, 0, 0)),
                  pl.BlockSpec((None, 1, tn), lambda n, i: (n, 0, 0)),
                  pl.BlockSpec((None, 1, tn), lambda n, i: (n + 2, 0, 0))],
        out_specs=[out_blk, out_blk, out_blk],
        scratch_shapes=[pltpu.VMEM((ts + CONV_HALO, tn), F32), pltpu.VMEM((ts + CONV_HALO, tn), F32)],
        compiler_params=_params(("arbitrary", "arbitrary")))(
            h2, w_up, w_up, conv_w, conv_w, conv_b, conv_b)


def _ffn_down(f_in, w_down, x1, target, g4):
    s = x1.shape[0]
    ts = 512

    def body(f_ref, w_ref, x1_ref, t_ref, g_ref, df_ref, dy_ref, loss_ref, dg_ref):
        @pl.when(pl.program_id(0) == 0)
        def _():
            loss_ref[...] = jnp.zeros_like(loss_ref)
            dg_ref[...] = jnp.zeros_like(dg_ref)

        f = _dot(f_ref[0], w_ref[0:FF_TILE, :]) + _dot(f_ref[1], w_ref[FF_TILE:, :])
        rf = _rstd(f)
        fn = f * rf
        g = g_ref[...]
        err = (x1_ref[...] + fn * g) - t_ref[...]
        loss_ref[...] += 0.5 * jnp.sum(jnp.mean(err * err, axis=-1))
        dy = err * (1.0 / D_MODEL)
        dy_ref[...] = dy
        dg_ref[...] += jnp.sum(dy * fn, axis=0, keepdims=True)
        dfn = dy * g
        df_ref[...] = (rf * (dfn - fn * jnp.mean(dfn * fn, axis=-1, keepdims=True))).astype(BF16)

    row = pl.BlockSpec((ts, D_MODEL), lambda i: (i, 0))
    return _call(
        body, name="ffn_down",
        out_shape=[jax.ShapeDtypeStruct((s, D_MODEL), BF16), jax.ShapeDtypeStruct((s, D_MODEL), F32),
                   jax.ShapeDtypeStruct((8, 128), F32), jax.ShapeDtypeStruct((1, D_MODEL), F32)],
        grid=(s // ts,),
        in_specs=[pl.BlockSpec((2, ts, FF_TILE), lambda i: (0, i, 0)), _full((D_FF, D_MODEL)),
                  row, row, _full((1, D_MODEL))],
        out_specs=[row, row, _full((8, 128)), _full((1, D_MODEL))],
        compiler_params=_params(("arbitrary",)))(f_in, w_down, x1, target, g4)


def _tn_matmul(a, b, name, ts=512):
    na, s, ka = a.shape
    nb, _, nbc = b.shape
    steps = s // ts

    def body(a_ref, b_ref, o_ref, acc_ref):
        @pl.when(pl.program_id(2) == 0)
        def _():
            acc_ref[...] = jnp.zeros_like(acc_ref)

        acc_ref[...] += _dot_tn(a_ref[...].astype(BF16), b_ref[...].astype(BF16))

        @pl.when(pl.program_id(2) == steps - 1)
        def _():
            o_ref[...] = acc_ref[...].astype(BF16)

    return _call(
        body, name=name, out_shape=jax.ShapeDtypeStruct((na, nb, ka, nbc), BF16),
        grid=(na, nb, steps),
        in_specs=[pl.BlockSpec((None, ts, ka), lambda i, j, r: (i, r, 0)),
                  pl.BlockSpec((None, ts, nbc), lambda i, j, r: (j, r, 0))],
        out_specs=pl.BlockSpec((None, None, ka, nbc), lambda i, j, r: (i, j, 0, 0)),
        scratch_shapes=[pltpu.VMEM((ka, nbc), F32)],
        compiler_params=_params(("parallel", "parallel", "arbitrary")))(a, b)


def _ffn_bwd_act(df, w_down, upre_g, upre_v, conv_w, conv_b):
    s = df.shape[0]
    ts = 256
    tn = FF_TILE
    nr = s // ts
    per = ts // CONV_HALO

    def body(df_ref, wd_ref, ug_ref, uv_ref, hg_ref, hv_ref, cwg_ref, cwv_ref, cbg_ref, cbv_ref,
             dug_ref, duv_ref, dcwg_ref, dcwv_ref, dcbg_ref, dcbv_ref, extg, extv, dxg, dxv):
        i = pl.program_id(1)
        first_rows = i == nr - 1

        @pl.when(i == 0)
        def _():
            dxg[ts:, :] = jnp.zeros((CONV_HALO, tn), F32)
            dxv[ts:, :] = jnp.zeros((CONV_HALO, tn), F32)
            for r in (dcwg_ref, dcwv_ref, dcbg_ref, dcbv_ref):
                r[...] = jnp.zeros_like(r)

        extg[0:CONV_HALO, :] = jnp.where(first_rows, 0.0, hg_ref[...])
        extv[0:CONV_HALO, :] = jnp.where(first_rows, 0.0, hv_ref[...])
        extg[CONV_HALO:, :] = ug_ref[...]
        extv[CONV_HALO:, :] = uv_ref[...]
        cwg, cwv = cwg_ref[...], cwv_ref[...]
        gate = _conv_rows(extg, cwg, cbg_ref[...], ts)
        val = _conv_rows(extv, cwv, cbv_ref[...], ts)
        sg = _sigmoid(gate)
        dfin = _dot_nt(df_ref[...], wd_ref[...])
        dval = dfin * (gate * sg)
        dgate = dfin * val * (sg * (1.0 + gate * (1.0 - sg)))

        def conv_bwd(dact, ext, dx, cw, dcw_ref, dcb_ref, du_ref):
            dx[0:ts, :] = dact
            dcb_ref[...] += jnp.sum(dact, axis=0, keepdims=True)
            for kk in range(3):
                lo = CONV_HALO - 2 + kk
                dcw_ref[kk:kk + 1, :] += jnp.sum(dact * ext[lo:lo + ts, :], axis=0, keepdims=True)
            du = cw[2:3, :] * dact + cw[1:2, :] * dx[1:1 + ts, :] + cw[0:1, :] * dx[2:2 + ts, :]
            du_ref[...] = du.astype(BF16)
            dx[ts:, :] = dx[0:CONV_HALO, :]

        conv_bwd(dgate, extg, dxg, cwg, dcwg_ref, dcbg_ref, dug_ref)
        conv_bwd(dval, extv, dxv, cwv, dcwv_ref, dcbv_ref, duv_ref)

    rows = lambda n, i: (n, nr - 1 - i, 0)
    halo = lambda n, i: (n, jnp.maximum((nr - 1 - i) * per - 1, 0), 0)
    act_blk = pl.BlockSpec((None, ts, tn), rows)
    halo_blk = pl.BlockSpec((None, CONV_HALO, tn), halo)
    cw_blk = lambda off: pl.BlockSpec((None, 3, tn), lambda n, i: (n + off, 0, 0))
    cb_blk = lambda off: pl.BlockSpec((None, 1, tn), lambda n, i: (n + off, 0, 0))
    acc_w = pl.BlockSpec((None, 3, tn), lambda n, i: (n, 0, 0))
    acc_b = pl.BlockSpec((None, 1, tn), lambda n, i: (n, 0, 0))
    dact = jax.ShapeDtypeStruct((2, s, tn), BF16)
    return _call(
        body, name="ffn_bwd_act",
        out_shape=[dact, dact, jax.ShapeDtypeStruct((2, 3, tn), F32), jax.ShapeDtypeStruct((2, 3, tn), F32),
                   jax.ShapeDtypeStruct((2, 1, tn), F32), jax.ShapeDtypeStruct((2, 1, tn), F32)],
        grid=(2, nr),
        in_specs=[pl.BlockSpec((ts, D_MODEL), lambda n, i: (nr - 1 - i, 0)),
                  pl.BlockSpec((tn, D_MODEL), lambda n, i: (n, 0)),
                  act_blk, act_blk, halo_blk, halo_blk,
                  cw_blk(0), cw_blk(2), cb_blk(0), cb_blk(2)],
        out_specs=[act_blk, act_blk, acc_w, acc_w, acc_b, acc_b],
        scratch_shapes=[pltpu.VMEM((ts + CONV_HALO, tn), F32)] * 4,
        compiler_params=_params(("arbitrary", "arbitrary")))(
            df, w_down, upre_g, upre_v, upre_g, upre_v, conv_w, conv_w, conv_b, conv_b)


def _ffn_bwd_in(dug, duv, w_up, x1, dy, mix, g3, g2):
    s = x1.shape[0]
    ts = 256

    def body(dg_ref, dv_ref, w_ref, x1_ref, dy_ref, mix_ref, g3_ref, g2_ref,
             dx1_ref, dmix_ref, dg3_ref, dg2_ref):
        @pl.when(pl.program_id(0) == 0)
        def _():
            dg3_ref[...] = jnp.zeros_like(dg3_ref)
            dg2_ref[...] = jnp.zeros_like(dg2_ref)

        dh = _dot_nt(dg_ref[0], w_ref[0]) + _dot_nt(dg_ref[1], w_ref[1])
        dh = dh + _dot_nt(dv_ref[0], w_ref[2]) + _dot_nt(dv_ref[1], w_ref[3])
        x1 = x1_ref[...]
        r3 = _rstd(x1)
        xn = x1 * r3
        dg3_ref[...] += jnp.sum(dh * xn, axis=0, keepdims=True)
        dxn = dh * g3_ref[...]
        dx1 = dy_ref[...] + r3 * (dxn - xn * jnp.mean(dxn * xn, axis=-1, keepdims=True))
        dx1_ref[...] = dx1
        mix = mix_ref[...]
        rm = _rstd(mix)
        mn = mix * rm
        dg2_ref[...] += jnp.sum(dx1 * mn, axis=0, keepdims=True)
        dmn = dx1 * g2_ref[...]
        dmix_ref[...] = (rm * (dmn - mn * jnp.mean(dmn * mn, axis=-1, keepdims=True))).astype(BF16)

    row = pl.BlockSpec((ts, D_MODEL), lambda i: (i, 0))
    act = pl.BlockSpec((2, ts, FF_TILE), lambda i: (0, i, 0))
    vec = _full((1, D_MODEL))
    return _call(
        body, name="ffn_bwd_in",
        out_shape=[jax.ShapeDtypeStruct((s, D_MODEL), F32), jax.ShapeDtypeStruct((s, D_MODEL), BF16),
                   jax.ShapeDtypeStruct((1, D_MODEL), F32), jax.ShapeDtypeStruct((1, D_MODEL), F32)],
        grid=(s // ts,),
        in_specs=[act, act, _full(w_up.shape), row, row, row, vec, vec],
        out_specs=[row, row, vec, vec],
        compiler_params=_params(("arbitrary",), vmem_mb=56))(dug, duv, w_up, x1, dy, mix, g3, g2)


def _mix_bwd(dmix, w_out, attn, attn_scale):
    s = dmix.shape[0]
    ts = 512

    def body(dm_ref, w_ref, a_ref, as_ref, dp_ref, do_ref, das_ref):
        @pl.when(pl.program_id(0) == 0)
        def _():
            das_ref[...] = jnp.zeros_like(das_ref)

        dm = dm_ref[...]
        dp_ref[...] = _dot_nt(dm, w_ref[0:D_POOL, :])
        da = _dot_nt(dm, w_ref[D_POOL:, :])
        ao = a_ref[...]
        ra = _rstd(ao)
        an = ao * ra
        das_ref[...] += jnp.sum(da * an, axis=0, keepdims=True)
        dan = da * as_ref[...]
        do_ref[...] = (ra * (dan - an * jnp.mean(dan * an, axis=-1, keepdims=True))).astype(BF16)

    row = lambda w: pl.BlockSpec((ts, w), lambda i: (i, 0))
    return _call(
        body, name="mix_bwd",
        out_shape=[jax.ShapeDtypeStruct((s, D_POOL), F32), jax.ShapeDtypeStruct((s, D_ATTN), BF16),
                   jax.ShapeDtypeStruct((1, D_ATTN), F32)],
        grid=(s // ts,),
        in_specs=[row(D_MODEL), _full((D_MODEL, D_MODEL)), row(D_ATTN), _full((1, D_ATTN))],
        out_specs=[row(D_POOL), row(D_ATTN), _full((1, D_ATTN))],
        compiler_params=_params(("arbitrary",)))(dmix, w_out, attn, attn_scale)


def _attn_bwd(q, k, v, do, totals):
    s = q.shape[0]
    tb = ATT_BLOCK
    nq = s // tb

    def body(q_ref, do_ref, t_ref, k_hbm, v_hbm, dq_ref, dk_hbm, dv_hbm,
             k_scr, v_scr, dkt_acc, dvt_acc, stage, *bufs):
        hp = pl.program_id(0)
        i = pl.program_id(1)
        lanes = pl.ds(pl.multiple_of(hp * HEAD_PAIR, HEAD_PAIR), HEAD_PAIR)

        @pl.when(i == 0)
        def _():
            pltpu.sync_copy(k_hbm.at[:, lanes], k_scr)
            pltpu.sync_copy(v_hbm.at[:, lanes], v_scr)
            dkt_acc[...] = jnp.zeros_like(dkt_acc)
            dvt_acc[...] = jnp.zeros_like(dvt_acc)

        upper = _tri("suffix")
        lower = _tri("prefix")
        causal = _causal_mask()
        lane = lax.broadcasted_iota(jnp.int32, (1, HEAD_PAIR), 1)
        first = lane < 64
        q2 = q_ref[...]
        do2 = do_ref[...]
        zero = jnp.zeros_like(q2)
        qs = (jnp.where(first, q2, zero), jnp.where(first, zero, q2))
        dos = (jnp.where(first, do2, zero), jnp.where(first, zero, do2))
        qcat_t = jnp.concatenate(qs, axis=0).astype(F32).T.astype(BF16)
        docat_t = jnp.concatenate(dos, axis=0).astype(F32).T.astype(BF16)
        tots = (t_ref[:, 0:1], t_ref[:, 1:2])

        z_buf, zc_buf, sg_buf, sg2_buf, in_buf, da_buf, dw_buf, pre_buf = [
            [[bufs[4 * kind + 2 * slot + e] for e in range(2)] for slot in range(2)]
            for kind in range(8)]
        pr_buf, dzr_buf, dzc_buf = bufs[32:34], bufs[34:36], bufs[36:38]

        for e in range(2):
            z_buf[1][e][...] = jnp.full((tb, tb), NEG_BIG, F32)
            zc_buf[1][e][...] = jnp.full((tb, tb), NEG_BIG, F32)
            for buf in (sg_buf, sg2_buf, in_buf, da_buf, dw_buf, pre_buf):
                buf[1][e][...] = jnp.zeros((tb, tb), F32)
        for buf in (pr_buf, dzr_buf, dzc_buf):
            buf[1][...] = jnp.zeros_like(buf[1])

        def rows(p):
            return pl.ds(pl.multiple_of(jnp.clip(p, 0, nq - 1) * tb, tb), tb)

        def split_heads(block):
            return jnp.concatenate([jnp.where(first, block, zero), jnp.where(first, zero, block)], axis=0)

        def trip(t, w, carry):
            r = 1 - w
            cs, cps, dq = list(carry[0:2]), list(carry[2:4]), carry[4]
            live4 = (t - 4 >= 0) & (t - 4 < i)
            dq = dq + jnp.where(live4, _dot(dzc_buf[r][...], split_heads(k_scr[rows(t - 4), :])), 0.0)
            dkt_acc[jnp.clip(t - 4, 0, nq - 1)] += jnp.where(live4, _dot(qcat_t, dzr_buf[r][...]), 0.0)
            dvt_acc[jnp.clip(t - 3, 0, nq - 1)] += _dot(docat_t, pr_buf[r][...])
            kj = k_scr[rows(t), :]
            for e in range(2):
                z_buf[w][e][...] = _dot_nt(qs[e], kj)
            vj = v_scr[rows(t - 1), :]
            for e in range(2):
                z = z_buf[r][e][...]
                sp, sig = _softplus(z, True)
                zc_buf[w][e][...] = z
                sg_buf[w][e][...] = sig
                in_buf[w][e][...] = _dot(sp.astype(BF16), upper)
                da_buf[w][e][...] = _dot_nt(dos[e], vj)
            for e in range(2):
                incl = in_buf[r][e][...]
                cs[e] = cs[e] + incl[:, 0:1]
                off = jnp.where(t - 2 < i, tots[e] - cs[e], -NEG_BIG)
                a = jnp.exp(zc_buf[r][e][...] - incl - off)
                dw = a * da_buf[r][e][...]
                dw_buf[w][e][...] = dw
                sg2_buf[w][e][...] = sg_buf[r][e][...]
                pr_buf[w][e * tb:(e + 1) * tb, :] = a.astype(BF16)
                pre_buf[w][e][...] = _dot(dw.astype(BF16), lower)
            for e in range(2):
                pre = pre_buf[r][e][...] + cps[e]
                dzb = (dw_buf[r][e][...] - sg2_buf[r][e][...] * pre).astype(BF16)
                cps[e] = pre[:, tb - 1:tb]
                dzr_buf[w][e * tb:(e + 1) * tb, :] = dzb
                dzc_buf[w][:, e * tb:(e + 1) * tb] = dzb
            return cs[0], cs[1], cps[0], cps[1], dq

        col = jnp.zeros((tb, 1), F32)
        carry = (col, col, col, col, jnp.zeros((tb, HEAD_PAIR), F32))
        def four_trips(n, cr):
            for u in range(4):
                cr = trip(4 * n + u, u % 2, cr)
            return cr

        carry = lax.fori_loop(0, jnp.where(i > 0, (i + 7) // 4, 0), four_trips, carry)

        cps, dq = carry[2:4], carry[4]
        kj = k_scr[rows(i), :]
        vj = v_scr[rows(i), :]
        dzs, probs = [], []
        for e in range(2):
            z = _dot_nt(qs[e], kj)
            sp, sig = _softplus(z, True)
            incl = _dot(jnp.where(causal, sp, 0.0).astype(BF16), upper)
            a = jnp.where(causal, jnp.exp(z - incl), 0.0)
            dw = a * _dot_nt(dos[e], vj)
            pre = _dot(dw.astype(BF16), lower) + cps[e]
            dzs.append(jnp.where(causal, dw - sig * pre, 0.0).astype(BF16))
            probs.append(a.astype(BF16))
        dq = dq + _dot(jnp.concatenate(dzs, axis=1), split_heads(kj))
        dkt_acc[i] += _dot(qcat_t, jnp.concatenate(dzs, axis=0))
        dvt_acc[i] += _dot(docat_t, jnp.concatenate(probs, axis=0))
        dq_ref[...] = (dq * Q_SCALE).astype(BF16)

        @pl.when(i == nq - 1)
        def _():
            for acc, dst in ((dkt_acc, dk_hbm), (dvt_acc, dv_hbm)):
                def flip(n, _, acc=acc):
                    at = pl.ds(pl.multiple_of(n * tb, tb), tb)
                    stage[at, :] = acc[n].T
                    return 0
                lax.fori_loop(0, nq, flip, 0)
                pltpu.sync_copy(stage, dst.at[:, lanes])

    blk = pl.BlockSpec((tb, HEAD_PAIR), lambda h, i: (i, h))
    grad = jax.ShapeDtypeStruct((s, D_ATTN), F32)
    return _call(
        body, name="attn_bwd",
        out_shape=[jax.ShapeDtypeStruct((s, D_ATTN), BF16), grad, grad],
        grid=(4, nq),
        in_specs=[blk, blk, pl.BlockSpec((None, tb, 2), lambda h, i: (h, i, 0)), ANY, ANY],
        out_specs=[blk, ANY, ANY],
        scratch_shapes=[pltpu.VMEM((s, HEAD_PAIR), BF16), pltpu.VMEM((s, HEAD_PAIR), BF16),
                        pltpu.VMEM((nq, HEAD_PAIR, tb), F32), pltpu.VMEM((nq, HEAD_PAIR, tb), F32),
                        pltpu.VMEM((s, HEAD_PAIR), F32)]
        + [pltpu.VMEM((tb, tb), F32)] * 32
        + [pltpu.VMEM((2 * tb, tb), BF16)] * 4 + [pltpu.VMEM((tb, 2 * tb), BF16)] * 2,
        compiler_params=_params(("arbitrary", "arbitrary"), vmem_mb=60))(q, do, totals, k, v)


def _pool_bwd(u, dmp, w_pool, pool_scale):
    s = u.shape[0]
    ts = 512
    nr = s // ts
    per = ts // POOL_HALO

    def body(u_ref, halo_ref, dm_ref, wp_ref, ps_ref, du_ref, dwp_ref, dps_ref, ext_ref, y_ref, dext_ref):
        i = pl.program_id(0)
        rb = nr - 1 - i

        @pl.when(i == 0)
        def _():
            dext_ref[ts:, :] = jnp.zeros((POOL_HALO, D_POOL), F32)
            dwp_ref[...] = jnp.zeros_like(dwp_ref)
            dps_ref[...] = jnp.zeros_like(dps_ref)

        ext_ref[0:POOL_HALO, :] = jnp.where(rb > 0, halo_ref[...], 0.0)
        ext_ref[POOL_HALO:, :] = u_ref[...]
        ps, cnts = [], []
        for g, window in enumerate(POOL_WINDOWS):
            p, cnt = _pool_means(ext_ref, g, window, ts, rb * ts)
            ps.append(p.astype(BF16))
            cnts.append(cnt)
            y_ref[:, g * POOL_GROUP:(g + 1) * POOL_GROUP] = _dot(ps[g], wp_ref[g].astype(BF16))
        y = y_ref[...]
        r = _rstd(y)
        yn = y * r
        dm = dm_ref[...]
        dps_ref[...] += jnp.sum(dm * yn, axis=0, keepdims=True)
        dn = dm * ps_ref[...]
        dy = r * (dn - yn * jnp.mean(dn * yn, axis=-1, keepdims=True))
        for g, window in enumerate(POOL_WINDOWS):
            cols = slice(g * POOL_GROUP, (g + 1) * POOL_GROUP)
            dyg = dy[:, cols].astype(BF16)
            dwp_ref[g] += _dot_tn(ps[g], dyg)
            dp = _dot_nt(dyg, wp_ref[g].astype(BF16))
            dext_ref[0:ts, cols] = dp / cnts[g]
            acc = dext_ref[0:ts, cols]
            for d in range(1, window):
                acc = acc + dext_ref[d:d + ts, cols]
            du_ref[:, cols] = (acc - dp).astype(BF16)
        dext_ref[ts:, :] = dext_ref[0:POOL_HALO, :]

    rows = pl.BlockSpec((ts, D_POOL), lambda i: (nr - 1 - i, 0))
    return _call(
        body, name="pool_bwd",
        out_shape=[jax.ShapeDtypeStruct((s, D_POOL), BF16), jax.ShapeDtypeStruct(w_pool.shape, F32),
                   jax.ShapeDtypeStruct((1, D_POOL), F32)],
        grid=(nr,),
        in_specs=[rows,
                  pl.BlockSpec((POOL_HALO, D_POOL), lambda i: (jnp.maximum((nr - 1 - i) * per - 1, 0), 0)),
                  rows, _full(w_pool.shape), _full((1, D_POOL))],
        out_specs=[rows, _full(w_pool.shape), _full((1, D_POOL))],
        scratch_shapes=[pltpu.VMEM((ts + POOL_HALO, D_POOL), F32), pltpu.VMEM((ts, D_POOL), F32),
                        pltpu.VMEM((ts + POOL_HALO, D_POOL), F32)],
        compiler_params=_params(("arbitrary",)))(u, u, dmp, w_pool, pool_scale)


def _in_proj_bwd(du, dq, dk, dv, w_in, x, dx1, g1):
    s = x.shape[0]
    ts = 512

    def body(du_ref, dq_ref, dk_ref, dv_ref, w_ref, x_ref, dx1_ref, g_ref, gx_ref, dg_ref):
        @pl.when(pl.program_id(0) == 0)
        def _():
            dg_ref[...] = jnp.zeros_like(dg_ref)

        dh = _dot_nt(du_ref[...], w_ref[0]) + _dot_nt(dq_ref[...], w_ref[1])
        dh = dh + _dot_nt(dk_ref[...].astype(BF16), w_ref[2]) + _dot_nt(dv_ref[...].astype(BF16), w_ref[3])
        xv = x_ref[...]
        r = _rstd(xv)
        xn = xv * r
        dg_ref[...] += jnp.sum(dh * xn, axis=0, keepdims=True)
        dxn = dh * g_ref[...]
        gx_ref[...] = dx1_ref[...] + r * (dxn - xn * jnp.mean(dxn * xn, axis=-1, keepdims=True))

    row = lambda w: pl.BlockSpec((ts, w), lambda i: (i, 0))
    return _call(
        body, name="in_proj_bwd",
        out_shape=[jax.ShapeDtypeStruct((s, D_MODEL), F32), jax.ShapeDtypeStruct((1, D_MODEL), F32)],
        grid=(s // ts,),
        in_specs=[row(D_POOL)] * 4 + [_full(w_in.shape), row(D_MODEL), row(D_MODEL), _full((1, D_MODEL))],
        out_specs=[row(D_MODEL), _full((1, D_MODEL))],
        compiler_params=_params(("arbitrary",)))(du, dq, dk, dv, w_in, x, dx1, g1)


_SMALL = ("norm_mix_pre", "w_pool", "pool_scale", "attn_scale", "norm_mix_post",
          "norm_ffn_pre", "conv_b", "norm_ffn_post")
_SMALL_SIZE = {"norm_mix_pre": 1024, "w_pool": 65536, "pool_scale": 512, "attn_scale": 512,
               "norm_mix_post": 1024, "norm_ffn_pre": 1024, "conv_b": 5632, "norm_ffn_post": 1024}
_SMALL_ROWS = 600
_CONVW_ROWS = 132
_PACK_ROWS = _SMALL_ROWS + _CONVW_ROWS + 4


def _pack_small(parts):
    flat = jnp.concatenate([parts[n].reshape(-1) for n in _SMALL])
    flat = jnp.pad(flat, (0, _SMALL_ROWS * 128 - flat.shape[0]))
    return flat.reshape(_SMALL_ROWS, 128)


def _unpack_small(packed, like):
    flat = packed.reshape(-1)
    out, off = {}, 0
    for n in _SMALL:
        out[n] = flat[off:off + _SMALL_SIZE[n]].reshape(like[n].shape)
        off += _SMALL_SIZE[n]
    return out


def kernel(x, norm_mix_pre, w_in, w_pool, pool_scale, attn_scale, w_out, norm_mix_post, norm_ffn_pre, w_up, conv_w, conv_b, w_down, norm_ffn_post, loss_target, m_norm_mix_pre, m_w_in, m_w_pool, m_pool_scale, m_attn_scale, m_w_out, m_norm_mix_post, m_norm_ffn_pre, m_w_up, m_conv_w, m_conv_b, m_w_down, m_norm_ffn_post, v_norm_mix_pre, v_w_in, v_w_pool, v_pool_scale, v_attn_scale, v_w_out, v_norm_mix_post, v_norm_ffn_pre, v_w_up, v_conv_w, v_conv_b, v_w_down, v_norm_ffn_post):
    weights = dict(norm_mix_pre=norm_mix_pre, w_in=w_in, w_pool=w_pool, pool_scale=pool_scale,
                   attn_scale=attn_scale, w_out=w_out, norm_mix_post=norm_mix_post,
                   norm_ffn_pre=norm_ffn_pre, w_up=w_up, conv_w=conv_w, conv_b=conv_b,
                   w_down=w_down, norm_ffn_post=norm_ffn_post)
    mom1 = dict(norm_mix_pre=m_norm_mix_pre, w_in=m_w_in, w_pool=m_w_pool, pool_scale=m_pool_scale,
                attn_scale=m_attn_scale, w_out=m_w_out, norm_mix_post=m_norm_mix_post,
                norm_ffn_pre=m_norm_ffn_pre, w_up=m_w_up, conv_w=m_conv_w, conv_b=m_conv_b,
                w_down=m_w_down, norm_ffn_post=m_norm_ffn_post)
    mom2 = dict(norm_mix_pre=v_norm_mix_pre, w_in=v_w_in, w_pool=v_w_pool, pool_scale=v_pool_scale,
                attn_scale=v_attn_scale, w_out=v_w_out, norm_mix_post=v_norm_mix_post,
                norm_ffn_pre=v_norm_ffn_pre, w_up=v_w_up, conv_w=v_conv_w, conv_b=v_conv_b,
                w_down=v_w_down, norm_ffn_post=v_norm_ffn_post)
    order = list(weights)

    xs = x[0]
    target = loss_target[0]
    wp = w_pool[0]
    shard = lax.axis_index("x") * 2 + lax.axis_index("y")

    slot = shard.astype(jnp.int32).reshape(1)
    win_g, = _gather_shards([_cast_bf16(w_in[0], "cast_w_in")])
    lands = [_into_slot(w_out[0], slot, BF16, "cast_w_out"), _into_slot(w_up[0], slot, BF16, "cast_w_up"),
             _into_slot(w_down[0], slot, BF16, "cast_w_down"), _into_slot(conv_w[0], slot, F32, "place_conv_w")]
    g_send, g_recv, g_lands, g_token = _exchange_start(None, lands, win_g, "gather_start")
    convb_g = conv_b[0].reshape(N_SHARD, 1, FF_TILE)

    u, q, k, v, vt, h1 = _in_proj(xs, norm_mix_pre + g_token[0:1, 0:1], win_g)
    mpool = _pool_fwd(u, wp, pool_scale)
    attn, totals = _attn_fwd(q, k, vt)
    _, (wout_g, wup_g, wdown_g, convw_g) = _exchange_wait(g_send, g_recv, g_lands, False, attn, "gather_wait")
    wout_f = wout_g.reshape(D_MODEL, D_MODEL)
    wdown_f = wdown_g.reshape(D_FF, D_MODEL)
    mattn, mix, x1, h2 = _mix_out(attn, mpool, xs, attn_scale, wout_f, norm_mix_post, norm_ffn_pre)
    upre_g, upre_v, f_in = _ffn_up(h2, wup_g, convw_g, convb_g)
    df, dy, loss_tile, d_post = _ffn_down(f_in, wdown_f, x1, target, norm_ffn_post)

    d_wdown = _tn_matmul(f_in, df[None], "dw_down")
    dug, duv, dcw_g, dcw_v, dcb_g, dcb_v = _ffn_bwd_act(df, wdown_f, upre_g, upre_v, convw_g, convb_g)
    d_wup = jnp.concatenate([_tn_matmul(h2[None], dug, "dw_up_gate")[0],
                             _tn_matmul(h2[None], duv, "dw_up_value")[0]], axis=0)
    early = [d_wup, d_wdown.reshape(N_SHARD, D_FF // N_SHARD, D_MODEL)]
    s_send, s_recv, s_thru, s_token = _exchange_start(
        early, [lax.empty((3,) + g.shape[1:], g.dtype) for g in early], d_wup, "scatter_start")
    dx1, dmix, d_ffn_pre, d_mix_post = _ffn_bwd_in(
        dug, duv, wup_g, x1, dy, mix, norm_ffn_pre + s_token[0:1, 0:1], norm_mix_post)
    d_wout = jnp.concatenate([_tn_matmul(mpool[None], dmix[None], "dw_out_pool")[0, 0],
                              _tn_matmul(mattn[None], dmix[None], "dw_out_attn")[0, 0]], axis=0)
    dmp, do, d_attn_scale = _mix_bwd(dmix, wout_f, attn, attn_scale)
    dq, dk, dv = _attn_bwd(q, k, v, do, totals)
    du, d_wpool, d_pool_scale = _pool_bwd(u, dmp, wp, pool_scale)
    d_win = jnp.stack([_tn_matmul(h1[None], t[None], "dw_in_%d" % n)[0, 0]
                       for n, t in enumerate((du, dq, dk, dv))])
    grad_x, d_mix_pre = _in_proj_bwd(du, dq, dk, dv, win_g, xs, dx1, norm_mix_pre)

    d_convw = jnp.concatenate([dcw_g, dcw_v], axis=0)
    d_convb = jnp.concatenate([dcb_g, dcb_v], axis=0).reshape(1, 2 * D_FF)
    small_parts = dict(norm_mix_pre=d_mix_pre, w_pool=d_wpool, pool_scale=d_pool_scale,
                       attn_scale=d_attn_scale, norm_mix_post=d_mix_post, norm_ffn_pre=d_ffn_pre,
                       conv_b=d_convb, norm_ffn_post=d_post)
    packed = jnp.concatenate([_pack_small(small_parts), d_convw.reshape(_CONVW_ROWS, 128),
                              loss_tile[0:4]], axis=0)
    late, gathered = _scatter_grads([d_win, d_wout.reshape(N_SHARD, D_MODEL // N_SHARD, D_MODEL)], packed)
    early_srcs, early_lands = _exchange_wait(s_send, s_recv, s_thru, True, grad_x, "scatter_wait")
    quarter = [_sum_slots(r, (3, 0, 1, 2), "sum_chips_%d" % n) for n, r in enumerate(late)]
    quarter += [_sum_own_and_received(early_srcs[n], slot, early_lands[n], "sum_chips_%d" % (n + 2))
                for n in range(2)]
    sibling = _swap_with_sibling(quarter)
    small_sum = _sum_slots(gathered, tuple(range(8)), "sum_small")

    results = {}
    for n, name in enumerate(("w_in", "w_out", "w_up", "w_down")):
        res = _adamw([quarter[n], sibling[n]], weights[name][0], mom1[name][0], mom2[name][0],
                     "adamw_" + name)
        results[name] = [t[None] for t in res]
    g_convw = lax.dynamic_slice_in_dim(
        small_sum[_SMALL_ROWS:_SMALL_ROWS + _CONVW_ROWS].reshape(N_SHARD, 3, FF_TILE), shard, 1, axis=0)[0]
    convw_pad = lambda t: jnp.pad(t, ((0, 5), (0, 0)))
    res = _adamw([convw_pad(g_convw)], convw_pad(conv_w[0]), convw_pad(m_conv_w[0]),
                 convw_pad(v_conv_w[0]), "adamw_conv_w")
    results["conv_w"] = [t[:3][None] for t in res]
    pack_w = _pack_small(weights)
    pack_m = _pack_small(mom1)
    pack_v = _pack_small(mom2)
    res = _adamw([small_sum[:_SMALL_ROWS]], pack_w, pack_m, pack_v, "adamw_small")
    unpacked = [_unpack_small(t, weights) for t in res]
    for name in _SMALL:
        results[name] = [t[name] for t in unpacked]

    loss = small_sum[_SMALL_ROWS + _CONVW_ROWS, 0]
    outs = [loss, grad_x[None]]
    for slot in range(4):
        outs.extend(results[name][slot] for name in order)
    return tuple(outs)
```

```python
import functools

import jax
import jax.numpy as jnp
from jax import lax
from jax.experimental import pallas as pl
from jax.experimental.pallas import tpu as pltpu

F32 = jnp.float32
BF16 = jnp.bfloat16

D_MODEL = 1024
D_POOL = 512
D_ATTN = 512
POOL_WINDOWS = (2, 4, 8, 16)
POOL_GROUP = 128
POOL_HALO = 16
CONV_HALO = 8
D_FF = 2816
FF_TILE = 1408
N_SHARD = 4
EPS = 1e-6
Q_SCALE = 0.125
ATT_BLOCK = 256
HEAD_PAIR = 128
MIB = 1 << 20
NEG_BIG = -1e30

ADAM_LR = 0.001
ADAM_B1 = 0.9
ADAM_B2 = 0.999
ADAM_EPS = 1e-08
ADAM_WD = 0.01
ADAM_STEP = 10

NT_DIMS = (((1,), (1,)), ((), ()))
TN_DIMS = (((0,), (0,)), ((), ()))
MESH = pl.DeviceIdType.MESH
ANY = pl.BlockSpec(memory_space=pl.ANY)
HBM_SPEC = pl.BlockSpec(memory_space=pltpu.HBM)
SEM_SPEC = pl.BlockSpec(memory_space=pltpu.SEMAPHORE)
DATAFLOW = pltpu.SideEffectType.DATAFLOW_SIDE_EFFECTING


def _call(body, **kw):
    return pl.pallas_call(body, **kw)


def _params(sem=None, vmem_mb=48):
    return pltpu.CompilerParams(dimension_semantics=sem, vmem_limit_bytes=vmem_mb * MIB)


def _rstd(v):
    return lax.rsqrt(jnp.mean(v * v, axis=-1, keepdims=True) + EPS)


def _dot(a, b):
    return jnp.dot(a, b, preferred_element_type=F32)


def _dot_nt(a, b):
    return lax.dot_general(a, b, NT_DIMS, preferred_element_type=F32)


def _dot_tn(a, b):
    return lax.dot_general(a, b, TN_DIMS, preferred_element_type=F32)


def _row_tile(rows, cap):
    t = min(rows, cap)
    t -= t % 8
    while rows % t:
        t -= 8
    return t


def _full(shape):
    nd = len(shape)
    return pl.BlockSpec(shape, lambda *_: (0,) * nd)


def _chip_peers():
    x, y, c = lax.axis_index("x"), lax.axis_index("y"), lax.axis_index("c")
    return x, y, c, [(1 - x, y), (x, 1 - y), (1 - x, 1 - y)]


def _cast_bf16(a, name):
    def body(a_ref, o_ref):
        o_ref[...] = a_ref[...].astype(BF16)

    return _call(body, name=name, out_shape=jax.ShapeDtypeStruct(a.shape, BF16),
                 grid=(1,), in_specs=[_full(a.shape)], out_specs=_full(a.shape),
                 compiler_params=_params(("arbitrary",)))(a)


def _into_slot(a, slot, dtype, name):
    nd = a.ndim

    def body(slot_ref, a_ref, o_ref):
        o_ref[...] = a_ref[...].astype(dtype)

    return _call(
        body, name=name, out_shape=jax.ShapeDtypeStruct((N_SHARD,) + a.shape, dtype),
        grid_spec=pltpu.PrefetchScalarGridSpec(
            num_scalar_prefetch=1, grid=(1,),
            in_specs=[pl.BlockSpec(a.shape, lambda i, slot_ref: (0,) * nd)],
            out_specs=pl.BlockSpec((None,) + a.shape, lambda i, slot_ref: (slot_ref[0],) + (0,) * nd)),
        compiler_params=_params(("arbitrary",)))(slot, a)


def _exchange_copies(srcs, lands, send, recv):
    x, y, c, chips = _chip_peers()
    copies = []
    for t in range(len(lands)):
        for k, (px, py) in enumerate(chips):
            copies.append(pltpu.make_async_remote_copy(
                src_ref=lands[t].at[2 * x + y] if srcs is None else srcs[t].at[2 * px + py],
                dst_ref=lands[t].at[2 * x + y] if srcs is None else lands[t].at[k],
                send_sem=send.at[3 * t + k], recv_sem=recv.at[3 * t + k],
                device_id=(px, py, c), device_id_type=MESH))
    return copies


def _exchange_start(srcs, lands, after, name):
    n = len(lands)
    operands = list(lands) if srcs is None else list(srcs) + list(lands)
    m = len(operands)

    def body(*refs):
        for cp in _exchange_copies(None if srcs is None else refs[:n], refs[m - n:m], refs[m + 1], refs[m + 2]):
            cp.start()
        refs[-1][...] = jnp.zeros_like(refs[-1])

    res = _call(
        body, name=name,
        out_shape=[pltpu.SemaphoreType.DMA((3 * n,)), pltpu.SemaphoreType.DMA((3 * n,))]
        + [pltpu.HBM(a.shape, a.dtype) for a in operands] + [jax.ShapeDtypeStruct((8, 128), F32)],
        in_specs=[HBM_SPEC] * m + [ANY],
        out_specs=[SEM_SPEC, SEM_SPEC] + [HBM_SPEC] * m + [pl.BlockSpec(memory_space=pltpu.VMEM)],
        input_output_aliases={j: j + 2 for j in range(m)},
        compiler_params=pltpu.CompilerParams(has_side_effects=DATAFLOW),
    )(*[pltpu.with_memory_space_constraint(a, pltpu.HBM) for a in operands], after)
    return res[0], res[1], res[2:2 + m], res[-1]


def _exchange_wait(send, recv, operands, scatter, after, name):
    m = len(operands)
    n = m // 2 if scatter else m

    def body(*refs):
        for cp in _exchange_copies(refs[:n] if scatter else None, refs[m - n:m], refs[m], refs[m + 1]):
            cp.wait_send()
            cp.wait_recv()

    res = _call(
        body, name=name, out_shape=[pltpu.HBM(a.shape, a.dtype) for a in operands],
        in_specs=[HBM_SPEC] * m + [SEM_SPEC, SEM_SPEC, ANY], out_specs=[HBM_SPEC] * m,
        input_output_aliases={j: j for j in range(m)},
        compiler_params=pltpu.CompilerParams(has_side_effects=DATAFLOW),
    )(*operands, send, recv, after)
    return res[:m - n], res[m - n:]


def _gather_shards(shards):
    n = len(shards)

    def body(*refs):
        ins, outs = refs[:n], refs[n:2 * n]
        send, recv, loc = refs[2 * n:]
        x, y, c, chips = _chip_peers()
        b = 2 * x + y
        local = [pltpu.make_async_copy(ins[t], outs[t].at[b], loc.at[t]) for t in range(n)]
        for cp in local:
            cp.start()
        remote = []
        for t in range(n):
            for k, (px, py) in enumerate(chips):
                remote.append(pltpu.make_async_remote_copy(
                    src_ref=ins[t], dst_ref=outs[t].at[b],
                    send_sem=send.at[3 * t + k], recv_sem=recv.at[3 * t + k],
                    device_id=(px, py, c), device_id_type=MESH))
        for cp in remote:
            cp.start()
        for cp in remote:
            cp.wait()
        for cp in local:
            cp.wait()

    return _call(
        body, name="gather_w_in",
        out_shape=[jax.ShapeDtypeStruct((N_SHARD,) + s.shape, s.dtype) for s in shards],
        in_specs=[ANY] * n, out_specs=[ANY] * n,
        scratch_shapes=[pltpu.SemaphoreType.DMA((3 * n,)), pltpu.SemaphoreType.DMA((3 * n,)),
                        pltpu.SemaphoreType.DMA((n,))],
    )(*shards)


def _scatter_grads(grads, small):
    n = len(grads)

    def body(*refs):
        ins, small_in = refs[:n], refs[n]
        outs, small_out = refs[n + 1:2 * n + 1], refs[2 * n + 1]
        send, recv, loc, ssend, srecv = refs[2 * n + 2:]
        x, y, c, chips = _chip_peers()
        b = 2 * x + y
        me = 4 * x + 2 * y + c
        local = [pltpu.make_async_copy(ins[t].at[b], outs[t].at[3], loc.at[t]) for t in range(n)]
        local.append(pltpu.make_async_copy(small_in, small_out.at[me], loc.at[n]))
        for cp in local:
            cp.start()
        remote = []
        for t in range(n):
            for k, (px, py) in enumerate(chips):
                remote.append(pltpu.make_async_remote_copy(
                    src_ref=ins[t].at[2 * px + py], dst_ref=outs[t].at[k],
                    send_sem=send.at[3 * t + k], recv_sem=recv.at[3 * t + k],
                    device_id=(px, py, c), device_id_type=MESH))
        for r in range(1, 8):
            px = 1 - x if r & 4 else x
            py = 1 - y if r & 2 else y
            pc = 1 - c if r & 1 else c
            remote.append(pltpu.make_async_remote_copy(
                src_ref=small_in, dst_ref=small_out.at[me],
                send_sem=ssend.at[r - 1], recv_sem=srecv.at[r - 1],
                device_id=(px, py, pc), device_id_type=MESH))
        for cp in remote:
            cp.start()
        for cp in remote:
            cp.wait()
        for cp in local:
            cp.wait()

    out_shape = [jax.ShapeDtypeStruct(g.shape, g.dtype) for g in grads]
    out_shape.append(jax.ShapeDtypeStruct((8,) + small.shape, small.dtype))
    res = _call(
        body, name="scatter_grads", out_shape=out_shape,
        in_specs=[ANY] * (n + 1), out_specs=[ANY] * (n + 1),
        scratch_shapes=[pltpu.SemaphoreType.DMA((3 * n,)), pltpu.SemaphoreType.DMA((3 * n,)),
                        pltpu.SemaphoreType.DMA((n + 1,)),
                        pltpu.SemaphoreType.DMA((7,)), pltpu.SemaphoreType.DMA((7,))],
    )(*grads, small)
    return res[:n], res[n]


def _swap_with_sibling(parts):
    n = len(parts)

    def body(*refs):
        ins, outs = refs[:n], refs[n:2 * n]
        send, recv = refs[2 * n:]
        x, y, c = lax.axis_index("x"), lax.axis_index("y"), lax.axis_index("c")
        copies = [pltpu.make_async_remote_copy(
            src_ref=ins[t], dst_ref=outs[t], send_sem=send.at[t], recv_sem=recv.at[t],
            device_id=(x, y, 1 - c), device_id_type=MESH) for t in range(n)]
        for cp in copies:
            cp.start()
        for cp in copies:
            cp.wait()

    return _call(
        body, name="swap_sibling",
        out_shape=[jax.ShapeDtypeStruct(p.shape, p.dtype) for p in parts],
        in_specs=[ANY] * n, out_specs=[ANY] * n,
        scratch_shapes=[pltpu.SemaphoreType.DMA((n,)), pltpu.SemaphoreType.DMA((n,))],
    )(*parts)


def _sum_slots(buf, order, name):
    k, rows, cols = buf.shape
    tr = _row_tile(rows, 256)

    def body(b_ref, o_ref):
        acc = b_ref[order[0]].astype(F32)
        for s in order[1:]:
            acc = acc + b_ref[s].astype(F32)
        o_ref[...] = acc

    return _call(body, name=name, out_shape=jax.ShapeDtypeStruct((rows, cols), F32),
                 grid=(rows // tr,),
                 in_specs=[pl.BlockSpec((k, tr, cols), lambda i: (0, i, 0))],
                 out_specs=pl.BlockSpec((tr, cols), lambda i: (i, 0)),
                 compiler_params=_params(("parallel",)))(buf)


def _sum_own_and_received(src, slot, land, name):
    _, rows, cols = src.shape
    tr = _row_tile(rows, 256)

    def body(slot_ref, s_ref, l_ref, o_ref):
        acc = s_ref[...].astype(F32)
        for k in range(3):
            acc = acc + l_ref[k].astype(F32)
        o_ref[...] = acc

    return _call(
        body, name=name, out_shape=jax.ShapeDtypeStruct((rows, cols), F32),
        grid_spec=pltpu.PrefetchScalarGridSpec(
            num_scalar_prefetch=1, grid=(rows // tr,),
            in_specs=[pl.BlockSpec((None, tr, cols), lambda i, slot_ref: (slot_ref[0], i, 0)),
                      pl.BlockSpec((3, tr, cols), lambda i, slot_ref: (0, i, 0))],
            out_specs=pl.BlockSpec((tr, cols), lambda i, slot_ref: (i, 0))),
        compiler_params=_params(("parallel",)))(slot, src, land)


def _adamw(grad_parts, w, m, v, name):
    rows, cols = w.shape
    tr = _row_tile(rows, 256)
    npart = len(grad_parts)

    def body(*refs):
        gp = refs[:npart]
        w_ref, m_ref, v_ref, g_out, d_out, m_out, v_out = refs[npart:]
        g = gp[0][...]
        for p in gp[1:]:
            g = g + p[...]
        mm = ADAM_B1 * m_ref[...] + (1.0 - ADAM_B1) * g
        vv = ADAM_B2 * v_ref[...] + (1.0 - ADAM_B2) * jnp.square(g)
        m_hat = mm / (1.0 - ADAM_B1 ** ADAM_STEP)
        v_hat = vv / (1.0 - ADAM_B2 ** ADAM_STEP)
        g_out[...] = g
        d_out[...] = -ADAM_LR * (m_hat / (jnp.sqrt(v_hat) + ADAM_EPS) + ADAM_WD * w_ref[...])
        m_out[...] = mm
        v_out[...] = vv

    spec = pl.BlockSpec((tr, cols), lambda i: (i, 0))
    shp = jax.ShapeDtypeStruct((rows, cols), F32)
    return _call(body, name=name, out_shape=[shp] * 4, grid=(rows // tr,),
                 in_specs=[spec] * (npart + 3), out_specs=[spec] * 4,
                 compiler_params=_params(("parallel",)))(*grad_parts, w, m, v)


def _in_proj(x, g1, w_in):
    s = x.shape[0]
    ts = 512

    def body(x_ref, g_ref, w_ref, u_ref, q_ref, k_ref, v_ref, vt_ref, h_ref):
        xv = x_ref[...]
        h = (xv * _rstd(xv) * g_ref[...]).astype(BF16)
        h_ref[...] = h
        u_ref[...] = _dot(h, w_ref[0])
        q_ref[...] = (_dot(h, w_ref[1]) * Q_SCALE).astype(BF16)
        k_ref[...] = _dot(h, w_ref[2]).astype(BF16)
        v = _dot(h, w_ref[3])
        v_ref[...] = v.astype(BF16)
        vt = v.T.astype(BF16)
        for n in range(ts // ATT_BLOCK):
            vt_ref[n] = vt[:, n * ATT_BLOCK:(n + 1) * ATT_BLOCK]

    row = lambda w: pl.BlockSpec((ts, w), lambda i: (i, 0))
    half = jax.ShapeDtypeStruct((s, D_POOL), BF16)
    return _call(
        body, name="in_proj",
        out_shape=[jax.ShapeDtypeStruct((s, D_POOL), F32), half, half, half,
                   jax.ShapeDtypeStruct((s // ATT_BLOCK, D_ATTN, ATT_BLOCK), BF16),
                   jax.ShapeDtypeStruct((s, D_MODEL), BF16)],
        grid=(s // ts,),
        in_specs=[row(D_MODEL), _full((1, D_MODEL)), _full(w_in.shape)],
        out_specs=[row(D_POOL)] * 4
        + [pl.BlockSpec((ts // ATT_BLOCK, D_ATTN, ATT_BLOCK), lambda i: (i, 0, 0)), row(D_MODEL)],
        compiler_params=_params(("parallel",)))(x, g1, w_in)


def _pool_means(ext_ref, g, window, ts, row0):
    cols = slice(g * POOL_GROUP, (g + 1) * POOL_GROUP)
    cur = ext_ref[POOL_HALO:POOL_HALO + ts, cols]
    acc = cur
    for d in range(1, window):
        acc = acc + ext_ref[POOL_HALO - d:POOL_HALO - d + ts, cols]
    t1 = row0 + 1 + lax.broadcasted_iota(jnp.int32, (ts, 1), 0)
    cnt = jnp.minimum(t1, window).astype(F32)
    return acc / cnt - cur, cnt


def _pool_fwd(u, w_pool, pool_scale):
    s = u.shape[0]
    ts = 512
    per = ts // POOL_HALO

    def body(u_ref, halo_ref, wp_ref, ps_ref, o_ref, ext_ref, y_ref):
        i = pl.program_id(0)
        ext_ref[0:POOL_HALO, :] = jnp.where(i > 0, halo_ref[...], 0.0)
        ext_ref[POOL_HALO:, :] = u_ref[...]
        for g, window in enumerate(POOL_WINDOWS):
            p, _ = _pool_means(ext_ref, g, window, ts, i * ts)
            y_ref[:, g * POOL_GROUP:(g + 1) * POOL_GROUP] = _dot(
                p.astype(BF16), wp_ref[g].astype(BF16))
        y = y_ref[...]
        o_ref[...] = (y * _rstd(y) * ps_ref[...]).astype(BF16)

    return _call(
        body, name="pool_fwd", out_shape=jax.ShapeDtypeStruct((s, D_POOL), BF16),
        grid=(s // ts,),
        in_specs=[pl.BlockSpec((ts, D_POOL), lambda i: (i, 0)),
                  pl.BlockSpec((POOL_HALO, D_POOL), lambda i: (jnp.maximum(i * per - 1, 0), 0)),
                  _full(w_pool.shape), _full((1, D_POOL))],
        out_specs=pl.BlockSpec((ts, D_POOL), lambda i: (i, 0)),
        scratch_shapes=[pltpu.VMEM((ts + POOL_HALO, D_POOL), F32), pltpu.VMEM((ts, D_POOL), F32)],
        compiler_params=_params(("parallel",)))(u, u, w_pool, pool_scale)


def _tri(kind):
    r = lax.broadcasted_iota(jnp.int32, (ATT_BLOCK, ATT_BLOCK), 0)
    c = lax.broadcasted_iota(jnp.int32, (ATT_BLOCK, ATT_BLOCK), 1)
    return jnp.where(r >= c if kind == "suffix" else r <= c, 1.0, 0.0).astype(BF16)


def _causal_mask():
    r = lax.broadcasted_iota(jnp.int32, (ATT_BLOCK, ATT_BLOCK), 0)
    c = lax.broadcasted_iota(jnp.int32, (ATT_BLOCK, ATT_BLOCK), 1)
    return c < r


def _softplus(z, with_sigmoid=False):
    ope = 1.0 + jnp.exp(jnp.minimum(z, 80.0))
    sp = jnp.maximum(z, jnp.log(ope))
    if with_sigmoid:
        return sp, 1.0 - 1.0 / ope
    return sp


def _attn_fwd(q, k, vt):
    s = q.shape[0]
    tb = ATT_BLOCK
    nq = s // tb

    def body(q_ref, k_ref, vt_ref, o_ref, t_ref, *bufs):
        i = pl.program_id(1)
        suffix = _tri("prefix")
        r_idx = lax.broadcasted_iota(jnp.int32, (tb, tb), 0)
        c_idx = lax.broadcasted_iota(jnp.int32, (tb, tb), 1)
        causal = r_idx < c_idx
        lane = lax.broadcasted_iota(jnp.int32, (1, HEAD_PAIR), 1)
        first = lane < 64
        top = lax.broadcasted_iota(jnp.int32, (HEAD_PAIR, 1), 0) < 64
        q2 = q_ref[...]
        zero = jnp.zeros_like(q2)
        qs = (jnp.where(first, q2, zero), jnp.where(first, zero, q2))

        def values_t(j):
            vt = vt_ref[j]
            none = jnp.zeros_like(vt)
            return jnp.concatenate([jnp.where(top, vt, none), jnp.where(top, none, vt)], axis=1)

        def scores(j, cs, masked):
            kj = k_ref[pl.ds(pl.multiple_of(j * tb, tb), tb), :]
            new_cs, args = [], []
            for e in range(2):
                z = _dot_nt(kj, qs[e])
                sp = _softplus(z)
                if masked:
                    sp = jnp.where(causal, sp, 0.0)
                incl = _dot(suffix, sp.astype(BF16))
                arg = z - incl - cs[e]
                if masked:
                    arg = jnp.where(causal, arg, NEG_BIG)
                args.append(arg)
                new_cs.append(cs[e] + incl[0:1, :])
            return new_cs, args

        def weigh(j, args, o):
            probs = [jnp.exp(arg).astype(BF16) for arg in args]
            return o + _dot(values_t(j), jnp.concatenate(probs, axis=0))

        z_buf, zc_buf, in_buf = [[[bufs[4 * kind + 2 * slot + e] for e in range(2)]
                                  for slot in range(2)] for kind in range(3)]
        pr_buf = [bufs[12], bufs[13]]

        @pl.when((pl.program_id(0) == 0) & (i == 0))
        def _():
            for b in bufs:
                b[...] = jnp.zeros_like(b)

        def block_of(p):
            return jnp.clip(i - 1 - p, 0, nq - 1)

        def trip(t, w, carry):
            r = 1 - w
            cs, o = list(carry[0:2]), carry[2]
            live3 = (t - 3 >= 0) & (t - 3 < i)
            o = o + jnp.where(live3, _dot(values_t(block_of(t - 3)), pr_buf[r][...]), 0.0)
            kj = k_ref[pl.ds(pl.multiple_of(block_of(t) * tb, tb), tb), :]
            for e in range(2):
                z_buf[w][e][...] = _dot_nt(kj, qs[e])
            for e in range(2):
                z = z_buf[r][e][...]
                zc_buf[w][e][...] = z
                in_buf[w][e][...] = _dot(suffix, _softplus(z).astype(BF16))
            live2 = (t - 2 >= 0) & (t - 2 < i)
            for e in range(2):
                incl = in_buf[r][e][...]
                arg = zc_buf[r][e][...] - incl - jnp.where(live2, cs[e], -NEG_BIG)
                pr_buf[w][e * tb:(e + 1) * tb, :] = jnp.exp(arg).astype(BF16)
                cs[e] = jnp.where(live2, cs[e] + incl[0:1, :], cs[e])
            return cs[0], cs[1], o

        def four_trips(n, cr):
            for u in range(4):
                cr = trip(4 * n + u, u % 2, cr)
            return cr

        row = jnp.zeros((1, tb), F32)
        cs, args = scores(i, (row, row), True)
        carry = (cs[0], cs[1], weigh(i, args, jnp.zeros((HEAD_PAIR, tb), F32)))
        carry = lax.fori_loop(0, jnp.where(i > 0, (i + 6) // 4, 0), four_trips, carry)
        o_ref[...] = carry[2].T
        totals = jnp.where(r_idx == 0, carry[0], jnp.where(r_idx == 1, carry[1], 0.0))
        t_ref[...] = totals.T[:, 0:2]

    score_buf = pltpu.VMEM((tb, tb), F32)
    return _call(
        body, name="attn_fwd",
        out_shape=[jax.ShapeDtypeStruct((s, D_ATTN), F32),
                   jax.ShapeDtypeStruct((4, s, 2), F32)],
        grid=(4, nq),
        in_specs=[pl.BlockSpec((tb, HEAD_PAIR), lambda h, i: (i, h)),
                  pl.BlockSpec((s, HEAD_PAIR), lambda h, i: (0, h)),
                  pl.BlockSpec((nq, HEAD_PAIR, tb), lambda h, i: (0, h, 0))],
        out_specs=[pl.BlockSpec((tb, HEAD_PAIR), lambda h, i: (i, h)),
                   pl.BlockSpec((None, tb, 2), lambda h, i: (h, i, 0))],
        scratch_shapes=[score_buf] * 12 + [pltpu.VMEM((2 * tb, tb), BF16)] * 2,
        compiler_params=_params(("arbitrary", "arbitrary")))(q, k, vt)


def _mix_out(attn, mpool, x, attn_scale, w_out, g2, g3):
    s = x.shape[0]
    ts = 512

    def body(a_ref, p_ref, x_ref, as_ref, w_ref, g2_ref, g3_ref, ma_ref, mix_ref, x1_ref, h2_ref):
        ao = a_ref[...]
        ma = (ao * _rstd(ao) * as_ref[...]).astype(BF16)
        ma_ref[...] = ma
        mix = _dot(p_ref[...], w_ref[0:D_POOL, :]) + _dot(ma, w_ref[D_POOL:, :])
        mix_ref[...] = mix
        x1 = x_ref[...] + mix * _rstd(mix) * g2_ref[...]
        x1_ref[...] = x1
        h2_ref[...] = (x1 * _rstd(x1) * g3_ref[...]).astype(BF16)

    row = lambda w: pl.BlockSpec((ts, w), lambda i: (i, 0))
    return _call(
        body, name="mix_out",
        out_shape=[jax.ShapeDtypeStruct((s, D_ATTN), BF16), jax.ShapeDtypeStruct((s, D_MODEL), F32),
                   jax.ShapeDtypeStruct((s, D_MODEL), F32), jax.ShapeDtypeStruct((s, D_MODEL), BF16)],
        grid=(s // ts,),
        in_specs=[row(D_ATTN), row(D_POOL), row(D_MODEL), _full((1, D_ATTN)),
                  _full((D_MODEL, D_MODEL)), _full((1, D_MODEL)), _full((1, D_MODEL))],
        out_specs=[row(D_ATTN), row(D_MODEL), row(D_MODEL), row(D_MODEL)],
        compiler_params=_params(("parallel",)))(attn, mpool, x, attn_scale, w_out, g2, g3)


def _earlier_rows(halo, block):
    ts = block.shape[0]
    ext = jnp.concatenate([halo, block], axis=0)
    return tuple(pltpu.roll(ext, d, axis=0)[CONV_HALO:CONV_HALO + ts, :] for d in (1, 2))


def _later_rows(block, halo):
    ts = block.shape[0]
    ext = jnp.concatenate([block, halo], axis=0)
    return tuple(pltpu.roll(ext, ts + CONV_HALO - d, axis=0)[0:ts, :] for d in (1, 2))


def _conv_rows(x0, x1, x2, cw, cb):
    y = cb + cw[0:1, :] * x2
    y = y + cw[1:2, :] * x1
    return y + cw[2:3, :] * x0


def _sigmoid(v):
    return 1.0 / (1.0 + jnp.exp(-v))


def _ffn_up(h2, w_up, conv_w, conv_b):
    s = h2.shape[0]
    ts = 256
    tn = FF_TILE

    def body(h_ref, wg_ref, wv_ref, cwg_ref, cwv_ref, cbg_ref, cbv_ref,
             ug_ref, uv_ref, f_ref, tailg, tailv):
        i = pl.program_id(1)

        @pl.when(i == 0)
        def _():
            tailg[...] = jnp.zeros_like(tailg)
            tailv[...] = jnp.zeros_like(tailv)

        h = h_ref[...]
        ug = _dot(h, wg_ref[...])
        uv = _dot(h, wv_ref[...])
        ug_ref[...] = ug
        uv_ref[...] = uv
        gate = _conv_rows(ug, *_earlier_rows(tailg[...], ug), cwg_ref[...], cbg_ref[...])
        val = _conv_rows(uv, *_earlier_rows(tailv[...], uv), cwv_ref[...], cbv_ref[...])
        f_ref[...] = (gate * _sigmoid(gate) * val).astype(BF16)
        tailg[...] = ug[ts - CONV_HALO:, :]
        tailv[...] = uv[ts - CONV_HALO:, :]

    out_blk = pl.BlockSpec((None, ts, tn), lambda n, i: (n, i, 0))
    act = jax.ShapeDtypeStruct((2, s, tn), F32)
    return _call(
        body, name="ffn_up",
        out_shape=[act, act, jax.ShapeDtypeStruct((2, s, tn), BF16)],
        grid=(2, s // ts),
        in_specs=[pl.BlockSpec((ts, D_MODEL), lambda n, i: (i, 0)),
                  pl.BlockSpec((None, D_MODEL, tn), lambda n, i: (n, 0, 0)),
                  pl.BlockSpec((None, D_MODEL, tn), lambda n, i: (n + 2, 0, 0)),
                  pl.BlockSpec((None, 3, tn), lambda n, i: (n, 0, 0)),
                  pl.BlockSpec((None, 3, tn), lambda n, i: (n + 2, 0, 0)),
                  pl.BlockSpec((None, 1, tn), lambda n, i: (n, 0, 0)),
                  pl.BlockSpec((None, 1, tn), lambda n, i: (n + 2, 0, 0))],
        out_specs=[out_blk, out_blk, out_blk],
        scratch_shapes=[pltpu.VMEM((CONV_HALO, tn), F32), pltpu.VMEM((CONV_HALO, tn), F32)],
        compiler_params=_params(("arbitrary", "arbitrary")))(
            h2, w_up, w_up, conv_w, conv_w, conv_b, conv_b)


def _ffn_down(f_in, w_down, x1, target, g4):
    s = x1.shape[0]
    ts = 512

    def body(f_ref, w_ref, x1_ref, t_ref, g_ref, df_ref, dy_ref, loss_ref, dg_ref):
        @pl.when(pl.program_id(0) == 0)
        def _():
            loss_ref[...] = jnp.zeros_like(loss_ref)
            dg_ref[...] = jnp.zeros_like(dg_ref)

        f = _dot(f_ref[0], w_ref[0:FF_TILE, :]) + _dot(f_ref[1], w_ref[FF_TILE:, :])
        rf = _rstd(f)
        fn = f * rf
        g = g_ref[...]
        err = (x1_ref[...] + fn * g) - t_ref[...]
        loss_ref[...] += 0.5 * jnp.sum(jnp.mean(err * err, axis=-1))
        dy = err * (1.0 / D_MODEL)
        dy_ref[...] = dy
        dg_ref[...] += jnp.sum(dy * fn, axis=0, keepdims=True)
        dfn = dy * g
        df_ref[...] = (rf * (dfn - fn * jnp.mean(dfn * fn, axis=-1, keepdims=True))).astype(BF16)

    row = pl.BlockSpec((ts, D_MODEL), lambda i: (i, 0))
    return _call(
        body, name="ffn_down",
        out_shape=[jax.ShapeDtypeStruct((s, D_MODEL), BF16), jax.ShapeDtypeStruct((s, D_MODEL), F32),
                   jax.ShapeDtypeStruct((8, 128), F32), jax.ShapeDtypeStruct((1, D_MODEL), F32)],
        grid=(s // ts,),
        in_specs=[pl.BlockSpec((2, ts, FF_TILE), lambda i: (0, i, 0)), _full((D_FF, D_MODEL)),
                  row, row, _full((1, D_MODEL))],
        out_specs=[row, row, _full((8, 128)), _full((1, D_MODEL))],
        compiler_params=_params(("arbitrary",)))(f_in, w_down, x1, target, g4)


def _tn_matmul(a, b, name, ts=512):
    na, s, ka = a.shape
    nb, _, nbc = b.shape
    steps = s // ts

    def body(a_ref, b_ref, o_ref, acc_ref):
        @pl.when(pl.program_id(2) == 0)
        def _():
            acc_ref[...] = jnp.zeros_like(acc_ref)

        acc_ref[...] += _dot_tn(a_ref[...].astype(BF16), b_ref[...].astype(BF16))

        @pl.when(pl.program_id(2) == steps - 1)
        def _():
            o_ref[...] = acc_ref[...].astype(BF16)

    return _call(
        body, name=name, out_shape=jax.ShapeDtypeStruct((na, nb, ka, nbc), BF16),
        grid=(na, nb, steps),
        in_specs=[pl.BlockSpec((None, ts, ka), lambda i, j, r: (i, r, 0)),
                  pl.BlockSpec((None, ts, nbc), lambda i, j, r: (j, r, 0))],
        out_specs=pl.BlockSpec((None, None, ka, nbc), lambda i, j, r: (i, j, 0, 0)),
        scratch_shapes=[pltpu.VMEM((ka, nbc), F32)],
        compiler_params=_params(("parallel", "parallel", "arbitrary")))(a, b)


def _ffn_bwd_act(df, w_down, upre_g, upre_v, conv_w, conv_b):
    s = df.shape[0]
    ts = 256
    tn = FF_TILE
    nr = s // ts
    per = ts // CONV_HALO

    def body(df_ref, wd_ref, ug_ref, uv_ref, hg_ref, hv_ref, cwg_ref, cwv_ref, cbg_ref, cbv_ref,
             dug_ref, duv_ref, dcwg_ref, dcwv_ref, dcbg_ref, dcbv_ref, headg, headv):
        i = pl.program_id(1)
        first_rows = i == nr - 1

        @pl.when(i == 0)
        def _():
            for r in (headg, headv, dcwg_ref, dcwv_ref, dcbg_ref, dcbv_ref):
                r[...] = jnp.zeros_like(r)

        ug, uv = ug_ref[...], uv_ref[...]
        cwg, cwv = cwg_ref[...], cwv_ref[...]
        gate = _conv_rows(ug, *_earlier_rows(jnp.where(first_rows, 0.0, hg_ref[...]), ug), cwg, cbg_ref[...])
        val = _conv_rows(uv, *_earlier_rows(jnp.where(first_rows, 0.0, hv_ref[...]), uv), cwv, cbv_ref[...])
        sg = _sigmoid(gate)
        dfin = _dot_nt(df_ref[...], wd_ref[...])
        dval = dfin * (gate * sg)
        dgate = dfin * val * (sg * (1.0 + gate * (1.0 - sg)))

        def conv_bwd(dact, x, head, cw, dcw_ref, dcb_ref, du_ref):
            d1, d2 = _later_rows(dact, head[...])
            dcb_ref[...] += jnp.sum(dact, axis=0, keepdims=True)
            for kk, shifted in enumerate((d2, d1, dact)):
                dcw_ref[kk:kk + 1, :] += jnp.sum(x * shifted, axis=0, keepdims=True)
            du_ref[...] = (cw[2:3, :] * dact + cw[1:2, :] * d1 + cw[0:1, :] * d2).astype(BF16)
            head[...] = dact[0:CONV_HALO, :]

        conv_bwd(dgate, ug, headg, cwg, dcwg_ref, dcbg_ref, dug_ref)
        conv_bwd(dval, uv, headv, cwv, dcwv_ref, dcbv_ref, duv_ref)

    rows = lambda n, i: (n, nr - 1 - i, 0)
    halo = lambda n, i: (n, jnp.maximum((nr - 1 - i) * per - 1, 0), 0)
    act_blk = pl.BlockSpec((None, ts, tn), rows)
    halo_blk = pl.BlockSpec((None, CONV_HALO, tn), halo)
    cw_blk = lambda off: pl.BlockSpec((None, 3, tn), lambda n, i: (n + off, 0, 0))
    cb_blk = lambda off: pl.BlockSpec((None, 1, tn), lambda n, i: (n + off, 0, 0))
    acc_w = pl.BlockSpec((None, 3, tn), lambda n, i: (n, 0, 0))
    acc_b = pl.BlockSpec((None, 1, tn), lambda n, i: (n, 0, 0))
    dact = jax.ShapeDtypeStruct((2, s, tn), BF16)
    return _call(
        body, name="ffn_bwd_act",
        out_shape=[dact, dact, jax.ShapeDtypeStruct((2, 3, tn), F32), jax.ShapeDtypeStruct((2, 3, tn), F32),
                   jax.ShapeDtypeStruct((2, 1, tn), F32), jax.ShapeDtypeStruct((2, 1, tn), F32)],
        grid=(2, nr),
        in_specs=[pl.BlockSpec((ts, D_MODEL), lambda n, i: (nr - 1 - i, 0)),
                  pl.BlockSpec((tn, D_MODEL), lambda n, i: (n, 0)),
                  act_blk, act_blk, halo_blk, halo_blk,
                  cw_blk(0), cw_blk(2), cb_blk(0), cb_blk(2)],
        out_specs=[act_blk, act_blk, acc_w, acc_w, acc_b, acc_b],
        scratch_shapes=[pltpu.VMEM((CONV_HALO, tn), F32)] * 2,
        compiler_params=_params(("arbitrary", "arbitrary")))(
            df, w_down, upre_g, upre_v, upre_g, upre_v, conv_w, conv_w, conv_b, conv_b)


def _ffn_bwd_in(dug, duv, w_up, x1, dy, mix, g3, g2):
    s = x1.shape[0]
    ts = 256

    def body(dg_ref, dv_ref, w_ref, x1_ref, dy_ref, mix_ref, g3_ref, g2_ref,
             dx1_ref, dmix_ref, dg3_ref, dg2_ref):
        @pl.when(pl.program_id(0) == 0)
        def _():
            dg3_ref[...] = jnp.zeros_like(dg3_ref)
            dg2_ref[...] = jnp.zeros_like(dg2_ref)

        dh = _dot_nt(dg_ref[0], w_ref[0]) + _dot_nt(dg_ref[1], w_ref[1])
        dh = dh + _dot_nt(dv_ref[0], w_ref[2]) + _dot_nt(dv_ref[1], w_ref[3])
        x1 = x1_ref[...]
        r3 = _rstd(x1)
        xn = x1 * r3
        dg3_ref[...] += jnp.sum(dh * xn, axis=0, keepdims=True)
        dxn = dh * g3_ref[...]
        dx1 = dy_ref[...] + r3 * (dxn - xn * jnp.mean(dxn * xn, axis=-1, keepdims=True))
        dx1_ref[...] = dx1
        mix = mix_ref[...]
        rm = _rstd(mix)
        mn = mix * rm
        dg2_ref[...] += jnp.sum(dx1 * mn, axis=0, keepdims=True)
        dmn = dx1 * g2_ref[...]
        dmix_ref[...] = (rm * (dmn - mn * jnp.mean(dmn * mn, axis=-1, keepdims=True))).astype(BF16)

    row = pl.BlockSpec((ts, D_MODEL), lambda i: (i, 0))
    act = pl.BlockSpec((2, ts, FF_TILE), lambda i: (0, i, 0))
    vec = _full((1, D_MODEL))
    return _call(
        body, name="ffn_bwd_in",
        out_shape=[jax.ShapeDtypeStruct((s, D_MODEL), F32), jax.ShapeDtypeStruct((s, D_MODEL), BF16),
                   jax.ShapeDtypeStruct((1, D_MODEL), F32), jax.ShapeDtypeStruct((1, D_MODEL), F32)],
        grid=(s // ts,),
        in_specs=[act, act, _full(w_up.shape), row, row, row, vec, vec],
        out_specs=[row, row, vec, vec],
        compiler_params=_params(("arbitrary",), vmem_mb=56))(dug, duv, w_up, x1, dy, mix, g3, g2)


def _mix_bwd(dmix, w_out, attn, attn_scale):
    s = dmix.shape[0]
    ts = 512

    def body(dm_ref, w_ref, a_ref, as_ref, dp_ref, do_ref, das_ref):
        @pl.when(pl.program_id(0) == 0)
        def _():
            das_ref[...] = jnp.zeros_like(das_ref)

        dm = dm_ref[...]
        dp_ref[...] = _dot_nt(dm, w_ref[0:D_POOL, :])
        da = _dot_nt(dm, w_ref[D_POOL:, :])
        ao = a_ref[...]
        ra = _rstd(ao)
        an = ao * ra
        das_ref[...] += jnp.sum(da * an, axis=0, keepdims=True)
        dan = da * as_ref[...]
        do_ref[...] = (ra * (dan - an * jnp.mean(dan * an, axis=-1, keepdims=True))).astype(BF16)

    row = lambda w: pl.BlockSpec((ts, w), lambda i: (i, 0))
    return _call(
        body, name="mix_bwd",
        out_shape=[jax.ShapeDtypeStruct((s, D_POOL), F32), jax.ShapeDtypeStruct((s, D_ATTN), BF16),
                   jax.ShapeDtypeStruct((1, D_ATTN), F32)],
        grid=(s // ts,),
        in_specs=[row(D_MODEL), _full((D_MODEL, D_MODEL)), row(D_ATTN), _full((1, D_ATTN))],
        out_specs=[row(D_POOL), row(D_ATTN), _full((1, D_ATTN))],
        compiler_params=_params(("arbitrary",)))(dmix, w_out, attn, attn_scale)


def _attn_bwd(q, k, v, do, totals):
    s = q.shape[0]
    tb = ATT_BLOCK
    nq = s // tb

    def body(q_ref, do_ref, t_ref, k_hbm, v_hbm, dq_ref, dk_hbm, dv_hbm,
             k_scr, v_scr, dkt_acc, dvt_acc, stage, *bufs):
        hp = pl.program_id(0)
        i = pl.program_id(1)
        lanes = pl.ds(pl.multiple_of(hp * HEAD_PAIR, HEAD_PAIR), HEAD_PAIR)

        @pl.when(i == 0)
        def _():
            pltpu.sync_copy(k_hbm.at[:, lanes], k_scr)
            pltpu.sync_copy(v_hbm.at[:, lanes], v_scr)
            dkt_acc[...] = jnp.zeros_like(dkt_acc)
            dvt_acc[...] = jnp.zeros_like(dvt_acc)

        upper = _tri("suffix")
        lower = _tri("prefix")
        causal = _causal_mask()
        lane = lax.broadcasted_iota(jnp.int32, (1, HEAD_PAIR), 1)
        first = lane < 64
        q2 = q_ref[...]
        do2 = do_ref[...]
        zero = jnp.zeros_like(q2)
        qs = (jnp.where(first, q2, zero), jnp.where(first, zero, q2))
        dos = (jnp.where(first, do2, zero), jnp.where(first, zero, do2))
        qcat_t = jnp.concatenate(qs, axis=0).astype(F32).T.astype(BF16)
        docat_t = jnp.concatenate(dos, axis=0).astype(F32).T.astype(BF16)
        tots = (t_ref[:, 0:1], t_ref[:, 1:2])

        z_buf, zc_buf, sg_buf, sg2_buf, in_buf, da_buf, dw_buf, pre_buf = [
            [[bufs[4 * kind + 2 * slot + e] for e in range(2)] for slot in range(2)]
            for kind in range(8)]
        pr_buf, dzr_buf, dzc_buf = bufs[32:34], bufs[34:36], bufs[36:38]

        for e in range(2):
            z_buf[1][e][...] = jnp.full((tb, tb), NEG_BIG, F32)
            zc_buf[1][e][...] = jnp.full((tb, tb), NEG_BIG, F32)
            for buf in (sg_buf, sg2_buf, in_buf, da_buf, dw_buf, pre_buf):
                buf[1][e][...] = jnp.zeros((tb, tb), F32)
        for buf in (pr_buf, dzr_buf, dzc_buf):
            buf[1][...] = jnp.zeros_like(buf[1])

        def rows(p):
            return pl.ds(pl.multiple_of(jnp.clip(p, 0, nq - 1) * tb, tb), tb)

        def split_heads(block):
            return jnp.concatenate([jnp.where(first, block, zero), jnp.where(first, zero, block)], axis=0)

        def trip(t, w, carry):
            r = 1 - w
            cs, cps, dq = list(carry[0:2]), list(carry[2:4]), carry[4]
            live4 = (t - 4 >= 0) & (t - 4 < i)
            dq = dq + jnp.where(live4, _dot(dzc_buf[r][...], split_heads(k_scr[rows(t - 4), :])), 0.0)
            dkt_acc[jnp.clip(t - 4, 0, nq - 1)] += jnp.where(live4, _dot(qcat_t, dzr_buf[r][...]), 0.0)
            dvt_acc[jnp.clip(t - 3, 0, nq - 1)] += _dot(docat_t, pr_buf[r][...])
            kj = k_scr[rows(t), :]
            for e in range(2):
                z_buf[w][e][...] = _dot_nt(qs[e], kj)
            vj = v_scr[rows(t - 1), :]
            for e in range(2):
                z = z_buf[r][e][...]
                sp, sig = _softplus(z, True)
                zc_buf[w][e][...] = z
                sg_buf[w][e][...] = sig
                in_buf[w][e][...] = _dot(sp.astype(BF16), upper)
                da_buf[w][e][...] = _dot_nt(dos[e], vj)
            for e in range(2):
                incl = in_buf[r][e][...]
                cs[e] = cs[e] + incl[:, 0:1]
                off = jnp.where(t - 2 < i, tots[e] - cs[e], -NEG_BIG)
                a = jnp.exp(zc_buf[r][e][...] - incl - off)
                dw = a * da_buf[r][e][...]
                dw_buf[w][e][...] = dw
                sg2_buf[w][e][...] = sg_buf[r][e][...]
                pr_buf[w][e * tb:(e + 1) * tb, :] = a.astype(BF16)
                pre_buf[w][e][...] = _dot(dw.astype(BF16), lower)
            for e in range(2):
                pre = pre_buf[r][e][...] + cps[e]
                dzb = (dw_buf[r][e][...] - sg2_buf[r][e][...] * pre).astype(BF16)
                cps[e] = pre[:, tb - 1:tb]
                dzr_buf[w][e * tb:(e + 1) * tb, :] = dzb
                dzc_buf[w][:, e * tb:(e + 1) * tb] = dzb
            return cs[0], cs[1], cps[0], cps[1], dq

        col = jnp.zeros((tb, 1), F32)
        carry = (col, col, col, col, jnp.zeros((tb, HEAD_PAIR), F32))
        def four_trips(n, cr):
            for u in range(4):
                cr = trip(4 * n + u, u % 2, cr)
            return cr

        carry = lax.fori_loop(0, jnp.where(i > 0, (i + 7) // 4, 0), four_trips, carry)

        cps, dq = carry[2:4], carry[4]
        kj = k_scr[rows(i), :]
        vj = v_scr[rows(i), :]
        dzs, probs = [], []
        for e in range(2):
            z = _dot_nt(qs[e], kj)
            sp, sig = _softplus(z, True)
            incl = _dot(jnp.where(causal, sp, 0.0).astype(BF16), upper)
            a = jnp.where(causal, jnp.exp(z - incl), 0.0)
            dw = a * _dot_nt(dos[e], vj)
            pre = _dot(dw.astype(BF16), lower) + cps[e]
            dzs.append(jnp.where(causal, dw - sig * pre, 0.0).astype(BF16))
            probs.append(a.astype(BF16))
        dq = dq + _dot(jnp.concatenate(dzs, axis=1), split_heads(kj))
        dkt_acc[i] += _dot(qcat_t, jnp.concatenate(dzs, axis=0))
        dvt_acc[i] += _dot(docat_t, jnp.concatenate(probs, axis=0))
        dq_ref[...] = (dq * Q_SCALE).astype(BF16)

        @pl.when(i == nq - 1)
        def _():
            for acc, dst in ((dkt_acc, dk_hbm), (dvt_acc, dv_hbm)):
                def flip(n, _, acc=acc):
                    at = pl.ds(pl.multiple_of(n * tb, tb), tb)
                    stage[at, :] = acc[n].T
                    return 0
                lax.fori_loop(0, nq, flip, 0)
                pltpu.sync_copy(stage, dst.at[:, lanes])

    blk = pl.BlockSpec((tb, HEAD_PAIR), lambda h, i: (i, h))
    grad = jax.ShapeDtypeStruct((s, D_ATTN), F32)
    return _call(
        body, name="attn_bwd",
        out_shape=[jax.ShapeDtypeStruct((s, D_ATTN), BF16), grad, grad],
        grid=(4, nq),
        in_specs=[blk, blk, pl.BlockSpec((None, tb, 2), lambda h, i: (h, i, 0)), ANY, ANY],
        out_specs=[blk, ANY, ANY],
        scratch_shapes=[pltpu.VMEM((s, HEAD_PAIR), BF16), pltpu.VMEM((s, HEAD_PAIR), BF16),
                        pltpu.VMEM((nq, HEAD_PAIR, tb), F32), pltpu.VMEM((nq, HEAD_PAIR, tb), F32),
                        pltpu.VMEM((s, HEAD_PAIR), F32)]
        + [pltpu.VMEM((tb, tb), F32)] * 32
        + [pltpu.VMEM((2 * tb, tb), BF16)] * 4 + [pltpu.VMEM((tb, 2 * tb), BF16)] * 2,
        compiler_params=_params(("arbitrary", "arbitrary"), vmem_mb=60))(q, do, totals, k, v)


def _pool_bwd(u, dmp, w_pool, pool_scale):
    s = u.shape[0]
    ts = 512
    nr = s // ts
    per = ts // POOL_HALO

    def body(u_ref, halo_ref, dm_ref, wp_ref, ps_ref, du_ref, dwp_ref, dps_ref, ext_ref, y_ref, dext_ref):
        i = pl.program_id(0)
        rb = nr - 1 - i

        @pl.when(i == 0)
        def _():
            dext_ref[ts:, :] = jnp.zeros((POOL_HALO, D_POOL), F32)
            dwp_ref[...] = jnp.zeros_like(dwp_ref)
            dps_ref[...] = jnp.zeros_like(dps_ref)

        ext_ref[0:POOL_HALO, :] = jnp.where(rb > 0, halo_ref[...], 0.0)
        ext_ref[POOL_HALO:, :] = u_ref[...]
        ps, cnts = [], []
        for g, window in enumerate(POOL_WINDOWS):
            p, cnt = _pool_means(ext_ref, g, window, ts, rb * ts)
            ps.append(p.astype(BF16))
            cnts.append(cnt)
            y_ref[:, g * POOL_GROUP:(g + 1) * POOL_GROUP] = _dot(ps[g], wp_ref[g].astype(BF16))
        y = y_ref[...]
        r = _rstd(y)
        yn = y * r
        dm = dm_ref[...]
        dps_ref[...] += jnp.sum(dm * yn, axis=0, keepdims=True)
        dn = dm * ps_ref[...]
        dy = r * (dn - yn * jnp.mean(dn * yn, axis=-1, keepdims=True))
        for g, window in enumerate(POOL_WINDOWS):
            cols = slice(g * POOL_GROUP, (g + 1) * POOL_GROUP)
            dyg = dy[:, cols].astype(BF16)
            dwp_ref[g] += _dot_tn(ps[g], dyg)
            dp = _dot_nt(dyg, wp_ref[g].astype(BF16))
            dext_ref[0:ts, cols] = dp / cnts[g]
            acc = dext_ref[0:ts, cols]
            for d in range(1, window):
                acc = acc + dext_ref[d:d + ts, cols]
            du_ref[:, cols] = (acc - dp).astype(BF16)
        dext_ref[ts:, :] = dext_ref[0:POOL_HALO, :]

    rows = pl.BlockSpec((ts, D_POOL), lambda i: (nr - 1 - i, 0))
    return _call(
        body, name="pool_bwd",
        out_shape=[jax.ShapeDtypeStruct((s, D_POOL), BF16), jax.ShapeDtypeStruct(w_pool.shape, F32),
                   jax.ShapeDtypeStruct((1, D_POOL), F32)],
        grid=(nr,),
        in_specs=[rows,
                  pl.BlockSpec((POOL_HALO, D_POOL), lambda i: (jnp.maximum((nr - 1 - i) * per - 1, 0), 0)),
                  rows, _full(w_pool.shape), _full((1, D_POOL))],
        out_specs=[rows, _full(w_pool.shape), _full((1, D_POOL))],
        scratch_shapes=[pltpu.VMEM((ts + POOL_HALO, D_POOL), F32), pltpu.VMEM((ts, D_POOL), F32),
                        pltpu.VMEM((ts + POOL_HALO, D_POOL), F32)],
        compiler_params=_params(("arbitrary",)))(u, u, dmp, w_pool, pool_scale)


def _in_proj_bwd(du, dq, dk, dv, w_in, x, dx1, g1):
    s = x.shape[0]
    ts = 512

    def body(du_ref, dq_ref, dk_ref, dv_ref, w_ref, x_ref, dx1_ref, g_ref, gx_ref, dg_ref):
        @pl.when(pl.program_id(0) == 0)
        def _():
            dg_ref[...] = jnp.zeros_like(dg_ref)

        dh = _dot_nt(du_ref[...], w_ref[0]) + _dot_nt(dq_ref[...], w_ref[1])
        dh = dh + _dot_nt(dk_ref[...].astype(BF16), w_ref[2]) + _dot_nt(dv_ref[...].astype(BF16), w_ref[3])
        xv = x_ref[...]
        r = _rstd(xv)
        xn = xv * r
        dg_ref[...] += jnp.sum(dh * xn, axis=0, keepdims=True)
        dxn = dh * g_ref[...]
        gx_ref[...] = dx1_ref[...] + r * (dxn - xn * jnp.mean(dxn * xn, axis=-1, keepdims=True))

    row = lambda w: pl.BlockSpec((ts, w), lambda i: (i, 0))
    return _call(
        body, name="in_proj_bwd",
        out_shape=[jax.ShapeDtypeStruct((s, D_MODEL), F32), jax.ShapeDtypeStruct((1, D_MODEL), F32)],
        grid=(s // ts,),
        in_specs=[row(D_POOL)] * 4 + [_full(w_in.shape), row(D_MODEL), row(D_MODEL), _full((1, D_MODEL))],
        out_specs=[row(D_MODEL), _full((1, D_MODEL))],
        compiler_params=_params(("arbitrary",)))(du, dq, dk, dv, w_in, x, dx1, g1)


_SMALL = ("norm_mix_pre", "w_pool", "pool_scale", "attn_scale", "norm_mix_post",
          "norm_ffn_pre", "conv_b", "norm_ffn_post")
_SMALL_SIZE = {"norm_mix_pre": 1024, "w_pool": 65536, "pool_scale": 512, "attn_scale": 512,
               "norm_mix_post": 1024, "norm_ffn_pre": 1024, "conv_b": 5632, "norm_ffn_post": 1024}
_SMALL_ROWS = 600
_CONVW_ROWS = 132
_PACK_ROWS = _SMALL_ROWS + _CONVW_ROWS + 4


def _pack_small(parts):
    flat = jnp.concatenate([parts[n].reshape(-1) for n in _SMALL])
    flat = jnp.pad(flat, (0, _SMALL_ROWS * 128 - flat.shape[0]))
    return flat.reshape(_SMALL_ROWS, 128)


def _unpack_small(packed, like):
    flat = packed.reshape(-1)
    out, off = {}, 0
    for n in _SMALL:
        out[n] = flat[off:off + _SMALL_SIZE[n]].reshape(like[n].shape)
        off += _SMALL_SIZE[n]
    return out


def kernel(x, norm_mix_pre, w_in, w_pool, pool_scale, attn_scale, w_out, norm_mix_post, norm_ffn_pre, w_up, conv_w, conv_b, w_down, norm_ffn_post, loss_target, m_norm_mix_pre, m_w_in, m_w_pool, m_pool_scale, m_attn_scale, m_w_out, m_norm_mix_post, m_norm_ffn_pre, m_w_up, m_conv_w, m_conv_b, m_w_down, m_norm_ffn_post, v_norm_mix_pre, v_w_in, v_w_pool, v_pool_scale, v_attn_scale, v_w_out, v_norm_mix_post, v_norm_ffn_pre, v_w_up, v_conv_w, v_conv_b, v_w_down, v_norm_ffn_post):
    weights = dict(norm_mix_pre=norm_mix_pre, w_in=w_in, w_pool=w_pool, pool_scale=pool_scale,
                   attn_scale=attn_scale, w_out=w_out, norm_mix_post=norm_mix_post,
                   norm_ffn_pre=norm_ffn_pre, w_up=w_up, conv_w=conv_w, conv_b=conv_b,
                   w_down=w_down, norm_ffn_post=norm_ffn_post)
    mom1 = dict(norm_mix_pre=m_norm_mix_pre, w_in=m_w_in, w_pool=m_w_pool, pool_scale=m_pool_scale,
                attn_scale=m_attn_scale, w_out=m_w_out, norm_mix_post=m_norm_mix_post,
                norm_ffn_pre=m_norm_ffn_pre, w_up=m_w_up, conv_w=m_conv_w, conv_b=m_conv_b,
                w_down=m_w_down, norm_ffn_post=m_norm_ffn_post)
    mom2 = dict(norm_mix_pre=v_norm_mix_pre, w_in=v_w_in, w_pool=v_w_pool, pool_scale=v_pool_scale,
                attn_scale=v_attn_scale, w_out=v_w_out, norm_mix_post=v_norm_mix_post,
                norm_ffn_pre=v_norm_ffn_pre, w_up=v_w_up, conv_w=v_conv_w, conv_b=v_conv_b,
                w_down=v_w_down, norm_ffn_post=v_norm_ffn_post)
    order = list(weights)

    xs = x[0]
    target = loss_target[0]
    wp = w_pool[0]
    shard = lax.axis_index("x") * 2 + lax.axis_index("y")

    slot = shard.astype(jnp.int32).reshape(1)
    win_g, = _gather_shards([_cast_bf16(w_in[0], "cast_w_in")])
    lands = [_into_slot(w_out[0], slot, BF16, "cast_w_out"), _into_slot(w_up[0], slot, BF16, "cast_w_up"),
             _into_slot(w_down[0], slot, BF16, "cast_w_down"), _into_slot(conv_w[0], slot, F32, "place_conv_w")]
    g_send, g_recv, g_lands, g_token = _exchange_start(None, lands, win_g, "gather_start")
    convb_g = conv_b[0].reshape(N_SHARD, 1, FF_TILE)

    u, q, k, v, vt, h1 = _in_proj(xs, norm_mix_pre + g_token[0:1, 0:1], win_g)
    mpool = _pool_fwd(u, wp, pool_scale)
    attn, totals = _attn_fwd(q, k, vt)
    _, (wout_g, wup_g, wdown_g, convw_g) = _exchange_wait(g_send, g_recv, g_lands, False, attn, "gather_wait")
    wout_f = wout_g.reshape(D_MODEL, D_MODEL)
    wdown_f = wdown_g.reshape(D_FF, D_MODEL)
    mattn, mix, x1, h2 = _mix_out(attn, mpool, xs, attn_scale, wout_f, norm_mix_post, norm_ffn_pre)
    upre_g, upre_v, f_in = _ffn_up(h2, wup_g, convw_g, convb_g)
    df, dy, loss_tile, d_post = _ffn_down(f_in, wdown_f, x1, target, norm_ffn_post)

    d_wdown = _tn_matmul(f_in, df[None], "dw_down")
    dug, duv, dcw_g, dcw_v, dcb_g, dcb_v = _ffn_bwd_act(df, wdown_f, upre_g, upre_v, convw_g, convb_g)
    d_wup = jnp.concatenate([_tn_matmul(h2[None], dug, "dw_up_gate")[0],
                             _tn_matmul(h2[None], duv, "dw_up_value")[0]], axis=0)
    early = [d_wup, d_wdown.reshape(N_SHARD, D_FF // N_SHARD, D_MODEL)]
    s_send, s_recv, s_thru, s_token = _exchange_start(
        early, [lax.empty((3,) + g.shape[1:], g.dtype) for g in early], d_wup, "scatter_start")
    dx1, dmix, d_ffn_pre, d_mix_post = _ffn_bwd_in(
        dug, duv, wup_g, x1, dy, mix, norm_ffn_pre + s_token[0:1, 0:1], norm_mix_post)
    d_wout = jnp.concatenate([_tn_matmul(mpool[None], dmix[None], "dw_out_pool")[0, 0],
                              _tn_matmul(mattn[None], dmix[None], "dw_out_attn")[0, 0]], axis=0)
    dmp, do, d_attn_scale = _mix_bwd(dmix, wout_f, attn, attn_scale)
    dq, dk, dv = _attn_bwd(q, k, v, do, totals)
    du, d_wpool, d_pool_scale = _pool_bwd(u, dmp, wp, pool_scale)
    d_win = jnp.stack([_tn_matmul(h1[None], t[None], "dw_in_%d" % n)[0, 0]
                       for n, t in enumerate((du, dq, dk, dv))])
    grad_x, d_mix_pre = _in_proj_bwd(du, dq, dk, dv, win_g, xs, dx1, norm_mix_pre)

    d_convw = jnp.concatenate([dcw_g, dcw_v], axis=0)
    d_convb = jnp.concatenate([dcb_g, dcb_v], axis=0).reshape(1, 2 * D_FF)
    small_parts = dict(norm_mix_pre=d_mix_pre, w_pool=d_wpool, pool_scale=d_pool_scale,
                       attn_scale=d_attn_scale, norm_mix_post=d_mix_post, norm_ffn_pre=d_ffn_pre,
                       conv_b=d_convb, norm_ffn_post=d_post)
    packed = jnp.concatenate([_pack_small(small_parts), d_convw.reshape(_CONVW_ROWS, 128),
                              loss_tile[0:4]], axis=0)
    late, gathered = _scatter_grads([d_win, d_wout.reshape(N_SHARD, D_MODEL // N_SHARD, D_MODEL)], packed)
    early_srcs, early_lands = _exchange_wait(s_send, s_recv, s_thru, True, grad_x, "scatter_wait")
    quarter = [_sum_slots(r, (3, 0, 1, 2), "sum_chips_%d" % n) for n, r in enumerate(late)]
    quarter += [_sum_own_and_received(early_srcs[n], slot, early_lands[n], "sum_chips_%d" % (n + 2))
                for n in range(2)]
    sibling = _swap_with_sibling(quarter)
    small_sum = _sum_slots(gathered, tuple(range(8)), "sum_small")

    results = {}
    for n, name in enumerate(("w_in", "w_out", "w_up", "w_down")):
        res = _adamw([quarter[n], sibling[n]], weights[name][0], mom1[name][0], mom2[name][0],
                     "adamw_" + name)
        results[name] = [t[None] for t in res]
    g_convw = lax.dynamic_slice_in_dim(
        small_sum[_SMALL_ROWS:_SMALL_ROWS + _CONVW_ROWS].reshape(N_SHARD, 3, FF_TILE), shard, 1, axis=0)[0]
    convw_pad = lambda t: jnp.pad(t, ((0, 5), (0, 0)))
    res = _adamw([convw_pad(g_convw)], convw_pad(conv_w[0]), convw_pad(m_conv_w[0]),
                 convw_pad(v_conv_w[0]), "adamw_conv_w")
    results["conv_w"] = [t[:3][None] for t in res]
    pack_w = _pack_small(weights)
    pack_m = _pack_small(mom1)
    pack_v = _pack_small(mom2)
    res = _adamw([small_sum[:_SMALL_ROWS]], pack_w, pack_m, pack_v, "adamw_small")
    unpacked = [_unpack_small(t, weights) for t in res]
    for name in _SMALL:
        results[name] = [t[name] for t in unpacked]

    loss = small_sum[_SMALL_ROWS + _CONVW_ROWS, 0]
    outs = [loss, grad_x[None]]
    for slot in range(4):
        outs.extend(results[name][slot] for name in order)
    return tuple(outs)
```

```python
import functools

import jax
import jax.numpy as jnp
from jax import lax
from jax.experimental import pallas as pl
from jax.experimental.pallas import tpu as pltpu

F32 = jnp.float32
BF16 = jnp.bfloat16

D_MODEL = 1024
D_POOL = 512
D_ATTN = 512
POOL_WINDOWS = (2, 4, 8, 16)
POOL_GROUP = 128
POOL_HALO = 16
CONV_HALO = 8
D_FF = 2816
FF_TILE = 1408
N_SHARD = 4
EPS = 1e-6
Q_SCALE = 0.125
ATT_BLOCK = 256
HEAD_PAIR = 128
MIB = 1 << 20
NEG_BIG = -1e30

ADAM_LR = 0.001
ADAM_B1 = 0.9
ADAM_B2 = 0.999
ADAM_EPS = 1e-08
ADAM_WD = 0.01
ADAM_STEP = 10

NT_DIMS = (((1,), (1,)), ((), ()))
TN_DIMS = (((0,), (0,)), ((), ()))
MESH = pl.DeviceIdType.MESH
ANY = pl.BlockSpec(memory_space=pl.ANY)
HBM_SPEC = pl.BlockSpec(memory_space=pltpu.HBM)
SEM_SPEC = pl.BlockSpec(memory_space=pltpu.SEMAPHORE)
DATAFLOW = pltpu.SideEffectType.DATAFLOW_SIDE_EFFECTING


def _call(body, **kw):
    return pl.pallas_call(body, **kw)


def _params(sem=None, vmem_mb=48):
    return pltpu.CompilerParams(dimension_semantics=sem, vmem_limit_bytes=vmem_mb * MIB)


def _rstd(v):
    return lax.rsqrt(jnp.mean(v * v, axis=-1, keepdims=True) + EPS)


def _dot(a, b):
    return jnp.dot(a, b, preferred_element_type=F32)


def _dot_nt(a, b):
    return lax.dot_general(a, b, NT_DIMS, preferred_element_type=F32)


def _dot_tn(a, b):
    return lax.dot_general(a, b, TN_DIMS, preferred_element_type=F32)


def _row_tile(rows, cap):
    t = min(rows, cap)
    t -= t % 8
    while rows % t:
        t -= 8
    return t


def _full(shape):
    nd = len(shape)
    return pl.BlockSpec(shape, lambda *_: (0,) * nd)


def _chip_peers():
    x, y, c = lax.axis_index("x"), lax.axis_index("y"), lax.axis_index("c")
    return x, y, c, [(1 - x, y), (x, 1 - y), (1 - x, 1 - y)]


def _cast_bf16(a, name):
    def body(a_ref, o_ref):
        o_ref[...] = a_ref[...].astype(BF16)

    return _call(body, name=name, out_shape=jax.ShapeDtypeStruct(a.shape, BF16),
                 grid=(1,), in_specs=[_full(a.shape)], out_specs=_full(a.shape),
                 compiler_params=_params(("arbitrary",)))(a)


def _into_slot(a, slot, dtype, name):
    nd = a.ndim

    def body(slot_ref, a_ref, o_ref):
        o_ref[...] = a_ref[...].astype(dtype)

    return _call(
        body, name=name, out_shape=jax.ShapeDtypeStruct((N_SHARD,) + a.shape, dtype),
        grid_spec=pltpu.PrefetchScalarGridSpec(
            num_scalar_prefetch=1, grid=(1,),
            in_specs=[pl.BlockSpec(a.shape, lambda i, slot_ref: (0,) * nd)],
            out_specs=pl.BlockSpec((None,) + a.shape, lambda i, slot_ref: (slot_ref[0],) + (0,) * nd)),
        compiler_params=_params(("arbitrary",)))(slot, a)


def _exchange_copies(srcs, lands, send, recv):
    x, y, c, chips = _chip_peers()
    copies = []
    for t in range(len(lands)):
        for k, (px, py) in enumerate(chips):
            copies.append(pltpu.make_async_remote_copy(
                src_ref=lands[t].at[2 * x + y] if srcs is None else srcs[t].at[2 * px + py],
                dst_ref=lands[t].at[2 * x + y] if srcs is None else lands[t].at[k],
                send_sem=send.at[3 * t + k], recv_sem=recv.at[3 * t + k],
                device_id=(px, py, c), device_id_type=MESH))
    return copies


def _exchange_start(srcs, lands, after, name):
    n = len(lands)
    operands = list(lands) if srcs is None else list(srcs) + list(lands)
    m = len(operands)

    def body(*refs):
        for cp in _exchange_copies(None if srcs is None else refs[:n], refs[m - n:m], refs[m + 1], refs[m + 2]):
            cp.start()
        refs[-1][...] = jnp.zeros_like(refs[-1])

    res = _call(
        body, name=name,
        out_shape=[pltpu.SemaphoreType.DMA((3 * n,)), pltpu.SemaphoreType.DMA((3 * n,))]
        + [pltpu.HBM(a.shape, a.dtype) for a in operands] + [jax.ShapeDtypeStruct((8, 128), F32)],
        in_specs=[HBM_SPEC] * m + [ANY],
        out_specs=[SEM_SPEC, SEM_SPEC] + [HBM_SPEC] * m + [pl.BlockSpec(memory_space=pltpu.VMEM)],
        input_output_aliases={j: j + 2 for j in range(m)},
        compiler_params=pltpu.CompilerParams(has_side_effects=DATAFLOW),
    )(*[pltpu.with_memory_space_constraint(a, pltpu.HBM) for a in operands], after)
    return res[0], res[1], res[2:2 + m], res[-1]


def _exchange_wait(send, recv, operands, scatter, after, name):
    m = len(operands)
    n = m // 2 if scatter else m

    def body(*refs):
        for cp in _exchange_copies(refs[:n] if scatter else None, refs[m - n:m], refs[m], refs[m + 1]):
            cp.wait_send()
            cp.wait_recv()

    res = _call(
        body, name=name, out_shape=[pltpu.HBM(a.shape, a.dtype) for a in operands],
        in_specs=[HBM_SPEC] * m + [SEM_SPEC, SEM_SPEC, ANY], out_specs=[HBM_SPEC] * m,
        input_output_aliases={j: j for j in range(m)},
        compiler_params=pltpu.CompilerParams(has_side_effects=DATAFLOW),
    )(*operands, send, recv, after)
    return res[:m - n], res[m - n:]


def _gather_shards(shards):
    n = len(shards)

    def body(*refs):
        ins, outs = refs[:n], refs[n:2 * n]
        send, recv, loc = refs[2 * n:]
        x, y, c, chips = _chip_peers()
        b = 2 * x + y
        local = [pltpu.make_async_copy(ins[t], outs[t].at[b], loc.at[t]) for t in range(n)]
        for cp in local:
            cp.start()
        remote = []
        for t in range(n):
            for k, (px, py) in enumerate(chips):
                remote.append(pltpu.make_async_remote_copy(
                    src_ref=ins[t], dst_ref=outs[t].at[b],
                    send_sem=send.at[3 * t + k], recv_sem=recv.at[3 * t + k],
                    device_id=(px, py, c), device_id_type=MESH))
        for cp in remote:
            cp.start()
        for cp in remote:
            cp.wait()
        for cp in local:
            cp.wait()

    return _call(
        body, name="gather_w_in",
        out_shape=[jax.ShapeDtypeStruct((N_SHARD,) + s.shape, s.dtype) for s in shards],
        in_specs=[ANY] * n, out_specs=[ANY] * n,
        scratch_shapes=[pltpu.SemaphoreType.DMA((3 * n,)), pltpu.SemaphoreType.DMA((3 * n,)),
                        pltpu.SemaphoreType.DMA((n,))],
    )(*shards)


def _scatter_grads(grads, small):
    n = len(grads)

    def body(*refs):
        ins, small_in = refs[:n], refs[n]
        outs, small_out = refs[n + 1:2 * n + 1], refs[2 * n + 1]
        send, recv, loc, ssend, srecv = refs[2 * n + 2:]
        x, y, c, chips = _chip_peers()
        b = 2 * x + y
        me = 4 * x + 2 * y + c
        local = [pltpu.make_async_copy(ins[t].at[b], outs[t].at[3], loc.at[t]) for t in range(n)]
        local.append(pltpu.make_async_copy(small_in, small_out.at[me], loc.at[n]))
        for cp in local:
            cp.start()
        remote = []
        for t in range(n):
            for k, (px, py) in enumerate(chips):
                remote.append(pltpu.make_async_remote_copy(
                    src_ref=ins[t].at[2 * px + py], dst_ref=outs[t].at[k],
                    send_sem=send.at[3 * t + k], recv_sem=recv.at[3 * t + k],
                    device_id=(px, py, c), device_id_type=MESH))
        for r in range(1, 8):
            px = 1 - x if r & 4 else x
            py = 1 - y if r & 2 else y
            pc = 1 - c if r & 1 else c
            remote.append(pltpu.make_async_remote_copy(
                src_ref=small_in, dst_ref=small_out.at[me],
                send_sem=ssend.at[r - 1], recv_sem=srecv.at[r - 1],
                device_id=(px, py, pc), device_id_type=MESH))
        for cp in remote:
            cp.start()
        for cp in remote:
            cp.wait()
        for cp in local:
            cp.wait()

    out_shape = [jax.ShapeDtypeStruct(g.shape, g.dtype) for g in grads]
    out_shape.append(jax.ShapeDtypeStruct((8,) + small.shape, small.dtype))
    res = _call(
        body, name="scatter_grads", out_shape=out_shape,
        in_specs=[ANY] * (n + 1), out_specs=[ANY] * (n + 1),
        scratch_shapes=[pltpu.SemaphoreType.DMA((3 * n,)), pltpu.SemaphoreType.DMA((3 * n,)),
                        pltpu.SemaphoreType.DMA((n + 1,)),
                        pltpu.SemaphoreType.DMA((7,)), pltpu.SemaphoreType.DMA((7,))],
    )(*grads, small)
    return res[:n], res[n]


def _swap_with_sibling(parts):
    n = len(parts)

    def body(*refs):
        ins, outs = refs[:n], refs[n:2 * n]
        send, recv = refs[2 * n:]
        x, y, c = lax.axis_index("x"), lax.axis_index("y"), lax.axis_index("c")
        copies = [pltpu.make_async_remote_copy(
            src_ref=ins[t], dst_ref=outs[t], send_sem=send.at[t], recv_sem=recv.at[t],
            device_id=(x, y, 1 - c), device_id_type=MESH) for t in range(n)]
        for cp in copies:
            cp.start()
        for cp in copies:
            cp.wait()

    return _call(
        body, name="swap_sibling",
        out_shape=[jax.ShapeDtypeStruct(p.shape, p.dtype) for p in parts],
        in_specs=[ANY] * n, out_specs=[ANY] * n,
        scratch_shapes=[pltpu.SemaphoreType.DMA((n,)), pltpu.SemaphoreType.DMA((n,))],
    )(*parts)


def _sum_slots(buf, order, name):
    k, rows, cols = buf.shape
    tr = _row_tile(rows, 256)

    def body(b_ref, o_ref):
        acc = b_ref[order[0]].astype(F32)
        for s in order[1:]:
            acc = acc + b_ref[s].astype(F32)
        o_ref[...] = acc

    return _call(body, name=name, out_shape=jax.ShapeDtypeStruct((rows, cols), F32),
                 grid=(rows // tr,),
                 in_specs=[pl.BlockSpec((k, tr, cols), lambda i: (0, i, 0))],
                 out_specs=pl.BlockSpec((tr, cols), lambda i: (i, 0)),
                 compiler_params=_params(("parallel",)))(buf)


def _sum_own_and_received(src, slot, land, name):
    _, rows, cols = src.shape
    tr = _row_tile(rows, 256)

    def body(slot_ref, s_ref, l_ref, o_ref):
        acc = s_ref[...].astype(F32)
        for k in range(3):
            acc = acc + l_ref[k].astype(F32)
        o_ref[...] = acc

    return _call(
        body, name=name, out_shape=jax.ShapeDtypeStruct((rows, cols), F32),
        grid_spec=pltpu.PrefetchScalarGridSpec(
            num_scalar_prefetch=1, grid=(rows // tr,),
            in_specs=[pl.BlockSpec((None, tr, cols), lambda i, slot_ref: (slot_ref[0], i, 0)),
                      pl.BlockSpec((3, tr, cols), lambda i, slot_ref: (0, i, 0))],
            out_specs=pl.BlockSpec((tr, cols), lambda i, slot_ref: (i, 0))),
        compiler_params=_params(("parallel",)))(slot, src, land)


def _adamw(grad_parts, w, m, v, name):
    rows, cols = w.shape
    tr = _row_tile(rows, 256)
    npart = len(grad_parts)

    def body(*refs):
        gp = refs[:npart]
        w_ref, m_ref, v_ref, g_out, d_out, m_out, v_out = refs[npart:]
        g = gp[0][...]
        for p in gp[1:]:
            g = g + p[...]
        mm = ADAM_B1 * m_ref[...] + (1.0 - ADAM_B1) * g
        vv = ADAM_B2 * v_ref[...] + (1.0 - ADAM_B2) * jnp.square(g)
        m_hat = mm / (1.0 - ADAM_B1 ** ADAM_STEP)
        v_hat = vv / (1.0 - ADAM_B2 ** ADAM_STEP)
        g_out[...] = g
        d_out[...] = -ADAM_LR * (m_hat / (jnp.sqrt(v_hat) + ADAM_EPS) + ADAM_WD * w_ref[...])
        m_out[...] = mm
        v_out[...] = vv

    spec = pl.BlockSpec((tr, cols), lambda i: (i, 0))
    shp = jax.ShapeDtypeStruct((rows, cols), F32)
    return _call(body, name=name, out_shape=[shp] * 4, grid=(rows // tr,),
                 in_specs=[spec] * (npart + 3), out_specs=[spec] * 4,
                 compiler_params=_params(("parallel",)))(*grad_parts, w, m, v)


def _in_proj(x, g1, w_in):
    s = x.shape[0]
    ts = 512

    def body(x_ref, g_ref, w_ref, u_ref, q_ref, k_ref, v_ref, vt_ref, h_ref):
        xv = x_ref[...]
        h = (xv * _rstd(xv) * g_ref[...]).astype(BF16)
        h_ref[...] = h
        u_ref[...] = _dot(h, w_ref[0])
        q_ref[...] = (_dot(h, w_ref[1]) * Q_SCALE).astype(BF16)
        k_ref[...] = _dot(h, w_ref[2]).astype(BF16)
        v = _dot(h, w_ref[3])
        v_ref[...] = v.astype(BF16)
        vt = v.T.astype(BF16)
        for n in range(ts // ATT_BLOCK):
            vt_ref[n] = vt[:, n * ATT_BLOCK:(n + 1) * ATT_BLOCK]

    row = lambda w: pl.BlockSpec((ts, w), lambda i: (i, 0))
    half = jax.ShapeDtypeStruct((s, D_POOL), BF16)
    return _call(
        body, name="in_proj",
        out_shape=[jax.ShapeDtypeStruct((s, D_POOL), F32), half, half, half,
                   jax.ShapeDtypeStruct((s // ATT_BLOCK, D_ATTN, ATT_BLOCK), BF16),
                   jax.ShapeDtypeStruct((s, D_MODEL), BF16)],
        grid=(s // ts,),
        in_specs=[row(D_MODEL), _full((1, D_MODEL)), _full(w_in.shape)],
        out_specs=[row(D_POOL)] * 4
        + [pl.BlockSpec((ts // ATT_BLOCK, D_ATTN, ATT_BLOCK), lambda i: (i, 0, 0)), row(D_MODEL)],
        compiler_params=_params(("parallel",)))(x, g1, w_in)


def _pool_means(ext_ref, g, window, ts, row0):
    cols = slice(g * POOL_GROUP, (g + 1) * POOL_GROUP)
    cur = ext_ref[POOL_HALO:POOL_HALO + ts, cols]
    acc = cur
    for d in range(1, window):
        acc = acc + ext_ref[POOL_HALO - d:POOL_HALO - d + ts, cols]
    t1 = row0 + 1 + lax.broadcasted_iota(jnp.int32, (ts, 1), 0)
    cnt = jnp.minimum(t1, window).astype(F32)
    return acc / cnt - cur, cnt


def _pool_fwd(u, w_pool, pool_scale):
    s = u.shape[0]
    ts = 512
    per = ts // POOL_HALO

    def body(u_ref, halo_ref, wp_ref, ps_ref, o_ref, ext_ref, y_ref):
        i = pl.program_id(0)
        ext_ref[0:POOL_HALO, :] = jnp.where(i > 0, halo_ref[...], 0.0)
        ext_ref[POOL_HALO:, :] = u_ref[...]
        for g, window in enumerate(POOL_WINDOWS):
            p, _ = _pool_means(ext_ref, g, window, ts, i * ts)
            y_ref[:, g * POOL_GROUP:(g + 1) * POOL_GROUP] = _dot(
                p.astype(BF16), wp_ref[g].astype(BF16))
        y = y_ref[...]
        o_ref[...] = (y * _rstd(y) * ps_ref[...]).astype(BF16)

    return _call(
        body, name="pool_fwd", out_shape=jax.ShapeDtypeStruct((s, D_POOL), BF16),
        grid=(s // ts,),
        in_specs=[pl.BlockSpec((ts, D_POOL), lambda i: (i, 0)),
                  pl.BlockSpec((POOL_HALO, D_POOL), lambda i: (jnp.maximum(i * per - 1, 0), 0)),
                  _full(w_pool.shape), _full((1, D_POOL))],
        out_specs=pl.BlockSpec((ts, D_POOL), lambda i: (i, 0)),
        scratch_shapes=[pltpu.VMEM((ts + POOL_HALO, D_POOL), F32), pltpu.VMEM((ts, D_POOL), F32)],
        compiler_params=_params(("parallel",)))(u, u, w_pool, pool_scale)


def _tri(kind):
    r = lax.broadcasted_iota(jnp.int32, (ATT_BLOCK, ATT_BLOCK), 0)
    c = lax.broadcasted_iota(jnp.int32, (ATT_BLOCK, ATT_BLOCK), 1)
    return jnp.where(r >= c if kind == "suffix" else r <= c, 1.0, 0.0).astype(BF16)


def _causal_mask():
    r = lax.broadcasted_iota(jnp.int32, (ATT_BLOCK, ATT_BLOCK), 0)
    c = lax.broadcasted_iota(jnp.int32, (ATT_BLOCK, ATT_BLOCK), 1)
    return c < r


def _softplus(z, with_sigmoid=False):
    ope = 1.0 + jnp.exp(jnp.minimum(z, 80.0))
    sp = jnp.maximum(z, jnp.log(ope))
    if with_sigmoid:
        return sp, 1.0 - 1.0 / ope
    return sp


def _attn_fwd(q, k, vt):
    s = q.shape[0]
    tb = ATT_BLOCK
    nq = s // tb

    def body(q_ref, k_ref, vt_ref, o_ref, t_ref, *bufs):
        i = pl.program_id(1)
        suffix = _tri("prefix")
        r_idx = lax.broadcasted_iota(jnp.int32, (tb, tb), 0)
        c_idx = lax.broadcasted_iota(jnp.int32, (tb, tb), 1)
        causal = r_idx < c_idx
        lane = lax.broadcasted_iota(jnp.int32, (1, HEAD_PAIR), 1)
        first = lane < 64
        top = lax.broadcasted_iota(jnp.int32, (HEAD_PAIR, 1), 0) < 64
        q2 = q_ref[...]
        zero = jnp.zeros_like(q2)
        qs = (jnp.where(first, q2, zero), jnp.where(first, zero, q2))

        def values_t(j):
            vt = vt_ref[j]
            none = jnp.zeros_like(vt)
            return jnp.concatenate([jnp.where(top, vt, none), jnp.where(top, none, vt)], axis=1)

        def scores(j, cs, masked):
            kj = k_ref[pl.ds(pl.multiple_of(j * tb, tb), tb), :]
            new_cs, args = [], []
            for e in range(2):
                z = _dot_nt(kj, qs[e])
                sp = _softplus(z)
                if masked:
                    sp = jnp.where(causal, sp, 0.0)
                incl = _dot(suffix, sp.astype(BF16))
                arg = z - incl - cs[e]
                if masked:
                    arg = jnp.where(causal, arg, NEG_BIG)
                args.append(arg)
                new_cs.append(cs[e] + incl[0:1, :])
            return new_cs, args

        def weigh(j, args, o):
            probs = [jnp.exp(arg).astype(BF16) for arg in args]
            return o + _dot(values_t(j), jnp.concatenate(probs, axis=0))

        z_buf, zc_buf, in_buf = [[[bufs[4 * kind + 2 * slot + e] for e in range(2)]
                                  for slot in range(2)] for kind in range(3)]
        pr_buf = [bufs[12], bufs[13]]

        @pl.when((pl.program_id(0) == 0) & (i == 0))
        def _():
            for b in bufs:
                b[...] = jnp.zeros_like(b)

        def block_of(p):
            return jnp.clip(i - 1 - p, 0, nq - 1)

        def trip(t, w, carry):
            r = 1 - w
            cs, o = list(carry[0:2]), carry[2]
            live3 = (t - 3 >= 0) & (t - 3 < i)
            o = o + jnp.where(live3, _dot(values_t(block_of(t - 3)), pr_buf[r][...]), 0.0)
            kj = k_ref[pl.ds(pl.multiple_of(block_of(t) * tb, tb), tb), :]
            for e in range(2):
                z_buf[w][e][...] = _dot_nt(kj, qs[e])
            for e in range(2):
                z = z_buf[r][e][...]
                zc_buf[w][e][...] = z
                in_buf[w][e][...] = _dot(suffix, _softplus(z).astype(BF16))
            live2 = (t - 2 >= 0) & (t - 2 < i)
            for e in range(2):
                incl = in_buf[r][e][...]
                arg = zc_buf[r][e][...] - incl - jnp.where(live2, cs[e], -NEG_BIG)
                pr_buf[w][e * tb:(e + 1) * tb, :] = jnp.exp(arg).astype(BF16)
                cs[e] = jnp.where(live2, cs[e] + incl[0:1, :], cs[e])
            return cs[0], cs[1], o

        def four_trips(n, cr):
            for u in range(4):
                cr = trip(4 * n + u, u % 2, cr)
            return cr

        row = jnp.zeros((1, tb), F32)
        cs, args = scores(i, (row, row), True)
        carry = (cs[0], cs[1], weigh(i, args, jnp.zeros((HEAD_PAIR, tb), F32)))
        carry = lax.fori_loop(0, jnp.where(i > 0, (i + 6) // 4, 0), four_trips, carry)
        o_ref[...] = carry[2].T
        totals = jnp.where(r_idx == 0, carry[0], jnp.where(r_idx == 1, carry[1], 0.0))
        t_ref[...] = totals.T[:, 0:2]

    score_buf = pltpu.VMEM((tb, tb), F32)
    return _call(
        body, name="attn_fwd",
        out_shape=[jax.ShapeDtypeStruct((s, D_ATTN), F32),
                   jax.ShapeDtypeStruct((4, s, 2), F32)],
        grid=(4, nq),
        in_specs=[pl.BlockSpec((tb, HEAD_PAIR), lambda h, i: (i, h)),
                  pl.BlockSpec((s, HEAD_PAIR), lambda h, i: (0, h)),
                  pl.BlockSpec((nq, HEAD_PAIR, tb), lambda h, i: (0, h, 0))],
        out_specs=[pl.BlockSpec((tb, HEAD_PAIR), lambda h, i: (i, h)),
                   pl.BlockSpec((None, tb, 2), lambda h, i: (h, i, 0))],
        scratch_shapes=[score_buf] * 12 + [pltpu.VMEM((2 * tb, tb), BF16)] * 2,
        compiler_params=_params(("arbitrary", "arbitrary")))(q, k, vt)


def _mix_out(attn, mpool, x, attn_scale, w_out, g2, g3):
    s = x.shape[0]
    ts = 512

    def body(a_ref, p_ref, x_ref, as_ref, w_ref, g2_ref, g3_ref, ma_ref, mix_ref, x1_ref, h2_ref):
        ao = a_ref[...]
        ma = (ao * _rstd(ao) * as_ref[...]).astype(BF16)
        ma_ref[...] = ma
        mix = _dot(p_ref[...], w_ref[0:D_POOL, :]) + _dot(ma, w_ref[D_POOL:, :])
        mix_ref[...] = mix
        x1 = x_ref[...] + mix * _rstd(mix) * g2_ref[...]
        x1_ref[...] = x1
        h2_ref[...] = (x1 * _rstd(x1) * g3_ref[...]).astype(BF16)

    row = lambda w: pl.BlockSpec((ts, w), lambda i: (i, 0))
    return _call(
        body, name="mix_out",
        out_shape=[jax.ShapeDtypeStruct((s, D_ATTN), BF16), jax.ShapeDtypeStruct((s, D_MODEL), F32),
                   jax.ShapeDtypeStruct((s, D_MODEL), F32), jax.ShapeDtypeStruct((s, D_MODEL), BF16)],
        grid=(s // ts,),
        in_specs=[row(D_ATTN), row(D_POOL), row(D_MODEL), _full((1, D_ATTN)),
                  _full((D_MODEL, D_MODEL)), _full((1, D_MODEL)), _full((1, D_MODEL))],
        out_specs=[row(D_ATTN), row(D_MODEL), row(D_MODEL), row(D_MODEL)],
        compiler_params=_params(("parallel",)))(attn, mpool, x, attn_scale, w_out, g2, g3)


def _earlier_rows(halo, block):
    ts = block.shape[0]
    ext = jnp.concatenate([halo, block], axis=0)
    return tuple(pltpu.roll(ext, d, axis=0)[CONV_HALO:CONV_HALO + ts, :] for d in (1, 2))


def _later_rows(block, halo):
    ts = block.shape[0]
    ext = jnp.concatenate([block, halo], axis=0)
    return tuple(pltpu.roll(ext, ts + CONV_HALO - d, axis=0)[0:ts, :] for d in (1, 2))


def _conv_rows(x0, x1, x2, cw, cb):
    y = cb + cw[0:1, :] * x2
    y = y + cw[1:2, :] * x1
    return y + cw[2:3, :] * x0


def _sigmoid(v):
    return 1.0 / (1.0 + jnp.exp(-v))


def _ffn_up(h2, w_up, conv_w, conv_b):
    s = h2.shape[0]
    ts = 256
    tn = FF_TILE

    def body(h_ref, wg_ref, wv_ref, cwg_ref, cwv_ref, cbg_ref, cbv_ref,
             ug_ref, uv_ref, f_ref, tailg, tailv):
        i = pl.program_id(1)

        @pl.when(i == 0)
        def _():
            tailg[...] = jnp.zeros_like(tailg)
            tailv[...] = jnp.zeros_like(tailv)

        h = h_ref[...]
        ug = _dot(h, wg_ref[...])
        uv = _dot(h, wv_ref[...])
        ug_ref[...] = ug
        uv_ref[...] = uv
        gate = _conv_rows(ug, *_earlier_rows(tailg[...], ug), cwg_ref[...], cbg_ref[...])
        val = _conv_rows(uv, *_earlier_rows(tailv[...], uv), cwv_ref[...], cbv_ref[...])
        f_ref[...] = (gate * _sigmoid(gate) * val).astype(BF16)
        tailg[...] = ug[ts - CONV_HALO:, :]
        tailv[...] = uv[ts - CONV_HALO:, :]

    out_blk = pl.BlockSpec((None, ts, tn), lambda n, i: (n, i, 0))
    act = jax.ShapeDtypeStruct((2, s, tn), F32)
    return _call(
        body, name="ffn_up",
        out_shape=[act, act, jax.ShapeDtypeStruct((2, s, tn), BF16)],
        grid=(2, s // ts),
        in_specs=[pl.BlockSpec((ts, D_MODEL), lambda n, i: (i, 0)),
                  pl.BlockSpec((None, D_MODEL, tn), lambda n, i: (n, 0, 0)),
                  pl.BlockSpec((None, D_MODEL, tn), lambda n, i: (n + 2, 0, 0)),
                  pl.BlockSpec((None, 3, tn), lambda n, i: (n, 0, 0)),
                  pl.BlockSpec((None, 3, tn), lambda n, i: (n + 2, 0, 0)),
                  pl.BlockSpec((None, 1, tn), lambda n, i: (n, 0, 0)),
                  pl.BlockSpec((None, 1, tn), lambda n, i: (n + 2, 0, 0))],
        out_specs=[out_blk, out_blk, out_blk],
        scratch_shapes=[pltpu.VMEM((CONV_HALO, tn), F32), pltpu.VMEM((CONV_HALO, tn), F32)],
        compiler_params=_params(("arbitrary", "arbitrary")))(
            h2, w_up, w_up, conv_w, conv_w, conv_b, conv_b)


def _ffn_down(f_in, w_down, x1, target, g4):
    s = x1.shape[0]
    ts = 512

    def body(f_ref, w_ref, x1_ref, t_ref, g_ref, df_ref, dy_ref, loss_ref, dg_ref):
        @pl.when(pl.program_id(0) == 0)
        def _():
            loss_ref[...] = jnp.zeros_like(loss_ref)
            dg_ref[...] = jnp.zeros_like(dg_ref)

        f = _dot(f_ref[0], w_ref[0:FF_TILE, :]) + _dot(f_ref[1], w_ref[FF_TILE:, :])
        rf = _rstd(f)
        fn = f * rf
        g = g_ref[...]
        err = (x1_ref[...] + fn * g) - t_ref[...]
        loss_ref[...] += 0.5 * jnp.sum(jnp.mean(err * err, axis=-1))
        dy = err * (1.0 / D_MODEL)
        dy_ref[...] = dy
        dg_ref[...] += jnp.sum(dy * fn, axis=0, keepdims=True)
        dfn = dy * g
        df_ref[...] = (rf * (dfn - fn * jnp.mean(dfn * fn, axis=-1, keepdims=True))).astype(BF16)

    row = pl.BlockSpec((ts, D_MODEL), lambda i: (i, 0))
    return _call(
        body, name="ffn_down",
        out_shape=[jax.ShapeDtypeStruct((s, D_MODEL), BF16), jax.ShapeDtypeStruct((s, D_MODEL), F32),
                   jax.ShapeDtypeStruct((8, 128), F32), jax.ShapeDtypeStruct((1, D_MODEL), F32)],
        grid=(s // ts,),
        in_specs=[pl.BlockSpec((2, ts, FF_TILE), lambda i: (0, i, 0)), _full((D_FF, D_MODEL)),
                  row, row, _full((1, D_MODEL))],
        out_specs=[row, row, _full((8, 128)), _full((1, D_MODEL))],
        compiler_params=_params(("arbitrary",)))(f_in, w_down, x1, target, g4)


def _tn_matmul(a, b, name, ts=512):
    na, s, ka = a.shape
    nb, _, nbc = b.shape
    steps = s // ts

    def body(a_ref, b_ref, o_ref, acc_ref):
        @pl.when(pl.program_id(2) == 0)
        def _():
            acc_ref[...] = jnp.zeros_like(acc_ref)

        acc_ref[...] += _dot_tn(a_ref[...].astype(BF16), b_ref[...].astype(BF16))

        @pl.when(pl.program_id(2) == steps - 1)
        def _():
            o_ref[...] = acc_ref[...].astype(BF16)

    return _call(
        body, name=name, out_shape=jax.ShapeDtypeStruct((na, nb, ka, nbc), BF16),
        grid=(na, nb, steps),
        in_specs=[pl.BlockSpec((None, ts, ka), lambda i, j, r: (i, r, 0)),
                  pl.BlockSpec((None, ts, nbc), lambda i, j, r: (j, r, 0))],
        out_specs=pl.BlockSpec((None, None, ka, nbc), lambda i, j, r: (i, j, 0, 0)),
        scratch_shapes=[pltpu.VMEM((ka, nbc), F32)],
        compiler_params=_params(("parallel", "parallel", "arbitrary")))(a, b)


def _ffn_bwd_act(df, w_down, upre_g, upre_v, conv_w, conv_b):
    s = df.shape[0]
    ts = 256
    tn = FF_TILE
    nr = s // ts
    per = ts // CONV_HALO

    def body(df_ref, wd_ref, ug_ref, uv_ref, hg_ref, hv_ref, cwg_ref, cwv_ref, cbg_ref, cbv_ref,
             dug_ref, duv_ref, dcwg_ref, dcwv_ref, dcbg_ref, dcbv_ref, headg, headv):
        i = pl.program_id(1)
        first_rows = i == nr - 1

        @pl.when(i == 0)
        def _():
            for r in (headg, headv, dcwg_ref, dcwv_ref, dcbg_ref, dcbv_ref):
                r[...] = jnp.zeros_like(r)

        ug, uv = ug_ref[...], uv_ref[...]
        cwg, cwv = cwg_ref[...], cwv_ref[...]
        gate = _conv_rows(ug, *_earlier_rows(jnp.where(first_rows, 0.0, hg_ref[...]), ug), cwg, cbg_ref[...])
        val = _conv_rows(uv, *_earlier_rows(jnp.where(first_rows, 0.0, hv_ref[...]), uv), cwv, cbv_ref[...])
        sg = _sigmoid(gate)
        dfin = _dot_nt(df_ref[...], wd_ref[...])
        dval = dfin * (gate * sg)
        dgate = dfin * val * (sg * (1.0 + gate * (1.0 - sg)))

        def conv_bwd(dact, x, head, cw, dcw_ref, dcb_ref, du_ref):
            d1, d2 = _later_rows(dact, head[...])
            dcb_ref[...] += jnp.sum(dact, axis=0, keepdims=True)
            for kk, shifted in enumerate((d2, d1, dact)):
                dcw_ref[kk:kk + 1, :] += jnp.sum(x * shifted, axis=0, keepdims=True)
            du_ref[...] = (cw[2:3, :] * dact + cw[1:2, :] * d1 + cw[0:1, :] * d2).astype(BF16)
            head[...] = dact[0:CONV_HALO, :]

        conv_bwd(dgate, ug, headg, cwg, dcwg_ref, dcbg_ref, dug_ref)
        conv_bwd(dval, uv, headv, cwv, dcwv_ref, dcbv_ref, duv_ref)

    rows = lambda n, i: (n, nr - 1 - i, 0)
    halo = lambda n, i: (n, jnp.maximum((nr - 1 - i) * per - 1, 0), 0)
    act_blk = pl.BlockSpec((None, ts, tn), rows)
    halo_blk = pl.BlockSpec((None, CONV_HALO, tn), halo)
    cw_blk = lambda off: pl.BlockSpec((None, 3, tn), lambda n, i: (n + off, 0, 0))
    cb_blk = lambda off: pl.BlockSpec((None, 1, tn), lambda n, i: (n + off, 0, 0))
    acc_w = pl.BlockSpec((None, 3, tn), lambda n, i: (n, 0, 0))
    acc_b = pl.BlockSpec((None, 1, tn), lambda n, i: (n, 0, 0))
    dact = jax.ShapeDtypeStruct((2, s, tn), BF16)
    return _call(
        body, name="ffn_bwd_act",
        out_shape=[dact, dact, jax.ShapeDtypeStruct((2, 3, tn), F32), jax.ShapeDtypeStruct((2, 3, tn), F32),
                   jax.ShapeDtypeStruct((2, 1, tn), F32), jax.ShapeDtypeStruct((2, 1, tn), F32)],
        grid=(2, nr),
        in_specs=[pl.BlockSpec((ts, D_MODEL), lambda n, i: (nr - 1 - i, 0)),
                  pl.BlockSpec((tn, D_MODEL), lambda n, i: (n, 0)),
                  act_blk, act_blk, halo_blk, halo_blk,
                  cw_blk(0), cw_blk(2), cb_blk(0), cb_blk(2)],
        out_specs=[act_blk, act_blk, acc_w, acc_w, acc_b, acc_b],
        scratch_shapes=[pltpu.VMEM((CONV_HALO, tn), F32)] * 2,
        compiler_params=_params(("arbitrary", "arbitrary")))(
            df, w_down, upre_g, upre_v, upre_g, upre_v, conv_w, conv_w, conv_b, conv_b)


def _ffn_bwd_in(dug, duv, w_up, x1, dy, mix, g3, g2):
    s = x1.shape[0]
    ts = 256

    def body(dg_ref, dv_ref, w_ref, x1_ref, dy_ref, mix_ref, g3_ref, g2_ref,
             dx1_ref, dmix_ref, dg3_ref, dg2_ref):
        @pl.when(pl.program_id(0) == 0)
        def _():
            dg3_ref[...] = jnp.zeros_like(dg3_ref)
            dg2_ref[...] = jnp.zeros_like(dg2_ref)

        dh = _dot_nt(dg_ref[0], w_ref[0]) + _dot_nt(dg_ref[1], w_ref[1])
        dh = dh + _dot_nt(dv_ref[0], w_ref[2]) + _dot_nt(dv_ref[1], w_ref[3])
        x1 = x1_ref[...]
        r3 = _rstd(x1)
        xn = x1 * r3
        dg3_ref[...] += jnp.sum(dh * xn, axis=0, keepdims=True)
        dxn = dh * g3_ref[...]
        dx1 = dy_ref[...] + r3 * (dxn - xn * jnp.mean(dxn * xn, axis=-1, keepdims=True))
        dx1_ref[...] = dx1
        mix = mix_ref[...]
        rm = _rstd(mix)
        mn = mix * rm
        dg2_ref[...] += jnp.sum(dx1 * mn, axis=0, keepdims=True)
        dmn = dx1 * g2_ref[...]
        dmix_ref[...] = (rm * (dmn - mn * jnp.mean(dmn * mn, axis=-1, keepdims=True))).astype(BF16)

    row = pl.BlockSpec((ts, D_MODEL), lambda i: (i, 0))
    act = pl.BlockSpec((2, ts, FF_TILE), lambda i: (0, i, 0))
    vec = _full((1, D_MODEL))
    return _call(
        body, name="ffn_bwd_in",
        out_shape=[jax.ShapeDtypeStruct((s, D_MODEL), F32), jax.ShapeDtypeStruct((s, D_MODEL), BF16),
                   jax.ShapeDtypeStruct((1, D_MODEL), F32), jax.ShapeDtypeStruct((1, D_MODEL), F32)],
        grid=(s // ts,),
        in_specs=[act, act, _full(w_up.shape), row, row, row, vec, vec],
        out_specs=[row, row, vec, vec],
        compiler_params=_params(("arbitrary",), vmem_mb=56))(dug, duv, w_up, x1, dy, mix, g3, g2)


def _mix_bwd(dmix, w_out, attn, attn_scale):
    s = dmix.shape[0]
    ts = 512

    def body(dm_ref, w_ref, a_ref, as_ref, dp_ref, do_ref, das_ref):
        @pl.when(pl.program_id(0) == 0)
        def _():
            das_ref[...] = jnp.zeros_like(das_ref)

        dm = dm_ref[...]
        dp_ref[...] = _dot_nt(dm, w_ref[0:D_POOL, :])
        da = _dot_nt(dm, w_ref[D_POOL:, :])
        ao = a_ref[...]
        ra = _rstd(ao)
        an = ao * ra
        das_ref[...] += jnp.sum(da * an, axis=0, keepdims=True)
        dan = da * as_ref[...]
        do_ref[...] = (ra * (dan - an * jnp.mean(dan * an, axis=-1, keepdims=True))).astype(BF16)

    row = lambda w: pl.BlockSpec((ts, w), lambda i: (i, 0))
    return _call(
        body, name="mix_bwd",
        out_shape=[jax.ShapeDtypeStruct((s, D_POOL), F32), jax.ShapeDtypeStruct((s, D_ATTN), BF16),
                   jax.ShapeDtypeStruct((1, D_ATTN), F32)],
        grid=(s // ts,),
        in_specs=[row(D_MODEL), _full((D_MODEL, D_MODEL)), row(D_ATTN), _full((1, D_ATTN))],
        out_specs=[row(D_POOL), row(D_ATTN), _full((1, D_ATTN))],
        compiler_params=_params(("arbitrary",)))(dmix, w_out, attn, attn_scale)


def _attn_bwd(q, k, v, do, totals):
    s = q.shape[0]
    tb = ATT_BLOCK
    nq = s // tb

    def body(q_ref, do_ref, t_ref, k_hbm, v_hbm, dq_ref, dk_hbm, dv_hbm,
             k_scr, v_scr, dkt_acc, dvt_acc, stage, *bufs):
        hp = pl.program_id(0)
        i = pl.program_id(1)
        lanes = pl.ds(pl.multiple_of(hp * HEAD_PAIR, HEAD_PAIR), HEAD_PAIR)

        @pl.when(i == 0)
        def _():
            pltpu.sync_copy(k_hbm.at[:, lanes], k_scr)
            pltpu.sync_copy(v_hbm.at[:, lanes], v_scr)
            dkt_acc[...] = jnp.zeros_like(dkt_acc)
            dvt_acc[...] = jnp.zeros_like(dvt_acc)

        upper = _tri("suffix")
        lower = _tri("prefix")
        causal = _causal_mask()
        lane = lax.broadcasted_iota(jnp.int32, (1, HEAD_PAIR), 1)
        first = lane < 64
        q2 = q_ref[...]
        do2 = do_ref[...]
        zero = jnp.zeros_like(q2)
        qs = (jnp.where(first, q2, zero), jnp.where(first, zero, q2))
        dos = (jnp.where(first, do2, zero), jnp.where(first, zero, do2))
        qcat_t = jnp.concatenate(qs, axis=0).astype(F32).T.astype(BF16)
        docat_t = jnp.concatenate(dos, axis=0).astype(F32).T.astype(BF16)
        tots = (t_ref[:, 0:1], t_ref[:, 1:2])

        z_ring, sg_ring = [[[bufs[8 * kind + 2 * slot + e] for e in range(2)] for slot in range(4)]
                           for kind in range(2)]
        in_buf, da_buf, dw_buf, pre_buf = [
            [[bufs[16 + 4 * kind + 2 * slot + e] for e in range(2)] for slot in range(2)]
            for kind in range(4)]
        pr_buf, dzr_buf, dzc_buf = bufs[32:34], bufs[34:36], bufs[36:38]

        for e in range(2):
            for slot in (2, 3):
                z_ring[slot][e][...] = jnp.full((tb, tb), NEG_BIG, F32)
            for slot in (1, 2, 3):
                sg_ring[slot][e][...] = jnp.zeros((tb, tb), F32)
            for buf in (in_buf, da_buf, dw_buf, pre_buf):
                buf[1][e][...] = jnp.zeros((tb, tb), F32)
        for buf in (pr_buf, dzr_buf, dzc_buf):
            buf[1][...] = jnp.zeros_like(buf[1])

        def rows(p):
            return pl.ds(pl.multiple_of(jnp.clip(p, 0, nq - 1) * tb, tb), tb)

        def split_heads(block):
            return jnp.concatenate([jnp.where(first, block, zero), jnp.where(first, zero, block)], axis=0)

        def trip(t, u, carry):
            w, r = u % 2, 1 - u % 2
            cs, cps, dq = list(carry[0:2]), list(carry[2:4]), carry[4]
            live4 = (t - 4 >= 0) & (t - 4 < i)
            dq = dq + jnp.where(live4, _dot(dzc_buf[r][...], split_heads(k_scr[rows(t - 4), :])), 0.0)
            dkt_acc[jnp.clip(t - 4, 0, nq - 1)] += jnp.where(live4, _dot(qcat_t, dzr_buf[r][...]), 0.0)
            dvt_acc[jnp.clip(t - 3, 0, nq - 1)] += _dot(docat_t, pr_buf[r][...])
            kj = k_scr[rows(t), :]
            for e in range(2):
                z_ring[u][e][...] = _dot_nt(qs[e], kj)
            vj = v_scr[rows(t - 1), :]
            for e in range(2):
                sp, sig = _softplus(z_ring[(u - 1) % 4][e][...], True)
                sg_ring[(u - 1) % 4][e][...] = sig
                in_buf[w][e][...] = _dot(sp.astype(BF16), upper)
                da_buf[w][e][...] = _dot_nt(dos[e], vj)
            for e in range(2):
                incl = in_buf[r][e][...]
                cs[e] = cs[e] + incl[:, 0:1]
                off = jnp.where(t - 2 < i, tots[e] - cs[e], -NEG_BIG)
                a = jnp.exp(z_ring[(u - 2) % 4][e][...] - incl - off)
                dw = a * da_buf[r][e][...]
                dw_buf[w][e][...] = dw
                pr_buf[w][e * tb:(e + 1) * tb, :] = a.astype(BF16)
                pre_buf[w][e][...] = _dot(dw.astype(BF16), lower)
            for e in range(2):
                pre = pre_buf[r][e][...] + cps[e]
                dzb = (dw_buf[r][e][...] - sg_ring[(u - 3) % 4][e][...] * pre).astype(BF16)
                cps[e] = pre[:, tb - 1:tb]
                dzr_buf[w][e * tb:(e + 1) * tb, :] = dzb
                dzc_buf[w][:, e * tb:(e + 1) * tb] = dzb
            return cs[0], cs[1], cps[0], cps[1], dq

        col = jnp.zeros((tb, 1), F32)
        carry = (col, col, col, col, jnp.zeros((tb, HEAD_PAIR), F32))
        def four_trips(n, cr):
            for u in range(4):
                cr = trip(4 * n + u, u, cr)
            return cr

        carry = lax.fori_loop(0, jnp.where(i > 0, (i + 7) // 4, 0), four_trips, carry)

        cps, dq = carry[2:4], carry[4]
        kj = k_scr[rows(i), :]
        vj = v_scr[rows(i), :]
        dzs, probs = [], []
        for e in range(2):
            z = _dot_nt(qs[e], kj)
            sp, sig = _softplus(z, True)
            incl = _dot(jnp.where(causal, sp, 0.0).astype(BF16), upper)
            a = jnp.where(causal, jnp.exp(z - incl), 0.0)
            dw = a * _dot_nt(dos[e], vj)
            pre = _dot(dw.astype(BF16), lower) + cps[e]
            dzs.append(jnp.where(causal, dw - sig * pre, 0.0).astype(BF16))
            probs.append(a.astype(BF16))
        dq = dq + _dot(jnp.concatenate(dzs, axis=1), split_heads(kj))
        dkt_acc[i] += _dot(qcat_t, jnp.concatenate(dzs, axis=0))
        dvt_acc[i] += _dot(docat_t, jnp.concatenate(probs, axis=0))
        dq_ref[...] = (dq * Q_SCALE).astype(BF16)

        @pl.when(i == nq - 1)
        def _():
            for acc, dst in ((dkt_acc, dk_hbm), (dvt_acc, dv_hbm)):
                def flip(n, _, acc=acc):
                    at = pl.ds(pl.multiple_of(n * tb, tb), tb)
                    stage[at, :] = acc[n].T
                    return 0
                lax.fori_loop(0, nq, flip, 0)
                pltpu.sync_copy(stage, dst.at[:, lanes])

    blk = pl.BlockSpec((tb, HEAD_PAIR), lambda h, i: (i, h))
    grad = jax.ShapeDtypeStruct((s, D_ATTN), F32)
    return _call(
        body, name="attn_bwd",
        out_shape=[jax.ShapeDtypeStruct((s, D_ATTN), BF16), grad, grad],
        grid=(4, nq),
        in_specs=[blk, blk, pl.BlockSpec((None, tb, 2), lambda h, i: (h, i, 0)), ANY, ANY],
        out_specs=[blk, ANY, ANY],
        scratch_shapes=[pltpu.VMEM((s, HEAD_PAIR), BF16), pltpu.VMEM((s, HEAD_PAIR), BF16),
                        pltpu.VMEM((nq, HEAD_PAIR, tb), F32), pltpu.VMEM((nq, HEAD_PAIR, tb), F32),
                        pltpu.VMEM((s, HEAD_PAIR), F32)]
        + [pltpu.VMEM((tb, tb), F32)] * 32
        + [pltpu.VMEM((2 * tb, tb), BF16)] * 4 + [pltpu.VMEM((tb, 2 * tb), BF16)] * 2,
        compiler_params=_params(("arbitrary", "arbitrary"), vmem_mb=60))(q, do, totals, k, v)


def _pool_bwd(u, dmp, w_pool, pool_scale):
    s = u.shape[0]
    ts = 512
    nr = s // ts
    per = ts // POOL_HALO

    def body(u_ref, halo_ref, dm_ref, wp_ref, ps_ref, du_ref, dwp_ref, dps_ref, ext_ref, y_ref, dext_ref):
        i = pl.program_id(0)
        rb = nr - 1 - i

        @pl.when(i == 0)
        def _():
            dext_ref[ts:, :] = jnp.zeros((POOL_HALO, D_POOL), F32)
            dwp_ref[...] = jnp.zeros_like(dwp_ref)
            dps_ref[...] = jnp.zeros_like(dps_ref)

        ext_ref[0:POOL_HALO, :] = jnp.where(rb > 0, halo_ref[...], 0.0)
        ext_ref[POOL_HALO:, :] = u_ref[...]
        ps, cnts = [], []
        for g, window in enumerate(POOL_WINDOWS):
            p, cnt = _pool_means(ext_ref, g, window, ts, rb * ts)
            ps.append(p.astype(BF16))
            cnts.append(cnt)
            y_ref[:, g * POOL_GROUP:(g + 1) * POOL_GROUP] = _dot(ps[g], wp_ref[g].astype(BF16))
        y = y_ref[...]
        r = _rstd(y)
        yn = y * r
        dm = dm_ref[...]
        dps_ref[...] += jnp.sum(dm * yn, axis=0, keepdims=True)
        dn = dm * ps_ref[...]
        dy = r * (dn - yn * jnp.mean(dn * yn, axis=-1, keepdims=True))
        for g, window in enumerate(POOL_WINDOWS):
            cols = slice(g * POOL_GROUP, (g + 1) * POOL_GROUP)
            dyg = dy[:, cols].astype(BF16)
            dwp_ref[g] += _dot_tn(ps[g], dyg)
            dp = _dot_nt(dyg, wp_ref[g].astype(BF16))
            dext_ref[0:ts, cols] = dp / cnts[g]
            acc = dext_ref[0:ts, cols]
            for d in range(1, window):
                acc = acc + dext_ref[d:d + ts, cols]
            du_ref[:, cols] = (acc - dp).astype(BF16)
        dext_ref[ts:, :] = dext_ref[0:POOL_HALO, :]

    rows = pl.BlockSpec((ts, D_POOL), lambda i: (nr - 1 - i, 0))
    return _call(
        body, name="pool_bwd",
        out_shape=[jax.ShapeDtypeStruct((s, D_POOL), BF16), jax.ShapeDtypeStruct(w_pool.shape, F32),
                   jax.ShapeDtypeStruct((1, D_POOL), F32)],
        grid=(nr,),
        in_specs=[rows,
                  pl.BlockSpec((POOL_HALO, D_POOL), lambda i: (jnp.maximum((nr - 1 - i) * per - 1, 0), 0)),
                  rows, _full(w_pool.shape), _full((1, D_POOL))],
        out_specs=[rows, _full(w_pool.shape), _full((1, D_POOL))],
        scratch_shapes=[pltpu.VMEM((ts + POOL_HALO, D_POOL), F32), pltpu.VMEM((ts, D_POOL), F32),
                        pltpu.VMEM((ts + POOL_HALO, D_POOL), F32)],
        compiler_params=_params(("arbitrary",)))(u, u, dmp, w_pool, pool_scale)


def _in_proj_bwd(du, dq, dk, dv, w_in, x, dx1, g1):
    s = x.shape[0]
    ts = 512

    def body(du_ref, dq_ref, dk_ref, dv_ref, w_ref, x_ref, dx1_ref, g_ref, gx_ref, dg_ref):
        @pl.when(pl.program_id(0) == 0)
        def _():
            dg_ref[...] = jnp.zeros_like(dg_ref)

        dh = _dot_nt(du_ref[...], w_ref[0]) + _dot_nt(dq_ref[...], w_ref[1])
        dh = dh + _dot_nt(dk_ref[...].astype(BF16), w_ref[2]) + _dot_nt(dv_ref[...].astype(BF16), w_ref[3])
        xv = x_ref[...]
        r = _rstd(xv)
        xn = xv * r
        dg_ref[...] += jnp.sum(dh * xn, axis=0, keepdims=True)
        dxn = dh * g_ref[...]
        gx_ref[...] = dx1_ref[...] + r * (dxn - xn * jnp.mean(dxn * xn, axis=-1, keepdims=True))

    row = lambda w: pl.BlockSpec((ts, w), lambda i: (i, 0))
    return _call(
        body, name="in_proj_bwd",
        out_shape=[jax.ShapeDtypeStruct((s, D_MODEL), F32), jax.ShapeDtypeStruct((1, D_MODEL), F32)],
        grid=(s // ts,),
        in_specs=[row(D_POOL)] * 4 + [_full(w_in.shape), row(D_MODEL), row(D_MODEL), _full((1, D_MODEL))],
        out_specs=[row(D_MODEL), _full((1, D_MODEL))],
        compiler_params=_params(("arbitrary",)))(du, dq, dk, dv, w_in, x, dx1, g1)


_SMALL = ("norm_mix_pre", "w_pool", "pool_scale", "attn_scale", "norm_mix_post",
          "norm_ffn_pre", "conv_b", "norm_ffn_post")
_SMALL_SIZE = {"norm_mix_pre": 1024, "w_pool": 65536, "pool_scale": 512, "attn_scale": 512,
               "norm_mix_post": 1024, "norm_ffn_pre": 1024, "conv_b": 5632, "norm_ffn_post": 1024}
_SMALL_ROWS = 600
_CONVW_ROWS = 132
_PACK_ROWS = _SMALL_ROWS + _CONVW_ROWS + 4


def _pack_small(parts):
    flat = jnp.concatenate([parts[n].reshape(-1) for n in _SMALL])
    flat = jnp.pad(flat, (0, _SMALL_ROWS * 128 - flat.shape[0]))
    return flat.reshape(_SMALL_ROWS, 128)


def _unpack_small(packed, like):
    flat = packed.reshape(-1)
    out, off = {}, 0
    for n in _SMALL:
        out[n] = flat[off:off + _SMALL_SIZE[n]].reshape(like[n].shape)
        off += _SMALL_SIZE[n]
    return out


def kernel(x, norm_mix_pre, w_in, w_pool, pool_scale, attn_scale, w_out, norm_mix_post, norm_ffn_pre, w_up, conv_w, conv_b, w_down, norm_ffn_post, loss_target, m_norm_mix_pre, m_w_in, m_w_pool, m_pool_scale, m_attn_scale, m_w_out, m_norm_mix_post, m_norm_ffn_pre, m_w_up, m_conv_w, m_conv_b, m_w_down, m_norm_ffn_post, v_norm_mix_pre, v_w_in, v_w_pool, v_pool_scale, v_attn_scale, v_w_out, v_norm_mix_post, v_norm_ffn_pre, v_w_up, v_conv_w, v_conv_b, v_w_down, v_norm_ffn_post):
    weights = dict(norm_mix_pre=norm_mix_pre, w_in=w_in, w_pool=w_pool, pool_scale=pool_scale,
                   attn_scale=attn_scale, w_out=w_out, norm_mix_post=norm_mix_post,
                   norm_ffn_pre=norm_ffn_pre, w_up=w_up, conv_w=conv_w, conv_b=conv_b,
                   w_down=w_down, norm_ffn_post=norm_ffn_post)
    mom1 = dict(norm_mix_pre=m_norm_mix_pre, w_in=m_w_in, w_pool=m_w_pool, pool_scale=m_pool_scale,
                attn_scale=m_attn_scale, w_out=m_w_out, norm_mix_post=m_norm_mix_post,
                norm_ffn_pre=m_norm_ffn_pre, w_up=m_w_up, conv_w=m_conv_w, conv_b=m_conv_b,
                w_down=m_w_down, norm_ffn_post=m_norm_ffn_post)
    mom2 = dict(norm_mix_pre=v_norm_mix_pre, w_in=v_w_in, w_pool=v_w_pool, pool_scale=v_pool_scale,
                attn_scale=v_attn_scale, w_out=v_w_out, norm_mix_post=v_norm_mix_post,
                norm_ffn_pre=v_norm_ffn_pre, w_up=v_w_up, conv_w=v_conv_w, conv_b=v_conv_b,
                w_down=v_w_down, norm_ffn_post=v_norm_ffn_post)
    order = list(weights)

    xs = x[0]
    target = loss_target[0]
    wp = w_pool[0]
    shard = lax.axis_index("x") * 2 + lax.axis_index("y")

    slot = shard.astype(jnp.int32).reshape(1)
    win_g, = _gather_shards([_cast_bf16(w_in[0], "cast_w_in")])
    lands = [_into_slot(w_out[0], slot, BF16, "cast_w_out"), _into_slot(w_up[0], slot, BF16, "cast_w_up"),
             _into_slot(w_down[0], slot, BF16, "cast_w_down"), _into_slot(conv_w[0], slot, F32, "place_conv_w")]
    g_send, g_recv, g_lands, g_token = _exchange_start(None, lands, win_g, "gather_start")
    convb_g = conv_b[0].reshape(N_SHARD, 1, FF_TILE)

    u, q, k, v, vt, h1 = _in_proj(xs, norm_mix_pre + g_token[0:1, 0:1], win_g)
    mpool = _pool_fwd(u, wp, pool_scale)
    attn, totals = _attn_fwd(q, k, vt)
    _, (wout_g, wup_g, wdown_g, convw_g) = _exchange_wait(g_send, g_recv, g_lands, False, attn, "gather_wait")
    wout_f = wout_g.reshape(D_MODEL, D_MODEL)
    wdown_f = wdown_g.reshape(D_FF, D_MODEL)
    mattn, mix, x1, h2 = _mix_out(attn, mpool, xs, attn_scale, wout_f, norm_mix_post, norm_ffn_pre)
    upre_g, upre_v, f_in = _ffn_up(h2, wup_g, convw_g, convb_g)
    df, dy, loss_tile, d_post = _ffn_down(f_in, wdown_f, x1, target, norm_ffn_post)

    d_wdown = _tn_matmul(f_in, df[None], "dw_down")
    dug, duv, dcw_g, dcw_v, dcb_g, dcb_v = _ffn_bwd_act(df, wdown_f, upre_g, upre_v, convw_g, convb_g)
    d_wup = jnp.concatenate([_tn_matmul(h2[None], dug, "dw_up_gate")[0],
                             _tn_matmul(h2[None], duv, "dw_up_value")[0]], axis=0)
    early = [d_wup, d_wdown.reshape(N_SHARD, D_FF // N_SHARD, D_MODEL)]
    s_send, s_recv, s_thru, s_token = _exchange_start(
        early, [lax.empty((3,) + g.shape[1:], g.dtype) for g in early], d_wup, "scatter_start")
    dx1, dmix, d_ffn_pre, d_mix_post = _ffn_bwd_in(
        dug, duv, wup_g, x1, dy, mix, norm_ffn_pre + s_token[0:1, 0:1], norm_mix_post)
    d_wout = jnp.concatenate([_tn_matmul(mpool[None], dmix[None], "dw_out_pool")[0, 0],
                              _tn_matmul(mattn[None], dmix[None], "dw_out_attn")[0, 0]], axis=0)
    dmp, do, d_attn_scale = _mix_bwd(dmix, wout_f, attn, attn_scale)
    dq, dk, dv = _attn_bwd(q, k, v, do, totals)
    du, d_wpool, d_pool_scale = _pool_bwd(u, dmp, wp, pool_scale)
    d_win = jnp.stack([_tn_matmul(h1[None], t[None], "dw_in_%d" % n)[0, 0]
                       for n, t in enumerate((du, dq, dk, dv))])
    grad_x, d_mix_pre = _in_proj_bwd(du, dq, dk, dv, win_g, xs, dx1, norm_mix_pre)

    d_convw = jnp.concatenate([dcw_g, dcw_v], axis=0)
    d_convb = jnp.concatenate([dcb_g, dcb_v], axis=0).reshape(1, 2 * D_FF)
    small_parts = dict(norm_mix_pre=d_mix_pre, w_pool=d_wpool, pool_scale=d_pool_scale,
                       attn_scale=d_attn_scale, norm_mix_post=d_mix_post, norm_ffn_pre=d_ffn_pre,
                       conv_b=d_convb, norm_ffn_post=d_post)
    packed = jnp.concatenate([_pack_small(small_parts), d_convw.reshape(_CONVW_ROWS, 128),
                              loss_tile[0:4]], axis=0)
    late, gathered = _scatter_grads([d_win, d_wout.reshape(N_SHARD, D_MODEL // N_SHARD, D_MODEL)], packed)
    early_srcs, early_lands = _exchange_wait(s_send, s_recv, s_thru, True, grad_x, "scatter_wait")
    quarter = [_sum_slots(r, (3, 0, 1, 2), "sum_chips_%d" % n) for n, r in enumerate(late)]
    quarter += [_sum_own_and_received(early_srcs[n], slot, early_lands[n], "sum_chips_%d" % (n + 2))
                for n in range(2)]
    sibling = _swap_with_sibling(quarter)
    small_sum = _sum_slots(gathered, tuple(range(8)), "sum_small")

    results = {}
    for n, name in enumerate(("w_in", "w_out", "w_up", "w_down")):
        res = _adamw([quarter[n], sibling[n]], weights[name][0], mom1[name][0], mom2[name][0],
                     "adamw_" + name)
        results[name] = [t[None] for t in res]
    g_convw = lax.dynamic_slice_in_dim(
        small_sum[_SMALL_ROWS:_SMALL_ROWS + _CONVW_ROWS].reshape(N_SHARD, 3, FF_TILE), shard, 1, axis=0)[0]
    convw_pad = lambda t: jnp.pad(t, ((0, 5), (0, 0)))
    res = _adamw([convw_pad(g_convw)], convw_pad(conv_w[0]), convw_pad(m_conv_w[0]),
                 convw_pad(v_conv_w[0]), "adamw_conv_w")
    results["conv_w"] = [t[:3][None] for t in res]
    pack_w = _pack_small(weights)
    pack_m = _pack_small(mom1)
    pack_v = _pack_small(mom2)
    res = _adamw([small_sum[:_SMALL_ROWS]], pack_w, pack_m, pack_v, "adamw_small")
    unpacked = [_unpack_small(t, weights) for t in res]
    for name in _SMALL:
        results[name] = [t[name] for t in unpacked]

    loss = small_sum[_SMALL_ROWS + _CONVW_ROWS, 0]
    outs = [loss, grad_x[None]]
    for slot in range(4):
        outs.extend(results[name][slot] for name in order)
    return tuple(outs)
```

```python
import functools

import jax
import jax.numpy as jnp
from jax import lax
from jax.experimental import pallas as pl
from jax.experimental.pallas import tpu as pltpu

F32 = jnp.float32
BF16 = jnp.bfloat16

D_MODEL = 1024
D_POOL = 512
D_ATTN = 512
POOL_WINDOWS = (2, 4, 8, 16)
POOL_GROUP = 128
POOL_HALO = 16
CONV_HALO = 8
D_FF = 2816
FF_TILE = 1408
N_SHARD = 4
EPS = 1e-6
Q_SCALE = 0.125
ATT_BLOCK = 256
HEAD_PAIR = 128
MIB = 1 << 20
NEG_BIG = -1e30

ADAM_LR = 0.001
ADAM_B1 = 0.9
ADAM_B2 = 0.999
ADAM_EPS = 1e-08
ADAM_WD = 0.01
ADAM_STEP = 10

NT_DIMS = (((1,), (1,)), ((), ()))
TN_DIMS = (((0,), (0,)), ((), ()))
MESH = pl.DeviceIdType.MESH
ANY = pl.BlockSpec(memory_space=pl.ANY)
HBM_SPEC = pl.BlockSpec(memory_space=pltpu.HBM)
SEM_SPEC = pl.BlockSpec(memory_space=pltpu.SEMAPHORE)
DATAFLOW = pltpu.SideEffectType.DATAFLOW_SIDE_EFFECTING


def _call(body, **kw):
    return pl.pallas_call(body, **kw)


def _params(sem=None, vmem_mb=48):
    return pltpu.CompilerParams(dimension_semantics=sem, vmem_limit_bytes=vmem_mb * MIB)


def _rstd(v):
    return lax.rsqrt(jnp.mean(v * v, axis=-1, keepdims=True) + EPS)


def _dot(a, b):
    return jnp.dot(a, b, preferred_element_type=F32)


def _dot_nt(a, b):
    return lax.dot_general(a, b, NT_DIMS, preferred_element_type=F32)


def _dot_tn(a, b):
    return lax.dot_general(a, b, TN_DIMS, preferred_element_type=F32)


def _row_tile(rows, cap):
    t = min(rows, cap)
    t -= t % 8
    while rows % t:
        t -= 8
    return t


def _full(shape):
    nd = len(shape)
    return pl.BlockSpec(shape, lambda *_: (0,) * nd)


def _chip_peers():
    x, y, c = lax.axis_index("x"), lax.axis_index("y"), lax.axis_index("c")
    return x, y, c, [(1 - x, y), (x, 1 - y), (1 - x, 1 - y)]


def _cast_bf16(a, name):
    def body(a_ref, o_ref):
        o_ref[...] = a_ref[...].astype(BF16)

    return _call(body, name=name, out_shape=jax.ShapeDtypeStruct(a.shape, BF16),
                 grid=(1,), in_specs=[_full(a.shape)], out_specs=_full(a.shape),
                 compiler_params=_params(("arbitrary",)))(a)


def _into_slot(a, slot, dtype, name):
    nd = a.ndim

    def body(slot_ref, a_ref, o_ref):
        o_ref[...] = a_ref[...].astype(dtype)

    return _call(
        body, name=name, out_shape=jax.ShapeDtypeStruct((N_SHARD,) + a.shape, dtype),
        grid_spec=pltpu.PrefetchScalarGridSpec(
            num_scalar_prefetch=1, grid=(1,),
            in_specs=[pl.BlockSpec(a.shape, lambda i, slot_ref: (0,) * nd)],
            out_specs=pl.BlockSpec((None,) + a.shape, lambda i, slot_ref: (slot_ref[0],) + (0,) * nd)),
        compiler_params=_params(("arbitrary",)))(slot, a)


def _exchange_copies(srcs, lands, send, recv):
    x, y, c, chips = _chip_peers()
    copies = []
    for t in range(len(lands)):
        for k, (px, py) in enumerate(chips):
            copies.append(pltpu.make_async_remote_copy(
                src_ref=lands[t].at[2 * x + y] if srcs is None else srcs[t].at[2 * px + py],
                dst_ref=lands[t].at[2 * x + y] if srcs is None else lands[t].at[k],
                send_sem=send.at[3 * t + k], recv_sem=recv.at[3 * t + k],
                device_id=(px, py, c), device_id_type=MESH))
    return copies


def _exchange_start(srcs, lands, after, name):
    n = len(lands)
    operands = list(lands) if srcs is None else list(srcs) + list(lands)
    m = len(operands)

    def body(*refs):
        for cp in _exchange_copies(None if srcs is None else refs[:n], refs[m - n:m], refs[m + 1], refs[m + 2]):
            cp.start()
        refs[-1][...] = jnp.zeros_like(refs[-1])

    res = _call(
        body, name=name,
        out_shape=[pltpu.SemaphoreType.DMA((3 * n,)), pltpu.SemaphoreType.DMA((3 * n,))]
        + [pltpu.HBM(a.shape, a.dtype) for a in operands] + [jax.ShapeDtypeStruct((8, 128), F32)],
        in_specs=[HBM_SPEC] * m + [ANY],
        out_specs=[SEM_SPEC, SEM_SPEC] + [HBM_SPEC] * m + [pl.BlockSpec(memory_space=pltpu.VMEM)],
        input_output_aliases={j: j + 2 for j in range(m)},
        compiler_params=pltpu.CompilerParams(has_side_effects=DATAFLOW),
    )(*[pltpu.with_memory_space_constraint(a, pltpu.HBM) for a in operands], after)
    return res[0], res[1], res[2:2 + m], res[-1]


def _exchange_wait(send, recv, operands, scatter, after, name):
    m = len(operands)
    n = m // 2 if scatter else m

    def body(*refs):
        for cp in _exchange_copies(refs[:n] if scatter else None, refs[m - n:m], refs[m], refs[m + 1]):
            cp.wait_send()
            cp.wait_recv()

    res = _call(
        body, name=name, out_shape=[pltpu.HBM(a.shape, a.dtype) for a in operands],
        in_specs=[HBM_SPEC] * m + [SEM_SPEC, SEM_SPEC, ANY], out_specs=[HBM_SPEC] * m,
        input_output_aliases={j: j for j in range(m)},
        compiler_params=pltpu.CompilerParams(has_side_effects=DATAFLOW),
    )(*operands, send, recv, after)
    return res[:m - n], res[m - n:]


def _gather_shards(shards):
    n = len(shards)

    def body(*refs):
        ins, outs = refs[:n], refs[n:2 * n]
        send, recv, loc = refs[2 * n:]
        x, y, c, chips = _chip_peers()
        b = 2 * x + y
        local = [pltpu.make_async_copy(ins[t], outs[t].at[b], loc.at[t]) for t in range(n)]
        for cp in local:
            cp.start()
        remote = []
        for t in range(n):
            for k, (px, py) in enumerate(chips):
                remote.append(pltpu.make_async_remote_copy(
                    src_ref=ins[t], dst_ref=outs[t].at[b],
                    send_sem=send.at[3 * t + k], recv_sem=recv.at[3 * t + k],
                    device_id=(px, py, c), device_id_type=MESH))
        for cp in remote:
            cp.start()
        for cp in remote:
            cp.wait()
        for cp in local:
            cp.wait()

    return _call(
        body, name="gather_w_in",
        out_shape=[jax.ShapeDtypeStruct((N_SHARD,) + s.shape, s.dtype) for s in shards],
        in_specs=[ANY] * n, out_specs=[ANY] * n,
        scratch_shapes=[pltpu.SemaphoreType.DMA((3 * n,)), pltpu.SemaphoreType.DMA((3 * n,)),
                        pltpu.SemaphoreType.DMA((n,))],
    )(*shards)


def _scatter_grads(grads, small):
    n = len(grads)

    def body(*refs):
        ins, small_in = refs[:n], refs[n]
        outs, small_out = refs[n + 1:2 * n + 1], refs[2 * n + 1]
        send, recv, loc, ssend, srecv = refs[2 * n + 2:]
        x, y, c, chips = _chip_peers()
        b = 2 * x + y
        me = 4 * x + 2 * y + c
        local = [pltpu.make_async_copy(ins[t].at[b], outs[t].at[3], loc.at[t]) for t in range(n)]
        local.append(pltpu.make_async_copy(small_in, small_out.at[me], loc.at[n]))
        for cp in local:
            cp.start()
        remote = []
        for t in range(n):
            for k, (px, py) in enumerate(chips):
                remote.append(pltpu.make_async_remote_copy(
                    src_ref=ins[t].at[2 * px + py], dst_ref=outs[t].at[k],
                    send_sem=send.at[3 * t + k], recv_sem=recv.at[3 * t + k],
                    device_id=(px, py, c), device_id_type=MESH))
        for r in range(1, 8):
            px = 1 - x if r & 4 else x
            py = 1 - y if r & 2 else y
            pc = 1 - c if r & 1 else c
            remote.append(pltpu.make_async_remote_copy(
                src_ref=small_in, dst_ref=small_out.at[me],
                send_sem=ssend.at[r - 1], recv_sem=srecv.at[r - 1],
                device_id=(px, py, pc), device_id_type=MESH))
        for cp in remote:
            cp.start()
        for cp in remote:
            cp.wait()
        for cp in local:
            cp.wait()

    out_shape = [jax.ShapeDtypeStruct(g.shape, g.dtype) for g in grads]
    out_shape.append(jax.ShapeDtypeStruct((8,) + small.shape, small.dtype))
    res = _call(
        body, name="scatter_grads", out_shape=out_shape,
        in_specs=[ANY] * (n + 1), out_specs=[ANY] * (n + 1),
        scratch_shapes=[pltpu.SemaphoreType.DMA((3 * n,)), pltpu.SemaphoreType.DMA((3 * n,)),
                        pltpu.SemaphoreType.DMA((n + 1,)),
                        pltpu.SemaphoreType.DMA((7,)), pltpu.SemaphoreType.DMA((7,))],
    )(*grads, small)
    return res[:n], res[n]


def _swap_with_sibling(parts):
    n = len(parts)

    def body(*refs):
        ins, outs = refs[:n], refs[n:2 * n]
        send, recv = refs[2 * n:]
        x, y, c = lax.axis_index("x"), lax.axis_index("y"), lax.axis_index("c")
        copies = [pltpu.make_async_remote_copy(
            src_ref=ins[t], dst_ref=outs[t], send_sem=send.at[t], recv_sem=recv.at[t],
            device_id=(x, y, 1 - c), device_id_type=MESH) for t in range(n)]
        for cp in copies:
            cp.start()
        for cp in copies:
            cp.wait()

    return _call(
        body, name="swap_sibling",
        out_shape=[jax.ShapeDtypeStruct(p.shape, p.dtype) for p in parts],
        in_specs=[ANY] * n, out_specs=[ANY] * n,
        scratch_shapes=[pltpu.SemaphoreType.DMA((n,)), pltpu.SemaphoreType.DMA((n,))],
    )(*parts)


def _sum_slots(buf, order, name):
    k, rows, cols = buf.shape
    tr = _row_tile(rows, 256)

    def body(b_ref, o_ref):
        acc = b_ref[order[0]].astype(F32)
        for s in order[1:]:
            acc = acc + b_ref[s].astype(F32)
        o_ref[...] = acc

    return _call(body, name=name, out_shape=jax.ShapeDtypeStruct((rows, cols), F32),
                 grid=(rows // tr,),
                 in_specs=[pl.BlockSpec((k, tr, cols), lambda i: (0, i, 0))],
                 out_specs=pl.BlockSpec((tr, cols), lambda i: (i, 0)),
                 compiler_params=_params(("parallel",)))(buf)


def _sum_own_and_received(src, slot, land, name):
    _, rows, cols = src.shape
    tr = _row_tile(rows, 256)

    def body(slot_ref, s_ref, l_ref, o_ref):
        acc = s_ref[...].astype(F32)
        for k in range(3):
            acc = acc + l_ref[k].astype(F32)
        o_ref[...] = acc

    return _call(
        body, name=name, out_shape=jax.ShapeDtypeStruct((rows, cols), F32),
        grid_spec=pltpu.PrefetchScalarGridSpec(
            num_scalar_prefetch=1, grid=(rows // tr,),
            in_specs=[pl.BlockSpec((None, tr, cols), lambda i, slot_ref: (slot_ref[0], i, 0)),
                      pl.BlockSpec((3, tr, cols), lambda i, slot_ref: (0, i, 0))],
            out_specs=pl.BlockSpec((tr, cols), lambda i, slot_ref: (i, 0))),
        compiler_params=_params(("parallel",)))(slot, src, land)


def _adamw(grad_parts, w, m, v, name):
    rows, cols = w.shape
    tr = _row_tile(rows, 256)
    npart = len(grad_parts)

    def body(*refs):
        gp = refs[:npart]
        w_ref, m_ref, v_ref, g_out, d_out, m_out, v_out = refs[npart:]
        g = gp[0][...]
        for p in gp[1:]:
            g = g + p[...]
        mm = ADAM_B1 * m_ref[...] + (1.0 - ADAM_B1) * g
        vv = ADAM_B2 * v_ref[...] + (1.0 - ADAM_B2) * jnp.square(g)
        m_hat = mm / (1.0 - ADAM_B1 ** ADAM_STEP)
        v_hat = vv / (1.0 - ADAM_B2 ** ADAM_STEP)
        g_out[...] = g
        d_out[...] = -ADAM_LR * (m_hat / (jnp.sqrt(v_hat) + ADAM_EPS) + ADAM_WD * w_ref[...])
        m_out[...] = mm
        v_out[...] = vv

    spec = pl.BlockSpec((tr, cols), lambda i: (i, 0))
    shp = jax.ShapeDtypeStruct((rows, cols), F32)
    return _call(body, name=name, out_shape=[shp] * 4, grid=(rows // tr,),
                 in_specs=[spec] * (npart + 3), out_specs=[spec] * 4,
                 compiler_params=_params(("parallel",)))(*grad_parts, w, m, v)


def _in_proj(x, g1, w_in):
    s = x.shape[0]
    ts = 512

    def body(x_ref, g_ref, w_ref, u_ref, q_ref, k_ref, v_ref, vt_ref, h_ref):
        xv = x_ref[...]
        h = (xv * _rstd(xv) * g_ref[...]).astype(BF16)
        h_ref[...] = h
        u_ref[...] = _dot(h, w_ref[0])
        q_ref[...] = (_dot(h, w_ref[1]) * Q_SCALE).astype(BF16)
        k_ref[...] = _dot(h, w_ref[2]).astype(BF16)
        v = _dot(h, w_ref[3])
        v_ref[...] = v.astype(BF16)
        vt = v.T.astype(BF16)
        for n in range(ts // ATT_BLOCK):
            vt_ref[n] = vt[:, n * ATT_BLOCK:(n + 1) * ATT_BLOCK]

    row = lambda w: pl.BlockSpec((ts, w), lambda i: (i, 0))
    half = jax.ShapeDtypeStruct((s, D_POOL), BF16)
    return _call(
        body, name="in_proj",
        out_shape=[jax.ShapeDtypeStruct((s, D_POOL), F32), half, half, half,
                   jax.ShapeDtypeStruct((s // ATT_BLOCK, D_ATTN, ATT_BLOCK), BF16),
                   jax.ShapeDtypeStruct((s, D_MODEL), BF16)],
        grid=(s // ts,),
        in_specs=[row(D_MODEL), _full((1, D_MODEL)), _full(w_in.shape)],
        out_specs=[row(D_POOL)] * 4
        + [pl.BlockSpec((ts // ATT_BLOCK, D_ATTN, ATT_BLOCK), lambda i: (i, 0, 0)), row(D_MODEL)],
        compiler_params=_params(("parallel",)))(x, g1, w_in)


def _pool_means(ext_ref, g, window, ts, row0):
    cols = slice(g * POOL_GROUP, (g + 1) * POOL_GROUP)
    cur = ext_ref[POOL_HALO:POOL_HALO + ts, cols]
    acc = cur
    for d in range(1, window):
        acc = acc + ext_ref[POOL_HALO - d:POOL_HALO - d + ts, cols]
    t1 = row0 + 1 + lax.broadcasted_iota(jnp.int32, (ts, 1), 0)
    cnt = jnp.minimum(t1, window).astype(F32)
    return acc / cnt - cur, cnt


def _pool_fwd(u, w_pool, pool_scale):
    s = u.shape[0]
    ts = 512
    per = ts // POOL_HALO

    def body(u_ref, halo_ref, wp_ref, ps_ref, o_ref, ext_ref, y_ref):
        i = pl.program_id(0)
        ext_ref[0:POOL_HALO, :] = jnp.where(i > 0, halo_ref[...], 0.0)
        ext_ref[POOL_HALO:, :] = u_ref[...]
        for g, window in enumerate(POOL_WINDOWS):
            p, _ = _pool_means(ext_ref, g, window, ts, i * ts)
            y_ref[:, g * POOL_GROUP:(g + 1) * POOL_GROUP] = _dot(
                p.astype(BF16), wp_ref[g].astype(BF16))
        y = y_ref[...]
        o_ref[...] = (y * _rstd(y) * ps_ref[...]).astype(BF16)

    return _call(
        body, name="pool_fwd", out_shape=jax.ShapeDtypeStruct((s, D_POOL), BF16),
        grid=(s // ts,),
        in_specs=[pl.BlockSpec((ts, D_POOL), lambda i: (i, 0)),
                  pl.BlockSpec((POOL_HALO, D_POOL), lambda i: (jnp.maximum(i * per - 1, 0), 0)),
                  _full(w_pool.shape), _full((1, D_POOL))],
        out_specs=pl.BlockSpec((ts, D_POOL), lambda i: (i, 0)),
        scratch_shapes=[pltpu.VMEM((ts + POOL_HALO, D_POOL), F32), pltpu.VMEM((ts, D_POOL), F32)],
        compiler_params=_params(("parallel",)))(u, u, w_pool, pool_scale)


def _tri(kind):
    r = lax.broadcasted_iota(jnp.int32, (ATT_BLOCK, ATT_BLOCK), 0)
    c = lax.broadcasted_iota(jnp.int32, (ATT_BLOCK, ATT_BLOCK), 1)
    return jnp.where(r >= c if kind == "suffix" else r <= c, 1.0, 0.0).astype(BF16)


def _causal_mask():
    r = lax.broadcasted_iota(jnp.int32, (ATT_BLOCK, ATT_BLOCK), 0)
    c = lax.broadcasted_iota(jnp.int32, (ATT_BLOCK, ATT_BLOCK), 1)
    return c < r


def _softplus(z, with_sigmoid=False):
    ope = 1.0 + jnp.exp(jnp.minimum(z, 80.0))
    sp = jnp.maximum(z, jnp.log(ope))
    if with_sigmoid:
        return sp, 1.0 - 1.0 / ope
    return sp


def _attn_fwd(q, k, vt):
    s = q.shape[0]
    tb = ATT_BLOCK
    nq = s // tb

    def body(q_ref, k_ref, vt_ref, o_ref, t_ref, *bufs):
        i = pl.program_id(1)
        suffix = _tri("prefix")
        r_idx = lax.broadcasted_iota(jnp.int32, (tb, tb), 0)
        c_idx = lax.broadcasted_iota(jnp.int32, (tb, tb), 1)
        causal = r_idx < c_idx
        lane = lax.broadcasted_iota(jnp.int32, (1, HEAD_PAIR), 1)
        first = lane < 64
        top = lax.broadcasted_iota(jnp.int32, (HEAD_PAIR, 1), 0) < 64
        q2 = q_ref[...]
        zero = jnp.zeros_like(q2)
        qs = (jnp.where(first, q2, zero), jnp.where(first, zero, q2))

        def values_t(j):
            vt = vt_ref[j]
            none = jnp.zeros_like(vt)
            return jnp.concatenate([jnp.where(top, vt, none), jnp.where(top, none, vt)], axis=1)

        def scores(j, cs, masked):
            kj = k_ref[pl.ds(pl.multiple_of(j * tb, tb), tb), :]
            new_cs, args = [], []
            for e in range(2):
                z = _dot_nt(kj, qs[e])
                sp = _softplus(z)
                if masked:
                    sp = jnp.where(causal, sp, 0.0)
                incl = _dot(suffix, sp.astype(BF16))
                arg = z - incl - cs[e]
                if masked:
                    arg = jnp.where(causal, arg, NEG_BIG)
                args.append(arg)
                new_cs.append(cs[e] + incl[0:1, :])
            return new_cs, args

        def weigh(j, args, o):
            probs = [jnp.exp(arg).astype(BF16) for arg in args]
            return o + _dot(values_t(j), jnp.concatenate(probs, axis=0))

        z_ring = [[bufs[2 * slot + e] for e in range(2)] for slot in range(4)]
        in_buf = [[bufs[8 + 2 * slot + e] for e in range(2)] for slot in range(2)]
        pr_buf = [bufs[12], bufs[13]]

        @pl.when((pl.program_id(0) == 0) & (i == 0))
        def _():
            for b in bufs:
                b[...] = jnp.zeros_like(b)

        def block_of(p):
            return jnp.clip(i - 1 - p, 0, nq - 1)

        def trip(t, u, carry):
            w, r = u % 2, 1 - u % 2
            cs, o = list(carry[0:2]), carry[2]
            live3 = (t - 3 >= 0) & (t - 3 < i)
            o = o + jnp.where(live3, _dot(values_t(block_of(t - 3)), pr_buf[r][...]), 0.0)
            kj = k_ref[pl.ds(pl.multiple_of(block_of(t) * tb, tb), tb), :]
            for e in range(2):
                z_ring[u][e][...] = _dot_nt(kj, qs[e])
            for e in range(2):
                in_buf[w][e][...] = _dot(suffix, _softplus(z_ring[(u - 1) % 4][e][...]).astype(BF16))
            live2 = (t - 2 >= 0) & (t - 2 < i)
            for e in range(2):
                incl = in_buf[r][e][...]
                arg = z_ring[(u - 2) % 4][e][...] - incl - jnp.where(live2, cs[e], -NEG_BIG)
                pr_buf[w][e * tb:(e + 1) * tb, :] = jnp.exp(arg).astype(BF16)
                cs[e] = jnp.where(live2, cs[e] + incl[0:1, :], cs[e])
            return cs[0], cs[1], o

        def four_trips(n, cr):
            for u in range(4):
                cr = trip(4 * n + u, u, cr)
            return cr

        row = jnp.zeros((1, tb), F32)
        cs, args = scores(i, (row, row), True)
        carry = (cs[0], cs[1], weigh(i, args, jnp.zeros((HEAD_PAIR, tb), F32)))
        carry = lax.fori_loop(0, jnp.where(i > 0, (i + 6) // 4, 0), four_trips, carry)
        o_ref[...] = carry[2].T
        totals = jnp.where(r_idx == 0, carry[0], jnp.where(r_idx == 1, carry[1], 0.0))
        t_ref[...] = totals.T[:, 0:2]

    score_buf = pltpu.VMEM((tb, tb), F32)
    return _call(
        body, name="attn_fwd",
        out_shape=[jax.ShapeDtypeStruct((s, D_ATTN), F32),
                   jax.ShapeDtypeStruct((4, s, 2), F32)],
        grid=(4, nq),
        in_specs=[pl.BlockSpec((tb, HEAD_PAIR), lambda h, i: (i, h)),
                  pl.BlockSpec((s, HEAD_PAIR), lambda h, i: (0, h)),
                  pl.BlockSpec((nq, HEAD_PAIR, tb), lambda h, i: (0, h, 0))],
        out_specs=[pl.BlockSpec((tb, HEAD_PAIR), lambda h, i: (i, h)),
                   pl.BlockSpec((None, tb, 2), lambda h, i: (h, i, 0))],
        scratch_shapes=[score_buf] * 12 + [pltpu.VMEM((2 * tb, tb), BF16)] * 2,
        compiler_params=_params(("arbitrary", "arbitrary")))(q, k, vt)


def _mix_out(attn, mpool, x, attn_scale, w_out, g2, g3):
    s = x.shape[0]
    ts = 512

    def body(a_ref, p_ref, x_ref, as_ref, w_ref, g2_ref, g3_ref, ma_ref, mix_ref, x1_ref, h2_ref):
        ao = a_ref[...]
        ma = (ao * _rstd(ao) * as_ref[...]).astype(BF16)
        ma_ref[...] = ma
        mix = _dot(p_ref[...], w_ref[0:D_POOL, :]) + _dot(ma, w_ref[D_POOL:, :])
        mix_ref[...] = mix
        x1 = x_ref[...] + mix * _rstd(mix) * g2_ref[...]
        x1_ref[...] = x1
        h2_ref[...] = (x1 * _rstd(x1) * g3_ref[...]).astype(BF16)

    row = lambda w: pl.BlockSpec((ts, w), lambda i: (i, 0))
    return _call(
        body, name="mix_out",
        out_shape=[jax.ShapeDtypeStruct((s, D_ATTN), BF16), jax.ShapeDtypeStruct((s, D_MODEL), F32),
                   jax.ShapeDtypeStruct((s, D_MODEL), F32), jax.ShapeDtypeStruct((s, D_MODEL), BF16)],
        grid=(s // ts,),
        in_specs=[row(D_ATTN), row(D_POOL), row(D_MODEL), _full((1, D_ATTN)),
                  _full((D_MODEL, D_MODEL)), _full((1, D_MODEL)), _full((1, D_MODEL))],
        out_specs=[row(D_ATTN), row(D_MODEL), row(D_MODEL), row(D_MODEL)],
        compiler_params=_params(("parallel",)))(attn, mpool, x, attn_scale, w_out, g2, g3)


def _earlier_rows(halo, block):
    ts = block.shape[0]
    ext = jnp.concatenate([halo, block], axis=0)
    return tuple(pltpu.roll(ext, d, axis=0)[CONV_HALO:CONV_HALO + ts, :] for d in (1, 2))


def _later_rows(block, halo):
    ts = block.shape[0]
    ext = jnp.concatenate([block, halo], axis=0)
    return tuple(pltpu.roll(ext, ts + CONV_HALO - d, axis=0)[0:ts, :] for d in (1, 2))


def _conv_rows(x0, x1, x2, cw, cb):
    y = cb + cw[0:1, :] * x2
    y = y + cw[1:2, :] * x1
    return y + cw[2:3, :] * x0


def _sigmoid(v):
    return 1.0 / (1.0 + jnp.exp(-v))


def _ffn_up(h2, w_up, conv_w, conv_b):
    s = h2.shape[0]
    ts = 256
    tn = FF_TILE

    def body(h_ref, wg_ref, wv_ref, cwg_ref, cwv_ref, cbg_ref, cbv_ref,
             ug_ref, uv_ref, f_ref, tailg, tailv):
        i = pl.program_id(1)

        @pl.when(i == 0)
        def _():
            tailg[...] = jnp.zeros_like(tailg)
            tailv[...] = jnp.zeros_like(tailv)

        h = h_ref[...]
        ug = _dot(h, wg_ref[...])
        uv = _dot(h, wv_ref[...])
        ug_ref[...] = ug
        uv_ref[...] = uv
        gate = _conv_rows(ug, *_earlier_rows(tailg[...], ug), cwg_ref[...], cbg_ref[...])
        val = _conv_rows(uv, *_earlier_rows(tailv[...], uv), cwv_ref[...], cbv_ref[...])
        f_ref[...] = (gate * _sigmoid(gate) * val).astype(BF16)
        tailg[...] = ug[ts - CONV_HALO:, :]
        tailv[...] = uv[ts - CONV_HALO:, :]

    out_blk = pl.BlockSpec((None, ts, tn), lambda n, i: (n, i, 0))
    act = jax.ShapeDtypeStruct((2, s, tn), F32)
    return _call(
        body, name="ffn_up",
        out_shape=[act, act, jax.ShapeDtypeStruct((2, s, tn), BF16)],
        grid=(2, s // ts),
        in_specs=[pl.BlockSpec((ts, D_MODEL), lambda n, i: (i, 0)),
                  pl.BlockSpec((None, D_MODEL, tn), lambda n, i: (n, 0, 0)),
                  pl.BlockSpec((None, D_MODEL, tn), lambda n, i: (n + 2, 0, 0)),
                  pl.BlockSpec((None, 3, tn), lambda n, i: (n, 0, 0)),
                  pl.BlockSpec((None, 3, tn), lambda n, i: (n + 2, 0, 0)),
                  pl.BlockSpec((None, 1, tn), lambda n, i: (n, 0, 0)),
                  pl.BlockSpec((None, 1, tn), lambda n, i: (n + 2, 0, 0))],
        out_specs=[out_blk, out_blk, out_blk],
        scratch_shapes=[pltpu.VMEM((CONV_HALO, tn), F32), pltpu.VMEM((CONV_HALO, tn), F32)],
        compiler_params=_params(("arbitrary", "arbitrary")))(
            h2, w_up, w_up, conv_w, conv_w, conv_b, conv_b)


def _ffn_down(f_in, w_down, x1, target, g4):
    s = x1.shape[0]
    ts = 512

    def body(f_ref, w_ref, x1_ref, t_ref, g_ref, df_ref, dy_ref, loss_ref, dg_ref):
        @pl.when(pl.program_id(0) == 0)
        def _():
            loss_ref[...] = jnp.zeros_like(loss_ref)
            dg_ref[...] = jnp.zeros_like(dg_ref)

        f = _dot(f_ref[0], w_ref[0:FF_TILE, :]) + _dot(f_ref[1], w_ref[FF_TILE:, :])
        rf = _rstd(f)
        fn = f * rf
        g = g_ref[...]
        err = (x1_ref[...] + fn * g) - t_ref[...]
        loss_ref[...] += 0.5 * jnp.sum(jnp.mean(err * err, axis=-1))
        dy = err * (1.0 / D_MODEL)
        dy_ref[...] = dy
        dg_ref[...] += jnp.sum(dy * fn, axis=0, keepdims=True)
        dfn = dy * g
        df_ref[...] = (rf * (dfn - fn * jnp.mean(dfn * fn, axis=-1, keepdims=True))).astype(BF16)

    row = pl.BlockSpec((ts, D_MODEL), lambda i: (i, 0))
    return _call(
        body, name="ffn_down",
        out_shape=[jax.ShapeDtypeStruct((s, D_MODEL), BF16), jax.ShapeDtypeStruct((s, D_MODEL), F32),
                   jax.ShapeDtypeStruct((8, 128), F32), jax.ShapeDtypeStruct((1, D_MODEL), F32)],
        grid=(s // ts,),
        in_specs=[pl.BlockSpec((2, ts, FF_TILE), lambda i: (0, i, 0)), _full((D_FF, D_MODEL)),
                  row, row, _full((1, D_MODEL))],
        out_specs=[row, row, _full((8, 128)), _full((1, D_MODEL))],
        compiler_params=_params(("arbitrary",)))(f_in, w_down, x1, target, g4)


def _tn_matmul(a, b, name, ts=512):
    na, s, ka = a.shape
    nb, _, nbc = b.shape
    steps = s // ts

    def body(a_ref, b_ref, o_ref, acc_ref):
        @pl.when(pl.program_id(2) == 0)
        def _():
            acc_ref[...] = jnp.zeros_like(acc_ref)

        acc_ref[...] += _dot_tn(a_ref[...].astype(BF16), b_ref[...].astype(BF16))

        @pl.when(pl.program_id(2) == steps - 1)
        def _():
            o_ref[...] = acc_ref[...].astype(BF16)

    return _call(
        body, name=name, out_shape=jax.ShapeDtypeStruct((na, nb, ka, nbc), BF16),
        grid=(na, nb, steps),
        in_specs=[pl.BlockSpec((None, ts, ka), lambda i, j, r: (i, r, 0)),
                  pl.BlockSpec((None, ts, nbc), lambda i, j, r: (j, r, 0))],
        out_specs=pl.BlockSpec((None, None, ka, nbc), lambda i, j, r: (i, j, 0, 0)),
        scratch_shapes=[pltpu.VMEM((ka, nbc), F32)],
        compiler_params=_params(("parallel", "parallel", "arbitrary")))(a, b)


def _ffn_bwd_act(df, w_down, upre_g, upre_v, conv_w, conv_b):
    s = df.shape[0]
    ts = 256
    tn = FF_TILE
    nr = s // ts
    per = ts // CONV_HALO

    def body(df_ref, wd_ref, ug_ref, uv_ref, hg_ref, hv_ref, cwg_ref, cwv_ref, cbg_ref, cbv_ref,
             dug_ref, duv_ref, dcwg_ref, dcwv_ref, dcbg_ref, dcbv_ref, headg, headv):
        i = pl.program_id(1)
        first_rows = i == nr - 1

        @pl.when(i == 0)
        def _():
            for r in (headg, headv, dcwg_ref, dcwv_ref, dcbg_ref, dcbv_ref):
                r[...] = jnp.zeros_like(r)

        ug, uv = ug_ref[...], uv_ref[...]
        cwg, cwv = cwg_ref[...], cwv_ref[...]
        gate = _conv_rows(ug, *_earlier_rows(jnp.where(first_rows, 0.0, hg_ref[...]), ug), cwg, cbg_ref[...])
        val = _conv_rows(uv, *_earlier_rows(jnp.where(first_rows, 0.0, hv_ref[...]), uv), cwv, cbv_ref[...])
        sg = _sigmoid(gate)
        dfin = _dot_nt(df_ref[...], wd_ref[...])
        dval = dfin * (gate * sg)
        dgate = dfin * val * (sg * (1.0 + gate * (1.0 - sg)))

        def conv_bwd(dact, x, head, cw, dcw_ref, dcb_ref, du_ref):
            d1, d2 = _later_rows(dact, head[...])
            dcb_ref[...] += jnp.sum(dact, axis=0, keepdims=True)
            for kk, shifted in enumerate((d2, d1, dact)):
                dcw_ref[kk:kk + 1, :] += jnp.sum(x * shifted, axis=0, keepdims=True)
            du_ref[...] = (cw[2:3, :] * dact + cw[1:2, :] * d1 + cw[0:1, :] * d2).astype(BF16)
            head[...] = dact[0:CONV_HALO, :]

        conv_bwd(dgate, ug, headg, cwg, dcwg_ref, dcbg_ref, dug_ref)
        conv_bwd(dval, uv, headv, cwv, dcwv_ref, dcbv_ref, duv_ref)

    rows = lambda n, i: (n, nr - 1 - i, 0)
    halo = lambda n, i: (n, jnp.maximum((nr - 1 - i) * per - 1, 0), 0)
    act_blk = pl.BlockSpec((None, ts, tn), rows)
    halo_blk = pl.BlockSpec((None, CONV_HALO, tn), halo)
    cw_blk = lambda off: pl.BlockSpec((None, 3, tn), lambda n, i: (n + off, 0, 0))
    cb_blk = lambda off: pl.BlockSpec((None, 1, tn), lambda n, i: (n + off, 0, 0))
    acc_w = pl.BlockSpec((None, 3, tn), lambda n, i: (n, 0, 0))
    acc_b = pl.BlockSpec((None, 1, tn), lambda n, i: (n, 0, 0))
    dact = jax.ShapeDtypeStruct((2, s, tn), BF16)
    return _call(
        body, name="ffn_bwd_act",
        out_shape=[dact, dact, jax.ShapeDtypeStruct((2, 3, tn), F32), jax.ShapeDtypeStruct((2, 3, tn), F32),
                   jax.ShapeDtypeStruct((2, 1, tn), F32), jax.ShapeDtypeStruct((2, 1, tn), F32)],
        grid=(2, nr),
        in_specs=[pl.BlockSpec((ts, D_MODEL), lambda n, i: (nr - 1 - i, 0)),
                  pl.BlockSpec((tn, D_MODEL), lambda n, i: (n, 0)),
                  act_blk, act_blk, halo_blk, halo_blk,
                  cw_blk(0), cw_blk(2), cb_blk(0), cb_blk(2)],
        out_specs=[act_blk, act_blk, acc_w, acc_w, acc_b, acc_b],
        scratch_shapes=[pltpu.VMEM((CONV_HALO, tn), F32)] * 2,
        compiler_params=_params(("arbitrary", "arbitrary")))(
            df, w_down, upre_g, upre_v, upre_g, upre_v, conv_w, conv_w, conv_b, conv_b)


def _ffn_bwd_in(dug, duv, w_up, x1, dy, mix, g3, g2):
    s = x1.shape[0]
    ts = 256

    def body(dg_ref, dv_ref, w_ref, x1_ref, dy_ref, mix_ref, g3_ref, g2_ref,
             dx1_ref, dmix_ref, dg3_ref, dg2_ref):
        @pl.when(pl.program_id(0) == 0)
        def _():
            dg3_ref[...] = jnp.zeros_like(dg3_ref)
            dg2_ref[...] = jnp.zeros_like(dg2_ref)

        dh = _dot_nt(dg_ref[0], w_ref[0]) + _dot_nt(dg_ref[1], w_ref[1])
        dh = dh + _dot_nt(dv_ref[0], w_ref[2]) + _dot_nt(dv_ref[1], w_ref[3])
        x1 = x1_ref[...]
        r3 = _rstd(x1)
        xn = x1 * r3
        dg3_ref[...] += jnp.sum(dh * xn, axis=0, keepdims=True)
        dxn = dh * g3_ref[...]
        dx1 = dy_ref[...] + r3 * (dxn - xn * jnp.mean(dxn * xn, axis=-1, keepdims=True))
        dx1_ref[...] = dx1
        mix = mix_ref[...]
        rm = _rstd(mix)
        mn = mix * rm
        dg2_ref[...] += jnp.sum(dx1 * mn, axis=0, keepdims=True)
        dmn = dx1 * g2_ref[...]
        dmix_ref[...] = (rm * (dmn - mn * jnp.mean(dmn * mn, axis=-1, keepdims=True))).astype(BF16)

    row = pl.BlockSpec((ts, D_MODEL), lambda i: (i, 0))
    act = pl.BlockSpec((2, ts, FF_TILE), lambda i: (0, i, 0))
    vec = _full((1, D_MODEL))
    return _call(
        body, name="ffn_bwd_in",
        out_shape=[jax.ShapeDtypeStruct((s, D_MODEL), F32), jax.ShapeDtypeStruct((s, D_MODEL), BF16),
                   jax.ShapeDtypeStruct((1, D_MODEL), F32), jax.ShapeDtypeStruct((1, D_MODEL), F32)],
        grid=(s // ts,),
        in_specs=[act, act, _full(w_up.shape), row, row, row, vec, vec],
        out_specs=[row, row, vec, vec],
        compiler_params=_params(("arbitrary",), vmem_mb=56))(dug, duv, w_up, x1, dy, mix, g3, g2)


def _mix_bwd(dmix, w_out, attn, attn_scale):
    s = dmix.shape[0]
    ts = 512

    def body(dm_ref, w_ref, a_ref, as_ref, dp_ref, do_ref, das_ref):
        @pl.when(pl.program_id(0) == 0)
        def _():
            das_ref[...] = jnp.zeros_like(das_ref)

        dm = dm_ref[...]
        dp_ref[...] = _dot_nt(dm, w_ref[0:D_POOL, :])
        da = _dot_nt(dm, w_ref[D_POOL:, :])
        ao = a_ref[...]
        ra = _rstd(ao)
        an = ao * ra
        das_ref[...] += jnp.sum(da * an, axis=0, keepdims=True)
        dan = da * as_ref[...]
        do_ref[...] = (ra * (dan - an * jnp.mean(dan * an, axis=-1, keepdims=True))).astype(BF16)

    row = lambda w: pl.BlockSpec((ts, w), lambda i: (i, 0))
    return _call(
        body, name="mix_bwd",
        out_shape=[jax.ShapeDtypeStruct((s, D_POOL), F32), jax.ShapeDtypeStruct((s, D_ATTN), BF16),
                   jax.ShapeDtypeStruct((1, D_ATTN), F32)],
        grid=(s // ts,),
        in_specs=[row(D_MODEL), _full((D_MODEL, D_MODEL)), row(D_ATTN), _full((1, D_ATTN))],
        out_specs=[row(D_POOL), row(D_ATTN), _full((1, D_ATTN))],
        compiler_params=_params(("arbitrary",)))(dmix, w_out, attn, attn_scale)


def _attn_bwd(q, k, v, do, totals):
    s = q.shape[0]
    tb = ATT_BLOCK
    nq = s // tb

    def body(q_ref, do_ref, t_ref, k_hbm, v_hbm, dq_ref, dk_hbm, dv_hbm,
             k_scr, v_scr, dkt_acc, dvt_acc, stage, *bufs):
        hp = pl.program_id(0)
        i = pl.program_id(1)
        lanes = pl.ds(pl.multiple_of(hp * HEAD_PAIR, HEAD_PAIR), HEAD_PAIR)

        @pl.when(i == 0)
        def _():
            pltpu.sync_copy(k_hbm.at[:, lanes], k_scr)
            pltpu.sync_copy(v_hbm.at[:, lanes], v_scr)
            dkt_acc[...] = jnp.zeros_like(dkt_acc)
            dvt_acc[...] = jnp.zeros_like(dvt_acc)

        upper = _tri("suffix")
        lower = _tri("prefix")
        causal = _causal_mask()
        lane = lax.broadcasted_iota(jnp.int32, (1, HEAD_PAIR), 1)
        first = lane < 64
        q2 = q_ref[...]
        do2 = do_ref[...]
        zero = jnp.zeros_like(q2)
        qs = (jnp.where(first, q2, zero), jnp.where(first, zero, q2))
        dos = (jnp.where(first, do2, zero), jnp.where(first, zero, do2))
        qcat_t = jnp.concatenate(qs, axis=0).astype(F32).T.astype(BF16)
        docat_t = jnp.concatenate(dos, axis=0).astype(F32).T.astype(BF16)
        tots = (t_ref[:, 0:1], t_ref[:, 1:2])

        z_ring, sg_ring = [[[bufs[8 * kind + 2 * slot + e] for e in range(2)] for slot in range(4)]
                           for kind in range(2)]
        in_buf, da_buf, dw_buf, pre_buf = [
            [[bufs[16 + 4 * kind + 2 * slot + e] for e in range(2)] for slot in range(2)]
            for kind in range(4)]
        pr_buf, dzr_buf, dzc_buf = bufs[32:34], bufs[34:36], bufs[36:38]

        for e in range(2):
            for slot in (2, 3):
                z_ring[slot][e][...] = jnp.full((tb, tb), NEG_BIG, F32)
            for slot in (1, 2, 3):
                sg_ring[slot][e][...] = jnp.zeros((tb, tb), F32)
            for buf in (in_buf, da_buf, dw_buf, pre_buf):
                buf[1][e][...] = jnp.zeros((tb, tb), F32)
        for buf in (pr_buf, dzr_buf, dzc_buf):
            buf[1][...] = jnp.zeros_like(buf[1])

        def rows(p):
            return pl.ds(pl.multiple_of(jnp.clip(p, 0, nq - 1) * tb, tb), tb)

        def split_heads(block):
            return jnp.concatenate([jnp.where(first, block, zero), jnp.where(first, zero, block)], axis=0)

        def trip(t, u, carry):
            w, r = u % 2, 1 - u % 2
            cs, cps, dq = list(carry[0:2]), list(carry[2:4]), carry[4]
            live4 = (t - 4 >= 0) & (t - 4 < i)
            dq = dq + jnp.where(live4, _dot(dzc_buf[r][...], split_heads(k_scr[rows(t - 4), :])), 0.0)
            dkt_acc[jnp.clip(t - 4, 0, nq - 1)] += jnp.where(live4, _dot(qcat_t, dzr_buf[r][...]), 0.0)
            dvt_acc[jnp.clip(t - 3, 0, nq - 1)] += _dot(docat_t, pr_buf[r][...])
            kj = k_scr[rows(t), :]
            for e in range(2):
                z_ring[u][e][...] = _dot_nt(qs[e], kj)
            vj = v_scr[rows(t - 1), :]
            for e in range(2):
                sp, sig = _softplus(z_ring[(u - 1) % 4][e][...], True)
                sg_ring[(u - 1) % 4][e][...] = sig
                in_buf[w][e][...] = _dot(sp.astype(BF16), upper)
                da_buf[w][e][...] = _dot_nt(dos[e], vj)
            for e in range(2):
                incl = in_buf[r][e][...]
                cs[e] = cs[e] + incl[:, 0:1]
                off = jnp.where(t - 2 < i, tots[e] - cs[e], -NEG_BIG)
                a = jnp.exp(z_ring[(u - 2) % 4][e][...] - incl - off)
                dw = a * da_buf[r][e][...]
                dw_buf[w][e][...] = dw
                pr_buf[w][e * tb:(e + 1) * tb, :] = a.astype(BF16)
                pre_buf[w][e][...] = _dot(dw.astype(BF16), lower)
            for e in range(2):
                pre = pre_buf[r][e][...] + cps[e]
                dzb = (dw_buf[r][e][...] - sg_ring[(u - 3) % 4][e][...] * pre).astype(BF16)
                cps[e] = pre[:, tb - 1:tb]
                dzr_buf[w][e * tb:(e + 1) * tb, :] = dzb
                dzc_buf[w][:, e * tb:(e + 1) * tb] = dzb
            return cs[0], cs[1], cps[0], cps[1], dq

        col = jnp.zeros((tb, 1), F32)
        carry = (col, col, col, col, jnp.zeros((tb, HEAD_PAIR), F32))
        def four_trips(n, cr):
            for u in range(4):
                cr = trip(4 * n + u, u, cr)
            return cr

        carry = lax.fori_loop(0, jnp.where(i > 0, (i + 7) // 4, 0), four_trips, carry)

        cps, dq = carry[2:4], carry[4]
        kj = k_scr[rows(i), :]
        vj = v_scr[rows(i), :]
        dzs, probs = [], []
        for e in range(2):
            z = _dot_nt(qs[e], kj)
            sp, sig = _softplus(z, True)
            incl = _dot(jnp.where(causal, sp, 0.0).astype(BF16), upper)
            a = jnp.where(causal, jnp.exp(z - incl), 0.0)
            dw = a * _dot_nt(dos[e], vj)
            pre = _dot(dw.astype(BF16), lower) + cps[e]
            dzs.append(jnp.where(causal, dw - sig * pre, 0.0).astype(BF16))
            probs.append(a.astype(BF16))
        dq = dq + _dot(jnp.concatenate(dzs, axis=1), split_heads(kj))
        dkt_acc[i] += _dot(qcat_t, jnp.concatenate(dzs, axis=0))
        dvt_acc[i] += _dot(docat_t, jnp.concatenate(probs, axis=0))
        dq_ref[...] = (dq * Q_SCALE).astype(BF16)

        @pl.when(i == nq - 1)
        def _():
            for acc, dst in ((dkt_acc, dk_hbm), (dvt_acc, dv_hbm)):
                def flip(n, _, acc=acc):
                    at = pl.ds(pl.multiple_of(n * tb, tb), tb)
                    stage[at, :] = acc[n].T
                    return 0
                lax.fori_loop(0, nq, flip, 0)
                pltpu.sync_copy(stage, dst.at[:, lanes])

    blk = pl.BlockSpec((tb, HEAD_PAIR), lambda h, i: (i, h))
    grad = jax.ShapeDtypeStruct((s, D_ATTN), F32)
    return _call(
        body, name="attn_bwd",
        out_shape=[jax.ShapeDtypeStruct((s, D_ATTN), BF16), grad, grad],
        grid=(4, nq),
        in_specs=[blk, blk, pl.BlockSpec((None, tb, 2), lambda h, i: (h, i, 0)), ANY, ANY],
        out_specs=[blk, ANY, ANY],
        scratch_shapes=[pltpu.VMEM((s, HEAD_PAIR), BF16), pltpu.VMEM((s, HEAD_PAIR), BF16),
                        pltpu.VMEM((nq, HEAD_PAIR, tb), F32), pltpu.VMEM((nq, HEAD_PAIR, tb), F32),
                        pltpu.VMEM((s, HEAD_PAIR), F32)]
        + [pltpu.VMEM((tb, tb), F32)] * 32
        + [pltpu.VMEM((2 * tb, tb), BF16)] * 4 + [pltpu.VMEM((tb, 2 * tb), BF16)] * 2,
        compiler_params=_params(("arbitrary", "arbitrary"), vmem_mb=60))(q, do, totals, k, v)


def _pool_bwd(u, dmp, w_pool, pool_scale):
    s = u.shape[0]
    ts = 512
    nr = s // ts
    per = ts // POOL_HALO

    def body(u_ref, halo_ref, dm_ref, wp_ref, ps_ref, du_ref, dwp_ref, dps_ref, ext_ref, y_ref, dext_ref):
        i = pl.program_id(0)
        rb = nr - 1 - i

        @pl.when(i == 0)
        def _():
            dext_ref[ts:, :] = jnp.zeros((POOL_HALO, D_POOL), F32)
            dwp_ref[...] = jnp.zeros_like(dwp_ref)
            dps_ref[...] = jnp.zeros_like(dps_ref)

        ext_ref[0:POOL_HALO, :] = jnp.where(rb > 0, halo_ref[...], 0.0)
        ext_ref[POOL_HALO:, :] = u_ref[...]
        ps, cnts = [], []
        for g, window in enumerate(POOL_WINDOWS):
            p, cnt = _pool_means(ext_ref, g, window, ts, rb * ts)
            ps.append(p.astype(BF16))
            cnts.append(cnt)
            y_ref[:, g * POOL_GROUP:(g + 1) * POOL_GROUP] = _dot(ps[g], wp_ref[g].astype(BF16))
        y = y_ref[...]
        r = _rstd(y)
        yn = y * r
        dm = dm_ref[...]
        dps_ref[...] += jnp.sum(dm * yn, axis=0, keepdims=True)
        dn = dm * ps_ref[...]
        dy = r * (dn - yn * jnp.mean(dn * yn, axis=-1, keepdims=True))
        for g, window in enumerate(POOL_WINDOWS):
            cols = slice(g * POOL_GROUP, (g + 1) * POOL_GROUP)
            dyg = dy[:, cols].astype(BF16)
            dwp_ref[g] += _dot_tn(ps[g], dyg)
            dp = _dot_nt(dyg, wp_ref[g].astype(BF16))
            dext_ref[0:ts, cols] = dp / cnts[g]
            acc = dext_ref[0:ts, cols]
            for d in range(1, window):
                acc = acc + dext_ref[d:d + ts, cols]
            du_ref[:, cols] = (acc - dp).astype(BF16)
        dext_ref[ts:, :] = dext_ref[0:POOL_HALO, :]

    rows = pl.BlockSpec((ts, D_POOL), lambda i: (nr - 1 - i, 0))
    return _call(
        body, name="pool_bwd",
        out_shape=[jax.ShapeDtypeStruct((s, D_POOL), BF16), jax.ShapeDtypeStruct(w_pool.shape, F32),
                   jax.ShapeDtypeStruct((1, D_POOL), F32)],
        grid=(nr,),
        in_specs=[rows,
                  pl.BlockSpec((POOL_HALO, D_POOL), lambda i: (jnp.maximum((nr - 1 - i) * per - 1, 0), 0)),
                  rows, _full(w_pool.shape), _full((1, D_POOL))],
        out_specs=[rows, _full(w_pool.shape), _full((1, D_POOL))],
        scratch_shapes=[pltpu.VMEM((ts + POOL_HALO, D_POOL), F32), pltpu.VMEM((ts, D_POOL), F32),
                        pltpu.VMEM((ts + POOL_HALO, D_POOL), F32)],
        compiler_params=_params(("arbitrary",)))(u, u, dmp, w_pool, pool_scale)


def _in_proj_bwd(du, dq, dk, dv, w_in, x, dx1, g1):
    s = x.shape[0]
    ts = 512

    def body(du_ref, dq_ref, dk_ref, dv_ref, w_ref, x_ref, dx1_ref, g_ref, gx_ref, dg_ref):
        @pl.when(pl.program_id(0) == 0)
        def _():
            dg_ref[...] = jnp.zeros_like(dg_ref)

        dh = _dot_nt(du_ref[...], w_ref[0]) + _dot_nt(dq_ref[...], w_ref[1])
        dh = dh + _dot_nt(dk_ref[...].astype(BF16), w_ref[2]) + _dot_nt(dv_ref[...].astype(BF16), w_ref[3])
        xv = x_ref[...]
        r = _rstd(xv)
        xn = xv * r
        dg_ref[...] += jnp.sum(dh * xn, axis=0, keepdims=True)
        dxn = dh * g_ref[...]
        gx_ref[...] = dx1_ref[...] + r * (dxn - xn * jnp.mean(dxn * xn, axis=-1, keepdims=True))

    row = lambda w: pl.BlockSpec((ts, w), lambda i: (i, 0))
    return _call(
        body, name="in_proj_bwd",
        out_shape=[jax.ShapeDtypeStruct((s, D_MODEL), F32), jax.ShapeDtypeStruct((1, D_MODEL), F32)],
        grid=(s // ts,),
        in_specs=[row(D_POOL)] * 4 + [_full(w_in.shape), row(D_MODEL), row(D_MODEL), _full((1, D_MODEL))],
        out_specs=[row(D_MODEL), _full((1, D_MODEL))],
        compiler_params=_params(("arbitrary",)))(du, dq, dk, dv, w_in, x, dx1, g1)


_SMALL = ("norm_mix_pre", "w_pool", "pool_scale", "attn_scale", "norm_mix_post",
          "norm_ffn_pre", "conv_b", "norm_ffn_post")
_SMALL_SIZE = {"norm_mix_pre": 1024, "w_pool": 65536, "pool_scale": 512, "attn_scale": 512,
               "norm_mix_post": 1024, "norm_ffn_pre": 1024, "conv_b": 5632, "norm_ffn_post": 1024}
_SMALL_ROWS = 600
_CONVW_ROWS = 132
_PACK_ROWS = _SMALL_ROWS + _CONVW_ROWS + 4


def _pack_small(parts):
    flat = jnp.concatenate([parts[n].reshape(-1) for n in _SMALL])
    flat = jnp.pad(flat, (0, _SMALL_ROWS * 128 - flat.shape[0]))
    return flat.reshape(_SMALL_ROWS, 128)


def _unpack_small(packed, like):
    flat = packed.reshape(-1)
    out, off = {}, 0
    for n in _SMALL:
        out[n] = flat[off:off + _SMALL_SIZE[n]].reshape(like[n].shape)
        off += _SMALL_SIZE[n]
    return out


def kernel(x, norm_mix_pre, w_in, w_pool, pool_scale, attn_scale, w_out, norm_mix_post, norm_ffn_pre, w_up, conv_w, conv_b, w_down, norm_ffn_post, loss_target, m_norm_mix_pre, m_w_in, m_w_pool, m_pool_scale, m_attn_scale, m_w_out, m_norm_mix_post, m_norm_ffn_pre, m_w_up, m_conv_w, m_conv_b, m_w_down, m_norm_ffn_post, v_norm_mix_pre, v_w_in, v_w_pool, v_pool_scale, v_attn_scale, v_w_out, v_norm_mix_post, v_norm_ffn_pre, v_w_up, v_conv_w, v_conv_b, v_w_down, v_norm_ffn_post):
    weights = dict(norm_mix_pre=norm_mix_pre, w_in=w_in, w_pool=w_pool, pool_scale=pool_scale,
                   attn_scale=attn_scale, w_out=w_out, norm_mix_post=norm_mix_post,
                   norm_ffn_pre=norm_ffn_pre, w_up=w_up, conv_w=conv_w, conv_b=conv_b,
                   w_down=w_down, norm_ffn_post=norm_ffn_post)
    mom1 = dict(norm_mix_pre=m_norm_mix_pre, w_in=m_w_in, w_pool=m_w_pool, pool_scale=m_pool_scale,
                attn_scale=m_attn_scale, w_out=m_w_out, norm_mix_post=m_norm_mix_post,
                norm_ffn_pre=m_norm_ffn_pre, w_up=m_w_up, conv_w=m_conv_w, conv_b=m_conv_b,
                w_down=m_w_down, norm_ffn_post=m_norm_ffn_post)
    mom2 = dict(norm_mix_pre=v_norm_mix_pre, w_in=v_w_in, w_pool=v_w_pool, pool_scale=v_pool_scale,
                attn_scale=v_attn_scale, w_out=v_w_out, norm_mix_post=v_norm_mix_post,
                norm_ffn_pre=v_norm_ffn_pre, w_up=v_w_up, conv_w=v_conv_w, conv_b=v_conv_b,
                w_down=v_w_down, norm_ffn_post=v_norm_ffn_post)
    order = list(weights)

    xs = x[0]
    target = loss_target[0]
    wp = w_pool[0]
    shard = lax.axis_index("x") * 2 + lax.axis_index("y")

    slot = shard.astype(jnp.int32).reshape(1)
    win_g, = _gather_shards([_cast_bf16(w_in[0], "cast_w_in")])
    lands = [_into_slot(w_out[0], slot, BF16, "cast_w_out"), _into_slot(w_up[0], slot, BF16, "cast_w_up"),
             _into_slot(w_down[0], slot, BF16, "cast_w_down"), _into_slot(conv_w[0], slot, F32, "place_conv_w")]
    g_send, g_recv, g_lands, g_token = _exchange_start(None, lands, win_g, "gather_start")
    convb_g = conv_b[0].reshape(N_SHARD, 1, FF_TILE)

    u, q, k, v, vt, h1 = _in_proj(xs, norm_mix_pre + g_token[0:1, 0:1], win_g)
    mpool = _pool_fwd(u, wp, pool_scale)
    attn, totals = _attn_fwd(q, k, vt)
    _, (wout_g, wup_g, wdown_g, convw_g) = _exchange_wait(g_send, g_recv, g_lands, False, attn, "gather_wait")
    wout_f = wout_g.reshape(D_MODEL, D_MODEL)
    wdown_f = wdown_g.reshape(D_FF, D_MODEL)
    mattn, mix, x1, h2 = _mix_out(attn, mpool, xs, attn_scale, wout_f, norm_mix_post, norm_ffn_pre)
    upre_g, upre_v, f_in = _ffn_up(h2, wup_g, convw_g, convb_g)
    df, dy, loss_tile, d_post = _ffn_down(f_in, wdown_f, x1, target, norm_ffn_post)

    d_wdown = _tn_matmul(f_in, df[None], "dw_down")
    dug, duv, dcw_g, dcw_v, dcb_g, dcb_v = _ffn_bwd_act(df, wdown_f, upre_g, upre_v, convw_g, convb_g)
    d_wup = jnp.concatenate([_tn_matmul(h2[None], dug, "dw_up_gate")[0],
                             _tn_matmul(h2[None], duv, "dw_up_value")[0]], axis=0)
    early = [d_wup, d_wdown.reshape(N_SHARD, D_FF // N_SHARD, D_MODEL)]
    s_send, s_recv, s_thru, s_token = _exchange_start(
        early, [lax.empty((3,) + g.shape[1:], g.dtype) for g in early], d_wup, "scatter_start")
    dx1, dmix, d_ffn_pre, d_mix_post = _ffn_bwd_in(
        dug, duv, wup_g, x1, dy, mix, norm_ffn_pre + s_token[0:1, 0:1], norm_mix_post)
    d_wout = jnp.concatenate([_tn_matmul(mpool[None], dmix[None], "dw_out_pool")[0, 0],
                              _tn_matmul(mattn[None], dmix[None], "dw_out_attn")[0, 0]], axis=0)
    dmp, do, d_attn_scale = _mix_bwd(dmix, wout_f, attn, attn_scale)
    dq, dk, dv = _attn_bwd(q, k, v, do, totals)
    du, d_wpool, d_pool_scale = _pool_bwd(u, dmp, wp, pool_scale)
    d_win = jnp.stack([_tn_matmul(h1[None], t[None], "dw_in_%d" % n)[0, 0]
                       for n, t in enumerate((du, dq, dk, dv))])
    grad_x, d_mix_pre = _in_proj_bwd(du, dq, dk, dv, win_g, xs, dx1, norm_mix_pre)

    d_convw = jnp.concatenate([dcw_g, dcw_v], axis=0)
    d_convb = jnp.concatenate([dcb_g, dcb_v], axis=0).reshape(1, 2 * D_FF)
    small_parts = dict(norm_mix_pre=d_mix_pre, w_pool=d_wpool, pool_scale=d_pool_scale,
                       attn_scale=d_attn_scale, norm_mix_post=d_mix_post, norm_ffn_pre=d_ffn_pre,
                       conv_b=d_convb, norm_ffn_post=d_post)
    packed = jnp.concatenate([_pack_small(small_parts), d_convw.reshape(_CONVW_ROWS, 128),
                              loss_tile[0:4]], axis=0)
    late, gathered = _scatter_grads([d_win, d_wout.reshape(N_SHARD, D_MODEL // N_SHARD, D_MODEL)], packed)
    early_srcs, early_lands = _exchange_wait(s_send, s_recv, s_thru, True, grad_x, "scatter_wait")
    quarter = [_sum_slots(r, (3, 0, 1, 2), "sum_chips_%d" % n) for n, r in enumerate(late)]
    quarter += [_sum_own_and_received(early_srcs[n], slot, early_lands[n], "sum_chips_%d" % (n + 2))
                for n in range(2)]
    sibling = _swap_with_sibling(quarter)
    small_sum = _sum_slots(gathered, tuple(range(8)), "sum_small")

    results = {}
    for n, name in enumerate(("w_in", "w_out", "w_up", "w_down")):
        res = _adamw([quarter[n], sibling[n]], weights[name][0], mom1[name][0], mom2[name][0],
                     "adamw_" + name)
        results[name] = [t[None] for t in res]
    g_convw = lax.dynamic_slice_in_dim(
        small_sum[_SMALL_ROWS:_SMALL_ROWS + _CONVW_ROWS].reshape(N_SHARD, 3, FF_TILE), shard, 1, axis=0)[0]
    convw_pad = lambda t: jnp.pad(t, ((0, 5), (0, 0)))
    res = _adamw([convw_pad(g_convw)], convw_pad(conv_w[0]), convw_pad(m_conv_w[0]),
                 convw_pad(v_conv_w[0]), "adamw_conv_w")
    results["conv_w"] = [t[:3][None] for t in res]
    pack_w = _pack_small(weights)
    pack_m = _pack_small(mom1)
    pack_v = _pack_small(mom2)
    res = _adamw([small_sum[:_SMALL_ROWS]], pack_w, pack_m, pack_v, "adamw_small")
    unpacked = [_unpack_small(t, weights) for t in res]
    for name in _SMALL:
        results[name] = [t[name] for t in unpacked]

    loss = small_sum[_SMALL_ROWS + _CONVW_ROWS, 0]
    outs = [loss, grad_x[None]]
    for slot in range(4):
        outs.extend(results[name][slot] for name in order)
    return tuple(outs)
```

```python
import functools

import jax
import jax.numpy as jnp
from jax import lax
from jax.experimental import pallas as pl
from jax.experimental.pallas import tpu as pltpu

F32 = jnp.float32
BF16 = jnp.bfloat16

D_MODEL = 1024
D_POOL = 512
D_ATTN = 512
POOL_WINDOWS = (2, 4, 8, 16)
POOL_GROUP = 128
POOL_HALO = 16
CONV_HALO = 8
D_FF = 2816
FF_TILE = 1408
N_SHARD = 4
EPS = 1e-6
Q_SCALE = 0.125
ATT_BLOCK = 256
HEAD_PAIR = 128
MIB = 1 << 20
NEG_BIG = -1e30

ADAM_LR = 0.001
ADAM_B1 = 0.9
ADAM_B2 = 0.999
ADAM_EPS = 1e-08
ADAM_WD = 0.01
ADAM_STEP = 10

NT_DIMS = (((1,), (1,)), ((), ()))
TN_DIMS = (((0,), (0,)), ((), ()))
MESH = pl.DeviceIdType.MESH
ANY = pl.BlockSpec(memory_space=pl.ANY)
HBM_SPEC = pl.BlockSpec(memory_space=pltpu.HBM)
SEM_SPEC = pl.BlockSpec(memory_space=pltpu.SEMAPHORE)
DATAFLOW = pltpu.SideEffectType.DATAFLOW_SIDE_EFFECTING


def _call(body, **kw):
    return pl.pallas_call(body, **kw)


def _params(sem=None, vmem_mb=48):
    return pltpu.CompilerParams(dimension_semantics=sem, vmem_limit_bytes=vmem_mb * MIB)


def _rstd(v):
    return lax.rsqrt(jnp.mean(v * v, axis=-1, keepdims=True) + EPS)


def _dot(a, b):
    return jnp.dot(a, b, preferred_element_type=F32)


def _dot_nt(a, b):
    return lax.dot_general(a, b, NT_DIMS, preferred_element_type=F32)


def _dot_tn(a, b):
    return lax.dot_general(a, b, TN_DIMS, preferred_element_type=F32)


def _row_tile(rows, cap):
    t = min(rows, cap)
    t -= t % 8
    while rows % t:
        t -= 8
    return t


def _full(shape):
    nd = len(shape)
    return pl.BlockSpec(shape, lambda *_: (0,) * nd)


def _chip_peers():
    x, y, c = lax.axis_index("x"), lax.axis_index("y"), lax.axis_index("c")
    return x, y, c, [(1 - x, y), (x, 1 - y), (1 - x, 1 - y)]


def _cast_bf16(a, name):
    def body(a_ref, o_ref):
        o_ref[...] = a_ref[...].astype(BF16)

    return _call(body, name=name, out_shape=jax.ShapeDtypeStruct(a.shape, BF16),
                 grid=(1,), in_specs=[_full(a.shape)], out_specs=_full(a.shape),
                 compiler_params=_params(("arbitrary",)))(a)


def _into_slot(a, slot, dtype, name):
    nd = a.ndim

    def body(slot_ref, a_ref, o_ref):
        o_ref[...] = a_ref[...].astype(dtype)

    return _call(
        body, name=name, out_shape=jax.ShapeDtypeStruct((N_SHARD,) + a.shape, dtype),
        grid_spec=pltpu.PrefetchScalarGridSpec(
            num_scalar_prefetch=1, grid=(1,),
            in_specs=[pl.BlockSpec(a.shape, lambda i, slot_ref: (0,) * nd)],
            out_specs=pl.BlockSpec((None,) + a.shape, lambda i, slot_ref: (slot_ref[0],) + (0,) * nd)),
        compiler_params=_params(("arbitrary",)))(slot, a)


def _exchange_copies(srcs, lands, send, recv):
    x, y, c, chips = _chip_peers()
    copies = []
    for t in range(len(lands)):
        for k, (px, py) in enumerate(chips):
            copies.append(pltpu.make_async_remote_copy(
                src_ref=lands[t].at[2 * x + y] if srcs is None else srcs[t].at[2 * px + py],
                dst_ref=lands[t].at[2 * x + y] if srcs is None else lands[t].at[k],
                send_sem=send.at[3 * t + k], recv_sem=recv.at[3 * t + k],
                device_id=(px, py, c), device_id_type=MESH))
    return copies


def _exchange_start(srcs, lands, after, name):
    n = len(lands)
    operands = list(lands) if srcs is None else list(srcs) + list(lands)
    m = len(operands)

    def body(*refs):
        for cp in _exchange_copies(None if srcs is None else refs[:n], refs[m - n:m], refs[m + 1], refs[m + 2]):
            cp.start()
        refs[-1][...] = jnp.zeros_like(refs[-1])

    res = _call(
        body, name=name,
        out_shape=[pltpu.SemaphoreType.DMA((3 * n,)), pltpu.SemaphoreType.DMA((3 * n,))]
        + [pltpu.HBM(a.shape, a.dtype) for a in operands] + [jax.ShapeDtypeStruct((8, 128), F32)],
        in_specs=[HBM_SPEC] * m + [ANY],
        out_specs=[SEM_SPEC, SEM_SPEC] + [HBM_SPEC] * m + [pl.BlockSpec(memory_space=pltpu.VMEM)],
        input_output_aliases={j: j + 2 for j in range(m)},
        compiler_params=pltpu.CompilerParams(has_side_effects=DATAFLOW),
    )(*[pltpu.with_memory_space_constraint(a, pltpu.HBM) for a in operands], after)
    return res[0], res[1], res[2:2 + m], res[-1]


def _exchange_wait(send, recv, operands, scatter, after, name):
    m = len(operands)
    n = m // 2 if scatter else m

    def body(*refs):
        for cp in _exchange_copies(refs[:n] if scatter else None, refs[m - n:m], refs[m], refs[m + 1]):
            cp.wait_send()
            cp.wait_recv()

    res = _call(
        body, name=name, out_shape=[pltpu.HBM(a.shape, a.dtype) for a in operands],
        in_specs=[HBM_SPEC] * m + [SEM_SPEC, SEM_SPEC, ANY], out_specs=[HBM_SPEC] * m,
        input_output_aliases={j: j for j in range(m)},
        compiler_params=pltpu.CompilerParams(has_side_effects=DATAFLOW),
    )(*operands, send, recv, after)
    return res[:m - n], res[m - n:]


def _gather_shards(shards):
    n = len(shards)

    def body(*refs):
        ins, outs = refs[:n], refs[n:2 * n]
        send, recv, loc = refs[2 * n:]
        x, y, c, chips = _chip_peers()
        b = 2 * x + y
        local = [pltpu.make_async_copy(ins[t], outs[t].at[b], loc.at[t]) for t in range(n)]
        for cp in local:
            cp.start()
        remote = []
        for t in range(n):
            for k, (px, py) in enumerate(chips):
                remote.append(pltpu.make_async_remote_copy(
                    src_ref=ins[t], dst_ref=outs[t].at[b],
                    send_sem=send.at[3 * t + k], recv_sem=recv.at[3 * t + k],
                    device_id=(px, py, c), device_id_type=MESH))
        for cp in remote:
            cp.start()
        for cp in remote:
            cp.wait()
        for cp in local:
            cp.wait()

    return _call(
        body, name="gather_w_in",
        out_shape=[jax.ShapeDtypeStruct((N_SHARD,) + s.shape, s.dtype) for s in shards],
        in_specs=[ANY] * n, out_specs=[ANY] * n,
        scratch_shapes=[pltpu.SemaphoreType.DMA((3 * n,)), pltpu.SemaphoreType.DMA((3 * n,)),
                        pltpu.SemaphoreType.DMA((n,))],
    )(*shards)


def _scatter_grads(grads, small):
    n = len(grads)

    def body(*refs):
        ins, small_in = refs[:n], refs[n]
        outs, small_out = refs[n + 1:2 * n + 1], refs[2 * n + 1]
        send, recv, loc, ssend, srecv = refs[2 * n + 2:]
        x, y, c, chips = _chip_peers()
        b = 2 * x + y
        me = 4 * x + 2 * y + c
        local = [pltpu.make_async_copy(ins[t].at[b], outs[t].at[3], loc.at[t]) for t in range(n)]
        local.append(pltpu.make_async_copy(small_in, small_out.at[me], loc.at[n]))
        for cp in local:
            cp.start()
        remote = []
        for t in range(n):
            for k, (px, py) in enumerate(chips):
                remote.append(pltpu.make_async_remote_copy(
                    src_ref=ins[t].at[2 * px + py], dst_ref=outs[t].at[k],
                    send_sem=send.at[3 * t + k], recv_sem=recv.at[3 * t + k],
                    device_id=(px, py, c), device_id_type=MESH))
        for r in range(1, 8):
            px = 1 - x if r & 4 else x
            py = 1 - y if r & 2 else y
            pc = 1 - c if r & 1 else c
            remote.append(pltpu.make_async_remote_copy(
                src_ref=small_in, dst_ref=small_out.at[me],
                send_sem=ssend.at[r - 1], recv_sem=srecv.at[r - 1],
                device_id=(px, py, pc), device_id_type=MESH))
        for cp in remote:
            cp.start()
        for cp in remote:
            cp.wait()
        for cp in local:
            cp.wait()

    out_shape = [jax.ShapeDtypeStruct(g.shape, g.dtype) for g in grads]
    out_shape.append(jax.ShapeDtypeStruct((8,) + small.shape, small.dtype))
    res = _call(
        body, name="scatter_grads", out_shape=out_shape,
        in_specs=[ANY] * (n + 1), out_specs=[ANY] * (n + 1),
        scratch_shapes=[pltpu.SemaphoreType.DMA((3 * n,)), pltpu.SemaphoreType.DMA((3 * n,)),
                        pltpu.SemaphoreType.DMA((n + 1,)),
                        pltpu.SemaphoreType.DMA((7,)), pltpu.SemaphoreType.DMA((7,))],
    )(*grads, small)
    return res[:n], res[n]


def _swap_with_sibling(parts):
    n = len(parts)

    def body(*refs):
        ins, outs = refs[:n], refs[n:2 * n]
        send, recv = refs[2 * n:]
        x, y, c = lax.axis_index("x"), lax.axis_index("y"), lax.axis_index("c")
        copies = [pltpu.make_async_remote_copy(
            src_ref=ins[t], dst_ref=outs[t], send_sem=send.at[t], recv_sem=recv.at[t],
            device_id=(x, y, 1 - c), device_id_type=MESH) for t in range(n)]
        for cp in copies:
            cp.start()
        for cp in copies:
            cp.wait()

    return _call(
        body, name="swap_sibling",
        out_shape=[jax.ShapeDtypeStruct(p.shape, p.dtype) for p in parts],
        in_specs=[ANY] * n, out_specs=[ANY] * n,
        scratch_shapes=[pltpu.SemaphoreType.DMA((n,)), pltpu.SemaphoreType.DMA((n,))],
    )(*parts)


def _sum_slots(buf, order, name):
    k, rows, cols = buf.shape
    tr = _row_tile(rows, 256)

    def body(b_ref, o_ref):
        acc = b_ref[order[0]].astype(F32)
        for s in order[1:]:
            acc = acc + b_ref[s].astype(F32)
        o_ref[...] = acc

    return _call(body, name=name, out_shape=jax.ShapeDtypeStruct((rows, cols), F32),
                 grid=(rows // tr,),
                 in_specs=[pl.BlockSpec((k, tr, cols), lambda i: (0, i, 0))],
                 out_specs=pl.BlockSpec((tr, cols), lambda i: (i, 0)),
                 compiler_params=_params(("parallel",)))(buf)


def _sum_own_and_received(src, slot, land, name):
    _, rows, cols = src.shape
    tr = _row_tile(rows, 256)

    def body(slot_ref, s_ref, l_ref, o_ref):
        acc = s_ref[...].astype(F32)
        for k in range(3):
            acc = acc + l_ref[k].astype(F32)
        o_ref[...] = acc

    return _call(
        body, name=name, out_shape=jax.ShapeDtypeStruct((rows, cols), F32),
        grid_spec=pltpu.PrefetchScalarGridSpec(
            num_scalar_prefetch=1, grid=(rows // tr,),
            in_specs=[pl.BlockSpec((None, tr, cols), lambda i, slot_ref: (slot_ref[0], i, 0)),
                      pl.BlockSpec((3, tr, cols), lambda i, slot_ref: (0, i, 0))],
            out_specs=pl.BlockSpec((tr, cols), lambda i, slot_ref: (i, 0))),
        compiler_params=_params(("parallel",)))(slot, src, land)


def _adamw(grad_parts, w, m, v, name):
    rows, cols = w.shape
    tr = _row_tile(rows, 256)
    npart = len(grad_parts)

    def body(*refs):
        gp = refs[:npart]
        w_ref, m_ref, v_ref, g_out, d_out, m_out, v_out = refs[npart:]
        g = gp[0][...]
        for p in gp[1:]:
            g = g + p[...]
        mm = ADAM_B1 * m_ref[...] + (1.0 - ADAM_B1) * g
        vv = ADAM_B2 * v_ref[...] + (1.0 - ADAM_B2) * jnp.square(g)
        m_hat = mm / (1.0 - ADAM_B1 ** ADAM_STEP)
        v_hat = vv / (1.0 - ADAM_B2 ** ADAM_STEP)
        g_out[...] = g
        d_out[...] = -ADAM_LR * (m_hat / (jnp.sqrt(v_hat) + ADAM_EPS) + ADAM_WD * w_ref[...])
        m_out[...] = mm
        v_out[...] = vv

    spec = pl.BlockSpec((tr, cols), lambda i: (i, 0))
    shp = jax.ShapeDtypeStruct((rows, cols), F32)
    return _call(body, name=name, out_shape=[shp] * 4, grid=(rows // tr,),
                 in_specs=[spec] * (npart + 3), out_specs=[spec] * 4,
                 compiler_params=_params(("parallel",)))(*grad_parts, w, m, v)


def _in_proj(x, g1, w_in):
    s = x.shape[0]
    ts = 512

    def body(x_ref, g_ref, w_ref, u_ref, q_ref, k_ref, v_ref, kt_ref, vt_ref, h_ref):
        xv = x_ref[...]
        h = (xv * _rstd(xv) * g_ref[...]).astype(BF16)
        h_ref[...] = h
        u_ref[...] = _dot(h, w_ref[0])
        q_ref[...] = (_dot(h, w_ref[1]) * Q_SCALE).astype(BF16)
        k = _dot(h, w_ref[2])
        k_ref[...] = k.astype(BF16)
        v = _dot(h, w_ref[3])
        v_ref[...] = v.astype(BF16)
        for src, dst in ((k, kt_ref), (v, vt_ref)):
            src_t = src.T.astype(BF16)
            for n in range(ts // ATT_BLOCK):
                dst[n] = src_t[:, n * ATT_BLOCK:(n + 1) * ATT_BLOCK]

    row = lambda w: pl.BlockSpec((ts, w), lambda i: (i, 0))
    half = jax.ShapeDtypeStruct((s, D_POOL), BF16)
    return _call(
        body, name="in_proj",
        out_shape=[jax.ShapeDtypeStruct((s, D_POOL), F32), half, half, half,
                   jax.ShapeDtypeStruct((s // ATT_BLOCK, D_ATTN, ATT_BLOCK), BF16),
                   jax.ShapeDtypeStruct((s // ATT_BLOCK, D_ATTN, ATT_BLOCK), BF16),
                   jax.ShapeDtypeStruct((s, D_MODEL), BF16)],
        grid=(s // ts,),
        in_specs=[row(D_MODEL), _full((1, D_MODEL)), _full(w_in.shape)],
        out_specs=[row(D_POOL)] * 4
        + [pl.BlockSpec((ts // ATT_BLOCK, D_ATTN, ATT_BLOCK), lambda i: (i, 0, 0))] * 2 + [row(D_MODEL)],
        compiler_params=_params(("parallel",)))(x, g1, w_in)


def _pool_means(ext_ref, g, window, ts, row0):
    cols = slice(g * POOL_GROUP, (g + 1) * POOL_GROUP)
    cur = ext_ref[POOL_HALO:POOL_HALO + ts, cols]
    acc = cur
    for d in range(1, window):
        acc = acc + ext_ref[POOL_HALO - d:POOL_HALO - d + ts, cols]
    t1 = row0 + 1 + lax.broadcasted_iota(jnp.int32, (ts, 1), 0)
    cnt = jnp.minimum(t1, window).astype(F32)
    return acc / cnt - cur, cnt


def _pool_fwd(u, w_pool, pool_scale):
    s = u.shape[0]
    ts = 512
    per = ts // POOL_HALO

    def body(u_ref, halo_ref, wp_ref, ps_ref, o_ref, ext_ref, y_ref):
        i = pl.program_id(0)
        ext_ref[0:POOL_HALO, :] = jnp.where(i > 0, halo_ref[...], 0.0)
        ext_ref[POOL_HALO:, :] = u_ref[...]
        for g, window in enumerate(POOL_WINDOWS):
            p, _ = _pool_means(ext_ref, g, window, ts, i * ts)
            y_ref[:, g * POOL_GROUP:(g + 1) * POOL_GROUP] = _dot(
                p.astype(BF16), wp_ref[g].astype(BF16))
        y = y_ref[...]
        o_ref[...] = (y * _rstd(y) * ps_ref[...]).astype(BF16)

    return _call(
        body, name="pool_fwd", out_shape=jax.ShapeDtypeStruct((s, D_POOL), BF16),
        grid=(s // ts,),
        in_specs=[pl.BlockSpec((ts, D_POOL), lambda i: (i, 0)),
                  pl.BlockSpec((POOL_HALO, D_POOL), lambda i: (jnp.maximum(i * per - 1, 0), 0)),
                  _full(w_pool.shape), _full((1, D_POOL))],
        out_specs=pl.BlockSpec((ts, D_POOL), lambda i: (i, 0)),
        scratch_shapes=[pltpu.VMEM((ts + POOL_HALO, D_POOL), F32), pltpu.VMEM((ts, D_POOL), F32)],
        compiler_params=_params(("parallel",)))(u, u, w_pool, pool_scale)


def _tri(kind):
    r = lax.broadcasted_iota(jnp.int32, (ATT_BLOCK, ATT_BLOCK), 0)
    c = lax.broadcasted_iota(jnp.int32, (ATT_BLOCK, ATT_BLOCK), 1)
    return jnp.where(r >= c if kind == "suffix" else r <= c, 1.0, 0.0).astype(BF16)


def _causal_mask():
    r = lax.broadcasted_iota(jnp.int32, (ATT_BLOCK, ATT_BLOCK), 0)
    c = lax.broadcasted_iota(jnp.int32, (ATT_BLOCK, ATT_BLOCK), 1)
    return c < r


def _softplus(z, with_sigmoid=False):
    ope = 1.0 + jnp.exp(jnp.minimum(z, 80.0))
    sp = jnp.maximum(z, jnp.log(ope))
    if with_sigmoid:
        return sp, 1.0 - 1.0 / ope
    return sp


def _attn_fwd(q, k, vt):
    s = q.shape[0]
    tb = ATT_BLOCK
    nq = s // tb

    def body(q_ref, k_ref, vt_ref, o_ref, t_ref, *bufs):
        i = pl.program_id(1)
        suffix = _tri("prefix")
        r_idx = lax.broadcasted_iota(jnp.int32, (tb, tb), 0)
        c_idx = lax.broadcasted_iota(jnp.int32, (tb, tb), 1)
        causal = r_idx < c_idx
        lane = lax.broadcasted_iota(jnp.int32, (1, HEAD_PAIR), 1)
        first = lane < 64
        top = lax.broadcasted_iota(jnp.int32, (HEAD_PAIR, 1), 0) < 64
        q2 = q_ref[...]
        zero = jnp.zeros_like(q2)
        qs_t = tuple(jnp.where(first, q2, zero).astype(F32).T.astype(BF16) if e == 0 else
                     jnp.where(first, zero, q2).astype(F32).T.astype(BF16) for e in range(2))

        def values_t(j):
            vt = vt_ref[j]
            none = jnp.zeros_like(vt)
            return jnp.concatenate([jnp.where(top, vt, none), jnp.where(top, none, vt)], axis=1)

        def scores(j, cs, masked):
            kj = k_ref[pl.ds(pl.multiple_of(j * tb, tb), tb), :]
            new_cs, args = [], []
            for e in range(2):
                z = _dot(kj, qs_t[e])
                sp = _softplus(z)
                if masked:
                    sp = jnp.where(causal, sp, 0.0)
                incl = _dot(suffix, sp.astype(BF16))
                arg = z - incl - cs[e]
                if masked:
                    arg = jnp.where(causal, arg, NEG_BIG)
                args.append(arg)
                new_cs.append(cs[e] + incl[0:1, :])
            return new_cs, args

        def weigh(j, args, o):
            probs = [jnp.exp(arg).astype(BF16) for arg in args]
            return o + _dot(values_t(j), jnp.concatenate(probs, axis=0))

        z_ring = [[bufs[2 * slot + e] for e in range(2)] for slot in range(4)]
        in_buf = [[bufs[8 + 2 * slot + e] for e in range(2)] for slot in range(2)]
        pr_buf = [bufs[12], bufs[13]]

        @pl.when((pl.program_id(0) == 0) & (i == 0))
        def _():
            for b in bufs:
                b[...] = jnp.zeros_like(b)

        def block_of(p):
            return jnp.clip(i - 1 - p, 0, nq - 1)

        def trip(t, u, carry):
            w, r = u % 2, 1 - u % 2
            cs, o = list(carry[0:2]), carry[2]
            live3 = (t - 3 >= 0) & (t - 3 < i)
            o = o + jnp.where(live3, _dot(values_t(block_of(t - 3)), pr_buf[r][...]), 0.0)
            kj = k_ref[pl.ds(pl.multiple_of(block_of(t) * tb, tb), tb), :]
            for e in range(2):
                z_ring[u][e][...] = _dot(kj, qs_t[e])
            for e in range(2):
                in_buf[w][e][...] = _dot(suffix, _softplus(z_ring[(u - 1) % 4][e][...]).astype(BF16))
            live2 = (t - 2 >= 0) & (t - 2 < i)
            for e in range(2):
                incl = in_buf[r][e][...]
                arg = z_ring[(u - 2) % 4][e][...] - incl - jnp.where(live2, cs[e], -NEG_BIG)
                pr_buf[w][e * tb:(e + 1) * tb, :] = jnp.exp(arg).astype(BF16)
                cs[e] = jnp.where(live2, cs[e] + incl[0:1, :], cs[e])
            return cs[0], cs[1], o

        def four_trips(n, cr):
            for u in range(4):
                cr = trip(4 * n + u, u, cr)
            return cr

        row = jnp.zeros((1, tb), F32)
        cs, args = scores(i, (row, row), True)
        carry = (cs[0], cs[1], weigh(i, args, jnp.zeros((HEAD_PAIR, tb), F32)))
        carry = lax.fori_loop(0, jnp.where(i > 0, (i + 6) // 4, 0), four_trips, carry)
        o_ref[...] = carry[2].T
        totals = jnp.where(r_idx == 0, carry[0], jnp.where(r_idx == 1, carry[1], 0.0))
        t_ref[...] = totals.T[:, 0:2]

    score_buf = pltpu.VMEM((tb, tb), F32)
    return _call(
        body, name="attn_fwd",
        out_shape=[jax.ShapeDtypeStruct((s, D_ATTN), F32),
                   jax.ShapeDtypeStruct((4, s, 2), F32)],
        grid=(4, nq),
        in_specs=[pl.BlockSpec((tb, HEAD_PAIR), lambda h, i: (i, h)),
                  pl.BlockSpec((s, HEAD_PAIR), lambda h, i: (0, h)),
                  pl.BlockSpec((nq, HEAD_PAIR, tb), lambda h, i: (0, h, 0))],
        out_specs=[pl.BlockSpec((tb, HEAD_PAIR), lambda h, i: (i, h)),
                   pl.BlockSpec((None, tb, 2), lambda h, i: (h, i, 0))],
        scratch_shapes=[score_buf] * 12 + [pltpu.VMEM((2 * tb, tb), BF16)] * 2,
        compiler_params=_params(("arbitrary", "arbitrary")))(q, k, vt)


def _mix_out(attn, mpool, x, attn_scale, w_out, g2, g3):
    s = x.shape[0]
    ts = 512

    def body(a_ref, p_ref, x_ref, as_ref, w_ref, g2_ref, g3_ref, ma_ref, mix_ref, x1_ref, h2_ref):
        ao = a_ref[...]
        ma = (ao * _rstd(ao) * as_ref[...]).astype(BF16)
        ma_ref[...] = ma
        mix = _dot(p_ref[...], w_ref[0:D_POOL, :]) + _dot(ma, w_ref[D_POOL:, :])
        mix_ref[...] = mix
        x1 = x_ref[...] + mix * _rstd(mix) * g2_ref[...]
        x1_ref[...] = x1
        h2_ref[...] = (x1 * _rstd(x1) * g3_ref[...]).astype(BF16)

    row = lambda w: pl.BlockSpec((ts, w), lambda i: (i, 0))
    return _call(
        body, name="mix_out",
        out_shape=[jax.ShapeDtypeStruct((s, D_ATTN), BF16), jax.ShapeDtypeStruct((s, D_MODEL), F32),
                   jax.ShapeDtypeStruct((s, D_MODEL), F32), jax.ShapeDtypeStruct((s, D_MODEL), BF16)],
        grid=(s // ts,),
        in_specs=[row(D_ATTN), row(D_POOL), row(D_MODEL), _full((1, D_ATTN)),
                  _full((D_MODEL, D_MODEL)), _full((1, D_MODEL)), _full((1, D_MODEL))],
        out_specs=[row(D_ATTN), row(D_MODEL), row(D_MODEL), row(D_MODEL)],
        compiler_params=_params(("parallel",)))(attn, mpool, x, attn_scale, w_out, g2, g3)


def _earlier_rows(halo, block):
    ts = block.shape[0]
    ext = jnp.concatenate([halo, block], axis=0)
    return tuple(pltpu.roll(ext, d, axis=0)[CONV_HALO:CONV_HALO + ts, :] for d in (1, 2))


def _later_rows(block, halo):
    ts = block.shape[0]
    ext = jnp.concatenate([block, halo], axis=0)
    return tuple(pltpu.roll(ext, ts + CONV_HALO - d, axis=0)[0:ts, :] for d in (1, 2))


def _conv_rows(x0, x1, x2, cw, cb):
    y = cb + cw[0:1, :] * x2
    y = y + cw[1:2, :] * x1
    return y + cw[2:3, :] * x0


def _sigmoid(v):
    return 1.0 / (1.0 + jnp.exp(-v))


def _ffn_up(h2, w_up, conv_w, conv_b):
    s = h2.shape[0]
    ts = 256
    tn = FF_TILE

    def body(h_ref, wg_ref, wv_ref, cwg_ref, cwv_ref, cbg_ref, cbv_ref,
             ug_ref, uv_ref, f_ref, tailg, tailv):
        i = pl.program_id(1)

        @pl.when(i == 0)
        def _():
            tailg[...] = jnp.zeros_like(tailg)
            tailv[...] = jnp.zeros_like(tailv)

        h = h_ref[...]
        ug = _dot(h, wg_ref[...])
        uv = _dot(h, wv_ref[...])
        ug_ref[...] = ug
        uv_ref[...] = uv
        gate = _conv_rows(ug, *_earlier_rows(tailg[...], ug), cwg_ref[...], cbg_ref[...])
        val = _conv_rows(uv, *_earlier_rows(tailv[...], uv), cwv_ref[...], cbv_ref[...])
        f_ref[...] = (gate * _sigmoid(gate) * val).astype(BF16)
        tailg[...] = ug[ts - CONV_HALO:, :]
        tailv[...] = uv[ts - CONV_HALO:, :]

    out_blk = pl.BlockSpec((None, ts, tn), lambda n, i: (n, i, 0))
    act = jax.ShapeDtypeStruct((2, s, tn), F32)
    return _call(
        body, name="ffn_up",
        out_shape=[act, act, jax.ShapeDtypeStruct((2, s, tn), BF16)],
        grid=(2, s // ts),
        in_specs=[pl.BlockSpec((ts, D_MODEL), lambda n, i: (i, 0)),
                  pl.BlockSpec((None, D_MODEL, tn), lambda n, i: (n, 0, 0)),
                  pl.BlockSpec((None, D_MODEL, tn), lambda n, i: (n + 2, 0, 0)),
                  pl.BlockSpec((None, 3, tn), lambda n, i: (n, 0, 0)),
                  pl.BlockSpec((None, 3, tn), lambda n, i: (n + 2, 0, 0)),
                  pl.BlockSpec((None, 1, tn), lambda n, i: (n, 0, 0)),
                  pl.BlockSpec((None, 1, tn), lambda n, i: (n + 2, 0, 0))],
        out_specs=[out_blk, out_blk, out_blk],
        scratch_shapes=[pltpu.VMEM((CONV_HALO, tn), F32), pltpu.VMEM((CONV_HALO, tn), F32)],
        compiler_params=_params(("arbitrary", "arbitrary")))(
            h2, w_up, w_up, conv_w, conv_w, conv_b, conv_b)


def _ffn_down(f_in, w_down, x1, target, g4):
    s = x1.shape[0]
    ts = 512

    def body(f_ref, w_ref, x1_ref, t_ref, g_ref, df_ref, dy_ref, loss_ref, dg_ref):
        @pl.when(pl.program_id(0) == 0)
        def _():
            loss_ref[...] = jnp.zeros_like(loss_ref)
            dg_ref[...] = jnp.zeros_like(dg_ref)

        f = _dot(f_ref[0], w_ref[0:FF_TILE, :]) + _dot(f_ref[1], w_ref[FF_TILE:, :])
        rf = _rstd(f)
        fn = f * rf
        g = g_ref[...]
        err = (x1_ref[...] + fn * g) - t_ref[...]
        loss_ref[...] += 0.5 * jnp.sum(jnp.mean(err * err, axis=-1))
        dy = err * (1.0 / D_MODEL)
        dy_ref[...] = dy
        dg_ref[...] += jnp.sum(dy * fn, axis=0, keepdims=True)
        dfn = dy * g
        df_ref[...] = (rf * (dfn - fn * jnp.mean(dfn * fn, axis=-1, keepdims=True))).astype(BF16)

    row = pl.BlockSpec((ts, D_MODEL), lambda i: (i, 0))
    return _call(
        body, name="ffn_down",
        out_shape=[jax.ShapeDtypeStruct((s, D_MODEL), BF16), jax.ShapeDtypeStruct((s, D_MODEL), F32),
                   jax.ShapeDtypeStruct((8, 128), F32), jax.ShapeDtypeStruct((1, D_MODEL), F32)],
        grid=(s // ts,),
        in_specs=[pl.BlockSpec((2, ts, FF_TILE), lambda i: (0, i, 0)), _full((D_FF, D_MODEL)),
                  row, row, _full((1, D_MODEL))],
        out_specs=[row, row, _full((8, 128)), _full((1, D_MODEL))],
        compiler_params=_params(("arbitrary",)))(f_in, w_down, x1, target, g4)


def _tn_matmul(a, b, name, ts=512):
    na, s, ka = a.shape
    nb, _, nbc = b.shape
    steps = s // ts

    def body(a_ref, b_ref, o_ref, acc_ref):
        @pl.when(pl.program_id(2) == 0)
        def _():
            acc_ref[...] = jnp.zeros_like(acc_ref)

        acc_ref[...] += _dot_tn(a_ref[...].astype(BF16), b_ref[...].astype(BF16))

        @pl.when(pl.program_id(2) == steps - 1)
        def _():
            o_ref[...] = acc_ref[...].astype(BF16)

    return _call(
        body, name=name, out_shape=jax.ShapeDtypeStruct((na, nb, ka, nbc), BF16),
        grid=(na, nb, steps),
        in_specs=[pl.BlockSpec((None, ts, ka), lambda i, j, r: (i, r, 0)),
                  pl.BlockSpec((None, ts, nbc), lambda i, j, r: (j, r, 0))],
        out_specs=pl.BlockSpec((None, None, ka, nbc), lambda i, j, r: (i, j, 0, 0)),
        scratch_shapes=[pltpu.VMEM((ka, nbc), F32)],
        compiler_params=_params(("parallel", "parallel", "arbitrary")))(a, b)


def _ffn_bwd_act(df, w_down, upre_g, upre_v, conv_w, conv_b):
    s = df.shape[0]
    ts = 256
    tn = FF_TILE
    nr = s // ts
    per = ts // CONV_HALO

    def body(df_ref, wd_ref, ug_ref, uv_ref, hg_ref, hv_ref, cwg_ref, cwv_ref, cbg_ref, cbv_ref,
             dug_ref, duv_ref, dcwg_ref, dcwv_ref, dcbg_ref, dcbv_ref, headg, headv):
        i = pl.program_id(1)
        first_rows = i == nr - 1

        @pl.when(i == 0)
        def _():
            for r in (headg, headv, dcwg_ref, dcwv_ref, dcbg_ref, dcbv_ref):
                r[...] = jnp.zeros_like(r)

        ug, uv = ug_ref[...], uv_ref[...]
        cwg, cwv = cwg_ref[...], cwv_ref[...]
        gate = _conv_rows(ug, *_earlier_rows(jnp.where(first_rows, 0.0, hg_ref[...]), ug), cwg, cbg_ref[...])
        val = _conv_rows(uv, *_earlier_rows(jnp.where(first_rows, 0.0, hv_ref[...]), uv), cwv, cbv_ref[...])
        sg = _sigmoid(gate)
        dfin = _dot_nt(df_ref[...], wd_ref[...])
        dval = dfin * (gate * sg)
        dgate = dfin * val * (sg * (1.0 + gate * (1.0 - sg)))

        def conv_bwd(dact, x, head, cw, dcw_ref, dcb_ref, du_ref):
            d1, d2 = _later_rows(dact, head[...])
            dcb_ref[...] += jnp.sum(dact, axis=0, keepdims=True)
            for kk, shifted in enumerate((d2, d1, dact)):
                dcw_ref[kk:kk + 1, :] += jnp.sum(x * shifted, axis=0, keepdims=True)
            du_ref[...] = (cw[2:3, :] * dact + cw[1:2, :] * d1 + cw[0:1, :] * d2).astype(BF16)
            head[...] = dact[0:CONV_HALO, :]

        conv_bwd(dgate, ug, headg, cwg, dcwg_ref, dcbg_ref, dug_ref)
        conv_bwd(dval, uv, headv, cwv, dcwv_ref, dcbv_ref, duv_ref)

    rows = lambda n, i: (n, nr - 1 - i, 0)
    halo = lambda n, i: (n, jnp.maximum((nr - 1 - i) * per - 1, 0), 0)
    act_blk = pl.BlockSpec((None, ts, tn), rows)
    halo_blk = pl.BlockSpec((None, CONV_HALO, tn), halo)
    cw_blk = lambda off: pl.BlockSpec((None, 3, tn), lambda n, i: (n + off, 0, 0))
    cb_blk = lambda off: pl.BlockSpec((None, 1, tn), lambda n, i: (n + off, 0, 0))
    acc_w = pl.BlockSpec((None, 3, tn), lambda n, i: (n, 0, 0))
    acc_b = pl.BlockSpec((None, 1, tn), lambda n, i: (n, 0, 0))
    dact = jax.ShapeDtypeStruct((2, s, tn), BF16)
    return _call(
        body, name="ffn_bwd_act",
        out_shape=[dact, dact, jax.ShapeDtypeStruct((2, 3, tn), F32), jax.ShapeDtypeStruct((2, 3, tn), F32),
                   jax.ShapeDtypeStruct((2, 1, tn), F32), jax.ShapeDtypeStruct((2, 1, tn), F32)],
        grid=(2, nr),
        in_specs=[pl.BlockSpec((ts, D_MODEL), lambda n, i: (nr - 1 - i, 0)),
                  pl.BlockSpec((tn, D_MODEL), lambda n, i: (n, 0)),
                  act_blk, act_blk, halo_blk, halo_blk,
                  cw_blk(0), cw_blk(2), cb_blk(0), cb_blk(2)],
        out_specs=[act_blk, act_blk, acc_w, acc_w, acc_b, acc_b],
        scratch_shapes=[pltpu.VMEM((CONV_HALO, tn), F32)] * 2,
        compiler_params=_params(("arbitrary", "arbitrary")))(
            df, w_down, upre_g, upre_v, upre_g, upre_v, conv_w, conv_w, conv_b, conv_b)


def _ffn_bwd_in(dug, duv, w_up, x1, dy, mix, g3, g2):
    s = x1.shape[0]
    ts = 256

    def body(dg_ref, dv_ref, w_ref, x1_ref, dy_ref, mix_ref, g3_ref, g2_ref,
             dx1_ref, dmix_ref, dg3_ref, dg2_ref):
        @pl.when(pl.program_id(0) == 0)
        def _():
            dg3_ref[...] = jnp.zeros_like(dg3_ref)
            dg2_ref[...] = jnp.zeros_like(dg2_ref)

        dh = _dot_nt(dg_ref[0], w_ref[0]) + _dot_nt(dg_ref[1], w_ref[1])
        dh = dh + _dot_nt(dv_ref[0], w_ref[2]) + _dot_nt(dv_ref[1], w_ref[3])
        x1 = x1_ref[...]
        r3 = _rstd(x1)
        xn = x1 * r3
        dg3_ref[...] += jnp.sum(dh * xn, axis=0, keepdims=True)
        dxn = dh * g3_ref[...]
        dx1 = dy_ref[...] + r3 * (dxn - xn * jnp.mean(dxn * xn, axis=-1, keepdims=True))
        dx1_ref[...] = dx1
        mix = mix_ref[...]
        rm = _rstd(mix)
        mn = mix * rm
        dg2_ref[...] += jnp.sum(dx1 * mn, axis=0, keepdims=True)
        dmn = dx1 * g2_ref[...]
        dmix_ref[...] = (rm * (dmn - mn * jnp.mean(dmn * mn, axis=-1, keepdims=True))).astype(BF16)

    row = pl.BlockSpec((ts, D_MODEL), lambda i: (i, 0))
    act = pl.BlockSpec((2, ts, FF_TILE), lambda i: (0, i, 0))
    vec = _full((1, D_MODEL))
    return _call(
        body, name="ffn_bwd_in",
        out_shape=[jax.ShapeDtypeStruct((s, D_MODEL), F32), jax.ShapeDtypeStruct((s, D_MODEL), BF16),
                   jax.ShapeDtypeStruct((1, D_MODEL), F32), jax.ShapeDtypeStruct((1, D_MODEL), F32)],
        grid=(s // ts,),
        in_specs=[act, act, _full(w_up.shape), row, row, row, vec, vec],
        out_specs=[row, row, vec, vec],
        compiler_params=_params(("arbitrary",), vmem_mb=56))(dug, duv, w_up, x1, dy, mix, g3, g2)


def _mix_bwd(dmix, w_out, attn, attn_scale):
    s = dmix.shape[0]
    ts = 512

    def body(dm_ref, w_ref, a_ref, as_ref, dp_ref, do_ref, das_ref):
        @pl.when(pl.program_id(0) == 0)
        def _():
            das_ref[...] = jnp.zeros_like(das_ref)

        dm = dm_ref[...]
        dp_ref[...] = _dot_nt(dm, w_ref[0:D_POOL, :])
        da = _dot_nt(dm, w_ref[D_POOL:, :])
        ao = a_ref[...]
        ra = _rstd(ao)
        an = ao * ra
        das_ref[...] += jnp.sum(da * an, axis=0, keepdims=True)
        dan = da * as_ref[...]
        do_ref[...] = (ra * (dan - an * jnp.mean(dan * an, axis=-1, keepdims=True))).astype(BF16)

    row = lambda w: pl.BlockSpec((ts, w), lambda i: (i, 0))
    return _call(
        body, name="mix_bwd",
        out_shape=[jax.ShapeDtypeStruct((s, D_POOL), F32), jax.ShapeDtypeStruct((s, D_ATTN), BF16),
                   jax.ShapeDtypeStruct((1, D_ATTN), F32)],
        grid=(s // ts,),
        in_specs=[row(D_MODEL), _full((D_MODEL, D_MODEL)), row(D_ATTN), _full((1, D_ATTN))],
        out_specs=[row(D_POOL), row(D_ATTN), _full((1, D_ATTN))],
        compiler_params=_params(("arbitrary",)))(dmix, w_out, attn, attn_scale)


def _attn_bwd(q, k, kt, vt, do, totals):
    s = q.shape[0]
    tb = ATT_BLOCK
    nq = s // tb

    def body(q_ref, do_ref, t_ref, k_hbm, kt_hbm, vt_hbm, dq_ref, dk_hbm, dv_hbm,
             k_scr, kt_scr, vt_scr, dkt_acc, dvt_acc, stage, *bufs):
        hp = pl.program_id(0)
        i = pl.program_id(1)
        lanes = pl.ds(pl.multiple_of(hp * HEAD_PAIR, HEAD_PAIR), HEAD_PAIR)

        @pl.when(i == 0)
        def _():
            pltpu.sync_copy(k_hbm.at[:, lanes], k_scr)
            pltpu.sync_copy(kt_hbm.at[:, lanes, :], kt_scr)
            pltpu.sync_copy(vt_hbm.at[:, lanes, :], vt_scr)
            dkt_acc[...] = jnp.zeros_like(dkt_acc)
            dvt_acc[...] = jnp.zeros_like(dvt_acc)

        upper = _tri("suffix")
        lower = _tri("prefix")
        causal = _causal_mask()
        lane = lax.broadcasted_iota(jnp.int32, (1, HEAD_PAIR), 1)
        first = lane < 64
        q2 = q_ref[...]
        do2 = do_ref[...]
        zero = jnp.zeros_like(q2)
        qs = (jnp.where(first, q2, zero), jnp.where(first, zero, q2))
        dos = (jnp.where(first, do2, zero), jnp.where(first, zero, do2))
        qcat_t = jnp.concatenate(qs, axis=0).astype(F32).T.astype(BF16)
        docat_t = jnp.concatenate(dos, axis=0).astype(F32).T.astype(BF16)
        tots = (t_ref[:, 0:1], t_ref[:, 1:2])

        z_ring, sg_ring = [[[bufs[8 * kind + 2 * slot + e] for e in range(2)] for slot in range(4)]
                           for kind in range(2)]
        in_buf, da_buf, dw_buf, pre_buf = [
            [[bufs[16 + 4 * kind + 2 * slot + e] for e in range(2)] for slot in range(2)]
            for kind in range(4)]
        pr_buf, dzr_buf, dzc_buf = bufs[32:34], bufs[34:36], bufs[36:38]

        for e in range(2):
            for slot in (2, 3):
                z_ring[slot][e][...] = jnp.full((tb, tb), NEG_BIG, F32)
            for slot in (1, 2, 3):
                sg_ring[slot][e][...] = jnp.zeros((tb, tb), F32)
            for buf in (in_buf, da_buf, dw_buf, pre_buf):
                buf[1][e][...] = jnp.zeros((tb, tb), F32)
        for buf in (pr_buf, dzr_buf, dzc_buf):
            buf[1][...] = jnp.zeros_like(buf[1])

        def rows(p):
            return pl.ds(pl.multiple_of(jnp.clip(p, 0, nq - 1) * tb, tb), tb)

        def split_heads(block):
            return jnp.concatenate([jnp.where(first, block, zero), jnp.where(first, zero, block)], axis=0)

        def trip(t, u, carry):
            w, r = u % 2, 1 - u % 2
            cs, cps, dq = list(carry[0:2]), list(carry[2:4]), carry[4]
            live4 = (t - 4 >= 0) & (t - 4 < i)
            dq = dq + jnp.where(live4, _dot(dzc_buf[r][...], split_heads(k_scr[rows(t - 4), :])), 0.0)
            dkt_acc[jnp.clip(t - 4, 0, nq - 1)] += jnp.where(live4, _dot(qcat_t, dzr_buf[r][...]), 0.0)
            dvt_acc[jnp.clip(t - 3, 0, nq - 1)] += _dot(docat_t, pr_buf[r][...])
            ktj = kt_scr[jnp.clip(t, 0, nq - 1)]
            for e in range(2):
                z_ring[u][e][...] = _dot(qs[e], ktj)
            vtj = vt_scr[jnp.clip(t - 1, 0, nq - 1)]
            for e in range(2):
                sp, sig = _softplus(z_ring[(u - 1) % 4][e][...], True)
                sg_ring[(u - 1) % 4][e][...] = sig
                in_buf[w][e][...] = _dot(sp.astype(BF16), upper)
                da_buf[w][e][...] = _dot(dos[e], vtj)
            for e in range(2):
                incl = in_buf[r][e][...]
                cs[e] = cs[e] + incl[:, 0:1]
                off = jnp.where(t - 2 < i, tots[e] - cs[e], -NEG_BIG)
                a = jnp.exp(z_ring[(u - 2) % 4][e][...] - incl - off)
                dw = a * da_buf[r][e][...]
                dw_buf[w][e][...] = dw
                pr_buf[w][e * tb:(e + 1) * tb, :] = a.astype(BF16)
                pre_buf[w][e][...] = _dot(dw.astype(BF16), lower)
            for e in range(2):
                pre = pre_buf[r][e][...] + cps[e]
                dzb = (dw_buf[r][e][...] - sg_ring[(u - 3) % 4][e][...] * pre).astype(BF16)
                cps[e] = pre[:, tb - 1:tb]
                dzr_buf[w][e * tb:(e + 1) * tb, :] = dzb
                dzc_buf[w][:, e * tb:(e + 1) * tb] = dzb
            return cs[0], cs[1], cps[0], cps[1], dq

        col = jnp.zeros((tb, 1), F32)
        carry = (col, col, col, col, jnp.zeros((tb, HEAD_PAIR), F32))
        def four_trips(n, cr):
            for u in range(4):
                cr = trip(4 * n + u, u, cr)
            return cr

        carry = lax.fori_loop(0, jnp.where(i > 0, (i + 7) // 4, 0), four_trips, carry)

        cps, dq = carry[2:4], carry[4]
        kj = k_scr[rows(i), :]
        dzs, probs = [], []
        for e in range(2):
            z = _dot(qs[e], kt_scr[i])
            sp, sig = _softplus(z, True)
            incl = _dot(jnp.where(causal, sp, 0.0).astype(BF16), upper)
            a = jnp.where(causal, jnp.exp(z - incl), 0.0)
            dw = a * _dot(dos[e], vt_scr[i])
            pre = _dot(dw.astype(BF16), lower) + cps[e]
            dzs.append(jnp.where(causal, dw - sig * pre, 0.0).astype(BF16))
            probs.append(a.astype(BF16))
        dq = dq + _dot(jnp.concatenate(dzs, axis=1), split_heads(kj))
        dkt_acc[i] += _dot(qcat_t, jnp.concatenate(dzs, axis=0))
        dvt_acc[i] += _dot(docat_t, jnp.concatenate(probs, axis=0))
        dq_ref[...] = (dq * Q_SCALE).astype(BF16)

        @pl.when(i == nq - 1)
        def _():
            for acc, dst in ((dkt_acc, dk_hbm), (dvt_acc, dv_hbm)):
                def flip(n, _, acc=acc):
                    at = pl.ds(pl.multiple_of(n * tb, tb), tb)
                    stage[at, :] = acc[n].T
                    return 0
                lax.fori_loop(0, nq, flip, 0)
                pltpu.sync_copy(stage, dst.at[:, lanes])

    blk = pl.BlockSpec((tb, HEAD_PAIR), lambda h, i: (i, h))
    grad = jax.ShapeDtypeStruct((s, D_ATTN), F32)
    return _call(
        body, name="attn_bwd",
        out_shape=[jax.ShapeDtypeStruct((s, D_ATTN), BF16), grad, grad],
        grid=(4, nq),
        in_specs=[blk, blk, pl.BlockSpec((None, tb, 2), lambda h, i: (h, i, 0)), ANY, ANY, ANY],
        out_specs=[blk, ANY, ANY],
        scratch_shapes=[pltpu.VMEM((s, HEAD_PAIR), BF16), pltpu.VMEM((nq, HEAD_PAIR, tb), BF16),
                        pltpu.VMEM((nq, HEAD_PAIR, tb), BF16),
                        pltpu.VMEM((nq, HEAD_PAIR, tb), F32), pltpu.VMEM((nq, HEAD_PAIR, tb), F32),
                        pltpu.VMEM((s, HEAD_PAIR), F32)]
        + [pltpu.VMEM((tb, tb), F32)] * 32
        + [pltpu.VMEM((2 * tb, tb), BF16)] * 4 + [pltpu.VMEM((tb, 2 * tb), BF16)] * 2,
        compiler_params=_params(("arbitrary", "arbitrary"), vmem_mb=60))(q, do, totals, k, kt, vt)


def _pool_bwd(u, dmp, w_pool, pool_scale):
    s = u.shape[0]
    ts = 512
    nr = s // ts
    per = ts // POOL_HALO

    def body(u_ref, halo_ref, dm_ref, wp_ref, ps_ref, du_ref, dwp_ref, dps_ref, ext_ref, y_ref, dext_ref):
        i = pl.program_id(0)
        rb = nr - 1 - i

        @pl.when(i == 0)
        def _():
            dext_ref[ts:, :] = jnp.zeros((POOL_HALO, D_POOL), F32)
            dwp_ref[...] = jnp.zeros_like(dwp_ref)
            dps_ref[...] = jnp.zeros_like(dps_ref)

        ext_ref[0:POOL_HALO, :] = jnp.where(rb > 0, halo_ref[...], 0.0)
        ext_ref[POOL_HALO:, :] = u_ref[...]
        ps, cnts = [], []
        for g, window in enumerate(POOL_WINDOWS):
            p, cnt = _pool_means(ext_ref, g, window, ts, rb * ts)
            ps.append(p.astype(BF16))
            cnts.append(cnt)
            y_ref[:, g * POOL_GROUP:(g + 1) * POOL_GROUP] = _dot(ps[g], wp_ref[g].astype(BF16))
        y = y_ref[...]
        r = _rstd(y)
        yn = y * r
        dm = dm_ref[...]
        dps_ref[...] += jnp.sum(dm * yn, axis=0, keepdims=True)
        dn = dm * ps_ref[...]
        dy = r * (dn - yn * jnp.mean(dn * yn, axis=-1, keepdims=True))
        for g, window in enumerate(POOL_WINDOWS):
            cols = slice(g * POOL_GROUP, (g + 1) * POOL_GROUP)
            dyg = dy[:, cols].astype(BF16)
            dwp_ref[g] += _dot_tn(ps[g], dyg)
            dp = _dot_nt(dyg, wp_ref[g].astype(BF16))
            dext_ref[0:ts, cols] = dp / cnts[g]
            acc = dext_ref[0:ts, cols]
            for d in range(1, window):
                acc = acc + dext_ref[d:d + ts, cols]
            du_ref[:, cols] = (acc - dp).astype(BF16)
        dext_ref[ts:, :] = dext_ref[0:POOL_HALO, :]

    rows = pl.BlockSpec((ts, D_POOL), lambda i: (nr - 1 - i, 0))
    return _call(
        body, name="pool_bwd",
        out_shape=[jax.ShapeDtypeStruct((s, D_POOL), BF16), jax.ShapeDtypeStruct(w_pool.shape, F32),
                   jax.ShapeDtypeStruct((1, D_POOL), F32)],
        grid=(nr,),
        in_specs=[rows,
                  pl.BlockSpec((POOL_HALO, D_POOL), lambda i: (jnp.maximum((nr - 1 - i) * per - 1, 0), 0)),
                  rows, _full(w_pool.shape), _full((1, D_POOL))],
        out_specs=[rows, _full(w_pool.shape), _full((1, D_POOL))],
        scratch_shapes=[pltpu.VMEM((ts + POOL_HALO, D_POOL), F32), pltpu.VMEM((ts, D_POOL), F32),
                        pltpu.VMEM((ts + POOL_HALO, D_POOL), F32)],
        compiler_params=_params(("arbitrary",)))(u, u, dmp, w_pool, pool_scale)


def _in_proj_bwd(du, dq, dk, dv, w_in, x, dx1, g1):
    s = x.shape[0]
    ts = 512

    def body(du_ref, dq_ref, dk_ref, dv_ref, w_ref, x_ref, dx1_ref, g_ref, gx_ref, dg_ref):
        @pl.when(pl.program_id(0) == 0)
        def _():
            dg_ref[...] = jnp.zeros_like(dg_ref)

        dh = _dot_nt(du_ref[...], w_ref[0]) + _dot_nt(dq_ref[...], w_ref[1])
        dh = dh + _dot_nt(dk_ref[...].astype(BF16), w_ref[2]) + _dot_nt(dv_ref[...].astype(BF16), w_ref[3])
        xv = x_ref[...]
        r = _rstd(xv)
        xn = xv * r
        dg_ref[...] += jnp.sum(dh * xn, axis=0, keepdims=True)
        dxn = dh * g_ref[...]
        gx_ref[...] = dx1_ref[...] + r * (dxn - xn * jnp.mean(dxn * xn, axis=-1, keepdims=True))

    row = lambda w: pl.BlockSpec((ts, w), lambda i: (i, 0))
    return _call(
        body, name="in_proj_bwd",
        out_shape=[jax.ShapeDtypeStruct((s, D_MODEL), F32), jax.ShapeDtypeStruct((1, D_MODEL), F32)],
        grid=(s // ts,),
        in_specs=[row(D_POOL)] * 4 + [_full(w_in.shape), row(D_MODEL), row(D_MODEL), _full((1, D_MODEL))],
        out_specs=[row(D_MODEL), _full((1, D_MODEL))],
        compiler_params=_params(("arbitrary",)))(du, dq, dk, dv, w_in, x, dx1, g1)


_SMALL = ("norm_mix_pre", "w_pool", "pool_scale", "attn_scale", "norm_mix_post",
          "norm_ffn_pre", "conv_b", "norm_ffn_post")
_SMALL_SIZE = {"norm_mix_pre": 1024, "w_pool": 65536, "pool_scale": 512, "attn_scale": 512,
               "norm_mix_post": 1024, "norm_ffn_pre": 1024, "conv_b": 5632, "norm_ffn_post": 1024}
_SMALL_ROWS = 600
_CONVW_ROWS = 132
_PACK_ROWS = _SMALL_ROWS + _CONVW_ROWS + 4


def _pack_small(parts):
    flat = jnp.concatenate([parts[n].reshape(-1) for n in _SMALL])
    flat = jnp.pad(flat, (0, _SMALL_ROWS * 128 - flat.shape[0]))
    return flat.reshape(_SMALL_ROWS, 128)


def _unpack_small(packed, like):
    flat = packed.reshape(-1)
    out, off = {}, 0
    for n in _SMALL:
        out[n] = flat[off:off + _SMALL_SIZE[n]].reshape(like[n].shape)
        off += _SMALL_SIZE[n]
    return out


def kernel(x, norm_mix_pre, w_in, w_pool, pool_scale, attn_scale, w_out, norm_mix_post, norm_ffn_pre, w_up, conv_w, conv_b, w_down, norm_ffn_post, loss_target, m_norm_mix_pre, m_w_in, m_w_pool, m_pool_scale, m_attn_scale, m_w_out, m_norm_mix_post, m_norm_ffn_pre, m_w_up, m_conv_w, m_conv_b, m_w_down, m_norm_ffn_post, v_norm_mix_pre, v_w_in, v_w_pool, v_pool_scale, v_attn_scale, v_w_out, v_norm_mix_post, v_norm_ffn_pre, v_w_up, v_conv_w, v_conv_b, v_w_down, v_norm_ffn_post):
    weights = dict(norm_mix_pre=norm_mix_pre, w_in=w_in, w_pool=w_pool, pool_scale=pool_scale,
                   attn_scale=attn_scale, w_out=w_out, norm_mix_post=norm_mix_post,
                   norm_ffn_pre=norm_ffn_pre, w_up=w_up, conv_w=conv_w, conv_b=conv_b,
                   w_down=w_down, norm_ffn_post=norm_ffn_post)
    mom1 = dict(norm_mix_pre=m_norm_mix_pre, w_in=m_w_in, w_pool=m_w_pool, pool_scale=m_pool_scale,
                attn_scale=m_attn_scale, w_out=m_w_out, norm_mix_post=m_norm_mix_post,
                norm_ffn_pre=m_norm_ffn_pre, w_up=m_w_up, conv_w=m_conv_w, conv_b=m_conv_b,
                w_down=m_w_down, norm_ffn_post=m_norm_ffn_post)
    mom2 = dict(norm_mix_pre=v_norm_mix_pre, w_in=v_w_in, w_pool=v_w_pool, pool_scale=v_pool_scale,
                attn_scale=v_attn_scale, w_out=v_w_out, norm_mix_post=v_norm_mix_post,
                norm_ffn_pre=v_norm_ffn_pre, w_up=v_w_up, conv_w=v_conv_w, conv_b=v_conv_b,
                w_down=v_w_down, norm_ffn_post=v_norm_ffn_post)
    order = list(weights)

    xs = x[0]
    target = loss_target[0]
    wp = w_pool[0]
    shard = lax.axis_index("x") * 2 + lax.axis_index("y")

    slot = shard.astype(jnp.int32).reshape(1)
    win_g, = _gather_shards([_cast_bf16(w_in[0], "cast_w_in")])
    lands = [_into_slot(w_out[0], slot, BF16, "cast_w_out"), _into_slot(w_up[0], slot, BF16, "cast_w_up"),
             _into_slot(w_down[0], slot, BF16, "cast_w_down"), _into_slot(conv_w[0], slot, F32, "place_conv_w")]
    g_send, g_recv, g_lands, g_token = _exchange_start(None, lands, win_g, "gather_start")
    convb_g = conv_b[0].reshape(N_SHARD, 1, FF_TILE)

    u, q, k, v, kt, vt, h1 = _in_proj(xs, norm_mix_pre + g_token[0:1, 0:1], win_g)
    mpool = _pool_fwd(u, wp, pool_scale)
    attn, totals = _attn_fwd(q, k, vt)
    _, (wout_g, wup_g, wdown_g, convw_g) = _exchange_wait(g_send, g_recv, g_lands, False, attn, "gather_wait")
    wout_f = wout_g.reshape(D_MODEL, D_MODEL)
    wdown_f = wdown_g.reshape(D_FF, D_MODEL)
    mattn, mix, x1, h2 = _mix_out(attn, mpool, xs, attn_scale, wout_f, norm_mix_post, norm_ffn_pre)
    upre_g, upre_v, f_in = _ffn_up(h2, wup_g, convw_g, convb_g)
    df, dy, loss_tile, d_post = _ffn_down(f_in, wdown_f, x1, target, norm_ffn_post)

    d_wdown = _tn_matmul(f_in, df[None], "dw_down")
    dug, duv, dcw_g, dcw_v, dcb_g, dcb_v = _ffn_bwd_act(df, wdown_f, upre_g, upre_v, convw_g, convb_g)
    d_wup = jnp.concatenate([_tn_matmul(h2[None], dug, "dw_up_gate")[0],
                             _tn_matmul(h2[None], duv, "dw_up_value")[0]], axis=0)
    early = [d_wup, d_wdown.reshape(N_SHARD, D_FF // N_SHARD, D_MODEL)]
    s_send, s_recv, s_thru, s_token = _exchange_start(
        early, [lax.empty((3,) + g.shape[1:], g.dtype) for g in early], d_wup, "scatter_start")
    dx1, dmix, d_ffn_pre, d_mix_post = _ffn_bwd_in(
        dug, duv, wup_g, x1, dy, mix, norm_ffn_pre + s_token[0:1, 0:1], norm_mix_post)
    d_wout = jnp.concatenate([_tn_matmul(mpool[None], dmix[None], "dw_out_pool")[0, 0],
                              _tn_matmul(mattn[None], dmix[None], "dw_out_attn")[0, 0]], axis=0)
    dmp, do, d_attn_scale = _mix_bwd(dmix, wout_f, attn, attn_scale)
    dq, dk, dv = _attn_bwd(q, k, kt, vt, do, totals)
    du, d_wpool, d_pool_scale = _pool_bwd(u, dmp, wp, pool_scale)
    d_win = jnp.stack([_tn_matmul(h1[None], t[None], "dw_in_%d" % n)[0, 0]
                       for n, t in enumerate((du, dq, dk, dv))])
    grad_x, d_mix_pre = _in_proj_bwd(du, dq, dk, dv, win_g, xs, dx1, norm_mix_pre)

    d_convw = jnp.concatenate([dcw_g, dcw_v], axis=0)
    d_convb = jnp.concatenate([dcb_g, dcb_v], axis=0).reshape(1, 2 * D_FF)
    small_parts = dict(norm_mix_pre=d_mix_pre, w_pool=d_wpool, pool_scale=d_pool_scale,
                       attn_scale=d_attn_scale, norm_mix_post=d_mix_post, norm_ffn_pre=d_ffn_pre,
                       conv_b=d_convb, norm_ffn_post=d_post)
    packed = jnp.concatenate([_pack_small(small_parts), d_convw.reshape(_CONVW_ROWS, 128),
                              loss_tile[0:4]], axis=0)
    late, gathered = _scatter_grads([d_win, d_wout.reshape(N_SHARD, D_MODEL // N_SHARD, D_MODEL)], packed)
    early_srcs, early_lands = _exchange_wait(s_send, s_recv, s_thru, True, grad_x, "scatter_wait")
    quarter = [_sum_slots(r, (3, 0, 1, 2), "sum_chips_%d" % n) for n, r in enumerate(late)]
    quarter += [_sum_own_and_received(early_srcs[n], slot, early_lands[n], "sum_chips_%d" % (n + 2))
                for n in range(2)]
    sibling = _swap_with_sibling(quarter)
    small_sum = _sum_slots(gathered, tuple(range(8)), "sum_small")

    results = {}
    for n, name in enumerate(("w_in", "w_out", "w_up", "w_down")):
        res = _adamw([quarter[n], sibling[n]], weights[name][0], mom1[name][0], mom2[name][0],
                     "adamw_" + name)
        results[name] = [t[None] for t in res]
    g_convw = lax.dynamic_slice_in_dim(
        small_sum[_SMALL_ROWS:_SMALL_ROWS + _CONVW_ROWS].reshape(N_SHARD, 3, FF_TILE), shard, 1, axis=0)[0]
    convw_pad = lambda t: jnp.pad(t, ((0, 5), (0, 0)))
    res = _adamw([convw_pad(g_convw)], convw_pad(conv_w[0]), convw_pad(m_conv_w[0]),
                 convw_pad(v_conv_w[0]), "adamw_conv_w")
    results["conv_w"] = [t[:3][None] for t in res]
    pack_w = _pack_small(weights)
    pack_m = _pack_small(mom1)
    pack_v = _pack_small(mom2)
    res = _adamw([small_sum[:_SMALL_ROWS]], pack_w, pack_m, pack_v, "adamw_small")
    unpacked = [_unpack_small(t, weights) for t in res]
    for name in _SMALL:
        results[name] = [t[name] for t in unpacked]

    loss = small_sum[_SMALL_ROWS + _CONVW_ROWS, 0]
    outs = [loss, grad_x[None]]
    for slot in range(4):
        outs.extend(results[name][slot] for name in order)
    return tuple(outs)
```

```python
import functools

import jax
import jax.numpy as jnp
from jax import lax
from jax.experimental import pallas as pl
from jax.experimental.pallas import tpu as pltpu

F32 = jnp.float32
BF16 = jnp.bfloat16

D_MODEL = 1024
D_POOL = 512
D_ATTN = 512
POOL_WINDOWS = (2, 4, 8, 16)
POOL_GROUP = 128
POOL_HALO = 16
CONV_HALO = 8
D_FF = 2816
FF_TILE = 1408
N_SHARD = 4
EPS = 1e-6
Q_SCALE = 0.125
ATT_BLOCK = 256
HEAD_PAIR = 128
MIB = 1 << 20
NEG_BIG = -1e30

ADAM_LR = 0.001
ADAM_B1 = 0.9
ADAM_B2 = 0.999
ADAM_EPS = 1e-08
ADAM_WD = 0.01
ADAM_STEP = 10

NT_DIMS = (((1,), (1,)), ((), ()))
TN_DIMS = (((0,), (0,)), ((), ()))
MESH = pl.DeviceIdType.MESH
ANY = pl.BlockSpec(memory_space=pl.ANY)
HBM_SPEC = pl.BlockSpec(memory_space=pltpu.HBM)
SEM_SPEC = pl.BlockSpec(memory_space=pltpu.SEMAPHORE)
DATAFLOW = pltpu.SideEffectType.DATAFLOW_SIDE_EFFECTING


def _call(body, **kw):
    return pl.pallas_call(body, **kw)


def _params(sem=None, vmem_mb=48):
    return pltpu.CompilerParams(dimension_semantics=sem, vmem_limit_bytes=vmem_mb * MIB)


def _rstd(v):
    return lax.rsqrt(jnp.mean(v * v, axis=-1, keepdims=True) + EPS)


def _dot(a, b):
    return jnp.dot(a, b, preferred_element_type=F32)


def _dot_nt(a, b):
    return lax.dot_general(a, b, NT_DIMS, preferred_element_type=F32)


def _dot_tn(a, b):
    return lax.dot_general(a, b, TN_DIMS, preferred_element_type=F32)


def _row_tile(rows, cap):
    t = min(rows, cap)
    t -= t % 8
    while rows % t:
        t -= 8
    return t


def _full(shape):
    nd = len(shape)
    return pl.BlockSpec(shape, lambda *_: (0,) * nd)


def _chip_peers():
    x, y, c = lax.axis_index("x"), lax.axis_index("y"), lax.axis_index("c")
    return x, y, c, [(1 - x, y), (x, 1 - y), (1 - x, 1 - y)]


def _cast_bf16(a, name):
    def body(a_ref, o_ref):
        o_ref[...] = a_ref[...].astype(BF16)

    return _call(body, name=name, out_shape=jax.ShapeDtypeStruct(a.shape, BF16),
                 grid=(1,), in_specs=[_full(a.shape)], out_specs=_full(a.shape),
                 compiler_params=_params(("arbitrary",)))(a)


def _into_slot(a, slot, dtype, name):
    nd = a.ndim

    def body(slot_ref, a_ref, o_ref):
        o_ref[...] = a_ref[...].astype(dtype)

    return _call(
        body, name=name, out_shape=jax.ShapeDtypeStruct((N_SHARD,) + a.shape, dtype),
        grid_spec=pltpu.PrefetchScalarGridSpec(
            num_scalar_prefetch=1, grid=(1,),
            in_specs=[pl.BlockSpec(a.shape, lambda i, slot_ref: (0,) * nd)],
            out_specs=pl.BlockSpec((None,) + a.shape, lambda i, slot_ref: (slot_ref[0],) + (0,) * nd)),
        compiler_params=_params(("arbitrary",)))(slot, a)


def _exchange_copies(srcs, lands, send, recv):
    x, y, c, chips = _chip_peers()
    copies = []
    for t in range(len(lands)):
        for k, (px, py) in enumerate(chips):
            copies.append(pltpu.make_async_remote_copy(
                src_ref=lands[t].at[2 * x + y] if srcs is None else srcs[t].at[2 * px + py],
                dst_ref=lands[t].at[2 * x + y] if srcs is None else lands[t].at[k],
                send_sem=send.at[3 * t + k], recv_sem=recv.at[3 * t + k],
                device_id=(px, py, c), device_id_type=MESH))
    return copies


def _exchange_start(srcs, lands, after, name):
    n = len(lands)
    operands = list(lands) if srcs is None else list(srcs) + list(lands)
    m = len(operands)

    def body(*refs):
        for cp in _exchange_copies(None if srcs is None else refs[:n], refs[m - n:m], refs[m + 1], refs[m + 2]):
            cp.start()
        refs[-1][...] = jnp.zeros_like(refs[-1])

    res = _call(
        body, name=name,
        out_shape=[pltpu.SemaphoreType.DMA((3 * n,)), pltpu.SemaphoreType.DMA((3 * n,))]
        + [pltpu.HBM(a.shape, a.dtype) for a in operands] + [jax.ShapeDtypeStruct((8, 128), F32)],
        in_specs=[HBM_SPEC] * m + [ANY],
        out_specs=[SEM_SPEC, SEM_SPEC] + [HBM_SPEC] * m + [pl.BlockSpec(memory_space=pltpu.VMEM)],
        input_output_aliases={j: j + 2 for j in range(m)},
        compiler_params=pltpu.CompilerParams(has_side_effects=DATAFLOW),
    )(*[pltpu.with_memory_space_constraint(a, pltpu.HBM) for a in operands], after)
    return res[0], res[1], res[2:2 + m], res[-1]


def _exchange_wait(send, recv, operands, scatter, after, name):
    m = len(operands)
    n = m // 2 if scatter else m

    def body(*refs):
        for cp in _exchange_copies(refs[:n] if scatter else None, refs[m - n:m], refs[m], refs[m + 1]):
            cp.wait_send()
            cp.wait_recv()

    res = _call(
        body, name=name, out_shape=[pltpu.HBM(a.shape, a.dtype) for a in operands],
        in_specs=[HBM_SPEC] * m + [SEM_SPEC, SEM_SPEC, ANY], out_specs=[HBM_SPEC] * m,
        input_output_aliases={j: j for j in range(m)},
        compiler_params=pltpu.CompilerParams(has_side_effects=DATAFLOW),
    )(*operands, send, recv, after)
    return res[:m - n], res[m - n:]


def _gather_shards(shards):
    n = len(shards)

    def body(*refs):
        ins, outs = refs[:n], refs[n:2 * n]
        send, recv, loc = refs[2 * n:]
        x, y, c, chips = _chip_peers()
        b = 2 * x + y
        local = [pltpu.make_async_copy(ins[t], outs[t].at[b], loc.at[t]) for t in range(n)]
        for cp in local:
            cp.start()
        remote = []
        for t in range(n):
            for k, (px, py) in enumerate(chips):
                remote.append(pltpu.make_async_remote_copy(
                    src_ref=ins[t], dst_ref=outs[t].at[b],
                    send_sem=send.at[3 * t + k], recv_sem=recv.at[3 * t + k],
                    device_id=(px, py, c), device_id_type=MESH))
        for cp in remote:
            cp.start()
        for cp in remote:
            cp.wait()
        for cp in local:
            cp.wait()

    return _call(
        body, name="gather_w_in",
        out_shape=[jax.ShapeDtypeStruct((N_SHARD,) + s.shape, s.dtype) for s in shards],
        in_specs=[ANY] * n, out_specs=[ANY] * n,
        scratch_shapes=[pltpu.SemaphoreType.DMA((3 * n,)), pltpu.SemaphoreType.DMA((3 * n,)),
                        pltpu.SemaphoreType.DMA((n,))],
    )(*shards)


def _scatter_grads(grads, small):
    n = len(grads)

    def body(*refs):
        ins, small_in = refs[:n], refs[n]
        outs, small_out = refs[n + 1:2 * n + 1], refs[2 * n + 1]
        send, recv, loc, ssend, srecv = refs[2 * n + 2:]
        x, y, c, chips = _chip_peers()
        b = 2 * x + y
        me = 4 * x + 2 * y + c
        local = [pltpu.make_async_copy(ins[t].at[b], outs[t].at[3], loc.at[t]) for t in range(n)]
        local.append(pltpu.make_async_copy(small_in, small_out.at[me], loc.at[n]))
        for cp in local:
            cp.start()
        remote = []
        for t in range(n):
            for k, (px, py) in enumerate(chips):
                remote.append(pltpu.make_async_remote_copy(
                    src_ref=ins[t].at[2 * px + py], dst_ref=outs[t].at[k],
                    send_sem=send.at[3 * t + k], recv_sem=recv.at[3 * t + k],
                    device_id=(px, py, c), device_id_type=MESH))
        for r in range(1, 8):
            px = 1 - x if r & 4 else x
            py = 1 - y if r & 2 else y
            pc = 1 - c if r & 1 else c
            remote.append(pltpu.make_async_remote_copy(
                src_ref=small_in, dst_ref=small_out.at[me],
                send_sem=ssend.at[r - 1], recv_sem=srecv.at[r - 1],
                device_id=(px, py, pc), device_id_type=MESH))
        for cp in remote:
            cp.start()
        for cp in remote:
            cp.wait()
        for cp in local:
            cp.wait()

    out_shape = [jax.ShapeDtypeStruct(g.shape, g.dtype) for g in grads]
    out_shape.append(jax.ShapeDtypeStruct((8,) + small.shape, small.dtype))
    res = _call(
        body, name="scatter_grads", out_shape=out_shape,
        in_specs=[ANY] * (n + 1), out_specs=[ANY] * (n + 1),
        scratch_shapes=[pltpu.SemaphoreType.DMA((3 * n,)), pltpu.SemaphoreType.DMA((3 * n,)),
                        pltpu.SemaphoreType.DMA((n + 1,)),
                        pltpu.SemaphoreType.DMA((7,)), pltpu.SemaphoreType.DMA((7,))],
    )(*grads, small)
    return res[:n], res[n]


def _swap_with_sibling(parts):
    n = len(parts)

    def body(*refs):
        ins, outs = refs[:n], refs[n:2 * n]
        send, recv = refs[2 * n:]
        x, y, c = lax.axis_index("x"), lax.axis_index("y"), lax.axis_index("c")
        copies = [pltpu.make_async_remote_copy(
            src_ref=ins[t], dst_ref=outs[t], send_sem=send.at[t], recv_sem=recv.at[t],
            device_id=(x, y, 1 - c), device_id_type=MESH) for t in range(n)]
        for cp in copies:
            cp.start()
        for cp in copies:
            cp.wait()

    return _call(
        body, name="swap_sibling",
        out_shape=[jax.ShapeDtypeStruct(p.shape, p.dtype) for p in parts],
        in_specs=[ANY] * n, out_specs=[ANY] * n,
        scratch_shapes=[pltpu.SemaphoreType.DMA((n,)), pltpu.SemaphoreType.DMA((n,))],
    )(*parts)


def _sum_slots(buf, order, name):
    k, rows, cols = buf.shape
    tr = _row_tile(rows, 256)

    def body(b_ref, o_ref):
        acc = b_ref[order[0]].astype(F32)
        for s in order[1:]:
            acc = acc + b_ref[s].astype(F32)
        o_ref[...] = acc

    return _call(body, name=name, out_shape=jax.ShapeDtypeStruct((rows, cols), F32),
                 grid=(rows // tr,),
                 in_specs=[pl.BlockSpec((k, tr, cols), lambda i: (0, i, 0))],
                 out_specs=pl.BlockSpec((tr, cols), lambda i: (i, 0)),
                 compiler_params=_params(("parallel",)))(buf)


def _sum_own_and_received(src, slot, land, name):
    _, rows, cols = src.shape
    tr = _row_tile(rows, 256)

    def body(slot_ref, s_ref, l_ref, o_ref):
        acc = s_ref[...].astype(F32)
        for k in range(3):
            acc = acc + l_ref[k].astype(F32)
        o_ref[...] = acc

    return _call(
        body, name=name, out_shape=jax.ShapeDtypeStruct((rows, cols), F32),
        grid_spec=pltpu.PrefetchScalarGridSpec(
            num_scalar_prefetch=1, grid=(rows // tr,),
            in_specs=[pl.BlockSpec((None, tr, cols), lambda i, slot_ref: (slot_ref[0], i, 0)),
                      pl.BlockSpec((3, tr, cols), lambda i, slot_ref: (0, i, 0))],
            out_specs=pl.BlockSpec((tr, cols), lambda i, slot_ref: (i, 0))),
        compiler_params=_params(("parallel",)))(slot, src, land)


def _adamw(grad_parts, w, m, v, name):
    rows, cols = w.shape
    tr = _row_tile(rows, 256)
    npart = len(grad_parts)

    def body(*refs):
        gp = refs[:npart]
        w_ref, m_ref, v_ref, g_out, d_out, m_out, v_out = refs[npart:]
        g = gp[0][...]
        for p in gp[1:]:
            g = g + p[...]
        mm = ADAM_B1 * m_ref[...] + (1.0 - ADAM_B1) * g
        vv = ADAM_B2 * v_ref[...] + (1.0 - ADAM_B2) * jnp.square(g)
        m_hat = mm / (1.0 - ADAM_B1 ** ADAM_STEP)
        v_hat = vv / (1.0 - ADAM_B2 ** ADAM_STEP)
        g_out[...] = g
        d_out[...] = -ADAM_LR * (m_hat / (jnp.sqrt(v_hat) + ADAM_EPS) + ADAM_WD * w_ref[...])
        m_out[...] = mm
        v_out[...] = vv

    spec = pl.BlockSpec((tr, cols), lambda i: (i, 0))
    shp = jax.ShapeDtypeStruct((rows, cols), F32)
    return _call(body, name=name, out_shape=[shp] * 4, grid=(rows // tr,),
                 in_specs=[spec] * (npart + 3), out_specs=[spec] * 4,
                 compiler_params=_params(("parallel",)))(*grad_parts, w, m, v)


def _in_proj(x, g1, w_in):
    s = x.shape[0]
    ts = 512

    def body(x_ref, g_ref, w_ref, u_ref, q_ref, k_ref, v_ref, kt_ref, vt_ref, h_ref):
        xv = x_ref[...]
        h = (xv * _rstd(xv) * g_ref[...]).astype(BF16)
        h_ref[...] = h
        u_ref[...] = _dot(h, w_ref[0])
        q_ref[...] = (_dot(h, w_ref[1]) * Q_SCALE).astype(BF16)
        k = _dot(h, w_ref[2])
        k_ref[...] = k.astype(BF16)
        v = _dot(h, w_ref[3])
        v_ref[...] = v.astype(BF16)
        for src, dst in ((k, kt_ref), (v, vt_ref)):
            src_t = src.T.astype(BF16)
            for n in range(ts // ATT_BLOCK):
                dst[n] = src_t[:, n * ATT_BLOCK:(n + 1) * ATT_BLOCK]

    row = lambda w: pl.BlockSpec((ts, w), lambda i: (i, 0))
    half = jax.ShapeDtypeStruct((s, D_POOL), BF16)
    return _call(
        body, name="in_proj",
        out_shape=[jax.ShapeDtypeStruct((s, D_POOL), F32), half, half, half,
                   jax.ShapeDtypeStruct((s // ATT_BLOCK, D_ATTN, ATT_BLOCK), BF16),
                   jax.ShapeDtypeStruct((s // ATT_BLOCK, D_ATTN, ATT_BLOCK), BF16),
                   jax.ShapeDtypeStruct((s, D_MODEL), BF16)],
        grid=(s // ts,),
        in_specs=[row(D_MODEL), _full((1, D_MODEL)), _full(w_in.shape)],
        out_specs=[row(D_POOL)] * 4
        + [pl.BlockSpec((ts // ATT_BLOCK, D_ATTN, ATT_BLOCK), lambda i: (i, 0, 0))] * 2 + [row(D_MODEL)],
        compiler_params=_params(("parallel",)))(x, g1, w_in)


def _pool_means(ext_ref, g, window, ts, row0):
    cols = slice(g * POOL_GROUP, (g + 1) * POOL_GROUP)
    cur = ext_ref[POOL_HALO:POOL_HALO + ts, cols]
    acc = cur
    for d in range(1, window):
        acc = acc + ext_ref[POOL_HALO - d:POOL_HALO - d + ts, cols]
    t1 = row0 + 1 + lax.broadcasted_iota(jnp.int32, (ts, 1), 0)
    cnt = jnp.minimum(t1, window).astype(F32)
    return acc / cnt - cur, cnt


def _pool_fwd(u, w_pool, pool_scale):
    s = u.shape[0]
    ts = 512
    per = ts // POOL_HALO

    def body(u_ref, halo_ref, wp_ref, ps_ref, o_ref, ext_ref, y_ref):
        i = pl.program_id(0)
        ext_ref[0:POOL_HALO, :] = jnp.where(i > 0, halo_ref[...], 0.0)
        ext_ref[POOL_HALO:, :] = u_ref[...]
        for g, window in enumerate(POOL_WINDOWS):
            p, _ = _pool_means(ext_ref, g, window, ts, i * ts)
            y_ref[:, g * POOL_GROUP:(g + 1) * POOL_GROUP] = _dot(
                p.astype(BF16), wp_ref[g].astype(BF16))
        y = y_ref[...]
        o_ref[...] = (y * _rstd(y) * ps_ref[...]).astype(BF16)

    return _call(
        body, name="pool_fwd", out_shape=jax.ShapeDtypeStruct((s, D_POOL), BF16),
        grid=(s // ts,),
        in_specs=[pl.BlockSpec((ts, D_POOL), lambda i: (i, 0)),
                  pl.BlockSpec((POOL_HALO, D_POOL), lambda i: (jnp.maximum(i * per - 1, 0), 0)),
                  _full(w_pool.shape), _full((1, D_POOL))],
        out_specs=pl.BlockSpec((ts, D_POOL), lambda i: (i, 0)),
        scratch_shapes=[pltpu.VMEM((ts + POOL_HALO, D_POOL), F32), pltpu.VMEM((ts, D_POOL), F32)],
        compiler_params=_params(("parallel",)))(u, u, w_pool, pool_scale)


def _tri(kind):
    r = lax.broadcasted_iota(jnp.int32, (ATT_BLOCK, ATT_BLOCK), 0)
    c = lax.broadcasted_iota(jnp.int32, (ATT_BLOCK, ATT_BLOCK), 1)
    return jnp.where(r >= c if kind == "suffix" else r <= c, 1.0, 0.0).astype(BF16)


def _causal_mask():
    r = lax.broadcasted_iota(jnp.int32, (ATT_BLOCK, ATT_BLOCK), 0)
    c = lax.broadcasted_iota(jnp.int32, (ATT_BLOCK, ATT_BLOCK), 1)
    return c < r


def _softplus(z, with_sigmoid=False):
    ope = 1.0 + jnp.exp(jnp.minimum(z, 80.0))
    sp = jnp.maximum(z, jnp.log(ope))
    if with_sigmoid:
        return sp, 1.0 - 1.0 / ope
    return sp


def _attn_fwd(q, k, vt):
    s = q.shape[0]
    tb = ATT_BLOCK
    nq = s // tb

    def body(q_ref, k_ref, vt_ref, o_ref, t_ref, *bufs):
        i = pl.program_id(1)
        suffix = _tri("prefix")
        r_idx = lax.broadcasted_iota(jnp.int32, (tb, tb), 0)
        c_idx = lax.broadcasted_iota(jnp.int32, (tb, tb), 1)
        causal = r_idx < c_idx
        lane = lax.broadcasted_iota(jnp.int32, (1, HEAD_PAIR), 1)
        first = lane < 64
        top = lax.broadcasted_iota(jnp.int32, (HEAD_PAIR, 1), 0) < 64
        q2 = q_ref[...]
        zero = jnp.zeros_like(q2)
        qs_t = tuple(jnp.where(first, q2, zero).astype(F32).T.astype(BF16) if e == 0 else
                     jnp.where(first, zero, q2).astype(F32).T.astype(BF16) for e in range(2))

        def values_t(j):
            vt = vt_ref[j]
            none = jnp.zeros_like(vt)
            return jnp.concatenate([jnp.where(top, vt, none), jnp.where(top, none, vt)], axis=1)

        z_ring = [[bufs[2 * slot + e] for e in range(2)] for slot in range(4)]
        in_buf = [[bufs[8 + 2 * slot + e] for e in range(2)] for slot in range(2)]
        pr_buf = [bufs[12], bufs[13]]

        def block_of(p):
            return jnp.clip(i - p, 0, nq - 1)

        def trip(t, u, carry):
            first_trips = isinstance(t, int)
            w, r = u % 2, 1 - u % 2
            cs, o = list(carry[0:2]), carry[2]
            if not first_trips or t >= 3:
                o = o + jnp.where(t - 3 <= i, _dot(values_t(block_of(t - 3)), pr_buf[r][...]), 0.0)
            kj = k_ref[pl.ds(pl.multiple_of(block_of(t) * tb, tb), tb), :]
            for e in range(2):
                z_ring[u][e][...] = _dot(kj, qs_t[e])
            if not first_trips or t >= 1:
                for e in range(2):
                    sp = _softplus(z_ring[(u - 1) % 4][e][...])
                    if first_trips and t == 1:
                        sp = jnp.where(causal, sp, 0.0)
                    in_buf[w][e][...] = _dot(suffix, sp.astype(BF16))
            if not first_trips or t >= 2:
                live2 = t - 2 <= i
                for e in range(2):
                    incl = in_buf[r][e][...]
                    arg = z_ring[(u - 2) % 4][e][...] - incl - jnp.where(live2, cs[e], -NEG_BIG)
                    if first_trips and t == 2:
                        arg = jnp.where(causal, arg, NEG_BIG)
                    pr_buf[w][e * tb:(e + 1) * tb, :] = jnp.exp(arg).astype(BF16)
                    cs[e] = jnp.where(live2, cs[e] + incl[0:1, :], cs[e])
            return cs[0], cs[1], o

        def four_trips(n, cr):
            for u in range(4):
                cr = trip(4 * n + u, u, cr)
            return cr

        row = jnp.zeros((1, tb), F32)
        carry = four_trips(0, (row, row, jnp.zeros((HEAD_PAIR, tb), F32)))
        carry = lax.fori_loop(1, (i + 7) // 4, four_trips, carry)
        o_ref[...] = carry[2].T
        totals = jnp.where(r_idx == 0, carry[0], jnp.where(r_idx == 1, carry[1], 0.0))
        t_ref[...] = totals.T[:, 0:2]

    score_buf = pltpu.VMEM((tb, tb), F32)
    return _call(
        body, name="attn_fwd",
        out_shape=[jax.ShapeDtypeStruct((s, D_ATTN), F32),
                   jax.ShapeDtypeStruct((4, s, 2), F32)],
        grid=(4, nq),
        in_specs=[pl.BlockSpec((tb, HEAD_PAIR), lambda h, i: (i, h)),
                  pl.BlockSpec((s, HEAD_PAIR), lambda h, i: (0, h)),
                  pl.BlockSpec((nq, HEAD_PAIR, tb), lambda h, i: (0, h, 0))],
        out_specs=[pl.BlockSpec((tb, HEAD_PAIR), lambda h, i: (i, h)),
                   pl.BlockSpec((None, tb, 2), lambda h, i: (h, i, 0))],
        scratch_shapes=[score_buf] * 12 + [pltpu.VMEM((2 * tb, tb), BF16)] * 2,
        compiler_params=_params(("arbitrary", "arbitrary")))(q, k, vt)


def _mix_out(attn, mpool, x, attn_scale, w_out, g2, g3):
    s = x.shape[0]
    ts = 512

    def body(a_ref, p_ref, x_ref, as_ref, w_ref, g2_ref, g3_ref, ma_ref, mix_ref, x1_ref, h2_ref):
        ao = a_ref[...]
        ma = (ao * _rstd(ao) * as_ref[...]).astype(BF16)
        ma_ref[...] = ma
        mix = _dot(p_ref[...], w_ref[0:D_POOL, :]) + _dot(ma, w_ref[D_POOL:, :])
        mix_ref[...] = mix
        x1 = x_ref[...] + mix * _rstd(mix) * g2_ref[...]
        x1_ref[...] = x1
        h2_ref[...] = (x1 * _rstd(x1) * g3_ref[...]).astype(BF16)

    row = lambda w: pl.BlockSpec((ts, w), lambda i: (i, 0))
    return _call(
        body, name="mix_out",
        out_shape=[jax.ShapeDtypeStruct((s, D_ATTN), BF16), jax.ShapeDtypeStruct((s, D_MODEL), F32),
                   jax.ShapeDtypeStruct((s, D_MODEL), F32), jax.ShapeDtypeStruct((s, D_MODEL), BF16)],
        grid=(s // ts,),
        in_specs=[row(D_ATTN), row(D_POOL), row(D_MODEL), _full((1, D_ATTN)),
                  _full((D_MODEL, D_MODEL)), _full((1, D_MODEL)), _full((1, D_MODEL))],
        out_specs=[row(D_ATTN), row(D_MODEL), row(D_MODEL), row(D_MODEL)],
        compiler_params=_params(("parallel",)))(attn, mpool, x, attn_scale, w_out, g2, g3)


def _earlier_rows(halo, block):
    ts = block.shape[0]
    ext = jnp.concatenate([halo, block], axis=0)
    return tuple(pltpu.roll(ext, d, axis=0)[CONV_HALO:CONV_HALO + ts, :] for d in (1, 2))


def _later_rows(block, halo):
    ts = block.shape[0]
    ext = jnp.concatenate([block, halo], axis=0)
    return tuple(pltpu.roll(ext, ts + CONV_HALO - d, axis=0)[0:ts, :] for d in (1, 2))


def _conv_rows(x0, x1, x2, cw, cb):
    y = cb + cw[0:1, :] * x2
    y = y + cw[1:2, :] * x1
    return y + cw[2:3, :] * x0


def _sigmoid(v):
    return 1.0 / (1.0 + jnp.exp(-v))


def _ffn_up(h2, w_up, conv_w, conv_b):
    s = h2.shape[0]
    ts = 256
    tn = FF_TILE

    def body(h_ref, wg_ref, wv_ref, cwg_ref, cwv_ref, cbg_ref, cbv_ref,
             ug_ref, uv_ref, f_ref, tailg, tailv):
        i = pl.program_id(1)

        @pl.when(i == 0)
        def _():
            tailg[...] = jnp.zeros_like(tailg)
            tailv[...] = jnp.zeros_like(tailv)

        h = h_ref[...]
        ug = _dot(h, wg_ref[...])
        uv = _dot(h, wv_ref[...])
        ug_ref[...] = ug
        uv_ref[...] = uv
        gate = _conv_rows(ug, *_earlier_rows(tailg[...], ug), cwg_ref[...], cbg_ref[...])
        val = _conv_rows(uv, *_earlier_rows(tailv[...], uv), cwv_ref[...], cbv_ref[...])
        f_ref[...] = (gate * _sigmoid(gate) * val).astype(BF16)
        tailg[...] = ug[ts - CONV_HALO:, :]
        tailv[...] = uv[ts - CONV_HALO:, :]

    out_blk = pl.BlockSpec((None, ts, tn), lambda n, i: (n, i, 0))
    act = jax.ShapeDtypeStruct((2, s, tn), F32)
    return _call(
        body, name="ffn_up",
        out_shape=[act, act, jax.ShapeDtypeStruct((2, s, tn), BF16)],
        grid=(2, s // ts),
        in_specs=[pl.BlockSpec((ts, D_MODEL), lambda n, i: (i, 0)),
                  pl.BlockSpec((None, D_MODEL, tn), lambda n, i: (n, 0, 0)),
                  pl.BlockSpec((None, D_MODEL, tn), lambda n, i: (n + 2, 0, 0)),
                  pl.BlockSpec((None, 3, tn), lambda n, i: (n, 0, 0)),
                  pl.BlockSpec((None, 3, tn), lambda n, i: (n + 2, 0, 0)),
                  pl.BlockSpec((None, 1, tn), lambda n, i: (n, 0, 0)),
                  pl.BlockSpec((None, 1, tn), lambda n, i: (n + 2, 0, 0))],
        out_specs=[out_blk, out_blk, out_blk],
        scratch_shapes=[pltpu.VMEM((CONV_HALO, tn), F32), pltpu.VMEM((CONV_HALO, tn), F32)],
        compiler_params=_params(("arbitrary", "arbitrary")))(
            h2, w_up, w_up, conv_w, conv_w, conv_b, conv_b)


def _ffn_down(f_in, w_down, x1, target, g4):
    s = x1.shape[0]
    ts = 512

    def body(f_ref, w_ref, x1_ref, t_ref, g_ref, df_ref, dy_ref, loss_ref, dg_ref):
        @pl.when(pl.program_id(0) == 0)
        def _():
            loss_ref[...] = jnp.zeros_like(loss_ref)
            dg_ref[...] = jnp.zeros_like(dg_ref)

        f = _dot(f_ref[0], w_ref[0:FF_TILE, :]) + _dot(f_ref[1], w_ref[FF_TILE:, :])
        rf = _rstd(f)
        fn = f * rf
        g = g_ref[...]
        err = (x1_ref[...] + fn * g) - t_ref[...]
        loss_ref[...] += 0.5 * jnp.sum(jnp.mean(err * err, axis=-1))
        dy = err * (1.0 / D_MODEL)
        dy_ref[...] = dy
        dg_ref[...] += jnp.sum(dy * fn, axis=0, keepdims=True)
        dfn = dy * g
        df_ref[...] = (rf * (dfn - fn * jnp.mean(dfn * fn, axis=-1, keepdims=True))).astype(BF16)

    row = pl.BlockSpec((ts, D_MODEL), lambda i: (i, 0))
    return _call(
        body, name="ffn_down",
        out_shape=[jax.ShapeDtypeStruct((s, D_MODEL), BF16), jax.ShapeDtypeStruct((s, D_MODEL), F32),
                   jax.ShapeDtypeStruct((8, 128), F32), jax.ShapeDtypeStruct((1, D_MODEL), F32)],
        grid=(s // ts,),
        in_specs=[pl.BlockSpec((2, ts, FF_TILE), lambda i: (0, i, 0)), _full((D_FF, D_MODEL)),
                  row, row, _full((1, D_MODEL))],
        out_specs=[row, row, _full((8, 128)), _full((1, D_MODEL))],
        compiler_params=_params(("arbitrary",)))(f_in, w_down, x1, target, g4)


def _tn_matmul(a, b, name, ts=512):
    na, s, ka = a.shape
    nb, _, nbc = b.shape
    steps = s // ts

    def body(a_ref, b_ref, o_ref, acc_ref):
        @pl.when(pl.program_id(2) == 0)
        def _():
            acc_ref[...] = jnp.zeros_like(acc_ref)

        acc_ref[...] += _dot_tn(a_ref[...].astype(BF16), b_ref[...].astype(BF16))

        @pl.when(pl.program_id(2) == steps - 1)
        def _():
            o_ref[...] = acc_ref[...].astype(BF16)

    return _call(
        body, name=name, out_shape=jax.ShapeDtypeStruct((na, nb, ka, nbc), BF16),
        grid=(na, nb, steps),
        in_specs=[pl.BlockSpec((None, ts, ka), lambda i, j, r: (i, r, 0)),
                  pl.BlockSpec((None, ts, nbc), lambda i, j, r: (j, r, 0))],
        out_specs=pl.BlockSpec((None, None, ka, nbc), lambda i, j, r: (i, j, 0, 0)),
        scratch_shapes=[pltpu.VMEM((ka, nbc), F32)],
        compiler_params=_params(("parallel", "parallel", "arbitrary")))(a, b)


def _ffn_bwd_act(df, w_down, upre_g, upre_v, conv_w, conv_b):
    s = df.shape[0]
    ts = 256
    tn = FF_TILE
    nr = s // ts
    per = ts // CONV_HALO

    def body(df_ref, wd_ref, ug_ref, uv_ref, hg_ref, hv_ref, cwg_ref, cwv_ref, cbg_ref, cbv_ref,
             dug_ref, duv_ref, dcwg_ref, dcwv_ref, dcbg_ref, dcbv_ref, headg, headv):
        i = pl.program_id(1)
        first_rows = i == nr - 1

        @pl.when(i == 0)
        def _():
            for r in (headg, headv, dcwg_ref, dcwv_ref, dcbg_ref, dcbv_ref):
                r[...] = jnp.zeros_like(r)

        ug, uv = ug_ref[...], uv_ref[...]
        cwg, cwv = cwg_ref[...], cwv_ref[...]
        gate = _conv_rows(ug, *_earlier_rows(jnp.where(first_rows, 0.0, hg_ref[...]), ug), cwg, cbg_ref[...])
        val = _conv_rows(uv, *_earlier_rows(jnp.where(first_rows, 0.0, hv_ref[...]), uv), cwv, cbv_ref[...])
        sg = _sigmoid(gate)
        dfin = _dot_nt(df_ref[...], wd_ref[...])
        dval = dfin * (gate * sg)
        dgate = dfin * val * (sg * (1.0 + gate * (1.0 - sg)))

        def conv_bwd(dact, x, head, cw, dcw_ref, dcb_ref, du_ref):
            d1, d2 = _later_rows(dact, head[...])
            dcb_ref[...] += jnp.sum(dact, axis=0, keepdims=True)
            for kk, shifted in enumerate((d2, d1, dact)):
                dcw_ref[kk:kk + 1, :] += jnp.sum(x * shifted, axis=0, keepdims=True)
            du_ref[...] = (cw[2:3, :] * dact + cw[1:2, :] * d1 + cw[0:1, :] * d2).astype(BF16)
            head[...] = dact[0:CONV_HALO, :]

        conv_bwd(dgate, ug, headg, cwg, dcwg_ref, dcbg_ref, dug_ref)
        conv_bwd(dval, uv, headv, cwv, dcwv_ref, dcbv_ref, duv_ref)

    rows = lambda n, i: (n, nr - 1 - i, 0)
    halo = lambda n, i: (n, jnp.maximum((nr - 1 - i) * per - 1, 0), 0)
    act_blk = pl.BlockSpec((None, ts, tn), rows)
    halo_blk = pl.BlockSpec((None, CONV_HALO, tn), halo)
    cw_blk = lambda off: pl.BlockSpec((None, 3, tn), lambda n, i: (n + off, 0, 0))
    cb_blk = lambda off: pl.BlockSpec((None, 1, tn), lambda n, i: (n + off, 0, 0))
    acc_w = pl.BlockSpec((None, 3, tn), lambda n, i: (n, 0, 0))
    acc_b = pl.BlockSpec((None, 1, tn), lambda n, i: (n, 0, 0))
    dact = jax.ShapeDtypeStruct((2, s, tn), BF16)
    return _call(
        body, name="ffn_bwd_act",
        out_shape=[dact, dact, jax.ShapeDtypeStruct((2, 3, tn), F32), jax.ShapeDtypeStruct((2, 3, tn), F32),
                   jax.ShapeDtypeStruct((2, 1, tn), F32), jax.ShapeDtypeStruct((2, 1, tn), F32)],
        grid=(2, nr),
        in_specs=[pl.BlockSpec((ts, D_MODEL), lambda n, i: (nr - 1 - i, 0)),
                  pl.BlockSpec((tn, D_MODEL), lambda n, i: (n, 0)),
                  act_blk, act_blk, halo_blk, halo_blk,
                  cw_blk(0), cw_blk(2), cb_blk(0), cb_blk(2)],
        out_specs=[act_blk, act_blk, acc_w, acc_w, acc_b, acc_b],
        scratch_shapes=[pltpu.VMEM((CONV_HALO, tn), F32)] * 2,
        compiler_params=_params(("arbitrary", "arbitrary")))(
            df, w_down, upre_g, upre_v, upre_g, upre_v, conv_w, conv_w, conv_b, conv_b)


def _ffn_bwd_in(dug, duv, w_up, x1, dy, mix, g3, g2):
    s = x1.shape[0]
    ts = 256

    def body(dg_ref, dv_ref, w_ref, x1_ref, dy_ref, mix_ref, g3_ref, g2_ref,
             dx1_ref, dmix_ref, dg3_ref, dg2_ref):
        @pl.when(pl.program_id(0) == 0)
        def _():
            dg3_ref[...] = jnp.zeros_like(dg3_ref)
            dg2_ref[...] = jnp.zeros_like(dg2_ref)

        dh = _dot_nt(dg_ref[0], w_ref[0]) + _dot_nt(dg_ref[1], w_ref[1])
        dh = dh + _dot_nt(dv_ref[0], w_ref[2]) + _dot_nt(dv_ref[1], w_ref[3])
        x1 = x1_ref[...]
        r3 = _rstd(x1)
        xn = x1 * r3
        dg3_ref[...] += jnp.sum(dh * xn, axis=0, keepdims=True)
        dxn = dh * g3_ref[...]
        dx1 = dy_ref[...] + r3 * (dxn - xn * jnp.mean(dxn * xn, axis=-1, keepdims=True))
        dx1_ref[...] = dx1
        mix = mix_ref[...]
        rm = _rstd(mix)
        mn = mix * rm
        dg2_ref[...] += jnp.sum(dx1 * mn, axis=0, keepdims=True)
        dmn = dx1 * g2_ref[...]
        dmix_ref[...] = (rm * (dmn - mn * jnp.mean(dmn * mn, axis=-1, keepdims=True))).astype(BF16)

    row = pl.BlockSpec((ts, D_MODEL), lambda i: (i, 0))
    act = pl.BlockSpec((2, ts, FF_TILE), lambda i: (0, i, 0))
    vec = _full((1, D_MODEL))
    return _call(
        body, name="ffn_bwd_in",
        out_shape=[jax.ShapeDtypeStruct((s, D_MODEL), F32), jax.ShapeDtypeStruct((s, D_MODEL), BF16),
                   jax.ShapeDtypeStruct((1, D_MODEL), F32), jax.ShapeDtypeStruct((1, D_MODEL), F32)],
        grid=(s // ts,),
        in_specs=[act, act, _full(w_up.shape), row, row, row, vec, vec],
        out_specs=[row, row, vec, vec],
        compiler_params=_params(("arbitrary",), vmem_mb=56))(dug, duv, w_up, x1, dy, mix, g3, g2)


def _mix_bwd(dmix, w_out, attn, attn_scale):
    s = dmix.shape[0]
    ts = 512

    def body(dm_ref, w_ref, a_ref, as_ref, dp_ref, do_ref, das_ref):
        @pl.when(pl.program_id(0) == 0)
        def _():
            das_ref[...] = jnp.zeros_like(das_ref)

        dm = dm_ref[...]
        dp_ref[...] = _dot_nt(dm, w_ref[0:D_POOL, :])
        da = _dot_nt(dm, w_ref[D_POOL:, :])
        ao = a_ref[...]
        ra = _rstd(ao)
        an = ao * ra
        das_ref[...] += jnp.sum(da * an, axis=0, keepdims=True)
        dan = da * as_ref[...]
        do_ref[...] = (ra * (dan - an * jnp.mean(dan * an, axis=-1, keepdims=True))).astype(BF16)

    row = lambda w: pl.BlockSpec((ts, w), lambda i: (i, 0))
    return _call(
        body, name="mix_bwd",
        out_shape=[jax.ShapeDtypeStruct((s, D_POOL), F32), jax.ShapeDtypeStruct((s, D_ATTN), BF16),
                   jax.ShapeDtypeStruct((1, D_ATTN), F32)],
        grid=(s // ts,),
        in_specs=[row(D_MODEL), _full((D_MODEL, D_MODEL)), row(D_ATTN), _full((1, D_ATTN))],
        out_specs=[row(D_POOL), row(D_ATTN), _full((1, D_ATTN))],
        compiler_params=_params(("arbitrary",)))(dmix, w_out, attn, attn_scale)


def _attn_bwd(q, k, kt, vt, do, totals):
    s = q.shape[0]
    tb = ATT_BLOCK
    nq = s // tb

    def body(q_ref, do_ref, t_ref, k_hbm, kt_hbm, vt_hbm, dq_ref, dk_hbm, dv_hbm,
             k_scr, kt_scr, vt_scr, dkt_acc, dvt_acc, stage, *bufs):
        hp = pl.program_id(0)
        i = pl.program_id(1)
        lanes = pl.ds(pl.multiple_of(hp * HEAD_PAIR, HEAD_PAIR), HEAD_PAIR)

        @pl.when(i == 0)
        def _():
            pltpu.sync_copy(k_hbm.at[:, lanes], k_scr)
            pltpu.sync_copy(kt_hbm.at[:, lanes, :], kt_scr)
            pltpu.sync_copy(vt_hbm.at[:, lanes, :], vt_scr)
            dkt_acc[...] = jnp.zeros_like(dkt_acc)
            dvt_acc[...] = jnp.zeros_like(dvt_acc)

        upper = _tri("suffix")
        lower = _tri("prefix")
        causal = _causal_mask()
        lane = lax.broadcasted_iota(jnp.int32, (1, HEAD_PAIR), 1)
        first = lane < 64
        q2 = q_ref[...]
        do2 = do_ref[...]
        zero = jnp.zeros_like(q2)
        qs = (jnp.where(first, q2, zero), jnp.where(first, zero, q2))
        dos = (jnp.where(first, do2, zero), jnp.where(first, zero, do2))
        qcat_t = jnp.concatenate(qs, axis=0).astype(F32).T.astype(BF16)
        docat_t = jnp.concatenate(dos, axis=0).astype(F32).T.astype(BF16)
        tots = (t_ref[:, 0:1], t_ref[:, 1:2])

        z_ring, sg_ring = [[[bufs[8 * kind + 2 * slot + e] for e in range(2)] for slot in range(4)]
                           for kind in range(2)]
        in_buf, da_buf, dw_buf, pre_buf = [
            [[bufs[16 + 4 * kind + 2 * slot + e] for e in range(2)] for slot in range(2)]
            for kind in range(4)]
        pr_buf, dzr_buf, dzc_buf = bufs[32:34], bufs[34:36], bufs[36:38]

        for e in range(2):
            for slot in (2, 3):
                z_ring[slot][e][...] = jnp.full((tb, tb), NEG_BIG, F32)
            for slot in (1, 2, 3):
                sg_ring[slot][e][...] = jnp.zeros((tb, tb), F32)
            for buf in (in_buf, da_buf, dw_buf, pre_buf):
                buf[1][e][...] = jnp.zeros((tb, tb), F32)
        for buf in (pr_buf, dzr_buf, dzc_buf):
            buf[1][...] = jnp.zeros_like(buf[1])

        def rows(p):
            return pl.ds(pl.multiple_of(jnp.clip(p, 0, nq - 1) * tb, tb), tb)

        def split_heads(block):
            return jnp.concatenate([jnp.where(first, block, zero), jnp.where(first, zero, block)], axis=0)

        def trip(t, u, carry):
            w, r = u % 2, 1 - u % 2
            cs, cps, dq = list(carry[0:2]), list(carry[2:4]), carry[4]
            live4 = (t - 4 >= 0) & (t - 4 < i)
            dq = dq + jnp.where(live4, _dot(dzc_buf[r][...], split_heads(k_scr[rows(t - 4), :])), 0.0)
            dkt_acc[jnp.clip(t - 4, 0, nq - 1)] += jnp.where(live4, _dot(qcat_t, dzr_buf[r][...]), 0.0)
            dvt_acc[jnp.clip(t - 3, 0, nq - 1)] += _dot(docat_t, pr_buf[r][...])
            ktj = kt_scr[jnp.clip(t, 0, nq - 1)]
            for e in range(2):
                z_ring[u][e][...] = _dot(qs[e], ktj)
            vtj = vt_scr[jnp.clip(t - 1, 0, nq - 1)]
            for e in range(2):
                sp, sig = _softplus(z_ring[(u - 1) % 4][e][...], True)
                sg_ring[(u - 1) % 4][e][...] = sig
                in_buf[w][e][...] = _dot(sp.astype(BF16), upper)
                da_buf[w][e][...] = _dot(dos[e], vtj)
            for e in range(2):
                incl = in_buf[r][e][...]
                cs[e] = cs[e] + incl[:, 0:1]
                off = jnp.where(t - 2 < i, tots[e] - cs[e], -NEG_BIG)
                a = jnp.exp(z_ring[(u - 2) % 4][e][...] - incl - off)
                dw = a * da_buf[r][e][...]
                dw_buf[w][e][...] = dw
                pr_buf[w][e * tb:(e + 1) * tb, :] = a.astype(BF16)
                pre_buf[w][e][...] = _dot(dw.astype(BF16), lower)
            for e in range(2):
                pre = pre_buf[r][e][...] + cps[e]
                dzb = (dw_buf[r][e][...] - sg_ring[(u - 3) % 4][e][...] * pre).astype(BF16)
                cps[e] = pre[:, tb - 1:tb]
                dzr_buf[w][e * tb:(e + 1) * tb, :] = dzb
                dzc_buf[w][:, e * tb:(e + 1) * tb] = dzb
            return cs[0], cs[1], cps[0], cps[1], dq

        col = jnp.zeros((tb, 1), F32)
        carry = (col, col, col, col, jnp.zeros((tb, HEAD_PAIR), F32))
        def four_trips(n, cr):
            for u in range(4):
                cr = trip(4 * n + u, u, cr)
            return cr

        carry = lax.fori_loop(0, jnp.where(i > 0, (i + 7) // 4, 0), four_trips, carry)

        cps, dq = carry[2:4], carry[4]
        kj = k_scr[rows(i), :]
        dzs, probs = [], []
        for e in range(2):
            z = _dot(qs[e], kt_scr[i])
            sp, sig = _softplus(z, True)
            incl = _dot(jnp.where(causal, sp, 0.0).astype(BF16), upper)
            a = jnp.where(causal, jnp.exp(z - incl), 0.0)
            dw = a * _dot(dos[e], vt_scr[i])
            pre = _dot(dw.astype(BF16), lower) + cps[e]
            dzs.append(jnp.where(causal, dw - sig * pre, 0.0).astype(BF16))
            probs.append(a.astype(BF16))
        dq = dq + _dot(jnp.concatenate(dzs, axis=1), split_heads(kj))
        dkt_acc[i] += _dot(qcat_t, jnp.concatenate(dzs, axis=0))
        dvt_acc[i] += _dot(docat_t, jnp.concatenate(probs, axis=0))
        dq_ref[...] = (dq * Q_SCALE).astype(BF16)

        @pl.when(i == nq - 1)
        def _():
            for acc, dst in ((dkt_acc, dk_hbm), (dvt_acc, dv_hbm)):
                def flip(n, _, acc=acc):
                    at = pl.ds(pl.multiple_of(n * tb, tb), tb)
                    stage[at, :] = acc[n].T
                    return 0
                lax.fori_loop(0, nq, flip, 0)
                pltpu.sync_copy(stage, dst.at[:, lanes])

    blk = pl.BlockSpec((tb, HEAD_PAIR), lambda h, i: (i, h))
    grad = jax.ShapeDtypeStruct((s, D_ATTN), F32)
    return _call(
        body, name="attn_bwd",
        out_shape=[jax.ShapeDtypeStruct((s, D_ATTN), BF16), grad, grad],
        grid=(4, nq),
        in_specs=[blk, blk, pl.BlockSpec((None, tb, 2), lambda h, i: (h, i, 0)), ANY, ANY, ANY],
        out_specs=[blk, ANY, ANY],
        scratch_shapes=[pltpu.VMEM((s, HEAD_PAIR), BF16), pltpu.VMEM((nq, HEAD_PAIR, tb), BF16),
                        pltpu.VMEM((nq, HEAD_PAIR, tb), BF16),
                        pltpu.VMEM((nq, HEAD_PAIR, tb), F32), pltpu.VMEM((nq, HEAD_PAIR, tb), F32),
                        pltpu.VMEM((s, HEAD_PAIR), F32)]
        + [pltpu.VMEM((tb, tb), F32)] * 32
        + [pltpu.VMEM((2 * tb, tb), BF16)] * 4 + [pltpu.VMEM((tb, 2 * tb), BF16)] * 2,
        compiler_params=_params(("arbitrary", "arbitrary"), vmem_mb=60))(q, do, totals, k, kt, vt)


def _pool_bwd(u, dmp, w_pool, pool_scale):
    s = u.shape[0]
    ts = 512
    nr = s // ts
    per = ts // POOL_HALO

    def body(u_ref, halo_ref, dm_ref, wp_ref, ps_ref, du_ref, dwp_ref, dps_ref, ext_ref, y_ref, dext_ref):
        i = pl.program_id(0)
        rb = nr - 1 - i

        @pl.when(i == 0)
        def _():
            dext_ref[ts:, :] = jnp.zeros((POOL_HALO, D_POOL), F32)
            dwp_ref[...] = jnp.zeros_like(dwp_ref)
            dps_ref[...] = jnp.zeros_like(dps_ref)

        ext_ref[0:POOL_HALO, :] = jnp.where(rb > 0, halo_ref[...], 0.0)
        ext_ref[POOL_HALO:, :] = u_ref[...]
        ps, cnts = [], []
        for g, window in enumerate(POOL_WINDOWS):
            p, cnt = _pool_means(ext_ref, g, window, ts, rb * ts)
            ps.append(p.astype(BF16))
            cnts.append(cnt)
            y_ref[:, g * POOL_GROUP:(g + 1) * POOL_GROUP] = _dot(ps[g], wp_ref[g].astype(BF16))
        y = y_ref[...]
        r = _rstd(y)
        yn = y * r
        dm = dm_ref[...]
        dps_ref[...] += jnp.sum(dm * yn, axis=0, keepdims=True)
        dn = dm * ps_ref[...]
        dy = r * (dn - yn * jnp.mean(dn * yn, axis=-1, keepdims=True))
        for g, window in enumerate(POOL_WINDOWS):
            cols = slice(g * POOL_GROUP, (g + 1) * POOL_GROUP)
            dyg = dy[:, cols].astype(BF16)
            dwp_ref[g] += _dot_tn(ps[g], dyg)
            dp = _dot_nt(dyg, wp_ref[g].astype(BF16))
            dext_ref[0:ts, cols] = dp / cnts[g]
            acc = dext_ref[0:ts, cols]
            for d in range(1, window):
                acc = acc + dext_ref[d:d + ts, cols]
            du_ref[:, cols] = (acc - dp).astype(BF16)
        dext_ref[ts:, :] = dext_ref[0:POOL_HALO, :]

    rows = pl.BlockSpec((ts, D_POOL), lambda i: (nr - 1 - i, 0))
    return _call(
        body, name="pool_bwd",
        out_shape=[jax.ShapeDtypeStruct((s, D_POOL), BF16), jax.ShapeDtypeStruct(w_pool.shape, F32),
                   jax.ShapeDtypeStruct((1, D_POOL), F32)],
        grid=(nr,),
        in_specs=[rows,
                  pl.BlockSpec((POOL_HALO, D_POOL), lambda i: (jnp.maximum((nr - 1 - i) * per - 1, 0), 0)),
                  rows, _full(w_pool.shape), _full((1, D_POOL))],
        out_specs=[rows, _full(w_pool.shape), _full((1, D_POOL))],
        scratch_shapes=[pltpu.VMEM((ts + POOL_HALO, D_POOL), F32), pltpu.VMEM((ts, D_POOL), F32),
                        pltpu.VMEM((ts + POOL_HALO, D_POOL), F32)],
        compiler_params=_params(("arbitrary",)))(u, u, dmp, w_pool, pool_scale)


def _in_proj_bwd(du, dq, dk, dv, w_in, x, dx1, g1):
    s = x.shape[0]
    ts = 512

    def body(du_ref, dq_ref, dk_ref, dv_ref, w_ref, x_ref, dx1_ref, g_ref, gx_ref, dg_ref):
        @pl.when(pl.program_id(0) == 0)
        def _():
            dg_ref[...] = jnp.zeros_like(dg_ref)

        dh = _dot_nt(du_ref[...], w_ref[0]) + _dot_nt(dq_ref[...], w_ref[1])
        dh = dh + _dot_nt(dk_ref[...].astype(BF16), w_ref[2]) + _dot_nt(dv_ref[...].astype(BF16), w_ref[3])
        xv = x_ref[...]
        r = _rstd(xv)
        xn = xv * r
        dg_ref[...] += jnp.sum(dh * xn, axis=0, keepdims=True)
        dxn = dh * g_ref[...]
        gx_ref[...] = dx1_ref[...] + r * (dxn - xn * jnp.mean(dxn * xn, axis=-1, keepdims=True))

    row = lambda w: pl.BlockSpec((ts, w), lambda i: (i, 0))
    return _call(
        body, name="in_proj_bwd",
        out_shape=[jax.ShapeDtypeStruct((s, D_MODEL), F32), jax.ShapeDtypeStruct((1, D_MODEL), F32)],
        grid=(s // ts,),
        in_specs=[row(D_POOL)] * 4 + [_full(w_in.shape), row(D_MODEL), row(D_MODEL), _full((1, D_MODEL))],
        out_specs=[row(D_MODEL), _full((1, D_MODEL))],
        compiler_params=_params(("arbitrary",)))(du, dq, dk, dv, w_in, x, dx1, g1)


_SMALL = ("norm_mix_pre", "w_pool", "pool_scale", "attn_scale", "norm_mix_post",
          "norm_ffn_pre", "conv_b", "norm_ffn_post")
_SMALL_SIZE = {"norm_mix_pre": 1024, "w_pool": 65536, "pool_scale": 512, "attn_scale": 512,
               "norm_mix_post": 1024, "norm_ffn_pre": 1024, "conv_b": 5632, "norm_ffn_post": 1024}
_SMALL_ROWS = 600
_CONVW_ROWS = 132
_PACK_ROWS = _SMALL_ROWS + _CONVW_ROWS + 4


def _pack_small(parts):
    flat = jnp.concatenate([parts[n].reshape(-1) for n in _SMALL])
    flat = jnp.pad(flat, (0, _SMALL_ROWS * 128 - flat.shape[0]))
    return flat.reshape(_SMALL_ROWS, 128)


def _unpack_small(packed, like):
    flat = packed.reshape(-1)
    out, off = {}, 0
    for n in _SMALL:
        out[n] = flat[off:off + _SMALL_SIZE[n]].reshape(like[n].shape)
        off += _SMALL_SIZE[n]
    return out


def kernel(x, norm_mix_pre, w_in, w_pool, pool_scale, attn_scale, w_out, norm_mix_post, norm_ffn_pre, w_up, conv_w, conv_b, w_down, norm_ffn_post, loss_target, m_norm_mix_pre, m_w_in, m_w_pool, m_pool_scale, m_attn_scale, m_w_out, m_norm_mix_post, m_norm_ffn_pre, m_w_up, m_conv_w, m_conv_b, m_w_down, m_norm_ffn_post, v_norm_mix_pre, v_w_in, v_w_pool, v_pool_scale, v_attn_scale, v_w_out, v_norm_mix_post, v_norm_ffn_pre, v_w_up, v_conv_w, v_conv_b, v_w_down, v_norm_ffn_post):
    weights = dict(norm_mix_pre=norm_mix_pre, w_in=w_in, w_pool=w_pool, pool_scale=pool_scale,
                   attn_scale=attn_scale, w_out=w_out, norm_mix_post=norm_mix_post,
                   norm_ffn_pre=norm_ffn_pre, w_up=w_up, conv_w=conv_w, conv_b=conv_b,
                   w_down=w_down, norm_ffn_post=norm_ffn_post)
    mom1 = dict(norm_mix_pre=m_norm_mix_pre, w_in=m_w_in, w_pool=m_w_pool, pool_scale=m_pool_scale,
                attn_scale=m_attn_scale, w_out=m_w_out, norm_mix_post=m_norm_mix_post,
                norm_ffn_pre=m_norm_ffn_pre, w_up=m_w_up, conv_w=m_conv_w, conv_b=m_conv_b,
                w_down=m_w_down, norm_ffn_post=m_norm_ffn_post)
    mom2 = dict(norm_mix_pre=v_norm_mix_pre, w_in=v_w_in, w_pool=v_w_pool, pool_scale=v_pool_scale,
                attn_scale=v_attn_scale, w_out=v_w_out, norm_mix_post=v_norm_mix_post,
                norm_ffn_pre=v_norm_ffn_pre, w_up=v_w_up, conv_w=v_conv_w, conv_b=v_conv_b,
                w_down=v_w_down, norm_ffn_post=v_norm_ffn_post)
    order = list(weights)

    xs = x[0]
    target = loss_target[0]
    wp = w_pool[0]
    shard = lax.axis_index("x") * 2 + lax.axis_index("y")

    slot = shard.astype(jnp.int32).reshape(1)
    win_g, = _gather_shards([_cast_bf16(w_in[0], "cast_w_in")])
    lands = [_into_slot(w_out[0], slot, BF16, "cast_w_out"), _into_slot(w_up[0], slot, BF16, "cast_w_up"),
             _into_slot(w_down[0], slot, BF16, "cast_w_down"), _into_slot(conv_w[0], slot, F32, "place_conv_w")]
    g_send, g_recv, g_lands, g_token = _exchange_start(None, lands, win_g, "gather_start")
    convb_g = conv_b[0].reshape(N_SHARD, 1, FF_TILE)

    u, q, k, v, kt, vt, h1 = _in_proj(xs, norm_mix_pre + g_token[0:1, 0:1], win_g)
    mpool = _pool_fwd(u, wp, pool_scale)
    attn, totals = _attn_fwd(q, k, vt)
    _, (wout_g, wup_g, wdown_g, convw_g) = _exchange_wait(g_send, g_recv, g_lands, False, attn, "gather_wait")
    wout_f = wout_g.reshape(D_MODEL, D_MODEL)
    wdown_f = wdown_g.reshape(D_FF, D_MODEL)
    mattn, mix, x1, h2 = _mix_out(attn, mpool, xs, attn_scale, wout_f, norm_mix_post, norm_ffn_pre)
    upre_g, upre_v, f_in = _ffn_up(h2, wup_g, convw_g, convb_g)
    df, dy, loss_tile, d_post = _ffn_down(f_in, wdown_f, x1, target, norm_ffn_post)

    d_wdown = _tn_matmul(f_in, df[None], "dw_down")
    dug, duv, dcw_g, dcw_v, dcb_g, dcb_v = _ffn_bwd_act(df, wdown_f, upre_g, upre_v, convw_g, convb_g)
    d_wup = jnp.concatenate([_tn_matmul(h2[None], dug, "dw_up_gate")[0],
                             _tn_matmul(h2[None], duv, "dw_up_value")[0]], axis=0)
    early = [d_wup, d_wdown.reshape(N_SHARD, D_FF // N_SHARD, D_MODEL)]
    s_send, s_recv, s_thru, s_token = _exchange_start(
        early, [lax.empty((3,) + g.shape[1:], g.dtype) for g in early], d_wup, "scatter_start")
    dx1, dmix, d_ffn_pre, d_mix_post = _ffn_bwd_in(
        dug, duv, wup_g, x1, dy, mix, norm_ffn_pre + s_token[0:1, 0:1], norm_mix_post)
    d_wout = jnp.concatenate([_tn_matmul(mpool[None], dmix[None], "dw_out_pool")[0, 0],
                              _tn_matmul(mattn[None], dmix[None], "dw_out_attn")[0, 0]], axis=0)
    dmp, do, d_attn_scale = _mix_bwd(dmix, wout_f, attn, attn_scale)
    dq, dk, dv = _attn_bwd(q, k, kt, vt, do, totals)
    du, d_wpool, d_pool_scale = _pool_bwd(u, dmp, wp, pool_scale)
    d_win = jnp.stack([_tn_matmul(h1[None], t[None], "dw_in_%d" % n)[0, 0]
                       for n, t in enumerate((du, dq, dk, dv))])
    grad_x, d_mix_pre = _in_proj_bwd(du, dq, dk, dv, win_g, xs, dx1, norm_mix_pre)

    d_convw = jnp.concatenate([dcw_g, dcw_v], axis=0)
    d_convb = jnp.concatenate([dcb_g, dcb_v], axis=0).reshape(1, 2 * D_FF)
    small_parts = dict(norm_mix_pre=d_mix_pre, w_pool=d_wpool, pool_scale=d_pool_scale,
                       attn_scale=d_attn_scale, norm_mix_post=d_mix_post, norm_ffn_pre=d_ffn_pre,
                       conv_b=d_convb, norm_ffn_post=d_post)
    packed = jnp.concatenate([_pack_small(small_parts), d_convw.reshape(_CONVW_ROWS, 128),
                              loss_tile[0:4]], axis=0)
    late, gathered = _scatter_grads([d_win, d_wout.reshape(N_SHARD, D_MODEL // N_SHARD, D_MODEL)], packed)
    early_srcs, early_lands = _exchange_wait(s_send, s_recv, s_thru, True, grad_x, "scatter_wait")
    quarter = [_sum_slots(r, (3, 0, 1, 2), "sum_chips_%d" % n) for n, r in enumerate(late)]
    quarter += [_sum_own_and_received(early_srcs[n], slot, early_lands[n], "sum_chips_%d" % (n + 2))
                for n in range(2)]
    sibling = _swap_with_sibling(quarter)
    small_sum = _sum_slots(gathered, tuple(range(8)), "sum_small")

    results = {}
    for n, name in enumerate(("w_in", "w_out", "w_up", "w_down")):
        res = _adamw([quarter[n], sibling[n]], weights[name][0], mom1[name][0], mom2[name][0],
                     "adamw_" + name)
        results[name] = [t[None] for t in res]
    g_convw = lax.dynamic_slice_in_dim(
        small_sum[_SMALL_ROWS:_SMALL_ROWS + _CONVW_ROWS].reshape(N_SHARD, 3, FF_TILE), shard, 1, axis=0)[0]
    convw_pad = lambda t: jnp.pad(t, ((0, 5), (0, 0)))
    res = _adamw([convw_pad(g_convw)], convw_pad(conv_w[0]), convw_pad(m_conv_w[0]),
                 convw_pad(v_conv_w[0]), "adamw_conv_w")
    results["conv_w"] = [t[:3][None] for t in res]
    pack_w = _pack_small(weights)
    pack_m = _pack_small(mom1)
    pack_v = _pack_small(mom2)
    res = _adamw([small_sum[:_SMALL_ROWS]], pack_w, pack_m, pack_v, "adamw_small")
    unpacked = [_unpack_small(t, weights) for t in res]
    for name in _SMALL:
        results[name] = [t[name] for t in unpacked]

    loss = small_sum[_SMALL_ROWS + _CONVW_ROWS, 0]
    outs = [loss, grad_x[None]]
    for slot in range(4):
        outs.extend(results[name][slot] for name in order)
    return tuple(outs)
```

```python
import functools

import jax
import jax.numpy as jnp
from jax import lax
from jax.experimental import pallas as pl
from jax.experimental.pallas import tpu as pltpu

F32 = jnp.float32
BF16 = jnp.bfloat16

D_MODEL = 1024
D_POOL = 512
D_ATTN = 512
POOL_WINDOWS = (2, 4, 8, 16)
POOL_GROUP = 128
POOL_HALO = 16
CONV_HALO = 8
D_FF = 2816
FF_TILE = 1408
N_SHARD = 4
EPS = 1e-6
Q_SCALE = 0.125
ATT_BLOCK = 256
HEAD_PAIR = 128
MIB = 1 << 20
NEG_BIG = -1e30

ADAM_LR = 0.001
ADAM_B1 = 0.9
ADAM_B2 = 0.999
ADAM_EPS = 1e-08
ADAM_WD = 0.01
ADAM_STEP = 10

NT_DIMS = (((1,), (1,)), ((), ()))
TN_DIMS = (((0,), (0,)), ((), ()))
MESH = pl.DeviceIdType.MESH
ANY = pl.BlockSpec(memory_space=pl.ANY)
HBM_SPEC = pl.BlockSpec(memory_space=pltpu.HBM)
SEM_SPEC = pl.BlockSpec(memory_space=pltpu.SEMAPHORE)
DATAFLOW = pltpu.SideEffectType.DATAFLOW_SIDE_EFFECTING


def _call(body, **kw):
    return pl.pallas_call(body, **kw)


def _params(sem=None, vmem_mb=48):
    return pltpu.CompilerParams(dimension_semantics=sem, vmem_limit_bytes=vmem_mb * MIB)


def _rstd(v):
    return lax.rsqrt(jnp.mean(v * v, axis=-1, keepdims=True) + EPS)


def _dot(a, b):
    return jnp.dot(a, b, preferred_element_type=F32)


def _dot_nt(a, b):
    return lax.dot_general(a, b, NT_DIMS, preferred_element_type=F32)


def _dot_tn(a, b):
    return lax.dot_general(a, b, TN_DIMS, preferred_element_type=F32)


def _row_tile(rows, cap):
    t = min(rows, cap)
    t -= t % 8
    while rows % t:
        t -= 8
    return t


def _full(shape):
    nd = len(shape)
    return pl.BlockSpec(shape, lambda *_: (0,) * nd)


def _chip_peers():
    x, y, c = lax.axis_index("x"), lax.axis_index("y"), lax.axis_index("c")
    return x, y, c, [(1 - x, y), (x, 1 - y), (1 - x, 1 - y)]


def _cast_bf16(a, name):
    def body(a_ref, o_ref):
        o_ref[...] = a_ref[...].astype(BF16)

    return _call(body, name=name, out_shape=jax.ShapeDtypeStruct(a.shape, BF16),
                 grid=(1,), in_specs=[_full(a.shape)], out_specs=_full(a.shape),
                 compiler_params=_params(("arbitrary",)))(a)


def _into_slot(a, slot, dtype, name):
    nd = a.ndim

    def body(slot_ref, a_ref, o_ref):
        o_ref[...] = a_ref[...].astype(dtype)

    return _call(
        body, name=name, out_shape=jax.ShapeDtypeStruct((N_SHARD,) + a.shape, dtype),
        grid_spec=pltpu.PrefetchScalarGridSpec(
            num_scalar_prefetch=1, grid=(1,),
            in_specs=[pl.BlockSpec(a.shape, lambda i, slot_ref: (0,) * nd)],
            out_specs=pl.BlockSpec((None,) + a.shape, lambda i, slot_ref: (slot_ref[0],) + (0,) * nd)),
        compiler_params=_params(("arbitrary",)))(slot, a)


def _exchange_copies(srcs, lands, send, recv):
    x, y, c, chips = _chip_peers()
    copies = []
    for t in range(len(lands)):
        for k, (px, py) in enumerate(chips):
            copies.append(pltpu.make_async_remote_copy(
                src_ref=lands[t].at[2 * x + y] if srcs is None else srcs[t].at[2 * px + py],
                dst_ref=lands[t].at[2 * x + y] if srcs is None else lands[t].at[k],
                send_sem=send.at[3 * t + k], recv_sem=recv.at[3 * t + k],
                device_id=(px, py, c), device_id_type=MESH))
    return copies


def _exchange_start(srcs, lands, after, name):
    n = len(lands)
    operands = list(lands) if srcs is None else list(srcs) + list(lands)
    m = len(operands)

    def body(*refs):
        for cp in _exchange_copies(None if srcs is None else refs[:n], refs[m - n:m], refs[m + 1], refs[m + 2]):
            cp.start()
        refs[-1][...] = jnp.zeros_like(refs[-1])

    res = _call(
        body, name=name,
        out_shape=[pltpu.SemaphoreType.DMA((3 * n,)), pltpu.SemaphoreType.DMA((3 * n,))]
        + [pltpu.HBM(a.shape, a.dtype) for a in operands] + [jax.ShapeDtypeStruct((8, 128), F32)],
        in_specs=[HBM_SPEC] * m + [ANY],
        out_specs=[SEM_SPEC, SEM_SPEC] + [HBM_SPEC] * m + [pl.BlockSpec(memory_space=pltpu.VMEM)],
        input_output_aliases={j: j + 2 for j in range(m)},
        compiler_params=pltpu.CompilerParams(has_side_effects=DATAFLOW),
    )(*[pltpu.with_memory_space_constraint(a, pltpu.HBM) for a in operands], after)
    return res[0], res[1], res[2:2 + m], res[-1]


def _exchange_wait(send, recv, operands, scatter, after, name):
    m = len(operands)
    n = m // 2 if scatter else m

    def body(*refs):
        for cp in _exchange_copies(refs[:n] if scatter else None, refs[m - n:m], refs[m], refs[m + 1]):
            cp.wait_send()
            cp.wait_recv()

    res = _call(
        body, name=name, out_shape=[pltpu.HBM(a.shape, a.dtype) for a in operands],
        in_specs=[HBM_SPEC] * m + [SEM_SPEC, SEM_SPEC, ANY], out_specs=[HBM_SPEC] * m,
        input_output_aliases={j: j for j in range(m)},
        compiler_params=pltpu.CompilerParams(has_side_effects=DATAFLOW),
    )(*operands, send, recv, after)
    return res[:m - n], res[m - n:]


def _gather_shards(shards):
    n = len(shards)

    def body(*refs):
        ins, outs = refs[:n], refs[n:2 * n]
        send, recv, loc = refs[2 * n:]
        x, y, c, chips = _chip_peers()
        b = 2 * x + y
        local = [pltpu.make_async_copy(ins[t], outs[t].at[b], loc.at[t]) for t in range(n)]
        for cp in local:
            cp.start()
        remote = []
        for t in range(n):
            for k, (px, py) in enumerate(chips):
                remote.append(pltpu.make_async_remote_copy(
                    src_ref=ins[t], dst_ref=outs[t].at[b],
                    send_sem=send.at[3 * t + k], recv_sem=recv.at[3 * t + k],
                    device_id=(px, py, c), device_id_type=MESH))
        for cp in remote:
            cp.start()
        for cp in remote:
            cp.wait()
        for cp in local:
            cp.wait()

    return _call(
        body, name="gather_w_in",
        out_shape=[jax.ShapeDtypeStruct((N_SHARD,) + s.shape, s.dtype) for s in shards],
        in_specs=[ANY] * n, out_specs=[ANY] * n,
        scratch_shapes=[pltpu.SemaphoreType.DMA((3 * n,)), pltpu.SemaphoreType.DMA((3 * n,)),
                        pltpu.SemaphoreType.DMA((n,))],
    )(*shards)


def _scatter_grads(grads, small):
    n = len(grads)

    def body(*refs):
        ins, small_in = refs[:n], refs[n]
        outs, small_out = refs[n + 1:2 * n + 1], refs[2 * n + 1]
        send, recv, loc, ssend, srecv = refs[2 * n + 2:]
        x, y, c, chips = _chip_peers()
        b = 2 * x + y
        me = 4 * x + 2 * y + c
        local = [pltpu.make_async_copy(ins[t].at[b], outs[t].at[3], loc.at[t]) for t in range(n)]
        local.append(pltpu.make_async_copy(small_in, small_out.at[me], loc.at[n]))
        for cp in local:
            cp.start()
        remote = []
        for t in range(n):
            for k, (px, py) in enumerate(chips):
                remote.append(pltpu.make_async_remote_copy(
                    src_ref=ins[t].at[2 * px + py], dst_ref=outs[t].at[k],
                    send_sem=send.at[3 * t + k], recv_sem=recv.at[3 * t + k],
                    device_id=(px, py, c), device_id_type=MESH))
        for r in range(1, 8):
            px = 1 - x if r & 4 else x
            py = 1 - y if r & 2 else y
            pc = 1 - c if r & 1 else c
            remote.append(pltpu.make_async_remote_copy(
                src_ref=small_in, dst_ref=small_out.at[me],
                send_sem=ssend.at[r - 1], recv_sem=srecv.at[r - 1],
                device_id=(px, py, pc), device_id_type=MESH))
        for cp in remote:
            cp.start()
        for cp in remote:
            cp.wait()
        for cp in local:
            cp.wait()

    out_shape = [jax.ShapeDtypeStruct(g.shape, g.dtype) for g in grads]
    out_shape.append(jax.ShapeDtypeStruct((8,) + small.shape, small.dtype))
    res = _call(
        body, name="scatter_grads", out_shape=out_shape,
        in_specs=[ANY] * (n + 1), out_specs=[ANY] * (n + 1),
        scratch_shapes=[pltpu.SemaphoreType.DMA((3 * n,)), pltpu.SemaphoreType.DMA((3 * n,)),
                        pltpu.SemaphoreType.DMA((n + 1,)),
                        pltpu.SemaphoreType.DMA((7,)), pltpu.SemaphoreType.DMA((7,))],
    )(*grads, small)
    return res[:n], res[n]


def _swap_with_sibling(parts):
    n = len(parts)

    def body(*refs):
        ins, outs = refs[:n], refs[n:2 * n]
        send, recv = refs[2 * n:]
        x, y, c = lax.axis_index("x"), lax.axis_index("y"), lax.axis_index("c")
        copies = [pltpu.make_async_remote_copy(
            src_ref=ins[t], dst_ref=outs[t], send_sem=send.at[t], recv_sem=recv.at[t],
            device_id=(x, y, 1 - c), device_id_type=MESH) for t in range(n)]
        for cp in copies:
            cp.start()
        for cp in copies:
            cp.wait()

    return _call(
        body, name="swap_sibling",
        out_shape=[jax.ShapeDtypeStruct(p.shape, p.dtype) for p in parts],
        in_specs=[ANY] * n, out_specs=[ANY] * n,
        scratch_shapes=[pltpu.SemaphoreType.DMA((n,)), pltpu.SemaphoreType.DMA((n,))],
    )(*parts)


def _sum_slots(buf, order, name):
    k, rows, cols = buf.shape
    tr = _row_tile(rows, 256)

    def body(b_ref, o_ref):
        acc = b_ref[order[0]].astype(F32)
        for s in order[1:]:
            acc = acc + b_ref[s].astype(F32)
        o_ref[...] = acc

    return _call(body, name=name, out_shape=jax.ShapeDtypeStruct((rows, cols), F32),
                 grid=(rows // tr,),
                 in_specs=[pl.BlockSpec((k, tr, cols), lambda i: (0, i, 0))],
                 out_specs=pl.BlockSpec((tr, cols), lambda i: (i, 0)),
                 compiler_params=_params(("parallel",)))(buf)


def _sum_own_and_received(src, slot, land, name):
    _, rows, cols = src.shape
    tr = _row_tile(rows, 256)

    def body(slot_ref, s_ref, l_ref, o_ref):
        acc = s_ref[...].astype(F32)
        for k in range(3):
            acc = acc + l_ref[k].astype(F32)
        o_ref[...] = acc

    return _call(
        body, name=name, out_shape=jax.ShapeDtypeStruct((rows, cols), F32),
        grid_spec=pltpu.PrefetchScalarGridSpec(
            num_scalar_prefetch=1, grid=(rows // tr,),
            in_specs=[pl.BlockSpec((None, tr, cols), lambda i, slot_ref: (slot_ref[0], i, 0)),
                      pl.BlockSpec((3, tr, cols), lambda i, slot_ref: (0, i, 0))],
            out_specs=pl.BlockSpec((tr, cols), lambda i, slot_ref: (i, 0))),
        compiler_params=_params(("parallel",)))(slot, src, land)


def _adamw(grad_parts, w, m, v, name):
    rows, cols = w.shape
    tr = _row_tile(rows, 256)
    npart = len(grad_parts)

    def body(*refs):
        gp = refs[:npart]
        w_ref, m_ref, v_ref, g_out, d_out, m_out, v_out = refs[npart:]
        g = gp[0][...]
        for p in gp[1:]:
            g = g + p[...]
        mm = ADAM_B1 * m_ref[...] + (1.0 - ADAM_B1) * g
        vv = ADAM_B2 * v_ref[...] + (1.0 - ADAM_B2) * jnp.square(g)
        m_hat = mm / (1.0 - ADAM_B1 ** ADAM_STEP)
        v_hat = vv / (1.0 - ADAM_B2 ** ADAM_STEP)
        g_out[...] = g
        d_out[...] = -ADAM_LR * (m_hat / (jnp.sqrt(v_hat) + ADAM_EPS) + ADAM_WD * w_ref[...])
        m_out[...] = mm
        v_out[...] = vv

    spec = pl.BlockSpec((tr, cols), lambda i: (i, 0))
    shp = jax.ShapeDtypeStruct((rows, cols), F32)
    return _call(body, name=name, out_shape=[shp] * 4, grid=(rows // tr,),
                 in_specs=[spec] * (npart + 3), out_specs=[spec] * 4,
                 compiler_params=_params(("parallel",)))(*grad_parts, w, m, v)


def _in_proj(x, g1, w_in):
    s = x.shape[0]
    ts = 512

    def body(x_ref, g_ref, w_ref, u_ref, q_ref, k_ref, kt_ref, vt_ref, h_ref):
        xv = x_ref[...]
        h = (xv * _rstd(xv) * g_ref[...]).astype(BF16)
        h_ref[...] = h
        u_ref[...] = _dot(h, w_ref[0])
        q_ref[...] = (_dot(h, w_ref[1]) * Q_SCALE).astype(BF16)
        k = _dot(h, w_ref[2])
        k_ref[...] = k.astype(BF16)
        v = _dot(h, w_ref[3])
        for src, dst in ((k, kt_ref), (v, vt_ref)):
            src_t = src.T.astype(BF16)
            for n in range(ts // ATT_BLOCK):
                dst[n] = src_t[:, n * ATT_BLOCK:(n + 1) * ATT_BLOCK]

    row = lambda w: pl.BlockSpec((ts, w), lambda i: (i, 0))
    half = jax.ShapeDtypeStruct((s, D_POOL), BF16)
    return _call(
        body, name="in_proj",
        out_shape=[jax.ShapeDtypeStruct((s, D_POOL), F32), half, half,
                   jax.ShapeDtypeStruct((s // ATT_BLOCK, D_ATTN, ATT_BLOCK), BF16),
                   jax.ShapeDtypeStruct((s // ATT_BLOCK, D_ATTN, ATT_BLOCK), BF16),
                   jax.ShapeDtypeStruct((s, D_MODEL), BF16)],
        grid=(s // ts,),
        in_specs=[row(D_MODEL), _full((1, D_MODEL)), _full(w_in.shape)],
        out_specs=[row(D_POOL)] * 3
        + [pl.BlockSpec((ts // ATT_BLOCK, D_ATTN, ATT_BLOCK), lambda i: (i, 0, 0))] * 2 + [row(D_MODEL)],
        compiler_params=_params(("parallel",)))(x, g1, w_in)


def _pool_means(ext_ref, g, window, ts, row0):
    cols = slice(g * POOL_GROUP, (g + 1) * POOL_GROUP)
    cur = ext_ref[POOL_HALO:POOL_HALO + ts, cols]
    acc = cur
    for d in range(1, window):
        acc = acc + ext_ref[POOL_HALO - d:POOL_HALO - d + ts, cols]
    t1 = row0 + 1 + lax.broadcasted_iota(jnp.int32, (ts, 1), 0)
    cnt = jnp.minimum(t1, window).astype(F32)
    return acc / cnt - cur, cnt


def _pool_fwd(u, w_pool, pool_scale):
    s = u.shape[0]
    ts = 512
    per = ts // POOL_HALO

    def body(u_ref, halo_ref, wp_ref, ps_ref, o_ref, ext_ref, y_ref):
        i = pl.program_id(0)
        ext_ref[0:POOL_HALO, :] = jnp.where(i > 0, halo_ref[...], 0.0)
        ext_ref[POOL_HALO:, :] = u_ref[...]
        for g, window in enumerate(POOL_WINDOWS):
            p, _ = _pool_means(ext_ref, g, window, ts, i * ts)
            y_ref[:, g * POOL_GROUP:(g + 1) * POOL_GROUP] = _dot(
                p.astype(BF16), wp_ref[g].astype(BF16))
        y = y_ref[...]
        o_ref[...] = (y * _rstd(y) * ps_ref[...]).astype(BF16)

    return _call(
        body, name="pool_fwd", out_shape=jax.ShapeDtypeStruct((s, D_POOL), BF16),
        grid=(s // ts,),
        in_specs=[pl.BlockSpec((ts, D_POOL), lambda i: (i, 0)),
                  pl.BlockSpec((POOL_HALO, D_POOL), lambda i: (jnp.maximum(i * per - 1, 0), 0)),
                  _full(w_pool.shape), _full((1, D_POOL))],
        out_specs=pl.BlockSpec((ts, D_POOL), lambda i: (i, 0)),
        scratch_shapes=[pltpu.VMEM((ts + POOL_HALO, D_POOL), F32), pltpu.VMEM((ts, D_POOL), F32)],
        compiler_params=_params(("parallel",)))(u, u, w_pool, pool_scale)


def _tri(kind):
    r = lax.broadcasted_iota(jnp.int32, (ATT_BLOCK, ATT_BLOCK), 0)
    c = lax.broadcasted_iota(jnp.int32, (ATT_BLOCK, ATT_BLOCK), 1)
    return jnp.where(r >= c if kind == "suffix" else r <= c, 1.0, 0.0).astype(BF16)


def _causal_mask():
    r = lax.broadcasted_iota(jnp.int32, (ATT_BLOCK, ATT_BLOCK), 0)
    c = lax.broadcasted_iota(jnp.int32, (ATT_BLOCK, ATT_BLOCK), 1)
    return c < r


def _softplus(z, with_sigmoid=False):
    ope = 1.0 + jnp.exp(jnp.minimum(z, 80.0))
    sp = jnp.maximum(z, jnp.log(ope))
    if with_sigmoid:
        return sp, 1.0 - 1.0 / ope
    return sp


def _attn_fwd(q, k, vt):
    s = q.shape[0]
    tb = ATT_BLOCK
    nq = s // tb

    def body(q_ref, k_ref, vt_ref, o_ref, t_ref, *bufs):
        i = pl.program_id(1)
        suffix = _tri("prefix")
        r_idx = lax.broadcasted_iota(jnp.int32, (tb, tb), 0)
        c_idx = lax.broadcasted_iota(jnp.int32, (tb, tb), 1)
        causal = r_idx < c_idx
        lane = lax.broadcasted_iota(jnp.int32, (1, HEAD_PAIR), 1)
        first = lane < 64
        top = lax.broadcasted_iota(jnp.int32, (HEAD_PAIR, 1), 0) < 64
        q2 = q_ref[...]
        zero = jnp.zeros_like(q2)
        qs_t = tuple(jnp.where(first, q2, zero).astype(F32).T.astype(BF16) if e == 0 else
                     jnp.where(first, zero, q2).astype(F32).T.astype(BF16) for e in range(2))

        def values_t(j):
            vt = vt_ref[j]
            none = jnp.zeros_like(vt)
            return jnp.concatenate([jnp.where(top, vt, none), jnp.where(top, none, vt)], axis=1)

        z_ring = [[bufs[2 * slot + e] for e in range(2)] for slot in range(4)]
        in_buf = [[bufs[8 + 2 * slot + e] for e in range(2)] for slot in range(2)]
        pr_buf = [bufs[12], bufs[13]]

        def block_of(p):
            return jnp.clip(i - p, 0, nq - 1)

        def trip(t, u, carry):
            first_trips = isinstance(t, int)
            w, r = u % 2, 1 - u % 2
            cs, o = list(carry[0:2]), carry[2]
            if not first_trips or t >= 3:
                o = o + jnp.where(t - 3 <= i, _dot(values_t(block_of(t - 3)), pr_buf[r][...]), 0.0)
            kj = k_ref[pl.ds(pl.multiple_of(block_of(t) * tb, tb), tb), :]
            for e in range(2):
                z_ring[u][e][...] = _dot(kj, qs_t[e])
            if not first_trips or t >= 1:
                for e in range(2):
                    sp = _softplus(z_ring[(u - 1) % 4][e][...])
                    if first_trips and t == 1:
                        sp = jnp.where(causal, sp, 0.0)
                    in_buf[w][e][...] = _dot(suffix, sp.astype(BF16))
            if not first_trips or t >= 2:
                live2 = t - 2 <= i
                for e in range(2):
                    incl = in_buf[r][e][...]
                    arg = z_ring[(u - 2) % 4][e][...] - incl - jnp.where(live2, cs[e], -NEG_BIG)
                    if first_trips and t == 2:
                        arg = jnp.where(causal, arg, NEG_BIG)
                    pr_buf[w][e * tb:(e + 1) * tb, :] = jnp.exp(arg).astype(BF16)
                    cs[e] = jnp.where(live2, cs[e] + incl[0:1, :], cs[e])
            return cs[0], cs[1], o

        def four_trips(n, cr):
            for u in range(4):
                cr = trip(4 * n + u, u, cr)
            return cr

        row = jnp.zeros((1, tb), F32)
        carry = four_trips(0, (row, row, jnp.zeros((HEAD_PAIR, tb), F32)))
        carry = lax.fori_loop(1, (i + 7) // 4, four_trips, carry)
        o_ref[...] = carry[2].T
        totals = jnp.where(r_idx == 0, carry[0], jnp.where(r_idx == 1, carry[1], 0.0))
        t_ref[...] = totals.T[:, 0:2]

    score_buf = pltpu.VMEM((tb, tb), F32)
    return _call(
        body, name="attn_fwd",
        out_shape=[jax.ShapeDtypeStruct((s, D_ATTN), F32),
                   jax.ShapeDtypeStruct((4, s, 2), F32)],
        grid=(4, nq),
        in_specs=[pl.BlockSpec((tb, HEAD_PAIR), lambda h, i: (i, h)),
                  pl.BlockSpec((s, HEAD_PAIR), lambda h, i: (0, h)),
                  pl.BlockSpec((nq, HEAD_PAIR, tb), lambda h, i: (0, h, 0))],
        out_specs=[pl.BlockSpec((tb, HEAD_PAIR), lambda h, i: (i, h)),
                   pl.BlockSpec((None, tb, 2), lambda h, i: (h, i, 0))],
        scratch_shapes=[score_buf] * 12 + [pltpu.VMEM((2 * tb, tb), BF16)] * 2,
        compiler_params=_params(("arbitrary", "arbitrary")))(q, k, vt)


def _mix_out(attn, mpool, x, attn_scale, w_out, g2, g3):
    s = x.shape[0]
    ts = 512

    def body(a_ref, p_ref, x_ref, as_ref, w_ref, g2_ref, g3_ref, ma_ref, mix_ref, x1_ref, h2_ref):
        ao = a_ref[...]
        ma = (ao * _rstd(ao) * as_ref[...]).astype(BF16)
        ma_ref[...] = ma
        mix = _dot(p_ref[...], w_ref[0:D_POOL, :]) + _dot(ma, w_ref[D_POOL:, :])
        mix_ref[...] = mix
        x1 = x_ref[...] + mix * _rstd(mix) * g2_ref[...]
        x1_ref[...] = x1
        h2_ref[...] = (x1 * _rstd(x1) * g3_ref[...]).astype(BF16)

    row = lambda w: pl.BlockSpec((ts, w), lambda i: (i, 0))
    return _call(
        body, name="mix_out",
        out_shape=[jax.ShapeDtypeStruct((s, D_ATTN), BF16), jax.ShapeDtypeStruct((s, D_MODEL), F32),
                   jax.ShapeDtypeStruct((s, D_MODEL), F32), jax.ShapeDtypeStruct((s, D_MODEL), BF16)],
        grid=(s // ts,),
        in_specs=[row(D_ATTN), row(D_POOL), row(D_MODEL), _full((1, D_ATTN)),
                  _full((D_MODEL, D_MODEL)), _full((1, D_MODEL)), _full((1, D_MODEL))],
        out_specs=[row(D_ATTN), row(D_MODEL), row(D_MODEL), row(D_MODEL)],
        compiler_params=_params(("parallel",)))(attn, mpool, x, attn_scale, w_out, g2, g3)


def _earlier_rows(halo, block):
    ts = block.shape[0]
    ext = jnp.concatenate([halo, block], axis=0)
    return tuple(pltpu.roll(ext, d, axis=0)[CONV_HALO:CONV_HALO + ts, :] for d in (1, 2))


def _later_rows(block, halo):
    ts = block.shape[0]
    ext = jnp.concatenate([block, halo], axis=0)
    return tuple(pltpu.roll(ext, ts + CONV_HALO - d, axis=0)[0:ts, :] for d in (1, 2))


def _conv_rows(x0, x1, x2, cw, cb):
    y = cb + cw[0:1, :] * x2
    y = y + cw[1:2, :] * x1
    return y + cw[2:3, :] * x0


def _sigmoid(v):
    return 1.0 / (1.0 + jnp.exp(-v))


def _ffn_up(h2, w_up, conv_w, conv_b):
    s = h2.shape[0]
    ts = 256
    tn = FF_TILE

    def body(h_ref, wg_ref, wv_ref, cwg_ref, cwv_ref, cbg_ref, cbv_ref,
             ug_ref, uv_ref, f_ref, tailg, tailv):
        i = pl.program_id(1)

        @pl.when(i == 0)
        def _():
            tailg[...] = jnp.zeros_like(tailg)
            tailv[...] = jnp.zeros_like(tailv)

        h = h_ref[...]
        ug = _dot(h, wg_ref[...])
        uv = _dot(h, wv_ref[...])
        ug_ref[...] = ug
        uv_ref[...] = uv
        gate = _conv_rows(ug, *_earlier_rows(tailg[...], ug), cwg_ref[...], cbg_ref[...])
        val = _conv_rows(uv, *_earlier_rows(tailv[...], uv), cwv_ref[...], cbv_ref[...])
        f_ref[...] = (gate * _sigmoid(gate) * val).astype(BF16)
        tailg[...] = ug[ts - CONV_HALO:, :]
        tailv[...] = uv[ts - CONV_HALO:, :]

    out_blk = pl.BlockSpec((None, ts, tn), lambda n, i: (n, i, 0))
    act = jax.ShapeDtypeStruct((2, s, tn), F32)
    return _call(
        body, name="ffn_up",
        out_shape=[act, act, jax.ShapeDtypeStruct((2, s, tn), BF16)],
        grid=(2, s // ts),
        in_specs=[pl.BlockSpec((ts, D_MODEL), lambda n, i: (i, 0)),
                  pl.BlockSpec((None, D_MODEL, tn), lambda n, i: (n, 0, 0)),
                  pl.BlockSpec((None, D_MODEL, tn), lambda n, i: (n + 2, 0, 0)),
                  pl.BlockSpec((None, 3, tn), lambda n, i: (n, 0, 0)),
                  pl.BlockSpec((None, 3, tn), lambda n, i: (n + 2, 0, 0)),
                  pl.BlockSpec((None, 1, tn), lambda n, i: (n, 0, 0)),
                  pl.BlockSpec((None, 1, tn), lambda n, i: (n + 2, 0, 0))],
        out_specs=[out_blk, out_blk, out_blk],
        scratch_shapes=[pltpu.VMEM((CONV_HALO, tn), F32), pltpu.VMEM((CONV_HALO, tn), F32)],
        compiler_params=_params(("arbitrary", "arbitrary")))(
            h2, w_up, w_up, conv_w, conv_w, conv_b, conv_b)


def _ffn_down(f_in, w_down, x1, target, g4):
    s = x1.shape[0]
    ts = 512

    def body(f_ref, w_ref, x1_ref, t_ref, g_ref, df_ref, dy_ref, loss_ref, dg_ref):
        @pl.when(pl.program_id(0) == 0)
        def _():
            loss_ref[...] = jnp.zeros_like(loss_ref)
            dg_ref[...] = jnp.zeros_like(dg_ref)

        f = _dot(f_ref[0], w_ref[0:FF_TILE, :]) + _dot(f_ref[1], w_ref[FF_TILE:, :])
        rf = _rstd(f)
        fn = f * rf
        g = g_ref[...]
        err = (x1_ref[...] + fn * g) - t_ref[...]
        loss_ref[...] += 0.5 * jnp.sum(jnp.mean(err * err, axis=-1))
        dy = err * (1.0 / D_MODEL)
        dy_ref[...] = dy
        dg_ref[...] += jnp.sum(dy * fn, axis=0, keepdims=True)
        dfn = dy * g
        df_ref[...] = (rf * (dfn - fn * jnp.mean(dfn * fn, axis=-1, keepdims=True))).astype(BF16)

    row = pl.BlockSpec((ts, D_MODEL), lambda i: (i, 0))
    return _call(
        body, name="ffn_down",
        out_shape=[jax.ShapeDtypeStruct((s, D_MODEL), BF16), jax.ShapeDtypeStruct((s, D_MODEL), F32),
                   jax.ShapeDtypeStruct((8, 128), F32), jax.ShapeDtypeStruct((1, D_MODEL), F32)],
        grid=(s // ts,),
        in_specs=[pl.BlockSpec((2, ts, FF_TILE), lambda i: (0, i, 0)), _full((D_FF, D_MODEL)),
                  row, row, _full((1, D_MODEL))],
        out_specs=[row, row, _full((8, 128)), _full((1, D_MODEL))],
        compiler_params=_params(("arbitrary",)))(f_in, w_down, x1, target, g4)


def _tn_matmul(a, b, name, ts=512):
    na, s, ka = a.shape
    nb, _, nbc = b.shape
    steps = s // ts

    def body(a_ref, b_ref, o_ref, acc_ref):
        @pl.when(pl.program_id(2) == 0)
        def _():
            acc_ref[...] = jnp.zeros_like(acc_ref)

        acc_ref[...] += _dot_tn(a_ref[...].astype(BF16), b_ref[...].astype(BF16))

        @pl.when(pl.program_id(2) == steps - 1)
        def _():
            o_ref[...] = acc_ref[...].astype(BF16)

    return _call(
        body, name=name, out_shape=jax.ShapeDtypeStruct((na, nb, ka, nbc), BF16),
        grid=(na, nb, steps),
        in_specs=[pl.BlockSpec((None, ts, ka), lambda i, j, r: (i, r, 0)),
                  pl.BlockSpec((None, ts, nbc), lambda i, j, r: (j, r, 0))],
        out_specs=pl.BlockSpec((None, None, ka, nbc), lambda i, j, r: (i, j, 0, 0)),
        scratch_shapes=[pltpu.VMEM((ka, nbc), F32)],
        compiler_params=_params(("parallel", "parallel", "arbitrary")))(a, b)


def _ffn_bwd_act(df, w_down, upre_g, upre_v, conv_w, conv_b):
    s = df.shape[0]
    ts = 256
    tn = FF_TILE
    nr = s // ts
    per = ts // CONV_HALO

    def body(df_ref, wd_ref, ug_ref, uv_ref, hg_ref, hv_ref, cwg_ref, cwv_ref, cbg_ref, cbv_ref,
             dug_ref, duv_ref, dcwg_ref, dcwv_ref, dcbg_ref, dcbv_ref, headg, headv):
        i = pl.program_id(1)
        first_rows = i == nr - 1

        @pl.when(i == 0)
        def _():
            for r in (headg, headv, dcwg_ref, dcwv_ref, dcbg_ref, dcbv_ref):
                r[...] = jnp.zeros_like(r)

        ug, uv = ug_ref[...], uv_ref[...]
        cwg, cwv = cwg_ref[...], cwv_ref[...]
        gate = _conv_rows(ug, *_earlier_rows(jnp.where(first_rows, 0.0, hg_ref[...]), ug), cwg, cbg_ref[...])
        val = _conv_rows(uv, *_earlier_rows(jnp.where(first_rows, 0.0, hv_ref[...]), uv), cwv, cbv_ref[...])
        sg = _sigmoid(gate)
        dfin = _dot_nt(df_ref[...], wd_ref[...])
        dval = dfin * (gate * sg)
        dgate = dfin * val * (sg * (1.0 + gate * (1.0 - sg)))

        def conv_bwd(dact, x, head, cw, dcw_ref, dcb_ref, du_ref):
            d1, d2 = _later_rows(dact, head[...])
            dcb_ref[...] += jnp.sum(dact, axis=0, keepdims=True)
            for kk, shifted in enumerate((d2, d1, dact)):
                dcw_ref[kk:kk + 1, :] += jnp.sum(x * shifted, axis=0, keepdims=True)
            du_ref[...] = (cw[2:3, :] * dact + cw[1:2, :] * d1 + cw[0:1, :] * d2).astype(BF16)
            head[...] = dact[0:CONV_HALO, :]

        conv_bwd(dgate, ug, headg, cwg, dcwg_ref, dcbg_ref, dug_ref)
        conv_bwd(dval, uv, headv, cwv, dcwv_ref, dcbv_ref, duv_ref)

    rows = lambda n, i: (n, nr - 1 - i, 0)
    halo = lambda n, i: (n, jnp.maximum((nr - 1 - i) * per - 1, 0), 0)
    act_blk = pl.BlockSpec((None, ts, tn), rows)
    halo_blk = pl.BlockSpec((None, CONV_HALO, tn), halo)
    cw_blk = lambda off: pl.BlockSpec((None, 3, tn), lambda n, i: (n + off, 0, 0))
    cb_blk = lambda off: pl.BlockSpec((None, 1, tn), lambda n, i: (n + off, 0, 0))
    acc_w = pl.BlockSpec((None, 3, tn), lambda n, i: (n, 0, 0))
    acc_b = pl.BlockSpec((None, 1, tn), lambda n, i: (n, 0, 0))
    dact = jax.ShapeDtypeStruct((2, s, tn), BF16)
    return _call(
        body, name="ffn_bwd_act",
        out_shape=[dact, dact, jax.ShapeDtypeStruct((2, 3, tn), F32), jax.ShapeDtypeStruct((2, 3, tn), F32),
                   jax.ShapeDtypeStruct((2, 1, tn), F32), jax.ShapeDtypeStruct((2, 1, tn), F32)],
        grid=(2, nr),
        in_specs=[pl.BlockSpec((ts, D_MODEL), lambda n, i: (nr - 1 - i, 0)),
                  pl.BlockSpec((tn, D_MODEL), lambda n, i: (n, 0)),
                  act_blk, act_blk, halo_blk, halo_blk,
                  cw_blk(0), cw_blk(2), cb_blk(0), cb_blk(2)],
        out_specs=[act_blk, act_blk, acc_w, acc_w, acc_b, acc_b],
        scratch_shapes=[pltpu.VMEM((CONV_HALO, tn), F32)] * 2,
        compiler_params=_params(("arbitrary", "arbitrary")))(
            df, w_down, upre_g, upre_v, upre_g, upre_v, conv_w, conv_w, conv_b, conv_b)


def _ffn_bwd_in(dug, duv, w_up, x1, dy, mix, g3, g2):
    s = x1.shape[0]
    ts = 256

    def body(dg_ref, dv_ref, w_ref, x1_ref, dy_ref, mix_ref, g3_ref, g2_ref,
             dx1_ref, dmix_ref, dg3_ref, dg2_ref):
        @pl.when(pl.program_id(0) == 0)
        def _():
            dg3_ref[...] = jnp.zeros_like(dg3_ref)
            dg2_ref[...] = jnp.zeros_like(dg2_ref)

        dh = _dot_nt(dg_ref[0], w_ref[0]) + _dot_nt(dg_ref[1], w_ref[1])
        dh = dh + _dot_nt(dv_ref[0], w_ref[2]) + _dot_nt(dv_ref[1], w_ref[3])
        x1 = x1_ref[...]
        r3 = _rstd(x1)
        xn = x1 * r3
        dg3_ref[...] += jnp.sum(dh * xn, axis=0, keepdims=True)
        dxn = dh * g3_ref[...]
        dx1 = dy_ref[...] + r3 * (dxn - xn * jnp.mean(dxn * xn, axis=-1, keepdims=True))
        dx1_ref[...] = dx1
        mix = mix_ref[...]
        rm = _rstd(mix)
        mn = mix * rm
        dg2_ref[...] += jnp.sum(dx1 * mn, axis=0, keepdims=True)
        dmn = dx1 * g2_ref[...]
        dmix_ref[...] = (rm * (dmn - mn * jnp.mean(dmn * mn, axis=-1, keepdims=True))).astype(BF16)

    row = pl.BlockSpec((ts, D_MODEL), lambda i: (i, 0))
    act = pl.BlockSpec((2, ts, FF_TILE), lambda i: (0, i, 0))
    vec = _full((1, D_MODEL))
    return _call(
        body, name="ffn_bwd_in",
        out_shape=[jax.ShapeDtypeStruct((s, D_MODEL), F32), jax.ShapeDtypeStruct((s, D_MODEL), BF16),
                   jax.ShapeDtypeStruct((1, D_MODEL), F32), jax.ShapeDtypeStruct((1, D_MODEL), F32)],
        grid=(s // ts,),
        in_specs=[act, act, _full(w_up.shape), row, row, row, vec, vec],
        out_specs=[row, row, vec, vec],
        compiler_params=_params(("arbitrary",), vmem_mb=56))(dug, duv, w_up, x1, dy, mix, g3, g2)


def _mix_bwd(dmix, w_out, attn, attn_scale):
    s = dmix.shape[0]
    ts = 512

    def body(dm_ref, w_ref, a_ref, as_ref, dp_ref, do_ref, das_ref):
        @pl.when(pl.program_id(0) == 0)
        def _():
            das_ref[...] = jnp.zeros_like(das_ref)

        dm = dm_ref[...]
        dp_ref[...] = _dot_nt(dm, w_ref[0:D_POOL, :])
        da = _dot_nt(dm, w_ref[D_POOL:, :])
        ao = a_ref[...]
        ra = _rstd(ao)
        an = ao * ra
        das_ref[...] += jnp.sum(da * an, axis=0, keepdims=True)
        dan = da * as_ref[...]
        do_ref[...] = (ra * (dan - an * jnp.mean(dan * an, axis=-1, keepdims=True))).astype(BF16)

    row = lambda w: pl.BlockSpec((ts, w), lambda i: (i, 0))
    return _call(
        body, name="mix_bwd",
        out_shape=[jax.ShapeDtypeStruct((s, D_POOL), F32), jax.ShapeDtypeStruct((s, D_ATTN), BF16),
                   jax.ShapeDtypeStruct((1, D_ATTN), F32)],
        grid=(s // ts,),
        in_specs=[row(D_MODEL), _full((D_MODEL, D_MODEL)), row(D_ATTN), _full((1, D_ATTN))],
        out_specs=[row(D_POOL), row(D_ATTN), _full((1, D_ATTN))],
        compiler_params=_params(("arbitrary",)))(dmix, w_out, attn, attn_scale)


def _attn_bwd(q, k, kt, vt, do, totals):
    s = q.shape[0]
    tb = ATT_BLOCK
    nq = s // tb

    def body(q_ref, do_ref, t_ref, k_hbm, kt_hbm, vt_hbm, dq_ref, dk_hbm, dv_hbm,
             k_scr, kt_scr, vt_scr, dkt_acc, dvt_acc, stage, *bufs):
        hp = pl.program_id(0)
        i = pl.program_id(1)
        lanes = pl.ds(pl.multiple_of(hp * HEAD_PAIR, HEAD_PAIR), HEAD_PAIR)

        @pl.when(i == 0)
        def _():
            pltpu.sync_copy(k_hbm.at[:, lanes], k_scr)
            pltpu.sync_copy(kt_hbm.at[:, lanes, :], kt_scr)
            pltpu.sync_copy(vt_hbm.at[:, lanes, :], vt_scr)
            dkt_acc[...] = jnp.zeros_like(dkt_acc)
            dvt_acc[...] = jnp.zeros_like(dvt_acc)

        upper = _tri("suffix")
        lower = _tri("prefix")
        causal = _causal_mask()
        lane = lax.broadcasted_iota(jnp.int32, (1, HEAD_PAIR), 1)
        first = lane < 64
        q2 = q_ref[...]
        do2 = do_ref[...]
        zero = jnp.zeros_like(q2)
        qs = (jnp.where(first, q2, zero), jnp.where(first, zero, q2))
        dos = (jnp.where(first, do2, zero), jnp.where(first, zero, do2))
        qcat_t = jnp.concatenate(qs, axis=0).astype(F32).T.astype(BF16)
        docat_t = jnp.concatenate(dos, axis=0).astype(F32).T.astype(BF16)
        tots = (t_ref[:, 0:1], t_ref[:, 1:2])

        z_ring, sg_ring = [[[bufs[8 * kind + 2 * slot + e] for e in range(2)] for slot in range(4)]
                           for kind in range(2)]
        in_buf, da_buf, dw_buf, pre_buf = [
            [[bufs[16 + 4 * kind + 2 * slot + e] for e in range(2)] for slot in range(2)]
            for kind in range(4)]
        pr_buf, dzr_buf, dzc_buf = bufs[32:34], bufs[34:36], bufs[36:38]

        for e in range(2):
            for slot in (2, 3):
                z_ring[slot][e][...] = jnp.full((tb, tb), NEG_BIG, F32)
            for slot in (1, 2, 3):
                sg_ring[slot][e][...] = jnp.zeros((tb, tb), F32)
            for buf in (in_buf, da_buf, dw_buf, pre_buf):
                buf[1][e][...] = jnp.zeros((tb, tb), F32)
        for buf in (pr_buf, dzr_buf, dzc_buf):
            buf[1][...] = jnp.zeros_like(buf[1])

        def rows(p):
            return pl.ds(pl.multiple_of(jnp.clip(p, 0, nq - 1) * tb, tb), tb)

        def split_heads(block):
            return jnp.concatenate([jnp.where(first, block, zero), jnp.where(first, zero, block)], axis=0)

        def trip(t, u, carry):
            w, r = u % 2, 1 - u % 2
            cs, cps, dq = list(carry[0:2]), list(carry[2:4]), carry[4]
            live4 = (t - 4 >= 0) & (t - 4 < i)
            dq = dq + jnp.where(live4, _dot(dzc_buf[r][...], split_heads(k_scr[rows(t - 4), :])), 0.0)
            dkt_acc[jnp.clip(t - 4, 0, nq - 1)] += jnp.where(live4, _dot(qcat_t, dzr_buf[r][...]), 0.0)
            dvt_acc[jnp.clip(t - 3, 0, nq - 1)] += _dot(docat_t, pr_buf[r][...])
            ktj = kt_scr[jnp.clip(t, 0, nq - 1)]
            for e in range(2):
                z_ring[u][e][...] = _dot(qs[e], ktj)
            vtj = vt_scr[jnp.clip(t - 1, 0, nq - 1)]
            for e in range(2):
                sp, sig = _softplus(z_ring[(u - 1) % 4][e][...], True)
                sg_ring[(u - 1) % 4][e][...] = sig
                in_buf[w][e][...] = _dot(sp.astype(BF16), upper)
                da_buf[w][e][...] = _dot(dos[e], vtj)
            for e in range(2):
                incl = in_buf[r][e][...]
                cs[e] = cs[e] + incl[:, 0:1]
                off = jnp.where(t - 2 < i, tots[e] - cs[e], -NEG_BIG)
                a = jnp.exp(z_ring[(u - 2) % 4][e][...] - incl - off)
                dw = a * da_buf[r][e][...]
                dw_buf[w][e][...] = dw
                pr_buf[w][e * tb:(e + 1) * tb, :] = a.astype(BF16)
                pre_buf[w][e][...] = _dot(dw.astype(BF16), lower)
            for e in range(2):
                pre = pre_buf[r][e][...] + cps[e]
                dzb = (dw_buf[r][e][...] - sg_ring[(u - 3) % 4][e][...] * pre).astype(BF16)
                cps[e] = pre[:, tb - 1:tb]
                dzr_buf[w][e * tb:(e + 1) * tb, :] = dzb
                dzc_buf[w][:, e * tb:(e + 1) * tb] = dzb
            return cs[0], cs[1], cps[0], cps[1], dq

        col = jnp.zeros((tb, 1), F32)
        carry = (col, col, col, col, jnp.zeros((tb, HEAD_PAIR), F32))
        def four_trips(n, cr):
            for u in range(4):
                cr = trip(4 * n + u, u, cr)
            return cr

        carry = lax.fori_loop(0, jnp.where(i > 0, (i + 7) // 4, 0), four_trips, carry)

        cps, dq = carry[2:4], carry[4]
        kj = k_scr[rows(i), :]
        dzs, probs = [], []
        for e in range(2):
            z = _dot(qs[e], kt_scr[i])
            sp, sig = _softplus(z, True)
            incl = _dot(jnp.where(causal, sp, 0.0).astype(BF16), upper)
            a = jnp.where(causal, jnp.exp(z - incl), 0.0)
            dw = a * _dot(dos[e], vt_scr[i])
            pre = _dot(dw.astype(BF16), lower) + cps[e]
            dzs.append(jnp.where(causal, dw - sig * pre, 0.0).astype(BF16))
            probs.append(a.astype(BF16))
        dq = dq + _dot(jnp.concatenate(dzs, axis=1), split_heads(kj))
        dkt_acc[i] += _dot(qcat_t, jnp.concatenate(dzs, axis=0))
        dvt_acc[i] += _dot(docat_t, jnp.concatenate(probs, axis=0))
        dq_ref[...] = (dq * Q_SCALE).astype(BF16)

        @pl.when(i == nq - 1)
        def _():
            for acc, dst in ((dkt_acc, dk_hbm), (dvt_acc, dv_hbm)):
                def flip(n, _, acc=acc):
                    at = pl.ds(pl.multiple_of(n * tb, tb), tb)
                    stage[at, :] = acc[n].T
                    return 0
                lax.fori_loop(0, nq, flip, 0)
                pltpu.sync_copy(stage, dst.at[:, lanes])

    blk = pl.BlockSpec((tb, HEAD_PAIR), lambda h, i: (i, h))
    grad = jax.ShapeDtypeStruct((s, D_ATTN), F32)
    return _call(
        body, name="attn_bwd",
        out_shape=[jax.ShapeDtypeStruct((s, D_ATTN), BF16), grad, grad],
        grid=(4, nq),
        in_specs=[blk, blk, pl.BlockSpec((None, tb, 2), lambda h, i: (h, i, 0)), ANY, ANY, ANY],
        out_specs=[blk, ANY, ANY],
        scratch_shapes=[pltpu.VMEM((s, HEAD_PAIR), BF16), pltpu.VMEM((nq, HEAD_PAIR, tb), BF16),
                        pltpu.VMEM((nq, HEAD_PAIR, tb), BF16),
                        pltpu.VMEM((nq, HEAD_PAIR, tb), F32), pltpu.VMEM((nq, HEAD_PAIR, tb), F32),
                        pltpu.VMEM((s, HEAD_PAIR), F32)]
        + [pltpu.VMEM((tb, tb), F32)] * 32
        + [pltpu.VMEM((2 * tb, tb), BF16)] * 4 + [pltpu.VMEM((tb, 2 * tb), BF16)] * 2,
        compiler_params=_params(("arbitrary", "arbitrary"), vmem_mb=60))(q, do, totals, k, kt, vt)


def _pool_bwd(u, dmp, w_pool, pool_scale):
    s = u.shape[0]
    ts = 512
    nr = s // ts
    per = ts // POOL_HALO

    def body(u_ref, halo_ref, dm_ref, wp_ref, ps_ref, du_ref, dwp_ref, dps_ref, ext_ref, y_ref, dext_ref):
        i = pl.program_id(0)
        rb = nr - 1 - i

        @pl.when(i == 0)
        def _():
            dext_ref[ts:, :] = jnp.zeros((POOL_HALO, D_POOL), F32)
            dwp_ref[...] = jnp.zeros_like(dwp_ref)
            dps_ref[...] = jnp.zeros_like(dps_ref)

        ext_ref[0:POOL_HALO, :] = jnp.where(rb > 0, halo_ref[...], 0.0)
        ext_ref[POOL_HALO:, :] = u_ref[...]
        ps, cnts = [], []
        for g, window in enumerate(POOL_WINDOWS):
            p, cnt = _pool_means(ext_ref, g, window, ts, rb * ts)
            ps.append(p.astype(BF16))
            cnts.append(cnt)
            y_ref[:, g * POOL_GROUP:(g + 1) * POOL_GROUP] = _dot(ps[g], wp_ref[g].astype(BF16))
        y = y_ref[...]
        r = _rstd(y)
        yn = y * r
        dm = dm_ref[...]
        dps_ref[...] += jnp.sum(dm * yn, axis=0, keepdims=True)
        dn = dm * ps_ref[...]
        dy = r * (dn - yn * jnp.mean(dn * yn, axis=-1, keepdims=True))
        for g, window in enumerate(POOL_WINDOWS):
            cols = slice(g * POOL_GROUP, (g + 1) * POOL_GROUP)
            dyg = dy[:, cols].astype(BF16)
            dwp_ref[g] += _dot_tn(ps[g], dyg)
            dp = _dot_nt(dyg, wp_ref[g].astype(BF16))
            dext_ref[0:ts, cols] = dp / cnts[g]
            acc = dext_ref[0:ts, cols]
            for d in range(1, window):
                acc = acc + dext_ref[d:d + ts, cols]
            du_ref[:, cols] = (acc - dp).astype(BF16)
        dext_ref[ts:, :] = dext_ref[0:POOL_HALO, :]

    rows = pl.BlockSpec((ts, D_POOL), lambda i: (nr - 1 - i, 0))
    return _call(
        body, name="pool_bwd",
        out_shape=[jax.ShapeDtypeStruct((s, D_POOL), BF16), jax.ShapeDtypeStruct(w_pool.shape, F32),
                   jax.ShapeDtypeStruct((1, D_POOL), F32)],
        grid=(nr,),
        in_specs=[rows,
                  pl.BlockSpec((POOL_HALO, D_POOL), lambda i: (jnp.maximum((nr - 1 - i) * per - 1, 0), 0)),
                  rows, _full(w_pool.shape), _full((1, D_POOL))],
        out_specs=[rows, _full(w_pool.shape), _full((1, D_POOL))],
        scratch_shapes=[pltpu.VMEM((ts + POOL_HALO, D_POOL), F32), pltpu.VMEM((ts, D_POOL), F32),
                        pltpu.VMEM((ts + POOL_HALO, D_POOL), F32)],
        compiler_params=_params(("arbitrary",)))(u, u, dmp, w_pool, pool_scale)


def _in_proj_bwd(du, dq, dk, dv, w_in, x, dx1, g1):
    s = x.shape[0]
    ts = 512

    def body(du_ref, dq_ref, dk_ref, dv_ref, w_ref, x_ref, dx1_ref, g_ref, gx_ref, dg_ref):
        @pl.when(pl.program_id(0) == 0)
        def _():
            dg_ref[...] = jnp.zeros_like(dg_ref)

        dh = _dot_nt(du_ref[...], w_ref[0]) + _dot_nt(dq_ref[...], w_ref[1])
        dh = dh + _dot_nt(dk_ref[...].astype(BF16), w_ref[2]) + _dot_nt(dv_ref[...].astype(BF16), w_ref[3])
        xv = x_ref[...]
        r = _rstd(xv)
        xn = xv * r
        dg_ref[...] += jnp.sum(dh * xn, axis=0, keepdims=True)
        dxn = dh * g_ref[...]
        gx_ref[...] = dx1_ref[...] + r * (dxn - xn * jnp.mean(dxn * xn, axis=-1, keepdims=True))

    row = lambda w: pl.BlockSpec((ts, w), lambda i: (i, 0))
    return _call(
        body, name="in_proj_bwd",
        out_shape=[jax.ShapeDtypeStruct((s, D_MODEL), F32), jax.ShapeDtypeStruct((1, D_MODEL), F32)],
        grid=(s // ts,),
        in_specs=[row(D_POOL)] * 4 + [_full(w_in.shape), row(D_MODEL), row(D_MODEL), _full((1, D_MODEL))],
        out_specs=[row(D_MODEL), _full((1, D_MODEL))],
        compiler_params=_params(("arbitrary",)))(du, dq, dk, dv, w_in, x, dx1, g1)


_SMALL = ("norm_mix_pre", "w_pool", "pool_scale", "attn_scale", "norm_mix_post",
          "norm_ffn_pre", "conv_b", "norm_ffn_post")
_SMALL_SIZE = {"norm_mix_pre": 1024, "w_pool": 65536, "pool_scale": 512, "attn_scale": 512,
               "norm_mix_post": 1024, "norm_ffn_pre": 1024, "conv_b": 5632, "norm_ffn_post": 1024}
_SMALL_ROWS = 600
_CONVW_ROWS = 132
_PACK_ROWS = _SMALL_ROWS + _CONVW_ROWS + 4


def _pack_small(parts):
    flat = jnp.concatenate([parts[n].reshape(-1) for n in _SMALL])
    flat = jnp.pad(flat, (0, _SMALL_ROWS * 128 - flat.shape[0]))
    return flat.reshape(_SMALL_ROWS, 128)


def _unpack_small(packed, like):
    flat = packed.reshape(-1)
    out, off = {}, 0
    for n in _SMALL:
        out[n] = flat[off:off + _SMALL_SIZE[n]].reshape(like[n].shape)
        off += _SMALL_SIZE[n]
    return out


def kernel(x, norm_mix_pre, w_in, w_pool, pool_scale, attn_scale, w_out, norm_mix_post, norm_ffn_pre, w_up, conv_w, conv_b, w_down, norm_ffn_post, loss_target, m_norm_mix_pre, m_w_in, m_w_pool, m_pool_scale, m_attn_scale, m_w_out, m_norm_mix_post, m_norm_ffn_pre, m_w_up, m_conv_w, m_conv_b, m_w_down, m_norm_ffn_post, v_norm_mix_pre, v_w_in, v_w_pool, v_pool_scale, v_attn_scale, v_w_out, v_norm_mix_post, v_norm_ffn_pre, v_w_up, v_conv_w, v_conv_b, v_w_down, v_norm_ffn_post):
    weights = dict(norm_mix_pre=norm_mix_pre, w_in=w_in, w_pool=w_pool, pool_scale=pool_scale,
                   attn_scale=attn_scale, w_out=w_out, norm_mix_post=norm_mix_post,
                   norm_ffn_pre=norm_ffn_pre, w_up=w_up, conv_w=conv_w, conv_b=conv_b,
                   w_down=w_down, norm_ffn_post=norm_ffn_post)
    mom1 = dict(norm_mix_pre=m_norm_mix_pre, w_in=m_w_in, w_pool=m_w_pool, pool_scale=m_pool_scale,
                attn_scale=m_attn_scale, w_out=m_w_out, norm_mix_post=m_norm_mix_post,
                norm_ffn_pre=m_norm_ffn_pre, w_up=m_w_up, conv_w=m_conv_w, conv_b=m_conv_b,
                w_down=m_w_down, norm_ffn_post=m_norm_ffn_post)
    mom2 = dict(norm_mix_pre=v_norm_mix_pre, w_in=v_w_in, w_pool=v_w_pool, pool_scale=v_pool_scale,
                attn_scale=v_attn_scale, w_out=v_w_out, norm_mix_post=v_norm_mix_post,
                norm_ffn_pre=v_norm_ffn_pre, w_up=v_w_up, conv_w=v_conv_w, conv_b=v_conv_b,
                w_down=v_w_down, norm_ffn_post=v_norm_ffn_post)
    order = list(weights)

    xs = x[0]
    target = loss_target[0]
    wp = w_pool[0]
    shard = lax.axis_index("x") * 2 + lax.axis_index("y")

    slot = shard.astype(jnp.int32).reshape(1)
    win_g, = _gather_shards([_cast_bf16(w_in[0], "cast_w_in")])
    lands = [_into_slot(w_out[0], slot, BF16, "cast_w_out"), _into_slot(w_up[0], slot, BF16, "cast_w_up"),
             _into_slot(w_down[0], slot, BF16, "cast_w_down"), _into_slot(conv_w[0], slot, F32, "place_conv_w")]
    g_send, g_recv, g_lands, g_token = _exchange_start(None, lands, win_g, "gather_start")
    convb_g = conv_b[0].reshape(N_SHARD, 1, FF_TILE)

    u, q, k, kt, vt, h1 = _in_proj(xs, norm_mix_pre + g_token[0:1, 0:1], win_g)
    mpool = _pool_fwd(u, wp, pool_scale)
    attn, totals = _attn_fwd(q, k, vt)
    _, (wout_g, wup_g, wdown_g, convw_g) = _exchange_wait(g_send, g_recv, g_lands, False, attn, "gather_wait")
    wout_f = wout_g.reshape(D_MODEL, D_MODEL)
    wdown_f = wdown_g.reshape(D_FF, D_MODEL)
    mattn, mix, x1, h2 = _mix_out(attn, mpool, xs, attn_scale, wout_f, norm_mix_post, norm_ffn_pre)
    upre_g, upre_v, f_in = _ffn_up(h2, wup_g, convw_g, convb_g)
    df, dy, loss_tile, d_post = _ffn_down(f_in, wdown_f, x1, target, norm_ffn_post)

    d_wdown = _tn_matmul(f_in, df[None], "dw_down")
    dug, duv, dcw_g, dcw_v, dcb_g, dcb_v = _ffn_bwd_act(df, wdown_f, upre_g, upre_v, convw_g, convb_g)
    d_wup = jnp.concatenate([_tn_matmul(h2[None], dug, "dw_up_gate")[0],
                             _tn_matmul(h2[None], duv, "dw_up_value")[0]], axis=0)
    early = [d_wup, d_wdown.reshape(N_SHARD, D_FF // N_SHARD, D_MODEL)]
    s_send, s_recv, s_thru, s_token = _exchange_start(
        early, [lax.empty((3,) + g.shape[1:], g.dtype) for g in early], d_wup, "scatter_start")
    dx1, dmix, d_ffn_pre, d_mix_post = _ffn_bwd_in(
        dug, duv, wup_g, x1, dy, mix, norm_ffn_pre + s_token[0:1, 0:1], norm_mix_post)
    d_wout = jnp.concatenate([_tn_matmul(mpool[None], dmix[None], "dw_out_pool")[0, 0],
                              _tn_matmul(mattn[None], dmix[None], "dw_out_attn")[0, 0]], axis=0)
    dmp, do, d_attn_scale = _mix_bwd(dmix, wout_f, attn, attn_scale)
    dq, dk, dv = _attn_bwd(q, k, kt, vt, do, totals)
    du, d_wpool, d_pool_scale = _pool_bwd(u, dmp, wp, pool_scale)
    d_win = jnp.stack([_tn_matmul(h1[None], t[None], "dw_in_%d" % n)[0, 0]
                       for n, t in enumerate((du, dq, dk, dv))])
    grad_x, d_mix_pre = _in_proj_bwd(du, dq, dk, dv, win_g, xs, dx1, norm_mix_pre)

    d_convw = jnp.concatenate([dcw_g, dcw_v], axis=0)
    d_convb = jnp.concatenate([dcb_g, dcb_v], axis=0).reshape(1, 2 * D_FF)
    small_parts = dict(norm_mix_pre=d_mix_pre, w_pool=d_wpool, pool_scale=d_pool_scale,
                       attn_scale=d_attn_scale, norm_mix_post=d_mix_post, norm_ffn_pre=d_ffn_pre,
                       conv_b=d_convb, norm_ffn_post=d_post)
    packed = jnp.concatenate([_pack_small(small_parts), d_convw.reshape(_CONVW_ROWS, 128),
                              loss_tile[0:4]], axis=0)
    late, gathered = _scatter_grads([d_win, d_wout.reshape(N_SHARD, D_MODEL // N_SHARD, D_MODEL)], packed)
    early_srcs, early_lands = _exchange_wait(s_send, s_recv, s_thru, True, grad_x, "scatter_wait")
    quarter = [_sum_slots(r, (3, 0, 1, 2), "sum_chips_%d" % n) for n, r in enumerate(late)]
    quarter += [_sum_own_and_received(early_srcs[n], slot, early_lands[n], "sum_chips_%d" % (n + 2))
                for n in range(2)]
    sibling = _swap_with_sibling(quarter)
    small_sum = _sum_slots(gathered, tuple(range(8)), "sum_small")

    results = {}
    for n, name in enumerate(("w_in", "w_out", "w_up", "w_down")):
        res = _adamw([quarter[n], sibling[n]], weights[name][0], mom1[name][0], mom2[name][0],
                     "adamw_" + name)
        results[name] = [t[None] for t in res]
    g_convw = lax.dynamic_slice_in_dim(
        small_sum[_SMALL_ROWS:_SMALL_ROWS + _CONVW_ROWS].reshape(N_SHARD, 3, FF_TILE), shard, 1, axis=0)[0]
    convw_pad = lambda t: jnp.pad(t, ((0, 5), (0, 0)))
    res = _adamw([convw_pad(g_convw)], convw_pad(conv_w[0]), convw_pad(m_conv_w[0]),
                 convw_pad(v_conv_w[0]), "adamw_conv_w")
    results["conv_w"] = [t[:3][None] for t in res]
    pack_w = _pack_small(weights)
    pack_m = _pack_small(mom1)
    pack_v = _pack_small(mom2)
    res = _adamw([small_sum[:_SMALL_ROWS]], pack_w, pack_m, pack_v, "adamw_small")
    unpacked = [_unpack_small(t, weights) for t in res]
    for name in _SMALL:
        results[name] = [t[name] for t in unpacked]

    loss = small_sum[_SMALL_ROWS + _CONVW_ROWS, 0]
    outs = [loss, grad_x[None]]
    for slot in range(4):
        outs.extend(results[name][slot] for name in order)
    return tuple(outs)
```

```python
import functools

import jax
import jax.numpy as jnp
from jax import lax
from jax.experimental import pallas as pl
from jax.experimental.pallas import tpu as pltpu

F32 = jnp.float32
BF16 = jnp.bfloat16

D_MODEL = 1024
D_POOL = 512
D_ATTN = 512
POOL_WINDOWS = (2, 4, 8, 16)
POOL_GROUP = 128
POOL_HALO = 16
CONV_HALO = 8
D_FF = 2816
FF_TILE = 1408
N_SHARD = 4
EPS = 1e-6
Q_SCALE = 0.125
ATT_BLOCK = 256
HEAD_PAIR = 128
MIB = 1 << 20
NEG_BIG = -1e30

ADAM_LR = 0.001
ADAM_B1 = 0.9
ADAM_B2 = 0.999
ADAM_EPS = 1e-08
ADAM_WD = 0.01
ADAM_STEP = 10

NT_DIMS = (((1,), (1,)), ((), ()))
TN_DIMS = (((0,), (0,)), ((), ()))
MESH = pl.DeviceIdType.MESH
ANY = pl.BlockSpec(memory_space=pl.ANY)
HBM_SPEC = pl.BlockSpec(memory_space=pltpu.HBM)
SEM_SPEC = pl.BlockSpec(memory_space=pltpu.SEMAPHORE)
DATAFLOW = pltpu.SideEffectType.DATAFLOW_SIDE_EFFECTING


def _call(body, **kw):
    return pl.pallas_call(body, **kw)


def _params(sem=None, vmem_mb=48):
    return pltpu.CompilerParams(dimension_semantics=sem, vmem_limit_bytes=vmem_mb * MIB)


def _rstd(v):
    return lax.rsqrt(jnp.mean(v * v, axis=-1, keepdims=True) + EPS)


def _dot(a, b):
    return jnp.dot(a, b, preferred_element_type=F32)


def _dot_nt(a, b):
    return lax.dot_general(a, b, NT_DIMS, preferred_element_type=F32)


def _dot_tn(a, b):
    return lax.dot_general(a, b, TN_DIMS, preferred_element_type=F32)


def _row_tile(rows, cap):
    t = min(rows, cap)
    t -= t % 8
    while rows % t:
        t -= 8
    return t


def _full(shape):
    nd = len(shape)
    return pl.BlockSpec(shape, lambda *_: (0,) * nd)


def _chip_peers():
    x, y, c = lax.axis_index("x"), lax.axis_index("y"), lax.axis_index("c")
    return x, y, c, [(1 - x, y), (x, 1 - y), (1 - x, 1 - y)]


def _cast_bf16(a, name):
    def body(a_ref, o_ref):
        o_ref[...] = a_ref[...].astype(BF16)

    return _call(body, name=name, out_shape=jax.ShapeDtypeStruct(a.shape, BF16),
                 grid=(1,), in_specs=[_full(a.shape)], out_specs=_full(a.shape),
                 compiler_params=_params(("arbitrary",)))(a)


def _into_slot(a, slot, dtype, name):
    nd = a.ndim

    def body(slot_ref, a_ref, o_ref):
        o_ref[...] = a_ref[...].astype(dtype)

    return _call(
        body, name=name, out_shape=jax.ShapeDtypeStruct((N_SHARD,) + a.shape, dtype),
        grid_spec=pltpu.PrefetchScalarGridSpec(
            num_scalar_prefetch=1, grid=(1,),
            in_specs=[pl.BlockSpec(a.shape, lambda i, slot_ref: (0,) * nd)],
            out_specs=pl.BlockSpec((None,) + a.shape, lambda i, slot_ref: (slot_ref[0],) + (0,) * nd)),
        compiler_params=_params(("arbitrary",)))(slot, a)


def _exchange_copies(srcs, lands, send, recv):
    x, y, c, chips = _chip_peers()
    copies = []
    for t in range(len(lands)):
        for k, (px, py) in enumerate(chips):
            copies.append(pltpu.make_async_remote_copy(
                src_ref=lands[t].at[2 * x + y] if srcs is None else srcs[t].at[2 * px + py],
                dst_ref=lands[t].at[2 * x + y] if srcs is None else lands[t].at[k],
                send_sem=send.at[3 * t + k], recv_sem=recv.at[3 * t + k],
                device_id=(px, py, c), device_id_type=MESH))
    return copies


def _exchange_start(srcs, lands, after, name):
    n = len(lands)
    operands = list(lands) if srcs is None else list(srcs) + list(lands)
    m = len(operands)

    def body(*refs):
        for cp in _exchange_copies(None if srcs is None else refs[:n], refs[m - n:m], refs[m + 1], refs[m + 2]):
            cp.start()
        refs[-1][...] = jnp.zeros_like(refs[-1])

    res = _call(
        body, name=name,
        out_shape=[pltpu.SemaphoreType.DMA((3 * n,)), pltpu.SemaphoreType.DMA((3 * n,))]
        + [pltpu.HBM(a.shape, a.dtype) for a in operands] + [jax.ShapeDtypeStruct((8, 128), F32)],
        in_specs=[HBM_SPEC] * m + [ANY],
        out_specs=[SEM_SPEC, SEM_SPEC] + [HBM_SPEC] * m + [pl.BlockSpec(memory_space=pltpu.VMEM)],
        input_output_aliases={j: j + 2 for j in range(m)},
        compiler_params=pltpu.CompilerParams(has_side_effects=DATAFLOW),
    )(*[pltpu.with_memory_space_constraint(a, pltpu.HBM) for a in operands], after)
    return res[0], res[1], res[2:2 + m], res[-1]


def _exchange_wait(send, recv, operands, scatter, after, name):
    m = len(operands)
    n = m // 2 if scatter else m

    def body(*refs):
        for cp in _exchange_copies(refs[:n] if scatter else None, refs[m - n:m], refs[m], refs[m + 1]):
            cp.wait_send()
            cp.wait_recv()

    res = _call(
        body, name=name, out_shape=[pltpu.HBM(a.shape, a.dtype) for a in operands],
        in_specs=[HBM_SPEC] * m + [SEM_SPEC, SEM_SPEC, ANY], out_specs=[HBM_SPEC] * m,
        input_output_aliases={j: j for j in range(m)},
        compiler_params=pltpu.CompilerParams(has_side_effects=DATAFLOW),
    )(*operands, send, recv, after)
    return res[:m - n], res[m - n:]


def _gather_shards(shards):
    n = len(shards)

    def body(*refs):
        ins, outs = refs[:n], refs[n:2 * n]
        send, recv, loc = refs[2 * n:]
        x, y, c, chips = _chip_peers()
        b = 2 * x + y
        local = [pltpu.make_async_copy(ins[t], outs[t].at[b], loc.at[t]) for t in range(n)]
        for cp in local:
            cp.start()
        remote = []
        for t in range(n):
            for k, (px, py) in enumerate(chips):
                remote.append(pltpu.make_async_remote_copy(
                    src_ref=ins[t], dst_ref=outs[t].at[b],
                    send_sem=send.at[3 * t + k], recv_sem=recv.at[3 * t + k],
                    device_id=(px, py, c), device_id_type=MESH))
        for cp in remote:
            cp.start()
        for cp in remote:
            cp.wait()
        for cp in local:
            cp.wait()

    return _call(
        body, name="gather_w_in",
        out_shape=[jax.ShapeDtypeStruct((N_SHARD,) + s.shape, s.dtype) for s in shards],
        in_specs=[ANY] * n, out_specs=[ANY] * n,
        scratch_shapes=[pltpu.SemaphoreType.DMA((3 * n,)), pltpu.SemaphoreType.DMA((3 * n,)),
                        pltpu.SemaphoreType.DMA((n,))],
    )(*shards)


def _scatter_grads(grads, small):
    n = len(grads)

    def body(*refs):
        ins, small_in = refs[:n], refs[n]
        outs, small_out = refs[n + 1:2 * n + 1], refs[2 * n + 1]
        send, recv, loc, ssend, srecv = refs[2 * n + 2:]
        x, y, c, chips = _chip_peers()
        b = 2 * x + y
        me = 4 * x + 2 * y + c
        local = [pltpu.make_async_copy(ins[t].at[b], outs[t].at[3], loc.at[t]) for t in range(n)]
        local.append(pltpu.make_async_copy(small_in, small_out.at[me], loc.at[n]))
        for cp in local:
            cp.start()
        remote = []
        for t in range(n):
            for k, (px, py) in enumerate(chips):
                remote.append(pltpu.make_async_remote_copy(
                    src_ref=ins[t].at[2 * px + py], dst_ref=outs[t].at[k],
                    send_sem=send.at[3 * t + k], recv_sem=recv.at[3 * t + k],
                    device_id=(px, py, c), device_id_type=MESH))
        for r in range(1, 8):
            px = 1 - x if r & 4 else x
            py = 1 - y if r & 2 else y
            pc = 1 - c if r & 1 else c
            remote.append(pltpu.make_async_remote_copy(
                src_ref=small_in, dst_ref=small_out.at[me],
                send_sem=ssend.at[r - 1], recv_sem=srecv.at[r - 1],
                device_id=(px, py, pc), device_id_type=MESH))
        for cp in remote:
            cp.start()
        for cp in remote:
            cp.wait()
        for cp in local:
            cp.wait()

    out_shape = [jax.ShapeDtypeStruct(g.shape, g.dtype) for g in grads]
    out_shape.append(jax.ShapeDtypeStruct((8,) + small.shape, small.dtype))
    res = _call(
        body, name="scatter_grads", out_shape=out_shape,
        in_specs=[ANY] * (n + 1), out_specs=[ANY] * (n + 1),
        scratch_shapes=[pltpu.SemaphoreType.DMA((3 * n,)), pltpu.SemaphoreType.DMA((3 * n,)),
                        pltpu.SemaphoreType.DMA((n + 1,)),
                        pltpu.SemaphoreType.DMA((7,)), pltpu.SemaphoreType.DMA((7,))],
    )(*grads, small)
    return res[:n], res[n]


def _swap_with_sibling(parts):
    n = len(parts)

    def body(*refs):
        ins, outs = refs[:n], refs[n:2 * n]
        send, recv = refs[2 * n:]
        x, y, c = lax.axis_index("x"), lax.axis_index("y"), lax.axis_index("c")
        copies = [pltpu.make_async_remote_copy(
            src_ref=ins[t], dst_ref=outs[t], send_sem=send.at[t], recv_sem=recv.at[t],
            device_id=(x, y, 1 - c), device_id_type=MESH) for t in range(n)]
        for cp in copies:
            cp.start()
        for cp in copies:
            cp.wait()

    return _call(
        body, name="swap_sibling",
        out_shape=[jax.ShapeDtypeStruct(p.shape, p.dtype) for p in parts],
        in_specs=[ANY] * n, out_specs=[ANY] * n,
        scratch_shapes=[pltpu.SemaphoreType.DMA((n,)), pltpu.SemaphoreType.DMA((n,))],
    )(*parts)


def _sum_slots(buf, order, name):
    k, rows, cols = buf.shape
    tr = _row_tile(rows, 256)

    def body(b_ref, o_ref):
        acc = b_ref[order[0]].astype(F32)
        for s in order[1:]:
            acc = acc + b_ref[s].astype(F32)
        o_ref[...] = acc

    return _call(body, name=name, out_shape=jax.ShapeDtypeStruct((rows, cols), F32),
                 grid=(rows // tr,),
                 in_specs=[pl.BlockSpec((k, tr, cols), lambda i: (0, i, 0))],
                 out_specs=pl.BlockSpec((tr, cols), lambda i: (i, 0)),
                 compiler_params=_params(("parallel",)))(buf)


def _sum_own_and_received(src, slot, land, name):
    _, rows, cols = src.shape
    tr = _row_tile(rows, 256)

    def body(slot_ref, s_ref, l_ref, o_ref):
        acc = s_ref[...].astype(F32)
        for k in range(3):
            acc = acc + l_ref[k].astype(F32)
        o_ref[...] = acc

    return _call(
        body, name=name, out_shape=jax.ShapeDtypeStruct((rows, cols), F32),
        grid_spec=pltpu.PrefetchScalarGridSpec(
            num_scalar_prefetch=1, grid=(rows // tr,),
            in_specs=[pl.BlockSpec((None, tr, cols), lambda i, slot_ref: (slot_ref[0], i, 0)),
                      pl.BlockSpec((3, tr, cols), lambda i, slot_ref: (0, i, 0))],
            out_specs=pl.BlockSpec((tr, cols), lambda i, slot_ref: (i, 0))),
        compiler_params=_params(("parallel",)))(slot, src, land)


def _adamw(grad_parts, w, m, v, name):
    rows, cols = w.shape
    tr = _row_tile(rows, 256)
    npart = len(grad_parts)

    def body(*refs):
        gp = refs[:npart]
        w_ref, m_ref, v_ref, g_out, d_out, m_out, v_out = refs[npart:]
        g = gp[0][...]
        for p in gp[1:]:
            g = g + p[...]
        mm = ADAM_B1 * m_ref[...] + (1.0 - ADAM_B1) * g
        vv = ADAM_B2 * v_ref[...] + (1.0 - ADAM_B2) * jnp.square(g)
        m_hat = mm / (1.0 - ADAM_B1 ** ADAM_STEP)
        v_hat = vv / (1.0 - ADAM_B2 ** ADAM_STEP)
        g_out[...] = g
        d_out[...] = -ADAM_LR * (m_hat / (jnp.sqrt(v_hat) + ADAM_EPS) + ADAM_WD * w_ref[...])
        m_out[...] = mm
        v_out[...] = vv

    spec = pl.BlockSpec((tr, cols), lambda i: (i, 0))
    shp = jax.ShapeDtypeStruct((rows, cols), F32)
    return _call(body, name=name, out_shape=[shp] * 4, grid=(rows // tr,),
                 in_specs=[spec] * (npart + 3), out_specs=[spec] * 4,
                 compiler_params=_params(("parallel",)))(*grad_parts, w, m, v)


def _in_proj(x, g1, w_in):
    s = x.shape[0]
    ts = 512

    def body(x_ref, g_ref, w_ref, u_ref, q_ref, k_ref, v_ref, kt_ref, vt_ref, h_ref):
        xv = x_ref[...]
        h = (xv * _rstd(xv) * g_ref[...]).astype(BF16)
        h_ref[...] = h
        u_ref[...] = _dot(h, w_ref[0])
        q_ref[...] = (_dot(h, w_ref[1]) * Q_SCALE).astype(BF16)
        k = _dot(h, w_ref[2])
        k_ref[...] = k.astype(BF16)
        v = _dot(h, w_ref[3])
        v_ref[...] = v.astype(BF16)
        for src, dst in ((k, kt_ref), (v, vt_ref)):
            src_t = src.T.astype(BF16)
            for n in range(ts // ATT_BLOCK):
                dst[n] = src_t[:, n * ATT_BLOCK:(n + 1) * ATT_BLOCK]

    row = lambda w: pl.BlockSpec((ts, w), lambda i: (i, 0))
    half = jax.ShapeDtypeStruct((s, D_POOL), BF16)
    return _call(
        body, name="in_proj",
        out_shape=[jax.ShapeDtypeStruct((s, D_POOL), F32), half, half, half,
                   jax.ShapeDtypeStruct((s // ATT_BLOCK, D_ATTN, ATT_BLOCK), BF16),
                   jax.ShapeDtypeStruct((s // ATT_BLOCK, D_ATTN, ATT_BLOCK), BF16),
                   jax.ShapeDtypeStruct((s, D_MODEL), BF16)],
        grid=(s // ts,),
        in_specs=[row(D_MODEL), _full((1, D_MODEL)), _full(w_in.shape)],
        out_specs=[row(D_POOL)] * 4
        + [pl.BlockSpec((ts // ATT_BLOCK, D_ATTN, ATT_BLOCK), lambda i: (i, 0, 0))] * 2 + [row(D_MODEL)],
        compiler_params=_params(("parallel",)))(x, g1, w_in)


def _pool_means(ext_ref, g, window, ts, row0):
    cols = slice(g * POOL_GROUP, (g + 1) * POOL_GROUP)
    cur = ext_ref[POOL_HALO:POOL_HALO + ts, cols]
    acc = cur
    for d in range(1, window):
        acc = acc + ext_ref[POOL_HALO - d:POOL_HALO - d + ts, cols]
    t1 = row0 + 1 + lax.broadcasted_iota(jnp.int32, (ts, 1), 0)
    cnt = jnp.minimum(t1, window).astype(F32)
    return acc / cnt - cur, cnt


def _pool_fwd(u, w_pool, pool_scale):
    s = u.shape[0]
    ts = 512
    per = ts // POOL_HALO

    def body(u_ref, halo_ref, wp_ref, ps_ref, o_ref, ext_ref, y_ref):
        i = pl.program_id(0)
        ext_ref[0:POOL_HALO, :] = jnp.where(i > 0, halo_ref[...], 0.0)
        ext_ref[POOL_HALO:, :] = u_ref[...]
        for g, window in enumerate(POOL_WINDOWS):
            p, _ = _pool_means(ext_ref, g, window, ts, i * ts)
            y_ref[:, g * POOL_GROUP:(g + 1) * POOL_GROUP] = _dot(
                p.astype(BF16), wp_ref[g].astype(BF16))
        y = y_ref[...]
        o_ref[...] = (y * _rstd(y) * ps_ref[...]).astype(BF16)

    return _call(
        body, name="pool_fwd", out_shape=jax.ShapeDtypeStruct((s, D_POOL), BF16),
        grid=(s // ts,),
        in_specs=[pl.BlockSpec((ts, D_POOL), lambda i: (i, 0)),
                  pl.BlockSpec((POOL_HALO, D_POOL), lambda i: (jnp.maximum(i * per - 1, 0), 0)),
                  _full(w_pool.shape), _full((1, D_POOL))],
        out_specs=pl.BlockSpec((ts, D_POOL), lambda i: (i, 0)),
        scratch_shapes=[pltpu.VMEM((ts + POOL_HALO, D_POOL), F32), pltpu.VMEM((ts, D_POOL), F32)],
        compiler_params=_params(("parallel",)))(u, u, w_pool, pool_scale)


def _tri(kind):
    r = lax.broadcasted_iota(jnp.int32, (ATT_BLOCK, ATT_BLOCK), 0)
    c = lax.broadcasted_iota(jnp.int32, (ATT_BLOCK, ATT_BLOCK), 1)
    return jnp.where(r >= c if kind == "suffix" else r <= c, 1.0, 0.0).astype(BF16)


def _causal_mask():
    r = lax.broadcasted_iota(jnp.int32, (ATT_BLOCK, ATT_BLOCK), 0)
    c = lax.broadcasted_iota(jnp.int32, (ATT_BLOCK, ATT_BLOCK), 1)
    return c < r


def _softplus(z, with_sigmoid=False):
    ope = 1.0 + jnp.exp(jnp.minimum(z, 80.0))
    sp = jnp.maximum(z, jnp.log(ope))
    if with_sigmoid:
        return sp, 1.0 - 1.0 / ope
    return sp


def _attn_fwd(q, k, vt):
    s = q.shape[0]
    tb = ATT_BLOCK
    nq = s // tb

    def body(q_ref, k_ref, vt_ref, o_ref, t_ref, *bufs):
        i = pl.program_id(1)
        suffix = _tri("prefix")
        r_idx = lax.broadcasted_iota(jnp.int32, (tb, tb), 0)
        c_idx = lax.broadcasted_iota(jnp.int32, (tb, tb), 1)
        causal = r_idx < c_idx
        lane = lax.broadcasted_iota(jnp.int32, (1, HEAD_PAIR), 1)
        first = lane < 64
        top = lax.broadcasted_iota(jnp.int32, (HEAD_PAIR, 1), 0) < 64
        q2 = q_ref[...]
        zero = jnp.zeros_like(q2)
        qs_t = tuple(jnp.where(first, q2, zero).astype(F32).T.astype(BF16) if e == 0 else
                     jnp.where(first, zero, q2).astype(F32).T.astype(BF16) for e in range(2))

        def values_t(j):
            vt = vt_ref[j]
            none = jnp.zeros_like(vt)
            return jnp.concatenate([jnp.where(top, vt, none), jnp.where(top, none, vt)], axis=1)

        z_ring = [[bufs[2 * slot + e] for e in range(2)] for slot in range(4)]
        in_buf = [[bufs[8 + 2 * slot + e] for e in range(2)] for slot in range(2)]
        pr_buf = [bufs[12], bufs[13]]

        def block_of(p):
            return jnp.clip(i - p, 0, nq - 1)

        def trip(t, u, carry):
            first_trips = isinstance(t, int)
            w, r = u % 2, 1 - u % 2
            cs, o = list(carry[0:2]), carry[2]
            if not first_trips or t >= 3:
                o = o + jnp.where(t - 3 <= i, _dot(values_t(block_of(t - 3)), pr_buf[r][...]), 0.0)
            kj = k_ref[pl.ds(pl.multiple_of(block_of(t) * tb, tb), tb), :]
            for e in range(2):
                z_ring[u][e][...] = _dot(kj, qs_t[e])
            if not first_trips or t >= 1:
                for e in range(2):
                    sp = _softplus(z_ring[(u - 1) % 4][e][...])
                    if first_trips and t == 1:
                        sp = jnp.where(causal, sp, 0.0)
                    in_buf[w][e][...] = _dot(suffix, sp.astype(BF16))
            if not first_trips or t >= 2:
                live2 = t - 2 <= i
                for e in range(2):
                    incl = in_buf[r][e][...]
                    arg = z_ring[(u - 2) % 4][e][...] - incl - jnp.where(live2, cs[e], -NEG_BIG)
                    if first_trips and t == 2:
                        arg = jnp.where(causal, arg, NEG_BIG)
                    pr_buf[w][e * tb:(e + 1) * tb, :] = jnp.exp(arg).astype(BF16)
                    cs[e] = jnp.where(live2, cs[e] + incl[0:1, :], cs[e])
            return cs[0], cs[1], o

        def four_trips(n, cr):
            for u in range(4):
                cr = trip(4 * n + u, u, cr)
            return cr

        row = jnp.zeros((1, tb), F32)
        carry = four_trips(0, (row, row, jnp.zeros((HEAD_PAIR, tb), F32)))
        carry = lax.fori_loop(1, (i + 7) // 4, four_trips, carry)
        o_ref[...] = carry[2].T
        totals = jnp.where(r_idx == 0, carry[0], jnp.where(r_idx == 1, carry[1], 0.0))
        t_ref[...] = totals.T[:, 0:2]

    score_buf = pltpu.VMEM((tb, tb), F32)
    return _call(
        body, name="attn_fwd",
        out_shape=[jax.ShapeDtypeStruct((s, D_ATTN), F32),
                   jax.ShapeDtypeStruct((4, s, 2), F32)],
        grid=(4, nq),
        in_specs=[pl.BlockSpec((tb, HEAD_PAIR), lambda h, i: (i, h)),
                  pl.BlockSpec((s, HEAD_PAIR), lambda h, i: (0, h)),
                  pl.BlockSpec((nq, HEAD_PAIR, tb), lambda h, i: (0, h, 0))],
        out_specs=[pl.BlockSpec((tb, HEAD_PAIR), lambda h, i: (i, h)),
                   pl.BlockSpec((None, tb, 2), lambda h, i: (h, i, 0))],
        scratch_shapes=[score_buf] * 12 + [pltpu.VMEM((2 * tb, tb), BF16)] * 2,
        compiler_params=_params(("arbitrary", "arbitrary")))(q, k, vt)


def _mix_out(attn, mpool, x, attn_scale, w_out, g2, g3):
    s = x.shape[0]
    ts = 512

    def body(a_ref, p_ref, x_ref, as_ref, w_ref, g2_ref, g3_ref, ma_ref, mix_ref, x1_ref, h2_ref):
        ao = a_ref[...]
        ma = (ao * _rstd(ao) * as_ref[...]).astype(BF16)
        ma_ref[...] = ma
        mix = _dot(p_ref[...], w_ref[0:D_POOL, :]) + _dot(ma, w_ref[D_POOL:, :])
        mix_ref[...] = mix
        x1 = x_ref[...] + mix * _rstd(mix) * g2_ref[...]
        x1_ref[...] = x1
        h2_ref[...] = (x1 * _rstd(x1) * g3_ref[...]).astype(BF16)

    row = lambda w: pl.BlockSpec((ts, w), lambda i: (i, 0))
    return _call(
        body, name="mix_out",
        out_shape=[jax.ShapeDtypeStruct((s, D_ATTN), BF16), jax.ShapeDtypeStruct((s, D_MODEL), F32),
                   jax.ShapeDtypeStruct((s, D_MODEL), F32), jax.ShapeDtypeStruct((s, D_MODEL), BF16)],
        grid=(s // ts,),
        in_specs=[row(D_ATTN), row(D_POOL), row(D_MODEL), _full((1, D_ATTN)),
                  _full((D_MODEL, D_MODEL)), _full((1, D_MODEL)), _full((1, D_MODEL))],
        out_specs=[row(D_ATTN), row(D_MODEL), row(D_MODEL), row(D_MODEL)],
        compiler_params=_params(("parallel",)))(attn, mpool, x, attn_scale, w_out, g2, g3)


def _earlier_rows(halo, block):
    ts = block.shape[0]
    ext = jnp.concatenate([halo, block], axis=0)
    return tuple(pltpu.roll(ext, d, axis=0)[CONV_HALO:CONV_HALO + ts, :] for d in (1, 2))


def _later_rows(block, halo):
    ts = block.shape[0]
    ext = jnp.concatenate([block, halo], axis=0)
    return tuple(pltpu.roll(ext, ts + CONV_HALO - d, axis=0)[0:ts, :] for d in (1, 2))


def _conv_rows(x0, x1, x2, cw, cb):
    y = cb + cw[0:1, :] * x2
    y = y + cw[1:2, :] * x1
    return y + cw[2:3, :] * x0


def _sigmoid(v):
    return 1.0 / (1.0 + jnp.exp(-v))


def _ffn_up(h2, w_up, conv_w, conv_b):
    s = h2.shape[0]
    ts = 256
    tn = FF_TILE

    def body(h_ref, wg_ref, wv_ref, cwg_ref, cwv_ref, cbg_ref, cbv_ref,
             ug_ref, uv_ref, f_ref, tailg, tailv):
        i = pl.program_id(1)

        @pl.when(i == 0)
        def _():
            tailg[...] = jnp.zeros_like(tailg)
            tailv[...] = jnp.zeros_like(tailv)

        h = h_ref[...]
        ug = _dot(h, wg_ref[...])
        uv = _dot(h, wv_ref[...])
        ug_ref[...] = ug
        uv_ref[...] = uv
        gate = _conv_rows(ug, *_earlier_rows(tailg[...], ug), cwg_ref[...], cbg_ref[...])
        val = _conv_rows(uv, *_earlier_rows(tailv[...], uv), cwv_ref[...], cbv_ref[...])
        f_ref[...] = (gate * _sigmoid(gate) * val).astype(BF16)
        tailg[...] = ug[ts - CONV_HALO:, :]
        tailv[...] = uv[ts - CONV_HALO:, :]

    out_blk = pl.BlockSpec((None, ts, tn), lambda n, i: (n, i, 0))
    act = jax.ShapeDtypeStruct((2, s, tn), F32)
    return _call(
        body, name="ffn_up",
        out_shape=[act, act, jax.ShapeDtypeStruct((2, s, tn), BF16)],
        grid=(2, s // ts),
        in_specs=[pl.BlockSpec((ts, D_MODEL), lambda n, i: (i, 0)),
                  pl.BlockSpec((None, D_MODEL, tn), lambda n, i: (n, 0, 0)),
                  pl.BlockSpec((None, D_MODEL, tn), lambda n, i: (n + 2, 0, 0)),
                  pl.BlockSpec((None, 3, tn), lambda n, i: (n, 0, 0)),
                  pl.BlockSpec((None, 3, tn), lambda n, i: (n + 2, 0, 0)),
                  pl.BlockSpec((None, 1, tn), lambda n, i: (n, 0, 0)),
                  pl.BlockSpec((None, 1, tn), lambda n, i: (n + 2, 0, 0))],
        out_specs=[out_blk, out_blk, out_blk],
        scratch_shapes=[pltpu.VMEM((CONV_HALO, tn), F32), pltpu.VMEM((CONV_HALO, tn), F32)],
        compiler_params=_params(("arbitrary", "arbitrary")))(
            h2, w_up, w_up, conv_w, conv_w, conv_b, conv_b)


def _ffn_down(f_in, w_down, x1, target, g4):
    s = x1.shape[0]
    ts = 512

    def body(f_ref, w_ref, x1_ref, t_ref, g_ref, df_ref, dy_ref, loss_ref, dg_ref):
        @pl.when(pl.program_id(0) == 0)
        def _():
            loss_ref[...] = jnp.zeros_like(loss_ref)
            dg_ref[...] = jnp.zeros_like(dg_ref)

        f = _dot(f_ref[0], w_ref[0:FF_TILE, :]) + _dot(f_ref[1], w_ref[FF_TILE:, :])
        rf = _rstd(f)
        fn = f * rf
        g = g_ref[...]
        err = (x1_ref[...] + fn * g) - t_ref[...]
        loss_ref[...] += 0.5 * jnp.sum(jnp.mean(err * err, axis=-1))
        dy = err * (1.0 / D_MODEL)
        dy_ref[...] = dy
        dg_ref[...] += jnp.sum(dy * fn, axis=0, keepdims=True)
        dfn = dy * g
        df_ref[...] = (rf * (dfn - fn * jnp.mean(dfn * fn, axis=-1, keepdims=True))).astype(BF16)

    row = pl.BlockSpec((ts, D_MODEL), lambda i: (i, 0))
    return _call(
        body, name="ffn_down",
        out_shape=[jax.ShapeDtypeStruct((s, D_MODEL), BF16), jax.ShapeDtypeStruct((s, D_MODEL), F32),
                   jax.ShapeDtypeStruct((8, 128), F32), jax.ShapeDtypeStruct((1, D_MODEL), F32)],
        grid=(s // ts,),
        in_specs=[pl.BlockSpec((2, ts, FF_TILE), lambda i: (0, i, 0)), _full((D_FF, D_MODEL)),
                  row, row, _full((1, D_MODEL))],
        out_specs=[row, row, _full((8, 128)), _full((1, D_MODEL))],
        compiler_params=_params(("arbitrary",)))(f_in, w_down, x1, target, g4)


def _tn_matmul(a, b, name, ts=512):
    na, s, ka = a.shape
    nb, _, nbc = b.shape
    steps = s // ts

    def body(a_ref, b_ref, o_ref, acc_ref):
        @pl.when(pl.program_id(2) == 0)
        def _():
            acc_ref[...] = jnp.zeros_like(acc_ref)

        acc_ref[...] += _dot_tn(a_ref[...].astype(BF16), b_ref[...].astype(BF16))

        @pl.when(pl.program_id(2) == steps - 1)
        def _():
            o_ref[...] = acc_ref[...].astype(BF16)

    return _call(
        body, name=name, out_shape=jax.ShapeDtypeStruct((na, nb, ka, nbc), BF16),
        grid=(na, nb, steps),
        in_specs=[pl.BlockSpec((None, ts, ka), lambda i, j, r: (i, r, 0)),
                  pl.BlockSpec((None, ts, nbc), lambda i, j, r: (j, r, 0))],
        out_specs=pl.BlockSpec((None, None, ka, nbc), lambda i, j, r: (i, j, 0, 0)),
        scratch_shapes=[pltpu.VMEM((ka, nbc), F32)],
        compiler_params=_params(("parallel", "parallel", "arbitrary")))(a, b)


def _ffn_bwd_act(df, w_down, upre_g, upre_v, conv_w, conv_b):
    s = df.shape[0]
    ts = 256
    tn = FF_TILE
    nr = s // ts
    per = ts // CONV_HALO

    def body(df_ref, wd_ref, ug_ref, uv_ref, hg_ref, hv_ref, cwg_ref, cwv_ref, cbg_ref, cbv_ref,
             dug_ref, duv_ref, dcwg_ref, dcwv_ref, dcbg_ref, dcbv_ref, headg, headv):
        i = pl.program_id(1)
        first_rows = i == nr - 1

        @pl.when(i == 0)
        def _():
            for r in (headg, headv, dcwg_ref, dcwv_ref, dcbg_ref, dcbv_ref):
                r[...] = jnp.zeros_like(r)

        ug, uv = ug_ref[...], uv_ref[...]
        cwg, cwv = cwg_ref[...], cwv_ref[...]
        gate = _conv_rows(ug, *_earlier_rows(jnp.where(first_rows, 0.0, hg_ref[...]), ug), cwg, cbg_ref[...])
        val = _conv_rows(uv, *_earlier_rows(jnp.where(first_rows, 0.0, hv_ref[...]), uv), cwv, cbv_ref[...])
        sg = _sigmoid(gate)
        dfin = _dot_nt(df_ref[...], wd_ref[...])
        dval = dfin * (gate * sg)
        dgate = dfin * val * (sg * (1.0 + gate * (1.0 - sg)))

        def conv_bwd(dact, x, head, cw, dcw_ref, dcb_ref, du_ref):
            d1, d2 = _later_rows(dact, head[...])
            dcb_ref[...] += jnp.sum(dact, axis=0, keepdims=True)
            for kk, shifted in enumerate((d2, d1, dact)):
                dcw_ref[kk:kk + 1, :] += jnp.sum(x * shifted, axis=0, keepdims=True)
            du_ref[...] = (cw[2:3, :] * dact + cw[1:2, :] * d1 + cw[0:1, :] * d2).astype(BF16)
            head[...] = dact[0:CONV_HALO, :]

        conv_bwd(dgate, ug, headg, cwg, dcwg_ref, dcbg_ref, dug_ref)
        conv_bwd(dval, uv, headv, cwv, dcwv_ref, dcbv_ref, duv_ref)

    rows = lambda n, i: (n, nr - 1 - i, 0)
    halo = lambda n, i: (n, jnp.maximum((nr - 1 - i) * per - 1, 0), 0)
    act_blk = pl.BlockSpec((None, ts, tn), rows)
    halo_blk = pl.BlockSpec((None, CONV_HALO, tn), halo)
    cw_blk = lambda off: pl.BlockSpec((None, 3, tn), lambda n, i: (n + off, 0, 0))
    cb_blk = lambda off: pl.BlockSpec((None, 1, tn), lambda n, i: (n + off, 0, 0))
    acc_w = pl.BlockSpec((None, 3, tn), lambda n, i: (n, 0, 0))
    acc_b = pl.BlockSpec((None, 1, tn), lambda n, i: (n, 0, 0))
    dact = jax.ShapeDtypeStruct((2, s, tn), BF16)
    return _call(
        body, name="ffn_bwd_act",
        out_shape=[dact, dact, jax.ShapeDtypeStruct((2, 3, tn), F32), jax.ShapeDtypeStruct((2, 3, tn), F32),
                   jax.ShapeDtypeStruct((2, 1, tn), F32), jax.ShapeDtypeStruct((2, 1, tn), F32)],
        grid=(2, nr),
        in_specs=[pl.BlockSpec((ts, D_MODEL), lambda n, i: (nr - 1 - i, 0)),
                  pl.BlockSpec((tn, D_MODEL), lambda n, i: (n, 0)),
                  act_blk, act_blk, halo_blk, halo_blk,
                  cw_blk(0), cw_blk(2), cb_blk(0), cb_blk(2)],
        out_specs=[act_blk, act_blk, acc_w, acc_w, acc_b, acc_b],
        scratch_shapes=[pltpu.VMEM((CONV_HALO, tn), F32)] * 2,
        compiler_params=_params(("arbitrary", "arbitrary")))(
            df, w_down, upre_g, upre_v, upre_g, upre_v, conv_w, conv_w, conv_b, conv_b)


def _ffn_bwd_in(dug, duv, w_up, x1, dy, mix, g3, g2):
    s = x1.shape[0]
    ts = 256

    def body(dg_ref, dv_ref, w_ref, x1_ref, dy_ref, mix_ref, g3_ref, g2_ref,
             dx1_ref, dmix_ref, dg3_ref, dg2_ref):
        @pl.when(pl.program_id(0) == 0)
        def _():
            dg3_ref[...] = jnp.zeros_like(dg3_ref)
            dg2_ref[...] = jnp.zeros_like(dg2_ref)

        dh = _dot_nt(dg_ref[0], w_ref[0]) + _dot_nt(dg_ref[1], w_ref[1])
        dh = dh + _dot_nt(dv_ref[0], w_ref[2]) + _dot_nt(dv_ref[1], w_ref[3])
        x1 = x1_ref[...]
        r3 = _rstd(x1)
        xn = x1 * r3
        dg3_ref[...] += jnp.sum(dh * xn, axis=0, keepdims=True)
        dxn = dh * g3_ref[...]
        dx1 = dy_ref[...] + r3 * (dxn - xn * jnp.mean(dxn * xn, axis=-1, keepdims=True))
        dx1_ref[...] = dx1
        mix = mix_ref[...]
        rm = _rstd(mix)
        mn = mix * rm
        dg2_ref[...] += jnp.sum(dx1 * mn, axis=0, keepdims=True)
        dmn = dx1 * g2_ref[...]
        dmix_ref[...] = (rm * (dmn - mn * jnp.mean(dmn * mn, axis=-1, keepdims=True))).astype(BF16)

    row = pl.BlockSpec((ts, D_MODEL), lambda i: (i, 0))
    act = pl.BlockSpec((2, ts, FF_TILE), lambda i: (0, i, 0))
    vec = _full((1, D_MODEL))
    return _call(
        body, name="ffn_bwd_in",
        out_shape=[jax.ShapeDtypeStruct((s, D_MODEL), F32), jax.ShapeDtypeStruct((s, D_MODEL), BF16),
                   jax.ShapeDtypeStruct((1, D_MODEL), F32), jax.ShapeDtypeStruct((1, D_MODEL), F32)],
        grid=(s // ts,),
        in_specs=[act, act, _full(w_up.shape), row, row, row, vec, vec],
        out_specs=[row, row, vec, vec],
        compiler_params=_params(("arbitrary",), vmem_mb=56))(dug, duv, w_up, x1, dy, mix, g3, g2)


def _mix_bwd(dmix, w_out, attn, attn_scale):
    s = dmix.shape[0]
    ts = 512

    def body(dm_ref, w_ref, a_ref, as_ref, dp_ref, do_ref, das_ref):
        @pl.when(pl.program_id(0) == 0)
        def _():
            das_ref[...] = jnp.zeros_like(das_ref)

        dm = dm_ref[...]
        dp_ref[...] = _dot_nt(dm, w_ref[0:D_POOL, :])
        da = _dot_nt(dm, w_ref[D_POOL:, :])
        ao = a_ref[...]
        ra = _rstd(ao)
        an = ao * ra
        das_ref[...] += jnp.sum(da * an, axis=0, keepdims=True)
        dan = da * as_ref[...]
        do_ref[...] = (ra * (dan - an * jnp.mean(dan * an, axis=-1, keepdims=True))).astype(BF16)

    row = lambda w: pl.BlockSpec((ts, w), lambda i: (i, 0))
    return _call(
        body, name="mix_bwd",
        out_shape=[jax.ShapeDtypeStruct((s, D_POOL), F32), jax.ShapeDtypeStruct((s, D_ATTN), BF16),
                   jax.ShapeDtypeStruct((1, D_ATTN), F32)],
        grid=(s // ts,),
        in_specs=[row(D_MODEL), _full((D_MODEL, D_MODEL)), row(D_ATTN), _full((1, D_ATTN))],
        out_specs=[row(D_POOL), row(D_ATTN), _full((1, D_ATTN))],
        compiler_params=_params(("arbitrary",)))(dmix, w_out, attn, attn_scale)


def _attn_bwd(q, k, kt, vt, do, totals):
    s = q.shape[0]
    tb = ATT_BLOCK
    nq = s // tb

    def body(q_ref, do_ref, t_ref, k_hbm, kt_hbm, vt_hbm, dq_ref, dk_hbm, dv_hbm,
             k_scr, kt_scr, vt_scr, dkt_acc, dvt_acc, stage, *bufs):
        hp = pl.program_id(0)
        i = pl.program_id(1)
        lanes = pl.ds(pl.multiple_of(hp * HEAD_PAIR, HEAD_PAIR), HEAD_PAIR)

        @pl.when(i == 0)
        def _():
            pltpu.sync_copy(k_hbm.at[:, lanes], k_scr)
            pltpu.sync_copy(kt_hbm.at[:, lanes, :], kt_scr)
            pltpu.sync_copy(vt_hbm.at[:, lanes, :], vt_scr)
            dkt_acc[...] = jnp.zeros_like(dkt_acc)
            dvt_acc[...] = jnp.zeros_like(dvt_acc)

        upper = _tri("suffix")
        lower = _tri("prefix")
        causal = _causal_mask()
        lane = lax.broadcasted_iota(jnp.int32, (1, HEAD_PAIR), 1)
        first = lane < 64
        q2 = q_ref[...]
        do2 = do_ref[...]
        zero = jnp.zeros_like(q2)
        qs = (jnp.where(first, q2, zero), jnp.where(first, zero, q2))
        dos = (jnp.where(first, do2, zero), jnp.where(first, zero, do2))
        qcat_t = jnp.concatenate(qs, axis=0).astype(F32).T.astype(BF16)
        docat_t = jnp.concatenate(dos, axis=0).astype(F32).T.astype(BF16)
        tots = (t_ref[:, 0:1], t_ref[:, 1:2])

        z_ring, sg_ring = [[[bufs[8 * kind + 2 * slot + e] for e in range(2)] for slot in range(4)]
                           for kind in range(2)]
        in_buf, da_buf, dw_buf, pre_buf = [
            [[bufs[16 + 4 * kind + 2 * slot + e] for e in range(2)] for slot in range(2)]
            for kind in range(4)]
        pr_buf, dzr_buf, dzc_buf = bufs[32:34], bufs[34:36], bufs[36:38]

        def rows(p):
            return pl.ds(pl.multiple_of(jnp.clip(p, 0, nq - 1) * tb, tb), tb)

        def split_heads(block):
            return jnp.concatenate([jnp.where(first, block, zero), jnp.where(first, zero, block)], axis=0)

        def trip(t, u, carry):
            first_trips = isinstance(t, int)
            w, r = u % 2, 1 - u % 2
            cs, cps, dq = list(carry[0:2]), list(carry[2:4]), carry[4]
            if not first_trips:
                live4 = t - 4 < i
                dq = dq + jnp.where(live4, _dot(dzc_buf[r][...], split_heads(k_scr[rows(t - 4), :])), 0.0)
                dkt_acc[jnp.clip(t - 4, 0, nq - 1)] += jnp.where(live4, _dot(qcat_t, dzr_buf[r][...]), 0.0)
            if not first_trips or t >= 3:
                dvt_acc[jnp.clip(t - 3, 0, nq - 1)] += _dot(docat_t, pr_buf[r][...])
            ktj = kt_scr[jnp.clip(t, 0, nq - 1)]
            for e in range(2):
                z_ring[u][e][...] = _dot(qs[e], ktj)
            if not first_trips or t >= 1:
                vtj = vt_scr[jnp.clip(t - 1, 0, nq - 1)]
                for e in range(2):
                    sp, sig = _softplus(z_ring[(u - 1) % 4][e][...], True)
                    sg_ring[(u - 1) % 4][e][...] = sig
                    in_buf[w][e][...] = _dot(sp.astype(BF16), upper)
                    da_buf[w][e][...] = _dot(dos[e], vtj)
            if not first_trips or t >= 2:
                for e in range(2):
                    incl = in_buf[r][e][...]
                    cs[e] = cs[e] + incl[:, 0:1]
                    off = jnp.where(t - 2 < i, tots[e] - cs[e], -NEG_BIG)
                    a = jnp.exp(z_ring[(u - 2) % 4][e][...] - incl - off)
                    dw = a * da_buf[r][e][...]
                    dw_buf[w][e][...] = dw
                    pr_buf[w][e * tb:(e + 1) * tb, :] = a.astype(BF16)
                    pre_buf[w][e][...] = _dot(dw.astype(BF16), lower)
            if not first_trips or t >= 3:
                for e in range(2):
                    pre = pre_buf[r][e][...] + cps[e]
                    dzb = (dw_buf[r][e][...] - sg_ring[(u - 3) % 4][e][...] * pre).astype(BF16)
                    cps[e] = pre[:, tb - 1:tb]
                    dzr_buf[w][e * tb:(e + 1) * tb, :] = dzb
                    dzc_buf[w][:, e * tb:(e + 1) * tb] = dzb
            return cs[0], cs[1], cps[0], cps[1], dq

        def four_trips(n, cr):
            for u in range(4):
                cr = trip(4 * n + u, u, cr)
            return cr

        col = jnp.zeros((tb, 1), F32)
        carry = four_trips(0, (col, col, col, col, jnp.zeros((tb, HEAD_PAIR), F32)))
        carry = lax.fori_loop(1, jnp.where(i > 0, (i + 7) // 4, 1), four_trips, carry)

        cps, dq = carry[2:4], carry[4]
        kj = k_scr[rows(i), :]
        dzs, probs = [], []
        for e in range(2):
            z = _dot(qs[e], kt_scr[i])
            sp, sig = _softplus(z, True)
            incl = _dot(jnp.where(causal, sp, 0.0).astype(BF16), upper)
            a = jnp.where(causal, jnp.exp(z - incl), 0.0)
            dw = a * _dot(dos[e], vt_scr[i])
            pre = _dot(dw.astype(BF16), lower) + cps[e]
            dzs.append(jnp.where(causal, dw - sig * pre, 0.0).astype(BF16))
            probs.append(a.astype(BF16))
        dq = dq + _dot(jnp.concatenate(dzs, axis=1), split_heads(kj))
        dkt_acc[i] += _dot(qcat_t, jnp.concatenate(dzs, axis=0))
        dvt_acc[i] += _dot(docat_t, jnp.concatenate(probs, axis=0))
        dq_ref[...] = (dq * Q_SCALE).astype(BF16)

        @pl.when(i == nq - 1)
        def _():
            for acc, dst in ((dkt_acc, dk_hbm), (dvt_acc, dv_hbm)):
                def flip(n, _, acc=acc):
                    at = pl.ds(pl.multiple_of(n * tb, tb), tb)
                    stage[at, :] = acc[n].T
                    return 0
                lax.fori_loop(0, nq, flip, 0)
                pltpu.sync_copy(stage, dst.at[:, lanes])

    blk = pl.BlockSpec((tb, HEAD_PAIR), lambda h, i: (i, h))
    grad = jax.ShapeDtypeStruct((s, D_ATTN), F32)
    return _call(
        body, name="attn_bwd",
        out_shape=[jax.ShapeDtypeStruct((s, D_ATTN), BF16), grad, grad],
        grid=(4, nq),
        in_specs=[blk, blk, pl.BlockSpec((None, tb, 2), lambda h, i: (h, i, 0)), ANY, ANY, ANY],
        out_specs=[blk, ANY, ANY],
        scratch_shapes=[pltpu.VMEM((s, HEAD_PAIR), BF16), pltpu.VMEM((nq, HEAD_PAIR, tb), BF16),
                        pltpu.VMEM((nq, HEAD_PAIR, tb), BF16),
                        pltpu.VMEM((nq, HEAD_PAIR, tb), F32), pltpu.VMEM((nq, HEAD_PAIR, tb), F32),
                        pltpu.VMEM((s, HEAD_PAIR), F32)]
        + [pltpu.VMEM((tb, tb), F32)] * 32
        + [pltpu.VMEM((2 * tb, tb), BF16)] * 4 + [pltpu.VMEM((tb, 2 * tb), BF16)] * 2,
        compiler_params=_params(("arbitrary", "arbitrary"), vmem_mb=60))(q, do, totals, k, kt, vt)


def _pool_bwd(u, dmp, w_pool, pool_scale):
    s = u.shape[0]
    ts = 512
    nr = s // ts
    per = ts // POOL_HALO

    def body(u_ref, halo_ref, dm_ref, wp_ref, ps_ref, du_ref, dwp_ref, dps_ref, ext_ref, y_ref, dext_ref):
        i = pl.program_id(0)
        rb = nr - 1 - i

        @pl.when(i == 0)
        def _():
            dext_ref[ts:, :] = jnp.zeros((POOL_HALO, D_POOL), F32)
            dwp_ref[...] = jnp.zeros_like(dwp_ref)
            dps_ref[...] = jnp.zeros_like(dps_ref)

        ext_ref[0:POOL_HALO, :] = jnp.where(rb > 0, halo_ref[...], 0.0)
        ext_ref[POOL_HALO:, :] = u_ref[...]
        ps, cnts = [], []
        for g, window in enumerate(POOL_WINDOWS):
            p, cnt = _pool_means(ext_ref, g, window, ts, rb * ts)
            ps.append(p.astype(BF16))
            cnts.append(cnt)
            y_ref[:, g * POOL_GROUP:(g + 1) * POOL_GROUP] = _dot(ps[g], wp_ref[g].astype(BF16))
        y = y_ref[...]
        r = _rstd(y)
        yn = y * r
        dm = dm_ref[...]
        dps_ref[...] += jnp.sum(dm * yn, axis=0, keepdims=True)
        dn = dm * ps_ref[...]
        dy = r * (dn - yn * jnp.mean(dn * yn, axis=-1, keepdims=True))
        for g, window in enumerate(POOL_WINDOWS):
            cols = slice(g * POOL_GROUP, (g + 1) * POOL_GROUP)
            dyg = dy[:, cols].astype(BF16)
            dwp_ref[g] += _dot_tn(ps[g], dyg)
            dp = _dot_nt(dyg, wp_ref[g].astype(BF16))
            dext_ref[0:ts, cols] = dp / cnts[g]
            acc = dext_ref[0:ts, cols]
            for d in range(1, window):
                acc = acc + dext_ref[d:d + ts, cols]
            du_ref[:, cols] = (acc - dp).astype(BF16)
        dext_ref[ts:, :] = dext_ref[0:POOL_HALO, :]

    rows = pl.BlockSpec((ts, D_POOL), lambda i: (nr - 1 - i, 0))
    return _call(
        body, name="pool_bwd",
        out_shape=[jax.ShapeDtypeStruct((s, D_POOL), BF16), jax.ShapeDtypeStruct(w_pool.shape, F32),
                   jax.ShapeDtypeStruct((1, D_POOL), F32)],
        grid=(nr,),
        in_specs=[rows,
                  pl.BlockSpec((POOL_HALO, D_POOL), lambda i: (jnp.maximum((nr - 1 - i) * per - 1, 0), 0)),
                  rows, _full(w_pool.shape), _full((1, D_POOL))],
        out_specs=[rows, _full(w_pool.shape), _full((1, D_POOL))],
        scratch_shapes=[pltpu.VMEM((ts + POOL_HALO, D_POOL), F32), pltpu.VMEM((ts, D_POOL), F32),
                        pltpu.VMEM((ts + POOL_HALO, D_POOL), F32)],
        compiler_params=_params(("arbitrary",)))(u, u, dmp, w_pool, pool_scale)


def _in_proj_bwd(du, dq, dk, dv, w_in, x, dx1, g1):
    s = x.shape[0]
    ts = 512

    def body(du_ref, dq_ref, dk_ref, dv_ref, w_ref, x_ref, dx1_ref, g_ref, gx_ref, dg_ref):
        @pl.when(pl.program_id(0) == 0)
        def _():
            dg_ref[...] = jnp.zeros_like(dg_ref)

        dh = _dot_nt(du_ref[...], w_ref[0]) + _dot_nt(dq_ref[...], w_ref[1])
        dh = dh + _dot_nt(dk_ref[...].astype(BF16), w_ref[2]) + _dot_nt(dv_ref[...].astype(BF16), w_ref[3])
        xv = x_ref[...]
        r = _rstd(xv)
        xn = xv * r
        dg_ref[...] += jnp.sum(dh * xn, axis=0, keepdims=True)
        dxn = dh * g_ref[...]
        gx_ref[...] = dx1_ref[...] + r * (dxn - xn * jnp.mean(dxn * xn, axis=-1, keepdims=True))

    row = lambda w: pl.BlockSpec((ts, w), lambda i: (i, 0))
    return _call(
        body, name="in_proj_bwd",
        out_shape=[jax.ShapeDtypeStruct((s, D_MODEL), F32), jax.ShapeDtypeStruct((1, D_MODEL), F32)],
        grid=(s // ts,),
        in_specs=[row(D_POOL)] * 4 + [_full(w_in.shape), row(D_MODEL), row(D_MODEL), _full((1, D_MODEL))],
        out_specs=[row(D_MODEL), _full((1, D_MODEL))],
        compiler_params=_params(("arbitrary",)))(du, dq, dk, dv, w_in, x, dx1, g1)


_SMALL = ("norm_mix_pre", "w_pool", "pool_scale", "attn_scale", "norm_mix_post",
          "norm_ffn_pre", "conv_b", "norm_ffn_post")
_SMALL_SIZE = {"norm_mix_pre": 1024, "w_pool": 65536, "pool_scale": 512, "attn_scale": 512,
               "norm_mix_post": 1024, "norm_ffn_pre": 1024, "conv_b": 5632, "norm_ffn_post": 1024}
_SMALL_ROWS = 600
_CONVW_ROWS = 132
_PACK_ROWS = _SMALL_ROWS + _CONVW_ROWS + 4


def _pack_small(parts):
    flat = jnp.concatenate([parts[n].reshape(-1) for n in _SMALL])
    flat = jnp.pad(flat, (0, _SMALL_ROWS * 128 - flat.shape[0]))
    return flat.reshape(_SMALL_ROWS, 128)


def _unpack_small(packed, like):
    flat = packed.reshape(-1)
    out, off = {}, 0
    for n in _SMALL:
        out[n] = flat[off:off + _SMALL_SIZE[n]].reshape(like[n].shape)
        off += _SMALL_SIZE[n]
    return out


def kernel(x, norm_mix_pre, w_in, w_pool, pool_scale, attn_scale, w_out, norm_mix_post, norm_ffn_pre, w_up, conv_w, conv_b, w_down, norm_ffn_post, loss_target, m_norm_mix_pre, m_w_in, m_w_pool, m_pool_scale, m_attn_scale, m_w_out, m_norm_mix_post, m_norm_ffn_pre, m_w_up, m_conv_w, m_conv_b, m_w_down, m_norm_ffn_post, v_norm_mix_pre, v_w_in, v_w_pool, v_pool_scale, v_attn_scale, v_w_out, v_norm_mix_post, v_norm_ffn_pre, v_w_up, v_conv_w, v_conv_b, v_w_down, v_norm_ffn_post):
    weights = dict(norm_mix_pre=norm_mix_pre, w_in=w_in, w_pool=w_pool, pool_scale=pool_scale,
                   attn_scale=attn_scale, w_out=w_out, norm_mix_post=norm_mix_post,
                   norm_ffn_pre=norm_ffn_pre, w_up=w_up, conv_w=conv_w, conv_b=conv_b,
                   w_down=w_down, norm_ffn_post=norm_ffn_post)
    mom1 = dict(norm_mix_pre=m_norm_mix_pre, w_in=m_w_in, w_pool=m_w_pool, pool_scale=m_pool_scale,
                attn_scale=m_attn_scale, w_out=m_w_out, norm_mix_post=m_norm_mix_post,
                norm_ffn_pre=m_norm_ffn_pre, w_up=m_w_up, conv_w=m_conv_w, conv_b=m_conv_b,
                w_down=m_w_down, norm_ffn_post=m_norm_ffn_post)
    mom2 = dict(norm_mix_pre=v_norm_mix_pre, w_in=v_w_in, w_pool=v_w_pool, pool_scale=v_pool_scale,
                attn_scale=v_attn_scale, w_out=v_w_out, norm_mix_post=v_norm_mix_post,
                norm_ffn_pre=v_norm_ffn_pre, w_up=v_w_up, conv_w=v_conv_w, conv_b=v_conv_b,
                w_down=v_w_down, norm_ffn_post=v_norm_ffn_post)
    order = list(weights)

    xs = x[0]
    target = loss_target[0]
    wp = w_pool[0]
    shard = lax.axis_index("x") * 2 + lax.axis_index("y")

    slot = shard.astype(jnp.int32).reshape(1)
    win_g, = _gather_shards([_cast_bf16(w_in[0], "cast_w_in")])
    lands = [_into_slot(w_out[0], slot, BF16, "cast_w_out"), _into_slot(w_up[0], slot, BF16, "cast_w_up"),
             _into_slot(w_down[0], slot, BF16, "cast_w_down"), _into_slot(conv_w[0], slot, F32, "place_conv_w")]
    g_send, g_recv, g_lands, g_token = _exchange_start(None, lands, win_g, "gather_start")
    convb_g = conv_b[0].reshape(N_SHARD, 1, FF_TILE)

    u, q, k, v, kt, vt, h1 = _in_proj(xs, norm_mix_pre + g_token[0:1, 0:1], win_g)
    mpool = _pool_fwd(u, wp, pool_scale)
    attn, totals = _attn_fwd(q, k, vt)
    _, (wout_g, wup_g, wdown_g, convw_g) = _exchange_wait(g_send, g_recv, g_lands, False, attn, "gather_wait")
    wout_f = wout_g.reshape(D_MODEL, D_MODEL)
    wdown_f = wdown_g.reshape(D_FF, D_MODEL)
    mattn, mix, x1, h2 = _mix_out(attn, mpool, xs, attn_scale, wout_f, norm_mix_post, norm_ffn_pre)
    upre_g, upre_v, f_in = _ffn_up(h2, wup_g, convw_g, convb_g)
    df, dy, loss_tile, d_post = _ffn_down(f_in, wdown_f, x1, target, norm_ffn_post)

    d_wdown = _tn_matmul(f_in, df[None], "dw_down")
    dug, duv, dcw_g, dcw_v, dcb_g, dcb_v = _ffn_bwd_act(df, wdown_f, upre_g, upre_v, convw_g, convb_g)
    d_wup = jnp.concatenate([_tn_matmul(h2[None], dug, "dw_up_gate")[0],
                             _tn_matmul(h2[None], duv, "dw_up_value")[0]], axis=0)
    early = [d_wup, d_wdown.reshape(N_SHARD, D_FF // N_SHARD, D_MODEL)]
    s_send, s_recv, s_thru, s_token = _exchange_start(
        early, [lax.empty((3,) + g.shape[1:], g.dtype) for g in early], d_wup, "scatter_start")
    dx1, dmix, d_ffn_pre, d_mix_post = _ffn_bwd_in(
        dug, duv, wup_g, x1, dy, mix, norm_ffn_pre + s_token[0:1, 0:1], norm_mix_post)
    d_wout = jnp.concatenate([_tn_matmul(mpool[None], dmix[None], "dw_out_pool")[0, 0],
                              _tn_matmul(mattn[None], dmix[None], "dw_out_attn")[0, 0]], axis=0)
    dmp, do, d_attn_scale = _mix_bwd(dmix, wout_f, attn, attn_scale)
    dq, dk, dv = _attn_bwd(q, k, kt, vt, do, totals)
    du, d_wpool, d_pool_scale = _pool_bwd(u, dmp, wp, pool_scale)
    d_win = jnp.stack([_tn_matmul(h1[None], t[None], "dw_in_%d" % n)[0, 0]
                       for n, t in enumerate((du, dq, dk, dv))])
    grad_x, d_mix_pre = _in_proj_bwd(du, dq, dk, dv, win_g, xs, dx1, norm_mix_pre)

    d_convw = jnp.concatenate([dcw_g, dcw_v], axis=0)
    d_convb = jnp.concatenate([dcb_g, dcb_v], axis=0).reshape(1, 2 * D_FF)
    small_parts = dict(norm_mix_pre=d_mix_pre, w_pool=d_wpool, pool_scale=d_pool_scale,
                       attn_scale=d_attn_scale, norm_mix_post=d_mix_post, norm_ffn_pre=d_ffn_pre,
                       conv_b=d_convb, norm_ffn_post=d_post)
    packed = jnp.concatenate([_pack_small(small_parts), d_convw.reshape(_CONVW_ROWS, 128),
                              loss_tile[0:4]], axis=0)
    late, gathered = _scatter_grads([d_win, d_wout.reshape(N_SHARD, D_MODEL // N_SHARD, D_MODEL)], packed)
    early_srcs, early_lands = _exchange_wait(s_send, s_recv, s_thru, True, grad_x, "scatter_wait")
    quarter = [_sum_slots(r, (3, 0, 1, 2), "sum_chips_%d" % n) for n, r in enumerate(late)]
    quarter += [_sum_own_and_received(early_srcs[n], slot, early_lands[n], "sum_chips_%d" % (n + 2))
                for n in range(2)]
    sibling = _swap_with_sibling(quarter)
    small_sum = _sum_slots(gathered, tuple(range(8)), "sum_small")

    results = {}
    for n, name in enumerate(("w_in", "w_out", "w_up", "w_down")):
        res = _adamw([quarter[n], sibling[n]], weights[name][0], mom1[name][0], mom2[name][0],
                     "adamw_" + name)
        results[name] = [t[None] for t in res]
    g_convw = lax.dynamic_slice_in_dim(
        small_sum[_SMALL_ROWS:_SMALL_ROWS + _CONVW_ROWS].reshape(N_SHARD, 3, FF_TILE), shard, 1, axis=0)[0]
    convw_pad = lambda t: jnp.pad(t, ((0, 5), (0, 0)))
    res = _adamw([convw_pad(g_convw)], convw_pad(conv_w[0]), convw_pad(m_conv_w[0]),
                 convw_pad(v_conv_w[0]), "adamw_conv_w")
    results["conv_w"] = [t[:3][None] for t in res]
    pack_w = _pack_small(weights)
    pack_m = _pack_small(mom1)
    pack_v = _pack_small(mom2)
    res = _adamw([small_sum[:_SMALL_ROWS]], pack_w, pack_m, pack_v, "adamw_small")
    unpacked = [_unpack_small(t, weights) for t in res]
    for name in _SMALL:
        results[name] = [t[name] for t in unpacked]

    loss = small_sum[_SMALL_ROWS + _CONVW_ROWS, 0]
    outs = [loss, grad_x[None]]
    for slot in range(4):
        outs.extend(results[name][slot] for name in order)
    return tuple(outs)
```

```python
import functools

import jax
import jax.numpy as jnp
from jax import lax
from jax.experimental import pallas as pl
from jax.experimental.pallas import tpu as pltpu

F32 = jnp.float32
BF16 = jnp.bfloat16

D_MODEL = 1024
D_POOL = 512
D_ATTN = 512
POOL_WINDOWS = (2, 4, 8, 16)
POOL_GROUP = 128
POOL_HALO = 16
CONV_HALO = 8
D_FF = 2816
FF_TILE = 1408
N_SHARD = 4
EPS = 1e-6
Q_SCALE = 0.125
ATT_BLOCK = 256
HEAD_PAIR = 128
MIB = 1 << 20
NEG_BIG = -1e30

ADAM_LR = 0.001
ADAM_B1 = 0.9
ADAM_B2 = 0.999
ADAM_EPS = 1e-08
ADAM_WD = 0.01
ADAM_STEP = 10

NT_DIMS = (((1,), (1,)), ((), ()))
TN_DIMS = (((0,), (0,)), ((), ()))
MESH = pl.DeviceIdType.MESH
ANY = pl.BlockSpec(memory_space=pl.ANY)
HBM_SPEC = pl.BlockSpec(memory_space=pltpu.HBM)
SEM_SPEC = pl.BlockSpec(memory_space=pltpu.SEMAPHORE)
DATAFLOW = pltpu.SideEffectType.DATAFLOW_SIDE_EFFECTING


def _call(body, **kw):
    return pl.pallas_call(body, **kw)


def _params(sem=None, vmem_mb=48):
    return pltpu.CompilerParams(dimension_semantics=sem, vmem_limit_bytes=vmem_mb * MIB)


def _rstd(v):
    return lax.rsqrt(jnp.mean(v * v, axis=-1, keepdims=True) + EPS)


def _dot(a, b):
    return jnp.dot(a, b, preferred_element_type=F32)


def _dot_nt(a, b):
    return lax.dot_general(a, b, NT_DIMS, preferred_element_type=F32)


def _dot_tn(a, b):
    return lax.dot_general(a, b, TN_DIMS, preferred_element_type=F32)


def _row_tile(rows, cap):
    t = min(rows, cap)
    t -= t % 8
    while rows % t:
        t -= 8
    return t


def _full(shape):
    nd = len(shape)
    return pl.BlockSpec(shape, lambda *_: (0,) * nd)


def _chip_peers():
    x, y, c = lax.axis_index("x"), lax.axis_index("y"), lax.axis_index("c")
    return x, y, c, [(1 - x, y), (x, 1 - y), (1 - x, 1 - y)]


def _cast_bf16(a, name):
    def body(a_ref, o_ref):
        o_ref[...] = a_ref[...].astype(BF16)

    return _call(body, name=name, out_shape=jax.ShapeDtypeStruct(a.shape, BF16),
                 grid=(1,), in_specs=[_full(a.shape)], out_specs=_full(a.shape),
                 compiler_params=_params(("arbitrary",)))(a)


def _into_slot(a, slot, dtype, name):
    nd = a.ndim

    def body(slot_ref, a_ref, o_ref):
        o_ref[...] = a_ref[...].astype(dtype)

    return _call(
        body, name=name, out_shape=jax.ShapeDtypeStruct((N_SHARD,) + a.shape, dtype),
        grid_spec=pltpu.PrefetchScalarGridSpec(
            num_scalar_prefetch=1, grid=(1,),
            in_specs=[pl.BlockSpec(a.shape, lambda i, slot_ref: (0,) * nd)],
            out_specs=pl.BlockSpec((None,) + a.shape, lambda i, slot_ref: (slot_ref[0],) + (0,) * nd)),
        compiler_params=_params(("arbitrary",)))(slot, a)


def _exchange_copies(srcs, lands, send, recv):
    x, y, c, chips = _chip_peers()
    copies = []
    for t in range(len(lands)):
        for k, (px, py) in enumerate(chips):
            copies.append(pltpu.make_async_remote_copy(
                src_ref=lands[t].at[2 * x + y] if srcs is None else srcs[t].at[2 * px + py],
                dst_ref=lands[t].at[2 * x + y] if srcs is None else lands[t].at[k],
                send_sem=send.at[3 * t + k], recv_sem=recv.at[3 * t + k],
                device_id=(px, py, c), device_id_type=MESH))
    return copies


def _exchange_start(srcs, lands, after, name):
    n = len(lands)
    operands = list(lands) if srcs is None else list(srcs) + list(lands)
    m = len(operands)

    def body(*refs):
        for cp in _exchange_copies(None if srcs is None else refs[:n], refs[m - n:m], refs[m + 1], refs[m + 2]):
            cp.start()
        refs[-1][...] = jnp.zeros_like(refs[-1])

    res = _call(
        body, name=name,
        out_shape=[pltpu.SemaphoreType.DMA((3 * n,)), pltpu.SemaphoreType.DMA((3 * n,))]
        + [pltpu.HBM(a.shape, a.dtype) for a in operands] + [jax.ShapeDtypeStruct((8, 128), F32)],
        in_specs=[HBM_SPEC] * m + [ANY],
        out_specs=[SEM_SPEC, SEM_SPEC] + [HBM_SPEC] * m + [pl.BlockSpec(memory_space=pltpu.VMEM)],
        input_output_aliases={j: j + 2 for j in range(m)},
        compiler_params=pltpu.CompilerParams(has_side_effects=DATAFLOW),
    )(*[pltpu.with_memory_space_constraint(a, pltpu.HBM) for a in operands], after)
    return res[0], res[1], res[2:2 + m], res[-1]


def _exchange_wait(send, recv, operands, scatter, after, name):
    m = len(operands)
    n = m // 2 if scatter else m

    def body(*refs):
        for cp in _exchange_copies(refs[:n] if scatter else None, refs[m - n:m], refs[m], refs[m + 1]):
            cp.wait_send()
            cp.wait_recv()

    res = _call(
        body, name=name, out_shape=[pltpu.HBM(a.shape, a.dtype) for a in operands],
        in_specs=[HBM_SPEC] * m + [SEM_SPEC, SEM_SPEC, ANY], out_specs=[HBM_SPEC] * m,
        input_output_aliases={j: j for j in range(m)},
        compiler_params=pltpu.CompilerParams(has_side_effects=DATAFLOW),
    )(*operands, send, recv, after)
    return res[:m - n], res[m - n:]


def _gather_shards(shards):
    n = len(shards)

    def body(*refs):
        ins, outs = refs[:n], refs[n:2 * n]
        send, recv, loc = refs[2 * n:]
        x, y, c, chips = _chip_peers()
        b = 2 * x + y
        local = [pltpu.make_async_copy(ins[t], outs[t].at[b], loc.at[t]) for t in range(n)]
        for cp in local:
            cp.start()
        remote = []
        for t in range(n):
            for k, (px, py) in enumerate(chips):
                remote.append(pltpu.make_async_remote_copy(
                    src_ref=ins[t], dst_ref=outs[t].at[b],
                    send_sem=send.at[3 * t + k], recv_sem=recv.at[3 * t + k],
                    device_id=(px, py, c), device_id_type=MESH))
        for cp in remote:
            cp.start()
        for cp in remote:
            cp.wait()
        for cp in local:
            cp.wait()

    return _call(
        body, name="gather_w_in",
        out_shape=[jax.ShapeDtypeStruct((N_SHARD,) + s.shape, s.dtype) for s in shards],
        in_specs=[ANY] * n, out_specs=[ANY] * n,
        scratch_shapes=[pltpu.SemaphoreType.DMA((3 * n,)), pltpu.SemaphoreType.DMA((3 * n,)),
                        pltpu.SemaphoreType.DMA((n,))],
    )(*shards)


def _scatter_grads(grads, small):
    n = len(grads)

    def body(*refs):
        ins, small_in = refs[:n], refs[n]
        outs, small_out = refs[n + 1:2 * n + 1], refs[2 * n + 1]
        send, recv, loc, ssend, srecv = refs[2 * n + 2:]
        x, y, c, chips = _chip_peers()
        b = 2 * x + y
        me = 4 * x + 2 * y + c
        local = [pltpu.make_async_copy(ins[t].at[b], outs[t].at[3], loc.at[t]) for t in range(n)]
        local.append(pltpu.make_async_copy(small_in, small_out.at[me], loc.at[n]))
        for cp in local:
            cp.start()
        remote = []
        for t in range(n):
            for k, (px, py) in enumerate(chips):
                remote.append(pltpu.make_async_remote_copy(
                    src_ref=ins[t].at[2 * px + py], dst_ref=outs[t].at[k],
                    send_sem=send.at[3 * t + k], recv_sem=recv.at[3 * t + k],
                    device_id=(px, py, c), device_id_type=MESH))
        for r in range(1, 8):
            px = 1 - x if r & 4 else x
            py = 1 - y if r & 2 else y
            pc = 1 - c if r & 1 else c
            remote.append(pltpu.make_async_remote_copy(
                src_ref=small_in, dst_ref=small_out.at[me],
                send_sem=ssend.at[r - 1], recv_sem=srecv.at[r - 1],
                device_id=(px, py, pc), device_id_type=MESH))
        for cp in remote:
            cp.start()
        for cp in remote:
            cp.wait()
        for cp in local:
            cp.wait()

    out_shape = [jax.ShapeDtypeStruct(g.shape, g.dtype) for g in grads]
    out_shape.append(jax.ShapeDtypeStruct((8,) + small.shape, small.dtype))
    res = _call(
        body, name="scatter_grads", out_shape=out_shape,
        in_specs=[ANY] * (n + 1), out_specs=[ANY] * (n + 1),
        scratch_shapes=[pltpu.SemaphoreType.DMA((3 * n,)), pltpu.SemaphoreType.DMA((3 * n,)),
                        pltpu.SemaphoreType.DMA((n + 1,)),
                        pltpu.SemaphoreType.DMA((7,)), pltpu.SemaphoreType.DMA((7,))],
    )(*grads, small)
    return res[:n], res[n]


def _swap_with_sibling(parts):
    n = len(parts)

    def body(*refs):
        ins, outs = refs[:n], refs[n:2 * n]
        send, recv = refs[2 * n:]
        x, y, c = lax.axis_index("x"), lax.axis_index("y"), lax.axis_index("c")
        copies = [pltpu.make_async_remote_copy(
            src_ref=ins[t], dst_ref=outs[t], send_sem=send.at[t], recv_sem=recv.at[t],
            device_id=(x, y, 1 - c), device_id_type=MESH) for t in range(n)]
        for cp in copies:
            cp.start()
        for cp in copies:
            cp.wait()

    return _call(
        body, name="swap_sibling",
        out_shape=[jax.ShapeDtypeStruct(p.shape, p.dtype) for p in parts],
        in_specs=[ANY] * n, out_specs=[ANY] * n,
        scratch_shapes=[pltpu.SemaphoreType.DMA((n,)), pltpu.SemaphoreType.DMA((n,))],
    )(*parts)


def _sum_slots(buf, order, name):
    k, rows, cols = buf.shape
    tr = _row_tile(rows, 256)

    def body(b_ref, o_ref):
        acc = b_ref[order[0]].astype(F32)
        for s in order[1:]:
            acc = acc + b_ref[s].astype(F32)
        o_ref[...] = acc

    return _call(body, name=name, out_shape=jax.ShapeDtypeStruct((rows, cols), F32),
                 grid=(rows // tr,),
                 in_specs=[pl.BlockSpec((k, tr, cols), lambda i: (0, i, 0))],
                 out_specs=pl.BlockSpec((tr, cols), lambda i: (i, 0)),
                 compiler_params=_params(("parallel",)))(buf)


def _sum_own_and_received(src, slot, land, name):
    _, rows, cols = src.shape
    tr = _row_tile(rows, 256)

    def body(slot_ref, s_ref, l_ref, o_ref):
        acc = s_ref[...].astype(F32)
        for k in range(3):
            acc = acc + l_ref[k].astype(F32)
        o_ref[...] = acc

    return _call(
        body, name=name, out_shape=jax.ShapeDtypeStruct((rows, cols), F32),
        grid_spec=pltpu.PrefetchScalarGridSpec(
            num_scalar_prefetch=1, grid=(rows // tr,),
            in_specs=[pl.BlockSpec((None, tr, cols), lambda i, slot_ref: (slot_ref[0], i, 0)),
                      pl.BlockSpec((3, tr, cols), lambda i, slot_ref: (0, i, 0))],
            out_specs=pl.BlockSpec((tr, cols), lambda i, slot_ref: (i, 0))),
        compiler_params=_params(("parallel",)))(slot, src, land)


def _adamw(grad_parts, w, m, v, name):
    rows, cols = w.shape
    tr = _row_tile(rows, 256)
    npart = len(grad_parts)

    def body(*refs):
        gp = refs[:npart]
        w_ref, m_ref, v_ref, g_out, d_out, m_out, v_out = refs[npart:]
        g = gp[0][...]
        for p in gp[1:]:
            g = g + p[...]
        mm = ADAM_B1 * m_ref[...] + (1.0 - ADAM_B1) * g
        vv = ADAM_B2 * v_ref[...] + (1.0 - ADAM_B2) * jnp.square(g)
        m_hat = mm / (1.0 - ADAM_B1 ** ADAM_STEP)
        v_hat = vv / (1.0 - ADAM_B2 ** ADAM_STEP)
        g_out[...] = g
        d_out[...] = -ADAM_LR * (m_hat / (jnp.sqrt(v_hat) + ADAM_EPS) + ADAM_WD * w_ref[...])
        m_out[...] = mm
        v_out[...] = vv

    spec = pl.BlockSpec((tr, cols), lambda i: (i, 0))
    shp = jax.ShapeDtypeStruct((rows, cols), F32)
    return _call(body, name=name, out_shape=[shp] * 4, grid=(rows // tr,),
                 in_specs=[spec] * (npart + 3), out_specs=[spec] * 4,
                 compiler_params=_params(("parallel",)))(*grad_parts, w, m, v)


def _in_proj(x, g1, w_in):
    s = x.shape[0]
    ts = 512

    def body(x_ref, g_ref, w_ref, u_ref, q_ref, k_ref, kt_ref, vt_ref, h_ref):
        xv = x_ref[...]
        h = (xv * _rstd(xv) * g_ref[...]).astype(BF16)
        h_ref[...] = h
        u_ref[...] = _dot(h, w_ref[0])
        q_ref[...] = (_dot(h, w_ref[1]) * Q_SCALE).astype(BF16)
        k = _dot(h, w_ref[2])
        k_ref[...] = k.astype(BF16)
        v = _dot(h, w_ref[3])
        for src, dst in ((k, kt_ref), (v, vt_ref)):
            src_t = src.T.astype(BF16)
            for n in range(ts // ATT_BLOCK):
                dst[n] = src_t[:, n * ATT_BLOCK:(n + 1) * ATT_BLOCK]

    row = lambda w: pl.BlockSpec((ts, w), lambda i: (i, 0))
    half = jax.ShapeDtypeStruct((s, D_POOL), BF16)
    return _call(
        body, name="in_proj",
        out_shape=[jax.ShapeDtypeStruct((s, D_POOL), F32), half, half,
                   jax.ShapeDtypeStruct((s // ATT_BLOCK, D_ATTN, ATT_BLOCK), BF16),
                   jax.ShapeDtypeStruct((s // ATT_BLOCK, D_ATTN, ATT_BLOCK), BF16),
                   jax.ShapeDtypeStruct((s, D_MODEL), BF16)],
        grid=(s // ts,),
        in_specs=[row(D_MODEL), _full((1, D_MODEL)), _full(w_in.shape)],
        out_specs=[row(D_POOL)] * 3
        + [pl.BlockSpec((ts // ATT_BLOCK, D_ATTN, ATT_BLOCK), lambda i: (i, 0, 0))] * 2 + [row(D_MODEL)],
        compiler_params=_params(("parallel",)))(x, g1, w_in)


def _pool_means(ext_ref, g, window, ts, row0):
    cols = slice(g * POOL_GROUP, (g + 1) * POOL_GROUP)
    cur = ext_ref[POOL_HALO:POOL_HALO + ts, cols]
    acc = cur
    for d in range(1, window):
        acc = acc + ext_ref[POOL_HALO - d:POOL_HALO - d + ts, cols]
    t1 = row0 + 1 + lax.broadcasted_iota(jnp.int32, (ts, 1), 0)
    cnt = jnp.minimum(t1, window).astype(F32)
    return acc / cnt - cur, cnt


def _pool_fwd(u, w_pool, pool_scale):
    s = u.shape[0]
    ts = 512
    per = ts // POOL_HALO

    def body(u_ref, halo_ref, wp_ref, ps_ref, o_ref, ext_ref, y_ref):
        i = pl.program_id(0)
        ext_ref[0:POOL_HALO, :] = jnp.where(i > 0, halo_ref[...], 0.0)
        ext_ref[POOL_HALO:, :] = u_ref[...]
        for g, window in enumerate(POOL_WINDOWS):
            p, _ = _pool_means(ext_ref, g, window, ts, i * ts)
            y_ref[:, g * POOL_GROUP:(g + 1) * POOL_GROUP] = _dot(
                p.astype(BF16), wp_ref[g].astype(BF16))
        y = y_ref[...]
        o_ref[...] = (y * _rstd(y) * ps_ref[...]).astype(BF16)

    return _call(
        body, name="pool_fwd", out_shape=jax.ShapeDtypeStruct((s, D_POOL), BF16),
        grid=(s // ts,),
        in_specs=[pl.BlockSpec((ts, D_POOL), lambda i: (i, 0)),
                  pl.BlockSpec((POOL_HALO, D_POOL), lambda i: (jnp.maximum(i * per - 1, 0), 0)),
                  _full(w_pool.shape), _full((1, D_POOL))],
        out_specs=pl.BlockSpec((ts, D_POOL), lambda i: (i, 0)),
        scratch_shapes=[pltpu.VMEM((ts + POOL_HALO, D_POOL), F32), pltpu.VMEM((ts, D_POOL), F32)],
        compiler_params=_params(("parallel",)))(u, u, w_pool, pool_scale)


def _tri(kind):
    r = lax.broadcasted_iota(jnp.int32, (ATT_BLOCK, ATT_BLOCK), 0)
    c = lax.broadcasted_iota(jnp.int32, (ATT_BLOCK, ATT_BLOCK), 1)
    return jnp.where(r >= c if kind == "suffix" else r <= c, 1.0, 0.0).astype(BF16)


def _causal_mask():
    r = lax.broadcasted_iota(jnp.int32, (ATT_BLOCK, ATT_BLOCK), 0)
    c = lax.broadcasted_iota(jnp.int32, (ATT_BLOCK, ATT_BLOCK), 1)
    return c < r


def _softplus(z, with_sigmoid=False):
    ope = 1.0 + jnp.exp(jnp.minimum(z, 80.0))
    sp = jnp.maximum(z, jnp.log(ope))
    if with_sigmoid:
        return sp, 1.0 - 1.0 / ope
    return sp


def _attn_fwd(q, k, vt):
    s = q.shape[0]
    tb = ATT_BLOCK
    nq = s // tb

    def body(q_ref, k_ref, vt_ref, o_ref, t_ref, *bufs):
        i = pl.program_id(1)
        suffix = _tri("prefix")
        r_idx = lax.broadcasted_iota(jnp.int32, (tb, tb), 0)
        c_idx = lax.broadcasted_iota(jnp.int32, (tb, tb), 1)
        causal = r_idx < c_idx
        lane = lax.broadcasted_iota(jnp.int32, (1, HEAD_PAIR), 1)
        first = lane < 64
        top = lax.broadcasted_iota(jnp.int32, (HEAD_PAIR, 1), 0) < 64
        q2 = q_ref[...]
        zero = jnp.zeros_like(q2)
        qs_t = tuple(jnp.where(first, q2, zero).astype(F32).T.astype(BF16) if e == 0 else
                     jnp.where(first, zero, q2).astype(F32).T.astype(BF16) for e in range(2))

        def values_t(j):
            vt = vt_ref[j]
            none = jnp.zeros_like(vt)
            return jnp.concatenate([jnp.where(top, vt, none), jnp.where(top, none, vt)], axis=1)

        z_ring = [[bufs[2 * slot + e] for e in range(2)] for slot in range(4)]
        in_buf = [[bufs[8 + 2 * slot + e] for e in range(2)] for slot in range(2)]
        pr_buf = [bufs[12], bufs[13]]

        def block_of(p):
            return jnp.clip(i - p, 0, nq - 1)

        def trip(t, u, carry):
            first_trips = isinstance(t, int)
            w, r = u % 2, 1 - u % 2
            cs, o = list(carry[0:2]), carry[2]
            if not first_trips or t >= 3:
                o = o + jnp.where(t - 3 <= i, _dot(values_t(block_of(t - 3)), pr_buf[r][...]), 0.0)
            kj = k_ref[pl.ds(pl.multiple_of(block_of(t) * tb, tb), tb), :]
            for e in range(2):
                z_ring[u][e][...] = _dot(kj, qs_t[e])
            if not first_trips or t >= 1:
                for e in range(2):
                    sp = _softplus(z_ring[(u - 1) % 4][e][...])
                    if first_trips and t == 1:
                        sp = jnp.where(causal, sp, 0.0)
                    in_buf[w][e][...] = _dot(suffix, sp.astype(BF16))
            if not first_trips or t >= 2:
                live2 = t - 2 <= i
                for e in range(2):
                    incl = in_buf[r][e][...]
                    arg = z_ring[(u - 2) % 4][e][...] - incl - jnp.where(live2, cs[e], -NEG_BIG)
                    if first_trips and t == 2:
                        arg = jnp.where(causal, arg, NEG_BIG)
                    pr_buf[w][e * tb:(e + 1) * tb, :] = jnp.exp(arg).astype(BF16)
                    cs[e] = jnp.where(live2, cs[e] + incl[0:1, :], cs[e])
            return cs[0], cs[1], o

        def four_trips(n, cr):
            for u in range(4):
                cr = trip(4 * n + u, u, cr)
            return cr

        row = jnp.zeros((1, tb), F32)
        carry = four_trips(0, (row, row, jnp.zeros((HEAD_PAIR, tb), F32)))
        carry = lax.fori_loop(1, (i + 7) // 4, four_trips, carry)
        o_ref[...] = carry[2].T
        totals = jnp.where(r_idx == 0, carry[0], jnp.where(r_idx == 1, carry[1], 0.0))
        t_ref[...] = totals.T[:, 0:2]

    score_buf = pltpu.VMEM((tb, tb), F32)
    return _call(
        body, name="attn_fwd",
        out_shape=[jax.ShapeDtypeStruct((s, D_ATTN), F32),
                   jax.ShapeDtypeStruct((4, s, 2), F32)],
        grid=(4, nq),
        in_specs=[pl.BlockSpec((tb, HEAD_PAIR), lambda h, i: (i, h)),
                  pl.BlockSpec((s, HEAD_PAIR), lambda h, i: (0, h)),
                  pl.BlockSpec((nq, HEAD_PAIR, tb), lambda h, i: (0, h, 0))],
        out_specs=[pl.BlockSpec((tb, HEAD_PAIR), lambda h, i: (i, h)),
                   pl.BlockSpec((None, tb, 2), lambda h, i: (h, i, 0))],
        scratch_shapes=[score_buf] * 12 + [pltpu.VMEM((2 * tb, tb), BF16)] * 2,
        compiler_params=_params(("arbitrary", "arbitrary")))(q, k, vt)


def _mix_out(attn, mpool, x, attn_scale, w_out, g2, g3):
    s = x.shape[0]
    ts = 512

    def body(a_ref, p_ref, x_ref, as_ref, w_ref, g2_ref, g3_ref, ma_ref, mix_ref, x1_ref, h2_ref):
        ao = a_ref[...]
        ma = (ao * _rstd(ao) * as_ref[...]).astype(BF16)
        ma_ref[...] = ma
        mix = _dot(p_ref[...], w_ref[0:D_POOL, :]) + _dot(ma, w_ref[D_POOL:, :])
        mix_ref[...] = mix
        x1 = x_ref[...] + mix * _rstd(mix) * g2_ref[...]
        x1_ref[...] = x1
        h2_ref[...] = (x1 * _rstd(x1) * g3_ref[...]).astype(BF16)

    row = lambda w: pl.BlockSpec((ts, w), lambda i: (i, 0))
    return _call(
        body, name="mix_out",
        out_shape=[jax.ShapeDtypeStruct((s, D_ATTN), BF16), jax.ShapeDtypeStruct((s, D_MODEL), F32),
                   jax.ShapeDtypeStruct((s, D_MODEL), F32), jax.ShapeDtypeStruct((s, D_MODEL), BF16)],
        grid=(s // ts,),
        in_specs=[row(D_ATTN), row(D_POOL), row(D_MODEL), _full((1, D_ATTN)),
                  _full((D_MODEL, D_MODEL)), _full((1, D_MODEL)), _full((1, D_MODEL))],
        out_specs=[row(D_ATTN), row(D_MODEL), row(D_MODEL), row(D_MODEL)],
        compiler_params=_params(("parallel",)))(attn, mpool, x, attn_scale, w_out, g2, g3)


def _earlier_rows(halo, block):
    ts = block.shape[0]
    ext = jnp.concatenate([halo, block], axis=0)
    return tuple(pltpu.roll(ext, d, axis=0)[CONV_HALO:CONV_HALO + ts, :] for d in (1, 2))


def _later_rows(block, halo):
    ts = block.shape[0]
    ext = jnp.concatenate([block, halo], axis=0)
    return tuple(pltpu.roll(ext, ts + CONV_HALO - d, axis=0)[0:ts, :] for d in (1, 2))


def _conv_rows(x0, x1, x2, cw, cb):
    y = cb + cw[0:1, :] * x2
    y = y + cw[1:2, :] * x1
    return y + cw[2:3, :] * x0


def _sigmoid(v):
    return 1.0 / (1.0 + jnp.exp(-v))


def _ffn_up(h2, w_up, conv_w, conv_b):
    s = h2.shape[0]
    ts = 256
    tn = FF_TILE

    def body(h_ref, wg_ref, wv_ref, cwg_ref, cwv_ref, cbg_ref, cbv_ref,
             ug_ref, uv_ref, f_ref, tailg, tailv):
        i = pl.program_id(1)

        @pl.when(i == 0)
        def _():
            tailg[...] = jnp.zeros_like(tailg)
            tailv[...] = jnp.zeros_like(tailv)

        h = h_ref[...]
        ug = _dot(h, wg_ref[...])
        uv = _dot(h, wv_ref[...])
        ug_ref[...] = ug
        uv_ref[...] = uv
        gate = _conv_rows(ug, *_earlier_rows(tailg[...], ug), cwg_ref[...], cbg_ref[...])
        val = _conv_rows(uv, *_earlier_rows(tailv[...], uv), cwv_ref[...], cbv_ref[...])
        f_ref[...] = (gate * _sigmoid(gate) * val).astype(BF16)
        tailg[...] = ug[ts - CONV_HALO:, :]
        tailv[...] = uv[ts - CONV_HALO:, :]

    out_blk = pl.BlockSpec((None, ts, tn), lambda n, i: (n, i, 0))
    act = jax.ShapeDtypeStruct((2, s, tn), F32)
    return _call(
        body, name="ffn_up",
        out_shape=[act, act, jax.ShapeDtypeStruct((2, s, tn), BF16)],
        grid=(2, s // ts),
        in_specs=[pl.BlockSpec((ts, D_MODEL), lambda n, i: (i, 0)),
                  pl.BlockSpec((None, D_MODEL, tn), lambda n, i: (n, 0, 0)),
                  pl.BlockSpec((None, D_MODEL, tn), lambda n, i: (n + 2, 0, 0)),
                  pl.BlockSpec((None, 3, tn), lambda n, i: (n, 0, 0)),
                  pl.BlockSpec((None, 3, tn), lambda n, i: (n + 2, 0, 0)),
                  pl.BlockSpec((None, 1, tn), lambda n, i: (n, 0, 0)),
                  pl.BlockSpec((None, 1, tn), lambda n, i: (n + 2, 0, 0))],
        out_specs=[out_blk, out_blk, out_blk],
        scratch_shapes=[pltpu.VMEM((CONV_HALO, tn), F32), pltpu.VMEM((CONV_HALO, tn), F32)],
        compiler_params=_params(("arbitrary", "arbitrary")))(
            h2, w_up, w_up, conv_w, conv_w, conv_b, conv_b)


def _ffn_down(f_in, w_down, x1, target, g4):
    s = x1.shape[0]
    ts = 512

    def body(f_ref, w_ref, x1_ref, t_ref, g_ref, df_ref, dy_ref, loss_ref, dg_ref):
        @pl.when(pl.program_id(0) == 0)
        def _():
            loss_ref[...] = jnp.zeros_like(loss_ref)
            dg_ref[...] = jnp.zeros_like(dg_ref)

        f = _dot(f_ref[0], w_ref[0:FF_TILE, :]) + _dot(f_ref[1], w_ref[FF_TILE:, :])
        rf = _rstd(f)
        fn = f * rf
        g = g_ref[...]
        err = (x1_ref[...] + fn * g) - t_ref[...]
        loss_ref[...] += 0.5 * jnp.sum(jnp.mean(err * err, axis=-1))
        dy = err * (1.0 / D_MODEL)
        dy_ref[...] = dy
        dg_ref[...] += jnp.sum(dy * fn, axis=0, keepdims=True)
        dfn = dy * g
        df_ref[...] = (rf * (dfn - fn * jnp.mean(dfn * fn, axis=-1, keepdims=True))).astype(BF16)

    row = pl.BlockSpec((ts, D_MODEL), lambda i: (i, 0))
    return _call(
        body, name="ffn_down",
        out_shape=[jax.ShapeDtypeStruct((s, D_MODEL), BF16), jax.ShapeDtypeStruct((s, D_MODEL), F32),
                   jax.ShapeDtypeStruct((8, 128), F32), jax.ShapeDtypeStruct((1, D_MODEL), F32)],
        grid=(s // ts,),
        in_specs=[pl.BlockSpec((2, ts, FF_TILE), lambda i: (0, i, 0)), _full((D_FF, D_MODEL)),
                  row, row, _full((1, D_MODEL))],
        out_specs=[row, row, _full((8, 128)), _full((1, D_MODEL))],
        compiler_params=_params(("arbitrary",)))(f_in, w_down, x1, target, g4)


def _tn_matmul(a, b, name, ts=512):
    na, s, ka = a.shape
    nb, _, nbc = b.shape
    steps = s // ts

    def body(a_ref, b_ref, o_ref, acc_ref):
        @pl.when(pl.program_id(2) == 0)
        def _():
            acc_ref[...] = jnp.zeros_like(acc_ref)

        acc_ref[...] += _dot_tn(a_ref[...].astype(BF16), b_ref[...].astype(BF16))

        @pl.when(pl.program_id(2) == steps - 1)
        def _():
            o_ref[...] = acc_ref[...].astype(BF16)

    return _call(
        body, name=name, out_shape=jax.ShapeDtypeStruct((na, nb, ka, nbc), BF16),
        grid=(na, nb, steps),
        in_specs=[pl.BlockSpec((None, ts, ka), lambda i, j, r: (i, r, 0)),
                  pl.BlockSpec((None, ts, nbc), lambda i, j, r: (j, r, 0))],
        out_specs=pl.BlockSpec((None, None, ka, nbc), lambda i, j, r: (i, j, 0, 0)),
        scratch_shapes=[pltpu.VMEM((ka, nbc), F32)],
        compiler_params=_params(("parallel", "parallel", "arbitrary")))(a, b)


def _ffn_bwd_act(df, w_down, upre_g, upre_v, conv_w, conv_b):
    s = df.shape[0]
    ts = 256
    tn = FF_TILE
    nr = s // ts
    per = ts // CONV_HALO

    def body(df_ref, wd_ref, ug_ref, uv_ref, hg_ref, hv_ref, cwg_ref, cwv_ref, cbg_ref, cbv_ref,
             dug_ref, duv_ref, dcwg_ref, dcwv_ref, dcbg_ref, dcbv_ref, headg, headv):
        i = pl.program_id(1)
        first_rows = i == nr - 1

        @pl.when(i == 0)
        def _():
            for r in (headg, headv, dcwg_ref, dcwv_ref, dcbg_ref, dcbv_ref):
                r[...] = jnp.zeros_like(r)

        ug, uv = ug_ref[...], uv_ref[...]
        cwg, cwv = cwg_ref[...], cwv_ref[...]
        gate = _conv_rows(ug, *_earlier_rows(jnp.where(first_rows, 0.0, hg_ref[...]), ug), cwg, cbg_ref[...])
        val = _conv_rows(uv, *_earlier_rows(jnp.where(first_rows, 0.0, hv_ref[...]), uv), cwv, cbv_ref[...])
        sg = _sigmoid(gate)
        dfin = _dot_nt(df_ref[...], wd_ref[...])
        dval = dfin * (gate * sg)
        dgate = dfin * val * (sg * (1.0 + gate * (1.0 - sg)))

        def conv_bwd(dact, x, head, cw, dcw_ref, dcb_ref, du_ref):
            d1, d2 = _later_rows(dact, head[...])
            dcb_ref[...] += jnp.sum(dact, axis=0, keepdims=True)
            for kk, shifted in enumerate((d2, d1, dact)):
                dcw_ref[kk:kk + 1, :] += jnp.sum(x * shifted, axis=0, keepdims=True)
            du_ref[...] = (cw[2:3, :] * dact + cw[1:2, :] * d1 + cw[0:1, :] * d2).astype(BF16)
            head[...] = dact[0:CONV_HALO, :]

        conv_bwd(dgate, ug, headg, cwg, dcwg_ref, dcbg_ref, dug_ref)
        conv_bwd(dval, uv, headv, cwv, dcwv_ref, dcbv_ref, duv_ref)

    rows = lambda n, i: (n, nr - 1 - i, 0)
    halo = lambda n, i: (n, jnp.maximum((nr - 1 - i) * per - 1, 0), 0)
    act_blk = pl.BlockSpec((None, ts, tn), rows)
    halo_blk = pl.BlockSpec((None, CONV_HALO, tn), halo)
    cw_blk = lambda off: pl.BlockSpec((None, 3, tn), lambda n, i: (n + off, 0, 0))
    cb_blk = lambda off: pl.BlockSpec((None, 1, tn), lambda n, i: (n + off, 0, 0))
    acc_w = pl.BlockSpec((None, 3, tn), lambda n, i: (n, 0, 0))
    acc_b = pl.BlockSpec((None, 1, tn), lambda n, i: (n, 0, 0))
    dact = jax.ShapeDtypeStruct((2, s, tn), BF16)
    return _call(
        body, name="ffn_bwd_act",
        out_shape=[dact, dact, jax.ShapeDtypeStruct((2, 3, tn), F32), jax.ShapeDtypeStruct((2, 3, tn), F32),
                   jax.ShapeDtypeStruct((2, 1, tn), F32), jax.ShapeDtypeStruct((2, 1, tn), F32)],
        grid=(2, nr),
        in_specs=[pl.BlockSpec((ts, D_MODEL), lambda n, i: (nr - 1 - i, 0)),
                  pl.BlockSpec((tn, D_MODEL), lambda n, i: (n, 0)),
                  act_blk, act_blk, halo_blk, halo_blk,
                  cw_blk(0), cw_blk(2), cb_blk(0), cb_blk(2)],
        out_specs=[act_blk, act_blk, acc_w, acc_w, acc_b, acc_b],
        scratch_shapes=[pltpu.VMEM((CONV_HALO, tn), F32)] * 2,
        compiler_params=_params(("arbitrary", "arbitrary")))(
            df, w_down, upre_g, upre_v, upre_g, upre_v, conv_w, conv_w, conv_b, conv_b)


def _ffn_bwd_in(dug, duv, w_up, x1, dy, mix, g3, g2):
    s = x1.shape[0]
    ts = 256

    def body(dg_ref, dv_ref, w_ref, x1_ref, dy_ref, mix_ref, g3_ref, g2_ref,
             dx1_ref, dmix_ref, dg3_ref, dg2_ref):
        @pl.when(pl.program_id(0) == 0)
        def _():
            dg3_ref[...] = jnp.zeros_like(dg3_ref)
            dg2_ref[...] = jnp.zeros_like(dg2_ref)

        dh = _dot_nt(dg_ref[0], w_ref[0]) + _dot_nt(dg_ref[1], w_ref[1])
        dh = dh + _dot_nt(dv_ref[0], w_ref[2]) + _dot_nt(dv_ref[1], w_ref[3])
        x1 = x1_ref[...]
        r3 = _rstd(x1)
        xn = x1 * r3
        dg3_ref[...] += jnp.sum(dh * xn, axis=0, keepdims=True)
        dxn = dh * g3_ref[...]
        dx1 = dy_ref[...] + r3 * (dxn - xn * jnp.mean(dxn * xn, axis=-1, keepdims=True))
        dx1_ref[...] = dx1
        mix = mix_ref[...]
        rm = _rstd(mix)
        mn = mix * rm
        dg2_ref[...] += jnp.sum(dx1 * mn, axis=0, keepdims=True)
        dmn = dx1 * g2_ref[...]
        dmix_ref[...] = (rm * (dmn - mn * jnp.mean(dmn * mn, axis=-1, keepdims=True))).astype(BF16)

    row = pl.BlockSpec((ts, D_MODEL), lambda i: (i, 0))
    act = pl.BlockSpec((2, ts, FF_TILE), lambda i: (0, i, 0))
    vec = _full((1, D_MODEL))
    return _call(
        body, name="ffn_bwd_in",
        out_shape=[jax.ShapeDtypeStruct((s, D_MODEL), F32), jax.ShapeDtypeStruct((s, D_MODEL), BF16),
                   jax.ShapeDtypeStruct((1, D_MODEL), F32), jax.ShapeDtypeStruct((1, D_MODEL), F32)],
        grid=(s // ts,),
        in_specs=[act, act, _full(w_up.shape), row, row, row, vec, vec],
        out_specs=[row, row, vec, vec],
        compiler_params=_params(("arbitrary",), vmem_mb=56))(dug, duv, w_up, x1, dy, mix, g3, g2)


def _mix_bwd(dmix, w_out, attn, attn_scale):
    s = dmix.shape[0]
    ts = 512

    def body(dm_ref, w_ref, a_ref, as_ref, dp_ref, do_ref, das_ref):
        @pl.when(pl.program_id(0) == 0)
        def _():
            das_ref[...] = jnp.zeros_like(das_ref)

        dm = dm_ref[...]
        dp_ref[...] = _dot_nt(dm, w_ref[0:D_POOL, :])
        da = _dot_nt(dm, w_ref[D_POOL:, :])
        ao = a_ref[...]
        ra = _rstd(ao)
        an = ao * ra
        das_ref[...] += jnp.sum(da * an, axis=0, keepdims=True)
        dan = da * as_ref[...]
        do_ref[...] = (ra * (dan - an * jnp.mean(dan * an, axis=-1, keepdims=True))).astype(BF16)

    row = lambda w: pl.BlockSpec((ts, w), lambda i: (i, 0))
    return _call(
        body, name="mix_bwd",
        out_shape=[jax.ShapeDtypeStruct((s, D_POOL), F32), jax.ShapeDtypeStruct((s, D_ATTN), BF16),
                   jax.ShapeDtypeStruct((1, D_ATTN), F32)],
        grid=(s // ts,),
        in_specs=[row(D_MODEL), _full((D_MODEL, D_MODEL)), row(D_ATTN), _full((1, D_ATTN))],
        out_specs=[row(D_POOL), row(D_ATTN), _full((1, D_ATTN))],
        compiler_params=_params(("arbitrary",)))(dmix, w_out, attn, attn_scale)


def _attn_bwd(q, k, kt, vt, do, totals):
    s = q.shape[0]
    tb = ATT_BLOCK
    nq = s // tb

    def body(q_ref, do_ref, t_ref, k_hbm, kt_hbm, vt_hbm, dq_ref, dk_hbm, dv_hbm,
             k_scr, kt_scr, vt_scr, dkt_acc, dvt_acc, stage, *bufs):
        hp = pl.program_id(0)
        i = pl.program_id(1)
        lanes = pl.ds(pl.multiple_of(hp * HEAD_PAIR, HEAD_PAIR), HEAD_PAIR)

        @pl.when(i == 0)
        def _():
            pltpu.sync_copy(k_hbm.at[:, lanes], k_scr)
            pltpu.sync_copy(kt_hbm.at[:, lanes, :], kt_scr)
            pltpu.sync_copy(vt_hbm.at[:, lanes, :], vt_scr)
            dkt_acc[...] = jnp.zeros_like(dkt_acc)
            dvt_acc[...] = jnp.zeros_like(dvt_acc)

        upper = _tri("suffix")
        lower = _tri("prefix")
        causal = _causal_mask()
        lane = lax.broadcasted_iota(jnp.int32, (1, HEAD_PAIR), 1)
        first = lane < 64
        q2 = q_ref[...]
        do2 = do_ref[...]
        zero = jnp.zeros_like(q2)
        qs = (jnp.where(first, q2, zero), jnp.where(first, zero, q2))
        dos = (jnp.where(first, do2, zero), jnp.where(first, zero, do2))
        qcat_t = jnp.concatenate(qs, axis=0).astype(F32).T.astype(BF16)
        docat_t = jnp.concatenate(dos, axis=0).astype(F32).T.astype(BF16)
        tots = (t_ref[:, 0:1], t_ref[:, 1:2])

        z_ring, sg_ring = [[[bufs[8 * kind + 2 * slot + e] for e in range(2)] for slot in range(4)]
                           for kind in range(2)]
        in_buf, da_buf, dw_buf, pre_buf = [
            [[bufs[16 + 4 * kind + 2 * slot + e] for e in range(2)] for slot in range(2)]
            for kind in range(4)]
        pr_buf, dzr_buf, dzc_buf = bufs[32:34], bufs[34:36], bufs[36:38]

        def rows(p):
            return pl.ds(pl.multiple_of(jnp.clip(p, 0, nq - 1) * tb, tb), tb)

        def split_heads(block):
            return jnp.concatenate([jnp.where(first, block, zero), jnp.where(first, zero, block)], axis=0)

        def trip(t, u, carry):
            first_trips = isinstance(t, int)
            w, r = u % 2, 1 - u % 2
            cs, cps, dq = list(carry[0:2]), list(carry[2:4]), carry[4]
            if not first_trips:
                live4 = t - 4 < i
                dq = dq + jnp.where(live4, _dot(dzc_buf[r][...], split_heads(k_scr[rows(t - 4), :])), 0.0)
                dkt_acc[jnp.clip(t - 4, 0, nq - 1)] += jnp.where(live4, _dot(qcat_t, dzr_buf[r][...]), 0.0)
            if not first_trips or t >= 3:
                dvt_acc[jnp.clip(t - 3, 0, nq - 1)] += _dot(docat_t, pr_buf[r][...])
            ktj = kt_scr[jnp.clip(t, 0, nq - 1)]
            for e in range(2):
                z_ring[u][e][...] = _dot(qs[e], ktj)
            if not first_trips or t >= 1:
                vtj = vt_scr[jnp.clip(t - 1, 0, nq - 1)]
                for e in range(2):
                    sp, sig = _softplus(z_ring[(u - 1) % 4][e][...], True)
                    sg_ring[(u - 1) % 4][e][...] = sig
                    in_buf[w][e][...] = _dot(sp.astype(BF16), upper)
                    da_buf[w][e][...] = _dot(dos[e], vtj)
            if not first_trips or t >= 2:
                for e in range(2):
                    incl = in_buf[r][e][...]
                    cs[e] = cs[e] + incl[:, 0:1]
                    off = jnp.where(t - 2 < i, tots[e] - cs[e], -NEG_BIG)
                    a = jnp.exp(z_ring[(u - 2) % 4][e][...] - incl - off)
                    dw = a * da_buf[r][e][...]
                    dw_buf[w][e][...] = dw
                    pr_buf[w][e * tb:(e + 1) * tb, :] = a.astype(BF16)
                    pre_buf[w][e][...] = _dot(dw.astype(BF16), lower)
            if not first_trips or t >= 3:
                for e in range(2):
                    pre = pre_buf[r][e][...] + cps[e]
                    dzb = (dw_buf[r][e][...] - sg_ring[(u - 3) % 4][e][...] * pre).astype(BF16)
                    cps[e] = pre[:, tb - 1:tb]
                    dzr_buf[w][e * tb:(e + 1) * tb, :] = dzb
                    dzc_buf[w][:, e * tb:(e + 1) * tb] = dzb
            return cs[0], cs[1], cps[0], cps[1], dq

        def four_trips(n, cr):
            for u in range(4):
                cr = trip(4 * n + u, u, cr)
            return cr

        col = jnp.zeros((tb, 1), F32)
        carry = four_trips(0, (col, col, col, col, jnp.zeros((tb, HEAD_PAIR), F32)))
        carry = lax.fori_loop(1, jnp.where(i > 0, (i + 7) // 4, 1), four_trips, carry)

        cps, dq = carry[2:4], carry[4]
        kj = k_scr[rows(i), :]
        dzs, probs = [], []
        for e in range(2):
            z = _dot(qs[e], kt_scr[i])
            sp, sig = _softplus(z, True)
            incl = _dot(jnp.where(causal, sp, 0.0).astype(BF16), upper)
            a = jnp.where(causal, jnp.exp(z - incl), 0.0)
            dw = a * _dot(dos[e], vt_scr[i])
            pre = _dot(dw.astype(BF16), lower) + cps[e]
            dzs.append(jnp.where(causal, dw - sig * pre, 0.0).astype(BF16))
            probs.append(a.astype(BF16))
        dq = dq + _dot(jnp.concatenate(dzs, axis=1), split_heads(kj))
        dkt_acc[i] += _dot(qcat_t, jnp.concatenate(dzs, axis=0))
        dvt_acc[i] += _dot(docat_t, jnp.concatenate(probs, axis=0))
        dq_ref[...] = (dq * Q_SCALE).astype(BF16)

        @pl.when(i == nq - 1)
        def _():
            for acc, dst in ((dkt_acc, dk_hbm), (dvt_acc, dv_hbm)):
                def flip(n, _, acc=acc):
                    at = pl.ds(pl.multiple_of(n * tb, tb), tb)
                    stage[at, :] = acc[n].T
                    return 0
                lax.fori_loop(0, nq, flip, 0)
                pltpu.sync_copy(stage, dst.at[:, lanes])

    blk = pl.BlockSpec((tb, HEAD_PAIR), lambda h, i: (i, h))
    grad = jax.ShapeDtypeStruct((s, D_ATTN), F32)
    return _call(
        body, name="attn_bwd",
        out_shape=[jax.ShapeDtypeStruct((s, D_ATTN), BF16), grad, grad],
        grid=(4, nq),
        in_specs=[blk, blk, pl.BlockSpec((None, tb, 2), lambda h, i: (h, i, 0)), ANY, ANY, ANY],
        out_specs=[blk, ANY, ANY],
        scratch_shapes=[pltpu.VMEM((s, HEAD_PAIR), BF16), pltpu.VMEM((nq, HEAD_PAIR, tb), BF16),
                        pltpu.VMEM((nq, HEAD_PAIR, tb), BF16),
                        pltpu.VMEM((nq, HEAD_PAIR, tb), F32), pltpu.VMEM((nq, HEAD_PAIR, tb), F32),
                        pltpu.VMEM((s, HEAD_PAIR), F32)]
        + [pltpu.VMEM((tb, tb), F32)] * 32
        + [pltpu.VMEM((2 * tb, tb), BF16)] * 4 + [pltpu.VMEM((tb, 2 * tb), BF16)] * 2,
        compiler_params=_params(("arbitrary", "arbitrary"), vmem_mb=60))(q, do, totals, k, kt, vt)


def _pool_bwd(u, dmp, w_pool, pool_scale):
    s = u.shape[0]
    ts = 512
    nr = s // ts
    per = ts // POOL_HALO

    def body(u_ref, halo_ref, dm_ref, wp_ref, ps_ref, du_ref, dwp_ref, dps_ref, ext_ref, y_ref, dext_ref):
        i = pl.program_id(0)
        rb = nr - 1 - i

        @pl.when(i == 0)
        def _():
            dext_ref[ts:, :] = jnp.zeros((POOL_HALO, D_POOL), F32)
            dwp_ref[...] = jnp.zeros_like(dwp_ref)
            dps_ref[...] = jnp.zeros_like(dps_ref)

        ext_ref[0:POOL_HALO, :] = jnp.where(rb > 0, halo_ref[...], 0.0)
        ext_ref[POOL_HALO:, :] = u_ref[...]
        ps, cnts = [], []
        for g, window in enumerate(POOL_WINDOWS):
            p, cnt = _pool_means(ext_ref, g, window, ts, rb * ts)
            ps.append(p.astype(BF16))
            cnts.append(cnt)
            y_ref[:, g * POOL_GROUP:(g + 1) * POOL_GROUP] = _dot(ps[g], wp_ref[g].astype(BF16))
        y = y_ref[...]
        r = _rstd(y)
        yn = y * r
        dm = dm_ref[...]
        dps_ref[...] += jnp.sum(dm * yn, axis=0, keepdims=True)
        dn = dm * ps_ref[...]
        dy = r * (dn - yn * jnp.mean(dn * yn, axis=-1, keepdims=True))
        for g, window in enumerate(POOL_WINDOWS):
            cols = slice(g * POOL_GROUP, (g + 1) * POOL_GROUP)
            dyg = dy[:, cols].astype(BF16)
            dwp_ref[g] += _dot_tn(ps[g], dyg)
            dp = _dot_nt(dyg, wp_ref[g].astype(BF16))
            dext_ref[0:ts, cols] = dp / cnts[g]
            acc = dext_ref[0:ts, cols]
            for d in range(1, window):
                acc = acc + dext_ref[d:d + ts, cols]
            du_ref[:, cols] = (acc - dp).astype(BF16)
        dext_ref[ts:, :] = dext_ref[0:POOL_HALO, :]

    rows = pl.BlockSpec((ts, D_POOL), lambda i: (nr - 1 - i, 0))
    return _call(
        body, name="pool_bwd",
        out_shape=[jax.ShapeDtypeStruct((s, D_POOL), BF16), jax.ShapeDtypeStruct(w_pool.shape, F32),
                   jax.ShapeDtypeStruct((1, D_POOL), F32)],
        grid=(nr,),
        in_specs=[rows,
                  pl.BlockSpec((POOL_HALO, D_POOL), lambda i: (jnp.maximum((nr - 1 - i) * per - 1, 0), 0)),
                  rows, _full(w_pool.shape), _full((1, D_POOL))],
        out_specs=[rows, _full(w_pool.shape), _full((1, D_POOL))],
        scratch_shapes=[pltpu.VMEM((ts + POOL_HALO, D_POOL), F32), pltpu.VMEM((ts, D_POOL), F32),
                        pltpu.VMEM((ts + POOL_HALO, D_POOL), F32)],
        compiler_params=_params(("arbitrary",)))(u, u, dmp, w_pool, pool_scale)


def _in_proj_bwd(du, dq, dk, dv, w_in, x, dx1, g1):
    s = x.shape[0]
    ts = 512

    def body(du_ref, dq_ref, dk_ref, dv_ref, w_ref, x_ref, dx1_ref, g_ref, gx_ref, dg_ref):
        @pl.when(pl.program_id(0) == 0)
        def _():
            dg_ref[...] = jnp.zeros_like(dg_ref)

        dh = _dot_nt(du_ref[...], w_ref[0]) + _dot_nt(dq_ref[...], w_ref[1])
        dh = dh + _dot_nt(dk_ref[...].astype(BF16), w_ref[2]) + _dot_nt(dv_ref[...].astype(BF16), w_ref[3])
        xv = x_ref[...]
        r = _rstd(xv)
        xn = xv * r
        dg_ref[...] += jnp.sum(dh * xn, axis=0, keepdims=True)
        dxn = dh * g_ref[...]
        gx_ref[...] = dx1_ref[...] + r * (dxn - xn * jnp.mean(dxn * xn, axis=-1, keepdims=True))

    row = lambda w: pl.BlockSpec((ts, w), lambda i: (i, 0))
    return _call(
        body, name="in_proj_bwd",
        out_shape=[jax.ShapeDtypeStruct((s, D_MODEL), F32), jax.ShapeDtypeStruct((1, D_MODEL), F32)],
        grid=(s // ts,),
        in_specs=[row(D_POOL)] * 4 + [_full(w_in.shape), row(D_MODEL), row(D_MODEL), _full((1, D_MODEL))],
        out_specs=[row(D_MODEL), _full((1, D_MODEL))],
        compiler_params=_params(("arbitrary",)))(du, dq, dk, dv, w_in, x, dx1, g1)


_SMALL = ("norm_mix_pre", "w_pool", "pool_scale", "attn_scale", "norm_mix_post",
          "norm_ffn_pre", "conv_b", "norm_ffn_post")
_SMALL_SIZE = {"norm_mix_pre": 1024, "w_pool": 65536, "pool_scale": 512, "attn_scale": 512,
               "norm_mix_post": 1024, "norm_ffn_pre": 1024, "conv_b": 5632, "norm_ffn_post": 1024}
_SMALL_ROWS = 600
_CONVW_ROWS = 132
_PACK_ROWS = _SMALL_ROWS + _CONVW_ROWS + 4


def _pack_small(parts):
    flat = jnp.concatenate([parts[n].reshape(-1) for n in _SMALL])
    flat = jnp.pad(flat, (0, _SMALL_ROWS * 128 - flat.shape[0]))
    return flat.reshape(_SMALL_ROWS, 128)


def _unpack_small(packed, like):
    flat = packed.reshape(-1)
    out, off = {}, 0
    for n in _SMALL:
        out[n] = flat[off:off + _SMALL_SIZE[n]].reshape(like[n].shape)
        off += _SMALL_SIZE[n]
    return out


def kernel(x, norm_mix_pre, w_in, w_pool, pool_scale, attn_scale, w_out, norm_mix_post, norm_ffn_pre, w_up, conv_w, conv_b, w_down, norm_ffn_post, loss_target, m_norm_mix_pre, m_w_in, m_w_pool, m_pool_scale, m_attn_scale, m_w_out, m_norm_mix_post, m_norm_ffn_pre, m_w_up, m_conv_w, m_conv_b, m_w_down, m_norm_ffn_post, v_norm_mix_pre, v_w_in, v_w_pool, v_pool_scale, v_attn_scale, v_w_out, v_norm_mix_post, v_norm_ffn_pre, v_w_up, v_conv_w, v_conv_b, v_w_down, v_norm_ffn_post):
    weights = dict(norm_mix_pre=norm_mix_pre, w_in=w_in, w_pool=w_pool, pool_scale=pool_scale,
                   attn_scale=attn_scale, w_out=w_out, norm_mix_post=norm_mix_post,
                   norm_ffn_pre=norm_ffn_pre, w_up=w_up, conv_w=conv_w, conv_b=conv_b,
                   w_down=w_down, norm_ffn_post=norm_ffn_post)
    mom1 = dict(norm_mix_pre=m_norm_mix_pre, w_in=m_w_in, w_pool=m_w_pool, pool_scale=m_pool_scale,
                attn_scale=m_attn_scale, w_out=m_w_out, norm_mix_post=m_norm_mix_post,
                norm_ffn_pre=m_norm_ffn_pre, w_up=m_w_up, conv_w=m_conv_w, conv_b=m_conv_b,
                w_down=m_w_down, norm_ffn_post=m_norm_ffn_post)
    mom2 = dict(norm_mix_pre=v_norm_mix_pre, w_in=v_w_in, w_pool=v_w_pool, pool_scale=v_pool_scale,
                attn_scale=v_attn_scale, w_out=v_w_out, norm_mix_post=v_norm_mix_post,
                norm_ffn_pre=v_norm_ffn_pre, w_up=v_w_up, conv_w=v_conv_w, conv_b=v_conv_b,
                w_down=v_w_down, norm_ffn_post=v_norm_ffn_post)
    order = list(weights)

    xs = x[0]
    target = loss_target[0]
    wp = w_pool[0]
    shard = lax.axis_index("x") * 2 + lax.axis_index("y")

    slot = shard.astype(jnp.int32).reshape(1)
    win_g, = _gather_shards([_cast_bf16(w_in[0], "cast_w_in")])
    lands = [_into_slot(w_out[0], slot, BF16, "cast_w_out"), _into_slot(w_up[0], slot, BF16, "cast_w_up"),
             _into_slot(w_down[0], slot, BF16, "cast_w_down"), _into_slot(conv_w[0], slot, F32, "place_conv_w")]
    g_send, g_recv, g_lands, g_token = _exchange_start(None, lands, win_g, "gather_start")
    convb_g = conv_b[0].reshape(N_SHARD, 1, FF_TILE)

    u, q, k, kt, vt, h1 = _in_proj(xs, norm_mix_pre + g_token[0:1, 0:1], win_g)
    mpool = _pool_fwd(u, wp, pool_scale)
    attn, totals = _attn_fwd(q, k, vt)
    _, (wout_g, wup_g, wdown_g, convw_g) = _exchange_wait(g_send, g_recv, g_lands, False, attn, "gather_wait")
    wout_f = wout_g.reshape(D_MODEL, D_MODEL)
    wdown_f = wdown_g.reshape(D_FF, D_MODEL)
    mattn, mix, x1, h2 = _mix_out(attn, mpool, xs, attn_scale, wout_f, norm_mix_post, norm_ffn_pre)
    upre_g, upre_v, f_in = _ffn_up(h2, wup_g, convw_g, convb_g)
    df, dy, loss_tile, d_post = _ffn_down(f_in, wdown_f, x1, target, norm_ffn_post)

    d_wdown = _tn_matmul(f_in, df[None], "dw_down")
    dug, duv, dcw_g, dcw_v, dcb_g, dcb_v = _ffn_bwd_act(df, wdown_f, upre_g, upre_v, convw_g, convb_g)
    d_wup = jnp.concatenate([_tn_matmul(h2[None], dug, "dw_up_gate")[0],
                             _tn_matmul(h2[None], duv, "dw_up_value")[0]], axis=0)
    early = [d_wup, d_wdown.reshape(N_SHARD, D_FF // N_SHARD, D_MODEL)]
    s_send, s_recv, s_thru, s_token = _exchange_start(
        early, [lax.empty((3,) + g.shape[1:], g.dtype) for g in early], d_wup, "scatter_start")
    dx1, dmix, d_ffn_pre, d_mix_post = _ffn_bwd_in(
        dug, duv, wup_g, x1, dy, mix, norm_ffn_pre + s_token[0:1, 0:1], norm_mix_post)
    d_wout = jnp.concatenate([_tn_matmul(mpool[None], dmix[None], "dw_out_pool")[0, 0],
                              _tn_matmul(mattn[None], dmix[None], "dw_out_attn")[0, 0]], axis=0)
    dmp, do, d_attn_scale = _mix_bwd(dmix, wout_f, attn, attn_scale)
    dq, dk, dv = _attn_bwd(q, k, kt, vt, do, totals)
    du, d_wpool, d_pool_scale = _pool_bwd(u, dmp, wp, pool_scale)
    d_win = jnp.stack([_tn_matmul(h1[None], t[None], "dw_in_%d" % n)[0, 0]
                       for n, t in enumerate((du, dq, dk, dv))])
    grad_x, d_mix_pre = _in_proj_bwd(du, dq, dk, dv, win_g, xs, dx1, norm_mix_pre)

    d_convw = jnp.concatenate([dcw_g, dcw_v], axis=0)
    d_convb = jnp.concatenate([dcb_g, dcb_v], axis=0).reshape(1, 2 * D_FF)
    small_parts = dict(norm_mix_pre=d_mix_pre, w_pool=d_wpool, pool_scale=d_pool_scale,
                       attn_scale=d_attn_scale, norm_mix_post=d_mix_post, norm_ffn_pre=d_ffn_pre,
                       conv_b=d_convb, norm_ffn_post=d_post)
    packed = jnp.concatenate([_pack_small(small_parts), d_convw.reshape(_CONVW_ROWS, 128),
                              loss_tile[0:4]], axis=0)
    late, gathered = _scatter_grads([d_win, d_wout.reshape(N_SHARD, D_MODEL // N_SHARD, D_MODEL)], packed)
    early_srcs, early_lands = _exchange_wait(s_send, s_recv, s_thru, True, grad_x, "scatter_wait")
    quarter = [_sum_slots(r, (3, 0, 1, 2), "sum_chips_%d" % n) for n, r in enumerate(late)]
    quarter += [_sum_own_and_received(early_srcs[n], slot, early_lands[n], "sum_chips_%d" % (n + 2))
                for n in range(2)]
    sibling = _swap_with_sibling(quarter)
    small_sum = _sum_slots(gathered, tuple(range(8)), "sum_small")

    results = {}
    for n, name in enumerate(("w_in", "w_out", "w_up", "w_down")):
        res = _adamw([quarter[n], sibling[n]], weights[name][0], mom1[name][0], mom2[name][0],
                     "adamw_" + name)
        results[name] = [t[None] for t in res]
    g_convw = lax.dynamic_slice_in_dim(
        small_sum[_SMALL_ROWS:_SMALL_ROWS + _CONVW_ROWS].reshape(N_SHARD, 3, FF_TILE), shard, 1, axis=0)[0]
    convw_pad = lambda t: jnp.pad(t, ((0, 5), (0, 0)))
    res = _adamw([convw_pad(g_convw)], convw_pad(conv_w[0]), convw_pad(m_conv_w[0]),
                 convw_pad(v_conv_w[0]), "adamw_conv_w")
    results["conv_w"] = [t[:3][None] for t in res]
    pack_w = _pack_small(weights)
    pack_m = _pack_small(mom1)
    pack_v = _pack_small(mom2)
    res = _adamw([small_sum[:_SMALL_ROWS]], pack_w, pack_m, pack_v, "adamw_small")
    unpacked = [_unpack_small(t, weights) for t in res]
    for name in _SMALL:
        results[name] = [t[name] for t in unpacked]

    loss = small_sum[_SMALL_ROWS + _CONVW_ROWS, 0]
    outs = [loss, grad_x[None]]
    for slot in range(4):
        outs.extend(results[name][slot] for name in order)
    return tuple(outs)
```
